```python
import jax, jax.numpy as jnp
from jax import lax
import numpy as np

D_MODEL = 1024
BATCH = 32
SEQ = 2048
DEPTH = 1

CHUNK = 64
Q_BLOCK = 2 * CHUNK
HEAD_DIM = 64
MIX_WIDTH = D_MODEL
RWKV_WIDTH = MIX_WIDTH // 2
SB_WIDTH = MIX_WIDTH - RWKV_WIDTH
RWKV_HEADS = RWKV_WIDTH // HEAD_DIM
SB_HEADS = SB_WIDTH // HEAD_DIM
DECAY_LORA = 64
AAA_LORA = 64
GATE_LORA = 128
D_FF = 4 * D_MODEL
RMS_EPS = 1e-5
GN_EPS = 64e-5
RWKV_COLS = 3 * RWKV_WIDTH + DECAY_LORA + AAA_LORA + GATE_LORA
SB_COLS = 3 * SB_WIDTH
IN_COLS = RWKV_COLS + SB_COLS

kernel_name = "hybrid_rwkv7_stickbreak_block"


def _rmsnorm(x, g):
    xf = x.astype(jnp.float32)
    y = xf * lax.rsqrt(jnp.mean(xf * xf, axis=-1, keepdims=True) + RMS_EPS)
    return (y * g.astype(jnp.float32)).astype(x.dtype)


def _token_shift(p):
    return jnp.pad(p[:, :-1], ((0, 0), (1, 0), (0, 0)))


def _rwkv7_time_mix(p, mu, w0, w_decay_up, a0, w_aaa_up, w_gate_up, k_k, k_a, r_k, gn_w, gn_b):
    B, T, _ = p.shape
    H, dh = RWKV_HEADS, HEAD_DIM
    p = p + mu * (_token_shift(p) - p)
    W = RWKV_WIDTH
    splits = [W, 2 * W, 3 * W, 3 * W + DECAY_LORA, 3 * W + DECAY_LORA + AAA_LORA]
    r, k, v, xw, xa, xg = jnp.split(p, splits, axis=-1)
    w = -jax.nn.softplus(-(w0 + jnp.tanh(xw) @ w_decay_up)) - 0.5
    a = jax.nn.sigmoid(a0 + xa @ w_aaa_up)
    g = jax.nn.sigmoid(xg) @ w_gate_up

    def heads(t):
        return t.reshape(B, T, H, dh).astype(jnp.float32)

    r, w, k, v, a = heads(r), heads(w), heads(k), heads(v), heads(a)
    kk = k * k_k.reshape(H, dh).astype(jnp.float32)
    kk = kk / jnp.maximum(jnp.sqrt(jnp.sum(kk * kk, axis=-1, keepdims=True)), 1e-12)
    k = k * (1.0 + (a - 1.0) * k_a.reshape(H, dh).astype(jnp.float32))
    decay = jnp.exp(-jnp.exp(w))

    def step(S, inp):
        r_t, d_t, k_t, v_t, a_t, b_t = inp
        sa = jnp.einsum('bhvk,bhk->bhv', S, a_t)
        S = S * d_t[:, :, None, :] + sa[..., None] * b_t[:, :, None, :] + v_t[..., None] * k_t[:, :, None, :]
        return S, jnp.einsum('bhvk,bhk->bhv', S, r_t)

    def seq_first(t):
        return jnp.swapaxes(t, 0, 1)

    S0 = jnp.zeros((B, H, dh, dh), jnp.float32)
    xs = (seq_first(r), seq_first(decay), seq_first(k), seq_first(v), seq_first(-kk), seq_first(kk * a))
    _, y = lax.scan(step, S0, xs)
    y = jnp.swapaxes(y, 0, 1)
    mean = jnp.mean(y, axis=-1, keepdims=True)
    var = jnp.mean(jnp.square(y - mean), axis=-1, keepdims=True)
    y = (y - mean) * lax.rsqrt(var + GN_EPS) * gn_w.reshape(H, dh).astype(jnp.float32) + gn_b.reshape(H, dh).astype(jnp.float32)
    bonus = jnp.sum(r * k * r_k.astype(jnp.float32), axis=-1, keepdims=True) * v
    out = (y + bonus).reshape(B, T, RWKV_WIDTH) * g.astype(jnp.float32)
    return out.astype(p.dtype)


def _stick_breaking_attention(p, sb_gain):
    B, T, _ = p.shape
    H, dh = SB_HEADS, HEAD_DIM
    q, k, v = jnp.split(p, 3, axis=-1)

    def to_heads(t):
        return t.reshape(B, T, H, dh).transpose(0, 2, 1, 3)

    q, k, v = to_heads(q), to_heads(k), to_heads(v)
    scale = HEAD_DIM ** -0.5
    outs = []
    for blk in range(T // Q_BLOCK):
        start = blk * Q_BLOCK
        end = start + Q_BLOCK
        z = jnp.einsum('bhqd,bhkd->bhqk', q[:, :, start:end], k[:, :, :end]).astype(jnp.float32) * scale
        causal = jnp.arange(end)[None, :] < (start + jnp.arange(Q_BLOCK))[:, None]
        log_beta = jax.nn.log_sigmoid(z)
        log_keep = jnp.where(causal, log_beta - z, 0.0)
        log_a = log_beta + lax.cumsum(log_keep, axis=3, reverse=True) - log_keep
        att = jnp.where(causal, jnp.exp(log_a), 0.0)
        outs.append(jnp.einsum('bhqk,bhkd->bhqd', att.astype(v.dtype), v[:, :, :end]))
    o = jnp.concatenate(outs, axis=2).transpose(0, 2, 1, 3).astype(jnp.float32)
    o = o * lax.rsqrt(jnp.mean(o * o, axis=-1, keepdims=True) + RMS_EPS) * sb_gain.reshape(H, dh).astype(jnp.float32)
    return o.reshape(B, T, SB_WIDTH).astype(p.dtype)


def _fwd_setup_inputs(seed: int = 0) -> dict:
    key = jax.random.key(seed)
    ks = jax.random.split(key, 21)
    n = jax.random.normal
    f32 = jnp.float32
    return {
        "x": n(ks[0], (BATCH, SEQ, D_MODEL), f32),
        "ln1_g": 1.0 + 0.02 * n(ks[1], (DEPTH, D_MODEL), f32),
        "w_in": n(ks[2], (DEPTH, D_MODEL, IN_COLS), f32) * D_MODEL ** -0.5,
        "tok_mu": jax.random.uniform(ks[3], (DEPTH, RWKV_COLS), f32),
        "w0": jax.random.uniform(ks[4], (DEPTH, RWKV_WIDTH), f32, -6.0, -1.0),
        "w_decay_up": n(ks[5], (DEPTH, DECAY_LORA, RWKV_WIDTH), f32) * 0.5 * DECAY_LORA ** -0.5,
        "a0": 0.1 * n(ks[6], (DEPTH, RWKV_WIDTH), f32),
        "w_aaa_up": n(ks[7], (DEPTH, AAA_LORA, RWKV_WIDTH), f32) * 0.5 * AAA_LORA ** -0.5,
        "w_gate_up": n(ks[8], (DEPTH, GATE_LORA, RWKV_WIDTH), f32) * GATE_LORA ** -0.5,
        "k_k": 0.85 + 0.05 * n(ks[9], (DEPTH, RWKV_WIDTH), f32),
        "k_a": 1.0 + 0.05 * n(ks[10], (DEPTH, RWKV_WIDTH), f32),
        "r_k": 0.1 * n(ks[11], (DEPTH, RWKV_HEADS, HEAD_DIM), f32),
        "gn_w": 1.0 + 0.02 * n(ks[12], (DEPTH, RWKV_WIDTH), f32),
        "gn_b": 0.02 * n(ks[13], (DEPTH, RWKV_WIDTH), f32),
        "sb_gain": 1.0 + 0.02 * n(ks[14], (DEPTH, SB_WIDTH), f32),
        "w_out": n(ks[15], (DEPTH, MIX_WIDTH, D_MODEL), f32) * MIX_WIDTH ** -0.5,
        "ln2_g": 1.0 + 0.02 * n(ks[16], (DEPTH, D_MODEL), f32),
        "w_up": n(ks[17], (DEPTH, D_MODEL, D_FF), f32) * D_MODEL ** -0.5,
        "w_down": n(ks[18], (DEPTH, D_FF, D_MODEL), f32) * 0.5 * D_FF ** -0.5,
        "lnf_g": 1.0 + 0.02 * n(ks[19], (D_MODEL,), f32),
    }


def _fwd_reference(x, ln1_g, w_in, tok_mu, w0, w_decay_up, a0, w_aaa_up, w_gate_up, k_k, k_a, r_k,
              gn_w, gn_b, sb_gain, w_out, ln2_g, w_up, w_down, lnf_g):
    for l in range(DEPTH):
        h = _rmsnorm(x, ln1_g[l])
        p = h @ w_in[l]
        y_rwkv = _rwkv7_time_mix(p[..., :RWKV_COLS], tok_mu[l], w0[l], w_decay_up[l], a0[l],
                                 w_aaa_up[l], w_gate_up[l], k_k[l], k_a[l], r_k[l], gn_w[l], gn_b[l])
        y_sb = _stick_breaking_attention(p[..., RWKV_COLS:], sb_gain[l])
        x = x + jnp.concatenate([y_rwkv, y_sb], axis=-1) @ w_out[l]
        h = _rmsnorm(x, ln2_g[l])
        x = x + jnp.square(jax.nn.relu(h @ w_up[l])) @ w_down[l]
    return _rmsnorm(x, lnf_g)


import jax as _jax
import jax.numpy as _jnp

TWIN_FORMAT = 'train_step'
FWD_PARAMS = ['x', 'ln1_g', 'w_in', 'tok_mu', 'w0', 'w_decay_up', 'a0', 'w_aaa_up', 'w_gate_up', 'k_k', 'k_a', 'r_k', 'gn_w', 'gn_b', 'sb_gain', 'w_out', 'ln2_g', 'w_up', 'w_down', 'lnf_g']
TWIN_WEIGHTS = ['ln1_g', 'w_in', 'tok_mu', 'w0', 'w_decay_up', 'a0', 'w_aaa_up', 'w_gate_up', 'k_k', 'k_a', 'r_k', 'gn_w', 'gn_b', 'sb_gain', 'w_out', 'ln2_g', 'w_up', 'w_down', 'lnf_g']
TWIN_DIFF_INPUT = 'x'
TWIN_INPUTS = ['x', 'ln1_g', 'w_in', 'tok_mu', 'w0', 'w_decay_up', 'a0', 'w_aaa_up', 'w_gate_up', 'k_k', 'k_a', 'r_k', 'gn_w', 'gn_b', 'sb_gain', 'w_out', 'ln2_g', 'w_up', 'w_down', 'lnf_g', 'loss_target', 'm_ln1_g', 'm_w_in', 'm_tok_mu', 'm_w0', 'm_w_decay_up', 'm_a0', 'm_w_aaa_up', 'm_w_gate_up', 'm_k_k', 'm_k_a', 'm_r_k', 'm_gn_w', 'm_gn_b', 'm_sb_gain', 'm_w_out', 'm_ln2_g', 'm_w_up', 'm_w_down', 'm_lnf_g', 'v_ln1_g', 'v_w_in', 'v_tok_mu', 'v_w0', 'v_w_decay_up', 'v_a0', 'v_w_aaa_up', 'v_w_gate_up', 'v_k_k', 'v_k_a', 'v_r_k', 'v_gn_w', 'v_gn_b', 'v_sb_gain', 'v_w_out', 'v_ln2_g', 'v_w_up', 'v_w_down', 'v_lnf_g']
TWIN_OUTPUTS = ['loss', 'grad_x', 'grad_ln1_g', 'grad_w_in', 'grad_tok_mu', 'grad_w0', 'grad_w_decay_up', 'grad_a0', 'grad_w_aaa_up', 'grad_w_gate_up', 'grad_k_k', 'grad_k_a', 'grad_r_k', 'grad_gn_w', 'grad_gn_b', 'grad_sb_gain', 'grad_w_out', 'grad_ln2_g', 'grad_w_up', 'grad_w_down', 'grad_lnf_g', 'delta_ln1_g', 'delta_w_in', 'delta_tok_mu', 'delta_w0', 'delta_w_decay_up', 'delta_a0', 'delta_w_aaa_up', 'delta_w_gate_up', 'delta_k_k', 'delta_k_a', 'delta_r_k', 'delta_gn_w', 'delta_gn_b', 'delta_sb_gain', 'delta_w_out', 'delta_ln2_g', 'delta_w_up', 'delta_w_down', 'delta_lnf_g', 'new_m_ln1_g', 'new_m_w_in', 'new_m_tok_mu', 'new_m_w0', 'new_m_w_decay_up', 'new_m_a0', 'new_m_w_aaa_up', 'new_m_w_gate_up', 'new_m_k_k', 'new_m_k_a', 'new_m_r_k', 'new_m_gn_w', 'new_m_gn_b', 'new_m_sb_gain', 'new_m_w_out', 'new_m_ln2_g', 'new_m_w_up', 'new_m_w_down', 'new_m_lnf_g', 'new_v_ln1_g', 'new_v_w_in', 'new_v_tok_mu', 'new_v_w0', 'new_v_w_decay_up', 'new_v_a0', 'new_v_w_aaa_up', 'new_v_w_gate_up', 'new_v_k_k', 'new_v_k_a', 'new_v_r_k', 'new_v_gn_w', 'new_v_gn_b', 'new_v_sb_gain', 'new_v_w_out', 'new_v_ln2_g', 'new_v_w_up', 'new_v_w_down', 'new_v_lnf_g']
TWIN_LEAF_KINDS = {'loss': 'loss', 'grad_x': 'grad_x', 'grad_ln1_g': 'grad_w', 'grad_w_in': 'grad_w', 'grad_tok_mu': 'grad_w', 'grad_w0': 'grad_w', 'grad_w_decay_up': 'grad_w', 'grad_a0': 'grad_w', 'grad_w_aaa_up': 'grad_w', 'grad_w_gate_up': 'grad_w', 'grad_k_k': 'grad_w', 'grad_k_a': 'grad_w', 'grad_r_k': 'grad_w', 'grad_gn_w': 'grad_w', 'grad_gn_b': 'grad_w', 'grad_sb_gain': 'grad_w', 'grad_w_out': 'grad_w', 'grad_ln2_g': 'grad_w', 'grad_w_up': 'grad_w', 'grad_w_down': 'grad_w', 'grad_lnf_g': 'grad_w', 'delta_ln1_g': 'delta_w', 'delta_w_in': 'delta_w', 'delta_tok_mu': 'delta_w', 'delta_w0': 'delta_w', 'delta_w_decay_up': 'delta_w', 'delta_a0': 'delta_w', 'delta_w_aaa_up': 'delta_w', 'delta_w_gate_up': 'delta_w', 'delta_k_k': 'delta_w', 'delta_k_a': 'delta_w', 'delta_r_k': 'delta_w', 'delta_gn_w': 'delta_w', 'delta_gn_b': 'delta_w', 'delta_sb_gain': 'delta_w', 'delta_w_out': 'delta_w', 'delta_ln2_g': 'delta_w', 'delta_w_up': 'delta_w', 'delta_w_down': 'delta_w', 'delta_lnf_g': 'delta_w', 'new_m_ln1_g': 'new_m', 'new_m_w_in': 'new_m', 'new_m_tok_mu': 'new_m', 'new_m_w0': 'new_m', 'new_m_w_decay_up': 'new_m', 'new_m_a0': 'new_m', 'new_m_w_aaa_up': 'new_m', 'new_m_w_gate_up': 'new_m', 'new_m_k_k': 'new_m', 'new_m_k_a': 'new_m', 'new_m_r_k': 'new_m', 'new_m_gn_w': 'new_m', 'new_m_gn_b': 'new_m', 'new_m_sb_gain': 'new_m', 'new_m_w_out': 'new_m', 'new_m_ln2_g': 'new_m', 'new_m_w_up': 'new_m', 'new_m_w_down': 'new_m', 'new_m_lnf_g': 'new_m', 'new_v_ln1_g': 'new_v', 'new_v_w_in': 'new_v', 'new_v_tok_mu': 'new_v', 'new_v_w0': 'new_v', 'new_v_w_decay_up': 'new_v', 'new_v_a0': 'new_v', 'new_v_w_aaa_up': 'new_v', 'new_v_w_gate_up': 'new_v', 'new_v_k_k': 'new_v', 'new_v_k_a': 'new_v', 'new_v_r_k': 'new_v', 'new_v_gn_w': 'new_v', 'new_v_gn_b': 'new_v', 'new_v_sb_gain': 'new_v', 'new_v_w_out': 'new_v', 'new_v_ln2_g': 'new_v', 'new_v_w_up': 'new_v', 'new_v_w_down': 'new_v', 'new_v_lnf_g': 'new_v'}


def _forward(args):
    return _fwd_reference(*[args[k] for k in FWD_PARAMS])


def _output_shape():
    out = _jax.eval_shape(lambda: _forward(_fwd_setup_inputs(0)))
    return out.shape, out.dtype

N_MICROBATCH = 1
ADAM_LR = 0.001
ADAM_B1 = 0.9
ADAM_B2 = 0.999
ADAM_EPS = 1e-08
ADAM_WD = 0.01
ADAM_STEP = 10
PER_EXAMPLE_BATCH_AXIS = {'x': 0, 'loss_target': 0}
SHARED_INPUTS = []
_WEIGHT_DTYPES = {'ln1_g': _jnp.float32, 'w_in': _jnp.float32, 'tok_mu': _jnp.float32, 'w0': _jnp.float32, 'w_decay_up': _jnp.float32, 'a0': _jnp.float32, 'w_aaa_up': _jnp.float32, 'w_gate_up': _jnp.float32, 'k_k': _jnp.float32, 'k_a': _jnp.float32, 'r_k': _jnp.float32, 'gn_w': _jnp.float32, 'gn_b': _jnp.float32, 'sb_gain': _jnp.float32, 'w_out': _jnp.float32, 'ln2_g': _jnp.float32, 'w_up': _jnp.float32, 'w_down': _jnp.float32, 'lnf_g': _jnp.float32}
MOMENT_SCALE = {'ln1_g': 2.451854e-01, 'w_in': 1.323508e-01, 'tok_mu': 2.091391e-01, 'w0': 4.691329e-02, 'w_decay_up': 4.957496e-03, 'a0': 6.903663e-02, 'w_aaa_up': 5.185014e-02, 'w_gate_up': 1.124596e-01, 'k_k': 1.165295e-01, 'k_a': 1.529710e-01, 'r_k': 2.552247e-01, 'gn_w': 1.160793e-01, 'gn_b': 1.210168e-01, 'sb_gain': 1.923206e-01, 'w_out': 1.628370e-01, 'ln2_g': 1.225231e-01, 'w_up': 6.236156e-02, 'w_down': 2.228340e-01, 'lnf_g': 6.395881e+01}


def _to_microbatches(a, axis):
    t = _jnp.moveaxis(a, axis, 0)
    t = t.reshape((N_MICROBATCH, t.shape[0] // N_MICROBATCH) + t.shape[1:])
    return _jnp.moveaxis(t, 1, axis + 1)


def setup_inputs(seed: int = 0) -> dict:
    inp = _fwd_setup_inputs(seed)
    key = _jax.random.fold_in(_jax.random.key(seed), 7919)
    shape, _ = _output_shape()
    out = dict(inp)
    out["loss_target"] = _jax.random.normal(_jax.random.fold_in(key, 0), shape, _jnp.float32)
    for i, name in enumerate(TWIN_WEIGHTS):
        w = inp[name].astype(_jnp.float32)
        if MOMENT_SCALE is None:
            s = _jnp.sqrt(_jnp.mean(_jnp.square(w)) + 1e-30)
        else:
            s = MOMENT_SCALE[name]
        km, kv = _jax.random.split(_jax.random.fold_in(key, i + 1))
        out[name] = w
        out["m_" + name] = s * _jax.random.normal(km, w.shape, _jnp.float32)
        out["v_" + name] = (s * s) * _jax.random.uniform(kv, w.shape, _jnp.float32, 0.5, 1.5)
    if N_MICROBATCH > 1:
        for name, axis in PER_EXAMPLE_BATCH_AXIS.items():
            out[name] = _to_microbatches(out[name], axis)
    return {'x': out['x'], 'ln1_g': out['ln1_g'], 'w_in': out['w_in'], 'tok_mu': out['tok_mu'], 'w0': out['w0'], 'w_decay_up': out['w_decay_up'], 'a0': out['a0'], 'w_aaa_up': out['w_aaa_up'], 'w_gate_up': out['w_gate_up'], 'k_k': out['k_k'], 'k_a': out['k_a'], 'r_k': out['r_k'], 'gn_w': out['gn_w'], 'gn_b': out['gn_b'], 'sb_gain': out['sb_gain'], 'w_out': out['w_out'], 'ln2_g': out['ln2_g'], 'w_up': out['w_up'], 'w_down': out['w_down'], 'lnf_g': out['lnf_g'], 'loss_target': out['loss_target'], 'm_ln1_g': out['m_ln1_g'], 'm_w_in': out['m_w_in'], 'm_tok_mu': out['m_tok_mu'], 'm_w0': out['m_w0'], 'm_w_decay_up': out['m_w_decay_up'], 'm_a0': out['m_a0'], 'm_w_aaa_up': out['m_w_aaa_up'], 'm_w_gate_up': out['m_w_gate_up'], 'm_k_k': out['m_k_k'], 'm_k_a': out['m_k_a'], 'm_r_k': out['m_r_k'], 'm_gn_w': out['m_gn_w'], 'm_gn_b': out['m_gn_b'], 'm_sb_gain': out['m_sb_gain'], 'm_w_out': out['m_w_out'], 'm_ln2_g': out['m_ln2_g'], 'm_w_up': out['m_w_up'], 'm_w_down': out['m_w_down'], 'm_lnf_g': out['m_lnf_g'], 'v_ln1_g': out['v_ln1_g'], 'v_w_in': out['v_w_in'], 'v_tok_mu': out['v_tok_mu'], 'v_w0': out['v_w0'], 'v_w_decay_up': out['v_w_decay_up'], 'v_a0': out['v_a0'], 'v_w_aaa_up': out['v_w_aaa_up'], 'v_w_gate_up': out['v_w_gate_up'], 'v_k_k': out['v_k_k'], 'v_k_a': out['v_k_a'], 'v_r_k': out['v_r_k'], 'v_gn_w': out['v_gn_w'], 'v_gn_b': out['v_gn_b'], 'v_sb_gain': out['v_sb_gain'], 'v_w_out': out['v_w_out'], 'v_ln2_g': out['v_ln2_g'], 'v_w_up': out['v_w_up'], 'v_w_down': out['v_w_down'], 'v_lnf_g': out['v_lnf_g']}


def _loss(weights, diff, rest, loss_target):
    with _jax.named_scope("forward"):
        args = {**rest, TWIN_DIFF_INPUT: diff, **{k: w.astype(_WEIGHT_DTYPES[k]) for k, w in weights.items()}}
        y = _forward(args)
    with _jax.named_scope("loss_head"):
        err = _jnp.square(y.astype(_jnp.float32) - loss_target)
        return 0.5 * _jnp.sum(_jnp.mean(err, axis=-1)) if err.ndim else 0.5 * err


def _adamw(w, g, m, v):
    m = ADAM_B1 * m + (1.0 - ADAM_B1) * g
    v = ADAM_B2 * v + (1.0 - ADAM_B2) * _jnp.square(g)
    m_hat = m / (1.0 - ADAM_B1 ** ADAM_STEP)
    v_hat = v / (1.0 - ADAM_B2 ** ADAM_STEP)
    delta = -ADAM_LR * (m_hat / (_jnp.sqrt(v_hat) + ADAM_EPS) + ADAM_WD * w)
    return delta, m, v


def reference(x, ln1_g, w_in, tok_mu, w0, w_decay_up, a0, w_aaa_up, w_gate_up, k_k, k_a, r_k, gn_w, gn_b, sb_gain, w_out, ln2_g, w_up, w_down, lnf_g, loss_target, m_ln1_g, m_w_in, m_tok_mu, m_w0, m_w_decay_up, m_a0, m_w_aaa_up, m_w_gate_up, m_k_k, m_k_a, m_r_k, m_gn_w, m_gn_b, m_sb_gain, m_w_out, m_ln2_g, m_w_up, m_w_down, m_lnf_g, v_ln1_g, v_w_in, v_tok_mu, v_w0, v_w_decay_up, v_a0, v_w_aaa_up, v_w_gate_up, v_k_k, v_k_a, v_r_k, v_gn_w, v_gn_b, v_sb_gain, v_w_out, v_ln2_g, v_w_up, v_w_down, v_lnf_g):
    given = dict(x=x, ln1_g=ln1_g, w_in=w_in, tok_mu=tok_mu, w0=w0, w_decay_up=w_decay_up, a0=a0, w_aaa_up=w_aaa_up, w_gate_up=w_gate_up, k_k=k_k, k_a=k_a, r_k=r_k, gn_w=gn_w, gn_b=gn_b, sb_gain=sb_gain, w_out=w_out, ln2_g=ln2_g, w_up=w_up, w_down=w_down, lnf_g=lnf_g, loss_target=loss_target, m_ln1_g=m_ln1_g, m_w_in=m_w_in, m_tok_mu=m_tok_mu, m_w0=m_w0, m_w_decay_up=m_w_decay_up, m_a0=m_a0, m_w_aaa_up=m_w_aaa_up, m_w_gate_up=m_w_gate_up, m_k_k=m_k_k, m_k_a=m_k_a, m_r_k=m_r_k, m_gn_w=m_gn_w, m_gn_b=m_gn_b, m_sb_gain=m_sb_gain, m_w_out=m_w_out, m_ln2_g=m_ln2_g, m_w_up=m_w_up, m_w_down=m_w_down, m_lnf_g=m_lnf_g, v_ln1_g=v_ln1_g, v_w_in=v_w_in, v_tok_mu=v_tok_mu, v_w0=v_w0, v_w_decay_up=v_w_decay_up, v_a0=v_a0, v_w_aaa_up=v_w_aaa_up, v_w_gate_up=v_w_gate_up, v_k_k=v_k_k, v_k_a=v_k_a, v_r_k=v_r_k, v_gn_w=v_gn_w, v_gn_b=v_gn_b, v_sb_gain=v_sb_gain, v_w_out=v_w_out, v_ln2_g=v_ln2_g, v_w_up=v_w_up, v_w_down=v_w_down, v_lnf_g=v_lnf_g)
    weights = {n: given[n] for n in TWIN_WEIGHTS}
    shared = {n: given[n] for n in SHARED_INPUTS}
    per_example = {n: given[n] for n in ['x']}
    grad_fn = _jax.value_and_grad(_loss, argnums=(0, 1))

    def one_microbatch(ex, loss_target):
        ex = dict(ex)
        diff = ex.pop(TWIN_DIFF_INPUT)
        return grad_fn(weights, diff, {**shared, **ex}, loss_target)

    if N_MICROBATCH == 1:
        loss, (grad_w, grad_x) = one_microbatch(per_example, given["loss_target"])
    else:
        def body(carry, xs):
            loss_sum, grad_sum = carry
            l_k, (gw_k, gx_k) = one_microbatch(xs[0], xs[1])
            with _jax.named_scope("update"):
                return (loss_sum + l_k, _jax.tree.map(_jnp.add, grad_sum, gw_k)), gx_k

        init = (_jnp.zeros((), _jnp.float32), _jax.tree.map(_jnp.zeros_like, weights))
        (loss, grad_w), grad_x = _jax.lax.scan(body, init, (per_example, given["loss_target"]))
    with _jax.named_scope("update"):
        delta_w, new_m, new_v = {}, {}, {}
        for n in TWIN_WEIGHTS:
            delta_w[n], new_m[n], new_v[n] = _adamw(weights[n], grad_w[n], given["m_" + n], given["v_" + n])
    return (loss, grad_x, *[grad_w[n] for n in TWIN_WEIGHTS], *[delta_w[n] for n in TWIN_WEIGHTS],
            *[new_m[n] for n in TWIN_WEIGHTS], *[new_v[n] for n in TWIN_WEIGHTS])
```

```python
import functools
import math

import jax
import jax.numpy as jnp
from jax import lax
from jax.experimental import pallas as pl
from jax.experimental.pallas import tpu as pltpu

F32 = jnp.float32
BF16 = jnp.bfloat16

N_DEV = 8
HEAD_DIM = 64
LANES = 128
RWKV_W = 512
SB_W = 512
LORA_WA = 128
GATE_LORA = 128
RWKV_COLS = 3 * RWKV_W + LORA_WA + GATE_LORA
RMS_EPS = 1e-5
GN_EPS = 64e-5
CHUNK = 64
QB = 128
SB_SCALE = HEAD_DIM ** -0.5
ADAM_LR, ADAM_B1, ADAM_B2, ADAM_EPS, ADAM_WD, ADAM_STEP = 0.001, 0.9, 0.999, 1e-08, 0.01, 10
VMEM_LIMIT = 56 * 1024 * 1024


_DIMS = {
    "nn": (((1,), (0,)), ((), ())),
    "nt": (((1,), (1,)), ((), ())),
    "tn": (((0,), (0,)), ((), ())),
}


def _pieces(x, n):
    if n == 1:
        return [x.astype(BF16)]
    out, rem = [], x.astype(F32)
    for i in range(n):
        p = rem.astype(BF16)
        out.append(p)
        if i + 1 < n:
            rem = rem - p.astype(F32)
    return out


def _dot(a, b, form, pa, pb):
    pieces_a, pieces_b = _pieces(a, pa), _pieces(b, pb)
    keep = max(pa, pb)
    acc = None
    for i, ai in enumerate(pieces_a):
        for j, bj in enumerate(pieces_b):
            if i + j >= keep:
                continue
            t = lax.dot_general(ai, bj, _DIMS[form], preferred_element_type=F32)
            acc = t if acc is None else acc + t
    return acc


BOTH = (True, True)


@functools.partial(jax.custom_vjp, nondiff_argnums=(2, 3, 4, 5))
def mm(a, b, form, pa, pb, diff=BOTH):
    return _dot(a, b, form, pa, pb)


def _mm_fwd(a, b, form, pa, pb, diff):
    return _dot(a, b, form, pa, pb), (a, b)


def _mm_bwd(form, pa, pb, diff, res, g):
    a, b = res
    pg = max(pa, pb)
    da, db = jnp.zeros_like(a), jnp.zeros_like(b)
    if form == "nn":
        if diff[0]:
            da = mm(g, b, "nt", pg, pb)
        if diff[1]:
            db = mm(a, g, "tn", pa, pg)
    elif form == "nt":
        if diff[0]:
            da = mm(g, b, "nn", pg, pb)
        if diff[1]:
            db = mm(g, a, "tn", pg, pa)
    else:
        if diff[0]:
            da = mm(b, g, "nt", pb, pg)
        if diff[1]:
            db = mm(a, g, "nn", pa, pg)
    return da, db


mm.defvjp(_mm_fwd, _mm_bwd)


def _lane_lo(shape):
    return lax.broadcasted_iota(jnp.int32, shape, len(shape) - 1) < HEAD_DIM


def _segsum(x):
    lo = _lane_lo(x.shape)
    s_lo = jnp.sum(jnp.where(lo, x, 0.0), axis=-1, keepdims=True)
    s_hi = jnp.sum(jnp.where(lo, 0.0, x), axis=-1, keepdims=True)
    return jnp.where(lo, s_lo, s_hi)


def _sigmoid(x):
    return 0.5 * (jnp.tanh(0.5 * x) + 1.0)


def _softplus(x):
    return jnp.maximum(x, 0.0) + jnp.log(1.0 + jnp.exp(-jnp.abs(x)))


def f_norm(x, g):
    return x * lax.rsqrt(jnp.mean(x * x, axis=-1, keepdims=True) + RMS_EPS) * g


def f_prep(p, pprev, mu, w0, wd_pad, a0, wa_pad, wg):
    pm = p + mu * (pprev - p)
    r = pm[:, 0:RWKV_W]
    k = pm[:, RWKV_W:2 * RWKV_W]
    v = pm[:, 2 * RWKV_W:3 * RWKV_W]
    xwa = pm[:, 3 * RWKV_W:3 * RWKV_W + LORA_WA]
    xg = pm[:, 3 * RWKV_W + LORA_WA:RWKV_COLS]
    w = -_softplus(-(w0 + mm(jnp.tanh(xwa), wd_pad, "nn", 1, 1))) - 0.5
    lw = -jnp.exp(w)
    asig = _sigmoid(a0 + mm(xwa, wa_pad, "nn", 1, 1))
    gate = mm(_sigmoid(xg), wg, "nn", 1, 1)
    return r, k, v, lw, asig, gate


def _tri(n, kind):
    row = lax.broadcasted_iota(jnp.int32, (n, n), 0)
    col = lax.broadcasted_iota(jnp.int32, (n, n), 1)
    if kind == "lower_incl":
        return row >= col
    return row > col


def rwkv_chunk(state, r, kraw, v, lw, asig, k_k, k_a, r_k, gn_w, gn_b, hp):
    L = r.shape[0]
    lo = _lane_lo((1, LANES))
    kk = kraw * k_k
    kk = kk / jnp.maximum(jnp.sqrt(_segsum(kk * kk)), 1e-12)
    k = kraw * (1.0 + (asig - 1.0) * k_a)
    a = -kk
    b = kk * asig
    incl = _tri(L, "lower_incl")
    strict = _tri(L, "strict")
    c = mm(incl.astype(F32), lw, "nn", 1, 3, (False, True))
    cprev = c - lw
    at = a * jnp.exp(cprev)
    rt = r * jnp.exp(c)
    einv = jnp.exp(-c)
    bt = b * einv
    kt = k * einv
    eye = (lax.broadcasted_iota(jnp.int32, (L, L), 0) == lax.broadcasted_iota(jnp.int32, (L, L), 1)).astype(F32)
    u_all = jnp.zeros((L, LANES), F32)
    y_all = jnp.zeros((L, LANES), F32)
    for head_mask in (lo, jnp.logical_not(lo)):
        at_h = jnp.where(head_mask, at, 0.0)
        rt_h = jnp.where(head_mask, rt, 0.0)
        a_ab = jnp.where(strict, mm(at_h, bt, "nt", hp, hp), 0.0)
        a_ak = jnp.where(strict, mm(at_h, kt, "nt", hp, hp), 0.0)
        b_rb = jnp.where(incl, mm(rt_h, bt, "nt", hp, hp), 0.0)
        b_rk = jnp.where(incl, mm(rt_h, kt, "nt", hp, hp), 0.0)
        tinv = eye + a_ab
        pw = a_ab
        for _ in range(int(math.log2(L)) - 1):
            pw = mm(pw, pw, "nn", hp, hp)
            tinv = tinv + mm(tinv, pw, "nn", hp, hp)
        rhs = mm(at_h, state, "nt", hp, hp) + mm(a_ak, v, "nn", hp, hp)
        u_h = mm(tinv, rhs, "nn", hp, hp)
        y_h = mm(rt_h, state, "nt", hp, hp) + mm(b_rb, u_h, "nn", hp, hp) + mm(b_rk, v, "nn", hp, hp)
        u_all = jnp.where(head_mask, u_h, u_all)
        y_all = jnp.where(head_mask, y_h, y_all)
    c_last = jnp.sum(lw, axis=0, keepdims=True)
    efwd = jnp.exp(c_last - c)
    new_state = state * jnp.exp(c_last) + mm(u_all, b * efwd, "tn", hp, hp) + mm(v, k * efwd, "tn", hp, hp)
    row_head = lax.broadcasted_iota(jnp.int32, (LANES, LANES), 0) // HEAD_DIM
    col_head = lax.broadcasted_iota(jnp.int32, (LANES, LANES), 1) // HEAD_DIM
    new_state = jnp.where(row_head == col_head, new_state, 0.0)
    mean = _segsum(y_all) * (1.0 / HEAD_DIM)
    d = y_all - mean
    var = _segsum(d * d) * (1.0 / HEAD_DIM)
    yn = d * lax.rsqrt(var + GN_EPS) * gn_w + gn_b
    bonus = _segsum(r * k * r_k) * v
    return yn + bonus, new_state


def sb_tile(q, k, v, c_lo, c_hi, diag):
    lo = _lane_lo((1, LANES))
    below = _tri(QB, "strict")
    later = _tri(QB, "strict").astype(F32)
    out = jnp.zeros((QB, LANES), F32)
    sums = []
    for head_mask, carry in ((lo, c_lo), (jnp.logical_not(lo), c_hi)):
        z = mm(jnp.where(head_mask, q, 0.0), k, "nt", 1, 1) * SB_SCALE
        soft = jnp.log(1.0 + jnp.exp(-jnp.abs(z)))
        log_beta = jnp.minimum(z, 0.0) - soft
        log_keep = -jnp.maximum(z, 0.0) - soft
        if diag:
            log_keep = jnp.where(below, log_keep, 0.0)
        after = mm(log_keep, later, "nn", 3, 1, (True, False))
        log_a = log_beta + after + carry
        att = jnp.exp(log_a)
        if diag:
            att = jnp.where(below, att, 0.0)
        out = jnp.where(head_mask, mm(att, v, "nn", 1, 1), out)
        sums.append(jnp.sum(log_keep, axis=1, keepdims=True))
    return out, sums[0], sums[1]


def sb_post(o, gain):
    return o * lax.rsqrt(_segsum(o * o) * (1.0 / HEAD_DIM) + RMS_EPS) * gain


def f_final(x3, g, target):
    y = f_norm(x3, g)
    err = y - target
    return 0.5 * jnp.mean(err * err, axis=-1, keepdims=True)


def _params(sem):
    return pltpu.CompilerParams(dimension_semantics=sem, vmem_limit_bytes=VMEM_LIMIT)


def rw_call(name, body_fn, rows, pars, out_rows, out_accs, tile):
    rows = [item if isinstance(item, tuple) else (item, item.shape[1], 0) for item in rows]
    row_arrays = [arr for arr, _, _ in rows]
    n_rows = row_arrays[0].shape[0]
    tile = min(tile, n_rows)
    steps = n_rows // tile
    row_specs = [pl.BlockSpec((tile, cols), functools.partial(lambda i, c: (i, c), c=cblk)) for _, cols, cblk in rows]
    par_specs = [pl.BlockSpec(p.shape, lambda i: (0, 0)) for p in pars]
    nr, npar, nor, noa = len(row_arrays), len(pars), len(out_rows), len(out_accs)

    def body(*refs):
        row_vals = [refs[i][...] for i in range(nr)]
        par_vals = [refs[nr + i][...] for i in range(npar)]
        o_refs = refs[nr + npar:nr + npar + nor]
        a_refs = refs[nr + npar + nor:]
        row_outs, acc_outs = body_fn(row_vals, par_vals)
        for ref, val in zip(o_refs, row_outs):
            ref[...] = val.astype(ref.dtype)
        if noa:
            first = pl.program_id(0) == 0

            @pl.when(first)
            def _():
                for ref, val in zip(a_refs, acc_outs):
                    ref[...] = val.astype(ref.dtype)

            @pl.when(jnp.logical_not(first))
            def _():
                for ref, val in zip(a_refs, acc_outs):
                    ref[...] = ref[...] + val.astype(ref.dtype)

    out_shape = [jax.ShapeDtypeStruct((n_rows, c), dt) for c, dt in out_rows]
    out_shape += [jax.ShapeDtypeStruct(s, dt) for s, dt in out_accs]
    out_specs = [pl.BlockSpec((tile, c), lambda i: (i, 0)) for c, _ in out_rows]
    out_specs += [pl.BlockSpec(s, lambda i: (0, 0)) for s, _ in out_accs]
    outs = pl.pallas_call(
        body, name=name, grid=(steps,), in_specs=row_specs + par_specs, out_specs=out_specs,
        out_shape=out_shape, compiler_params=_params(("arbitrary",)),
    )(*row_arrays, *pars)
    return outs[:nor], outs[nor:]


def matmul(name, a, b, form, out_dtype, tm, tn, tk):
    tm, tn, tk = min(tm, a.shape[1 if form == "tn" else 0]), min(tn, b.shape[0 if form == "nt" else 1]), min(tk, a.shape[0 if form == "tn" else 1])
    if form == "nn":
        (m, kd), n = a.shape, b.shape[1]
        a_spec = pl.BlockSpec((tm, tk), lambda i, j, k: (i, k))
        b_spec = pl.BlockSpec((tk, tn), lambda i, j, k: (k, j))
    elif form == "nt":
        (m, kd), n = a.shape, b.shape[0]
        a_spec = pl.BlockSpec((tm, tk), lambda i, j, k: (i, k))
        b_spec = pl.BlockSpec((tn, tk), lambda i, j, k: (j, k))
    else:
        (kd, m), n = a.shape, b.shape[1]
        a_spec = pl.BlockSpec((tk, tm), lambda i, j, k: (k, i))
        b_spec = pl.BlockSpec((tk, tn), lambda i, j, k: (k, j))
    ksteps = kd // tk

    def body(a_ref, b_ref, o_ref, acc_ref):
        kstep = pl.program_id(2)
        part = lax.dot_general(a_ref[...].astype(BF16), b_ref[...].astype(BF16), _DIMS[form],
                               preferred_element_type=F32)

        @pl.when(kstep == 0)
        def _():
            acc_ref[...] = part

        @pl.when(kstep > 0)
        def _():
            acc_ref[...] = acc_ref[...] + part

        @pl.when(kstep == ksteps - 1)
        def _():
            o_ref[...] = acc_ref[...].astype(o_ref.dtype)

    return pl.pallas_call(
        body, name=name, grid=(m // tm, n // tn, ksteps), in_specs=[a_spec, b_spec],
        out_specs=pl.BlockSpec((tm, tn), lambda i, j, k: (i, j)),
        out_shape=jax.ShapeDtypeStruct((m, n), out_dtype),
        scratch_shapes=[pltpu.VMEM((tm, tn), F32)],
        compiler_params=_params(("parallel", "parallel", "arbitrary")),
    )(a, b)


RWKV_HP = 2


def _rwkv_specs(n_chunks):
    row = pl.BlockSpec((CHUNK, LANES), lambda h, b, c: (b * n_chunks + c, h))
    par = pl.BlockSpec((1, LANES), lambda h, b, c: (0, h))
    return row, par


def rwkv_fwd(r, kraw, v, lw, asig, k_k, k_a, r_k, gn_w, gn_b, n_seq, seq_len):
    n_chunks = seq_len // CHUNK
    n_pairs = RWKV_W // LANES
    row, par = _rwkv_specs(n_chunks)

    def body(r_ref, k_ref, v_ref, lw_ref, a_ref, kk_ref, ka_ref, rk_ref, gw_ref, gb_ref, z_ref, s0_ref, state):
        @pl.when(pl.program_id(2) == 0)
        def _():
            state[...] = jnp.zeros_like(state)

        s0 = state[...]
        s0_ref[0, 0, 0] = s0
        z, s1 = rwkv_chunk(s0, r_ref[...], k_ref[...], v_ref[...], lw_ref[...], a_ref[...], kk_ref[...],
                           ka_ref[...], rk_ref[...], gw_ref[...], gb_ref[...], RWKV_HP)
        z_ref[...] = z
        state[...] = s1

    return pl.pallas_call(
        body, name="rwkv_fwd", grid=(n_pairs, n_seq, n_chunks),
        in_specs=[row] * 5 + [par] * 5,
        out_specs=[row, pl.BlockSpec((1, 1, 1, LANES, LANES), lambda h, b, c: (h, b, c, 0, 0))],
        out_shape=[jax.ShapeDtypeStruct(r.shape, F32),
                   jax.ShapeDtypeStruct((n_pairs, n_seq, n_chunks, LANES, LANES), F32)],
        scratch_shapes=[pltpu.VMEM((LANES, LANES), F32)],
        compiler_params=_params(("arbitrary", "arbitrary", "arbitrary")),
    )(r, kraw, v, lw, asig, k_k, k_a, r_k, gn_w, gn_b)


def rwkv_bwd(r, kraw, v, lw, asig, k_k, k_a, r_k, gn_w, gn_b, s0_all, dz, n_seq, seq_len):
    n_chunks = seq_len // CHUNK
    n_pairs = RWKV_W // LANES
    row = pl.BlockSpec((CHUNK, LANES), lambda h, b, c: (b * n_chunks + n_chunks - 1 - c, h))
    par = pl.BlockSpec((1, LANES), lambda h, b, c: (0, h))
    s0_spec = pl.BlockSpec((1, 1, 1, LANES, LANES), lambda h, b, c: (h, b, n_chunks - 1 - c, 0, 0))

    def body(r_ref, k_ref, v_ref, lw_ref, a_ref, kk_ref, ka_ref, rk_ref, gw_ref, gb_ref, s0_ref, dz_ref,
             dr_ref, dk_ref, dv_ref, dlw_ref, da_ref, dkk_ref, dka_ref, drk_ref, dgw_ref, dgb_ref, dstate):
        first_of_seq = pl.program_id(2) == 0
        first_of_pair = jnp.logical_and(pl.program_id(1) == 0, first_of_seq)

        @pl.when(first_of_seq)
        def _():
            dstate[...] = jnp.zeros_like(dstate)

        fn = functools.partial(rwkv_chunk, hp=RWKV_HP)
        _, vjp = jax.vjp(fn, s0_ref[0, 0, 0], r_ref[...], k_ref[...], v_ref[...], lw_ref[...], a_ref[...],
                         kk_ref[...], ka_ref[...], rk_ref[...], gw_ref[...], gb_ref[...])
        ds0, dr, dk, dv, dlw, da, dkk, dka, drk, dgw, dgb = vjp((dz_ref[...], dstate[...]))
        dstate[...] = ds0
        dr_ref[...] = dr
        dk_ref[...] = dk
        dv_ref[...] = dv
        dlw_ref[...] = dlw
        da_ref[...] = da
        accs = ((dkk_ref, dkk), (dka_ref, dka), (drk_ref, drk), (dgw_ref, dgw), (dgb_ref, dgb))

        @pl.when(first_of_pair)
        def _():
            for ref, val in accs:
                ref[...] = val

        @pl.when(jnp.logical_not(first_of_pair))
        def _():
            for ref, val in accs:
                ref[...] = ref[...] + val

    rows_shape = jax.ShapeDtypeStruct(r.shape, F32)
    par_shape = jax.ShapeDtypeStruct((1, RWKV_W), F32)
    return pl.pallas_call(
        body, name="rwkv_bwd", grid=(n_pairs, n_seq, n_chunks),
        in_specs=[row] * 5 + [par] * 5 + [s0_spec, row],
        out_specs=[row] * 5 + [par] * 5,
        out_shape=[rows_shape] * 5 + [par_shape] * 5,
        scratch_shapes=[pltpu.VMEM((LANES, LANES), F32)],
        compiler_params=_params(("arbitrary", "arbitrary", "arbitrary")),
    )(r, kraw, v, lw, asig, k_k, k_a, r_k, gn_w, gn_b, s0_all, dz)


SB_Q0 = RWKV_COLS // LANES
SB_K0 = SB_Q0 + SB_W // LANES
SB_V0 = SB_K0 + SB_W // LANES


def _col_of(c_lo, c_hi):
    return jnp.where(_lane_lo((1, LANES)), c_lo, c_hi)


def sb_fwd(p, gain, n_seq, seq_len):
    n_pairs = SB_W // LANES
    n_q = seq_len // QB

    def seq_spec(c0):
        return pl.BlockSpec((seq_len, LANES), functools.partial(lambda b, h, c0: (b, c0 + h), c0=c0))

    out_spec = pl.BlockSpec((seq_len, LANES), lambda b, h: (b, h))

    def body(q_ref, k_ref, v_ref, g_ref, y_ref, o_ref, tot_ref):
        gain = g_ref[...]

        def q_block(i, _):
            qs = pl.multiple_of(i * QB, QB)
            q = q_ref[pl.ds(qs, QB), :]
            zero = jnp.zeros((QB, 1), F32)
            o, s_lo, s_hi = sb_tile(q, k_ref[pl.ds(qs, QB), :], v_ref[pl.ds(qs, QB), :], zero, zero, True)

            def k_block(jj, carry):
                o, c_lo, c_hi = carry
                ks = pl.multiple_of((i - 1 - jj) * QB, QB)
                o2, s_lo, s_hi = sb_tile(q, k_ref[pl.ds(ks, QB), :], v_ref[pl.ds(ks, QB), :], c_lo, c_hi, False)
                return o + o2, c_lo + s_lo, c_hi + s_hi

            o, c_lo, c_hi = lax.fori_loop(0, i, k_block, (o, s_lo, s_hi))
            o_ref[pl.ds(qs, QB), :] = o
            tot_ref[pl.ds(qs, QB), :] = jnp.broadcast_to(_col_of(c_lo, c_hi), (QB, LANES))
            y_ref[pl.ds(qs, QB), :] = sb_post(o, gain)
            return 0

        lax.fori_loop(0, n_q, q_block, 0)

    shape = jax.ShapeDtypeStruct((n_seq * seq_len, SB_W), F32)
    return pl.pallas_call(
        body, name="sb_fwd", grid=(n_seq, n_pairs),
        in_specs=[seq_spec(SB_Q0), seq_spec(SB_K0), seq_spec(SB_V0), pl.BlockSpec((1, LANES), lambda b, h: (0, h))],
        out_specs=[out_spec] * 3, out_shape=[shape] * 3,
        compiler_params=_params(("arbitrary", "arbitrary")),
    )(p, p, p, gain)


def sb_bwd(p, gain, o_raw, tot, dy, n_seq, seq_len):
    n_pairs = SB_W // LANES
    n_q = seq_len // QB

    def seq_spec(c0):
        return pl.BlockSpec((seq_len, LANES), functools.partial(lambda h, b, c0: (b, c0 + h), c0=c0))

    own = pl.BlockSpec((seq_len, LANES), lambda h, b: (b, h))
    par = pl.BlockSpec((1, LANES), lambda h, b: (0, h))

    def body(q_ref, k_ref, v_ref, g_ref, o_ref, tot_ref, dy_ref, dq_ref, dk_ref, dv_ref, dg_ref):
        gain = g_ref[...]
        lo = _lane_lo((1, LANES))
        dk_ref[...] = jnp.zeros_like(dk_ref)
        dv_ref[...] = jnp.zeros_like(dv_ref)

        def q_block(i, dgain):
            qs = pl.multiple_of(i * QB, QB)
            q = q_ref[pl.ds(qs, QB), :]
            _, post_vjp = jax.vjp(sb_post, o_ref[pl.ds(qs, QB), :], gain)
            do, dg_i = post_vjp(dy_ref[pl.ds(qs, QB), :])
            tot_i = tot_ref[pl.ds(qs, QB), :]
            t_lo = jnp.max(jnp.where(lo, tot_i, -jnp.inf), axis=1, keepdims=True)
            t_hi = jnp.max(jnp.where(lo, -jnp.inf, tot_i), axis=1, keepdims=True)

            def tile(ks, carry, diag):
                dq, rem_lo, rem_hi, g_lo, g_hi = carry
                k = k_ref[pl.ds(ks, QB), :]
                v = v_ref[pl.ds(ks, QB), :]
                zero = jnp.zeros((QB, 1), F32)
                _, s_lo, s_hi = sb_tile(q, k, v, zero, zero, diag)
                c_lo = rem_lo - s_lo
                c_hi = rem_hi - s_hi
                _, vjp = jax.vjp(functools.partial(sb_tile, diag=diag), q, k, v, c_lo, c_hi)
                dq_t, dk_t, dv_t, dc_lo, dc_hi = vjp((do, g_lo, g_hi))
                dk_ref[pl.ds(ks, QB), :] = dk_ref[pl.ds(ks, QB), :] + dk_t
                dv_ref[pl.ds(ks, QB), :] = dv_ref[pl.ds(ks, QB), :] + dv_t
                return dq + dq_t, c_lo, c_hi, g_lo + dc_lo, g_hi + dc_hi

            def k_block(j, carry):
                return tile(pl.multiple_of(j * QB, QB), carry, False)

            zero = jnp.zeros((QB, 1), F32)
            carry = (jnp.zeros((QB, LANES), F32), t_lo, t_hi, zero, zero)
            carry = lax.fori_loop(0, i, k_block, carry)
            carry = tile(qs, carry, True)
            dq_ref[pl.ds(qs, QB), :] = carry[0]
            return dgain + dg_i

        dgain = lax.fori_loop(0, n_q, q_block, jnp.zeros((1, LANES), F32))
        first = pl.program_id(1) == 0

        @pl.when(first)
        def _():
            dg_ref[...] = dgain

        @pl.when(jnp.logical_not(first))
        def _():
            dg_ref[...] = dg_ref[...] + dgain

    shape = jax.ShapeDtypeStruct((n_seq * seq_len, SB_W), F32)
    return pl.pallas_call(
        body, name="sb_bwd", grid=(n_pairs, n_seq),
        in_specs=[seq_spec(SB_Q0), seq_spec(SB_K0), seq_spec(SB_V0), par, own, own, own],
        out_specs=[own, own, own, par],
        out_shape=[shape, shape, shape, jax.ShapeDtypeStruct((1, SB_W), F32)],
        compiler_params=_params(("arbitrary", "arbitrary")),
    )(p, p, p, gain, o_raw, tot, dy)


def exchange(name, arrays, gather):
    n = len(arrays)

    def body(*refs):
        ins, outs = refs[:n], refs[n:2 * n]
        send_sems, recv_sems, local_sems = refs[2 * n:]
        x, y, c = lax.axis_index("x"), lax.axis_index("y"), lax.axis_index("c")
        me = 4 * x + 2 * y + c
        copies = []
        for a in range(n):
            own = ins[a] if gather else ins[a].at[me]
            local = pltpu.make_async_copy(own, outs[a].at[me], local_sems.at[a])
            local.start()
            copies.append(local)
            for j in range(1, N_DEV):
                px, py, pc = (x + (j >> 2)) % 2, (y + ((j >> 1) & 1)) % 2, (c + (j & 1)) % 2
                src = ins[a] if gather else ins[a].at[4 * px + 2 * py + pc]
                remote = pltpu.make_async_remote_copy(
                    src_ref=src, dst_ref=outs[a].at[me], send_sem=send_sems.at[a, j - 1],
                    recv_sem=recv_sems.at[a, j - 1], device_id=(px, py, pc), device_id_type=pl.DeviceIdType.MESH)
                remote.start()
                copies.append(remote)
        for cp in copies:
            cp.wait()

    any_spec = pl.BlockSpec(memory_space=pl.ANY)
    out_shape = [jax.ShapeDtypeStruct(((N_DEV,) + a.shape) if gather else a.shape, a.dtype) for a in arrays]
    return pl.pallas_call(
        body, name=name, in_specs=[any_spec] * n, out_specs=[any_spec] * n, out_shape=out_shape,
        scratch_shapes=[pltpu.SemaphoreType.DMA((n, N_DEV - 1)), pltpu.SemaphoreType.DMA((n, N_DEV - 1)),
                        pltpu.SemaphoreType.DMA((n,))],
    )(*arrays)


def adamw(name, w, parts, m, v, tile):
    rows, cols = w.shape
    spec = pl.BlockSpec((tile, cols), lambda i: (i, 0))
    part_spec = pl.BlockSpec((N_DEV, tile, cols), lambda i: (0, i, 0))

    def body(w_ref, p_ref, m_ref, v_ref, g_ref, d_ref, nm_ref, nv_ref):
        g = p_ref[0]
        for s in range(1, N_DEV):
            g = g + p_ref[s]
        new_m = ADAM_B1 * m_ref[...] + (1.0 - ADAM_B1) * g
        new_v = ADAM_B2 * v_ref[...] + (1.0 - ADAM_B2) * (g * g)
        m_hat = new_m / (1.0 - ADAM_B1 ** ADAM_STEP)
        v_hat = new_v / (1.0 - ADAM_B2 ** ADAM_STEP)
        g_ref[...] = g
        d_ref[...] = -ADAM_LR * (m_hat / (jnp.sqrt(v_hat) + ADAM_EPS) + ADAM_WD * w_ref[...])
        nm_ref[...] = new_m
        nv_ref[...] = new_v

    shape = jax.ShapeDtypeStruct((rows, cols), F32)
    return pl.pallas_call(
        body, name=name, grid=(rows // tile,), in_specs=[spec, part_spec, spec, spec],
        out_specs=[spec] * 4, out_shape=[shape] * 4, compiler_params=_params(("arbitrary",)),
    )(w, parts, m, v)


SMALL = ("ln1_g", "tok_mu", "w0", "a0", "k_k", "k_a", "r_k", "gn_w", "gn_b", "sb_gain", "ln2_g", "lnf_g")
BIG = ("w_in", "w_decay_up", "w_aaa_up", "w_gate_up", "w_out", "w_up", "w_down")
COL_SHARDED = ("w_in", "w_decay_up", "w_aaa_up", "w_gate_up", "w_up")
ORDER = ("ln1_g", "w_in", "tok_mu", "w0", "w_decay_up", "a0", "w_aaa_up", "w_gate_up", "k_k", "k_a", "r_k",
         "gn_w", "gn_b", "sb_gain", "w_out", "ln2_g", "w_up", "w_down", "lnf_g")


def _pack(vectors, rows):
    flat = jnp.concatenate([v.reshape(-1).astype(F32) for v in vectors])
    return jnp.pad(flat, (0, rows * LANES - flat.shape[0])).reshape(rows, LANES)


def _full_cols(gathered):
    d, k, cols = gathered.shape
    return jnp.transpose(gathered, (1, 0, 2)).reshape(k, d * cols)


def _col_parts(full):
    k, n = full.shape
    return jnp.transpose(full.reshape(k, N_DEV, n // N_DEV), (1, 0, 2))


def kernel(x, ln1_g, w_in, tok_mu, w0, w_decay_up, a0, w_aaa_up, w_gate_up, k_k, k_a, r_k, gn_w, gn_b, sb_gain, w_out, ln2_g, w_up, w_down, lnf_g, loss_target, m_ln1_g, m_w_in, m_tok_mu, m_w0, m_w_decay_up, m_a0, m_w_aaa_up, m_w_gate_up, m_k_k, m_k_a, m_r_k, m_gn_w, m_gn_b, m_sb_gain, m_w_out, m_ln2_g, m_w_up, m_w_down, m_lnf_g, v_ln1_g, v_w_in, v_tok_mu, v_w0, v_w_decay_up, v_a0, v_w_aaa_up, v_w_gate_up, v_k_k, v_k_a, v_r_k, v_gn_w, v_gn_b, v_sb_gain, v_w_out, v_ln2_g, v_w_up, v_w_down, v_lnf_g):
    args = dict(locals())
    weights = {n: args[n] for n in ORDER}
    mom_m = {n: args["m_" + n] for n in ORDER}
    mom_v = {n: args["v_" + n] for n in ORDER}

    n_seq, seq_len, d_model = x.shape
    n_tok = n_seq * seq_len
    x2d = x.reshape(n_tok, d_model)
    tgt = loss_target.reshape(n_tok, d_model)
    row = lambda t: t.reshape(1, -1).astype(F32)

    shards = [weights[n][0].astype(BF16) for n in BIG]
    gathered = dict(zip(BIG, exchange("gather_weights", shards, True)))
    w_in_f = _full_cols(gathered["w_in"])
    zeros64 = jnp.zeros((HEAD_DIM, RWKV_W), BF16)
    wd_pad = jnp.concatenate([_full_cols(gathered["w_decay_up"]), zeros64], axis=0)
    wa_pad = jnp.concatenate([zeros64, _full_cols(gathered["w_aaa_up"])], axis=0)
    wg_f = _full_cols(gathered["w_gate_up"])
    w_out_f = gathered["w_out"].reshape(d_model, d_model)
    w_up_f = _full_cols(gathered["w_up"])
    w_down_f = gathered["w_down"].reshape(-1, d_model)
    d_ff = w_up_f.shape[1]
    in_cols = w_in_f.shape[1]

    g1, mu, w0r, a0r = row(ln1_g), row(tok_mu), row(w0), row(a0)
    kkr, kar, rkr, gwr, gbr, sgr = row(k_k), row(k_a), row(r_k), row(gn_w), row(gn_b), row(sb_gain)
    g2, gf = row(ln2_g), row(lnf_g)

    (h1,), _ = rw_call("norm1", lambda r, p: ([f_norm(r[0], p[0])], []), [x2d], [g1], [(d_model, BF16)], [], 512)
    p = matmul("proj_in", h1, w_in_f, "nn", F32, 512, in_cols // 2, d_model)
    p_rwkv = p[:, :RWKV_COLS].reshape(n_seq, seq_len, RWKV_COLS)
    p_prev = jnp.pad(p_rwkv[:, :-1], ((0, 0), (1, 0), (0, 0))).reshape(n_tok, RWKV_COLS)
    prep_pars = [mu, w0r, wd_pad, a0r, wa_pad, wg_f]
    (r_, kraw, v_, lw, asig, gate), _ = rw_call(
        "rwkv_prep", lambda r, q: (list(f_prep(r[0], r[1], *q)), []), [(p, RWKV_COLS, 0), p_prev], prep_pars,
        [(RWKV_W, F32)] * 6, [], 256)
    z, s0_all = rwkv_fwd(r_, kraw, v_, lw, asig, kkr, kar, rkr, gwr, gbr, n_seq, seq_len)
    y_sb, o_raw, tot = sb_fwd(p, sgr, n_seq, seq_len)
    (ycat,), _ = rw_call("mix_cat", lambda r, q: ([jnp.concatenate([r[0] * r[1], r[2]], axis=1)], []),
                         [z, gate, y_sb], [], [(d_model, BF16)], [], 512)
    mix = matmul("proj_out", ycat, w_out_f, "nn", F32, 512, d_model, d_model)
    (x2, h2), _ = rw_call("resid_norm2", lambda r, q: ([r[0] + r[1], f_norm(r[0] + r[1], q[0])], []),
                          [x2d, mix], [g2], [(d_model, F32), (d_model, BF16)], [], 512)
    u = matmul("mlp_up", h2, w_up_f, "nn", F32, 512, d_ff // 4, d_model)
    (act,), _ = rw_call("relu2", lambda r, q: ([jnp.square(jnp.maximum(r[0], 0.0))], []), [u], [],
                        [(d_ff, BF16)], [], 256)
    mlp = matmul("mlp_down", act, w_down_f, "nn", F32, 512, d_model, d_ff // 2)

    def final_body(r, q):
        x3 = r[0] + r[1]
        loss_rows, vjp = jax.vjp(lambda a, g: f_final(a, g, r[2]), x3, q[0])
        dx3, dgf = vjp(jnp.ones_like(loss_rows))
        return [dx3], [jnp.broadcast_to(jnp.sum(loss_rows), (1, LANES)), dgf]

    (dx3,), (loss_acc, d_lnf) = rw_call("loss_head", final_body, [x2, mlp, tgt], [gf], [(d_model, F32)],
                                        [((1, LANES), F32), ((1, d_model), F32)], 256)

    dact = matmul("d_act", dx3, w_down_f, "nt", F32, 512, d_ff // 4, d_model)
    dw_down = matmul("dw_down", act, dx3, "tn", F32, 512, d_model, 512)
    (du,), _ = rw_call("d_relu2", lambda r, q: ([r[0] * (2.0 * jnp.maximum(r[1], 0.0))], []), [dact, u], [],
                       [(d_ff, BF16)], [], 256)
    dh2 = matmul("d_h2", du, w_up_f, "nt", F32, 512, d_model, d_ff // 2)
    dw_up = matmul("dw_up", h2, du, "tn", F32, d_model, 512, 512)

    def norm_bwd(r, q):
        xv, dh, dres = r
        _, vjp = jax.vjp(f_norm, xv, q[0])
        dx, dg = vjp(dh)
        return [dx + dres], [dg]

    (dx2,), (d_ln2,) = rw_call("d_norm2", norm_bwd, [x2, dh2, dx3], [g2], [(d_model, F32)],
                               [((1, d_model), F32)], 256)

    dycat = matmul("d_ycat", dx2, w_out_f, "nt", F32, 512, d_model, d_model)
    dw_out = matmul("dw_out", ycat, dx2, "tn", F32, d_model, d_model, 512)
    (dz, dgate), _ = rw_call("d_mix", lambda r, q: ([r[0] * r[2], r[0] * r[1]], []),
                             [(dycat, RWKV_W, 0), z, gate], [], [(RWKV_W, F32)] * 2, [], 512)
    dq, dk_sb, dv_sb, d_sg = sb_bwd(p, sgr, o_raw, tot, dycat[:, RWKV_W:], n_seq, seq_len)
    dr, dkraw, dv, dlw, dasig, d_kk, d_ka, d_rk, d_gw, d_gb = rwkv_bwd(
        r_, kraw, v_, lw, asig, kkr, kar, rkr, gwr, gbr, s0_all, dz, n_seq, seq_len)

    def prep_bwd(r, q):
        _, vjp = jax.vjp(f_prep, r[0], r[1], *[t.astype(F32) for t in q])
        grads = vjp(tuple(r[2:8]))
        return list(grads[:2]), list(grads[2:])

    par_shapes = [(t.shape, F32) for t in prep_pars]
    (dp_rwkv, dp_prev), (d_mu, d_w0, d_wd, d_a0, d_wa, d_wg) = rw_call(
        "d_rwkv_prep", prep_bwd, [(p, RWKV_COLS, 0), p_prev, dr, dkraw, dv, dlw, dasig, dgate], prep_pars,
        [(RWKV_COLS, F32)] * 2, par_shapes, 128)
    dp_shift = jnp.pad(dp_prev.reshape(n_seq, seq_len, RWKV_COLS)[:, 1:], ((0, 0), (0, 1), (0, 0)))
    dp = jnp.concatenate([dp_rwkv + dp_shift.reshape(n_tok, RWKV_COLS), dq, dk_sb, dv_sb], axis=1)

    dh1 = matmul("d_h1", dp, w_in_f, "nt", F32, 512, d_model, in_cols // 2)
    dw_in = matmul("dw_in", h1, dp, "tn", F32, d_model, in_cols // 2, 512)
    (dx,), (d_ln1,) = rw_call("d_norm1", norm_bwd, [x2d, dh1, dx2], [g1], [(d_model, F32)],
                              [((1, d_model), F32)], 256)

    big_grads = {
        "w_in": _col_parts(dw_in), "w_decay_up": _col_parts(d_wd[:HEAD_DIM]), "w_aaa_up": _col_parts(d_wa[HEAD_DIM:]),
        "w_gate_up": _col_parts(d_wg), "w_out": dw_out.reshape(N_DEV, -1, d_model), "w_up": _col_parts(dw_up),
        "w_down": dw_down.reshape(N_DEV, -1, d_model),
    }
    parts = dict(zip(BIG, exchange("scatter_grads", [big_grads[n] for n in BIG], False)))

    small_grads = {"ln1_g": d_ln1, "tok_mu": d_mu, "w0": d_w0, "a0": d_a0, "k_k": d_kk, "k_a": d_ka, "r_k": d_rk,
                   "gn_w": d_gw, "gn_b": d_gb, "sb_gain": d_sg, "ln2_g": d_ln2, "lnf_g": d_lnf}
    n_small = sum(int(weights[n].size) for n in SMALL)
    pack_rows = -(-(n_small + 1) // (8 * LANES)) * 8
    packed = _pack([small_grads[n] for n in SMALL] + [loss_acc[0, :1]], pack_rows)
    (small_parts,) = exchange("gather_small", [packed], True)

    results = {}
    for n in BIG:
        w2d = weights[n][0]
        tile = w2d.shape[0] if w2d.shape[0] <= 256 else 256
        results[n] = adamw("adamw_" + n, w2d, parts[n], mom_m[n][0], mom_v[n][0], tile)
    pk = lambda d: _pack([d[n] for n in SMALL] + [jnp.zeros((1,), F32)], pack_rows)
    sg, sd, sm, sv = adamw("adamw_small", pk(weights), small_parts, pk(mom_m), pk(mom_v), pack_rows)
    off = 0
    for n in SMALL:
        size = int(weights[n].size)
        results[n] = tuple(t.reshape(-1)[off:off + size] for t in (sg, sd, sm, sv))
        off += size
    loss = sg.reshape(-1)[off]

    out = [loss, dx.reshape(x.shape)]
    for kind in range(4):
        out += [results[n][kind].reshape(weights[n].shape) for n in ORDER]
    return tuple(out)
```

```python
import functools
import math

import jax
import jax.numpy as jnp
from jax import lax
from jax.experimental import pallas as pl
from jax.experimental.pallas import tpu as pltpu

F32 = jnp.float32
BF16 = jnp.bfloat16

N_DEV = 8
HEAD_DIM = 64
LANES = 128
RWKV_W = 512
SB_W = 512
LORA_WA = 128
GATE_LORA = 128
RWKV_COLS = 3 * RWKV_W + LORA_WA + GATE_LORA
RMS_EPS = 1e-5
GN_EPS = 64e-5
CHUNK = 64
QB = 128
SB_SCALE = HEAD_DIM ** -0.5
ADAM_LR, ADAM_B1, ADAM_B2, ADAM_EPS, ADAM_WD, ADAM_STEP = 0.001, 0.9, 0.999, 1e-08, 0.01, 10
VMEM_LIMIT = 56 * 1024 * 1024


_DIMS = {
    "nn": (((1,), (0,)), ((), ())),
    "nt": (((1,), (1,)), ((), ())),
    "tn": (((0,), (0,)), ((), ())),
}


def _pieces(x, n):
    if n == 1:
        return [x.astype(BF16)]
    out, rem = [], x.astype(F32)
    for i in range(n):
        p = rem.astype(BF16)
        out.append(p)
        if i + 1 < n:
            rem = rem - p.astype(F32)
    return out


def _dot(a, b, form, pa, pb):
    pieces_a, pieces_b = _pieces(a, pa), _pieces(b, pb)
    keep = max(pa, pb)
    acc = None
    for i, ai in enumerate(pieces_a):
        for j, bj in enumerate(pieces_b):
            if i + j >= keep:
                continue
            t = lax.dot_general(ai, bj, _DIMS[form], preferred_element_type=F32)
            acc = t if acc is None else acc + t
    return acc


BOTH = (True, True)


@functools.partial(jax.custom_vjp, nondiff_argnums=(2, 3, 4, 5))
def mm(a, b, form, pa, pb, diff=BOTH):
    return _dot(a, b, form, pa, pb)


def _mm_fwd(a, b, form, pa, pb, diff):
    return _dot(a, b, form, pa, pb), (a, b)


def _mm_bwd(form, pa, pb, diff, res, g):
    a, b = res
    pg = max(pa, pb)
    da, db = jnp.zeros_like(a), jnp.zeros_like(b)
    if form == "nn":
        if diff[0]:
            da = mm(g, b, "nt", pg, pb)
        if diff[1]:
            db = mm(a, g, "tn", pa, pg)
    elif form == "nt":
        if diff[0]:
            da = mm(g, b, "nn", pg, pb)
        if diff[1]:
            db = mm(g, a, "tn", pg, pa)
    else:
        if diff[0]:
            da = mm(b, g, "nt", pb, pg)
        if diff[1]:
            db = mm(a, g, "nn", pa, pg)
    return da, db


mm.defvjp(_mm_fwd, _mm_bwd)


def _lane_lo(shape):
    return lax.broadcasted_iota(jnp.int32, shape, len(shape) - 1) < HEAD_DIM


def _segsum(x):
    lo = _lane_lo(x.shape)
    s_lo = jnp.sum(jnp.where(lo, x, 0.0), axis=-1, keepdims=True)
    s_hi = jnp.sum(jnp.where(lo, 0.0, x), axis=-1, keepdims=True)
    return jnp.where(lo, s_lo, s_hi)


def _sigmoid(x):
    return 0.5 * (jnp.tanh(0.5 * x) + 1.0)


def _softplus(x):
    return jnp.maximum(x, 0.0) + jnp.log(1.0 + jnp.exp(-jnp.abs(x)))


def f_norm(x, g):
    return x * lax.rsqrt(jnp.mean(x * x, axis=-1, keepdims=True) + RMS_EPS) * g


def f_prep(p, pprev, mu, w0, wd_pad, a0, wa_pad, wg):
    pm = p + mu * (pprev - p)
    r = pm[:, 0:RWKV_W]
    k = pm[:, RWKV_W:2 * RWKV_W]
    v = pm[:, 2 * RWKV_W:3 * RWKV_W]
    xwa = pm[:, 3 * RWKV_W:3 * RWKV_W + LORA_WA]
    xg = pm[:, 3 * RWKV_W + LORA_WA:RWKV_COLS]
    w = -_softplus(-(w0 + mm(jnp.tanh(xwa), wd_pad, "nn", 1, 1))) - 0.5
    lw = -jnp.exp(w)
    asig = _sigmoid(a0 + mm(xwa, wa_pad, "nn", 1, 1))
    gate = mm(_sigmoid(xg), wg, "nn", 1, 1)
    return r, k, v, lw, asig, gate


def _tri(n, kind):
    row = lax.broadcasted_iota(jnp.int32, (n, n), 0)
    col = lax.broadcasted_iota(jnp.int32, (n, n), 1)
    if kind == "lower_incl":
        return row >= col
    return row > col


def rwkv_chunk(state, r, kraw, v, lw, asig, k_k, k_a, r_k, gn_w, gn_b, hp):
    n = len(r)
    L = r[0].shape[0]
    lo = _lane_lo((1, LANES))
    masks = (lo, jnp.logical_not(lo))
    incl = _tri(L, "lower_incl")
    strict = _tri(L, "strict")
    tri = incl.astype(F32)
    eye = (lax.broadcasted_iota(jnp.int32, (L, L), 0) == lax.broadcasted_iota(jnp.int32, (L, L), 1)).astype(F32)
    kk = [x * k_k for x in kraw]
    kk = [x / jnp.maximum(jnp.sqrt(_segsum(x * x)), 1e-12) for x in kk]
    k = [x * (1.0 + (s - 1.0) * k_a) for x, s in zip(kraw, asig)]
    b = [x * s for x, s in zip(kk, asig)]
    c = [mm(tri, x, "nn", 1, 3, (False, True)) for x in lw]
    at = [-x * jnp.exp(ci - li) for x, ci, li in zip(kk, c, lw)]
    rt = [x * jnp.exp(ci) for x, ci in zip(r, c)]
    einv = [jnp.exp(-ci) for ci in c]
    bt = [x * e for x, e in zip(b, einv)]
    kt = [x * e for x, e in zip(k, einv)]
    inst = [(s, m) for s in range(n) for m in masks]
    at_h = [jnp.where(m, at[s], 0.0) for s, m in inst]
    rt_h = [jnp.where(m, rt[s], 0.0) for s, m in inst]
    a_ab = [jnp.where(strict, mm(x, bt[s], "nt", hp, hp), 0.0) for x, (s, _) in zip(at_h, inst)]
    a_ak = [jnp.where(strict, mm(x, kt[s], "nt", hp, hp), 0.0) for x, (s, _) in zip(at_h, inst)]
    b_rb = [jnp.where(incl, mm(x, bt[s], "nt", hp, hp), 0.0) for x, (s, _) in zip(rt_h, inst)]
    b_rk = [jnp.where(incl, mm(x, kt[s], "nt", hp, hp), 0.0) for x, (s, _) in zip(rt_h, inst)]
    tinv = [eye + x for x in a_ab]
    pw = a_ab
    for _ in range(int(math.log2(L)) - 1):
        pw = [mm(x, x, "nn", hp, hp) for x in pw]
        tinv = [t + mm(t, x, "nn", hp, hp) for t, x in zip(tinv, pw)]
    rhs = [mm(x, state[s], "nt", hp, hp) + mm(y, v[s], "nn", hp, hp) for x, y, (s, _) in zip(at_h, a_ak, inst)]
    u_h = [mm(t, x, "nn", hp, hp) for t, x in zip(tinv, rhs)]
    y_h = [mm(x, state[s], "nt", hp, hp) + mm(m1, u, "nn", hp, hp) + mm(m2, v[s], "nn", hp, hp)
           for x, m1, m2, u, (s, _) in zip(rt_h, b_rb, b_rk, u_h, inst)]
    u_all = [jnp.where(lo, u_h[2 * s], u_h[2 * s + 1]) for s in range(n)]
    y_all = [jnp.where(lo, y_h[2 * s], y_h[2 * s + 1]) for s in range(n)]
    c_last = [jnp.sum(x, axis=0, keepdims=True) for x in lw]
    efwd = [jnp.exp(cl - ci) for cl, ci in zip(c_last, c)]
    new_state = [st * jnp.exp(cl) + mm(u, bi * e, "tn", hp, hp) + mm(vi, ki * e, "tn", hp, hp)
                 for st, cl, u, bi, e, vi, ki in zip(state, c_last, u_all, b, efwd, v, k)]
    row_head = lax.broadcasted_iota(jnp.int32, (LANES, LANES), 0) // HEAD_DIM
    col_head = lax.broadcasted_iota(jnp.int32, (LANES, LANES), 1) // HEAD_DIM
    new_state = [jnp.where(row_head == col_head, x, 0.0) for x in new_state]
    outs = []
    for y, ri, ki, vi in zip(y_all, r, k, v):
        mean = _segsum(y) * (1.0 / HEAD_DIM)
        d = y - mean
        var = _segsum(d * d) * (1.0 / HEAD_DIM)
        yn = d * lax.rsqrt(var + GN_EPS) * gn_w + gn_b
        outs.append(yn + _segsum(ri * ki * r_k) * vi)
    return outs, new_state


def sb_tile(q, k, v, c_lo, c_hi, diag, from_here=None):
    n = len(q)
    lo = _lane_lo((1, LANES))
    masks = (lo, jnp.logical_not(lo))
    below = _tri(QB, "strict")
    later = _tri(QB, "strict").astype(F32)
    inst = [(s, h) for s in range(n) for h in (0, 1)]
    carry = [(c_lo[s], c_hi[s])[h] for s, h in inst]
    z = [mm(jnp.where(masks[h], q[s], 0.0), k[s], "nt", 1, 1) * SB_SCALE for s, h in inst]
    soft = [jnp.log(1.0 + jnp.exp(-jnp.abs(x))) for x in z]
    log_beta = [jnp.minimum(x, 0.0) - sp for x, sp in zip(z, soft)]
    log_keep = [-jnp.maximum(x, 0.0) - sp for x, sp in zip(z, soft)]
    if diag:
        log_keep = [jnp.where(below, x, 0.0) for x in log_keep]
    own = [jnp.sum(x, axis=1, keepdims=True) for x in log_keep]
    if from_here is not None:
        carry = [lax.stop_gradient(from_here[s][h] - o) + cr for (s, h), o, cr in zip(inst, own, carry)]
    after = [mm(x, later, "nn", 3, 1, (True, False)) for x in log_keep]
    att = [jnp.exp(lb + af + cr) for lb, af, cr in zip(log_beta, after, carry)]
    if diag:
        att = [jnp.where(below, x, 0.0) for x in att]
    out_h = [mm(x, v[s], "nn", 1, 1) for x, (s, _) in zip(att, inst)]
    out = [jnp.where(lo, out_h[2 * s], out_h[2 * s + 1]) for s in range(n)]
    return out, [own[2 * s] for s in range(n)], [own[2 * s + 1] for s in range(n)]


def sb_post(o, gain):
    return o * lax.rsqrt(_segsum(o * o) * (1.0 / HEAD_DIM) + RMS_EPS) * gain


def f_final(x3, g, target):
    y = f_norm(x3, g)
    err = y - target
    return 0.5 * jnp.mean(err * err, axis=-1, keepdims=True)


def _params(sem):
    return pltpu.CompilerParams(dimension_semantics=sem, vmem_limit_bytes=VMEM_LIMIT)


def rw_call(name, body_fn, rows, pars, out_rows, out_accs, tile):
    rows = [item if isinstance(item, tuple) else (item, item.shape[1], 0) for item in rows]
    row_arrays = [arr for arr, _, _ in rows]
    n_rows = row_arrays[0].shape[0]
    tile = min(tile, n_rows)
    steps = n_rows // tile
    row_specs = [pl.BlockSpec((tile, cols), functools.partial(lambda i, c: (i, c), c=cblk)) for _, cols, cblk in rows]
    par_specs = [pl.BlockSpec(p.shape, lambda i: (0, 0)) for p in pars]
    nr, npar, nor, noa = len(row_arrays), len(pars), len(out_rows), len(out_accs)

    def body(*refs):
        row_vals = [refs[i][...] for i in range(nr)]
        par_vals = [refs[nr + i][...] for i in range(npar)]
        o_refs = refs[nr + npar:nr + npar + nor]
        a_refs = refs[nr + npar + nor:]
        row_outs, acc_outs = body_fn(row_vals, par_vals)
        for ref, val in zip(o_refs, row_outs):
            ref[...] = val.astype(ref.dtype)
        if noa:
            first = pl.program_id(0) == 0

            @pl.when(first)
            def _():
                for ref, val in zip(a_refs, acc_outs):
                    ref[...] = val.astype(ref.dtype)

            @pl.when(jnp.logical_not(first))
            def _():
                for ref, val in zip(a_refs, acc_outs):
                    ref[...] = ref[...] + val.astype(ref.dtype)

    out_shape = [jax.ShapeDtypeStruct((n_rows, c), dt) for c, dt in out_rows]
    out_shape += [jax.ShapeDtypeStruct(s, dt) for s, dt in out_accs]
    out_specs = [pl.BlockSpec((tile, c), lambda i: (i, 0)) for c, _ in out_rows]
    out_specs += [pl.BlockSpec(s, lambda i: (0, 0)) for s, _ in out_accs]
    outs = pl.pallas_call(
        body, name=name, grid=(steps,), in_specs=row_specs + par_specs, out_specs=out_specs,
        out_shape=out_shape, compiler_params=_params(("arbitrary",)),
    )(*row_arrays, *pars)
    return outs[:nor], outs[nor:]


def matmul(name, a, b, form, out_dtype, tm, tn, tk):
    tm, tn, tk = min(tm, a.shape[1 if form == "tn" else 0]), min(tn, b.shape[0 if form == "nt" else 1]), min(tk, a.shape[0 if form == "tn" else 1])
    if form == "nn":
        (m, kd), n = a.shape, b.shape[1]
        a_spec = pl.BlockSpec((tm, tk), lambda i, j, k: (i, k))
        b_spec = pl.BlockSpec((tk, tn), lambda i, j, k: (k, j))
    elif form == "nt":
        (m, kd), n = a.shape, b.shape[0]
        a_spec = pl.BlockSpec((tm, tk), lambda i, j, k: (i, k))
        b_spec = pl.BlockSpec((tn, tk), lambda i, j, k: (j, k))
    else:
        (kd, m), n = a.shape, b.shape[1]
        a_spec = pl.BlockSpec((tk, tm), lambda i, j, k: (k, i))
        b_spec = pl.BlockSpec((tk, tn), lambda i, j, k: (k, j))
    ksteps = kd // tk

    def body(a_ref, b_ref, o_ref, acc_ref):
        kstep = pl.program_id(2)
        part = lax.dot_general(a_ref[...].astype(BF16), b_ref[...].astype(BF16), _DIMS[form],
                               preferred_element_type=F32)

        @pl.when(kstep == 0)
        def _():
            acc_ref[...] = part

        @pl.when(kstep > 0)
        def _():
            acc_ref[...] = acc_ref[...] + part

        @pl.when(kstep == ksteps - 1)
        def _():
            o_ref[...] = acc_ref[...].astype(o_ref.dtype)

    return pl.pallas_call(
        body, name=name, grid=(m // tm, n // tn, ksteps), in_specs=[a_spec, b_spec],
        out_specs=pl.BlockSpec((tm, tn), lambda i, j, k: (i, j)),
        out_shape=jax.ShapeDtypeStruct((m, n), out_dtype),
        scratch_shapes=[pltpu.VMEM((tm, tn), F32)],
        compiler_params=_params(("parallel", "parallel", "arbitrary")),
    )(a, b)


RWKV_HP = 2


def rwkv_fwd(r, kraw, v, lw, asig, k_k, k_a, r_k, gn_w, gn_b, n_seq, seq_len):
    n_chunks = seq_len // CHUNK
    n_pairs = RWKV_W // LANES
    row = pl.BlockSpec((n_seq, CHUNK, LANES), lambda h, c: (0, c, h))
    par = pl.BlockSpec((1, LANES), lambda h, c: (0, h))
    s0_spec = pl.BlockSpec((1, 1, n_seq, LANES, LANES), lambda h, c: (h, c, 0, 0, 0))

    def body(r_ref, k_ref, v_ref, lw_ref, a_ref, kk_ref, ka_ref, rk_ref, gw_ref, gb_ref, z_ref, s0_ref, state):
        @pl.when(pl.program_id(1) == 0)
        def _():
            state[...] = jnp.zeros_like(state)

        pars = [ref[...] for ref in (kk_ref, ka_ref, rk_ref, gw_ref, gb_ref)]
        seqs = range(n_seq)
        s0 = [state[s] for s in seqs]
        rows = [[ref[s] for s in seqs] for ref in (r_ref, k_ref, v_ref, lw_ref, a_ref)]
        z, s1 = rwkv_chunk(s0, *rows, *pars, RWKV_HP)
        for s in seqs:
            s0_ref[0, 0, s] = s0[s]
            z_ref[s] = z[s]
            state[s] = s1[s]

    return pl.pallas_call(
        body, name="rwkv_fwd", grid=(n_pairs, n_chunks),
        in_specs=[row] * 5 + [par] * 5, out_specs=[row, s0_spec],
        out_shape=[jax.ShapeDtypeStruct(r.shape, F32),
                   jax.ShapeDtypeStruct((n_pairs, n_chunks, n_seq, LANES, LANES), F32)],
        scratch_shapes=[pltpu.VMEM((n_seq, LANES, LANES), F32)],
        compiler_params=_params(("arbitrary", "arbitrary")),
    )(r, kraw, v, lw, asig, k_k, k_a, r_k, gn_w, gn_b)


def rwkv_bwd(r, kraw, v, lw, asig, k_k, k_a, r_k, gn_w, gn_b, s0_all, dz, n_seq, seq_len):
    n_chunks = seq_len // CHUNK
    n_pairs = RWKV_W // LANES
    row = pl.BlockSpec((n_seq, CHUNK, LANES), lambda h, c: (0, n_chunks - 1 - c, h))
    par = pl.BlockSpec((1, LANES), lambda h, c: (0, h))
    s0_spec = pl.BlockSpec((1, 1, n_seq, LANES, LANES), lambda h, c: (h, n_chunks - 1 - c, 0, 0, 0))

    def body(r_ref, k_ref, v_ref, lw_ref, a_ref, kk_ref, ka_ref, rk_ref, gw_ref, gb_ref, s0_ref, dz_ref,
             dr_ref, dk_ref, dv_ref, dlw_ref, da_ref, dkk_ref, dka_ref, drk_ref, dgw_ref, dgb_ref, dstate):
        first = pl.program_id(1) == 0

        @pl.when(first)
        def _():
            dstate[...] = jnp.zeros_like(dstate)

        pars = [ref[...] for ref in (kk_ref, ka_ref, rk_ref, gw_ref, gb_ref)]
        fn = functools.partial(rwkv_chunk, hp=RWKV_HP)
        seqs = range(n_seq)
        rows = [[ref[s] for s in seqs] for ref in (r_ref, k_ref, v_ref, lw_ref, a_ref)]
        _, vjp = jax.vjp(fn, [s0_ref[0, 0, s] for s in seqs], *rows, *pars)
        grads = vjp(([dz_ref[s] for s in seqs], [dstate[s] for s in seqs]))
        for s in seqs:
            dstate[s] = grads[0][s]
            for ref, val in zip((dr_ref, dk_ref, dv_ref, dlw_ref, da_ref), grads[1:6]):
                ref[s] = val[s]
        accs = tuple(zip((dkk_ref, dka_ref, drk_ref, dgw_ref, dgb_ref), grads[6:]))

        @pl.when(first)
        def _():
            for ref, val in accs:
                ref[...] = val

        @pl.when(jnp.logical_not(first))
        def _():
            for ref, val in accs:
                ref[...] = ref[...] + val

    rows_shape = jax.ShapeDtypeStruct(r.shape, F32)
    par_shape = jax.ShapeDtypeStruct((1, RWKV_W), F32)
    return pl.pallas_call(
        body, name="rwkv_bwd", grid=(n_pairs, n_chunks),
        in_specs=[row] * 5 + [par] * 5 + [s0_spec, row],
        out_specs=[row] * 5 + [par] * 5,
        out_shape=[rows_shape] * 5 + [par_shape] * 5,
        scratch_shapes=[pltpu.VMEM((n_seq, LANES, LANES), F32)],
        compiler_params=_params(("arbitrary", "arbitrary")),
    )(r, kraw, v, lw, asig, k_k, k_a, r_k, gn_w, gn_b, s0_all, dz)


SB_Q0 = RWKV_COLS // LANES
SB_K0 = SB_Q0 + SB_W // LANES
SB_V0 = SB_K0 + SB_W // LANES
SB_SEQS = 2


def _col_of(c_lo, c_hi):
    return jnp.where(_lane_lo((1, LANES)), c_lo, c_hi)


def sb_fwd(p, gain, n_seq, seq_len):
    n_pairs = SB_W // LANES
    n_q = seq_len // QB
    nb = min(SB_SEQS, n_seq)

    def seq_spec(c0):
        return pl.BlockSpec((nb, seq_len, LANES), functools.partial(lambda b, h, c0: (b, 0, c0 + h), c0=c0))

    out_spec = pl.BlockSpec((nb, seq_len, LANES), lambda b, h: (b, 0, h))

    def body(q_ref, k_ref, v_ref, g_ref, y_ref, o_ref, tot_ref):
        gain = g_ref[...]

        def q_block(i, _):
            qs = pl.multiple_of(i * QB, QB)
            seqs = range(nb)
            zeros = [jnp.zeros((QB, 1), F32)] * nb
            qv = [q_ref[s, pl.ds(qs, QB), :] for s in seqs]

            def tiles(ks, c_lo, c_hi, diag):
                return sb_tile(qv, [k_ref[s, pl.ds(ks, QB), :] for s in seqs],
                               [v_ref[s, pl.ds(ks, QB), :] for s in seqs], c_lo, c_hi, diag)

            def k_block(jj, carry):
                o, c_lo, c_hi = carry
                o2, s_lo, s_hi = tiles(pl.multiple_of((i - 1 - jj) * QB, QB), c_lo, c_hi, False)
                add = lambda xs, ys: [x + y for x, y in zip(xs, ys)]
                return add(o, o2), add(c_lo, s_lo), add(c_hi, s_hi)

            o, c_lo, c_hi = lax.fori_loop(0, i, k_block, tiles(qs, zeros, zeros, True))
            for s in seqs:
                o_ref[s, pl.ds(qs, QB), :] = o[s]
                tot_ref[s, pl.ds(qs, QB), :] = jnp.broadcast_to(_col_of(c_lo[s], c_hi[s]), (QB, LANES))
                y_ref[s, pl.ds(qs, QB), :] = sb_post(o[s], gain)
            return 0

        lax.fori_loop(0, n_q, q_block, 0)

    shape = jax.ShapeDtypeStruct((n_seq, seq_len, SB_W), F32)
    return pl.pallas_call(
        body, name="sb_fwd", grid=(n_seq // nb, n_pairs),
        in_specs=[seq_spec(SB_Q0), seq_spec(SB_K0), seq_spec(SB_V0), pl.BlockSpec((1, LANES), lambda b, h: (0, h))],
        out_specs=[out_spec] * 3, out_shape=[shape] * 3,
        compiler_params=_params(("arbitrary", "arbitrary")),
    )(p, p, p, gain)


def sb_bwd(p, gain, o_raw, tot, dy, n_seq, seq_len):
    n_pairs = SB_W // LANES
    n_q = seq_len // QB
    nb = min(SB_SEQS, n_seq)

    def seq_spec(c0):
        return pl.BlockSpec((nb, seq_len, LANES), functools.partial(lambda h, b, c0: (b, 0, c0 + h), c0=c0))

    own = pl.BlockSpec((nb, seq_len, LANES), lambda h, b: (b, 0, h))
    par = pl.BlockSpec((1, LANES), lambda h, b: (0, h))

    def body(q_ref, k_ref, v_ref, g_ref, o_ref, tot_ref, dy_ref, dq_ref, dk_ref, dv_ref, dg_ref):
        gain = g_ref[...]
        lo = _lane_lo((1, LANES))
        dk_ref[...] = jnp.zeros_like(dk_ref)
        dv_ref[...] = jnp.zeros_like(dv_ref)

        def q_block(i, dgain):
            qs = pl.multiple_of(i * QB, QB)
            seqs = range(nb)
            zeros = [jnp.zeros((QB, 1), F32)] * nb
            qv, dov, t_lo, t_hi = [], [], [], []
            for s in seqs:
                qv.append(q_ref[s, pl.ds(qs, QB), :])
                _, post_vjp = jax.vjp(sb_post, o_ref[s, pl.ds(qs, QB), :], gain)
                do, dg_s = post_vjp(dy_ref[s, pl.ds(qs, QB), :])
                dov.append(do)
                dgain = dgain + dg_s
                tot_s = tot_ref[s, pl.ds(qs, QB), :]
                t_lo.append(jnp.max(jnp.where(lo, tot_s, -jnp.inf), axis=1, keepdims=True))
                t_hi.append(jnp.max(jnp.where(lo, -jnp.inf, tot_s), axis=1, keepdims=True))
            add = lambda xs, ys: [x + y for x, y in zip(xs, ys)]
            sub = lambda xs, ys: [x - y for x, y in zip(xs, ys)]

            def tile(ks, carry, diag):
                dq, rem_lo, rem_hi, g_lo, g_hi = carry
                kv = [k_ref[s, pl.ds(ks, QB), :] for s in seqs]
                vv = [v_ref[s, pl.ds(ks, QB), :] for s in seqs]
                fn = functools.partial(sb_tile, diag=diag, from_here=list(zip(rem_lo, rem_hi)))
                (_, s_lo, s_hi), vjp = jax.vjp(fn, qv, kv, vv, zeros, zeros)
                dq_t, dk_t, dv_t, dc_lo, dc_hi = vjp((dov, g_lo, g_hi))
                for s in seqs:
                    dk_ref[s, pl.ds(ks, QB), :] = dk_ref[s, pl.ds(ks, QB), :] + dk_t[s]
                    dv_ref[s, pl.ds(ks, QB), :] = dv_ref[s, pl.ds(ks, QB), :] + dv_t[s]
                return add(dq, dq_t), sub(rem_lo, s_lo), sub(rem_hi, s_hi), add(g_lo, dc_lo), add(g_hi, dc_hi)

            def k_block(j, carry):
                return tile(pl.multiple_of(j * QB, QB), carry, False)

            carry = ([jnp.zeros((QB, LANES), F32)] * nb, t_lo, t_hi, zeros, zeros)
            carry = lax.fori_loop(0, i, k_block, carry)
            carry = tile(qs, carry, True)
            for s in seqs:
                dq_ref[s, pl.ds(qs, QB), :] = carry[0][s]
            return dgain

        dgain = lax.fori_loop(0, n_q, q_block, jnp.zeros((1, LANES), F32))
        first = pl.program_id(1) == 0

        @pl.when(first)
        def _():
            dg_ref[...] = dgain

        @pl.when(jnp.logical_not(first))
        def _():
            dg_ref[...] = dg_ref[...] + dgain

    shape = jax.ShapeDtypeStruct((n_seq, seq_len, SB_W), F32)
    return pl.pallas_call(
        body, name="sb_bwd", grid=(n_pairs, n_seq // nb),
        in_specs=[seq_spec(SB_Q0), seq_spec(SB_K0), seq_spec(SB_V0), par, own, own, own],
        out_specs=[own, own, own, par],
        out_shape=[shape, shape, shape, jax.ShapeDtypeStruct((1, SB_W), F32)],
        compiler_params=_params(("arbitrary", "arbitrary")),
    )(p, p, p, gain, o_raw, tot, dy)


def exchange(name, arrays, gather):
    n = len(arrays)

    def body(*refs):
        ins, outs = refs[:n], refs[n:2 * n]
        send_sems, recv_sems, local_sems = refs[2 * n:]
        x, y, c = lax.axis_index("x"), lax.axis_index("y"), lax.axis_index("c")
        me = 4 * x + 2 * y + c
        copies = []
        for a in range(n):
            own = ins[a] if gather else ins[a].at[me]
            local = pltpu.make_async_copy(own, outs[a].at[me], local_sems.at[a])
            local.start()
            copies.append(local)
            for j in range(1, N_DEV):
                px, py, pc = (x + (j >> 2)) % 2, (y + ((j >> 1) & 1)) % 2, (c + (j & 1)) % 2
                src = ins[a] if gather else ins[a].at[4 * px + 2 * py + pc]
                remote = pltpu.make_async_remote_copy(
                    src_ref=src, dst_ref=outs[a].at[me], send_sem=send_sems.at[a, j - 1],
                    recv_sem=recv_sems.at[a, j - 1], device_id=(px, py, pc), device_id_type=pl.DeviceIdType.MESH)
                remote.start()
                copies.append(remote)
        for cp in copies:
            cp.wait()

    any_spec = pl.BlockSpec(memory_space=pl.ANY)
    out_shape = [jax.ShapeDtypeStruct(((N_DEV,) + a.shape) if gather else a.shape, a.dtype) for a in arrays]
    return pl.pallas_call(
        body, name=name, in_specs=[any_spec] * n, out_specs=[any_spec] * n, out_shape=out_shape,
        scratch_shapes=[pltpu.SemaphoreType.DMA((n, N_DEV - 1)), pltpu.SemaphoreType.DMA((n, N_DEV - 1)),
                        pltpu.SemaphoreType.DMA((n,))],
    )(*arrays)


def adamw(name, w, parts, m, v, tile):
    rows, cols = w.shape
    spec = pl.BlockSpec((tile, cols), lambda i: (i, 0))
    part_spec = pl.BlockSpec((N_DEV, tile, cols), lambda i: (0, i, 0))

    def body(w_ref, p_ref, m_ref, v_ref, g_ref, d_ref, nm_ref, nv_ref):
        g = p_ref[0]
        for s in range(1, N_DEV):
            g = g + p_ref[s]
        new_m = ADAM_B1 * m_ref[...] + (1.0 - ADAM_B1) * g
        new_v = ADAM_B2 * v_ref[...] + (1.0 - ADAM_B2) * (g * g)
        m_hat = new_m / (1.0 - ADAM_B1 ** ADAM_STEP)
        v_hat = new_v / (1.0 - ADAM_B2 ** ADAM_STEP)
        g_ref[...] = g
        d_ref[...] = -ADAM_LR * (m_hat / (jnp.sqrt(v_hat) + ADAM_EPS) + ADAM_WD * w_ref[...])
        nm_ref[...] = new_m
        nv_ref[...] = new_v

    shape = jax.ShapeDtypeStruct((rows, cols), F32)
    return pl.pallas_call(
        body, name=name, grid=(rows // tile,), in_specs=[spec, part_spec, spec, spec],
        out_specs=[spec] * 4, out_shape=[shape] * 4, compiler_params=_params(("arbitrary",)),
    )(w, parts, m, v)


SMALL = ("ln1_g", "tok_mu", "w0", "a0", "k_k", "k_a", "r_k", "gn_w", "gn_b", "sb_gain", "ln2_g", "lnf_g")
BIG = ("w_in", "w_decay_up", "w_aaa_up", "w_gate_up", "w_out", "w_up", "w_down")
COL_SHARDED = ("w_in", "w_decay_up", "w_aaa_up", "w_gate_up", "w_up")
ORDER = ("ln1_g", "w_in", "tok_mu", "w0", "w_decay_up", "a0", "w_aaa_up", "w_gate_up", "k_k", "k_a", "r_k",
         "gn_w", "gn_b", "sb_gain", "w_out", "ln2_g", "w_up", "w_down", "lnf_g")


def _pack(vectors, rows):
    flat = jnp.concatenate([v.reshape(-1).astype(F32) for v in vectors])
    return jnp.pad(flat, (0, rows * LANES - flat.shape[0])).reshape(rows, LANES)


def _full_cols(gathered):
    d, k, cols = gathered.shape
    return jnp.transpose(gathered, (1, 0, 2)).reshape(k, d * cols)


def _col_parts(full):
    k, n = full.shape
    return jnp.transpose(full.reshape(k, N_DEV, n // N_DEV), (1, 0, 2))


def kernel(x, ln1_g, w_in, tok_mu, w0, w_decay_up, a0, w_aaa_up, w_gate_up, k_k, k_a, r_k, gn_w, gn_b, sb_gain, w_out, ln2_g, w_up, w_down, lnf_g, loss_target, m_ln1_g, m_w_in, m_tok_mu, m_w0, m_w_decay_up, m_a0, m_w_aaa_up, m_w_gate_up, m_k_k, m_k_a, m_r_k, m_gn_w, m_gn_b, m_sb_gain, m_w_out, m_ln2_g, m_w_up, m_w_down, m_lnf_g, v_ln1_g, v_w_in, v_tok_mu, v_w0, v_w_decay_up, v_a0, v_w_aaa_up, v_w_gate_up, v_k_k, v_k_a, v_r_k, v_gn_w, v_gn_b, v_sb_gain, v_w_out, v_ln2_g, v_w_up, v_w_down, v_lnf_g):
    args = dict(locals())
    weights = {n: args[n] for n in ORDER}
    mom_m = {n: args["m_" + n] for n in ORDER}
    mom_v = {n: args["v_" + n] for n in ORDER}

    n_seq, seq_len, d_model = x.shape
    n_tok = n_seq * seq_len
    x2d = x.reshape(n_tok, d_model)
    tgt = loss_target.reshape(n_tok, d_model)
    row = lambda t: t.reshape(1, -1).astype(F32)

    shards = [weights[n][0].astype(BF16) for n in BIG]
    gathered = dict(zip(BIG, exchange("gather_weights", shards, True)))
    w_in_f = _full_cols(gathered["w_in"])
    zeros64 = jnp.zeros((HEAD_DIM, RWKV_W), BF16)
    wd_pad = jnp.concatenate([_full_cols(gathered["w_decay_up"]), zeros64], axis=0)
    wa_pad = jnp.concatenate([zeros64, _full_cols(gathered["w_aaa_up"])], axis=0)
    wg_f = _full_cols(gathered["w_gate_up"])
    w_out_f = gathered["w_out"].reshape(d_model, d_model)
    w_up_f = _full_cols(gathered["w_up"])
    w_down_f = gathered["w_down"].reshape(-1, d_model)
    d_ff = w_up_f.shape[1]
    in_cols = w_in_f.shape[1]

    g1, mu, w0r, a0r = row(ln1_g), row(tok_mu), row(w0), row(a0)
    kkr, kar, rkr, gwr, gbr, sgr = row(k_k), row(k_a), row(r_k), row(gn_w), row(gn_b), row(sb_gain)
    g2, gf = row(ln2_g), row(lnf_g)

    (h1,), _ = rw_call("norm1", lambda r, p: ([f_norm(r[0], p[0])], []), [x2d], [g1], [(d_model, BF16)], [], 512)
    p = matmul("proj_in", h1, w_in_f, "nn", F32, 512, in_cols // 2, d_model)
    p_rwkv = p[:, :RWKV_COLS].reshape(n_seq, seq_len, RWKV_COLS)
    p_prev = jnp.pad(p_rwkv[:, :-1], ((0, 0), (1, 0), (0, 0))).reshape(n_tok, RWKV_COLS)
    prep_pars = [mu, w0r, wd_pad, a0r, wa_pad, wg_f]
    (r_, kraw, v_, lw, asig, gate), _ = rw_call(
        "rwkv_prep", lambda r, q: (list(f_prep(r[0], r[1], *q)), []), [(p, RWKV_COLS, 0), p_prev], prep_pars,
        [(RWKV_W, F32)] * 6, [], 256)
    by_seq = lambda t: t.reshape(n_seq, seq_len, t.shape[-1])
    flat = lambda t: t.reshape(n_tok, t.shape[-1])
    rwkv_in = [by_seq(t) for t in (r_, kraw, v_, lw, asig)]
    z, s0_all = rwkv_fwd(*rwkv_in, kkr, kar, rkr, gwr, gbr, n_seq, seq_len)
    z = flat(z)
    y_sb, o_raw, tot = sb_fwd(by_seq(p), sgr, n_seq, seq_len)
    y_sb = flat(y_sb)
    (ycat,), _ = rw_call("mix_cat", lambda r, q: ([jnp.concatenate([r[0] * r[1], r[2]], axis=1)], []),
                         [z, gate, y_sb], [], [(d_model, BF16)], [], 512)
    mix = matmul("proj_out", ycat, w_out_f, "nn", F32, 512, d_model, d_model)
    (x2, h2), _ = rw_call("resid_norm2", lambda r, q: ([r[0] + r[1], f_norm(r[0] + r[1], q[0])], []),
                          [x2d, mix], [g2], [(d_model, F32), (d_model, BF16)], [], 512)
    u = matmul("mlp_up", h2, w_up_f, "nn", F32, 512, d_ff // 4, d_model)
    (act,), _ = rw_call("relu2", lambda r, q: ([jnp.square(jnp.maximum(r[0], 0.0))], []), [u], [],
                        [(d_ff, BF16)], [], 256)
    mlp = matmul("mlp_down", act, w_down_f, "nn", F32, 512, d_model, d_ff // 2)

    def final_body(r, q):
        x3 = r[0] + r[1]
        loss_rows, vjp = jax.vjp(lambda a, g: f_final(a, g, r[2]), x3, q[0])
        dx3, dgf = vjp(jnp.ones_like(loss_rows))
        return [dx3], [jnp.broadcast_to(jnp.sum(loss_rows), (1, LANES)), dgf]

    (dx3,), (loss_acc, d_lnf) = rw_call("loss_head", final_body, [x2, mlp, tgt], [gf], [(d_model, F32)],
                                        [((1, LANES), F32), ((1, d_model), F32)], 256)

    dact = matmul("d_act", dx3, w_down_f, "nt", F32, 512, d_ff // 4, d_model)
    dw_down = matmul("dw_down", act, dx3, "tn", F32, 512, d_model, 512)
    (du,), _ = rw_call("d_relu2", lambda r, q: ([r[0] * (2.0 * jnp.maximum(r[1], 0.0))], []), [dact, u], [],
                       [(d_ff, BF16)], [], 256)
    dh2 = matmul("d_h2", du, w_up_f, "nt", F32, 512, d_model, d_ff // 2)
    dw_up = matmul("dw_up", h2, du, "tn", F32, d_model, 512, 512)

    def norm_bwd(r, q):
        xv, dh, dres = r
        _, vjp = jax.vjp(f_norm, xv, q[0])
        dx, dg = vjp(dh)
        return [dx + dres], [dg]

    (dx2,), (d_ln2,) = rw_call("d_norm2", norm_bwd, [x2, dh2, dx3], [g2], [(d_model, F32)],
                               [((1, d_model), F32)], 256)

    dycat = matmul("d_ycat", dx2, w_out_f, "nt", F32, 512, d_model, d_model)
    dw_out = matmul("dw_out", ycat, dx2, "tn", F32, d_model, d_model, 512)
    (dz, dgate), _ = rw_call("d_mix", lambda r, q: ([r[0] * r[2], r[0] * r[1]], []),
                             [(dycat, RWKV_W, 0), z, gate], [], [(RWKV_W, F32)] * 2, [], 512)
    dq, dk_sb, dv_sb, d_sg = sb_bwd(by_seq(p), sgr, o_raw, tot, by_seq(dycat[:, RWKV_W:]), n_seq, seq_len)
    dq, dk_sb, dv_sb = flat(dq), flat(dk_sb), flat(dv_sb)
    dr, dkraw, dv, dlw, dasig, d_kk, d_ka, d_rk, d_gw, d_gb = rwkv_bwd(
        *rwkv_in, kkr, kar, rkr, gwr, gbr, s0_all, by_seq(dz), n_seq, seq_len)
    dr, dkraw, dv, dlw, dasig = [flat(t) for t in (dr, dkraw, dv, dlw, dasig)]

    def prep_bwd(r, q):
        _, vjp = jax.vjp(f_prep, r[0], r[1], *[t.astype(F32) for t in q])
        grads = vjp(tuple(r[2:8]))
        return list(grads[:2]), list(grads[2:])

    par_shapes = [(t.shape, F32) for t in prep_pars]
    (dp_rwkv, dp_prev), (d_mu, d_w0, d_wd, d_a0, d_wa, d_wg) = rw_call(
        "d_rwkv_prep", prep_bwd, [(p, RWKV_COLS, 0), p_prev, dr, dkraw, dv, dlw, dasig, dgate], prep_pars,
        [(RWKV_COLS, F32)] * 2, par_shapes, 128)
    dp_shift = jnp.pad(dp_prev.reshape(n_seq, seq_len, RWKV_COLS)[:, 1:], ((0, 0), (0, 1), (0, 0)))
    dp = jnp.concatenate([dp_rwkv + dp_shift.reshape(n_tok, RWKV_COLS), dq, dk_sb, dv_sb], axis=1)

    dh1 = matmul("d_h1", dp, w_in_f, "nt", F32, 512, d_model, in_cols // 2)
    dw_in = matmul("dw_in", h1, dp, "tn", F32, d_model, in_cols // 2, 512)
    (dx,), (d_ln1,) = rw_call("d_norm1", norm_bwd, [x2d, dh1, dx2], [g1], [(d_model, F32)],
                              [((1, d_model), F32)], 256)

    big_grads = {
        "w_in": _col_parts(dw_in), "w_decay_up": _col_parts(d_wd[:HEAD_DIM]), "w_aaa_up": _col_parts(d_wa[HEAD_DIM:]),
        "w_gate_up": _col_parts(d_wg), "w_out": dw_out.reshape(N_DEV, -1, d_model), "w_up": _col_parts(dw_up),
        "w_down": dw_down.reshape(N_DEV, -1, d_model),
    }
    parts = dict(zip(BIG, exchange("scatter_grads", [big_grads[n] for n in BIG], False)))

    small_grads = {"ln1_g": d_ln1, "tok_mu": d_mu, "w0": d_w0, "a0": d_a0, "k_k": d_kk, "k_a": d_ka, "r_k": d_rk,
                   "gn_w": d_gw, "gn_b": d_gb, "sb_gain": d_sg, "ln2_g": d_ln2, "lnf_g": d_lnf}
    n_small = sum(int(weights[n].size) for n in SMALL)
    pack_rows = -(-(n_small + 1) // (8 * LANES)) * 8
    packed = _pack([small_grads[n] for n in SMALL] + [loss_acc[0, :1]], pack_rows)
    (small_parts,) = exchange("gather_small", [packed], True)

    results = {}
    for n in BIG:
        w2d = weights[n][0]
        tile = w2d.shape[0] if w2d.shape[0] <= 256 else 256
        results[n] = adamw("adamw_" + n, w2d, parts[n], mom_m[n][0], mom_v[n][0], tile)
    pk = lambda d: _pack([d[n] for n in SMALL] + [jnp.zeros((1,), F32)], pack_rows)
    sg, sd, sm, sv = adamw("adamw_small", pk(weights), small_parts, pk(mom_m), pk(mom_v), pack_rows)
    off = 0
    for n in SMALL:
        size = int(weights[n].size)
        results[n] = tuple(t.reshape(-1)[off:off + size] for t in (sg, sd, sm, sv))
        off += size
    loss = sg.reshape(-1)[off]

    out = [loss, dx.reshape(x.shape)]
    for kind in range(4):
        out += [results[n][kind].reshape(weights[n].shape) for n in ORDER]
    return tuple(out)
```

```python
import functools
import math

import jax
import jax.numpy as jnp
from jax import lax
from jax.experimental import pallas as pl
from jax.experimental.pallas import tpu as pltpu

F32 = jnp.float32
BF16 = jnp.bfloat16

N_DEV = 8
HEAD_DIM = 64
LANES = 128
RWKV_W = 512
SB_W = 512
LORA_WA = 128
GATE_LORA = 128
RWKV_COLS = 3 * RWKV_W + LORA_WA + GATE_LORA
RMS_EPS = 1e-5
GN_EPS = 64e-5
CHUNK = 64
QB = 128
SB_SCALE = HEAD_DIM ** -0.5
ADAM_LR, ADAM_B1, ADAM_B2, ADAM_EPS, ADAM_WD, ADAM_STEP = 0.001, 0.9, 0.999, 1e-08, 0.01, 10
VMEM_LIMIT = 56 * 1024 * 1024


_DIMS = {
    "nn": (((1,), (0,)), ((), ())),
    "nt": (((1,), (1,)), ((), ())),
    "tn": (((0,), (0,)), ((), ())),
}


def _pieces(x, n):
    if n == 1:
        return [x.astype(BF16)]
    out, rem = [], x.astype(F32)
    for i in range(n):
        p = rem.astype(BF16)
        out.append(p)
        if i + 1 < n:
            rem = rem - p.astype(F32)
    return out


def _dot(a, b, form, pa, pb):
    pieces_a, pieces_b = _pieces(a, pa), _pieces(b, pb)
    keep = max(pa, pb)
    acc = None
    for i, ai in enumerate(pieces_a):
        for j, bj in enumerate(pieces_b):
            if i + j >= keep:
                continue
            t = lax.dot_general(ai, bj, _DIMS[form], preferred_element_type=F32)
            acc = t if acc is None else acc + t
    return acc


BOTH = (True, True)


@functools.partial(jax.custom_vjp, nondiff_argnums=(2, 3, 4, 5))
def mm(a, b, form, pa, pb, diff=BOTH):
    return _dot(a, b, form, pa, pb)


def _mm_fwd(a, b, form, pa, pb, diff):
    return _dot(a, b, form, pa, pb), (a, b)


def _mm_bwd(form, pa, pb, diff, res, g):
    a, b = res
    pg = max(pa, pb)
    da, db = jnp.zeros_like(a), jnp.zeros_like(b)
    if form == "nn":
        if diff[0]:
            da = mm(g, b, "nt", pg, pb)
        if diff[1]:
            db = mm(a, g, "tn", pa, pg)
    elif form == "nt":
        if diff[0]:
            da = mm(g, b, "nn", pg, pb)
        if diff[1]:
            db = mm(g, a, "tn", pg, pa)
    else:
        if diff[0]:
            da = mm(b, g, "nt", pb, pg)
        if diff[1]:
            db = mm(a, g, "nn", pa, pg)
    return da, db


mm.defvjp(_mm_fwd, _mm_bwd)


def _lane_lo(shape):
    return lax.broadcasted_iota(jnp.int32, shape, len(shape) - 1) < HEAD_DIM


def _segsum(x):
    lo = _lane_lo(x.shape)
    s_lo = jnp.sum(jnp.where(lo, x, 0.0), axis=-1, keepdims=True)
    s_hi = jnp.sum(jnp.where(lo, 0.0, x), axis=-1, keepdims=True)
    return jnp.where(lo, s_lo, s_hi)


def _sigmoid(x):
    return 0.5 * (jnp.tanh(0.5 * x) + 1.0)


def _softplus(x):
    return jnp.maximum(x, 0.0) + jnp.log(1.0 + jnp.exp(-jnp.abs(x)))


def f_norm(x, g):
    return x * lax.rsqrt(jnp.mean(x * x, axis=-1, keepdims=True) + RMS_EPS) * g


def f_prep(p, pprev, mu, w0, wd_pad, a0, wa_pad, wg):
    pm = p + mu * (pprev - p)
    r = pm[:, 0:RWKV_W]
    k = pm[:, RWKV_W:2 * RWKV_W]
    v = pm[:, 2 * RWKV_W:3 * RWKV_W]
    xwa = pm[:, 3 * RWKV_W:3 * RWKV_W + LORA_WA]
    xg = pm[:, 3 * RWKV_W + LORA_WA:RWKV_COLS]
    w = -_softplus(-(w0 + mm(jnp.tanh(xwa), wd_pad, "nn", 1, 1))) - 0.5
    lw = -jnp.exp(w)
    asig = _sigmoid(a0 + mm(xwa, wa_pad, "nn", 1, 1))
    gate = mm(_sigmoid(xg), wg, "nn", 1, 1)
    return r, k, v, lw, asig, gate


def _tri(n, kind):
    row = lax.broadcasted_iota(jnp.int32, (n, n), 0)
    col = lax.broadcasted_iota(jnp.int32, (n, n), 1)
    if kind == "lower_incl":
        return row >= col
    return row > col


def rwkv_chunk(state, r, kraw, v, lw, asig, k_k, k_a, r_k, gn_w, gn_b, hp):
    n = len(r)
    L = r[0].shape[0]
    lo = _lane_lo((1, LANES))
    masks = (lo, jnp.logical_not(lo))
    incl = _tri(L, "lower_incl")
    strict = _tri(L, "strict")
    tri = incl.astype(F32)
    eye = (lax.broadcasted_iota(jnp.int32, (L, L), 0) == lax.broadcasted_iota(jnp.int32, (L, L), 1)).astype(F32)
    kk = [x * k_k for x in kraw]
    kk = [x / jnp.maximum(jnp.sqrt(_segsum(x * x)), 1e-12) for x in kk]
    k = [x * (1.0 + (s - 1.0) * k_a) for x, s in zip(kraw, asig)]
    b = [x * s for x, s in zip(kk, asig)]
    c = [mm(tri, x, "nn", 1, 3, (False, True)) for x in lw]
    at = [-x * jnp.exp(ci - li) for x, ci, li in zip(kk, c, lw)]
    rt = [x * jnp.exp(ci) for x, ci in zip(r, c)]
    einv = [jnp.exp(-ci) for ci in c]
    bt = [x * e for x, e in zip(b, einv)]
    kt = [x * e for x, e in zip(k, einv)]
    inst = [(s, m) for s in range(n) for m in masks]
    at_h = [jnp.where(m, at[s], 0.0) for s, m in inst]
    rt_h = [jnp.where(m, rt[s], 0.0) for s, m in inst]
    a_ab = [jnp.where(strict, mm(x, bt[s], "nt", hp, hp), 0.0) for x, (s, _) in zip(at_h, inst)]
    a_ak = [jnp.where(strict, mm(x, kt[s], "nt", hp, hp), 0.0) for x, (s, _) in zip(at_h, inst)]
    b_rb = [jnp.where(incl, mm(x, bt[s], "nt", hp, hp), 0.0) for x, (s, _) in zip(rt_h, inst)]
    b_rk = [jnp.where(incl, mm(x, kt[s], "nt", hp, hp), 0.0) for x, (s, _) in zip(rt_h, inst)]
    tinv = [eye + x for x in a_ab]
    pw = a_ab
    for _ in range(int(math.log2(L)) - 1):
        pw = [mm(x, x, "nn", hp, hp) for x in pw]
        tinv = [t + mm(t, x, "nn", hp, hp) for t, x in zip(tinv, pw)]
    rhs = [mm(x, state[s], "nt", hp, hp) + mm(y, v[s], "nn", hp, hp) for x, y, (s, _) in zip(at_h, a_ak, inst)]
    u_h = [mm(t, x, "nn", hp, hp) for t, x in zip(tinv, rhs)]
    y_h = [mm(x, state[s], "nt", hp, hp) + mm(m1, u, "nn", hp, hp) + mm(m2, v[s], "nn", hp, hp)
           for x, m1, m2, u, (s, _) in zip(rt_h, b_rb, b_rk, u_h, inst)]
    u_all = [jnp.where(lo, u_h[2 * s], u_h[2 * s + 1]) for s in range(n)]
    y_all = [jnp.where(lo, y_h[2 * s], y_h[2 * s + 1]) for s in range(n)]
    c_last = [jnp.sum(x, axis=0, keepdims=True) for x in lw]
    efwd = [jnp.exp(cl - ci) for cl, ci in zip(c_last, c)]
    new_state = [st * jnp.exp(cl) + mm(u, bi * e, "tn", hp, hp) + mm(vi, ki * e, "tn", hp, hp)
                 for st, cl, u, bi, e, vi, ki in zip(state, c_last, u_all, b, efwd, v, k)]
    row_head = lax.broadcasted_iota(jnp.int32, (LANES, LANES), 0) // HEAD_DIM
    col_head = lax.broadcasted_iota(jnp.int32, (LANES, LANES), 1) // HEAD_DIM
    new_state = [jnp.where(row_head == col_head, x, 0.0) for x in new_state]
    outs = []
    for y, ri, ki, vi in zip(y_all, r, k, v):
        mean = _segsum(y) * (1.0 / HEAD_DIM)
        d = y - mean
        var = _segsum(d * d) * (1.0 / HEAD_DIM)
        yn = d * lax.rsqrt(var + GN_EPS) * gn_w + gn_b
        outs.append(yn + _segsum(ri * ki * r_k) * vi)
    return outs, new_state


def sb_tile(q, k, v, c_lo, c_hi, diag, from_here=None):
    n = len(q)
    lo = _lane_lo((1, LANES))
    below = _tri(QB, "strict")
    from_s = _tri(QB, "lower_incl").astype(F32)
    inst = [(s, h) for s in range(n) for h in (0, 1)]
    carry = [(c_lo[s], c_hi[s])[h] for s, h in inst]
    z = [mm(q[s][h], k[s], "nt", 1, 1) for s, h in inst]
    soft = [jnp.log(1.0 + jnp.exp(-jnp.abs(x))) for x in z]
    log_keep = [-jnp.maximum(x, 0.0) - sp for x, sp in zip(z, soft)]
    if diag:
        log_keep = [jnp.where(below, x, 0.0) for x in log_keep]
    own = [jnp.sum(x, axis=1, keepdims=True) for x in log_keep]
    if from_here is not None:
        carry = [lax.stop_gradient(from_here[s][h] - o) + cr for (s, h), o, cr in zip(inst, own, carry)]
    tail = [mm(x, from_s, "nn", SB_SUM_PIECES, 1, (True, False)) for x in log_keep]
    log_a = [x + tl + cr for x, tl, cr in zip(z, tail, carry)]
    if diag:
        log_a = [jnp.where(below, x, -1e30) for x in log_a]
    att = [jnp.exp(x) for x in log_a]
    out_h = [mm(x, v[s], "nn", 1, 1) for x, (s, _) in zip(att, inst)]
    out = [jnp.where(lo, out_h[2 * s], out_h[2 * s + 1]) for s in range(n)]
    return out, [own[2 * s] for s in range(n)], [own[2 * s + 1] for s in range(n)]


def sb_split_q(q):
    lo = _lane_lo((1, LANES))
    qs = q * SB_SCALE
    return jnp.where(lo, qs, 0.0), jnp.where(lo, 0.0, qs)


def sb_post(o, gain):
    return o * lax.rsqrt(_segsum(o * o) * (1.0 / HEAD_DIM) + RMS_EPS) * gain


def f_final(x3, g, target):
    y = f_norm(x3, g)
    err = y - target
    return 0.5 * jnp.mean(err * err, axis=-1, keepdims=True)


def _params(sem):
    return pltpu.CompilerParams(dimension_semantics=sem, vmem_limit_bytes=VMEM_LIMIT)


def rw_call(name, body_fn, rows, pars, out_rows, out_accs, tile):
    rows = [item if isinstance(item, tuple) else (item, item.shape[1], 0) for item in rows]
    row_arrays = [arr for arr, _, _ in rows]
    n_rows = row_arrays[0].shape[0]
    tile = min(tile, n_rows)
    steps = n_rows // tile
    row_specs = [pl.BlockSpec((tile, cols), functools.partial(lambda i, c: (i, c), c=cblk)) for _, cols, cblk in rows]
    par_specs = [pl.BlockSpec(p.shape, lambda i: (0, 0)) for p in pars]
    nr, npar, nor, noa = len(row_arrays), len(pars), len(out_rows), len(out_accs)

    def body(*refs):
        row_vals = [refs[i][...] for i in range(nr)]
        par_vals = [refs[nr + i][...] for i in range(npar)]
        o_refs = refs[nr + npar:nr + npar + nor]
        a_refs = refs[nr + npar + nor:]
        row_outs, acc_outs = body_fn(row_vals, par_vals)
        for ref, val in zip(o_refs, row_outs):
            ref[...] = val.astype(ref.dtype)
        if noa:
            first = pl.program_id(0) == 0

            @pl.when(first)
            def _():
                for ref, val in zip(a_refs, acc_outs):
                    ref[...] = val.astype(ref.dtype)

            @pl.when(jnp.logical_not(first))
            def _():
                for ref, val in zip(a_refs, acc_outs):
                    ref[...] = ref[...] + val.astype(ref.dtype)

    out_shape = [jax.ShapeDtypeStruct((n_rows, c), dt) for c, dt in out_rows]
    out_shape += [jax.ShapeDtypeStruct(s, dt) for s, dt in out_accs]
    out_specs = [pl.BlockSpec((tile, c), lambda i: (i, 0)) for c, _ in out_rows]
    out_specs += [pl.BlockSpec(s, lambda i: (0, 0)) for s, _ in out_accs]
    outs = pl.pallas_call(
        body, name=name, grid=(steps,), in_specs=row_specs + par_specs, out_specs=out_specs,
        out_shape=out_shape, compiler_params=_params(("arbitrary",)),
    )(*row_arrays, *pars)
    return outs[:nor], outs[nor:]


def matmul(name, a, b, form, out_dtype, tm, tn, tk):
    tm, tn, tk = min(tm, a.shape[1 if form == "tn" else 0]), min(tn, b.shape[0 if form == "nt" else 1]), min(tk, a.shape[0 if form == "tn" else 1])
    if form == "nn":
        (m, kd), n = a.shape, b.shape[1]
        a_spec = pl.BlockSpec((tm, tk), lambda i, j, k: (i, k))
        b_spec = pl.BlockSpec((tk, tn), lambda i, j, k: (k, j))
    elif form == "nt":
        (m, kd), n = a.shape, b.shape[0]
        a_spec = pl.BlockSpec((tm, tk), lambda i, j, k: (i, k))
        b_spec = pl.BlockSpec((tn, tk), lambda i, j, k: (j, k))
    else:
        (kd, m), n = a.shape, b.shape[1]
        a_spec = pl.BlockSpec((tk, tm), lambda i, j, k: (k, i))
        b_spec = pl.BlockSpec((tk, tn), lambda i, j, k: (k, j))
    ksteps = kd // tk

    def body(a_ref, b_ref, o_ref, acc_ref):
        kstep = pl.program_id(2)
        part = lax.dot_general(a_ref[...].astype(BF16), b_ref[...].astype(BF16), _DIMS[form],
                               preferred_element_type=F32)

        @pl.when(kstep == 0)
        def _():
            acc_ref[...] = part

        @pl.when(kstep > 0)
        def _():
            acc_ref[...] = acc_ref[...] + part

        @pl.when(kstep == ksteps - 1)
        def _():
            o_ref[...] = acc_ref[...].astype(o_ref.dtype)

    return pl.pallas_call(
        body, name=name, grid=(m // tm, n // tn, ksteps), in_specs=[a_spec, b_spec],
        out_specs=pl.BlockSpec((tm, tn), lambda i, j, k: (i, j)),
        out_shape=jax.ShapeDtypeStruct((m, n), out_dtype),
        scratch_shapes=[pltpu.VMEM((tm, tn), F32)],
        compiler_params=_params(("parallel", "parallel", "arbitrary")),
    )(a, b)


RWKV_HP = 1


def rwkv_fwd(r, kraw, v, lw, asig, k_k, k_a, r_k, gn_w, gn_b, n_seq, seq_len):
    n_chunks = seq_len // CHUNK
    n_pairs = RWKV_W // LANES
    row = pl.BlockSpec((n_seq, CHUNK, LANES), lambda h, c: (0, c, h))
    par = pl.BlockSpec((1, LANES), lambda h, c: (0, h))
    s0_spec = pl.BlockSpec((1, 1, n_seq, LANES, LANES), lambda h, c: (h, c, 0, 0, 0))

    def body(r_ref, k_ref, v_ref, lw_ref, a_ref, kk_ref, ka_ref, rk_ref, gw_ref, gb_ref, z_ref, s0_ref, state):
        @pl.when(pl.program_id(1) == 0)
        def _():
            state[...] = jnp.zeros_like(state)

        pars = [ref[...] for ref in (kk_ref, ka_ref, rk_ref, gw_ref, gb_ref)]
        seqs = range(n_seq)
        s0 = [state[s] for s in seqs]
        rows = [[ref[s] for s in seqs] for ref in (r_ref, k_ref, v_ref, lw_ref, a_ref)]
        z, s1 = rwkv_chunk(s0, *rows, *pars, RWKV_HP)
        for s in seqs:
            s0_ref[0, 0, s] = s0[s]
            z_ref[s] = z[s]
            state[s] = s1[s]

    return pl.pallas_call(
        body, name="rwkv_fwd", grid=(n_pairs, n_chunks),
        in_specs=[row] * 5 + [par] * 5, out_specs=[row, s0_spec],
        out_shape=[jax.ShapeDtypeStruct(r.shape, F32),
                   jax.ShapeDtypeStruct((n_pairs, n_chunks, n_seq, LANES, LANES), F32)],
        scratch_shapes=[pltpu.VMEM((n_seq, LANES, LANES), F32)],
        compiler_params=_params(("arbitrary", "arbitrary")),
    )(r, kraw, v, lw, asig, k_k, k_a, r_k, gn_w, gn_b)


def rwkv_bwd(r, kraw, v, lw, asig, k_k, k_a, r_k, gn_w, gn_b, s0_all, dz, n_seq, seq_len):
    n_chunks = seq_len // CHUNK
    n_pairs = RWKV_W // LANES
    row = pl.BlockSpec((n_seq, CHUNK, LANES), lambda h, c: (0, n_chunks - 1 - c, h))
    par = pl.BlockSpec((1, LANES), lambda h, c: (0, h))
    s0_spec = pl.BlockSpec((1, 1, n_seq, LANES, LANES), lambda h, c: (h, n_chunks - 1 - c, 0, 0, 0))

    def body(r_ref, k_ref, v_ref, lw_ref, a_ref, kk_ref, ka_ref, rk_ref, gw_ref, gb_ref, s0_ref, dz_ref,
             dr_ref, dk_ref, dv_ref, dlw_ref, da_ref, dkk_ref, dka_ref, drk_ref, dgw_ref, dgb_ref, dstate):
        first = pl.program_id(1) == 0

        @pl.when(first)
        def _():
            dstate[...] = jnp.zeros_like(dstate)

        pars = [ref[...] for ref in (kk_ref, ka_ref, rk_ref, gw_ref, gb_ref)]
        fn = functools.partial(rwkv_chunk, hp=RWKV_HP)
        seqs = range(n_seq)
        rows = [[ref[s] for s in seqs] for ref in (r_ref, k_ref, v_ref, lw_ref, a_ref)]
        _, vjp = jax.vjp(fn, [s0_ref[0, 0, s] for s in seqs], *rows, *pars)
        grads = vjp(([dz_ref[s] for s in seqs], [dstate[s] for s in seqs]))
        for s in seqs:
            dstate[s] = grads[0][s]
            for ref, val in zip((dr_ref, dk_ref, dv_ref, dlw_ref, da_ref), grads[1:6]):
                ref[s] = val[s]
        accs = tuple(zip((dkk_ref, dka_ref, drk_ref, dgw_ref, dgb_ref), grads[6:]))

        @pl.when(first)
        def _():
            for ref, val in accs:
                ref[...] = val

        @pl.when(jnp.logical_not(first))
        def _():
            for ref, val in accs:
                ref[...] = ref[...] + val

    rows_shape = jax.ShapeDtypeStruct(r.shape, F32)
    par_shape = jax.ShapeDtypeStruct((1, RWKV_W), F32)
    return pl.pallas_call(
        body, name="rwkv_bwd", grid=(n_pairs, n_chunks),
        in_specs=[row] * 5 + [par] * 5 + [s0_spec, row],
        out_specs=[row] * 5 + [par] * 5,
        out_shape=[rows_shape] * 5 + [par_shape] * 5,
        scratch_shapes=[pltpu.VMEM((n_seq, LANES, LANES), F32)],
        compiler_params=_params(("arbitrary", "arbitrary")),
    )(r, kraw, v, lw, asig, k_k, k_a, r_k, gn_w, gn_b, s0_all, dz)


SB_Q0 = RWKV_COLS // LANES
SB_K0 = SB_Q0 + SB_W // LANES
SB_V0 = SB_K0 + SB_W // LANES
SB_SEQS = 2
SB_SUM_PIECES = 2


def _col_of(c_lo, c_hi):
    return jnp.where(_lane_lo((1, LANES)), c_lo, c_hi)


def sb_fwd(p, gain, n_seq, seq_len):
    n_pairs = SB_W // LANES
    n_q = seq_len // QB
    nb = min(SB_SEQS, n_seq)

    def seq_spec(c0):
        return pl.BlockSpec((nb, seq_len, LANES), functools.partial(lambda b, h, c0: (b, 0, c0 + h), c0=c0))

    out_spec = pl.BlockSpec((nb, seq_len, LANES), lambda b, h: (b, 0, h))

    def body(q_ref, k_ref, v_ref, g_ref, y_ref, o_ref, tot_ref):
        gain = g_ref[...]

        def q_block(i, _):
            qs = pl.multiple_of(i * QB, QB)
            seqs = range(nb)
            zeros = [jnp.zeros((QB, 1), F32)] * nb
            qv = [sb_split_q(q_ref[s, pl.ds(qs, QB), :]) for s in seqs]

            def tiles(ks, c_lo, c_hi, diag):
                return sb_tile(qv, [k_ref[s, pl.ds(ks, QB), :] for s in seqs],
                               [v_ref[s, pl.ds(ks, QB), :] for s in seqs], c_lo, c_hi, diag)

            def k_block(jj, carry):
                o, c_lo, c_hi = carry
                o2, s_lo, s_hi = tiles(pl.multiple_of((i - 1 - jj) * QB, QB), c_lo, c_hi, False)
                add = lambda xs, ys: [x + y for x, y in zip(xs, ys)]
                return add(o, o2), add(c_lo, s_lo), add(c_hi, s_hi)

            o, c_lo, c_hi = lax.fori_loop(0, i, k_block, tiles(qs, zeros, zeros, True))
            for s in seqs:
                o_ref[s, pl.ds(qs, QB), :] = o[s]
                tot_ref[s, pl.ds(qs, QB), :] = jnp.broadcast_to(_col_of(c_lo[s], c_hi[s]), (QB, LANES))
                y_ref[s, pl.ds(qs, QB), :] = sb_post(o[s], gain)
            return 0

        lax.fori_loop(0, n_q, q_block, 0)

    shape = jax.ShapeDtypeStruct((n_seq, seq_len, SB_W), F32)
    return pl.pallas_call(
        body, name="sb_fwd", grid=(n_seq // nb, n_pairs),
        in_specs=[seq_spec(SB_Q0), seq_spec(SB_K0), seq_spec(SB_V0), pl.BlockSpec((1, LANES), lambda b, h: (0, h))],
        out_specs=[out_spec] * 3, out_shape=[shape] * 3,
        compiler_params=_params(("arbitrary", "arbitrary")),
    )(p, p, p, gain)


def sb_bwd(p, gain, o_raw, tot, dy, n_seq, seq_len):
    n_pairs = SB_W // LANES
    n_q = seq_len // QB
    nb = min(SB_SEQS, n_seq)

    def seq_spec(c0):
        return pl.BlockSpec((nb, seq_len, LANES), functools.partial(lambda h, b, c0: (b, 0, c0 + h), c0=c0))

    own = pl.BlockSpec((nb, seq_len, LANES), lambda h, b: (b, 0, h))
    par = pl.BlockSpec((1, LANES), lambda h, b: (0, h))

    def body(q_ref, k_ref, v_ref, g_ref, o_ref, tot_ref, dy_ref, dq_ref, dk_ref, dv_ref, dg_ref):
        gain = g_ref[...]
        lo = _lane_lo((1, LANES))
        dk_ref[...] = jnp.zeros_like(dk_ref)
        dv_ref[...] = jnp.zeros_like(dv_ref)

        def q_block(i, dgain):
            qs = pl.multiple_of(i * QB, QB)
            seqs = range(nb)
            zeros = [jnp.zeros((QB, 1), F32)] * nb
            qv, dov, t_lo, t_hi = [], [], [], []
            for s in seqs:
                qv.append(sb_split_q(q_ref[s, pl.ds(qs, QB), :]))
                _, post_vjp = jax.vjp(sb_post, o_ref[s, pl.ds(qs, QB), :], gain)
                do, dg_s = post_vjp(dy_ref[s, pl.ds(qs, QB), :])
                dov.append(do)
                dgain = dgain + dg_s
                tot_s = tot_ref[s, pl.ds(qs, QB), :]
                t_lo.append(jnp.max(jnp.where(lo, tot_s, -jnp.inf), axis=1, keepdims=True))
                t_hi.append(jnp.max(jnp.where(lo, -jnp.inf, tot_s), axis=1, keepdims=True))
            add = lambda xs, ys: [x + y for x, y in zip(xs, ys)]
            sub = lambda xs, ys: [x - y for x, y in zip(xs, ys)]

            def tile(ks, carry, diag):
                dq, rem_lo, rem_hi, g_lo, g_hi = carry
                kv = [k_ref[s, pl.ds(ks, QB), :] for s in seqs]
                vv = [v_ref[s, pl.ds(ks, QB), :] for s in seqs]
                fn = functools.partial(sb_tile, diag=diag, from_here=list(zip(rem_lo, rem_hi)))
                (_, s_lo, s_hi), vjp = jax.vjp(fn, qv, kv, vv, zeros, zeros)
                dq_t, dk_t, dv_t, dc_lo, dc_hi = vjp((dov, g_lo, g_hi))
                dq_t = [jnp.where(lo, d_lo, d_hi) for d_lo, d_hi in dq_t]
                for s in seqs:
                    dk_ref[s, pl.ds(ks, QB), :] = dk_ref[s, pl.ds(ks, QB), :] + dk_t[s]
                    dv_ref[s, pl.ds(ks, QB), :] = dv_ref[s, pl.ds(ks, QB), :] + dv_t[s]
                return add(dq, dq_t), sub(rem_lo, s_lo), sub(rem_hi, s_hi), add(g_lo, dc_lo), add(g_hi, dc_hi)

            def k_block(j, carry):
                return tile(pl.multiple_of(j * QB, QB), carry, False)

            carry = ([jnp.zeros((QB, LANES), F32)] * nb, t_lo, t_hi, zeros, zeros)
            carry = lax.fori_loop(0, i, k_block, carry)
            carry = tile(qs, carry, True)
            for s in seqs:
                dq_ref[s, pl.ds(qs, QB), :] = carry[0][s] * SB_SCALE
            return dgain

        dgain = lax.fori_loop(0, n_q, q_block, jnp.zeros((1, LANES), F32))
        first = pl.program_id(1) == 0

        @pl.when(first)
        def _():
            dg_ref[...] = dgain

        @pl.when(jnp.logical_not(first))
        def _():
            dg_ref[...] = dg_ref[...] + dgain

    shape = jax.ShapeDtypeStruct((n_seq, seq_len, SB_W), F32)
    return pl.pallas_call(
        body, name="sb_bwd", grid=(n_pairs, n_seq // nb),
        in_specs=[seq_spec(SB_Q0), seq_spec(SB_K0), seq_spec(SB_V0), par, own, own, own],
        out_specs=[own, own, own, par],
        out_shape=[shape, shape, shape, jax.ShapeDtypeStruct((1, SB_W), F32)],
        compiler_params=_params(("arbitrary", "arbitrary")),
    )(p, p, p, gain, o_raw, tot, dy)


def exchange(name, arrays, gather):
    n = len(arrays)

    def body(*refs):
        ins, outs = refs[:n], refs[n:2 * n]
        send_sems, recv_sems, local_sems = refs[2 * n:]
        x, y, c = lax.axis_index("x"), lax.axis_index("y"), lax.axis_index("c")
        me = 4 * x + 2 * y + c
        copies = []
        for a in range(n):
            own = ins[a] if gather else ins[a].at[me]
            local = pltpu.make_async_copy(own, outs[a].at[me], local_sems.at[a])
            local.start()
            copies.append(local)
            for j in range(1, N_DEV):
                px, py, pc = (x + (j >> 2)) % 2, (y + ((j >> 1) & 1)) % 2, (c + (j & 1)) % 2
                src = ins[a] if gather else ins[a].at[4 * px + 2 * py + pc]
                remote = pltpu.make_async_remote_copy(
                    src_ref=src, dst_ref=outs[a].at[me], send_sem=send_sems.at[a, j - 1],
                    recv_sem=recv_sems.at[a, j - 1], device_id=(px, py, pc), device_id_type=pl.DeviceIdType.MESH)
                remote.start()
                copies.append(remote)
        for cp in copies:
            cp.wait()

    any_spec = pl.BlockSpec(memory_space=pl.ANY)
    out_shape = [jax.ShapeDtypeStruct(((N_DEV,) + a.shape) if gather else a.shape, a.dtype) for a in arrays]
    return pl.pallas_call(
        body, name=name, in_specs=[any_spec] * n, out_specs=[any_spec] * n, out_shape=out_shape,
        scratch_shapes=[pltpu.SemaphoreType.DMA((n, N_DEV - 1)), pltpu.SemaphoreType.DMA((n, N_DEV - 1)),
                        pltpu.SemaphoreType.DMA((n,))],
    )(*arrays)


def adamw(name, w, parts, m, v, tile):
    rows, cols = w.shape
    spec = pl.BlockSpec((tile, cols), lambda i: (i, 0))
    part_spec = pl.BlockSpec((N_DEV, tile, cols), lambda i: (0, i, 0))

    def body(w_ref, p_ref, m_ref, v_ref, g_ref, d_ref, nm_ref, nv_ref):
        g = p_ref[0].astype(F32)
        for s in range(1, N_DEV):
            g = g + p_ref[s].astype(F32)
        new_m = ADAM_B1 * m_ref[...] + (1.0 - ADAM_B1) * g
        new_v = ADAM_B2 * v_ref[...] + (1.0 - ADAM_B2) * (g * g)
        m_hat = new_m / (1.0 - ADAM_B1 ** ADAM_STEP)
        v_hat = new_v / (1.0 - ADAM_B2 ** ADAM_STEP)
        g_ref[...] = g
        d_ref[...] = -ADAM_LR * (m_hat / (jnp.sqrt(v_hat) + ADAM_EPS) + ADAM_WD * w_ref[...])
        nm_ref[...] = new_m
        nv_ref[...] = new_v

    shape = jax.ShapeDtypeStruct((rows, cols), F32)
    return pl.pallas_call(
        body, name=name, grid=(rows // tile,), in_specs=[spec, part_spec, spec, spec],
        out_specs=[spec] * 4, out_shape=[shape] * 4, compiler_params=_params(("arbitrary",)),
    )(w, parts, m, v)


SMALL = ("ln1_g", "tok_mu", "w0", "a0", "k_k", "k_a", "r_k", "gn_w", "gn_b", "sb_gain", "ln2_g", "lnf_g")
BIG = ("w_in", "w_decay_up", "w_aaa_up", "w_gate_up", "w_out", "w_up", "w_down")
COL_SHARDED = ("w_in", "w_decay_up", "w_aaa_up", "w_gate_up", "w_up")
ORDER = ("ln1_g", "w_in", "tok_mu", "w0", "w_decay_up", "a0", "w_aaa_up", "w_gate_up", "k_k", "k_a", "r_k",
         "gn_w", "gn_b", "sb_gain", "w_out", "ln2_g", "w_up", "w_down", "lnf_g")


def _pack(vectors, rows):
    flat = jnp.concatenate([v.reshape(-1).astype(F32) for v in vectors])
    return jnp.pad(flat, (0, rows * LANES - flat.shape[0])).reshape(rows, LANES)


def _full_cols(gathered):
    d, k, cols = gathered.shape
    return jnp.transpose(gathered, (1, 0, 2)).reshape(k, d * cols)


def _col_parts(full):
    k, n = full.shape
    return jnp.transpose(full.reshape(k, N_DEV, n // N_DEV), (1, 0, 2))


def kernel(x, ln1_g, w_in, tok_mu, w0, w_decay_up, a0, w_aaa_up, w_gate_up, k_k, k_a, r_k, gn_w, gn_b, sb_gain, w_out, ln2_g, w_up, w_down, lnf_g, loss_target, m_ln1_g, m_w_in, m_tok_mu, m_w0, m_w_decay_up, m_a0, m_w_aaa_up, m_w_gate_up, m_k_k, m_k_a, m_r_k, m_gn_w, m_gn_b, m_sb_gain, m_w_out, m_ln2_g, m_w_up, m_w_down, m_lnf_g, v_ln1_g, v_w_in, v_tok_mu, v_w0, v_w_decay_up, v_a0, v_w_aaa_up, v_w_gate_up, v_k_k, v_k_a, v_r_k, v_gn_w, v_gn_b, v_sb_gain, v_w_out, v_ln2_g, v_w_up, v_w_down, v_lnf_g):
    args = dict(locals())
    weights = {n: args[n] for n in ORDER}
    mom_m = {n: args["m_" + n] for n in ORDER}
    mom_v = {n: args["v_" + n] for n in ORDER}

    n_seq, seq_len, d_model = x.shape
    n_tok = n_seq * seq_len
    x2d = x.reshape(n_tok, d_model)
    tgt = loss_target.reshape(n_tok, d_model)
    row = lambda t: t.reshape(1, -1).astype(F32)

    shards = [weights[n][0].astype(BF16) for n in BIG]
    gathered = dict(zip(BIG, exchange("gather_weights", shards, True)))
    w_in_f = _full_cols(gathered["w_in"])
    zeros64 = jnp.zeros((HEAD_DIM, RWKV_W), BF16)
    wd_pad = jnp.concatenate([_full_cols(gathered["w_decay_up"]), zeros64], axis=0)
    wa_pad = jnp.concatenate([zeros64, _full_cols(gathered["w_aaa_up"])], axis=0)
    wg_f = _full_cols(gathered["w_gate_up"])
    w_out_f = gathered["w_out"].reshape(d_model, d_model)
    w_up_f = _full_cols(gathered["w_up"])
    w_down_f = gathered["w_down"].reshape(-1, d_model)
    d_ff = w_up_f.shape[1]
    in_cols = w_in_f.shape[1]

    g1, mu, w0r, a0r = row(ln1_g), row(tok_mu), row(w0), row(a0)
    kkr, kar, rkr, gwr, gbr, sgr = row(k_k), row(k_a), row(r_k), row(gn_w), row(gn_b), row(sb_gain)
    g2, gf = row(ln2_g), row(lnf_g)

    (h1,), _ = rw_call("norm1", lambda r, p: ([f_norm(r[0], p[0])], []), [x2d], [g1], [(d_model, BF16)], [], 512)
    p = matmul("proj_in", h1, w_in_f, "nn", F32, 512, in_cols // 2, d_model)
    p_rwkv = p[:, :RWKV_COLS].reshape(n_seq, seq_len, RWKV_COLS)
    p_prev = jnp.pad(p_rwkv[:, :-1], ((0, 0), (1, 0), (0, 0))).reshape(n_tok, RWKV_COLS)
    prep_pars = [mu, w0r, wd_pad, a0r, wa_pad, wg_f]
    (r_, kraw, v_, lw, asig, gate), _ = rw_call(
        "rwkv_prep", lambda r, q: (list(f_prep(r[0], r[1], *q)), []), [(p, RWKV_COLS, 0), p_prev], prep_pars,
        [(RWKV_W, F32)] * 6, [], 256)
    by_seq = lambda t: t.reshape(n_seq, seq_len, t.shape[-1])
    flat = lambda t: t.reshape(n_tok, t.shape[-1])
    rwkv_in = [by_seq(t) for t in (r_, kraw, v_, lw, asig)]
    z, s0_all = rwkv_fwd(*rwkv_in, kkr, kar, rkr, gwr, gbr, n_seq, seq_len)
    z = flat(z)
    y_sb, o_raw, tot = sb_fwd(by_seq(p), sgr, n_seq, seq_len)
    y_sb = flat(y_sb)
    (ycat,), _ = rw_call("mix_cat", lambda r, q: ([jnp.concatenate([r[0] * r[1], r[2]], axis=1)], []),
                         [z, gate, y_sb], [], [(d_model, BF16)], [], 512)
    mix = matmul("proj_out", ycat, w_out_f, "nn", F32, 512, d_model, d_model)
    (x2, h2), _ = rw_call("resid_norm2", lambda r, q: ([r[0] + r[1], f_norm(r[0] + r[1], q[0])], []),
                          [x2d, mix], [g2], [(d_model, F32), (d_model, BF16)], [], 512)
    u = matmul("mlp_up", h2, w_up_f, "nn", F32, 512, d_ff // 4, d_model)
    (act,), _ = rw_call("relu2", lambda r, q: ([jnp.square(jnp.maximum(r[0], 0.0))], []), [u], [],
                        [(d_ff, BF16)], [], 256)
    mlp = matmul("mlp_down", act, w_down_f, "nn", F32, 512, d_model, d_ff // 2)

    def final_body(r, q):
        x3 = r[0] + r[1]
        loss_rows, vjp = jax.vjp(lambda a, g: f_final(a, g, r[2]), x3, q[0])
        dx3, dgf = vjp(jnp.ones_like(loss_rows))
        return [dx3], [jnp.broadcast_to(jnp.sum(loss_rows), (1, LANES)), dgf]

    (dx3,), (loss_acc, d_lnf) = rw_call("loss_head", final_body, [x2, mlp, tgt], [gf], [(d_model, F32)],
                                        [((1, LANES), F32), ((1, d_model), F32)], 256)

    dact = matmul("d_act", dx3, w_down_f, "nt", F32, 512, d_ff // 4, d_model)
    dw_down = matmul("dw_down", act, dx3, "tn", BF16, 512, d_model, 512)
    (du,), _ = rw_call("d_relu2", lambda r, q: ([r[0] * (2.0 * jnp.maximum(r[1], 0.0))], []), [dact, u], [],
                       [(d_ff, BF16)], [], 256)
    dh2 = matmul("d_h2", du, w_up_f, "nt", F32, 512, d_model, d_ff // 2)
    dw_up = matmul("dw_up", h2, du, "tn", BF16, d_model, 512, 512)

    def norm_bwd(r, q):
        xv, dh, dres = r
        _, vjp = jax.vjp(f_norm, xv, q[0])
        dx, dg = vjp(dh)
        return [dx + dres], [dg]

    (dx2,), (d_ln2,) = rw_call("d_norm2", norm_bwd, [x2, dh2, dx3], [g2], [(d_model, F32)],
                               [((1, d_model), F32)], 256)

    dycat = matmul("d_ycat", dx2, w_out_f, "nt", F32, 512, d_model, d_model)
    dw_out = matmul("dw_out", ycat, dx2, "tn", BF16, d_model, d_model, 512)
    (dz, dgate), _ = rw_call("d_mix", lambda r, q: ([r[0] * r[2], r[0] * r[1]], []),
                             [(dycat, RWKV_W, 0), z, gate], [], [(RWKV_W, F32)] * 2, [], 512)
    dq, dk_sb, dv_sb, d_sg = sb_bwd(by_seq(p), sgr, o_raw, tot, by_seq(dycat[:, RWKV_W:]), n_seq, seq_len)
    dq, dk_sb, dv_sb = flat(dq), flat(dk_sb), flat(dv_sb)
    dr, dkraw, dv, dlw, dasig, d_kk, d_ka, d_rk, d_gw, d_gb = rwkv_bwd(
        *rwkv_in, kkr, kar, rkr, gwr, gbr, s0_all, by_seq(dz), n_seq, seq_len)
    dr, dkraw, dv, dlw, dasig = [flat(t) for t in (dr, dkraw, dv, dlw, dasig)]

    def prep_bwd(r, q):
        _, vjp = jax.vjp(f_prep, r[0], r[1], *[t.astype(F32) for t in q])
        grads = vjp(tuple(r[2:8]))
        return list(grads[:2]), list(grads[2:])

    par_shapes = [(t.shape, F32) for t in prep_pars]
    (dp_rwkv, dp_prev), (d_mu, d_w0, d_wd, d_a0, d_wa, d_wg) = rw_call(
        "d_rwkv_prep", prep_bwd, [(p, RWKV_COLS, 0), p_prev, dr, dkraw, dv, dlw, dasig, dgate], prep_pars,
        [(RWKV_COLS, F32)] * 2, par_shapes, 128)
    dp_shift = jnp.pad(dp_prev.reshape(n_seq, seq_len, RWKV_COLS)[:, 1:], ((0, 0), (0, 1), (0, 0)))
    dp = jnp.concatenate([dp_rwkv + dp_shift.reshape(n_tok, RWKV_COLS), dq, dk_sb, dv_sb], axis=1)

    dh1 = matmul("d_h1", dp, w_in_f, "nt", F32, 512, d_model, in_cols // 2)
    dw_in = matmul("dw_in", h1, dp, "tn", BF16, d_model, in_cols // 2, 512)
    (dx,), (d_ln1,) = rw_call("d_norm1", norm_bwd, [x2d, dh1, dx2], [g1], [(d_model, F32)],
                              [((1, d_model), F32)], 256)

    lora_parts = lambda t: _col_parts(t).astype(BF16)
    big_grads = {
        "w_in": _col_parts(dw_in), "w_decay_up": lora_parts(d_wd[:HEAD_DIM]), "w_aaa_up": lora_parts(d_wa[HEAD_DIM:]),
        "w_gate_up": lora_parts(d_wg), "w_out": dw_out.reshape(N_DEV, -1, d_model), "w_up": _col_parts(dw_up),
        "w_down": dw_down.reshape(N_DEV, -1, d_model),
    }
    parts = dict(zip(BIG, exchange("scatter_grads", [big_grads[n] for n in BIG], False)))

    small_grads = {"ln1_g": d_ln1, "tok_mu": d_mu, "w0": d_w0, "a0": d_a0, "k_k": d_kk, "k_a": d_ka, "r_k": d_rk,
                   "gn_w": d_gw, "gn_b": d_gb, "sb_gain": d_sg, "ln2_g": d_ln2, "lnf_g": d_lnf}
    n_small = sum(int(weights[n].size) for n in SMALL)
    pack_rows = -(-(n_small + 1) // (8 * LANES)) * 8
    packed = _pack([small_grads[n] for n in SMALL] + [loss_acc[0, :1]], pack_rows)
    (small_parts,) = exchange("gather_small", [packed], True)

    results = {}
    for n in BIG:
        w2d = weights[n][0]
        tile = w2d.shape[0] if w2d.shape[0] <= 256 else 256
        results[n] = adamw("adamw_" + n, w2d, parts[n], mom_m[n][0], mom_v[n][0], tile)
    pk = lambda d: _pack([d[n] for n in SMALL] + [jnp.zeros((1,), F32)], pack_rows)
    sg, sd, sm, sv = adamw("adamw_small", pk(weights), small_parts, pk(mom_m), pk(mom_v), pack_rows)
    off = 0
    for n in SMALL:
        size = int(weights[n].size)
        results[n] = tuple(t.reshape(-1)[off:off + size] for t in (sg, sd, sm, sv))
        off += size
    loss = sg.reshape(-1)[off]

    out = [loss, dx.reshape(x.shape)]
    for kind in range(4):
        out += [results[n][kind].reshape(weights[n].shape) for n in ORDER]
    return tuple(out)
```

```python
import functools
import math

import jax
import jax.numpy as jnp
from jax import lax
from jax.experimental import pallas as pl
from jax.experimental.pallas import tpu as pltpu

F32 = jnp.float32
BF16 = jnp.bfloat16

N_DEV = 8
HEAD_DIM = 64
LANES = 128
RWKV_W = 512
SB_W = 512
LORA_WA = 128
GATE_LORA = 128
RWKV_COLS = 3 * RWKV_W + LORA_WA + GATE_LORA
RMS_EPS = 1e-5
GN_EPS = 64e-5
CHUNK = 64
QB = 128
SB_SCALE = HEAD_DIM ** -0.5
ADAM_LR, ADAM_B1, ADAM_B2, ADAM_EPS, ADAM_WD, ADAM_STEP = 0.001, 0.9, 0.999, 1e-08, 0.01, 10
VMEM_LIMIT = 56 * 1024 * 1024


_DIMS = {
    "nn": (((1,), (0,)), ((), ())),
    "nt": (((1,), (1,)), ((), ())),
    "tn": (((0,), (0,)), ((), ())),
}


def _pieces(x, n):
    if n == 1:
        return [x.astype(BF16)]
    out, rem = [], x.astype(F32)
    for i in range(n):
        p = rem.astype(BF16)
        out.append(p)
        if i + 1 < n:
            rem = rem - p.astype(F32)
    return out


def _dot(a, b, form, pa, pb):
    pieces_a, pieces_b = _pieces(a, pa), _pieces(b, pb)
    keep = max(pa, pb)
    acc = None
    for i, ai in enumerate(pieces_a):
        for j, bj in enumerate(pieces_b):
            if i + j >= keep:
                continue
            t = lax.dot_general(ai, bj, _DIMS[form], preferred_element_type=F32)
            acc = t if acc is None else acc + t
    return acc


BOTH = (True, True)


@functools.partial(jax.custom_vjp, nondiff_argnums=(2, 3, 4, 5))
def mm(a, b, form, pa, pb, diff=BOTH):
    return _dot(a, b, form, pa, pb)


def _mm_fwd(a, b, form, pa, pb, diff):
    return _dot(a, b, form, pa, pb), (a, b)


def _mm_bwd(form, pa, pb, diff, res, g):
    a, b = res
    pg = max(pa, pb)
    da, db = jnp.zeros_like(a), jnp.zeros_like(b)
    if form == "nn":
        if diff[0]:
            da = mm(g, b, "nt", pg, pb)
        if diff[1]:
            db = mm(a, g, "tn", pa, pg)
    elif form == "nt":
        if diff[0]:
            da = mm(g, b, "nn", pg, pb)
        if diff[1]:
            db = mm(g, a, "tn", pg, pa)
    else:
        if diff[0]:
            da = mm(b, g, "nt", pb, pg)
        if diff[1]:
            db = mm(a, g, "nn", pa, pg)
    return da, db


mm.defvjp(_mm_fwd, _mm_bwd)


def _lane_lo(shape):
    return lax.broadcasted_iota(jnp.int32, shape, len(shape) - 1) < HEAD_DIM


def _segsum(x):
    lo = _lane_lo(x.shape)
    s_lo = jnp.sum(jnp.where(lo, x, 0.0), axis=-1, keepdims=True)
    s_hi = jnp.sum(jnp.where(lo, 0.0, x), axis=-1, keepdims=True)
    return jnp.where(lo, s_lo, s_hi)


def _sigmoid(x):
    return 0.5 * (jnp.tanh(0.5 * x) + 1.0)


def _softplus(x):
    return jnp.maximum(x, 0.0) + jnp.log(1.0 + jnp.exp(-jnp.abs(x)))


def f_norm(x, g):
    return x * lax.rsqrt(jnp.mean(x * x, axis=-1, keepdims=True) + RMS_EPS) * g


def f_prep(p, pprev, mu, w0, wd_pad, a0, wa_pad, wg):
    pm = p + mu * (pprev - p)
    r = pm[:, 0:RWKV_W]
    k = pm[:, RWKV_W:2 * RWKV_W]
    v = pm[:, 2 * RWKV_W:3 * RWKV_W]
    xwa = pm[:, 3 * RWKV_W:3 * RWKV_W + LORA_WA]
    xg = pm[:, 3 * RWKV_W + LORA_WA:RWKV_COLS]
    w = -_softplus(-(w0 + mm(jnp.tanh(xwa), wd_pad, "nn", 1, 1))) - 0.5
    lw = -jnp.exp(w)
    asig = _sigmoid(a0 + mm(xwa, wa_pad, "nn", 1, 1))
    gate = mm(_sigmoid(xg), wg, "nn", 1, 1)
    return r, k, v, lw, asig, gate


def _tri(n, kind):
    row = lax.broadcasted_iota(jnp.int32, (n, n), 0)
    col = lax.broadcasted_iota(jnp.int32, (n, n), 1)
    if kind == "lower_incl":
        return row >= col
    return row > col


def rwkv_chunk(state, r, kraw, v, lw, asig, k_k, k_a, r_k, gn_w, gn_b, hp):
    n = len(r)
    L = r[0].shape[0]
    lo = _lane_lo((1, LANES))
    masks = (lo, jnp.logical_not(lo))
    incl = _tri(L, "lower_incl")
    strict = _tri(L, "strict")
    tri = incl.astype(F32)
    eye = (lax.broadcasted_iota(jnp.int32, (L, L), 0) == lax.broadcasted_iota(jnp.int32, (L, L), 1)).astype(F32)
    kk = [x * k_k for x in kraw]
    kk = [x / jnp.maximum(jnp.sqrt(_segsum(x * x)), 1e-12) for x in kk]
    k = [x * (1.0 + (s - 1.0) * k_a) for x, s in zip(kraw, asig)]
    b = [x * s for x, s in zip(kk, asig)]
    c = [mm(tri, x, "nn", 1, 3, (False, True)) for x in lw]
    at = [-x * jnp.exp(ci - li) for x, ci, li in zip(kk, c, lw)]
    rt = [x * jnp.exp(ci) for x, ci in zip(r, c)]
    einv = [jnp.exp(-ci) for ci in c]
    bt = [x * e for x, e in zip(b, einv)]
    kt = [x * e for x, e in zip(k, einv)]
    inst = [(s, m) for s in range(n) for m in masks]
    at_h = [jnp.where(m, at[s], 0.0) for s, m in inst]
    rt_h = [jnp.where(m, rt[s], 0.0) for s, m in inst]
    a_ab = [jnp.where(strict, mm(x, bt[s], "nt", hp, hp), 0.0) for x, (s, _) in zip(at_h, inst)]
    a_ak = [jnp.where(strict, mm(x, kt[s], "nt", hp, hp), 0.0) for x, (s, _) in zip(at_h, inst)]
    b_rb = [jnp.where(incl, mm(x, bt[s], "nt", hp, hp), 0.0) for x, (s, _) in zip(rt_h, inst)]
    b_rk = [jnp.where(incl, mm(x, kt[s], "nt", hp, hp), 0.0) for x, (s, _) in zip(rt_h, inst)]
    tinv = [eye + x for x in a_ab]
    pw = a_ab
    for _ in range(int(math.log2(L)) - 1):
        pw = [mm(x, x, "nn", hp, hp) for x in pw]
        tinv = [t + mm(t, x, "nn", hp, hp) for t, x in zip(tinv, pw)]
    rhs = [mm(x, state[s], "nt", hp, hp) + mm(y, v[s], "nn", hp, hp) for x, y, (s, _) in zip(at_h, a_ak, inst)]
    u_h = [mm(t, x, "nn", hp, hp) for t, x in zip(tinv, rhs)]
    y_h = [mm(x, state[s], "nt", hp, hp) + mm(m1, u, "nn", hp, hp) + mm(m2, v[s], "nn", hp, hp)
           for x, m1, m2, u, (s, _) in zip(rt_h, b_rb, b_rk, u_h, inst)]
    u_all = [jnp.where(lo, u_h[2 * s], u_h[2 * s + 1]) for s in range(n)]
    y_all = [jnp.where(lo, y_h[2 * s], y_h[2 * s + 1]) for s in range(n)]
    c_last = [jnp.sum(x, axis=0, keepdims=True) for x in lw]
    efwd = [jnp.exp(cl - ci) for cl, ci in zip(c_last, c)]
    new_state = [st * jnp.exp(cl) + mm(u, bi * e, "tn", hp, hp) + mm(vi, ki * e, "tn", hp, hp)
                 for st, cl, u, bi, e, vi, ki in zip(state, c_last, u_all, b, efwd, v, k)]
    row_head = lax.broadcasted_iota(jnp.int32, (LANES, LANES), 0) // HEAD_DIM
    col_head = lax.broadcasted_iota(jnp.int32, (LANES, LANES), 1) // HEAD_DIM
    new_state = [jnp.where(row_head == col_head, x, 0.0) for x in new_state]
    outs = []
    for y, ri, ki, vi in zip(y_all, r, k, v):
        mean = _segsum(y) * (1.0 / HEAD_DIM)
        d = y - mean
        var = _segsum(d * d) * (1.0 / HEAD_DIM)
        yn = d * lax.rsqrt(var + GN_EPS) * gn_w + gn_b
        outs.append(yn + _segsum(ri * ki * r_k) * vi)
    return outs, new_state


def sb_tile(q, k, v, c_lo, c_hi, diag, from_here=None):
    n = len(q)
    lo = _lane_lo((1, LANES))
    below = _tri(QB, "strict")
    from_s = _tri(QB, "lower_incl").astype(F32)
    inst = [(s, h) for s in range(n) for h in (0, 1)]
    carry = [(c_lo[s], c_hi[s])[h] for s, h in inst]
    z = [mm(q[s][h], k[s], "nt", 1, 1) for s, h in inst]
    soft = [jnp.log(1.0 + jnp.exp(-jnp.abs(x))) for x in z]
    log_keep = [-jnp.maximum(x, 0.0) - sp for x, sp in zip(z, soft)]
    if diag:
        log_keep = [jnp.where(below, x, 0.0) for x in log_keep]
    own = [jnp.sum(x, axis=1, keepdims=True) for x in log_keep]
    if from_here is not None:
        carry = [lax.stop_gradient(from_here[s][h] - o) + cr for (s, h), o, cr in zip(inst, own, carry)]
    tail = [mm(x, from_s, "nn", SB_SUM_PIECES, 1, (True, False)) for x in log_keep]
    log_a = [x + tl + cr for x, tl, cr in zip(z, tail, carry)]
    if diag:
        log_a = [jnp.where(below, x, -1e30) for x in log_a]
    att = [jnp.exp(x) for x in log_a]
    out_h = [mm(x, v[s], "nn", 1, 1) for x, (s, _) in zip(att, inst)]
    out = [jnp.where(lo, out_h[2 * s], out_h[2 * s + 1]) for s in range(n)]
    return out, [own[2 * s] for s in range(n)], [own[2 * s + 1] for s in range(n)]


def sb_split_q(q):
    lo = _lane_lo((1, LANES))
    qs = q * SB_SCALE
    return jnp.where(lo, qs, 0.0), jnp.where(lo, 0.0, qs)


def sb_post(o, gain):
    return o * lax.rsqrt(_segsum(o * o) * (1.0 / HEAD_DIM) + RMS_EPS) * gain


def f_final(x3, g, target):
    y = f_norm(x3, g)
    err = y - target
    return 0.5 * jnp.mean(err * err, axis=-1, keepdims=True)


def _params(sem):
    return pltpu.CompilerParams(dimension_semantics=sem, vmem_limit_bytes=VMEM_LIMIT)


def rw_call(name, body_fn, rows, pars, out_rows, out_accs, tile):
    rows = [item if isinstance(item, tuple) else (item, item.shape[1], 0) for item in rows]
    row_arrays = [arr for arr, _, _ in rows]
    n_rows = row_arrays[0].shape[0]
    tile = min(tile, n_rows)
    steps = n_rows // tile
    row_specs = [pl.BlockSpec((tile, cols), functools.partial(lambda i, c: (i, c), c=cblk)) for _, cols, cblk in rows]
    par_specs = [pl.BlockSpec(p.shape, lambda i: (0, 0)) for p in pars]
    nr, npar, nor, noa = len(row_arrays), len(pars), len(out_rows), len(out_accs)

    def body(*refs):
        row_vals = [refs[i][...] for i in range(nr)]
        par_vals = [refs[nr + i][...] for i in range(npar)]
        o_refs = refs[nr + npar:nr + npar + nor]
        a_refs = refs[nr + npar + nor:]
        row_outs, acc_outs = body_fn(row_vals, par_vals)
        for ref, val in zip(o_refs, row_outs):
            ref[...] = val.astype(ref.dtype)
        if noa:
            first = pl.program_id(0) == 0

            @pl.when(first)
            def _():
                for ref, val in zip(a_refs, acc_outs):
                    ref[...] = val.astype(ref.dtype)

            @pl.when(jnp.logical_not(first))
            def _():
                for ref, val in zip(a_refs, acc_outs):
                    ref[...] = ref[...] + val.astype(ref.dtype)

    out_shape = [jax.ShapeDtypeStruct((n_rows, c), dt) for c, dt in out_rows]
    out_shape += [jax.ShapeDtypeStruct(s, dt) for s, dt in out_accs]
    out_specs = [pl.BlockSpec((tile, c), lambda i: (i, 0)) for c, _ in out_rows]
    out_specs += [pl.BlockSpec(s, lambda i: (0, 0)) for s, _ in out_accs]
    outs = pl.pallas_call(
        body, name=name, grid=(steps,), in_specs=row_specs + par_specs, out_specs=out_specs,
        out_shape=out_shape, compiler_params=_params(("arbitrary",)),
    )(*row_arrays, *pars)
    return outs[:nor], outs[nor:]


def matmul(name, a, b, form, out_dtype, tm, tn, tk):
    tm, tn, tk = min(tm, a.shape[1 if form == "tn" else 0]), min(tn, b.shape[0 if form == "nt" else 1]), min(tk, a.shape[0 if form == "tn" else 1])
    if form == "nn":
        (m, kd), n = a.shape, b.shape[1]
        a_spec = pl.BlockSpec((tm, tk), lambda i, j, k: (i, k))
        b_spec = pl.BlockSpec((tk, tn), lambda i, j, k: (k, j))
    elif form == "nt":
        (m, kd), n = a.shape, b.shape[0]
        a_spec = pl.BlockSpec((tm, tk), lambda i, j, k: (i, k))
        b_spec = pl.BlockSpec((tn, tk), lambda i, j, k: (j, k))
    else:
        (kd, m), n = a.shape, b.shape[1]
        a_spec = pl.BlockSpec((tk, tm), lambda i, j, k: (k, i))
        b_spec = pl.BlockSpec((tk, tn), lambda i, j, k: (k, j))
    ksteps = kd // tk

    def body(a_ref, b_ref, o_ref, acc_ref):
        kstep = pl.program_id(2)
        part = lax.dot_general(a_ref[...].astype(BF16), b_ref[...].astype(BF16), _DIMS[form],
                               preferred_element_type=F32)

        @pl.when(kstep == 0)
        def _():
            acc_ref[...] = part

        @pl.when(kstep > 0)
        def _():
            acc_ref[...] = acc_ref[...] + part

        @pl.when(kstep == ksteps - 1)
        def _():
            o_ref[...] = acc_ref[...].astype(o_ref.dtype)

    return pl.pallas_call(
        body, name=name, grid=(m // tm, n // tn, ksteps), in_specs=[a_spec, b_spec],
        out_specs=pl.BlockSpec((tm, tn), lambda i, j, k: (i, j)),
        out_shape=jax.ShapeDtypeStruct((m, n), out_dtype),
        scratch_shapes=[pltpu.VMEM((tm, tn), F32)],
        compiler_params=_params(("parallel", "parallel", "arbitrary")),
    )(a, b)


RWKV_HP = 1


def rwkv_fwd(r, kraw, v, lw, asig, k_k, k_a, r_k, gn_w, gn_b, n_seq, seq_len):
    n_chunks = seq_len // CHUNK
    n_pairs = RWKV_W // LANES
    row = pl.BlockSpec((n_seq, CHUNK, LANES), lambda h, c: (0, c, h))
    par = pl.BlockSpec((1, LANES), lambda h, c: (0, h))
    s0_spec = pl.BlockSpec((1, 1, n_seq, LANES, LANES), lambda h, c: (h, c, 0, 0, 0))

    def body(r_ref, k_ref, v_ref, lw_ref, a_ref, kk_ref, ka_ref, rk_ref, gw_ref, gb_ref, z_ref, s0_ref, state):
        @pl.when(pl.program_id(1) == 0)
        def _():
            state[...] = jnp.zeros_like(state)

        pars = [ref[...] for ref in (kk_ref, ka_ref, rk_ref, gw_ref, gb_ref)]
        seqs = range(n_seq)
        s0 = [state[s] for s in seqs]
        rows = [[ref[s] for s in seqs] for ref in (r_ref, k_ref, v_ref, lw_ref, a_ref)]
        z, s1 = rwkv_chunk(s0, *rows, *pars, RWKV_HP)
        for s in seqs:
            s0_ref[0, 0, s] = s0[s]
            z_ref[s] = z[s]
            state[s] = s1[s]

    return pl.pallas_call(
        body, name="rwkv_fwd", grid=(n_pairs, n_chunks),
        in_specs=[row] * 5 + [par] * 5, out_specs=[row, s0_spec],
        out_shape=[jax.ShapeDtypeStruct(r.shape, F32),
                   jax.ShapeDtypeStruct((n_pairs, n_chunks, n_seq, LANES, LANES), F32)],
        scratch_shapes=[pltpu.VMEM((n_seq, LANES, LANES), F32)],
        compiler_params=_params(("arbitrary", "arbitrary")),
    )(r, kraw, v, lw, asig, k_k, k_a, r_k, gn_w, gn_b)


def rwkv_bwd(r, kraw, v, lw, asig, k_k, k_a, r_k, gn_w, gn_b, s0_all, dz, n_seq, seq_len):
    n_chunks = seq_len // CHUNK
    n_pairs = RWKV_W // LANES
    row = pl.BlockSpec((n_seq, CHUNK, LANES), lambda h, c: (0, n_chunks - 1 - c, h))
    par = pl.BlockSpec((1, LANES), lambda h, c: (0, h))
    s0_spec = pl.BlockSpec((1, 1, n_seq, LANES, LANES), lambda h, c: (h, n_chunks - 1 - c, 0, 0, 0))

    def body(r_ref, k_ref, v_ref, lw_ref, a_ref, kk_ref, ka_ref, rk_ref, gw_ref, gb_ref, s0_ref, dz_ref,
             dr_ref, dk_ref, dv_ref, dlw_ref, da_ref, dkk_ref, dka_ref, drk_ref, dgw_ref, dgb_ref, dstate):
        first = pl.program_id(1) == 0

        @pl.when(first)
        def _():
            dstate[...] = jnp.zeros_like(dstate)

        pars = [ref[...] for ref in (kk_ref, ka_ref, rk_ref, gw_ref, gb_ref)]
        fn = functools.partial(rwkv_chunk, hp=RWKV_HP)
        seqs = range(n_seq)
        rows = [[ref[s] for s in seqs] for ref in (r_ref, k_ref, v_ref, lw_ref, a_ref)]
        _, vjp = jax.vjp(fn, [s0_ref[0, 0, s] for s in seqs], *rows, *pars)
        grads = vjp(([dz_ref[s] for s in seqs], [dstate[s] for s in seqs]))
        for s in seqs:
            dstate[s] = grads[0][s]
            for ref, val in zip((dr_ref, dk_ref, dv_ref, dlw_ref, da_ref), grads[1:6]):
                ref[s] = val[s]
        accs = tuple(zip((dkk_ref, dka_ref, drk_ref, dgw_ref, dgb_ref), grads[6:]))

        @pl.when(first)
        def _():
            for ref, val in accs:
                ref[...] = val

        @pl.when(jnp.logical_not(first))
        def _():
            for ref, val in accs:
                ref[...] = ref[...] + val

    rows_shape = jax.ShapeDtypeStruct(r.shape, F32)
    par_shape = jax.ShapeDtypeStruct((1, RWKV_W), F32)
    return pl.pallas_call(
        body, name="rwkv_bwd", grid=(n_pairs, n_chunks),
        in_specs=[row] * 5 + [par] * 5 + [s0_spec, row],
        out_specs=[row] * 5 + [par] * 5,
        out_shape=[rows_shape] * 5 + [par_shape] * 5,
        scratch_shapes=[pltpu.VMEM((n_seq, LANES, LANES), F32)],
        compiler_params=_params(("arbitrary", "arbitrary")),
    )(r, kraw, v, lw, asig, k_k, k_a, r_k, gn_w, gn_b, s0_all, dz)


SB_Q0 = RWKV_COLS // LANES
SB_K0 = SB_Q0 + SB_W // LANES
SB_V0 = SB_K0 + SB_W // LANES
SB_SEQS = 2
SB_SUM_PIECES = 2
SB_DEAD = -110.0


def _col_of(c_lo, c_hi):
    return jnp.where(_lane_lo((1, LANES)), c_lo, c_hi)


def sb_fwd(p, gain, n_seq, seq_len):
    n_pairs = SB_W // LANES
    n_q = seq_len // QB
    nb = min(SB_SEQS, n_seq)

    def seq_spec(c0):
        return pl.BlockSpec((nb, seq_len, LANES), functools.partial(lambda b, h, c0: (b, 0, c0 + h), c0=c0))

    out_spec = pl.BlockSpec((nb, seq_len, LANES), lambda b, h: (b, 0, h))

    def body(q_ref, k_ref, v_ref, g_ref, y_ref, o_ref, tot_ref, first_ref):
        gain = g_ref[...]

        def q_block(i, _):
            qs = pl.multiple_of(i * QB, QB)
            seqs = range(nb)
            zeros = [jnp.zeros((QB, 1), F32)] * nb
            qv = [sb_split_q(q_ref[s, pl.ds(qs, QB), :]) for s in seqs]
            add = lambda xs, ys: [x + y for x, y in zip(xs, ys)]

            def tiles(ks, c_lo, c_hi, diag):
                return sb_tile(qv, [k_ref[s, pl.ds(ks, QB), :] for s in seqs],
                               [v_ref[s, pl.ds(ks, QB), :] for s in seqs], c_lo, c_hi, diag)

            def alive(c_lo, c_hi):
                top = functools.reduce(jnp.maximum, list(c_lo) + list(c_hi))
                return jnp.max(top) > SB_DEAD

            def k_block(state):
                j, _, (o, c_lo, c_hi) = state
                o2, s_lo, s_hi = tiles(pl.multiple_of(j * QB, QB), c_lo, c_hi, False)
                c_lo, c_hi = add(c_lo, s_lo), add(c_hi, s_hi)
                return j - 1, alive(c_lo, c_hi), (add(o, o2), c_lo, c_hi)

            o, c_lo, c_hi = tiles(qs, zeros, zeros, True)
            j, _, (o, c_lo, c_hi) = lax.while_loop(lambda st: jnp.logical_and(st[0] >= 0, st[1]), k_block,
                                                   (i - 1, alive(c_lo, c_hi), (o, c_lo, c_hi)))
            first_ref[pl.program_id(0), pl.program_id(1), i] = j + 1
            for s in seqs:
                o_ref[s, pl.ds(qs, QB), :] = o[s]
                tot_ref[s, pl.ds(qs, QB), :] = jnp.broadcast_to(_col_of(c_lo[s], c_hi[s]), (QB, LANES))
                y_ref[s, pl.ds(qs, QB), :] = sb_post(o[s], gain)
            return 0

        lax.fori_loop(0, n_q, q_block, 0)

    shape = jax.ShapeDtypeStruct((n_seq, seq_len, SB_W), F32)
    return pl.pallas_call(
        body, name="sb_fwd", grid=(n_seq // nb, n_pairs),
        in_specs=[seq_spec(SB_Q0), seq_spec(SB_K0), seq_spec(SB_V0), pl.BlockSpec((1, LANES), lambda b, h: (0, h))],
        out_specs=[out_spec] * 3 + [pl.BlockSpec(memory_space=pltpu.SMEM)],
        out_shape=[shape] * 3 + [jax.ShapeDtypeStruct((n_seq // nb, n_pairs, n_q), jnp.int32)],
        compiler_params=_params(("arbitrary", "arbitrary")),
    )(p, p, p, gain)


def sb_bwd(p, gain, o_raw, tot, dy, first, n_seq, seq_len):
    n_pairs = SB_W // LANES
    n_q = seq_len // QB
    nb = min(SB_SEQS, n_seq)

    def seq_spec(c0):
        return pl.BlockSpec((nb, seq_len, LANES), functools.partial(lambda h, b, c0: (b, 0, c0 + h), c0=c0))

    own = pl.BlockSpec((nb, seq_len, LANES), lambda h, b: (b, 0, h))
    par = pl.BlockSpec((1, LANES), lambda h, b: (0, h))

    def body(q_ref, k_ref, v_ref, g_ref, o_ref, tot_ref, dy_ref, first_ref, dq_ref, dk_ref, dv_ref, dg_ref):
        gain = g_ref[...]
        lo = _lane_lo((1, LANES))
        dk_ref[...] = jnp.zeros_like(dk_ref)
        dv_ref[...] = jnp.zeros_like(dv_ref)

        def q_block(i, dgain):
            qs = pl.multiple_of(i * QB, QB)
            seqs = range(nb)
            zeros = [jnp.zeros((QB, 1), F32)] * nb
            qv, dov, t_lo, t_hi = [], [], [], []
            for s in seqs:
                qv.append(sb_split_q(q_ref[s, pl.ds(qs, QB), :]))
                _, post_vjp = jax.vjp(sb_post, o_ref[s, pl.ds(qs, QB), :], gain)
                do, dg_s = post_vjp(dy_ref[s, pl.ds(qs, QB), :])
                dov.append(do)
                dgain = dgain + dg_s
                tot_s = tot_ref[s, pl.ds(qs, QB), :]
                t_lo.append(jnp.max(jnp.where(lo, tot_s, -jnp.inf), axis=1, keepdims=True))
                t_hi.append(jnp.max(jnp.where(lo, -jnp.inf, tot_s), axis=1, keepdims=True))
            add = lambda xs, ys: [x + y for x, y in zip(xs, ys)]
            sub = lambda xs, ys: [x - y for x, y in zip(xs, ys)]

            def tile(ks, carry, diag):
                dq, rem_lo, rem_hi, g_lo, g_hi = carry
                kv = [k_ref[s, pl.ds(ks, QB), :] for s in seqs]
                vv = [v_ref[s, pl.ds(ks, QB), :] for s in seqs]
                fn = functools.partial(sb_tile, diag=diag, from_here=list(zip(rem_lo, rem_hi)))
                (_, s_lo, s_hi), vjp = jax.vjp(fn, qv, kv, vv, zeros, zeros)
                dq_t, dk_t, dv_t, dc_lo, dc_hi = vjp((dov, g_lo, g_hi))
                dq_t = [jnp.where(lo, d_lo, d_hi) for d_lo, d_hi in dq_t]
                for s in seqs:
                    dk_ref[s, pl.ds(ks, QB), :] = dk_ref[s, pl.ds(ks, QB), :] + dk_t[s]
                    dv_ref[s, pl.ds(ks, QB), :] = dv_ref[s, pl.ds(ks, QB), :] + dv_t[s]
                return add(dq, dq_t), sub(rem_lo, s_lo), sub(rem_hi, s_hi), add(g_lo, dc_lo), add(g_hi, dc_hi)

            def k_block(j, carry):
                return tile(pl.multiple_of(j * QB, QB), carry, False)

            carry = ([jnp.zeros((QB, LANES), F32)] * nb, t_lo, t_hi, zeros, zeros)
            carry = lax.fori_loop(first_ref[pl.program_id(1), pl.program_id(0), i], i, k_block, carry)
            carry = tile(qs, carry, True)
            for s in seqs:
                dq_ref[s, pl.ds(qs, QB), :] = carry[0][s] * SB_SCALE
            return dgain

        dgain = lax.fori_loop(0, n_q, q_block, jnp.zeros((1, LANES), F32))
        first = pl.program_id(1) == 0

        @pl.when(first)
        def _():
            dg_ref[...] = dgain

        @pl.when(jnp.logical_not(first))
        def _():
            dg_ref[...] = dg_ref[...] + dgain

    shape = jax.ShapeDtypeStruct((n_seq, seq_len, SB_W), F32)
    return pl.pallas_call(
        body, name="sb_bwd", grid=(n_pairs, n_seq // nb),
        in_specs=[seq_spec(SB_Q0), seq_spec(SB_K0), seq_spec(SB_V0), par, own, own, own,
                  pl.BlockSpec(memory_space=pltpu.SMEM)],
        out_specs=[own, own, own, par],
        out_shape=[shape, shape, shape, jax.ShapeDtypeStruct((1, SB_W), F32)],
        compiler_params=_params(("arbitrary", "arbitrary")),
    )(p, p, p, gain, o_raw, tot, dy, first)


def exchange(name, arrays, gather):
    n = len(arrays)

    def body(*refs):
        ins, outs = refs[:n], refs[n:2 * n]
        send_sems, recv_sems, local_sems = refs[2 * n:]
        x, y, c = lax.axis_index("x"), lax.axis_index("y"), lax.axis_index("c")
        me = 4 * x + 2 * y + c
        copies = []
        for a in range(n):
            own = ins[a] if gather else ins[a].at[me]
            local = pltpu.make_async_copy(own, outs[a].at[me], local_sems.at[a])
            local.start()
            copies.append(local)
            for j in range(1, N_DEV):
                px, py, pc = (x + (j >> 2)) % 2, (y + ((j >> 1) & 1)) % 2, (c + (j & 1)) % 2
                src = ins[a] if gather else ins[a].at[4 * px + 2 * py + pc]
                remote = pltpu.make_async_remote_copy(
                    src_ref=src, dst_ref=outs[a].at[me], send_sem=send_sems.at[a, j - 1],
                    recv_sem=recv_sems.at[a, j - 1], device_id=(px, py, pc), device_id_type=pl.DeviceIdType.MESH)
                remote.start()
                copies.append(remote)
        for cp in copies:
            cp.wait()

    any_spec = pl.BlockSpec(memory_space=pl.ANY)
    out_shape = [jax.ShapeDtypeStruct(((N_DEV,) + a.shape) if gather else a.shape, a.dtype) for a in arrays]
    return pl.pallas_call(
        body, name=name, in_specs=[any_spec] * n, out_specs=[any_spec] * n, out_shape=out_shape,
        scratch_shapes=[pltpu.SemaphoreType.DMA((n, N_DEV - 1)), pltpu.SemaphoreType.DMA((n, N_DEV - 1)),
                        pltpu.SemaphoreType.DMA((n,))],
    )(*arrays)


def adamw(name, w, parts, m, v, tile):
    rows, cols = w.shape
    spec = pl.BlockSpec((tile, cols), lambda i: (i, 0))
    part_spec = pl.BlockSpec((N_DEV, tile, cols), lambda i: (0, i, 0))

    def body(w_ref, p_ref, m_ref, v_ref, g_ref, d_ref, nm_ref, nv_ref):
        g = p_ref[0].astype(F32)
        for s in range(1, N_DEV):
            g = g + p_ref[s].astype(F32)
        new_m = ADAM_B1 * m_ref[...] + (1.0 - ADAM_B1) * g
        new_v = ADAM_B2 * v_ref[...] + (1.0 - ADAM_B2) * (g * g)
        m_hat = new_m / (1.0 - ADAM_B1 ** ADAM_STEP)
        v_hat = new_v / (1.0 - ADAM_B2 ** ADAM_STEP)
        g_ref[...] = g
        d_ref[...] = -ADAM_LR * (m_hat / (jnp.sqrt(v_hat) + ADAM_EPS) + ADAM_WD * w_ref[...])
        nm_ref[...] = new_m
        nv_ref[...] = new_v

    shape = jax.ShapeDtypeStruct((rows, cols), F32)
    return pl.pallas_call(
        body, name=name, grid=(rows // tile,), in_specs=[spec, part_spec, spec, spec],
        out_specs=[spec] * 4, out_shape=[shape] * 4, compiler_params=_params(("arbitrary",)),
    )(w, parts, m, v)


SMALL = ("ln1_g", "tok_mu", "w0", "a0", "k_k", "k_a", "r_k", "gn_w", "gn_b", "sb_gain", "ln2_g", "lnf_g")
BIG = ("w_in", "w_decay_up", "w_aaa_up", "w_gate_up", "w_out", "w_up", "w_down")
COL_SHARDED = ("w_in", "w_decay_up", "w_aaa_up", "w_gate_up", "w_up")
ORDER = ("ln1_g", "w_in", "tok_mu", "w0", "w_decay_up", "a0", "w_aaa_up", "w_gate_up", "k_k", "k_a", "r_k",
         "gn_w", "gn_b", "sb_gain", "w_out", "ln2_g", "w_up", "w_down", "lnf_g")


def _pack(vectors, rows):
    flat = jnp.concatenate([v.reshape(-1).astype(F32) for v in vectors])
    return jnp.pad(flat, (0, rows * LANES - flat.shape[0])).reshape(rows, LANES)


def _full_cols(gathered):
    d, k, cols = gathered.shape
    return jnp.transpose(gathered, (1, 0, 2)).reshape(k, d * cols)


def _col_parts(full):
    k, n = full.shape
    return jnp.transpose(full.reshape(k, N_DEV, n // N_DEV), (1, 0, 2))


def kernel(x, ln1_g, w_in, tok_mu, w0, w_decay_up, a0, w_aaa_up, w_gate_up, k_k, k_a, r_k, gn_w, gn_b, sb_gain, w_out, ln2_g, w_up, w_down, lnf_g, loss_target, m_ln1_g, m_w_in, m_tok_mu, m_w0, m_w_decay_up, m_a0, m_w_aaa_up, m_w_gate_up, m_k_k, m_k_a, m_r_k, m_gn_w, m_gn_b, m_sb_gain, m_w_out, m_ln2_g, m_w_up, m_w_down, m_lnf_g, v_ln1_g, v_w_in, v_tok_mu, v_w0, v_w_decay_up, v_a0, v_w_aaa_up, v_w_gate_up, v_k_k, v_k_a, v_r_k, v_gn_w, v_gn_b, v_sb_gain, v_w_out, v_ln2_g, v_w_up, v_w_down, v_lnf_g):
    args = dict(locals())
    weights = {n: args[n] for n in ORDER}
    mom_m = {n: args["m_" + n] for n in ORDER}
    mom_v = {n: args["v_" + n] for n in ORDER}

    n_seq, seq_len, d_model = x.shape
    n_tok = n_seq * seq_len
    x2d = x.reshape(n_tok, d_model)
    tgt = loss_target.reshape(n_tok, d_model)
    row = lambda t: t.reshape(1, -1).astype(F32)

    shards = [weights[n][0].astype(BF16) for n in BIG]
    gathered = dict(zip(BIG, exchange("gather_weights", shards, True)))
    w_in_f = _full_cols(gathered["w_in"])
    zeros64 = jnp.zeros((HEAD_DIM, RWKV_W), BF16)
    wd_pad = jnp.concatenate([_full_cols(gathered["w_decay_up"]), zeros64], axis=0)
    wa_pad = jnp.concatenate([zeros64, _full_cols(gathered["w_aaa_up"])], axis=0)
    wg_f = _full_cols(gathered["w_gate_up"])
    w_out_f = gathered["w_out"].reshape(d_model, d_model)
    w_up_f = _full_cols(gathered["w_up"])
    w_down_f = gathered["w_down"].reshape(-1, d_model)
    d_ff = w_up_f.shape[1]
    in_cols = w_in_f.shape[1]

    g1, mu, w0r, a0r = row(ln1_g), row(tok_mu), row(w0), row(a0)
    kkr, kar, rkr, gwr, gbr, sgr = row(k_k), row(k_a), row(r_k), row(gn_w), row(gn_b), row(sb_gain)
    g2, gf = row(ln2_g), row(lnf_g)

    (h1,), _ = rw_call("norm1", lambda r, p: ([f_norm(r[0], p[0])], []), [x2d], [g1], [(d_model, BF16)], [], 512)
    p = matmul("proj_in", h1, w_in_f, "nn", F32, 512, in_cols // 2, d_model)
    p_rwkv = p[:, :RWKV_COLS].reshape(n_seq, seq_len, RWKV_COLS)
    p_prev = jnp.pad(p_rwkv[:, :-1], ((0, 0), (1, 0), (0, 0))).reshape(n_tok, RWKV_COLS)
    prep_pars = [mu, w0r, wd_pad, a0r, wa_pad, wg_f]
    (r_, kraw, v_, lw, asig, gate), _ = rw_call(
        "rwkv_prep", lambda r, q: (list(f_prep(r[0], r[1], *q)), []), [(p, RWKV_COLS, 0), p_prev], prep_pars,
        [(RWKV_W, F32)] * 6, [], 256)
    by_seq = lambda t: t.reshape(n_seq, seq_len, t.shape[-1])
    flat = lambda t: t.reshape(n_tok, t.shape[-1])
    rwkv_in = [by_seq(t) for t in (r_, kraw, v_, lw, asig)]
    z, s0_all = rwkv_fwd(*rwkv_in, kkr, kar, rkr, gwr, gbr, n_seq, seq_len)
    z = flat(z)
    y_sb, o_raw, tot, sb_first = sb_fwd(by_seq(p), sgr, n_seq, seq_len)
    y_sb = flat(y_sb)
    (ycat,), _ = rw_call("mix_cat", lambda r, q: ([jnp.concatenate([r[0] * r[1], r[2]], axis=1)], []),
                         [z, gate, y_sb], [], [(d_model, BF16)], [], 512)
    mix = matmul("proj_out", ycat, w_out_f, "nn", F32, 512, d_model, d_model)
    (x2, h2), _ = rw_call("resid_norm2", lambda r, q: ([r[0] + r[1], f_norm(r[0] + r[1], q[0])], []),
                          [x2d, mix], [g2], [(d_model, F32), (d_model, BF16)], [], 512)
    u = matmul("mlp_up", h2, w_up_f, "nn", F32, 512, d_ff // 4, d_model)
    (act,), _ = rw_call("relu2", lambda r, q: ([jnp.square(jnp.maximum(r[0], 0.0))], []), [u], [],
                        [(d_ff, BF16)], [], 256)
    mlp = matmul("mlp_down", act, w_down_f, "nn", F32, 512, d_model, d_ff // 2)

    def final_body(r, q):
        x3 = r[0] + r[1]
        loss_rows, vjp = jax.vjp(lambda a, g: f_final(a, g, r[2]), x3, q[0])
        dx3, dgf = vjp(jnp.ones_like(loss_rows))
        return [dx3], [jnp.broadcast_to(jnp.sum(loss_rows), (1, LANES)), dgf]

    (dx3,), (loss_acc, d_lnf) = rw_call("loss_head", final_body, [x2, mlp, tgt], [gf], [(d_model, F32)],
                                        [((1, LANES), F32), ((1, d_model), F32)], 256)

    dact = matmul("d_act", dx3, w_down_f, "nt", F32, 512, d_ff // 4, d_model)
    dw_down = matmul("dw_down", act, dx3, "tn", BF16, 512, d_model, 512)
    (du,), _ = rw_call("d_relu2", lambda r, q: ([r[0] * (2.0 * jnp.maximum(r[1], 0.0))], []), [dact, u], [],
                       [(d_ff, BF16)], [], 256)
    dh2 = matmul("d_h2", du, w_up_f, "nt", F32, 512, d_model, d_ff // 2)
    dw_up = matmul("dw_up", h2, du, "tn", BF16, d_model, 512, 512)

    def norm_bwd(r, q):
        xv, dh, dres = r
        _, vjp = jax.vjp(f_norm, xv, q[0])
        dx, dg = vjp(dh)
        return [dx + dres], [dg]

    (dx2,), (d_ln2,) = rw_call("d_norm2", norm_bwd, [x2, dh2, dx3], [g2], [(d_model, F32)],
                               [((1, d_model), F32)], 256)

    dycat = matmul("d_ycat", dx2, w_out_f, "nt", F32, 512, d_model, d_model)
    dw_out = matmul("dw_out", ycat, dx2, "tn", BF16, d_model, d_model, 512)
    (dz, dgate), _ = rw_call("d_mix", lambda r, q: ([r[0] * r[2], r[0] * r[1]], []),
                             [(dycat, RWKV_W, 0), z, gate], [], [(RWKV_W, F32)] * 2, [], 512)
    dq, dk_sb, dv_sb, d_sg = sb_bwd(by_seq(p), sgr, o_raw, tot, by_seq(dycat[:, RWKV_W:]), sb_first, n_seq, seq_len)
    dq, dk_sb, dv_sb = flat(dq), flat(dk_sb), flat(dv_sb)
    dr, dkraw, dv, dlw, dasig, d_kk, d_ka, d_rk, d_gw, d_gb = rwkv_bwd(
        *rwkv_in, kkr, kar, rkr, gwr, gbr, s0_all, by_seq(dz), n_seq, seq_len)
    dr, dkraw, dv, dlw, dasig = [flat(t) for t in (dr, dkraw, dv, dlw, dasig)]

    def prep_bwd(r, q):
        _, vjp = jax.vjp(f_prep, r[0], r[1], *[t.astype(F32) for t in q])
        grads = vjp(tuple(r[2:8]))
        return list(grads[:2]), list(grads[2:])

    par_shapes = [(t.shape, F32) for t in prep_pars]
    (dp_rwkv, dp_prev), (d_mu, d_w0, d_wd, d_a0, d_wa, d_wg) = rw_call(
        "d_rwkv_prep", prep_bwd, [(p, RWKV_COLS, 0), p_prev, dr, dkraw, dv, dlw, dasig, dgate], prep_pars,
        [(RWKV_COLS, F32)] * 2, par_shapes, 128)
    dp_shift = jnp.pad(dp_prev.reshape(n_seq, seq_len, RWKV_COLS)[:, 1:], ((0, 0), (0, 1), (0, 0)))
    dp = jnp.concatenate([dp_rwkv + dp_shift.reshape(n_tok, RWKV_COLS), dq, dk_sb, dv_sb], axis=1)

    dh1 = matmul("d_h1", dp, w_in_f, "nt", F32, 512, d_model, in_cols // 2)
    dw_in = matmul("dw_in", h1, dp, "tn", BF16, d_model, in_cols // 2, 512)
    (dx,), (d_ln1,) = rw_call("d_norm1", norm_bwd, [x2d, dh1, dx2], [g1], [(d_model, F32)],
                              [((1, d_model), F32)], 256)

    lora_parts = lambda t: _col_parts(t).astype(BF16)
    big_grads = {
        "w_in": _col_parts(dw_in), "w_decay_up": lora_parts(d_wd[:HEAD_DIM]), "w_aaa_up": lora_parts(d_wa[HEAD_DIM:]),
        "w_gate_up": lora_parts(d_wg), "w_out": dw_out.reshape(N_DEV, -1, d_model), "w_up": _col_parts(dw_up),
        "w_down": dw_down.reshape(N_DEV, -1, d_model),
    }
    parts = dict(zip(BIG, exchange("scatter_grads", [big_grads[n] for n in BIG], False)))

    small_grads = {"ln1_g": d_ln1, "tok_mu": d_mu, "w0": d_w0, "a0": d_a0, "k_k": d_kk, "k_a": d_ka, "r_k": d_rk,
                   "gn_w": d_gw, "gn_b": d_gb, "sb_gain": d_sg, "ln2_g": d_ln2, "lnf_g": d_lnf}
    n_small = sum(int(weights[n].size) for n in SMALL)
    pack_rows = -(-(n_small + 1) // (8 * LANES)) * 8
    packed = _pack([small_grads[n] for n in SMALL] + [loss_acc[0, :1]], pack_rows)
    (small_parts,) = exchange("gather_small", [packed], True)

    results = {}
    for n in BIG:
        w2d = weights[n][0]
        tile = w2d.shape[0] if w2d.shape[0] <= 256 else 256
        results[n] = adamw("adamw_" + n, w2d, parts[n], mom_m[n][0], mom_v[n][0], tile)
    pk = lambda d: _pack([d[n] for n in SMALL] + [jnp.zeros((1,), F32)], pack_rows)
    sg, sd, sm, sv = adamw("adamw_small", pk(weights), small_parts, pk(mom_m), pk(mom_v), pack_rows)
    off = 0
    for n in SMALL:
        size = int(weights[n].size)
        results[n] = tuple(t.reshape(-1)[off:off + size] for t in (sg, sd, sm, sv))
        off += size
    loss = sg.reshape(-1)[off]

    out = [loss, dx.reshape(x.shape)]
    for kind in range(4):
        out += [results[n][kind].reshape(weights[n].shape) for n in ORDER]
    return tuple(out)
```

```python
import functools
import math

import jax
import jax.numpy as jnp
from jax import lax
from jax.experimental import pallas as pl
from jax.experimental.pallas import tpu as pltpu

F32 = jnp.float32
BF16 = jnp.bfloat16

N_DEV = 8
HEAD_DIM = 64
LANES = 128
RWKV_W = 512
SB_W = 512
LORA_WA = 128
GATE_LORA = 128
RWKV_COLS = 3 * RWKV_W + LORA_WA + GATE_LORA
RMS_EPS = 1e-5
GN_EPS = 64e-5
CHUNK = 64
QB = 128
SB_SCALE = HEAD_DIM ** -0.5
ADAM_LR, ADAM_B1, ADAM_B2, ADAM_EPS, ADAM_WD, ADAM_STEP = 0.001, 0.9, 0.999, 1e-08, 0.01, 10
VMEM_LIMIT = 56 * 1024 * 1024


_DIMS = {
    "nn": (((1,), (0,)), ((), ())),
    "nt": (((1,), (1,)), ((), ())),
    "tn": (((0,), (0,)), ((), ())),
}


def _pieces(x, n):
    if n == 1:
        return [x.astype(BF16)]
    out, rem = [], x.astype(F32)
    for i in range(n):
        p = rem.astype(BF16)
        out.append(p)
        if i + 1 < n:
            rem = rem - p.astype(F32)
    return out


def _dot(a, b, form, pa, pb):
    pieces_a, pieces_b = _pieces(a, pa), _pieces(b, pb)
    keep = max(pa, pb)
    acc = None
    for i, ai in enumerate(pieces_a):
        for j, bj in enumerate(pieces_b):
            if i + j >= keep:
                continue
            t = lax.dot_general(ai, bj, _DIMS[form], preferred_element_type=F32)
            acc = t if acc is None else acc + t
    return acc


BOTH = (True, True)


@functools.partial(jax.custom_vjp, nondiff_argnums=(2, 3, 4, 5))
def mm(a, b, form, pa, pb, diff=BOTH):
    return _dot(a, b, form, pa, pb)


def _mm_fwd(a, b, form, pa, pb, diff):
    return _dot(a, b, form, pa, pb), (a, b)


def _mm_bwd(form, pa, pb, diff, res, g):
    a, b = res
    pg = max(pa, pb)
    da, db = jnp.zeros_like(a), jnp.zeros_like(b)
    if form == "nn":
        if diff[0]:
            da = mm(g, b, "nt", pg, pb)
        if diff[1]:
            db = mm(a, g, "tn", pa, pg)
    elif form == "nt":
        if diff[0]:
            da = mm(g, b, "nn", pg, pb)
        if diff[1]:
            db = mm(g, a, "tn", pg, pa)
    else:
        if diff[0]:
            da = mm(b, g, "nt", pb, pg)
        if diff[1]:
            db = mm(a, g, "nn", pa, pg)
    return da, db


mm.defvjp(_mm_fwd, _mm_bwd)


def _lane_lo(shape):
    return lax.broadcasted_iota(jnp.int32, shape, len(shape) - 1) < HEAD_DIM


def _segsum(x):
    lo = _lane_lo(x.shape)
    s_lo = jnp.sum(jnp.where(lo, x, 0.0), axis=-1, keepdims=True)
    s_hi = jnp.sum(jnp.where(lo, 0.0, x), axis=-1, keepdims=True)
    return jnp.where(lo, s_lo, s_hi)


def _sigmoid(x):
    return 0.5 * (jnp.tanh(0.5 * x) + 1.0)


def _softplus(x):
    return jnp.maximum(x, 0.0) + jnp.log(1.0 + jnp.exp(-jnp.abs(x)))


def f_norm(x, g):
    return x * lax.rsqrt(jnp.mean(x * x, axis=-1, keepdims=True) + RMS_EPS) * g


def f_prep(p, pprev, mu, w0, wd_pad, a0, wa_pad, wg):
    pm = p + mu * (pprev - p)
    r = pm[:, 0:RWKV_W]
    k = pm[:, RWKV_W:2 * RWKV_W]
    v = pm[:, 2 * RWKV_W:3 * RWKV_W]
    xwa = pm[:, 3 * RWKV_W:3 * RWKV_W + LORA_WA]
    xg = pm[:, 3 * RWKV_W + LORA_WA:RWKV_COLS]
    w = -_softplus(-(w0 + mm(jnp.tanh(xwa), wd_pad, "nn", 1, 1))) - 0.5
    lw = -jnp.exp(w)
    asig = _sigmoid(a0 + mm(xwa, wa_pad, "nn", 1, 1))
    gate = mm(_sigmoid(xg), wg, "nn", 1, 1)
    return r, k, v, lw, asig, gate


def _tri(n, kind):
    row = lax.broadcasted_iota(jnp.int32, (n, n), 0)
    col = lax.broadcasted_iota(jnp.int32, (n, n), 1)
    if kind == "lower_incl":
        return row >= col
    return row > col


def rwkv_chunk(state, r, kraw, v, lw, asig, k_k, k_a, r_k, gn_w, gn_b, hp):
    n = len(r)
    L = r[0].shape[0]
    lo = _lane_lo((1, LANES))
    masks = (lo, jnp.logical_not(lo))
    incl = _tri(L, "lower_incl")
    strict = _tri(L, "strict")
    tri = incl.astype(F32)
    eye = (lax.broadcasted_iota(jnp.int32, (L, L), 0) == lax.broadcasted_iota(jnp.int32, (L, L), 1)).astype(F32)
    kk = [x * k_k for x in kraw]
    kk = [x / jnp.maximum(jnp.sqrt(_segsum(x * x)), 1e-12) for x in kk]
    k = [x * (1.0 + (s - 1.0) * k_a) for x, s in zip(kraw, asig)]
    b = [x * s for x, s in zip(kk, asig)]
    c = [mm(tri, x, "nn", 1, 3, (False, True)) for x in lw]
    at = [-x * jnp.exp(ci - li) for x, ci, li in zip(kk, c, lw)]
    rt = [x * jnp.exp(ci) for x, ci in zip(r, c)]
    einv = [jnp.exp(-ci) for ci in c]
    bt = [x * e for x, e in zip(b, einv)]
    kt = [x * e for x, e in zip(k, einv)]
    inst = [(s, m) for s in range(n) for m in masks]
    at_h = [jnp.where(m, at[s], 0.0) for s, m in inst]
    rt_h = [jnp.where(m, rt[s], 0.0) for s, m in inst]
    a_ab = [jnp.where(strict, mm(x, bt[s], "nt", hp, hp), 0.0) for x, (s, _) in zip(at_h, inst)]
    a_ak = [jnp.where(strict, mm(x, kt[s], "nt", hp, hp), 0.0) for x, (s, _) in zip(at_h, inst)]
    b_rb = [jnp.where(incl, mm(x, bt[s], "nt", hp, hp), 0.0) for x, (s, _) in zip(rt_h, inst)]
    b_rk = [jnp.where(incl, mm(x, kt[s], "nt", hp, hp), 0.0) for x, (s, _) in zip(rt_h, inst)]
    tinv = [eye + x for x in a_ab]
    pw = a_ab
    for _ in range(int(math.log2(L)) - 1):
        pw = [mm(x, x, "nn", hp, hp) for x in pw]
        tinv = [t + mm(t, x, "nn", hp, hp) for t, x in zip(tinv, pw)]
    rhs = [mm(x, state[s], "nt", hp, hp) + mm(y, v[s], "nn", hp, hp) for x, y, (s, _) in zip(at_h, a_ak, inst)]
    u_h = [mm(t, x, "nn", hp, hp) for t, x in zip(tinv, rhs)]
    y_h = [mm(x, state[s], "nt", hp, hp) + mm(m1, u, "nn", hp, hp) + mm(m2, v[s], "nn", hp, hp)
           for x, m1, m2, u, (s, _) in zip(rt_h, b_rb, b_rk, u_h, inst)]
    u_all = [jnp.where(lo, u_h[2 * s], u_h[2 * s + 1]) for s in range(n)]
    y_all = [jnp.where(lo, y_h[2 * s], y_h[2 * s + 1]) for s in range(n)]
    c_last = [jnp.sum(x, axis=0, keepdims=True) for x in lw]
    efwd = [jnp.exp(cl - ci) for cl, ci in zip(c_last, c)]
    new_state = [st * jnp.exp(cl) + mm(u, bi * e, "tn", hp, hp) + mm(vi, ki * e, "tn", hp, hp)
                 for st, cl, u, bi, e, vi, ki in zip(state, c_last, u_all, b, efwd, v, k)]
    row_head = lax.broadcasted_iota(jnp.int32, (LANES, LANES), 0) // HEAD_DIM
    col_head = lax.broadcasted_iota(jnp.int32, (LANES, LANES), 1) // HEAD_DIM
    new_state = [jnp.where(row_head == col_head, x, 0.0) for x in new_state]
    outs = []
    for y, ri, ki, vi in zip(y_all, r, k, v):
        mean = _segsum(y) * (1.0 / HEAD_DIM)
        d = y - mean
        var = _segsum(d * d) * (1.0 / HEAD_DIM)
        yn = d * lax.rsqrt(var + GN_EPS) * gn_w + gn_b
        outs.append(yn + _segsum(ri * ki * r_k) * vi)
    return outs, new_state


def sb_tile(q, k, v, c_lo, c_hi, diag, from_here=None):
    n = len(q)
    lo = _lane_lo((1, LANES))
    below = _tri(QB, "strict")
    from_s = _tri(QB, "lower_incl").astype(F32)
    inst = [(s, h) for s in range(n) for h in (0, 1)]
    carry = [(c_lo[s], c_hi[s])[h] for s, h in inst]
    z = [mm(q[s][h], k[s], "nt", 1, 1) for s, h in inst]
    soft = [jnp.log(1.0 + jnp.exp(-jnp.abs(x))) for x in z]
    log_keep = [-jnp.maximum(x, 0.0) - sp for x, sp in zip(z, soft)]
    if diag:
        log_keep = [jnp.where(below, x, 0.0) for x in log_keep]
    own = [jnp.sum(x, axis=1, keepdims=True) for x in log_keep]
    if from_here is not None:
        carry = [lax.stop_gradient(from_here[s][h] - o) + cr for (s, h), o, cr in zip(inst, own, carry)]
    tail = [mm(x, from_s, "nn", SB_SUM_PIECES, 1, (True, False)) for x in log_keep]
    log_a = [x + tl + cr for x, tl, cr in zip(z, tail, carry)]
    if diag:
        log_a = [jnp.where(below, x, -1e30) for x in log_a]
    att = [jnp.exp(x) for x in log_a]
    out_h = [mm(x, v[s], "nn", 1, 1) for x, (s, _) in zip(att, inst)]
    out = [jnp.where(lo, out_h[2 * s], out_h[2 * s + 1]) for s in range(n)]
    return out, [own[2 * s] for s in range(n)], [own[2 * s + 1] for s in range(n)]


def sb_split_q(q):
    lo = _lane_lo((1, LANES))
    qs = q * SB_SCALE
    return jnp.where(lo, qs, 0.0), jnp.where(lo, 0.0, qs)


def sb_post(o, gain):
    return o * lax.rsqrt(_segsum(o * o) * (1.0 / HEAD_DIM) + RMS_EPS) * gain


def f_final(x3, g, target):
    y = f_norm(x3, g)
    err = y - target
    return 0.5 * jnp.mean(err * err, axis=-1, keepdims=True)


def _params(sem):
    return pltpu.CompilerParams(dimension_semantics=sem, vmem_limit_bytes=VMEM_LIMIT)


def rw_call(name, body_fn, rows, pars, out_rows, out_accs, tile):
    rows = [item if isinstance(item, tuple) else (item, item.shape[1], 0) for item in rows]
    row_arrays = [arr for arr, _, _ in rows]
    n_rows = row_arrays[0].shape[0]
    tile = min(tile, n_rows)
    steps = n_rows // tile
    row_specs = [pl.BlockSpec((tile, cols), functools.partial(lambda i, c: (i, c), c=cblk)) for _, cols, cblk in rows]
    par_specs = [pl.BlockSpec(p.shape, lambda i: (0, 0)) for p in pars]
    nr, npar, nor, noa = len(row_arrays), len(pars), len(out_rows), len(out_accs)

    def body(*refs):
        row_vals = [refs[i][...] for i in range(nr)]
        par_vals = [refs[nr + i][...] for i in range(npar)]
        o_refs = refs[nr + npar:nr + npar + nor]
        a_refs = refs[nr + npar + nor:]
        row_outs, acc_outs = body_fn(row_vals, par_vals)
        for ref, val in zip(o_refs, row_outs):
            ref[...] = val.astype(ref.dtype)
        if noa:
            first = pl.program_id(0) == 0

            @pl.when(first)
            def _():
                for ref, val in zip(a_refs, acc_outs):
                    ref[...] = val.astype(ref.dtype)

            @pl.when(jnp.logical_not(first))
            def _():
                for ref, val in zip(a_refs, acc_outs):
                    ref[...] = ref[...] + val.astype(ref.dtype)

    out_shape = [jax.ShapeDtypeStruct((n_rows, c), dt) for c, dt in out_rows]
    out_shape += [jax.ShapeDtypeStruct(s, dt) for s, dt in out_accs]
    out_specs = [pl.BlockSpec((tile, c), lambda i: (i, 0)) for c, _ in out_rows]
    out_specs += [pl.BlockSpec(s, lambda i: (0, 0)) for s, _ in out_accs]
    outs = pl.pallas_call(
        body, name=name, grid=(steps,), in_specs=row_specs + par_specs, out_specs=out_specs,
        out_shape=out_shape, compiler_params=_params(("arbitrary",)),
    )(*row_arrays, *pars)
    return outs[:nor], outs[nor:]


def matmul(name, a, b, form, out_dtype, tm, tn, tk, extras=(), epilogue=None):
    out_dtypes = out_dtype if isinstance(out_dtype, tuple) else (out_dtype,)
    tm, tn, tk = min(tm, a.shape[1 if form == "tn" else 0]), min(tn, b.shape[0 if form == "nt" else 1]), min(tk, a.shape[0 if form == "tn" else 1])
    if form == "nn":
        (m, kd), n = a.shape, b.shape[1]
        a_spec = pl.BlockSpec((tm, tk), lambda i, j, k: (i, k))
        b_spec = pl.BlockSpec((tk, tn), lambda i, j, k: (k, j))
    elif form == "nt":
        (m, kd), n = a.shape, b.shape[0]
        a_spec = pl.BlockSpec((tm, tk), lambda i, j, k: (i, k))
        b_spec = pl.BlockSpec((tn, tk), lambda i, j, k: (j, k))
    else:
        (kd, m), n = a.shape, b.shape[1]
        a_spec = pl.BlockSpec((tk, tm), lambda i, j, k: (k, i))
        b_spec = pl.BlockSpec((tk, tn), lambda i, j, k: (k, j))
    ksteps = kd // tk

    n_extra, n_out = len(extras), len(out_dtypes)

    def body(a_ref, b_ref, *rest):
        e_refs, o_refs = rest[:n_extra], rest[n_extra:n_extra + n_out]
        kstep = pl.program_id(2)
        part = lax.dot_general(a_ref[...].astype(BF16), b_ref[...].astype(BF16), _DIMS[form],
                               preferred_element_type=F32)

        def finish(acc):
            outs = epilogue(acc, *[r[...] for r in e_refs]) if epilogue else (acc,)
            for ref, val in zip(o_refs, outs):
                ref[...] = val.astype(ref.dtype)

        if ksteps == 1:
            finish(part)
            return
        acc_ref = rest[-1]

        @pl.when(kstep == 0)
        def _():
            acc_ref[...] = part

        @pl.when(jnp.logical_and(kstep > 0, kstep < ksteps - 1))
        def _():
            acc_ref[...] = acc_ref[...] + part

        @pl.when(kstep == ksteps - 1)
        def _():
            finish(acc_ref[...] + part)

    out_spec = pl.BlockSpec((tm, tn), lambda i, j, k: (i, j))
    outs = pl.pallas_call(
        body, name=name, grid=(m // tm, n // tn, ksteps), in_specs=[a_spec, b_spec] + [out_spec] * n_extra,
        out_specs=[out_spec] * n_out,
        out_shape=[jax.ShapeDtypeStruct((m, n), dt) for dt in out_dtypes],
        scratch_shapes=[pltpu.VMEM((tm, tn), F32)] if ksteps > 1 else [],
        compiler_params=_params(("parallel", "parallel", "arbitrary")),
    )(a, b, *extras)
    return outs if isinstance(out_dtype, tuple) else outs[0]


PREP_TILE = 256
PREP_TILE_BWD = 128
SUBLANES = 8


def _shift_in(rows, first):
    rolled = pltpu.roll(rows, 1, 0)
    row = lax.broadcasted_iota(jnp.int32, (SUBLANES, rows.shape[1]), 0)
    head = jnp.where(row == 0, first, rolled[0:SUBLANES])
    return jnp.concatenate([head, rolled[SUBLANES:]], axis=0), rolled


def rwkv_prep_fwd(p, pars, seq_len):
    n_tok = p.shape[0]
    tile = min(PREP_TILE, seq_len)
    tile_b = min(PREP_TILE_BWD, tile)
    steps, per_seq, sub = n_tok // tile, seq_len // tile, tile // tile_b
    n_par = len(pars)

    def body(p_ref, *rest):
        par_refs, out_refs, edge_ref, last8 = rest[:n_par], rest[n_par:n_par + 6], rest[n_par + 6], rest[n_par + 7]
        step = pl.program_id(0)

        @pl.when(step == 0)
        def _():
            last8[...] = jnp.zeros_like(last8)

        rows = p_ref[...]
        before = jnp.where(step % per_seq == 0, 0.0, pltpu.roll(last8[...], 1, 0))
        prev, rolled = _shift_in(rows, before)
        edge_ref[0] = prev[0:SUBLANES]
        for m in range(1, sub):
            edge_ref[m] = rolled[m * tile_b:m * tile_b + SUBLANES]
        last8[...] = rows[tile - SUBLANES:tile]
        for ref, val in zip(out_refs, f_prep(rows, prev, *[r[...] for r in par_refs])):
            ref[...] = val

    row_out = pl.BlockSpec((tile, RWKV_W), lambda i: (i, 0))
    outs = pl.pallas_call(
        body, name="rwkv_prep", grid=(steps,),
        in_specs=[pl.BlockSpec((tile, RWKV_COLS), lambda i: (i, 0))] + [pl.BlockSpec(t.shape, lambda i: (0, 0)) for t in pars],
        out_specs=[row_out] * 6 + [pl.BlockSpec((sub, SUBLANES, RWKV_COLS), lambda i: (i, 0, 0))],
        out_shape=[jax.ShapeDtypeStruct((n_tok, RWKV_W), F32)] * 6
        + [jax.ShapeDtypeStruct((steps * sub, SUBLANES, RWKV_COLS), F32)],
        scratch_shapes=[pltpu.VMEM((SUBLANES, RWKV_COLS), F32)],
        compiler_params=_params(("arbitrary",)),
    )(p, *pars)
    return outs[:6], outs[6]


def rwkv_prep_bwd(p, edges, pars, cots, d_sb, seq_len):
    n_tok = p.shape[0]
    tile = min(PREP_TILE_BWD, seq_len)
    steps, per_seq = n_tok // tile, seq_len // tile
    n_par = len(pars)
    back = lambda i: steps - 1 - i

    def body(p_ref, edge_ref, *rest):
        par_refs, rest = rest[:n_par], rest[n_par:]
        cot_refs, sb_refs, dp_ref, acc_refs, next8 = rest[:6], rest[6:9], rest[9], rest[10:10 + n_par], rest[10 + n_par]
        step = pl.program_id(0)
        first = step == 0

        @pl.when(first)
        def _():
            next8[...] = jnp.zeros_like(next8)

        rows = p_ref[...]
        prev, _ = _shift_in(rows, edge_ref[0])
        _, vjp = jax.vjp(f_prep, rows, prev, *[r[...].astype(F32) for r in par_refs])
        grads = vjp(tuple(r[...] for r in cot_refs))
        d_rows, d_prev = grads[0], grads[1]
        up = pltpu.roll(d_prev, tile - 1, 0)
        ends_seq = back(step) % per_seq == per_seq - 1
        after = jnp.where(ends_seq, 0.0, pltpu.roll(next8[...], SUBLANES - 1, 0))
        row = lax.broadcasted_iota(jnp.int32, (SUBLANES, RWKV_COLS), 0)
        tail = jnp.where(row == SUBLANES - 1, after, up[tile - SUBLANES:tile])
        d_rows = d_rows + jnp.concatenate([up[:tile - SUBLANES], tail], axis=0)
        next8[...] = d_prev[0:SUBLANES]
        dp_ref[...] = jnp.concatenate([d_rows] + [r[...] for r in sb_refs], axis=1).astype(dp_ref.dtype)

        @pl.when(first)
        def _():
            for ref, val in zip(acc_refs, grads[2:]):
                ref[...] = val

        @pl.when(jnp.logical_not(first))
        def _():
            for ref, val in zip(acc_refs, grads[2:]):
                ref[...] = ref[...] + val

    cols = RWKV_COLS + sum(t.shape[1] for t in d_sb)
    half = pl.BlockSpec((tile, RWKV_W), lambda i: (back(i), 0))
    par_specs = [pl.BlockSpec(t.shape, lambda i: (0, 0)) for t in pars]
    outs = pl.pallas_call(
        body, name="d_rwkv_prep", grid=(steps,),
        in_specs=[pl.BlockSpec((tile, RWKV_COLS), lambda i: (back(i), 0)),
                  pl.BlockSpec((1, SUBLANES, RWKV_COLS), lambda i: (back(i), 0, 0))] + par_specs + [half] * 9,
        out_specs=[pl.BlockSpec((tile, cols), lambda i: (back(i), 0))] + par_specs,
        out_shape=[jax.ShapeDtypeStruct((n_tok, cols), BF16)] + [jax.ShapeDtypeStruct(t.shape, F32) for t in pars],
        scratch_shapes=[pltpu.VMEM((SUBLANES, RWKV_COLS), F32)],
        compiler_params=_params(("arbitrary",)),
    )(p, edges, *pars, *cots, *d_sb)
    return outs[0], outs[1:]


RWKV_HP = 1


def rwkv_fwd(r, kraw, v, lw, asig, k_k, k_a, r_k, gn_w, gn_b, n_seq, seq_len):
    n_chunks = seq_len // CHUNK
    n_pairs = RWKV_W // LANES
    row = pl.BlockSpec((n_seq, CHUNK, LANES), lambda h, c: (0, c, h))
    par = pl.BlockSpec((1, LANES), lambda h, c: (0, h))
    s0_spec = pl.BlockSpec((1, 1, n_seq, LANES, LANES), lambda h, c: (h, c, 0, 0, 0))

    def body(r_ref, k_ref, v_ref, lw_ref, a_ref, kk_ref, ka_ref, rk_ref, gw_ref, gb_ref, z_ref, s0_ref, state):
        @pl.when(pl.program_id(1) == 0)
        def _():
            state[...] = jnp.zeros_like(state)

        pars = [ref[...] for ref in (kk_ref, ka_ref, rk_ref, gw_ref, gb_ref)]
        seqs = range(n_seq)
        s0 = [state[s] for s in seqs]
        rows = [[ref[s] for s in seqs] for ref in (r_ref, k_ref, v_ref, lw_ref, a_ref)]
        z, s1 = rwkv_chunk(s0, *rows, *pars, RWKV_HP)
        for s in seqs:
            s0_ref[0, 0, s] = s0[s]
            z_ref[s] = z[s]
            state[s] = s1[s]

    return pl.pallas_call(
        body, name="rwkv_fwd", grid=(n_pairs, n_chunks),
        in_specs=[row] * 5 + [par] * 5, out_specs=[row, s0_spec],
        out_shape=[jax.ShapeDtypeStruct(r.shape, F32),
                   jax.ShapeDtypeStruct((n_pairs, n_chunks, n_seq, LANES, LANES), F32)],
        scratch_shapes=[pltpu.VMEM((n_seq, LANES, LANES), F32)],
        compiler_params=_params(("arbitrary", "arbitrary")),
    )(r, kraw, v, lw, asig, k_k, k_a, r_k, gn_w, gn_b)


def rwkv_bwd(r, kraw, v, lw, asig, k_k, k_a, r_k, gn_w, gn_b, s0_all, dz, n_seq, seq_len):
    n_chunks = seq_len // CHUNK
    n_pairs = RWKV_W // LANES
    row = pl.BlockSpec((n_seq, CHUNK, LANES), lambda h, c: (0, n_chunks - 1 - c, h))
    par = pl.BlockSpec((1, LANES), lambda h, c: (0, h))
    s0_spec = pl.BlockSpec((1, 1, n_seq, LANES, LANES), lambda h, c: (h, n_chunks - 1 - c, 0, 0, 0))

    def body(r_ref, k_ref, v_ref, lw_ref, a_ref, kk_ref, ka_ref, rk_ref, gw_ref, gb_ref, s0_ref, dz_ref,
             dr_ref, dk_ref, dv_ref, dlw_ref, da_ref, dkk_ref, dka_ref, drk_ref, dgw_ref, dgb_ref, dstate):
        first = pl.program_id(1) == 0

        @pl.when(first)
        def _():
            dstate[...] = jnp.zeros_like(dstate)

        pars = [ref[...] for ref in (kk_ref, ka_ref, rk_ref, gw_ref, gb_ref)]
        fn = functools.partial(rwkv_chunk, hp=RWKV_HP)
        seqs = range(n_seq)
        rows = [[ref[s] for s in seqs] for ref in (r_ref, k_ref, v_ref, lw_ref, a_ref)]
        _, vjp = jax.vjp(fn, [s0_ref[0, 0, s] for s in seqs], *rows, *pars)
        grads = vjp(([dz_ref[s] for s in seqs], [dstate[s] for s in seqs]))
        for s in seqs:
            dstate[s] = grads[0][s]
            for ref, val in zip((dr_ref, dk_ref, dv_ref, dlw_ref, da_ref), grads[1:6]):
                ref[s] = val[s]
        accs = tuple(zip((dkk_ref, dka_ref, drk_ref, dgw_ref, dgb_ref), grads[6:]))

        @pl.when(first)
        def _():
            for ref, val in accs:
                ref[...] = val

        @pl.when(jnp.logical_not(first))
        def _():
            for ref, val in accs:
                ref[...] = ref[...] + val

    rows_shape = jax.ShapeDtypeStruct(r.shape, F32)
    par_shape = jax.ShapeDtypeStruct((1, RWKV_W), F32)
    return pl.pallas_call(
        body, name="rwkv_bwd", grid=(n_pairs, n_chunks),
        in_specs=[row] * 5 + [par] * 5 + [s0_spec, row],
        out_specs=[row] * 5 + [par] * 5,
        out_shape=[rows_shape] * 5 + [par_shape] * 5,
        scratch_shapes=[pltpu.VMEM((n_seq, LANES, LANES), F32)],
        compiler_params=_params(("arbitrary", "arbitrary")),
    )(r, kraw, v, lw, asig, k_k, k_a, r_k, gn_w, gn_b, s0_all, dz)


SB_Q0 = RWKV_COLS // LANES
SB_K0 = SB_Q0 + SB_W // LANES
SB_V0 = SB_K0 + SB_W // LANES
SB_SEQS = 2
SB_SUM_PIECES = 2
SB_DEAD = -110.0


def _col_of(c_lo, c_hi):
    return jnp.where(_lane_lo((1, LANES)), c_lo, c_hi)


def sb_fwd(p, gain, n_seq, seq_len):
    n_pairs = SB_W // LANES
    n_q = seq_len // QB
    nb = min(SB_SEQS, n_seq)

    def seq_spec(c0):
        return pl.BlockSpec((nb, seq_len, LANES), functools.partial(lambda b, h, c0: (b, 0, c0 + h), c0=c0))

    out_spec = pl.BlockSpec((nb, seq_len, LANES), lambda b, h: (b, 0, h))

    def body(q_ref, k_ref, v_ref, g_ref, y_ref, o_ref, tot_ref, first_ref):
        gain = g_ref[...]

        def q_block(i, _):
            qs = pl.multiple_of(i * QB, QB)
            seqs = range(nb)
            zeros = [jnp.zeros((QB, 1), F32)] * nb
            qv = [sb_split_q(q_ref[s, pl.ds(qs, QB), :]) for s in seqs]
            add = lambda xs, ys: [x + y for x, y in zip(xs, ys)]

            def tiles(ks, c_lo, c_hi, diag):
                return sb_tile(qv, [k_ref[s, pl.ds(ks, QB), :] for s in seqs],
                               [v_ref[s, pl.ds(ks, QB), :] for s in seqs], c_lo, c_hi, diag)

            def alive(c_lo, c_hi):
                top = functools.reduce(jnp.maximum, list(c_lo) + list(c_hi))
                return jnp.max(top) > SB_DEAD

            def k_block(state):
                j, _, (o, c_lo, c_hi) = state
                o2, s_lo, s_hi = tiles(pl.multiple_of(j * QB, QB), c_lo, c_hi, False)
                c_lo, c_hi = add(c_lo, s_lo), add(c_hi, s_hi)
                return j - 1, alive(c_lo, c_hi), (add(o, o2), c_lo, c_hi)

            o, c_lo, c_hi = tiles(qs, zeros, zeros, True)
            j, _, (o, c_lo, c_hi) = lax.while_loop(lambda st: jnp.logical_and(st[0] >= 0, st[1]), k_block,
                                                   (i - 1, alive(c_lo, c_hi), (o, c_lo, c_hi)))
            first_ref[pl.program_id(0), pl.program_id(1), i] = j + 1
            for s in seqs:
                o_ref[s, pl.ds(qs, QB), :] = o[s]
                tot_ref[s, pl.ds(qs, QB), :] = jnp.broadcast_to(_col_of(c_lo[s], c_hi[s]), (QB, LANES))
                y_ref[s, pl.ds(qs, QB), :] = sb_post(o[s], gain)
            return 0

        lax.fori_loop(0, n_q, q_block, 0)

    shape = jax.ShapeDtypeStruct((n_seq, seq_len, SB_W), F32)
    return pl.pallas_call(
        body, name="sb_fwd", grid=(n_seq // nb, n_pairs),
        in_specs=[seq_spec(SB_Q0), seq_spec(SB_K0), seq_spec(SB_V0), pl.BlockSpec((1, LANES), lambda b, h: (0, h))],
        out_specs=[out_spec] * 3 + [pl.BlockSpec(memory_space=pltpu.SMEM)],
        out_shape=[shape] * 3 + [jax.ShapeDtypeStruct((n_seq // nb, n_pairs, n_q), jnp.int32)],
        compiler_params=_params(("arbitrary", "arbitrary")),
    )(p, p, p, gain)


def sb_bwd(p, gain, o_raw, tot, dy, first, n_seq, seq_len):
    n_pairs = SB_W // LANES
    n_q = seq_len // QB
    nb = min(SB_SEQS, n_seq)

    def seq_spec(c0):
        return pl.BlockSpec((nb, seq_len, LANES), functools.partial(lambda h, b, c0: (b, 0, c0 + h), c0=c0))

    own = pl.BlockSpec((nb, seq_len, LANES), lambda h, b: (b, 0, h))
    par = pl.BlockSpec((1, LANES), lambda h, b: (0, h))

    def body(q_ref, k_ref, v_ref, g_ref, o_ref, tot_ref, dy_ref, first_ref, dq_ref, dk_ref, dv_ref, dg_ref):
        gain = g_ref[...]
        lo = _lane_lo((1, LANES))
        dk_ref[...] = jnp.zeros_like(dk_ref)
        dv_ref[...] = jnp.zeros_like(dv_ref)

        def q_block(i, dgain):
            qs = pl.multiple_of(i * QB, QB)
            seqs = range(nb)
            zeros = [jnp.zeros((QB, 1), F32)] * nb
            qv, dov, t_lo, t_hi = [], [], [], []
            for s in seqs:
                qv.append(sb_split_q(q_ref[s, pl.ds(qs, QB), :]))
                _, post_vjp = jax.vjp(sb_post, o_ref[s, pl.ds(qs, QB), :], gain)
                do, dg_s = post_vjp(dy_ref[s, pl.ds(qs, QB), :])
                dov.append(do)
                dgain = dgain + dg_s
                tot_s = tot_ref[s, pl.ds(qs, QB), :]
                t_lo.append(jnp.max(jnp.where(lo, tot_s, -jnp.inf), axis=1, keepdims=True))
                t_hi.append(jnp.max(jnp.where(lo, -jnp.inf, tot_s), axis=1, keepdims=True))
            add = lambda xs, ys: [x + y for x, y in zip(xs, ys)]
            sub = lambda xs, ys: [x - y for x, y in zip(xs, ys)]

            def tile(ks, carry, diag):
                dq, rem_lo, rem_hi, g_lo, g_hi = carry
                kv = [k_ref[s, pl.ds(ks, QB), :] for s in seqs]
                vv = [v_ref[s, pl.ds(ks, QB), :] for s in seqs]
                fn = functools.partial(sb_tile, diag=diag, from_here=list(zip(rem_lo, rem_hi)))
                (_, s_lo, s_hi), vjp = jax.vjp(fn, qv, kv, vv, zeros, zeros)
                dq_t, dk_t, dv_t, dc_lo, dc_hi = vjp((dov, g_lo, g_hi))
                dq_t = [jnp.where(lo, d_lo, d_hi) for d_lo, d_hi in dq_t]
                for s in seqs:
                    dk_ref[s, pl.ds(ks, QB), :] = dk_ref[s, pl.ds(ks, QB), :] + dk_t[s]
                    dv_ref[s, pl.ds(ks, QB), :] = dv_ref[s, pl.ds(ks, QB), :] + dv_t[s]
                return add(dq, dq_t), sub(rem_lo, s_lo), sub(rem_hi, s_hi), add(g_lo, dc_lo), add(g_hi, dc_hi)

            def k_block(j, carry):
                return tile(pl.multiple_of(j * QB, QB), carry, False)

            carry = ([jnp.zeros((QB, LANES), F32)] * nb, t_lo, t_hi, zeros, zeros)
            carry = lax.fori_loop(first_ref[pl.program_id(1), pl.program_id(0), i], i, k_block, carry)
            carry = tile(qs, carry, True)
            for s in seqs:
                dq_ref[s, pl.ds(qs, QB), :] = carry[0][s] * SB_SCALE
            return dgain

        dgain = lax.fori_loop(0, n_q, q_block, jnp.zeros((1, LANES), F32))
        first = pl.program_id(1) == 0

        @pl.when(first)
        def _():
            dg_ref[...] = dgain

        @pl.when(jnp.logical_not(first))
        def _():
            dg_ref[...] = dg_ref[...] + dgain

    shape = jax.ShapeDtypeStruct((n_seq, seq_len, SB_W), F32)
    return pl.pallas_call(
        body, name="sb_bwd", grid=(n_pairs, n_seq // nb),
        in_specs=[seq_spec(SB_Q0), seq_spec(SB_K0), seq_spec(SB_V0), par, own, own, own,
                  pl.BlockSpec(memory_space=pltpu.SMEM)],
        out_specs=[own, own, own, par],
        out_shape=[shape, shape, shape, jax.ShapeDtypeStruct((1, SB_W), F32)],
        compiler_params=_params(("arbitrary", "arbitrary")),
    )(p, p, p, gain, o_raw, tot, dy, first)


def exchange(name, arrays, gather):
    n = len(arrays)

    def body(*refs):
        ins, outs = refs[:n], refs[n:2 * n]
        send_sems, recv_sems, local_sems = refs[2 * n:]
        x, y, c = lax.axis_index("x"), lax.axis_index("y"), lax.axis_index("c")
        me = 4 * x + 2 * y + c
        copies = []
        for a in range(n):
            own = ins[a] if gather else ins[a].at[me]
            local = pltpu.make_async_copy(own, outs[a].at[me], local_sems.at[a])
            local.start()
            copies.append(local)
            for j in range(1, N_DEV):
                px, py, pc = (x + (j >> 2)) % 2, (y + ((j >> 1) & 1)) % 2, (c + (j & 1)) % 2
                src = ins[a] if gather else ins[a].at[4 * px + 2 * py + pc]
                remote = pltpu.make_async_remote_copy(
                    src_ref=src, dst_ref=outs[a].at[me], send_sem=send_sems.at[a, j - 1],
                    recv_sem=recv_sems.at[a, j - 1], device_id=(px, py, pc), device_id_type=pl.DeviceIdType.MESH)
                remote.start()
                copies.append(remote)
        for cp in copies:
            cp.wait()

    any_spec = pl.BlockSpec(memory_space=pl.ANY)
    out_shape = [jax.ShapeDtypeStruct(((N_DEV,) + a.shape) if gather else a.shape, a.dtype) for a in arrays]
    return pl.pallas_call(
        body, name=name, in_specs=[any_spec] * n, out_specs=[any_spec] * n, out_shape=out_shape,
        scratch_shapes=[pltpu.SemaphoreType.DMA((n, N_DEV - 1)), pltpu.SemaphoreType.DMA((n, N_DEV - 1)),
                        pltpu.SemaphoreType.DMA((n,))],
    )(*arrays)


def adamw(name, w, parts, m, v, tile):
    rows, cols = w.shape
    spec = pl.BlockSpec((tile, cols), lambda i: (i, 0))
    part_spec = pl.BlockSpec((N_DEV, tile, cols), lambda i: (0, i, 0))

    def body(w_ref, p_ref, m_ref, v_ref, g_ref, d_ref, nm_ref, nv_ref):
        g = p_ref[0].astype(F32)
        for s in range(1, N_DEV):
            g = g + p_ref[s].astype(F32)
        new_m = ADAM_B1 * m_ref[...] + (1.0 - ADAM_B1) * g
        new_v = ADAM_B2 * v_ref[...] + (1.0 - ADAM_B2) * (g * g)
        m_hat = new_m / (1.0 - ADAM_B1 ** ADAM_STEP)
        v_hat = new_v / (1.0 - ADAM_B2 ** ADAM_STEP)
        g_ref[...] = g
        d_ref[...] = -ADAM_LR * (m_hat / (jnp.sqrt(v_hat) + ADAM_EPS) + ADAM_WD * w_ref[...])
        nm_ref[...] = new_m
        nv_ref[...] = new_v

    shape = jax.ShapeDtypeStruct((rows, cols), F32)
    return pl.pallas_call(
        body, name=name, grid=(rows // tile,), in_specs=[spec, part_spec, spec, spec],
        out_specs=[spec] * 4, out_shape=[shape] * 4, compiler_params=_params(("arbitrary",)),
    )(w, parts, m, v)


SMALL = ("ln1_g", "tok_mu", "w0", "a0", "k_k", "k_a", "r_k", "gn_w", "gn_b", "sb_gain", "ln2_g", "lnf_g")
BIG = ("w_in", "w_decay_up", "w_aaa_up", "w_gate_up", "w_out", "w_up", "w_down")
COL_SHARDED = ("w_in", "w_decay_up", "w_aaa_up", "w_gate_up", "w_up")
ORDER = ("ln1_g", "w_in", "tok_mu", "w0", "w_decay_up", "a0", "w_aaa_up", "w_gate_up", "k_k", "k_a", "r_k",
         "gn_w", "gn_b", "sb_gain", "w_out", "ln2_g", "w_up", "w_down", "lnf_g")


def _pack(vectors, rows):
    flat = jnp.concatenate([v.reshape(-1).astype(F32) for v in vectors])
    return jnp.pad(flat, (0, rows * LANES - flat.shape[0])).reshape(rows, LANES)


def _full_cols(gathered):
    d, k, cols = gathered.shape
    return jnp.transpose(gathered, (1, 0, 2)).reshape(k, d * cols)


def _col_parts(full):
    k, n = full.shape
    return jnp.transpose(full.reshape(k, N_DEV, n // N_DEV), (1, 0, 2))


def kernel(x, ln1_g, w_in, tok_mu, w0, w_decay_up, a0, w_aaa_up, w_gate_up, k_k, k_a, r_k, gn_w, gn_b, sb_gain, w_out, ln2_g, w_up, w_down, lnf_g, loss_target, m_ln1_g, m_w_in, m_tok_mu, m_w0, m_w_decay_up, m_a0, m_w_aaa_up, m_w_gate_up, m_k_k, m_k_a, m_r_k, m_gn_w, m_gn_b, m_sb_gain, m_w_out, m_ln2_g, m_w_up, m_w_down, m_lnf_g, v_ln1_g, v_w_in, v_tok_mu, v_w0, v_w_decay_up, v_a0, v_w_aaa_up, v_w_gate_up, v_k_k, v_k_a, v_r_k, v_gn_w, v_gn_b, v_sb_gain, v_w_out, v_ln2_g, v_w_up, v_w_down, v_lnf_g):
    args = dict(locals())
    weights = {n: args[n] for n in ORDER}
    mom_m = {n: args["m_" + n] for n in ORDER}
    mom_v = {n: args["v_" + n] for n in ORDER}

    n_seq, seq_len, d_model = x.shape
    n_tok = n_seq * seq_len
    x2d = x.reshape(n_tok, d_model)
    tgt = loss_target.reshape(n_tok, d_model)
    row = lambda t: t.reshape(1, -1).astype(F32)

    shards = [weights[n][0].astype(BF16) for n in BIG]
    gathered = dict(zip(BIG, exchange("gather_weights", shards, True)))
    w_in_f = _full_cols(gathered["w_in"])
    zeros64 = jnp.zeros((HEAD_DIM, RWKV_W), BF16)
    wd_pad = jnp.concatenate([_full_cols(gathered["w_decay_up"]), zeros64], axis=0)
    wa_pad = jnp.concatenate([zeros64, _full_cols(gathered["w_aaa_up"])], axis=0)
    wg_f = _full_cols(gathered["w_gate_up"])
    w_out_f = gathered["w_out"].reshape(d_model, d_model)
    w_up_f = _full_cols(gathered["w_up"])
    w_down_f = gathered["w_down"].reshape(-1, d_model)
    d_ff = w_up_f.shape[1]
    in_cols = w_in_f.shape[1]

    g1, mu, w0r, a0r = row(ln1_g), row(tok_mu), row(w0), row(a0)
    kkr, kar, rkr, gwr, gbr, sgr = row(k_k), row(k_a), row(r_k), row(gn_w), row(gn_b), row(sb_gain)
    g2, gf = row(ln2_g), row(lnf_g)

    (h1,), _ = rw_call("norm1", lambda r, p: ([f_norm(r[0], p[0])], []), [x2d], [g1], [(d_model, BF16)], [], 512)
    p = matmul("proj_in", h1, w_in_f, "nn", F32, 512, in_cols // 2, d_model)
    prep_pars = [mu, w0r, wd_pad, a0r, wa_pad, wg_f]
    (r_, kraw, v_, lw, asig, gate), prep_edges = rwkv_prep_fwd(p, prep_pars, seq_len)
    by_seq = lambda t: t.reshape(n_seq, seq_len, t.shape[-1])
    flat = lambda t: t.reshape(n_tok, t.shape[-1])
    rwkv_in = [by_seq(t) for t in (r_, kraw, v_, lw, asig)]
    z, s0_all = rwkv_fwd(*rwkv_in, kkr, kar, rkr, gwr, gbr, n_seq, seq_len)
    z = flat(z)
    y_sb, o_raw, tot, sb_first = sb_fwd(by_seq(p), sgr, n_seq, seq_len)
    y_sb = flat(y_sb)
    (ycat,), _ = rw_call("mix_cat", lambda r, q: ([jnp.concatenate([r[0] * r[1], r[2]], axis=1)], []),
                         [z, gate, y_sb], [], [(d_model, BF16)], [], 512)
    mix = matmul("proj_out", ycat, w_out_f, "nn", F32, 512, d_model, d_model)
    (x2, h2), _ = rw_call("resid_norm2", lambda r, q: ([r[0] + r[1], f_norm(r[0] + r[1], q[0])], []),
                          [x2d, mix], [g2], [(d_model, F32), (d_model, BF16)], [], 512)
    u, act = matmul("mlp_up", h2, w_up_f, "nn", (F32, BF16), 512, d_ff // 4, d_model,
                    epilogue=lambda acc: (acc, jnp.square(jnp.maximum(acc, 0.0))))
    mlp = matmul("mlp_down", act, w_down_f, "nn", F32, 512, d_model, d_ff // 2)

    def final_body(r, q):
        x3 = r[0] + r[1]
        loss_rows, vjp = jax.vjp(lambda a, g: f_final(a, g, r[2]), x3, q[0])
        dx3, dgf = vjp(jnp.ones_like(loss_rows))
        return [dx3, dx3], [jnp.broadcast_to(jnp.sum(loss_rows), (1, LANES)), dgf]

    (dx3, dx3_b), (loss_acc, d_lnf) = rw_call("loss_head", final_body, [x2, mlp, tgt], [gf],
                                              [(d_model, F32), (d_model, BF16)],
                                              [((1, LANES), F32), ((1, d_model), F32)], 256)

    du = matmul("d_act", dx3_b, w_down_f, "nt", BF16, 512, d_ff // 4, d_model, extras=[u],
                epilogue=lambda acc, uv: (acc * (2.0 * jnp.maximum(uv, 0.0)),))
    dw_down = matmul("dw_down", act, dx3_b, "tn", BF16, 512, d_model, 1024)
    dh2 = matmul("d_h2", du, w_up_f, "nt", F32, 512, d_model, d_ff // 2)
    dw_up = matmul("dw_up", h2, du, "tn", BF16, d_model, 512, 1024)

    def norm_bwd(r, q):
        xv, dh, dres = r
        _, vjp = jax.vjp(f_norm, xv, q[0])
        dx, dg = vjp(dh)
        return [dx + dres, dx + dres], [dg]

    (dx2, dx2_b), (d_ln2,) = rw_call("d_norm2", norm_bwd, [x2, dh2, dx3], [g2], [(d_model, F32), (d_model, BF16)],
                                     [((1, d_model), F32)], 256)

    dycat = matmul("d_ycat", dx2_b, w_out_f, "nt", F32, 512, d_model, d_model)
    dw_out = matmul("dw_out", ycat, dx2_b, "tn", BF16, d_model, d_model, 1024)
    (dz, dgate), _ = rw_call("d_mix", lambda r, q: ([r[0] * r[2], r[0] * r[1]], []),
                             [(dycat, RWKV_W, 0), z, gate], [], [(RWKV_W, F32)] * 2, [], 512)
    dq, dk_sb, dv_sb, d_sg = sb_bwd(by_seq(p), sgr, o_raw, tot, by_seq(dycat[:, RWKV_W:]), sb_first, n_seq, seq_len)
    d_sb = [flat(dq), flat(dk_sb), flat(dv_sb)]
    dr, dkraw, dv, dlw, dasig, d_kk, d_ka, d_rk, d_gw, d_gb = rwkv_bwd(
        *rwkv_in, kkr, kar, rkr, gwr, gbr, s0_all, by_seq(dz), n_seq, seq_len)
    prep_cots = [flat(t) for t in (dr, dkraw, dv, dlw, dasig)] + [dgate]
    dp, (d_mu, d_w0, d_wd, d_a0, d_wa, d_wg) = rwkv_prep_bwd(p, prep_edges, prep_pars, prep_cots, d_sb, seq_len)

    dh1 = matmul("d_h1", dp, w_in_f, "nt", F32, 512, d_model, in_cols // 2)
    dw_in = matmul("dw_in", h1, dp, "tn", BF16, d_model, in_cols // 2, 1024)
    (dx,), (d_ln1,) = rw_call("d_norm1", norm_bwd, [x2d, dh1, dx2], [g1], [(d_model, F32)],
                              [((1, d_model), F32)], 256)

    lora_parts = lambda t: _col_parts(t).astype(BF16)
    big_grads = {
        "w_in": _col_parts(dw_in), "w_decay_up": lora_parts(d_wd[:HEAD_DIM]), "w_aaa_up": lora_parts(d_wa[HEAD_DIM:]),
        "w_gate_up": lora_parts(d_wg), "w_out": dw_out.reshape(N_DEV, -1, d_model), "w_up": _col_parts(dw_up),
        "w_down": dw_down.reshape(N_DEV, -1, d_model),
    }
    parts = dict(zip(BIG, exchange("scatter_grads", [big_grads[n] for n in BIG], False)))

    small_grads = {"ln1_g": d_ln1, "tok_mu": d_mu, "w0": d_w0, "a0": d_a0, "k_k": d_kk, "k_a": d_ka, "r_k": d_rk,
                   "gn_w": d_gw, "gn_b": d_gb, "sb_gain": d_sg, "ln2_g": d_ln2, "lnf_g": d_lnf}
    n_small = sum(int(weights[n].size) for n in SMALL)
    pack_rows = -(-(n_small + 1) // (8 * LANES)) * 8
    packed = _pack([small_grads[n] for n in SMALL] + [loss_acc[0, :1]], pack_rows)
    (small_parts,) = exchange("gather_small", [packed], True)

    results = {}
    for n in BIG:
        w2d = weights[n][0]
        tile = w2d.shape[0] if w2d.shape[0] <= 256 else 256
        results[n] = adamw("adamw_" + n, w2d, parts[n], mom_m[n][0], mom_v[n][0], tile)
    pk = lambda d: _pack([d[n] for n in SMALL] + [jnp.zeros((1,), F32)], pack_rows)
    sg, sd, sm, sv = adamw("adamw_small", pk(weights), small_parts, pk(mom_m), pk(mom_v), pack_rows)
    off = 0
    for n in SMALL:
        size = int(weights[n].size)
        results[n] = tuple(t.reshape(-1)[off:off + size] for t in (sg, sd, sm, sv))
        off += size
    loss = sg.reshape(-1)[off]

    out = [loss, dx.reshape(x.shape)]
    for kind in range(4):
        out += [results[n][kind].reshape(weights[n].shape) for n in ORDER]
    return tuple(out)
```

```python
import functools
import math

import jax
import jax.numpy as jnp
from jax import lax
from jax.experimental import pallas as pl
from jax.experimental.pallas import tpu as pltpu

F32 = jnp.float32
BF16 = jnp.bfloat16

N_DEV = 8
HEAD_DIM = 64
LANES = 128
RWKV_W = 512
SB_W = 512
LORA_WA = 128
GATE_LORA = 128
RWKV_COLS = 3 * RWKV_W + LORA_WA + GATE_LORA
RMS_EPS = 1e-5
GN_EPS = 64e-5
CHUNK = 64
QB = 128
SB_SCALE = HEAD_DIM ** -0.5
ADAM_LR, ADAM_B1, ADAM_B2, ADAM_EPS, ADAM_WD, ADAM_STEP = 0.001, 0.9, 0.999, 1e-08, 0.01, 10
VMEM_LIMIT = 56 * 1024 * 1024


_DIMS = {
    "nn": (((1,), (0,)), ((), ())),
    "nt": (((1,), (1,)), ((), ())),
    "tn": (((0,), (0,)), ((), ())),
}


def _pieces(x, n):
    if n == 1:
        return [x.astype(BF16)]
    out, rem = [], x.astype(F32)
    for i in range(n):
        p = rem.astype(BF16)
        out.append(p)
        if i + 1 < n:
            rem = rem - p.astype(F32)
    return out


def _dot(a, b, form, pa, pb):
    pieces_a, pieces_b = _pieces(a, pa), _pieces(b, pb)
    keep = max(pa, pb)
    acc = None
    for i, ai in enumerate(pieces_a):
        for j, bj in enumerate(pieces_b):
            if i + j >= keep:
                continue
            t = lax.dot_general(ai, bj, _DIMS[form], preferred_element_type=F32)
            acc = t if acc is None else acc + t
    return acc


BOTH = (True, True)


@functools.partial(jax.custom_vjp, nondiff_argnums=(2, 3, 4, 5))
def mm(a, b, form, pa, pb, diff=BOTH):
    return _dot(a, b, form, pa, pb)


def _mm_fwd(a, b, form, pa, pb, diff):
    return _dot(a, b, form, pa, pb), (a, b)


def _mm_bwd(form, pa, pb, diff, res, g):
    a, b = res
    pg = max(pa, pb)
    da, db = jnp.zeros_like(a), jnp.zeros_like(b)
    if form == "nn":
        if diff[0]:
            da = mm(g, b, "nt", pg, pb)
        if diff[1]:
            db = mm(a, g, "tn", pa, pg)
    elif form == "nt":
        if diff[0]:
            da = mm(g, b, "nn", pg, pb)
        if diff[1]:
            db = mm(g, a, "tn", pg, pa)
    else:
        if diff[0]:
            da = mm(b, g, "nt", pb, pg)
        if diff[1]:
            db = mm(a, g, "nn", pa, pg)
    return da, db


mm.defvjp(_mm_fwd, _mm_bwd)


def _lane_lo(shape):
    return lax.broadcasted_iota(jnp.int32, shape, len(shape) - 1) < HEAD_DIM


def _segsum(x):
    lo = _lane_lo(x.shape)
    s_lo = jnp.sum(jnp.where(lo, x, 0.0), axis=-1, keepdims=True)
    s_hi = jnp.sum(jnp.where(lo, 0.0, x), axis=-1, keepdims=True)
    return jnp.where(lo, s_lo, s_hi)


def _sigmoid(x):
    return 0.5 * (jnp.tanh(0.5 * x) + 1.0)


def _softplus(x):
    return jnp.maximum(x, 0.0) + jnp.log(1.0 + jnp.exp(-jnp.abs(x)))


def f_norm(x, g):
    return x * lax.rsqrt(jnp.mean(x * x, axis=-1, keepdims=True) + RMS_EPS) * g


def f_prep(p, pprev, mu, w0, wd_pad, a0, wa_pad, wg):
    pm = p + mu * (pprev - p)
    r = pm[:, 0:RWKV_W]
    k = pm[:, RWKV_W:2 * RWKV_W]
    v = pm[:, 2 * RWKV_W:3 * RWKV_W]
    xwa = pm[:, 3 * RWKV_W:3 * RWKV_W + LORA_WA]
    xg = pm[:, 3 * RWKV_W + LORA_WA:RWKV_COLS]
    w = -_softplus(-(w0 + mm(jnp.tanh(xwa), wd_pad, "nn", 1, 1))) - 0.5
    lw = -jnp.exp(w)
    asig = _sigmoid(a0 + mm(xwa, wa_pad, "nn", 1, 1))
    gate = mm(_sigmoid(xg), wg, "nn", 1, 1)
    return r, k, v, lw, asig, gate


def _tri(n, kind):
    row = lax.broadcasted_iota(jnp.int32, (n, n), 0)
    col = lax.broadcasted_iota(jnp.int32, (n, n), 1)
    if kind == "lower_incl":
        return row >= col
    return row > col


def rwkv_chunk(state, r, kraw, v, lw, asig, k_k, k_a, r_k, gn_w, gn_b, hp):
    n = len(r)
    L = r[0].shape[0]
    lo = _lane_lo((1, LANES))
    masks = (lo, jnp.logical_not(lo))
    incl = _tri(L, "lower_incl")
    strict = _tri(L, "strict")
    tri = incl.astype(F32)
    eye = (lax.broadcasted_iota(jnp.int32, (L, L), 0) == lax.broadcasted_iota(jnp.int32, (L, L), 1)).astype(F32)
    kk = [x * k_k for x in kraw]
    kk = [x / jnp.maximum(jnp.sqrt(_segsum(x * x)), 1e-12) for x in kk]
    k = [x * (1.0 + (s - 1.0) * k_a) for x, s in zip(kraw, asig)]
    b = [x * s for x, s in zip(kk, asig)]
    c = [mm(tri, x, "nn", 1, 3, (False, True)) for x in lw]
    at = [-x * jnp.exp(ci - li) for x, ci, li in zip(kk, c, lw)]
    rt = [x * jnp.exp(ci) for x, ci in zip(r, c)]
    einv = [jnp.exp(-ci) for ci in c]
    bt = [x * e for x, e in zip(b, einv)]
    kt = [x * e for x, e in zip(k, einv)]
    inst = [(s, m) for s in range(n) for m in masks]
    at_h = [jnp.where(m, at[s], 0.0) for s, m in inst]
    rt_h = [jnp.where(m, rt[s], 0.0) for s, m in inst]
    a_ab = [jnp.where(strict, mm(x, bt[s], "nt", hp, hp), 0.0) for x, (s, _) in zip(at_h, inst)]
    a_ak = [jnp.where(strict, mm(x, kt[s], "nt", hp, hp), 0.0) for x, (s, _) in zip(at_h, inst)]
    b_rb = [jnp.where(incl, mm(x, bt[s], "nt", hp, hp), 0.0) for x, (s, _) in zip(rt_h, inst)]
    b_rk = [jnp.where(incl, mm(x, kt[s], "nt", hp, hp), 0.0) for x, (s, _) in zip(rt_h, inst)]
    tinv = [eye + x for x in a_ab]
    pw = a_ab
    for _ in range(int(math.log2(L)) - 1):
        pw = [mm(x, x, "nn", hp, hp) for x in pw]
        tinv = [t + mm(t, x, "nn", hp, hp) for t, x in zip(tinv, pw)]
    rhs = [mm(x, state[s], "nt", hp, hp) + mm(y, v[s], "nn", hp, hp) for x, y, (s, _) in zip(at_h, a_ak, inst)]
    u_h = [mm(t, x, "nn", hp, hp) for t, x in zip(tinv, rhs)]
    y_h = [mm(x, state[s], "nt", hp, hp) + mm(m1, u, "nn", hp, hp) + mm(m2, v[s], "nn", hp, hp)
           for x, m1, m2, u, (s, _) in zip(rt_h, b_rb, b_rk, u_h, inst)]
    u_all = [jnp.where(lo, u_h[2 * s], u_h[2 * s + 1]) for s in range(n)]
    y_all = [jnp.where(lo, y_h[2 * s], y_h[2 * s + 1]) for s in range(n)]
    c_last = [jnp.sum(x, axis=0, keepdims=True) for x in lw]
    efwd = [jnp.exp(cl - ci) for cl, ci in zip(c_last, c)]
    new_state = [st * jnp.exp(cl) + mm(u, bi * e, "tn", hp, hp) + mm(vi, ki * e, "tn", hp, hp)
                 for st, cl, u, bi, e, vi, ki in zip(state, c_last, u_all, b, efwd, v, k)]
    row_head = lax.broadcasted_iota(jnp.int32, (LANES, LANES), 0) // HEAD_DIM
    col_head = lax.broadcasted_iota(jnp.int32, (LANES, LANES), 1) // HEAD_DIM
    new_state = [jnp.where(row_head == col_head, x, 0.0) for x in new_state]
    outs = []
    for y, ri, ki, vi in zip(y_all, r, k, v):
        mean = _segsum(y) * (1.0 / HEAD_DIM)
        d = y - mean
        var = _segsum(d * d) * (1.0 / HEAD_DIM)
        yn = d * lax.rsqrt(var + GN_EPS) * gn_w + gn_b
        outs.append(yn + _segsum(ri * ki * r_k) * vi)
    return outs, new_state


def sb_tile(q, k, v, c_lo, c_hi, diag, from_here=None):
    n = len(q)
    lo = _lane_lo((1, LANES))
    below = _tri(QB, "strict")
    from_s = _tri(QB, "lower_incl").astype(F32)
    inst = [(s, h) for s in range(n) for h in (0, 1)]
    carry = [(c_lo[s], c_hi[s])[h] for s, h in inst]
    z = [mm(q[s][h], k[s], "nt", 1, 1) for s, h in inst]
    soft = [jnp.log(1.0 + jnp.exp(-jnp.abs(x))) for x in z]
    log_keep = [-jnp.maximum(x, 0.0) - sp for x, sp in zip(z, soft)]
    if diag:
        log_keep = [jnp.where(below, x, 0.0) for x in log_keep]
    own = [jnp.sum(x, axis=1, keepdims=True) for x in log_keep]
    if from_here is not None:
        carry = [lax.stop_gradient(from_here[s][h] - o) + cr for (s, h), o, cr in zip(inst, own, carry)]
    tail = [mm(x, from_s, "nn", SB_SUM_PIECES, 1, (True, False)) for x in log_keep]
    log_a = [x + tl + cr for x, tl, cr in zip(z, tail, carry)]
    if diag:
        log_a = [jnp.where(below, x, -1e30) for x in log_a]
    att = [jnp.exp(x) for x in log_a]
    out_h = [mm(x, v[s], "nn", 1, 1) for x, (s, _) in zip(att, inst)]
    out = [jnp.where(lo, out_h[2 * s], out_h[2 * s + 1]) for s in range(n)]
    return out, [own[2 * s] for s in range(n)], [own[2 * s + 1] for s in range(n)]


def sb_split_q(q):
    lo = _lane_lo((1, LANES))
    qs = q * SB_SCALE
    return jnp.where(lo, qs, 0.0), jnp.where(lo, 0.0, qs)


def sb_post(o, gain):
    return o * lax.rsqrt(_segsum(o * o) * (1.0 / HEAD_DIM) + RMS_EPS) * gain


def f_final(x3, g, target):
    y = f_norm(x3, g)
    err = y - target
    return 0.5 * jnp.mean(err * err, axis=-1, keepdims=True)


def _params(sem):
    return pltpu.CompilerParams(dimension_semantics=sem, vmem_limit_bytes=VMEM_LIMIT)


def rw_call(name, body_fn, rows, pars, out_rows, out_accs, tile):
    rows = [item if isinstance(item, tuple) else (item, item.shape[1], 0) for item in rows]
    row_arrays = [arr for arr, _, _ in rows]
    n_rows = row_arrays[0].shape[0]
    tile = min(tile, n_rows)
    steps = n_rows // tile
    row_specs = [pl.BlockSpec((tile, cols), functools.partial(lambda i, c: (i, c), c=cblk)) for _, cols, cblk in rows]
    par_specs = [pl.BlockSpec(p.shape, lambda i: (0, 0)) for p in pars]
    nr, npar, nor, noa = len(row_arrays), len(pars), len(out_rows), len(out_accs)

    def body(*refs):
        row_vals = [refs[i][...] for i in range(nr)]
        par_vals = [refs[nr + i][...] for i in range(npar)]
        o_refs = refs[nr + npar:nr + npar + nor]
        a_refs = refs[nr + npar + nor:]
        row_outs, acc_outs = body_fn(row_vals, par_vals)
        for ref, val in zip(o_refs, row_outs):
            ref[...] = val.astype(ref.dtype)
        if noa:
            first = pl.program_id(0) == 0

            @pl.when(first)
            def _():
                for ref, val in zip(a_refs, acc_outs):
                    ref[...] = val.astype(ref.dtype)

            @pl.when(jnp.logical_not(first))
            def _():
                for ref, val in zip(a_refs, acc_outs):
                    ref[...] = ref[...] + val.astype(ref.dtype)

    out_shape = [jax.ShapeDtypeStruct((n_rows, c), dt) for c, dt in out_rows]
    out_shape += [jax.ShapeDtypeStruct(s, dt) for s, dt in out_accs]
    out_specs = [pl.BlockSpec((tile, c), lambda i: (i, 0)) for c, _ in out_rows]
    out_specs += [pl.BlockSpec(s, lambda i: (0, 0)) for s, _ in out_accs]
    outs = pl.pallas_call(
        body, name=name, grid=(steps,), in_specs=row_specs + par_specs, out_specs=out_specs,
        out_shape=out_shape, compiler_params=_params(("arbitrary",)),
    )(*row_arrays, *pars)
    return outs[:nor], outs[nor:]


def matmul(name, a, b, form, out_dtype, tm, tn, tk, extras=(), epilogue=None):
    out_dtypes = out_dtype if isinstance(out_dtype, tuple) else (out_dtype,)
    tm, tn, tk = min(tm, a.shape[1 if form == "tn" else 0]), min(tn, b.shape[0 if form == "nt" else 1]), min(tk, a.shape[0 if form == "tn" else 1])
    if form == "nn":
        (m, kd), n = a.shape, b.shape[1]
        a_spec = pl.BlockSpec((tm, tk), lambda i, j, k: (i, k))
        b_spec = pl.BlockSpec((tk, tn), lambda i, j, k: (k, j))
    elif form == "nt":
        (m, kd), n = a.shape, b.shape[0]
        a_spec = pl.BlockSpec((tm, tk), lambda i, j, k: (i, k))
        b_spec = pl.BlockSpec((tn, tk), lambda i, j, k: (j, k))
    else:
        (kd, m), n = a.shape, b.shape[1]
        a_spec = pl.BlockSpec((tk, tm), lambda i, j, k: (k, i))
        b_spec = pl.BlockSpec((tk, tn), lambda i, j, k: (k, j))
    ksteps = kd // tk

    n_extra, n_out = len(extras), len(out_dtypes)

    def body(a_ref, b_ref, *rest):
        e_refs, o_refs = rest[:n_extra], rest[n_extra:n_extra + n_out]
        kstep = pl.program_id(2)
        part = lax.dot_general(a_ref[...].astype(BF16), b_ref[...].astype(BF16), _DIMS[form],
                               preferred_element_type=F32)

        def finish(acc):
            outs = epilogue(acc, *[r[...] for r in e_refs]) if epilogue else (acc,)
            for ref, val in zip(o_refs, outs):
                ref[...] = val.astype(ref.dtype)

        if ksteps == 1:
            finish(part)
            return
        acc_ref = rest[-1]

        @pl.when(kstep == 0)
        def _():
            acc_ref[...] = part

        @pl.when(jnp.logical_and(kstep > 0, kstep < ksteps - 1))
        def _():
            acc_ref[...] = acc_ref[...] + part

        @pl.when(kstep == ksteps - 1)
        def _():
            finish(acc_ref[...] + part)

    out_spec = pl.BlockSpec((tm, tn), lambda i, j, k: (i, j))
    outs = pl.pallas_call(
        body, name=name, grid=(m // tm, n // tn, ksteps), in_specs=[a_spec, b_spec] + [out_spec] * n_extra,
        out_specs=[out_spec] * n_out,
        out_shape=[jax.ShapeDtypeStruct((m, n), dt) for dt in out_dtypes],
        scratch_shapes=[pltpu.VMEM((tm, tn), F32)] if ksteps > 1 else [],
        compiler_params=_params(("parallel", "parallel", "arbitrary")),
    )(a, b, *extras)
    return outs if isinstance(out_dtype, tuple) else outs[0]


PREP_TILE = 256
PREP_TILE_BWD = 128
SUBLANES = 8


def _shift_in(rows, first):
    rolled = pltpu.roll(rows, 1, 0)
    row = lax.broadcasted_iota(jnp.int32, (SUBLANES, rows.shape[1]), 0)
    head = jnp.where(row == 0, first, rolled[0:SUBLANES])
    return jnp.concatenate([head, rolled[SUBLANES:]], axis=0), rolled


def rwkv_prep_fwd(p, pars, seq_len):
    n_tok = p.shape[0]
    tile = min(PREP_TILE, seq_len)
    tile_b = min(PREP_TILE_BWD, tile)
    steps, per_seq, sub = n_tok // tile, seq_len // tile, tile // tile_b
    n_par = len(pars)

    def body(p_ref, *rest):
        par_refs, out_refs, edge_ref, last8 = rest[:n_par], rest[n_par:n_par + 6], rest[n_par + 6], rest[n_par + 7]
        step = pl.program_id(0)

        @pl.when(step == 0)
        def _():
            last8[...] = jnp.zeros_like(last8)

        rows = p_ref[...]
        before = jnp.where(step % per_seq == 0, 0.0, pltpu.roll(last8[...], 1, 0))
        prev, rolled = _shift_in(rows, before)
        edge_ref[0] = prev[0:SUBLANES]
        for m in range(1, sub):
            edge_ref[m] = rolled[m * tile_b:m * tile_b + SUBLANES]
        last8[...] = rows[tile - SUBLANES:tile]
        for ref, val in zip(out_refs, f_prep(rows, prev, *[r[...] for r in par_refs])):
            ref[...] = val

    row_out = pl.BlockSpec((tile, RWKV_W), lambda i: (i, 0))
    outs = pl.pallas_call(
        body, name="rwkv_prep", grid=(steps,),
        in_specs=[pl.BlockSpec((tile, RWKV_COLS), lambda i: (i, 0))] + [pl.BlockSpec(t.shape, lambda i: (0, 0)) for t in pars],
        out_specs=[row_out] * 6 + [pl.BlockSpec((sub, SUBLANES, RWKV_COLS), lambda i: (i, 0, 0))],
        out_shape=[jax.ShapeDtypeStruct((n_tok, RWKV_W), F32)] * 6
        + [jax.ShapeDtypeStruct((steps * sub, SUBLANES, RWKV_COLS), F32)],
        scratch_shapes=[pltpu.VMEM((SUBLANES, RWKV_COLS), F32)],
        compiler_params=_params(("arbitrary",)),
    )(p, *pars)
    return outs[:6], outs[6]


def rwkv_prep_bwd(p, edges, pars, cots, d_sb, seq_len):
    n_tok = p.shape[0]
    tile = min(PREP_TILE_BWD, seq_len)
    steps, per_seq = n_tok // tile, seq_len // tile
    n_par = len(pars)
    back = lambda i: steps - 1 - i

    def body(p_ref, edge_ref, *rest):
        par_refs, rest = rest[:n_par], rest[n_par:]
        cot_refs, sb_refs, dp_ref, acc_refs, next8 = rest[:6], rest[6:9], rest[9], rest[10:10 + n_par], rest[10 + n_par]
        step = pl.program_id(0)
        first = step == 0

        @pl.when(first)
        def _():
            next8[...] = jnp.zeros_like(next8)

        rows = p_ref[...]
        prev, _ = _shift_in(rows, edge_ref[0])
        _, vjp = jax.vjp(f_prep, rows, prev, *[r[...].astype(F32) for r in par_refs])
        grads = vjp(tuple(r[...] for r in cot_refs))
        d_rows, d_prev = grads[0], grads[1]
        up = pltpu.roll(d_prev, tile - 1, 0)
        ends_seq = back(step) % per_seq == per_seq - 1
        after = jnp.where(ends_seq, 0.0, pltpu.roll(next8[...], SUBLANES - 1, 0))
        row = lax.broadcasted_iota(jnp.int32, (SUBLANES, RWKV_COLS), 0)
        tail = jnp.where(row == SUBLANES - 1, after, up[tile - SUBLANES:tile])
        d_rows = d_rows + jnp.concatenate([up[:tile - SUBLANES], tail], axis=0)
        next8[...] = d_prev[0:SUBLANES]
        dp_ref[...] = jnp.concatenate([d_rows] + [r[...] for r in sb_refs], axis=1).astype(dp_ref.dtype)

        @pl.when(first)
        def _():
            for ref, val in zip(acc_refs, grads[2:]):
                ref[...] = val

        @pl.when(jnp.logical_not(first))
        def _():
            for ref, val in zip(acc_refs, grads[2:]):
                ref[...] = ref[...] + val

    cols = RWKV_COLS + sum(t.shape[1] for t in d_sb)
    half = pl.BlockSpec((tile, RWKV_W), lambda i: (back(i), 0))
    par_specs = [pl.BlockSpec(t.shape, lambda i: (0, 0)) for t in pars]
    outs = pl.pallas_call(
        body, name="d_rwkv_prep", grid=(steps,),
        in_specs=[pl.BlockSpec((tile, RWKV_COLS), lambda i: (back(i), 0)),
                  pl.BlockSpec((1, SUBLANES, RWKV_COLS), lambda i: (back(i), 0, 0))] + par_specs + [half] * 9,
        out_specs=[pl.BlockSpec((tile, cols), lambda i: (back(i), 0))] + par_specs,
        out_shape=[jax.ShapeDtypeStruct((n_tok, cols), BF16)] + [jax.ShapeDtypeStruct(t.shape, F32) for t in pars],
        scratch_shapes=[pltpu.VMEM((SUBLANES, RWKV_COLS), F32)],
        compiler_params=_params(("arbitrary",)),
    )(p, edges, *pars, *cots, *d_sb)
    return outs[0], outs[1:]


RWKV_HP = 1


class Hosted:
    def __init__(self, arrays, gather):
        self.arrays, self.gather, self.n = list(arrays), gather, len(arrays)

    def split(self, refs, n_in, n_out):
        n = self.n
        ins, outs, sems = refs[n_in:n_in + n], refs[n_in + n + n_out:n_in + 2 * n + n_out], refs[-3:]
        own = refs[:n_in] + refs[n_in + n:n_in + n + n_out] + refs[n_in + 2 * n + n_out:-3]
        return own, exchange_copies(ins, outs, *sems, self.gather)

    def run(self, copies, first, last):
        @pl.when(first)
        def _():
            for cp in copies:
                cp.start()

        @pl.when(last)
        def _():
            for cp in copies:
                cp.wait()


def rwkv_fwd(r, kraw, v, lw, asig, k_k, k_a, r_k, gn_w, gn_b, n_seq, seq_len, hosted):
    n_chunks = seq_len // CHUNK
    n_pairs = RWKV_W // LANES
    row = pl.BlockSpec((n_seq, CHUNK, LANES), lambda h, c: (0, c, h))
    par = pl.BlockSpec((1, LANES), lambda h, c: (0, h))
    s0_spec = pl.BlockSpec((1, 1, n_seq, LANES, LANES), lambda h, c: (h, c, 0, 0, 0))

    def body(*refs):
        own, copies = hosted.split(refs, 10, 2)
        r_ref, k_ref, v_ref, lw_ref, a_ref, kk_ref, ka_ref, rk_ref, gw_ref, gb_ref, z_ref, s0_ref, state = own
        step = pl.program_id(0) * n_chunks + pl.program_id(1)
        hosted.run(copies, step == 0, step == n_pairs * n_chunks - 1)

        @pl.when(pl.program_id(1) == 0)
        def _():
            state[...] = jnp.zeros_like(state)

        pars = [ref[...] for ref in (kk_ref, ka_ref, rk_ref, gw_ref, gb_ref)]
        seqs = range(n_seq)
        s0 = [state[s] for s in seqs]
        rows = [[ref[s] for s in seqs] for ref in (r_ref, k_ref, v_ref, lw_ref, a_ref)]
        z, s1 = rwkv_chunk(s0, *rows, *pars, RWKV_HP)
        for s in seqs:
            s0_ref[0, 0, s] = s0[s]
            z_ref[s] = z[s]
            state[s] = s1[s]

    outs = pl.pallas_call(
        body, name="rwkv_fwd", grid=(n_pairs, n_chunks),
        in_specs=[row] * 5 + [par] * 5 + [ANY_SPEC] * hosted.n, out_specs=[row, s0_spec] + [ANY_SPEC] * hosted.n,
        out_shape=[jax.ShapeDtypeStruct(r.shape, F32),
                   jax.ShapeDtypeStruct((n_pairs, n_chunks, n_seq, LANES, LANES), F32)]
        + exchange_shapes(hosted.arrays, hosted.gather),
        scratch_shapes=[pltpu.VMEM((n_seq, LANES, LANES), F32)] + exchange_sems(hosted.n),
        compiler_params=_params(("arbitrary", "arbitrary")),
    )(r, kraw, v, lw, asig, k_k, k_a, r_k, gn_w, gn_b, *hosted.arrays)
    return outs[0], outs[1], outs[2:]


def rwkv_bwd(r, kraw, v, lw, asig, k_k, k_a, r_k, gn_w, gn_b, s0_all, dz, n_seq, seq_len, hosted):
    n_chunks = seq_len // CHUNK
    n_pairs = RWKV_W // LANES
    row = pl.BlockSpec((n_seq, CHUNK, LANES), lambda h, c: (0, n_chunks - 1 - c, h))
    par = pl.BlockSpec((1, LANES), lambda h, c: (0, h))
    s0_spec = pl.BlockSpec((1, 1, n_seq, LANES, LANES), lambda h, c: (h, n_chunks - 1 - c, 0, 0, 0))

    def body(*refs):
        own, copies = hosted.split(refs, 12, 10)
        (r_ref, k_ref, v_ref, lw_ref, a_ref, kk_ref, ka_ref, rk_ref, gw_ref, gb_ref, s0_ref, dz_ref,
         dr_ref, dk_ref, dv_ref, dlw_ref, da_ref, dkk_ref, dka_ref, drk_ref, dgw_ref, dgb_ref, dstate) = own
        step = pl.program_id(0) * n_chunks + pl.program_id(1)
        hosted.run(copies, step == 0, step == n_pairs * n_chunks - 1)
        first = pl.program_id(1) == 0

        @pl.when(first)
        def _():
            dstate[...] = jnp.zeros_like(dstate)

        pars = [ref[...] for ref in (kk_ref, ka_ref, rk_ref, gw_ref, gb_ref)]
        fn = functools.partial(rwkv_chunk, hp=RWKV_HP)
        seqs = range(n_seq)
        rows = [[ref[s] for s in seqs] for ref in (r_ref, k_ref, v_ref, lw_ref, a_ref)]
        _, vjp = jax.vjp(fn, [s0_ref[0, 0, s] for s in seqs], *rows, *pars)
        grads = vjp(([dz_ref[s] for s in seqs], [dstate[s] for s in seqs]))
        for s in seqs:
            dstate[s] = grads[0][s]
            for ref, val in zip((dr_ref, dk_ref, dv_ref, dlw_ref, da_ref), grads[1:6]):
                ref[s] = val[s]
        accs = tuple(zip((dkk_ref, dka_ref, drk_ref, dgw_ref, dgb_ref), grads[6:]))

        @pl.when(first)
        def _():
            for ref, val in accs:
                ref[...] = val

        @pl.when(jnp.logical_not(first))
        def _():
            for ref, val in accs:
                ref[...] = ref[...] + val

    rows_shape = jax.ShapeDtypeStruct(r.shape, F32)
    par_shape = jax.ShapeDtypeStruct((1, RWKV_W), F32)
    outs = pl.pallas_call(
        body, name="rwkv_bwd", grid=(n_pairs, n_chunks),
        in_specs=[row] * 5 + [par] * 5 + [s0_spec, row] + [ANY_SPEC] * hosted.n,
        out_specs=[row] * 5 + [par] * 5 + [ANY_SPEC] * hosted.n,
        out_shape=[rows_shape] * 5 + [par_shape] * 5 + exchange_shapes(hosted.arrays, hosted.gather),
        scratch_shapes=[pltpu.VMEM((n_seq, LANES, LANES), F32)] + exchange_sems(hosted.n),
        compiler_params=_params(("arbitrary", "arbitrary")),
    )(r, kraw, v, lw, asig, k_k, k_a, r_k, gn_w, gn_b, s0_all, dz, *hosted.arrays)
    return outs[:10], outs[10:]


SB_Q0 = RWKV_COLS // LANES
SB_K0 = SB_Q0 + SB_W // LANES
SB_V0 = SB_K0 + SB_W // LANES
SB_SEQS = 2
SB_SUM_PIECES = 2
SB_DEAD = -110.0


def _col_of(c_lo, c_hi):
    return jnp.where(_lane_lo((1, LANES)), c_lo, c_hi)


def sb_fwd(p, gain, n_seq, seq_len):
    n_pairs = SB_W // LANES
    n_q = seq_len // QB
    nb = min(SB_SEQS, n_seq)

    def seq_spec(c0):
        return pl.BlockSpec((nb, seq_len, LANES), functools.partial(lambda b, h, c0: (b, 0, c0 + h), c0=c0))

    out_spec = pl.BlockSpec((nb, seq_len, LANES), lambda b, h: (b, 0, h))

    def body(q_ref, k_ref, v_ref, g_ref, y_ref, o_ref, tot_ref, first_ref):
        gain = g_ref[...]

        def q_block(i, _):
            qs = pl.multiple_of(i * QB, QB)
            seqs = range(nb)
            zeros = [jnp.zeros((QB, 1), F32)] * nb
            qv = [sb_split_q(q_ref[s, pl.ds(qs, QB), :]) for s in seqs]
            add = lambda xs, ys: [x + y for x, y in zip(xs, ys)]

            def tiles(ks, c_lo, c_hi, diag):
                return sb_tile(qv, [k_ref[s, pl.ds(ks, QB), :] for s in seqs],
                               [v_ref[s, pl.ds(ks, QB), :] for s in seqs], c_lo, c_hi, diag)

            def alive(c_lo, c_hi):
                top = functools.reduce(jnp.maximum, list(c_lo) + list(c_hi))
                return jnp.max(top) > SB_DEAD

            def k_block(state):
                j, _, (o, c_lo, c_hi) = state
                o2, s_lo, s_hi = tiles(pl.multiple_of(j * QB, QB), c_lo, c_hi, False)
                c_lo, c_hi = add(c_lo, s_lo), add(c_hi, s_hi)
                return j - 1, alive(c_lo, c_hi), (add(o, o2), c_lo, c_hi)

            o, c_lo, c_hi = tiles(qs, zeros, zeros, True)
            j, _, (o, c_lo, c_hi) = lax.while_loop(lambda st: jnp.logical_and(st[0] >= 0, st[1]), k_block,
                                                   (i - 1, alive(c_lo, c_hi), (o, c_lo, c_hi)))
            first_ref[pl.program_id(0), pl.program_id(1), i] = j + 1
            for s in seqs:
                o_ref[s, pl.ds(qs, QB), :] = o[s]
                tot_ref[s, pl.ds(qs, QB), :] = jnp.broadcast_to(_col_of(c_lo[s], c_hi[s]), (QB, LANES))
                y_ref[s, pl.ds(qs, QB), :] = sb_post(o[s], gain)
            return 0

        lax.fori_loop(0, n_q, q_block, 0)

    shape = jax.ShapeDtypeStruct((n_seq, seq_len, SB_W), F32)
    return pl.pallas_call(
        body, name="sb_fwd", grid=(n_seq // nb, n_pairs),
        in_specs=[seq_spec(SB_Q0), seq_spec(SB_K0), seq_spec(SB_V0), pl.BlockSpec((1, LANES), lambda b, h: (0, h))],
        out_specs=[out_spec] * 3 + [pl.BlockSpec(memory_space=pltpu.SMEM)],
        out_shape=[shape] * 3 + [jax.ShapeDtypeStruct((n_seq // nb, n_pairs, n_q), jnp.int32)],
        compiler_params=_params(("arbitrary", "arbitrary")),
    )(p, p, p, gain)


def sb_bwd(p, gain, o_raw, tot, dy, first, n_seq, seq_len):
    n_pairs = SB_W // LANES
    n_q = seq_len // QB
    nb = min(SB_SEQS, n_seq)

    def seq_spec(c0):
        return pl.BlockSpec((nb, seq_len, LANES), functools.partial(lambda h, b, c0: (b, 0, c0 + h), c0=c0))

    own = pl.BlockSpec((nb, seq_len, LANES), lambda h, b: (b, 0, h))
    par = pl.BlockSpec((1, LANES), lambda h, b: (0, h))

    def body(q_ref, k_ref, v_ref, g_ref, o_ref, tot_ref, dy_ref, first_ref, dq_ref, dk_ref, dv_ref, dg_ref):
        gain = g_ref[...]
        lo = _lane_lo((1, LANES))
        dk_ref[...] = jnp.zeros_like(dk_ref)
        dv_ref[...] = jnp.zeros_like(dv_ref)

        def q_block(i, dgain):
            qs = pl.multiple_of(i * QB, QB)
            seqs = range(nb)
            zeros = [jnp.zeros((QB, 1), F32)] * nb
            qv, dov, t_lo, t_hi = [], [], [], []
            for s in seqs:
                qv.append(sb_split_q(q_ref[s, pl.ds(qs, QB), :]))
                _, post_vjp = jax.vjp(sb_post, o_ref[s, pl.ds(qs, QB), :], gain)
                do, dg_s = post_vjp(dy_ref[s, pl.ds(qs, QB), :])
                dov.append(do)
                dgain = dgain + dg_s
                tot_s = tot_ref[s, pl.ds(qs, QB), :]
                t_lo.append(jnp.max(jnp.where(lo, tot_s, -jnp.inf), axis=1, keepdims=True))
                t_hi.append(jnp.max(jnp.where(lo, -jnp.inf, tot_s), axis=1, keepdims=True))
            add = lambda xs, ys: [x + y for x, y in zip(xs, ys)]
            sub = lambda xs, ys: [x - y for x, y in zip(xs, ys)]

            def tile(ks, carry, diag):
                dq, rem_lo, rem_hi, g_lo, g_hi = carry
                kv = [k_ref[s, pl.ds(ks, QB), :] for s in seqs]
                vv = [v_ref[s, pl.ds(ks, QB), :] for s in seqs]
                fn = functools.partial(sb_tile, diag=diag, from_here=list(zip(rem_lo, rem_hi)))
                (_, s_lo, s_hi), vjp = jax.vjp(fn, qv, kv, vv, zeros, zeros)
                dq_t, dk_t, dv_t, dc_lo, dc_hi = vjp((dov, g_lo, g_hi))
                dq_t = [jnp.where(lo, d_lo, d_hi) for d_lo, d_hi in dq_t]
                for s in seqs:
                    dk_ref[s, pl.ds(ks, QB), :] = dk_ref[s, pl.ds(ks, QB), :] + dk_t[s]
                    dv_ref[s, pl.ds(ks, QB), :] = dv_ref[s, pl.ds(ks, QB), :] + dv_t[s]
                return add(dq, dq_t), sub(rem_lo, s_lo), sub(rem_hi, s_hi), add(g_lo, dc_lo), add(g_hi, dc_hi)

            def k_block(j, carry):
                return tile(pl.multiple_of(j * QB, QB), carry, False)

            carry = ([jnp.zeros((QB, LANES), F32)] * nb, t_lo, t_hi, zeros, zeros)
            carry = lax.fori_loop(first_ref[pl.program_id(1), pl.program_id(0), i], i, k_block, carry)
            carry = tile(qs, carry, True)
            for s in seqs:
                dq_ref[s, pl.ds(qs, QB), :] = carry[0][s] * SB_SCALE
            return dgain

        dgain = lax.fori_loop(0, n_q, q_block, jnp.zeros((1, LANES), F32))
        first = pl.program_id(1) == 0

        @pl.when(first)
        def _():
            dg_ref[...] = dgain

        @pl.when(jnp.logical_not(first))
        def _():
            dg_ref[...] = dg_ref[...] + dgain

    shape = jax.ShapeDtypeStruct((n_seq, seq_len, SB_W), F32)
    return pl.pallas_call(
        body, name="sb_bwd", grid=(n_pairs, n_seq // nb),
        in_specs=[seq_spec(SB_Q0), seq_spec(SB_K0), seq_spec(SB_V0), par, own, own, own,
                  pl.BlockSpec(memory_space=pltpu.SMEM)],
        out_specs=[own, own, own, par],
        out_shape=[shape, shape, shape, jax.ShapeDtypeStruct((1, SB_W), F32)],
        compiler_params=_params(("arbitrary", "arbitrary")),
    )(p, p, p, gain, o_raw, tot, dy, first)


def exchange(name, arrays, gather):
    n = len(arrays)

    def body(*refs):
        copies = exchange_copies(refs[:n], refs[n:2 * n], *refs[2 * n:], gather)
        for cp in copies:
            cp.start()
        for cp in copies:
            cp.wait()

    return pl.pallas_call(
        body, name=name, in_specs=[ANY_SPEC] * n, out_specs=[ANY_SPEC] * n, out_shape=exchange_shapes(arrays, gather),
        scratch_shapes=exchange_sems(n),
    )(*arrays)


ANY_SPEC = pl.BlockSpec(memory_space=pl.ANY)


def exchange_shapes(arrays, gather):
    return [jax.ShapeDtypeStruct(((N_DEV,) + a.shape) if gather else a.shape, a.dtype) for a in arrays]


def exchange_sems(n):
    return [pltpu.SemaphoreType.DMA((n, N_DEV - 1)), pltpu.SemaphoreType.DMA((n, N_DEV - 1)),
            pltpu.SemaphoreType.DMA((n,))]


def exchange_copies(ins, outs, send_sems, recv_sems, local_sems, gather):
    x, y, c = lax.axis_index("x"), lax.axis_index("y"), lax.axis_index("c")
    me = 4 * x + 2 * y + c
    copies = []
    for a, (src_all, dst_all) in enumerate(zip(ins, outs)):
        own = src_all if gather else src_all.at[me]
        copies.append(pltpu.make_async_copy(own, dst_all.at[me], local_sems.at[a]))
        for j in range(1, N_DEV):
            px, py, pc = (x + (j >> 2)) % 2, (y + ((j >> 1) & 1)) % 2, (c + (j & 1)) % 2
            src = src_all if gather else src_all.at[4 * px + 2 * py + pc]
            copies.append(pltpu.make_async_remote_copy(
                src_ref=src, dst_ref=dst_all.at[me], send_sem=send_sems.at[a, j - 1],
                recv_sem=recv_sems.at[a, j - 1], device_id=(px, py, pc), device_id_type=pl.DeviceIdType.MESH))
    return copies


def adamw(name, w, parts, m, v, tile):
    rows, cols = w.shape
    spec = pl.BlockSpec((tile, cols), lambda i: (i, 0))
    part_spec = pl.BlockSpec((N_DEV, tile, cols), lambda i: (0, i, 0))

    def body(w_ref, p_ref, m_ref, v_ref, g_ref, d_ref, nm_ref, nv_ref):
        g = p_ref[0].astype(F32)
        for s in range(1, N_DEV):
            g = g + p_ref[s].astype(F32)
        new_m = ADAM_B1 * m_ref[...] + (1.0 - ADAM_B1) * g
        new_v = ADAM_B2 * v_ref[...] + (1.0 - ADAM_B2) * (g * g)
        m_hat = new_m / (1.0 - ADAM_B1 ** ADAM_STEP)
        v_hat = new_v / (1.0 - ADAM_B2 ** ADAM_STEP)
        g_ref[...] = g
        d_ref[...] = -ADAM_LR * (m_hat / (jnp.sqrt(v_hat) + ADAM_EPS) + ADAM_WD * w_ref[...])
        nm_ref[...] = new_m
        nv_ref[...] = new_v

    shape = jax.ShapeDtypeStruct((rows, cols), F32)
    return pl.pallas_call(
        body, name=name, grid=(rows // tile,), in_specs=[spec, part_spec, spec, spec],
        out_specs=[spec] * 4, out_shape=[shape] * 4, compiler_params=_params(("arbitrary",)),
    )(w, parts, m, v)


SMALL = ("ln1_g", "tok_mu", "w0", "a0", "k_k", "k_a", "r_k", "gn_w", "gn_b", "sb_gain", "ln2_g", "lnf_g")
EARLY = ("w_in", "w_decay_up", "w_aaa_up", "w_gate_up")
LATE = ("w_out", "w_up", "w_down")
BIG = EARLY + LATE
COL_SHARDED = ("w_in", "w_decay_up", "w_aaa_up", "w_gate_up", "w_up")
ORDER = ("ln1_g", "w_in", "tok_mu", "w0", "w_decay_up", "a0", "w_aaa_up", "w_gate_up", "k_k", "k_a", "r_k",
         "gn_w", "gn_b", "sb_gain", "w_out", "ln2_g", "w_up", "w_down", "lnf_g")


def _pack(vectors, rows):
    flat = jnp.concatenate([v.reshape(-1).astype(F32) for v in vectors])
    return jnp.pad(flat, (0, rows * LANES - flat.shape[0])).reshape(rows, LANES)


def _full_cols(gathered):
    d, k, cols = gathered.shape
    return jnp.transpose(gathered, (1, 0, 2)).reshape(k, d * cols)


def _col_parts(full):
    k, n = full.shape
    return jnp.transpose(full.reshape(k, N_DEV, n // N_DEV), (1, 0, 2))


def kernel(x, ln1_g, w_in, tok_mu, w0, w_decay_up, a0, w_aaa_up, w_gate_up, k_k, k_a, r_k, gn_w, gn_b, sb_gain, w_out, ln2_g, w_up, w_down, lnf_g, loss_target, m_ln1_g, m_w_in, m_tok_mu, m_w0, m_w_decay_up, m_a0, m_w_aaa_up, m_w_gate_up, m_k_k, m_k_a, m_r_k, m_gn_w, m_gn_b, m_sb_gain, m_w_out, m_ln2_g, m_w_up, m_w_down, m_lnf_g, v_ln1_g, v_w_in, v_tok_mu, v_w0, v_w_decay_up, v_a0, v_w_aaa_up, v_w_gate_up, v_k_k, v_k_a, v_r_k, v_gn_w, v_gn_b, v_sb_gain, v_w_out, v_ln2_g, v_w_up, v_w_down, v_lnf_g):
    args = dict(locals())
    weights = {n: args[n] for n in ORDER}
    mom_m = {n: args["m_" + n] for n in ORDER}
    mom_v = {n: args["v_" + n] for n in ORDER}

    n_seq, seq_len, d_model = x.shape
    n_tok = n_seq * seq_len
    x2d = x.reshape(n_tok, d_model)
    tgt = loss_target.reshape(n_tok, d_model)
    row = lambda t: t.reshape(1, -1).astype(F32)

    shard = {n: weights[n][0].astype(BF16) for n in BIG}
    gathered = dict(zip(EARLY, exchange("gather_early", [shard[n] for n in EARLY], True)))
    w_in_f = _full_cols(gathered["w_in"])
    zeros64 = jnp.zeros((HEAD_DIM, RWKV_W), BF16)
    wd_pad = jnp.concatenate([_full_cols(gathered["w_decay_up"]), zeros64], axis=0)
    wa_pad = jnp.concatenate([zeros64, _full_cols(gathered["w_aaa_up"])], axis=0)
    wg_f = _full_cols(gathered["w_gate_up"])
    in_cols = w_in_f.shape[1]

    g1, mu, w0r, a0r = row(ln1_g), row(tok_mu), row(w0), row(a0)
    kkr, kar, rkr, gwr, gbr, sgr = row(k_k), row(k_a), row(r_k), row(gn_w), row(gn_b), row(sb_gain)
    g2, gf = row(ln2_g), row(lnf_g)

    (h1,), _ = rw_call("norm1", lambda r, p: ([f_norm(r[0], p[0])], []), [x2d], [g1], [(d_model, BF16)], [], 512)
    p = matmul("proj_in", h1, w_in_f, "nn", F32, 512, in_cols // 2, d_model)
    prep_pars = [mu, w0r, wd_pad, a0r, wa_pad, wg_f]
    (r_, kraw, v_, lw, asig, gate), prep_edges = rwkv_prep_fwd(p, prep_pars, seq_len)
    by_seq = lambda t: t.reshape(n_seq, seq_len, t.shape[-1])
    flat = lambda t: t.reshape(n_tok, t.shape[-1])
    rwkv_in = [by_seq(t) for t in (r_, kraw, v_, lw, asig)]
    z, s0_all, late = rwkv_fwd(*rwkv_in, kkr, kar, rkr, gwr, gbr, n_seq, seq_len,
                               Hosted([shard[n] for n in LATE], True))
    gathered = dict(zip(LATE, late))
    w_out_f = gathered["w_out"].reshape(d_model, d_model)
    w_up_f = _full_cols(gathered["w_up"])
    w_down_f = gathered["w_down"].reshape(-1, d_model)
    d_ff = w_up_f.shape[1]
    z = flat(z)
    y_sb, o_raw, tot, sb_first = sb_fwd(by_seq(p), sgr, n_seq, seq_len)
    y_sb = flat(y_sb)
    (ycat,), _ = rw_call("mix_cat", lambda r, q: ([jnp.concatenate([r[0] * r[1], r[2]], axis=1)], []),
                         [z, gate, y_sb], [], [(d_model, BF16)], [], 512)
    mix = matmul("proj_out", ycat, w_out_f, "nn", F32, 512, d_model, d_model)
    (x2, h2), _ = rw_call("resid_norm2", lambda r, q: ([r[0] + r[1], f_norm(r[0] + r[1], q[0])], []),
                          [x2d, mix], [g2], [(d_model, F32), (d_model, BF16)], [], 512)
    u, act = matmul("mlp_up", h2, w_up_f, "nn", (F32, BF16), 512, d_ff // 4, d_model,
                    epilogue=lambda acc: (acc, jnp.square(jnp.maximum(acc, 0.0))))
    mlp = matmul("mlp_down", act, w_down_f, "nn", F32, 512, d_model, d_ff // 2)

    def final_body(r, q):
        x3 = r[0] + r[1]
        loss_rows, vjp = jax.vjp(lambda a, g: f_final(a, g, r[2]), x3, q[0])
        dx3, dgf = vjp(jnp.ones_like(loss_rows))
        return [dx3, dx3], [jnp.broadcast_to(jnp.sum(loss_rows), (1, LANES)), dgf]

    (dx3, dx3_b), (loss_acc, d_lnf) = rw_call("loss_head", final_body, [x2, mlp, tgt], [gf],
                                              [(d_model, F32), (d_model, BF16)],
                                              [((1, LANES), F32), ((1, d_model), F32)], 256)

    du = matmul("d_act", dx3_b, w_down_f, "nt", BF16, 512, d_ff // 4, d_model, extras=[u],
                epilogue=lambda acc, uv: (acc * (2.0 * jnp.maximum(uv, 0.0)),))
    dw_down = matmul("dw_down", act, dx3_b, "tn", BF16, 512, d_model, 1024)
    dh2 = matmul("d_h2", du, w_up_f, "nt", F32, 512, d_model, d_ff // 2)
    dw_up = matmul("dw_up", h2, du, "tn", BF16, d_model, 512, 1024)

    def norm_bwd(r, q):
        xv, dh, dres = r
        _, vjp = jax.vjp(f_norm, xv, q[0])
        dx, dg = vjp(dh)
        return [dx + dres, dx + dres], [dg]

    (dx2, dx2_b), (d_ln2,) = rw_call("d_norm2", norm_bwd, [x2, dh2, dx3], [g2], [(d_model, F32), (d_model, BF16)],
                                     [((1, d_model), F32)], 256)

    dycat = matmul("d_ycat", dx2_b, w_out_f, "nt", F32, 512, d_model, d_model)
    dw_out = matmul("dw_out", ycat, dx2_b, "tn", BF16, d_model, d_model, 1024)
    (dz, dgate), _ = rw_call("d_mix", lambda r, q: ([r[0] * r[2], r[0] * r[1]], []),
                             [(dycat, RWKV_W, 0), z, gate], [], [(RWKV_W, F32)] * 2, [], 512)
    dq, dk_sb, dv_sb, d_sg = sb_bwd(by_seq(p), sgr, o_raw, tot, by_seq(dycat[:, RWKV_W:]), sb_first, n_seq, seq_len)
    d_sb = [flat(dq), flat(dk_sb), flat(dv_sb)]
    late_grads = {"w_out": dw_out.reshape(N_DEV, -1, d_model), "w_up": _col_parts(dw_up),
                  "w_down": dw_down.reshape(N_DEV, -1, d_model)}
    (dr, dkraw, dv, dlw, dasig, d_kk, d_ka, d_rk, d_gw, d_gb), late_parts = rwkv_bwd(
        *rwkv_in, kkr, kar, rkr, gwr, gbr, s0_all, by_seq(dz), n_seq, seq_len,
        Hosted([late_grads[n] for n in LATE], False))
    prep_cots = [flat(t) for t in (dr, dkraw, dv, dlw, dasig)] + [dgate]
    dp, (d_mu, d_w0, d_wd, d_a0, d_wa, d_wg) = rwkv_prep_bwd(p, prep_edges, prep_pars, prep_cots, d_sb, seq_len)

    dh1 = matmul("d_h1", dp, w_in_f, "nt", F32, 512, d_model, in_cols // 2)
    dw_in = matmul("dw_in", h1, dp, "tn", BF16, d_model, in_cols // 2, 1024)
    (dx,), (d_ln1,) = rw_call("d_norm1", norm_bwd, [x2d, dh1, dx2], [g1], [(d_model, F32)],
                              [((1, d_model), F32)], 256)

    lora_parts = lambda t: _col_parts(t).astype(BF16)
    early_grads = {"w_in": _col_parts(dw_in), "w_decay_up": lora_parts(d_wd[:HEAD_DIM]),
                   "w_aaa_up": lora_parts(d_wa[HEAD_DIM:]), "w_gate_up": lora_parts(d_wg)}
    parts = dict(zip(EARLY, exchange("scatter_early", [early_grads[n] for n in EARLY], False)))
    parts.update(zip(LATE, late_parts))

    small_grads = {"ln1_g": d_ln1, "tok_mu": d_mu, "w0": d_w0, "a0": d_a0, "k_k": d_kk, "k_a": d_ka, "r_k": d_rk,
                   "gn_w": d_gw, "gn_b": d_gb, "sb_gain": d_sg, "ln2_g": d_ln2, "lnf_g": d_lnf}
    n_small = sum(int(weights[n].size) for n in SMALL)
    pack_rows = -(-(n_small + 1) // (8 * LANES)) * 8
    packed = _pack([small_grads[n] for n in SMALL] + [loss_acc[0, :1]], pack_rows)
    (small_parts,) = exchange("gather_small", [packed], True)

    results = {}
    for n in BIG:
        w2d = weights[n][0]
        tile = w2d.shape[0] if w2d.shape[0] <= 256 else 256
        results[n] = adamw("adamw_" + n, w2d, parts[n], mom_m[n][0], mom_v[n][0], tile)
    pk = lambda d: _pack([d[n] for n in SMALL] + [jnp.zeros((1,), F32)], pack_rows)
    sg, sd, sm, sv = adamw("adamw_small", pk(weights), small_parts, pk(mom_m), pk(mom_v), pack_rows)
    off = 0
    for n in SMALL:
        size = int(weights[n].size)
        results[n] = tuple(t.reshape(-1)[off:off + size] for t in (sg, sd, sm, sv))
        off += size
    loss = sg.reshape(-1)[off]

    out = [loss, dx.reshape(x.shape)]
    for kind in range(4):
        out += [results[n][kind].reshape(weights[n].shape) for n in ORDER]
    return tuple(out)
```

```python
import functools
import math

import jax
import jax.numpy as jnp
from jax import lax
from jax.experimental import pallas as pl
from jax.experimental.pallas import tpu as pltpu

F32 = jnp.float32
BF16 = jnp.bfloat16

N_DEV = 8
HEAD_DIM = 64
LANES = 128
RWKV_W = 512
SB_W = 512
LORA_WA = 128
GATE_LORA = 128
RWKV_COLS = 3 * RWKV_W + LORA_WA + GATE_LORA
RMS_EPS = 1e-5
GN_EPS = 64e-5
CHUNK = 64
QB = 128
SB_SCALE = HEAD_DIM ** -0.5
ADAM_LR, ADAM_B1, ADAM_B2, ADAM_EPS, ADAM_WD, ADAM_STEP = 0.001, 0.9, 0.999, 1e-08, 0.01, 10
VMEM_LIMIT = 56 * 1024 * 1024


_DIMS = {
    "nn": (((1,), (0,)), ((), ())),
    "nt": (((1,), (1,)), ((), ())),
    "tn": (((0,), (0,)), ((), ())),
}


def _pieces(x, n):
    if n == 1:
        return [x.astype(BF16)]
    out, rem = [], x.astype(F32)
    for i in range(n):
        p = rem.astype(BF16)
        out.append(p)
        if i + 1 < n:
            rem = rem - p.astype(F32)
    return out


def _dot(a, b, form, pa, pb):
    pieces_a, pieces_b = _pieces(a, pa), _pieces(b, pb)
    keep = max(pa, pb)
    acc = None
    for i, ai in enumerate(pieces_a):
        for j, bj in enumerate(pieces_b):
            if i + j >= keep:
                continue
            t = lax.dot_general(ai, bj, _DIMS[form], preferred_element_type=F32)
            acc = t if acc is None else acc + t
    return acc


BOTH = (True, True)


@functools.partial(jax.custom_vjp, nondiff_argnums=(2, 3, 4, 5))
def mm(a, b, form, pa, pb, diff=BOTH):
    return _dot(a, b, form, pa, pb)


def _mm_fwd(a, b, form, pa, pb, diff):
    return _dot(a, b, form, pa, pb), (a, b)


def _mm_bwd(form, pa, pb, diff, res, g):
    a, b = res
    pg = max(pa, pb)
    da, db = jnp.zeros_like(a), jnp.zeros_like(b)
    if form == "nn":
        if diff[0]:
            da = mm(g, b, "nt", pg, pb)
        if diff[1]:
            db = mm(a, g, "tn", pa, pg)
    elif form == "nt":
        if diff[0]:
            da = mm(g, b, "nn", pg, pb)
        if diff[1]:
            db = mm(g, a, "tn", pg, pa)
    else:
        if diff[0]:
            da = mm(b, g, "nt", pb, pg)
        if diff[1]:
            db = mm(a, g, "nn", pa, pg)
    return da, db


mm.defvjp(_mm_fwd, _mm_bwd)


def _lane_lo(shape):
    return lax.broadcasted_iota(jnp.int32, shape, len(shape) - 1) < HEAD_DIM


def _segsum(x):
    lo = _lane_lo(x.shape)
    s_lo = jnp.sum(jnp.where(lo, x, 0.0), axis=-1, keepdims=True)
    s_hi = jnp.sum(jnp.where(lo, 0.0, x), axis=-1, keepdims=True)
    return jnp.where(lo, s_lo, s_hi)


def _sigmoid(x):
    return 0.5 * (jnp.tanh(0.5 * x) + 1.0)


def _softplus(x):
    return jnp.maximum(x, 0.0) + jnp.log(1.0 + jnp.exp(-jnp.abs(x)))


def f_norm(x, g):
    return x * lax.rsqrt(jnp.mean(x * x, axis=-1, keepdims=True) + RMS_EPS) * g


def f_prep(p, pprev, mu, w0, wd_pad, a0, wa_pad, wg):
    pm = p + mu * (pprev - p)
    r = pm[:, 0:RWKV_W]
    k = pm[:, RWKV_W:2 * RWKV_W]
    v = pm[:, 2 * RWKV_W:3 * RWKV_W]
    xwa = pm[:, 3 * RWKV_W:3 * RWKV_W + LORA_WA]
    xg = pm[:, 3 * RWKV_W + LORA_WA:RWKV_COLS]
    w = -_softplus(-(w0 + mm(jnp.tanh(xwa), wd_pad, "nn", 1, 1))) - 0.5
    lw = -jnp.exp(w)
    asig = _sigmoid(a0 + mm(xwa, wa_pad, "nn", 1, 1))
    gate = mm(_sigmoid(xg), wg, "nn", 1, 1)
    return r, k, v, lw, asig, gate


def _tri(n, kind):
    row = lax.broadcasted_iota(jnp.int32, (n, n), 0)
    col = lax.broadcasted_iota(jnp.int32, (n, n), 1)
    if kind == "lower_incl":
        return row >= col
    return row > col


def rwkv_chunk(state, r, kraw, v, lw, asig, k_k, k_a, r_k, gn_w, gn_b, hp):
    n = len(r)
    L = r[0].shape[0]
    lo = _lane_lo((1, LANES))
    masks = (lo, jnp.logical_not(lo))
    incl = _tri(L, "lower_incl")
    strict = _tri(L, "strict")
    tri = incl.astype(F32)
    eye = (lax.broadcasted_iota(jnp.int32, (L, L), 0) == lax.broadcasted_iota(jnp.int32, (L, L), 1)).astype(F32)
    kk = [x * k_k for x in kraw]
    kk = [x / jnp.maximum(jnp.sqrt(_segsum(x * x)), 1e-12) for x in kk]
    k = [x * (1.0 + (s - 1.0) * k_a) for x, s in zip(kraw, asig)]
    b = [x * s for x, s in zip(kk, asig)]
    c = [mm(tri, x, "nn", 1, 3, (False, True)) for x in lw]
    at = [-x * jnp.exp(ci - li) for x, ci, li in zip(kk, c, lw)]
    rt = [x * jnp.exp(ci) for x, ci in zip(r, c)]
    einv = [jnp.exp(-ci) for ci in c]
    bt = [x * e for x, e in zip(b, einv)]
    kt = [x * e for x, e in zip(k, einv)]
    inst = [(s, m) for s in range(n) for m in masks]
    at_h = [jnp.where(m, at[s], 0.0) for s, m in inst]
    rt_h = [jnp.where(m, rt[s], 0.0) for s, m in inst]
    a_ab = [jnp.where(strict, mm(x, bt[s], "nt", hp, hp), 0.0) for x, (s, _) in zip(at_h, inst)]
    a_ak = [jnp.where(strict, mm(x, kt[s], "nt", hp, hp), 0.0) for x, (s, _) in zip(at_h, inst)]
    b_rb = [jnp.where(incl, mm(x, bt[s], "nt", hp, hp), 0.0) for x, (s, _) in zip(rt_h, inst)]
    b_rk = [jnp.where(incl, mm(x, kt[s], "nt", hp, hp), 0.0) for x, (s, _) in zip(rt_h, inst)]
    tinv = [eye + x for x in a_ab]
    pw = a_ab
    for _ in range(int(math.log2(L)) - 1):
        pw = [mm(x, x, "nn", hp, hp) for x in pw]
        tinv = [t + mm(t, x, "nn", hp, hp) for t, x in zip(tinv, pw)]
    rhs = [mm(x, state[s], "nt", hp, hp) + mm(y, v[s], "nn", hp, hp) for x, y, (s, _) in zip(at_h, a_ak, inst)]
    u_h = [mm(t, x, "nn", hp, hp) for t, x in zip(tinv, rhs)]
    y_h = [mm(x, state[s], "nt", hp, hp) + mm(m1, u, "nn", hp, hp) + mm(m2, v[s], "nn", hp, hp)
           for x, m1, m2, u, (s, _) in zip(rt_h, b_rb, b_rk, u_h, inst)]
    u_all = [jnp.where(lo, u_h[2 * s], u_h[2 * s + 1]) for s in range(n)]
    y_all = [jnp.where(lo, y_h[2 * s], y_h[2 * s + 1]) for s in range(n)]
    c_last = [jnp.sum(x, axis=0, keepdims=True) for x in lw]
    efwd = [jnp.exp(cl - ci) for cl, ci in zip(c_last, c)]
    new_state = [st * jnp.exp(cl) + mm(u, bi * e, "tn", hp, hp) + mm(vi, ki * e, "tn", hp, hp)
                 for st, cl, u, bi, e, vi, ki in zip(state, c_last, u_all, b, efwd, v, k)]
    row_head = lax.broadcasted_iota(jnp.int32, (LANES, LANES), 0) // HEAD_DIM
    col_head = lax.broadcasted_iota(jnp.int32, (LANES, LANES), 1) // HEAD_DIM
    new_state = [jnp.where(row_head == col_head, x, 0.0) for x in new_state]
    outs = []
    for y, ri, ki, vi in zip(y_all, r, k, v):
        mean = _segsum(y) * (1.0 / HEAD_DIM)
        d = y - mean
        var = _segsum(d * d) * (1.0 / HEAD_DIM)
        yn = d * lax.rsqrt(var + GN_EPS) * gn_w + gn_b
        outs.append(yn + _segsum(ri * ki * r_k) * vi)
    return outs, new_state


def sb_tile(q, k, v, c_lo, c_hi, diag, from_here=None):
    n = len(q)
    lo = _lane_lo((1, LANES))
    below = _tri(QB, "strict")
    from_s = _tri(QB, "lower_incl").astype(F32)
    inst = [(s, h) for s in range(n) for h in (0, 1)]
    carry = [(c_lo[s], c_hi[s])[h] for s, h in inst]
    z = [mm(q[s][h], k[s], "nt", 1, 1) for s, h in inst]
    soft = [jnp.log(1.0 + jnp.exp(-jnp.abs(x))) for x in z]
    log_keep = [-jnp.maximum(x, 0.0) - sp for x, sp in zip(z, soft)]
    if diag:
        log_keep = [jnp.where(below, x, 0.0) for x in log_keep]
    own = [jnp.sum(x, axis=1, keepdims=True) for x in log_keep]
    if from_here is not None:
        carry = [lax.stop_gradient(from_here[s][h] - o) + cr for (s, h), o, cr in zip(inst, own, carry)]
    tail = [mm(x, from_s, "nn", SB_SUM_PIECES, 1, (True, False)) for x in log_keep]
    log_a = [x + tl + cr for x, tl, cr in zip(z, tail, carry)]
    if diag:
        log_a = [jnp.where(below, x, -1e30) for x in log_a]
    att = [jnp.exp(x) for x in log_a]
    out_h = [mm(x, v[s], "nn", 1, 1) for x, (s, _) in zip(att, inst)]
    out = [jnp.where(lo, out_h[2 * s], out_h[2 * s + 1]) for s in range(n)]
    return out, [own[2 * s] for s in range(n)], [own[2 * s + 1] for s in range(n)]


def sb_split_q(q):
    lo = _lane_lo((1, LANES))
    qs = q * SB_SCALE
    return jnp.where(lo, qs, 0.0), jnp.where(lo, 0.0, qs)


def sb_post(o, gain):
    return o * lax.rsqrt(_segsum(o * o) * (1.0 / HEAD_DIM) + RMS_EPS) * gain


def f_final(x3, g, target):
    y = f_norm(x3, g)
    err = y - target
    return 0.5 * jnp.mean(err * err, axis=-1, keepdims=True)


def _params(sem):
    return pltpu.CompilerParams(dimension_semantics=sem, vmem_limit_bytes=VMEM_LIMIT)


def rw_call(name, body_fn, rows, pars, out_rows, out_accs, tile):
    rows = [item if isinstance(item, tuple) else (item, item.shape[1], 0) for item in rows]
    row_arrays = [arr for arr, _, _ in rows]
    n_rows = row_arrays[0].shape[0]
    tile = min(tile, n_rows)
    steps = n_rows // tile
    row_specs = [pl.BlockSpec((tile, cols), functools.partial(lambda i, c: (i, c), c=cblk)) for _, cols, cblk in rows]
    par_specs = [pl.BlockSpec(p.shape, lambda i: (0, 0)) for p in pars]
    nr, npar, nor, noa = len(row_arrays), len(pars), len(out_rows), len(out_accs)

    def body(*refs):
        row_vals = [refs[i][...] for i in range(nr)]
        par_vals = [refs[nr + i][...] for i in range(npar)]
        o_refs = refs[nr + npar:nr + npar + nor]
        a_refs = refs[nr + npar + nor:]
        row_outs, acc_outs = body_fn(row_vals, par_vals)
        for ref, val in zip(o_refs, row_outs):
            ref[...] = val.astype(ref.dtype)
        if noa:
            first = pl.program_id(0) == 0

            @pl.when(first)
            def _():
                for ref, val in zip(a_refs, acc_outs):
                    ref[...] = val.astype(ref.dtype)

            @pl.when(jnp.logical_not(first))
            def _():
                for ref, val in zip(a_refs, acc_outs):
                    ref[...] = ref[...] + val.astype(ref.dtype)

    out_shape = [jax.ShapeDtypeStruct((n_rows, c), dt) for c, dt in out_rows]
    out_shape += [jax.ShapeDtypeStruct(s, dt) for s, dt in out_accs]
    out_specs = [pl.BlockSpec((tile, c), lambda i: (i, 0)) for c, _ in out_rows]
    out_specs += [pl.BlockSpec(s, lambda i: (0, 0)) for s, _ in out_accs]
    outs = pl.pallas_call(
        body, name=name, grid=(steps,), in_specs=row_specs + par_specs, out_specs=out_specs,
        out_shape=out_shape, compiler_params=_params(("arbitrary",)),
    )(*row_arrays, *pars)
    return outs[:nor], outs[nor:]


def matmul(name, a, b, form, out_dtype, tm, tn, tk, extras=(), epilogue=None):
    out_dtypes = out_dtype if isinstance(out_dtype, tuple) else (out_dtype,)
    tm, tn, tk = min(tm, a.shape[1 if form == "tn" else 0]), min(tn, b.shape[0 if form == "nt" else 1]), min(tk, a.shape[0 if form == "tn" else 1])
    if form == "nn":
        (m, kd), n = a.shape, b.shape[1]
        a_spec = pl.BlockSpec((tm, tk), lambda i, j, k: (i, k))
        b_spec = pl.BlockSpec((tk, tn), lambda i, j, k: (k, j))
    elif form == "nt":
        (m, kd), n = a.shape, b.shape[0]
        a_spec = pl.BlockSpec((tm, tk), lambda i, j, k: (i, k))
        b_spec = pl.BlockSpec((tn, tk), lambda i, j, k: (j, k))
    else:
        (kd, m), n = a.shape, b.shape[1]
        a_spec = pl.BlockSpec((tk, tm), lambda i, j, k: (k, i))
        b_spec = pl.BlockSpec((tk, tn), lambda i, j, k: (k, j))
    ksteps = kd // tk

    n_extra, n_out = len(extras), len(out_dtypes)

    def body(a_ref, b_ref, *rest):
        e_refs, o_refs = rest[:n_extra], rest[n_extra:n_extra + n_out]
        kstep = pl.program_id(2)
        part = lax.dot_general(a_ref[...].astype(BF16), b_ref[...].astype(BF16), _DIMS[form],
                               preferred_element_type=F32)

        def finish(acc):
            outs = epilogue(acc, *[r[...] for r in e_refs]) if epilogue else (acc,)
            for ref, val in zip(o_refs, outs):
                ref[...] = val.astype(ref.dtype)

        if ksteps == 1:
            finish(part)
            return
        acc_ref = rest[-1]

        @pl.when(kstep == 0)
        def _():
            acc_ref[...] = part

        @pl.when(jnp.logical_and(kstep > 0, kstep < ksteps - 1))
        def _():
            acc_ref[...] = acc_ref[...] + part

        @pl.when(kstep == ksteps - 1)
        def _():
            finish(acc_ref[...] + part)

    out_spec = pl.BlockSpec((tm, tn), lambda i, j, k: (i, j))
    outs = pl.pallas_call(
        body, name=name, grid=(m // tm, n // tn, ksteps), in_specs=[a_spec, b_spec] + [out_spec] * n_extra,
        out_specs=[out_spec] * n_out,
        out_shape=[jax.ShapeDtypeStruct((m, n), dt) for dt in out_dtypes],
        scratch_shapes=[pltpu.VMEM((tm, tn), F32)] if ksteps > 1 else [],
        compiler_params=_params(("parallel", "parallel", "arbitrary")),
    )(a, b, *extras)
    return outs if isinstance(out_dtype, tuple) else outs[0]


PREP_TILE = 256
PREP_TILE_BWD = 128
SUBLANES = 8


def _shift_in(rows, first):
    rolled = pltpu.roll(rows, 1, 0)
    row = lax.broadcasted_iota(jnp.int32, (SUBLANES, rows.shape[1]), 0)
    head = jnp.where(row == 0, first, rolled[0:SUBLANES])
    return jnp.concatenate([head, rolled[SUBLANES:]], axis=0), rolled


def rwkv_prep_fwd(p, pars, seq_len):
    n_tok = p.shape[0]
    tile = min(PREP_TILE, seq_len)
    tile_b = min(PREP_TILE_BWD, tile)
    steps, per_seq, sub = n_tok // tile, seq_len // tile, tile // tile_b
    n_par = len(pars)

    def body(p_ref, *rest):
        par_refs, out_refs, edge_ref, last8 = rest[:n_par], rest[n_par:n_par + 6], rest[n_par + 6], rest[n_par + 7]
        step = pl.program_id(0)

        @pl.when(step == 0)
        def _():
            last8[...] = jnp.zeros_like(last8)

        rows = p_ref[...]
        before = jnp.where(step % per_seq == 0, 0.0, pltpu.roll(last8[...], 1, 0))
        prev, rolled = _shift_in(rows, before)
        edge_ref[0] = prev[0:SUBLANES]
        for m in range(1, sub):
            edge_ref[m] = rolled[m * tile_b:m * tile_b + SUBLANES]
        last8[...] = rows[tile - SUBLANES:tile]
        for ref, val in zip(out_refs, f_prep(rows, prev, *[r[...] for r in par_refs])):
            ref[...] = val

    row_out = pl.BlockSpec((tile, RWKV_W), lambda i: (i, 0))
    outs = pl.pallas_call(
        body, name="rwkv_prep", grid=(steps,),
        in_specs=[pl.BlockSpec((tile, RWKV_COLS), lambda i: (i, 0))] + [pl.BlockSpec(t.shape, lambda i: (0, 0)) for t in pars],
        out_specs=[row_out] * 6 + [pl.BlockSpec((sub, SUBLANES, RWKV_COLS), lambda i: (i, 0, 0))],
        out_shape=[jax.ShapeDtypeStruct((n_tok, RWKV_W), F32)] * 6
        + [jax.ShapeDtypeStruct((steps * sub, SUBLANES, RWKV_COLS), F32)],
        scratch_shapes=[pltpu.VMEM((SUBLANES, RWKV_COLS), F32)],
        compiler_params=_params(("arbitrary",)),
    )(p, *pars)
    return outs[:6], outs[6]


def rwkv_prep_bwd(p, edges, pars, cots, d_sb, seq_len):
    n_tok = p.shape[0]
    tile = min(PREP_TILE_BWD, seq_len)
    steps, per_seq = n_tok // tile, seq_len // tile
    n_par = len(pars)
    back = lambda i: steps - 1 - i

    def body(p_ref, edge_ref, *rest):
        par_refs, rest = rest[:n_par], rest[n_par:]
        cot_refs, sb_refs, dp_ref, acc_refs, next8 = rest[:6], rest[6:9], rest[9], rest[10:10 + n_par], rest[10 + n_par]
        step = pl.program_id(0)
        first = step == 0

        @pl.when(first)
        def _():
            next8[...] = jnp.zeros_like(next8)

        rows = p_ref[...]
        prev, _ = _shift_in(rows, edge_ref[0])
        _, vjp = jax.vjp(f_prep, rows, prev, *[r[...].astype(F32) for r in par_refs])
        grads = vjp(tuple(r[...] for r in cot_refs))
        d_rows, d_prev = grads[0], grads[1]
        up = pltpu.roll(d_prev, tile - 1, 0)
        ends_seq = back(step) % per_seq == per_seq - 1
        after = jnp.where(ends_seq, 0.0, pltpu.roll(next8[...], SUBLANES - 1, 0))
        row = lax.broadcasted_iota(jnp.int32, (SUBLANES, RWKV_COLS), 0)
        tail = jnp.where(row == SUBLANES - 1, after, up[tile - SUBLANES:tile])
        d_rows = d_rows + jnp.concatenate([up[:tile - SUBLANES], tail], axis=0)
        next8[...] = d_prev[0:SUBLANES]
        dp_ref[...] = jnp.concatenate([d_rows] + [r[...] for r in sb_refs], axis=1).astype(dp_ref.dtype)

        @pl.when(first)
        def _():
            for ref, val in zip(acc_refs, grads[2:]):
                ref[...] = val

        @pl.when(jnp.logical_not(first))
        def _():
            for ref, val in zip(acc_refs, grads[2:]):
                ref[...] = ref[...] + val

    cols = RWKV_COLS + sum(t.shape[1] for t in d_sb)
    half = pl.BlockSpec((tile, RWKV_W), lambda i: (back(i), 0))
    par_specs = [pl.BlockSpec(t.shape, lambda i: (0, 0)) for t in pars]
    outs = pl.pallas_call(
        body, name="d_rwkv_prep", grid=(steps,),
        in_specs=[pl.BlockSpec((tile, RWKV_COLS), lambda i: (back(i), 0)),
                  pl.BlockSpec((1, SUBLANES, RWKV_COLS), lambda i: (back(i), 0, 0))] + par_specs + [half] * 9,
        out_specs=[pl.BlockSpec((tile, cols), lambda i: (back(i), 0))] + par_specs,
        out_shape=[jax.ShapeDtypeStruct((n_tok, cols), BF16)] + [jax.ShapeDtypeStruct(t.shape, F32) for t in pars],
        scratch_shapes=[pltpu.VMEM((SUBLANES, RWKV_COLS), F32)],
        compiler_params=_params(("arbitrary",)),
    )(p, edges, *pars, *cots, *d_sb)
    return outs[0], outs[1:]


RWKV_HP = 1


class Hosted:
    def __init__(self, arrays, gather):
        self.arrays, self.gather, self.n = list(arrays), gather, len(arrays)

    def split(self, refs, n_in, n_out):
        n = self.n
        ins, outs, sems = refs[n_in:n_in + n], refs[n_in + n + n_out:n_in + 2 * n + n_out], refs[-3:]
        own = refs[:n_in] + refs[n_in + n:n_in + n + n_out] + refs[n_in + 2 * n + n_out:-3]
        return own, exchange_copies(ins, outs, *sems, self.gather)

    def run(self, copies, first, last):
        @pl.when(first)
        def _():
            for cp in copies:
                cp.start()

        @pl.when(last)
        def _():
            for cp in copies:
                cp.wait()


def rwkv_fwd(r, kraw, v, lw, asig, k_k, k_a, r_k, gn_w, gn_b, n_seq, seq_len, hosted):
    n_chunks = seq_len // CHUNK
    n_pairs = RWKV_W // LANES
    row = pl.BlockSpec((n_seq, CHUNK, LANES), lambda h, c: (0, c, h))
    par = pl.BlockSpec((1, LANES), lambda h, c: (0, h))
    s0_spec = pl.BlockSpec((1, 1, n_seq, LANES, LANES), lambda h, c: (h, c, 0, 0, 0))

    def body(*refs):
        own, copies = hosted.split(refs, 10, 2)
        r_ref, k_ref, v_ref, lw_ref, a_ref, kk_ref, ka_ref, rk_ref, gw_ref, gb_ref, z_ref, s0_ref, state = own
        step = pl.program_id(0) * n_chunks + pl.program_id(1)
        hosted.run(copies, step == 0, step == n_pairs * n_chunks - 1)

        @pl.when(pl.program_id(1) == 0)
        def _():
            state[...] = jnp.zeros_like(state)

        pars = [ref[...] for ref in (kk_ref, ka_ref, rk_ref, gw_ref, gb_ref)]
        seqs = range(n_seq)
        s0 = [state[s] for s in seqs]
        rows = [[ref[s] for s in seqs] for ref in (r_ref, k_ref, v_ref, lw_ref, a_ref)]
        z, s1 = rwkv_chunk(s0, *rows, *pars, RWKV_HP)
        for s in seqs:
            s0_ref[0, 0, s] = s0[s]
            z_ref[s] = z[s]
            state[s] = s1[s]

    outs = pl.pallas_call(
        body, name="rwkv_fwd", grid=(n_pairs, n_chunks),
        in_specs=[row] * 5 + [par] * 5 + [ANY_SPEC] * hosted.n, out_specs=[row, s0_spec] + [ANY_SPEC] * hosted.n,
        out_shape=[jax.ShapeDtypeStruct(r.shape, F32),
                   jax.ShapeDtypeStruct((n_pairs, n_chunks, n_seq, LANES, LANES), F32)]
        + exchange_shapes(hosted.arrays, hosted.gather),
        scratch_shapes=[pltpu.VMEM((n_seq, LANES, LANES), F32)] + exchange_sems(hosted.n),
        compiler_params=_params(("arbitrary", "arbitrary")),
    )(r, kraw, v, lw, asig, k_k, k_a, r_k, gn_w, gn_b, *hosted.arrays)
    return outs[0], outs[1], outs[2:]


def rwkv_bwd(r, kraw, v, lw, asig, k_k, k_a, r_k, gn_w, gn_b, s0_all, dz, n_seq, seq_len, hosted):
    n_chunks = seq_len // CHUNK
    n_pairs = RWKV_W // LANES
    row = pl.BlockSpec((n_seq, CHUNK, LANES), lambda h, c: (0, n_chunks - 1 - c, h))
    par = pl.BlockSpec((1, LANES), lambda h, c: (0, h))
    s0_spec = pl.BlockSpec((1, 1, n_seq, LANES, LANES), lambda h, c: (h, n_chunks - 1 - c, 0, 0, 0))

    def body(*refs):
        own, copies = hosted.split(refs, 12, 10)
        (r_ref, k_ref, v_ref, lw_ref, a_ref, kk_ref, ka_ref, rk_ref, gw_ref, gb_ref, s0_ref, dz_ref,
         dr_ref, dk_ref, dv_ref, dlw_ref, da_ref, dkk_ref, dka_ref, drk_ref, dgw_ref, dgb_ref, dstate) = own
        step = pl.program_id(0) * n_chunks + pl.program_id(1)
        hosted.run(copies, step == 0, step == n_pairs * n_chunks - 1)
        first = pl.program_id(1) == 0

        @pl.when(first)
        def _():
            dstate[...] = jnp.zeros_like(dstate)

        pars = [ref[...] for ref in (kk_ref, ka_ref, rk_ref, gw_ref, gb_ref)]
        fn = functools.partial(rwkv_chunk, hp=RWKV_HP)
        seqs = range(n_seq)
        rows = [[ref[s] for s in seqs] for ref in (r_ref, k_ref, v_ref, lw_ref, a_ref)]
        _, vjp = jax.vjp(fn, [s0_ref[0, 0, s] for s in seqs], *rows, *pars)
        grads = vjp(([dz_ref[s] for s in seqs], [dstate[s] for s in seqs]))
        for s in seqs:
            dstate[s] = grads[0][s]
            for ref, val in zip((dr_ref, dk_ref, dv_ref, dlw_ref, da_ref), grads[1:6]):
                ref[s] = val[s]
        accs = tuple(zip((dkk_ref, dka_ref, drk_ref, dgw_ref, dgb_ref), grads[6:]))

        @pl.when(first)
        def _():
            for ref, val in accs:
                ref[...] = val

        @pl.when(jnp.logical_not(first))
        def _():
            for ref, val in accs:
                ref[...] = ref[...] + val

    rows_shape = jax.ShapeDtypeStruct(r.shape, F32)
    par_shape = jax.ShapeDtypeStruct((1, RWKV_W), F32)
    outs = pl.pallas_call(
        body, name="rwkv_bwd", grid=(n_pairs, n_chunks),
        in_specs=[row] * 5 + [par] * 5 + [s0_spec, row] + [ANY_SPEC] * hosted.n,
        out_specs=[row] * 5 + [par] * 5 + [ANY_SPEC] * hosted.n,
        out_shape=[rows_shape] * 5 + [par_shape] * 5 + exchange_shapes(hosted.arrays, hosted.gather),
        scratch_shapes=[pltpu.VMEM((n_seq, LANES, LANES), F32)] + exchange_sems(hosted.n),
        compiler_params=_params(("arbitrary", "arbitrary")),
    )(r, kraw, v, lw, asig, k_k, k_a, r_k, gn_w, gn_b, s0_all, dz, *hosted.arrays)
    return outs[:10], outs[10:]


SB_Q0 = RWKV_COLS // LANES
SB_K0 = SB_Q0 + SB_W // LANES
SB_V0 = SB_K0 + SB_W // LANES
SB_SEQS = 4
SB_BUFFERS = pl.Buffered(1)
SB_SUM_PIECES = 2
SB_DEAD = -110.0


def _col_of(c_lo, c_hi):
    return jnp.where(_lane_lo((1, LANES)), c_lo, c_hi)


def sb_fwd(p, gain, n_seq, seq_len):
    n_pairs = SB_W // LANES
    n_q = seq_len // QB
    nb = min(SB_SEQS, n_seq)

    def seq_spec(c0):
        return pl.BlockSpec((nb, seq_len, LANES), functools.partial(lambda b, h, c0: (b, 0, c0 + h), c0=c0),
                            pipeline_mode=SB_BUFFERS)

    out_spec = pl.BlockSpec((nb, seq_len, LANES), lambda b, h: (b, 0, h), pipeline_mode=SB_BUFFERS)

    def body(q_ref, k_ref, v_ref, g_ref, y_ref, o_ref, tot_ref, first_ref):
        gain = g_ref[...]

        def q_block(i, _):
            qs = pl.multiple_of(i * QB, QB)
            seqs = range(nb)
            zeros = [jnp.zeros((QB, 1), F32)] * nb
            qv = [sb_split_q(q_ref[s, pl.ds(qs, QB), :]) for s in seqs]
            add = lambda xs, ys: [x + y for x, y in zip(xs, ys)]

            def tiles(ks, c_lo, c_hi, diag):
                return sb_tile(qv, [k_ref[s, pl.ds(ks, QB), :] for s in seqs],
                               [v_ref[s, pl.ds(ks, QB), :] for s in seqs], c_lo, c_hi, diag)

            def alive(c_lo, c_hi):
                top = functools.reduce(jnp.maximum, list(c_lo) + list(c_hi))
                return jnp.max(top) > SB_DEAD

            def k_block(state):
                j, _, (o, c_lo, c_hi) = state
                o2, s_lo, s_hi = tiles(pl.multiple_of(j * QB, QB), c_lo, c_hi, False)
                c_lo, c_hi = add(c_lo, s_lo), add(c_hi, s_hi)
                return j - 1, alive(c_lo, c_hi), (add(o, o2), c_lo, c_hi)

            o, c_lo, c_hi = tiles(qs, zeros, zeros, True)
            j, _, (o, c_lo, c_hi) = lax.while_loop(lambda st: jnp.logical_and(st[0] >= 0, st[1]), k_block,
                                                   (i - 1, alive(c_lo, c_hi), (o, c_lo, c_hi)))
            first_ref[pl.program_id(0), pl.program_id(1), i] = j + 1
            for s in seqs:
                o_ref[s, pl.ds(qs, QB), :] = o[s]
                tot_ref[s, pl.ds(qs, QB), :] = jnp.broadcast_to(_col_of(c_lo[s], c_hi[s]), (QB, LANES))
                y_ref[s, pl.ds(qs, QB), :] = sb_post(o[s], gain)
            return 0

        lax.fori_loop(0, n_q, q_block, 0)

    shape = jax.ShapeDtypeStruct((n_seq, seq_len, SB_W), F32)
    return pl.pallas_call(
        body, name="sb_fwd", grid=(n_seq // nb, n_pairs),
        in_specs=[seq_spec(SB_Q0), seq_spec(SB_K0), seq_spec(SB_V0), pl.BlockSpec((1, LANES), lambda b, h: (0, h))],
        out_specs=[out_spec] * 3 + [pl.BlockSpec(memory_space=pltpu.SMEM)],
        out_shape=[shape] * 3 + [jax.ShapeDtypeStruct((n_seq // nb, n_pairs, n_q), jnp.int32)],
        compiler_params=_params(("arbitrary", "arbitrary")),
    )(p, p, p, gain)


def sb_bwd(p, gain, o_raw, tot, dy, first, n_seq, seq_len):
    n_pairs = SB_W // LANES
    n_q = seq_len // QB
    nb = min(SB_SEQS, n_seq)

    def seq_spec(c0):
        return pl.BlockSpec((nb, seq_len, LANES), functools.partial(lambda h, b, c0: (b, 0, c0 + h), c0=c0),
                            pipeline_mode=SB_BUFFERS)

    own = pl.BlockSpec((nb, seq_len, LANES), lambda h, b: (b, 0, h), pipeline_mode=SB_BUFFERS)
    par = pl.BlockSpec((1, LANES), lambda h, b: (0, h))

    def body(q_ref, k_ref, v_ref, g_ref, o_ref, tot_ref, dy_ref, first_ref, dq_ref, dk_ref, dv_ref, dg_ref):
        gain = g_ref[...]
        lo = _lane_lo((1, LANES))
        dk_ref[...] = jnp.zeros_like(dk_ref)
        dv_ref[...] = jnp.zeros_like(dv_ref)

        def q_block(i, dgain):
            qs = pl.multiple_of(i * QB, QB)
            seqs = range(nb)
            zeros = [jnp.zeros((QB, 1), F32)] * nb
            qv, dov, t_lo, t_hi = [], [], [], []
            for s in seqs:
                qv.append(sb_split_q(q_ref[s, pl.ds(qs, QB), :]))
                _, post_vjp = jax.vjp(sb_post, o_ref[s, pl.ds(qs, QB), :], gain)
                do, dg_s = post_vjp(dy_ref[s, pl.ds(qs, QB), :])
                dov.append(do)
                dgain = dgain + dg_s
                tot_s = tot_ref[s, pl.ds(qs, QB), :]
                t_lo.append(jnp.max(jnp.where(lo, tot_s, -jnp.inf), axis=1, keepdims=True))
                t_hi.append(jnp.max(jnp.where(lo, -jnp.inf, tot_s), axis=1, keepdims=True))
            add = lambda xs, ys: [x + y for x, y in zip(xs, ys)]
            sub = lambda xs, ys: [x - y for x, y in zip(xs, ys)]

            def tile(ks, carry, diag):
                dq, rem_lo, rem_hi, g_lo, g_hi = carry
                kv = [k_ref[s, pl.ds(ks, QB), :] for s in seqs]
                vv = [v_ref[s, pl.ds(ks, QB), :] for s in seqs]
                fn = functools.partial(sb_tile, diag=diag, from_here=list(zip(rem_lo, rem_hi)))
                (_, s_lo, s_hi), vjp = jax.vjp(fn, qv, kv, vv, zeros, zeros)
                dq_t, dk_t, dv_t, dc_lo, dc_hi = vjp((dov, g_lo, g_hi))
                dq_t = [jnp.where(lo, d_lo, d_hi) for d_lo, d_hi in dq_t]
                for s in seqs:
                    dk_ref[s, pl.ds(ks, QB), :] = dk_ref[s, pl.ds(ks, QB), :] + dk_t[s]
                    dv_ref[s, pl.ds(ks, QB), :] = dv_ref[s, pl.ds(ks, QB), :] + dv_t[s]
                return add(dq, dq_t), sub(rem_lo, s_lo), sub(rem_hi, s_hi), add(g_lo, dc_lo), add(g_hi, dc_hi)

            def k_block(j, carry):
                return tile(pl.multiple_of(j * QB, QB), carry, False)

            carry = ([jnp.zeros((QB, LANES), F32)] * nb, t_lo, t_hi, zeros, zeros)
            carry = lax.fori_loop(first_ref[pl.program_id(1), pl.program_id(0), i], i, k_block, carry)
            carry = tile(qs, carry, True)
            for s in seqs:
                dq_ref[s, pl.ds(qs, QB), :] = carry[0][s] * SB_SCALE
            return dgain

        dgain = lax.fori_loop(0, n_q, q_block, jnp.zeros((1, LANES), F32))
        first = pl.program_id(1) == 0

        @pl.when(first)
        def _():
            dg_ref[...] = dgain

        @pl.when(jnp.logical_not(first))
        def _():
            dg_ref[...] = dg_ref[...] + dgain

    shape = jax.ShapeDtypeStruct((n_seq, seq_len, SB_W), F32)
    return pl.pallas_call(
        body, name="sb_bwd", grid=(n_pairs, n_seq // nb),
        in_specs=[seq_spec(SB_Q0), seq_spec(SB_K0), seq_spec(SB_V0), par, own, own, own,
                  pl.BlockSpec(memory_space=pltpu.SMEM)],
        out_specs=[own, own, own, par],
        out_shape=[shape, shape, shape, jax.ShapeDtypeStruct((1, SB_W), F32)],
        compiler_params=_params(("arbitrary", "arbitrary")),
    )(p, p, p, gain, o_raw, tot, dy, first)


def exchange(name, arrays, gather):
    n = len(arrays)

    def body(*refs):
        copies = exchange_copies(refs[:n], refs[n:2 * n], *refs[2 * n:], gather)
        for cp in copies:
            cp.start()
        for cp in copies:
            cp.wait()

    return pl.pallas_call(
        body, name=name, in_specs=[ANY_SPEC] * n, out_specs=[ANY_SPEC] * n, out_shape=exchange_shapes(arrays, gather),
        scratch_shapes=exchange_sems(n),
    )(*arrays)


ANY_SPEC = pl.BlockSpec(memory_space=pl.ANY)


def exchange_shapes(arrays, gather):
    return [jax.ShapeDtypeStruct(((N_DEV,) + a.shape) if gather else a.shape, a.dtype) for a in arrays]


def exchange_sems(n):
    return [pltpu.SemaphoreType.DMA((n, N_DEV - 1)), pltpu.SemaphoreType.DMA((n, N_DEV - 1)),
            pltpu.SemaphoreType.DMA((n,))]


def exchange_copies(ins, outs, send_sems, recv_sems, local_sems, gather):
    x, y, c = lax.axis_index("x"), lax.axis_index("y"), lax.axis_index("c")
    me = 4 * x + 2 * y + c
    copies = []
    for a, (src_all, dst_all) in enumerate(zip(ins, outs)):
        own = src_all if gather else src_all.at[me]
        copies.append(pltpu.make_async_copy(own, dst_all.at[me], local_sems.at[a]))
        for j in range(1, N_DEV):
            px, py, pc = (x + (j >> 2)) % 2, (y + ((j >> 1) & 1)) % 2, (c + (j & 1)) % 2
            src = src_all if gather else src_all.at[4 * px + 2 * py + pc]
            copies.append(pltpu.make_async_remote_copy(
                src_ref=src, dst_ref=dst_all.at[me], send_sem=send_sems.at[a, j - 1],
                recv_sem=recv_sems.at[a, j - 1], device_id=(px, py, pc), device_id_type=pl.DeviceIdType.MESH))
    return copies


def adamw(name, w, parts, m, v, tile):
    rows, cols = w.shape
    spec = pl.BlockSpec((tile, cols), lambda i: (i, 0))
    part_spec = pl.BlockSpec((N_DEV, tile, cols), lambda i: (0, i, 0))

    def body(w_ref, p_ref, m_ref, v_ref, g_ref, d_ref, nm_ref, nv_ref):
        g = p_ref[0].astype(F32)
        for s in range(1, N_DEV):
            g = g + p_ref[s].astype(F32)
        new_m = ADAM_B1 * m_ref[...] + (1.0 - ADAM_B1) * g
        new_v = ADAM_B2 * v_ref[...] + (1.0 - ADAM_B2) * (g * g)
        m_hat = new_m / (1.0 - ADAM_B1 ** ADAM_STEP)
        v_hat = new_v / (1.0 - ADAM_B2 ** ADAM_STEP)
        g_ref[...] = g
        d_ref[...] = -ADAM_LR * (m_hat / (jnp.sqrt(v_hat) + ADAM_EPS) + ADAM_WD * w_ref[...])
        nm_ref[...] = new_m
        nv_ref[...] = new_v

    shape = jax.ShapeDtypeStruct((rows, cols), F32)
    return pl.pallas_call(
        body, name=name, grid=(rows // tile,), in_specs=[spec, part_spec, spec, spec],
        out_specs=[spec] * 4, out_shape=[shape] * 4, compiler_params=_params(("arbitrary",)),
    )(w, parts, m, v)


SMALL = ("ln1_g", "tok_mu", "w0", "a0", "k_k", "k_a", "r_k", "gn_w", "gn_b", "sb_gain", "ln2_g", "lnf_g")
EARLY = ("w_in", "w_decay_up", "w_aaa_up", "w_gate_up")
LATE = ("w_out", "w_up", "w_down")
BIG = EARLY + LATE
COL_SHARDED = ("w_in", "w_decay_up", "w_aaa_up", "w_gate_up", "w_up")
ORDER = ("ln1_g", "w_in", "tok_mu", "w0", "w_decay_up", "a0", "w_aaa_up", "w_gate_up", "k_k", "k_a", "r_k",
         "gn_w", "gn_b", "sb_gain", "w_out", "ln2_g", "w_up", "w_down", "lnf_g")


def _pack(vectors, rows):
    flat = jnp.concatenate([v.reshape(-1).astype(F32) for v in vectors])
    return jnp.pad(flat, (0, rows * LANES - flat.shape[0])).reshape(rows, LANES)


def _full_cols(gathered):
    d, k, cols = gathered.shape
    return jnp.transpose(gathered, (1, 0, 2)).reshape(k, d * cols)


def _col_parts(full):
    k, n = full.shape
    return jnp.transpose(full.reshape(k, N_DEV, n // N_DEV), (1, 0, 2))


def kernel(x, ln1_g, w_in, tok_mu, w0, w_decay_up, a0, w_aaa_up, w_gate_up, k_k, k_a, r_k, gn_w, gn_b, sb_gain, w_out, ln2_g, w_up, w_down, lnf_g, loss_target, m_ln1_g, m_w_in, m_tok_mu, m_w0, m_w_decay_up, m_a0, m_w_aaa_up, m_w_gate_up, m_k_k, m_k_a, m_r_k, m_gn_w, m_gn_b, m_sb_gain, m_w_out, m_ln2_g, m_w_up, m_w_down, m_lnf_g, v_ln1_g, v_w_in, v_tok_mu, v_w0, v_w_decay_up, v_a0, v_w_aaa_up, v_w_gate_up, v_k_k, v_k_a, v_r_k, v_gn_w, v_gn_b, v_sb_gain, v_w_out, v_ln2_g, v_w_up, v_w_down, v_lnf_g):
    args = dict(locals())
    weights = {n: args[n] for n in ORDER}
    mom_m = {n: args["m_" + n] for n in ORDER}
    mom_v = {n: args["v_" + n] for n in ORDER}

    n_seq, seq_len, d_model = x.shape
    n_tok = n_seq * seq_len
    x2d = x.reshape(n_tok, d_model)
    tgt = loss_target.reshape(n_tok, d_model)
    row = lambda t: t.reshape(1, -1).astype(F32)

    shard = {n: weights[n][0].astype(BF16) for n in BIG}
    gathered = dict(zip(EARLY, exchange("gather_early", [shard[n] for n in EARLY], True)))
    w_in_f = _full_cols(gathered["w_in"])
    zeros64 = jnp.zeros((HEAD_DIM, RWKV_W), BF16)
    wd_pad = jnp.concatenate([_full_cols(gathered["w_decay_up"]), zeros64], axis=0)
    wa_pad = jnp.concatenate([zeros64, _full_cols(gathered["w_aaa_up"])], axis=0)
    wg_f = _full_cols(gathered["w_gate_up"])
    in_cols = w_in_f.shape[1]

    g1, mu, w0r, a0r = row(ln1_g), row(tok_mu), row(w0), row(a0)
    kkr, kar, rkr, gwr, gbr, sgr = row(k_k), row(k_a), row(r_k), row(gn_w), row(gn_b), row(sb_gain)
    g2, gf = row(ln2_g), row(lnf_g)

    (h1,), _ = rw_call("norm1", lambda r, p: ([f_norm(r[0], p[0])], []), [x2d], [g1], [(d_model, BF16)], [], 512)
    p = matmul("proj_in", h1, w_in_f, "nn", F32, 512, in_cols // 2, d_model)
    prep_pars = [mu, w0r, wd_pad, a0r, wa_pad, wg_f]
    (r_, kraw, v_, lw, asig, gate), prep_edges = rwkv_prep_fwd(p, prep_pars, seq_len)
    by_seq = lambda t: t.reshape(n_seq, seq_len, t.shape[-1])
    flat = lambda t: t.reshape(n_tok, t.shape[-1])
    rwkv_in = [by_seq(t) for t in (r_, kraw, v_, lw, asig)]
    z, s0_all, late = rwkv_fwd(*rwkv_in, kkr, kar, rkr, gwr, gbr, n_seq, seq_len,
                               Hosted([shard[n] for n in LATE], True))
    gathered = dict(zip(LATE, late))
    w_out_f = gathered["w_out"].reshape(d_model, d_model)
    w_up_f = _full_cols(gathered["w_up"])
    w_down_f = gathered["w_down"].reshape(-1, d_model)
    d_ff = w_up_f.shape[1]
    z = flat(z)
    y_sb, o_raw, tot, sb_first = sb_fwd(by_seq(p), sgr, n_seq, seq_len)
    y_sb = flat(y_sb)
    (ycat,), _ = rw_call("mix_cat", lambda r, q: ([jnp.concatenate([r[0] * r[1], r[2]], axis=1)], []),
                         [z, gate, y_sb], [], [(d_model, BF16)], [], 512)
    mix = matmul("proj_out", ycat, w_out_f, "nn", F32, 512, d_model, d_model)
    (x2, h2), _ = rw_call("resid_norm2", lambda r, q: ([r[0] + r[1], f_norm(r[0] + r[1], q[0])], []),
                          [x2d, mix], [g2], [(d_model, F32), (d_model, BF16)], [], 512)
    u, act = matmul("mlp_up", h2, w_up_f, "nn", (F32, BF16), 512, d_ff // 4, d_model,
                    epilogue=lambda acc: (acc, jnp.square(jnp.maximum(acc, 0.0))))
    mlp = matmul("mlp_down", act, w_down_f, "nn", F32, 512, d_model, d_ff)

    def final_body(r, q):
        x3 = r[0] + r[1]
        loss_rows, vjp = jax.vjp(lambda a, g: f_final(a, g, r[2]), x3, q[0])
        dx3, dgf = vjp(jnp.ones_like(loss_rows))
        return [dx3, dx3], [jnp.broadcast_to(jnp.sum(loss_rows), (1, LANES)), dgf]

    (dx3, dx3_b), (loss_acc, d_lnf) = rw_call("loss_head", final_body, [x2, mlp, tgt], [gf],
                                              [(d_model, F32), (d_model, BF16)],
                                              [((1, LANES), F32), ((1, d_model), F32)], 256)

    du = matmul("d_act", dx3_b, w_down_f, "nt", BF16, 512, d_ff // 4, d_model, extras=[u],
                epilogue=lambda acc, uv: (acc * (2.0 * jnp.maximum(uv, 0.0)),))
    dw_down = matmul("dw_down", act, dx3_b, "tn", BF16, 512, d_model, 1024)
    dh2 = matmul("d_h2", du, w_up_f, "nt", F32, 512, d_model, d_ff)
    dw_up = matmul("dw_up", h2, du, "tn", BF16, d_model, 512, 1024)

    def norm_bwd(r, q):
        xv, dh, dres = r
        _, vjp = jax.vjp(f_norm, xv, q[0])
        dx, dg = vjp(dh)
        return [dx + dres, dx + dres], [dg]

    (dx2, dx2_b), (d_ln2,) = rw_call("d_norm2", norm_bwd, [x2, dh2, dx3], [g2], [(d_model, F32), (d_model, BF16)],
                                     [((1, d_model), F32)], 256)

    dycat = matmul("d_ycat", dx2_b, w_out_f, "nt", F32, 512, d_model, d_model)
    dw_out = matmul("dw_out", ycat, dx2_b, "tn", BF16, d_model, d_model, 1024)
    (dz, dgate), _ = rw_call("d_mix", lambda r, q: ([r[0] * r[2], r[0] * r[1]], []),
                             [(dycat, RWKV_W, 0), z, gate], [], [(RWKV_W, F32)] * 2, [], 512)
    dq, dk_sb, dv_sb, d_sg = sb_bwd(by_seq(p), sgr, o_raw, tot, by_seq(dycat[:, RWKV_W:]), sb_first, n_seq, seq_len)
    d_sb = [flat(dq), flat(dk_sb), flat(dv_sb)]
    late_grads = {"w_out": dw_out.reshape(N_DEV, -1, d_model), "w_up": _col_parts(dw_up),
                  "w_down": dw_down.reshape(N_DEV, -1, d_model)}
    (dr, dkraw, dv, dlw, dasig, d_kk, d_ka, d_rk, d_gw, d_gb), late_parts = rwkv_bwd(
        *rwkv_in, kkr, kar, rkr, gwr, gbr, s0_all, by_seq(dz), n_seq, seq_len,
        Hosted([late_grads[n] for n in LATE], False))
    prep_cots = [flat(t) for t in (dr, dkraw, dv, dlw, dasig)] + [dgate]
    dp, (d_mu, d_w0, d_wd, d_a0, d_wa, d_wg) = rwkv_prep_bwd(p, prep_edges, prep_pars, prep_cots, d_sb, seq_len)

    dh1 = matmul("d_h1", dp, w_in_f, "nt", F32, 512, d_model, in_cols)
    dw_in = matmul("dw_in", h1, dp, "tn", BF16, d_model, in_cols // 2, 1024)
    (dx,), (d_ln1,) = rw_call("d_norm1", norm_bwd, [x2d, dh1, dx2], [g1], [(d_model, F32)],
                              [((1, d_model), F32)], 256)

    lora_parts = lambda t: _col_parts(t).astype(BF16)
    early_grads = {"w_in": _col_parts(dw_in), "w_decay_up": lora_parts(d_wd[:HEAD_DIM]),
                   "w_aaa_up": lora_parts(d_wa[HEAD_DIM:]), "w_gate_up": lora_parts(d_wg)}
    parts = dict(zip(EARLY, exchange("scatter_early", [early_grads[n] for n in EARLY], False)))
    parts.update(zip(LATE, late_parts))

    small_grads = {"ln1_g": d_ln1, "tok_mu": d_mu, "w0": d_w0, "a0": d_a0, "k_k": d_kk, "k_a": d_ka, "r_k": d_rk,
                   "gn_w": d_gw, "gn_b": d_gb, "sb_gain": d_sg, "ln2_g": d_ln2, "lnf_g": d_lnf}
    n_small = sum(int(weights[n].size) for n in SMALL)
    pack_rows = -(-(n_small + 1) // (8 * LANES)) * 8
    packed = _pack([small_grads[n] for n in SMALL] + [loss_acc[0, :1]], pack_rows)
    (small_parts,) = exchange("gather_small", [packed], True)

    results = {}
    for n in BIG:
        w2d = weights[n][0]
        tile = w2d.shape[0] if w2d.shape[0] <= 256 else 256
        results[n] = adamw("adamw_" + n, w2d, parts[n], mom_m[n][0], mom_v[n][0], tile)
    pk = lambda d: _pack([d[n] for n in SMALL] + [jnp.zeros((1,), F32)], pack_rows)
    sg, sd, sm, sv = adamw("adamw_small", pk(weights), small_parts, pk(mom_m), pk(mom_v), pack_rows)
    off = 0
    for n in SMALL:
        size = int(weights[n].size)
        results[n] = tuple(t.reshape(-1)[off:off + size] for t in (sg, sd, sm, sv))
        off += size
    loss = sg.reshape(-1)[off]

    out = [loss, dx.reshape(x.shape)]
    for kind in range(4):
        out += [results[n][kind].reshape(weights[n].shape) for n in ORDER]
    return tuple(out)
```

```python
import functools
import math

import jax
import jax.numpy as jnp
from jax import lax
from jax.experimental import pallas as pl
from jax.experimental.pallas import tpu as pltpu

F32 = jnp.float32
BF16 = jnp.bfloat16

N_DEV = 8
HEAD_DIM = 64
LANES = 128
RWKV_W = 512
SB_W = 512
LORA_WA = 128
GATE_LORA = 128
RWKV_COLS = 3 * RWKV_W + LORA_WA + GATE_LORA
RMS_EPS = 1e-5
GN_EPS = 64e-5
CHUNK = 64
QB = 128
SB_SCALE = HEAD_DIM ** -0.5
ADAM_LR, ADAM_B1, ADAM_B2, ADAM_EPS, ADAM_WD, ADAM_STEP = 0.001, 0.9, 0.999, 1e-08, 0.01, 10
VMEM_LIMIT = 56 * 1024 * 1024


_DIMS = {
    "nn": (((1,), (0,)), ((), ())),
    "nt": (((1,), (1,)), ((), ())),
    "tn": (((0,), (0,)), ((), ())),
}


def _pieces(x, n):
    if n == 1:
        return [x.astype(BF16)]
    out, rem = [], x.astype(F32)
    for i in range(n):
        p = rem.astype(BF16)
        out.append(p)
        if i + 1 < n:
            rem = rem - p.astype(F32)
    return out


def _dot(a, b, form, pa, pb):
    pieces_a, pieces_b = _pieces(a, pa), _pieces(b, pb)
    keep = max(pa, pb)
    acc = None
    for i, ai in enumerate(pieces_a):
        for j, bj in enumerate(pieces_b):
            if i + j >= keep:
                continue
            t = lax.dot_general(ai, bj, _DIMS[form], preferred_element_type=F32)
            acc = t if acc is None else acc + t
    return acc


BOTH = (True, True)


@functools.partial(jax.custom_vjp, nondiff_argnums=(2, 3, 4, 5))
def mm(a, b, form, pa, pb, diff=BOTH):
    return _dot(a, b, form, pa, pb)


def _mm_fwd(a, b, form, pa, pb, diff):
    return _dot(a, b, form, pa, pb), (a, b)


def _mm_bwd(form, pa, pb, diff, res, g):
    a, b = res
    pg = max(pa, pb)
    da, db = jnp.zeros_like(a), jnp.zeros_like(b)
    if form == "nn":
        if diff[0]:
            da = mm(g, b, "nt", pg, pb)
        if diff[1]:
            db = mm(a, g, "tn", pa, pg)
    elif form == "nt":
        if diff[0]:
            da = mm(g, b, "nn", pg, pb)
        if diff[1]:
            db = mm(g, a, "tn", pg, pa)
    else:
        if diff[0]:
            da = mm(b, g, "nt", pb, pg)
        if diff[1]:
            db = mm(a, g, "nn", pa, pg)
    return da, db


mm.defvjp(_mm_fwd, _mm_bwd)


def _lane_lo(shape):
    return lax.broadcasted_iota(jnp.int32, shape, len(shape) - 1) < HEAD_DIM


def _segsum(x):
    lo = _lane_lo(x.shape)
    s_lo = jnp.sum(jnp.where(lo, x, 0.0), axis=-1, keepdims=True)
    s_hi = jnp.sum(jnp.where(lo, 0.0, x), axis=-1, keepdims=True)
    return jnp.where(lo, s_lo, s_hi)


def _sigmoid(x):
    return 0.5 * (jnp.tanh(0.5 * x) + 1.0)


def _softplus(x):
    return jnp.maximum(x, 0.0) + jnp.log(1.0 + jnp.exp(-jnp.abs(x)))


def f_norm(x, g):
    return x * lax.rsqrt(jnp.mean(x * x, axis=-1, keepdims=True) + RMS_EPS) * g


def f_prep(p, pprev, mu, w0, wd_pad, a0, wa_pad, wg):
    pm = p + mu * (pprev - p)
    r = pm[:, 0:RWKV_W]
    k = pm[:, RWKV_W:2 * RWKV_W]
    v = pm[:, 2 * RWKV_W:3 * RWKV_W]
    xwa = pm[:, 3 * RWKV_W:3 * RWKV_W + LORA_WA]
    xg = pm[:, 3 * RWKV_W + LORA_WA:RWKV_COLS]
    w = -_softplus(-(w0 + mm(jnp.tanh(xwa), wd_pad, "nn", 1, 1))) - 0.5
    lw = -jnp.exp(w)
    asig = _sigmoid(a0 + mm(xwa, wa_pad, "nn", 1, 1))
    gate = mm(_sigmoid(xg), wg, "nn", 1, 1)
    return r, k, v, lw, asig, gate


def _tri(n, kind):
    row = lax.broadcasted_iota(jnp.int32, (n, n), 0)
    col = lax.broadcasted_iota(jnp.int32, (n, n), 1)
    if kind == "lower_incl":
        return row >= col
    return row > col


def rwkv_chunk(state, r, kraw, v, lw, asig, k_k, k_a, r_k, gn_w, gn_b, hp):
    n = len(r)
    L = r[0].shape[0]
    lo = _lane_lo((1, LANES))
    masks = (lo, jnp.logical_not(lo))
    incl = _tri(L, "lower_incl")
    strict = _tri(L, "strict")
    tri = incl.astype(F32)
    eye = (lax.broadcasted_iota(jnp.int32, (L, L), 0) == lax.broadcasted_iota(jnp.int32, (L, L), 1)).astype(F32)
    kk = [x * w for x, w in zip(kraw, k_k)]
    kk = [x / jnp.maximum(jnp.sqrt(_segsum(x * x)), 1e-12) for x in kk]
    k = [x * (1.0 + (s - 1.0) * w) for x, s, w in zip(kraw, asig, k_a)]
    b = [x * s for x, s in zip(kk, asig)]
    c = [mm(tri, x, "nn", 1, 3, (False, True)) for x in lw]
    at = [-x * jnp.exp(ci - li) for x, ci, li in zip(kk, c, lw)]
    rt = [x * jnp.exp(ci) for x, ci in zip(r, c)]
    einv = [jnp.exp(-ci) for ci in c]
    bt = [x * e for x, e in zip(b, einv)]
    kt = [x * e for x, e in zip(k, einv)]
    inst = [(s, m) for s in range(n) for m in masks]
    at_h = [jnp.where(m, at[s], 0.0) for s, m in inst]
    rt_h = [jnp.where(m, rt[s], 0.0) for s, m in inst]
    a_ab = [jnp.where(strict, mm(x, bt[s], "nt", hp, hp), 0.0) for x, (s, _) in zip(at_h, inst)]
    a_ak = [jnp.where(strict, mm(x, kt[s], "nt", hp, hp), 0.0) for x, (s, _) in zip(at_h, inst)]
    b_rb = [jnp.where(incl, mm(x, bt[s], "nt", hp, hp), 0.0) for x, (s, _) in zip(rt_h, inst)]
    b_rk = [jnp.where(incl, mm(x, kt[s], "nt", hp, hp), 0.0) for x, (s, _) in zip(rt_h, inst)]
    tinv = [eye + x for x in a_ab]
    pw = a_ab
    for _ in range(int(math.log2(L)) - 1):
        pw = [mm(x, x, "nn", hp, hp) for x in pw]
        tinv = [t + mm(t, x, "nn", hp, hp) for t, x in zip(tinv, pw)]
    rhs = [mm(x, state[s], "nt", hp, hp) + mm(y, v[s], "nn", hp, hp) for x, y, (s, _) in zip(at_h, a_ak, inst)]
    u_h = [mm(t, x, "nn", hp, hp) for t, x in zip(tinv, rhs)]
    y_h = [mm(x, state[s], "nt", hp, hp) + mm(m1, u, "nn", hp, hp) + mm(m2, v[s], "nn", hp, hp)
           for x, m1, m2, u, (s, _) in zip(rt_h, b_rb, b_rk, u_h, inst)]
    u_all = [jnp.where(lo, u_h[2 * s], u_h[2 * s + 1]) for s in range(n)]
    y_all = [jnp.where(lo, y_h[2 * s], y_h[2 * s + 1]) for s in range(n)]
    c_last = [jnp.sum(x, axis=0, keepdims=True) for x in lw]
    efwd = [jnp.exp(cl - ci) for cl, ci in zip(c_last, c)]
    new_state = [st * jnp.exp(cl) + mm(u, bi * e, "tn", hp, hp) + mm(vi, ki * e, "tn", hp, hp)
                 for st, cl, u, bi, e, vi, ki in zip(state, c_last, u_all, b, efwd, v, k)]
    row_head = lax.broadcasted_iota(jnp.int32, (LANES, LANES), 0) // HEAD_DIM
    col_head = lax.broadcasted_iota(jnp.int32, (LANES, LANES), 1) // HEAD_DIM
    new_state = [jnp.where(row_head == col_head, x, 0.0) for x in new_state]
    outs = []
    for y, ri, ki, vi, w_rk, w_gw, w_gb in zip(y_all, r, k, v, r_k, gn_w, gn_b):
        mean = _segsum(y) * (1.0 / HEAD_DIM)
        d = y - mean
        var = _segsum(d * d) * (1.0 / HEAD_DIM)
        yn = d * lax.rsqrt(var + GN_EPS) * w_gw + w_gb
        outs.append(yn + _segsum(ri * ki * w_rk) * vi)
    return outs, new_state


def sb_tile(q, k, v, c_lo, c_hi, diag, from_here=None):
    n = len(q)
    lo = _lane_lo((1, LANES))
    below = _tri(QB, "strict")
    from_s = _tri(QB, "lower_incl").astype(F32)
    inst = [(s, h) for s in range(n) for h in (0, 1)]
    carry = [(c_lo[s], c_hi[s])[h] for s, h in inst]
    z = [mm(q[s][h], k[s], "nt", 1, 1) for s, h in inst]
    soft = [jnp.log(1.0 + jnp.exp(-jnp.abs(x))) for x in z]
    log_keep = [-jnp.maximum(x, 0.0) - sp for x, sp in zip(z, soft)]
    if diag:
        log_keep = [jnp.where(below, x, 0.0) for x in log_keep]
    own = [jnp.sum(x, axis=1, keepdims=True) for x in log_keep]
    if from_here is not None:
        carry = [lax.stop_gradient(from_here[s][h] - o) + cr for (s, h), o, cr in zip(inst, own, carry)]
    tail = [mm(x, from_s, "nn", SB_SUM_PIECES, 1, (True, False)) for x in log_keep]
    log_a = [x + tl + cr for x, tl, cr in zip(z, tail, carry)]
    if diag:
        log_a = [jnp.where(below, x, -1e30) for x in log_a]
    att = [jnp.exp(x) for x in log_a]
    out_h = [mm(x, v[s], "nn", 1, 1) for x, (s, _) in zip(att, inst)]
    out = [jnp.where(lo, out_h[2 * s], out_h[2 * s + 1]) for s in range(n)]
    return out, [own[2 * s] for s in range(n)], [own[2 * s + 1] for s in range(n)]


def sb_split_q(q):
    lo = _lane_lo((1, LANES))
    qs = q * SB_SCALE
    return jnp.where(lo, qs, 0.0), jnp.where(lo, 0.0, qs)


def sb_post(o, gain):
    return o * lax.rsqrt(_segsum(o * o) * (1.0 / HEAD_DIM) + RMS_EPS) * gain


def f_final(x3, g, target):
    y = f_norm(x3, g)
    err = y - target
    return 0.5 * jnp.mean(err * err, axis=-1, keepdims=True)


def _params(sem):
    return pltpu.CompilerParams(dimension_semantics=sem, vmem_limit_bytes=VMEM_LIMIT)


def rw_call(name, body_fn, rows, pars, out_rows, out_accs, tile):
    rows = [item if isinstance(item, tuple) else (item, item.shape[1], 0) for item in rows]
    row_arrays = [arr for arr, _, _ in rows]
    n_rows = row_arrays[0].shape[0]
    tile = min(tile, n_rows)
    steps = n_rows // tile
    row_specs = [pl.BlockSpec((tile, cols), functools.partial(lambda i, c: (i, c), c=cblk)) for _, cols, cblk in rows]
    par_specs = [pl.BlockSpec(p.shape, lambda i: (0, 0)) for p in pars]
    nr, npar, nor, noa = len(row_arrays), len(pars), len(out_rows), len(out_accs)

    def body(*refs):
        row_vals = [refs[i][...] for i in range(nr)]
        par_vals = [refs[nr + i][...] for i in range(npar)]
        o_refs = refs[nr + npar:nr + npar + nor]
        a_refs = refs[nr + npar + nor:]
        row_outs, acc_outs = body_fn(row_vals, par_vals)
        for ref, val in zip(o_refs, row_outs):
            ref[...] = val.astype(ref.dtype)
        if noa:
            first = pl.program_id(0) == 0

            @pl.when(first)
            def _():
                for ref, val in zip(a_refs, acc_outs):
                    ref[...] = val.astype(ref.dtype)

            @pl.when(jnp.logical_not(first))
            def _():
                for ref, val in zip(a_refs, acc_outs):
                    ref[...] = ref[...] + val.astype(ref.dtype)

    out_shape = [jax.ShapeDtypeStruct((n_rows, c), dt) for c, dt in out_rows]
    out_shape += [jax.ShapeDtypeStruct(s, dt) for s, dt in out_accs]
    out_specs = [pl.BlockSpec((tile, c), lambda i: (i, 0)) for c, _ in out_rows]
    out_specs += [pl.BlockSpec(s, lambda i: (0, 0)) for s, _ in out_accs]
    outs = pl.pallas_call(
        body, name=name, grid=(steps,), in_specs=row_specs + par_specs, out_specs=out_specs,
        out_shape=out_shape, compiler_params=_params(("arbitrary",)),
    )(*row_arrays, *pars)
    return outs[:nor], outs[nor:]


def matmul(name, a, b, form, out_dtype, tm, tn, tk, extras=(), epilogue=None):
    out_dtypes = out_dtype if isinstance(out_dtype, tuple) else (out_dtype,)
    tm, tn, tk = min(tm, a.shape[1 if form == "tn" else 0]), min(tn, b.shape[0 if form == "nt" else 1]), min(tk, a.shape[0 if form == "tn" else 1])
    if form == "nn":
        (m, kd), n = a.shape, b.shape[1]
        a_spec = pl.BlockSpec((tm, tk), lambda i, j, k: (i, k))
        b_spec = pl.BlockSpec((tk, tn), lambda i, j, k: (k, j))
    elif form == "nt":
        (m, kd), n = a.shape, b.shape[0]
        a_spec = pl.BlockSpec((tm, tk), lambda i, j, k: (i, k))
        b_spec = pl.BlockSpec((tn, tk), lambda i, j, k: (j, k))
    else:
        (kd, m), n = a.shape, b.shape[1]
        a_spec = pl.BlockSpec((tk, tm), lambda i, j, k: (k, i))
        b_spec = pl.BlockSpec((tk, tn), lambda i, j, k: (k, j))
    ksteps = kd // tk

    n_extra, n_out = len(extras), len(out_dtypes)

    def body(a_ref, b_ref, *rest):
        e_refs, o_refs = rest[:n_extra], rest[n_extra:n_extra + n_out]
        kstep = pl.program_id(2)
        part = lax.dot_general(a_ref[...].astype(BF16), b_ref[...].astype(BF16), _DIMS[form],
                               preferred_element_type=F32)

        def finish(acc):
            outs = epilogue(acc, *[r[...] for r in e_refs]) if epilogue else (acc,)
            for ref, val in zip(o_refs, outs):
                ref[...] = val.astype(ref.dtype)

        if ksteps == 1:
            finish(part)
            return
        acc_ref = rest[-1]

        @pl.when(kstep == 0)
        def _():
            acc_ref[...] = part

        @pl.when(jnp.logical_and(kstep > 0, kstep < ksteps - 1))
        def _():
            acc_ref[...] = acc_ref[...] + part

        @pl.when(kstep == ksteps - 1)
        def _():
            finish(acc_ref[...] + part)

    out_spec = pl.BlockSpec((tm, tn), lambda i, j, k: (i, j))
    outs = pl.pallas_call(
        body, name=name, grid=(m // tm, n // tn, ksteps), in_specs=[a_spec, b_spec] + [out_spec] * n_extra,
        out_specs=[out_spec] * n_out,
        out_shape=[jax.ShapeDtypeStruct((m, n), dt) for dt in out_dtypes],
        scratch_shapes=[pltpu.VMEM((tm, tn), F32)] if ksteps > 1 else [],
        compiler_params=_params(("parallel", "parallel", "arbitrary")),
    )(a, b, *extras)
    return outs if isinstance(out_dtype, tuple) else outs[0]


PREP_TILE = 256
PREP_TILE_BWD = 128
SUBLANES = 8


def _shift_in(rows, first):
    rolled = pltpu.roll(rows, 1, 0)
    row = lax.broadcasted_iota(jnp.int32, (SUBLANES, rows.shape[1]), 0)
    head = jnp.where(row == 0, first, rolled[0:SUBLANES])
    return jnp.concatenate([head, rolled[SUBLANES:]], axis=0), rolled


def rwkv_prep_fwd(p, pars, seq_len):
    n_tok = p.shape[0]
    tile = min(PREP_TILE, seq_len)
    tile_b = min(PREP_TILE_BWD, tile)
    steps, per_seq, sub = n_tok // tile, seq_len // tile, tile // tile_b
    n_par = len(pars)

    def body(p_ref, *rest):
        par_refs, out_refs, edge_ref, last8 = rest[:n_par], rest[n_par:n_par + 6], rest[n_par + 6], rest[n_par + 7]
        step = pl.program_id(0)

        @pl.when(step == 0)
        def _():
            last8[...] = jnp.zeros_like(last8)

        rows = p_ref[...]
        before = jnp.where(step % per_seq == 0, 0.0, pltpu.roll(last8[...], 1, 0))
        prev, rolled = _shift_in(rows, before)
        edge_ref[0] = prev[0:SUBLANES]
        for m in range(1, sub):
            edge_ref[m] = rolled[m * tile_b:m * tile_b + SUBLANES]
        last8[...] = rows[tile - SUBLANES:tile]
        for ref, val in zip(out_refs, f_prep(rows, prev, *[r[...] for r in par_refs])):
            ref[...] = val

    row_out = pl.BlockSpec((tile, RWKV_W), lambda i: (i, 0))
    outs = pl.pallas_call(
        body, name="rwkv_prep", grid=(steps,),
        in_specs=[pl.BlockSpec((tile, RWKV_COLS), lambda i: (i, 0))] + [pl.BlockSpec(t.shape, lambda i: (0, 0)) for t in pars],
        out_specs=[row_out] * 6 + [pl.BlockSpec((sub, SUBLANES, RWKV_COLS), lambda i: (i, 0, 0))],
        out_shape=[jax.ShapeDtypeStruct((n_tok, RWKV_W), F32)] * 6
        + [jax.ShapeDtypeStruct((steps * sub, SUBLANES, RWKV_COLS), F32)],
        scratch_shapes=[pltpu.VMEM((SUBLANES, RWKV_COLS), F32)],
        compiler_params=_params(("arbitrary",)),
    )(p, *pars)
    return outs[:6], outs[6]


def rwkv_prep_bwd(p, edges, pars, cots, d_sb, seq_len):
    n_tok = p.shape[0]
    tile = min(PREP_TILE_BWD, seq_len)
    steps, per_seq = n_tok // tile, seq_len // tile
    n_par = len(pars)
    back = lambda i: steps - 1 - i

    def body(p_ref, edge_ref, *rest):
        par_refs, rest = rest[:n_par], rest[n_par:]
        cot_refs, sb_refs, dp_ref, acc_refs, next8 = rest[:6], rest[6:9], rest[9], rest[10:10 + n_par], rest[10 + n_par]
        step = pl.program_id(0)
        first = step == 0

        @pl.when(first)
        def _():
            next8[...] = jnp.zeros_like(next8)

        rows = p_ref[...]
        prev, _ = _shift_in(rows, edge_ref[0])
        _, vjp = jax.vjp(f_prep, rows, prev, *[r[...].astype(F32) for r in par_refs])
        grads = vjp(tuple(r[...] for r in cot_refs))
        d_rows, d_prev = grads[0], grads[1]
        up = pltpu.roll(d_prev, tile - 1, 0)
        ends_seq = back(step) % per_seq == per_seq - 1
        after = jnp.where(ends_seq, 0.0, pltpu.roll(next8[...], SUBLANES - 1, 0))
        row = lax.broadcasted_iota(jnp.int32, (SUBLANES, RWKV_COLS), 0)
        tail = jnp.where(row == SUBLANES - 1, after, up[tile - SUBLANES:tile])
        d_rows = d_rows + jnp.concatenate([up[:tile - SUBLANES], tail], axis=0)
        next8[...] = d_prev[0:SUBLANES]
        dp_ref[...] = jnp.concatenate([d_rows] + [r[...] for r in sb_refs], axis=1).astype(dp_ref.dtype)

        @pl.when(first)
        def _():
            for ref, val in zip(acc_refs, grads[2:]):
                ref[...] = val

        @pl.when(jnp.logical_not(first))
        def _():
            for ref, val in zip(acc_refs, grads[2:]):
                ref[...] = ref[...] + val

    cols = RWKV_COLS + sum(t.shape[1] for t in d_sb)
    half = pl.BlockSpec((tile, RWKV_W), lambda i: (back(i), 0))
    par_specs = [pl.BlockSpec(t.shape, lambda i: (0, 0)) for t in pars]
    outs = pl.pallas_call(
        body, name="d_rwkv_prep", grid=(steps,),
        in_specs=[pl.BlockSpec((tile, RWKV_COLS), lambda i: (back(i), 0)),
                  pl.BlockSpec((1, SUBLANES, RWKV_COLS), lambda i: (back(i), 0, 0))] + par_specs + [half] * 9,
        out_specs=[pl.BlockSpec((tile, cols), lambda i: (back(i), 0))] + par_specs,
        out_shape=[jax.ShapeDtypeStruct((n_tok, cols), BF16)] + [jax.ShapeDtypeStruct(t.shape, F32) for t in pars],
        scratch_shapes=[pltpu.VMEM((SUBLANES, RWKV_COLS), F32)],
        compiler_params=_params(("arbitrary",)),
    )(p, edges, *pars, *cots, *d_sb)
    return outs[0], outs[1:]


RWKV_HP = 1


RWKV_PAIRS = 4


def _rwkv_specs(n_seq, chunk_of):
    width = RWKV_PAIRS * LANES
    row = pl.BlockSpec((n_seq, CHUNK, width), lambda g, c: (0, chunk_of(c), g))
    par = pl.BlockSpec((1, width), lambda g, c: (0, g))
    s0 = pl.BlockSpec((1, 1, RWKV_PAIRS * n_seq, LANES, LANES), lambda g, c: (g, chunk_of(c), 0, 0, 0))
    return row, par, s0


class Hosted:
    def __init__(self, arrays, gather):
        self.arrays, self.gather, self.n = list(arrays), gather, len(arrays)

    def split(self, refs, n_in, n_out):
        n = self.n
        ins, outs, sems = refs[n_in:n_in + n], refs[n_in + n + n_out:n_in + 2 * n + n_out], refs[-3:]
        own = refs[:n_in] + refs[n_in + n:n_in + n + n_out] + refs[n_in + 2 * n + n_out:-3]
        return own, exchange_copies(ins, outs, *sems, self.gather)

    def run(self, copies, first, last):
        @pl.when(first)
        def _():
            for cp in copies:
                cp.start()

        @pl.when(last)
        def _():
            for cp in copies:
                cp.wait()


def rwkv_fwd(r, kraw, v, lw, asig, k_k, k_a, r_k, gn_w, gn_b, n_seq, seq_len, hosted):
    n_chunks = seq_len // CHUNK
    n_groups = RWKV_W // (RWKV_PAIRS * LANES)
    n_inst = RWKV_PAIRS * n_seq
    row, par, s0_spec = _rwkv_specs(n_seq, lambda c: c)
    inst = [(s, pl.ds(pp * LANES, LANES)) for pp in range(RWKV_PAIRS) for s in range(n_seq)]

    def body(*refs):
        own, copies = hosted.split(refs, 10, 2)
        row_refs, par_refs, (z_ref, s0_ref, state) = own[:5], own[5:10], own[10:]
        step = pl.program_id(0) * n_chunks + pl.program_id(1)
        hosted.run(copies, step == 0, step == n_groups * n_chunks - 1)

        @pl.when(pl.program_id(1) == 0)
        def _():
            state[...] = jnp.zeros_like(state)

        s0 = [state[i] for i in range(n_inst)]
        rows = [[ref[s, :, lanes] for s, lanes in inst] for ref in row_refs]
        pars = [[ref[:, lanes] for _, lanes in inst] for ref in par_refs]
        z, s1 = rwkv_chunk(s0, *rows, *pars, RWKV_HP)
        for i, (s, lanes) in enumerate(inst):
            s0_ref[0, 0, i] = s0[i]
            z_ref[s, :, lanes] = z[i]
            state[i] = s1[i]

    outs = pl.pallas_call(
        body, name="rwkv_fwd", grid=(n_groups, n_chunks),
        in_specs=[row] * 5 + [par] * 5 + [ANY_SPEC] * hosted.n, out_specs=[row, s0_spec] + [ANY_SPEC] * hosted.n,
        out_shape=[jax.ShapeDtypeStruct(r.shape, F32),
                   jax.ShapeDtypeStruct((n_groups, n_chunks, n_inst, LANES, LANES), F32)]
        + exchange_shapes(hosted.arrays, hosted.gather),
        scratch_shapes=[pltpu.VMEM((n_inst, LANES, LANES), F32)] + exchange_sems(hosted.n),
        compiler_params=_params(("arbitrary", "arbitrary")),
    )(r, kraw, v, lw, asig, k_k, k_a, r_k, gn_w, gn_b, *hosted.arrays)
    return outs[0], outs[1], outs[2:]


def rwkv_bwd(r, kraw, v, lw, asig, k_k, k_a, r_k, gn_w, gn_b, s0_all, dz, n_seq, seq_len, hosted):
    n_chunks = seq_len // CHUNK
    n_groups = RWKV_W // (RWKV_PAIRS * LANES)
    n_inst = RWKV_PAIRS * n_seq
    row, par, s0_spec = _rwkv_specs(n_seq, lambda c: n_chunks - 1 - c)
    inst = [(s, pl.ds(pp * LANES, LANES)) for pp in range(RWKV_PAIRS) for s in range(n_seq)]

    def body(*refs):
        own, copies = hosted.split(refs, 12, 10)
        row_refs, par_refs, s0_ref, dz_ref = own[:5], own[5:10], own[10], own[11]
        drow_refs, dpar_refs, dstate = own[12:17], own[17:22], own[22]
        step = pl.program_id(0) * n_chunks + pl.program_id(1)
        hosted.run(copies, step == 0, step == n_groups * n_chunks - 1)
        first = pl.program_id(1) == 0

        @pl.when(first)
        def _():
            dstate[...] = jnp.zeros_like(dstate)

        fn = functools.partial(rwkv_chunk, hp=RWKV_HP)
        rows = [[ref[s, :, lanes] for s, lanes in inst] for ref in row_refs]
        pars = [[ref[:, lanes] for _, lanes in inst] for ref in par_refs]
        _, vjp = jax.vjp(fn, [s0_ref[0, 0, i] for i in range(n_inst)], *rows, *pars)
        grads = vjp(([dz_ref[s, :, lanes] for s, lanes in inst], [dstate[i] for i in range(n_inst)]))
        for i, (s, lanes) in enumerate(inst):
            dstate[i] = grads[0][i]
            for ref, val in zip(drow_refs, grads[1:6]):
                ref[s, :, lanes] = val[i]

        def accumulate(start):
            for ref, val in zip(dpar_refs, grads[6:]):
                for pp in range(RWKV_PAIRS):
                    lanes = pl.ds(pp * LANES, LANES)
                    total = functools.reduce(jnp.add, val[pp * n_seq:(pp + 1) * n_seq])
                    ref[:, lanes] = total if start else ref[:, lanes] + total

        @pl.when(first)
        def _():
            accumulate(True)

        @pl.when(jnp.logical_not(first))
        def _():
            accumulate(False)

    rows_shape = jax.ShapeDtypeStruct(r.shape, F32)
    par_shape = jax.ShapeDtypeStruct((1, RWKV_W), F32)
    outs = pl.pallas_call(
        body, name="rwkv_bwd", grid=(n_groups, n_chunks),
        in_specs=[row] * 5 + [par] * 5 + [s0_spec, row] + [ANY_SPEC] * hosted.n,
        out_specs=[row] * 5 + [par] * 5 + [ANY_SPEC] * hosted.n,
        out_shape=[rows_shape] * 5 + [par_shape] * 5 + exchange_shapes(hosted.arrays, hosted.gather),
        scratch_shapes=[pltpu.VMEM((n_inst, LANES, LANES), F32)] + exchange_sems(hosted.n),
        compiler_params=_params(("arbitrary", "arbitrary")),
    )(r, kraw, v, lw, asig, k_k, k_a, r_k, gn_w, gn_b, s0_all, dz, *hosted.arrays)
    return outs[:10], outs[10:]


SB_Q0 = RWKV_COLS // LANES
SB_K0 = SB_Q0 + SB_W // LANES
SB_V0 = SB_K0 + SB_W // LANES
SB_SEQS = 4
SB_BUFFERS = pl.Buffered(1)
SB_SUM_PIECES = 2
SB_DEAD = -110.0


def _col_of(c_lo, c_hi):
    return jnp.where(_lane_lo((1, LANES)), c_lo, c_hi)


def sb_fwd(p, gain, n_seq, seq_len):
    n_pairs = SB_W // LANES
    n_q = seq_len // QB
    nb = min(SB_SEQS, n_seq)

    def seq_spec(c0):
        return pl.BlockSpec((nb, seq_len, LANES), functools.partial(lambda b, h, c0: (b, 0, c0 + h), c0=c0),
                            pipeline_mode=SB_BUFFERS)

    out_spec = pl.BlockSpec((nb, seq_len, LANES), lambda b, h: (b, 0, h), pipeline_mode=SB_BUFFERS)

    def body(q_ref, k_ref, v_ref, g_ref, y_ref, o_ref, tot_ref, first_ref):
        gain = g_ref[...]

        def q_block(i, _):
            qs = pl.multiple_of(i * QB, QB)
            seqs = range(nb)
            zeros = [jnp.zeros((QB, 1), F32)] * nb
            qv = [sb_split_q(q_ref[s, pl.ds(qs, QB), :]) for s in seqs]
            add = lambda xs, ys: [x + y for x, y in zip(xs, ys)]

            def tiles(ks, c_lo, c_hi, diag):
                return sb_tile(qv, [k_ref[s, pl.ds(ks, QB), :] for s in seqs],
                               [v_ref[s, pl.ds(ks, QB), :] for s in seqs], c_lo, c_hi, diag)

            def alive(c_lo, c_hi):
                top = functools.reduce(jnp.maximum, list(c_lo) + list(c_hi))
                return jnp.max(top) > SB_DEAD

            def k_block(state):
                j, _, (o, c_lo, c_hi) = state
                o2, s_lo, s_hi = tiles(pl.multiple_of(j * QB, QB), c_lo, c_hi, False)
                c_lo, c_hi = add(c_lo, s_lo), add(c_hi, s_hi)
                return j - 1, alive(c_lo, c_hi), (add(o, o2), c_lo, c_hi)

            o, c_lo, c_hi = tiles(qs, zeros, zeros, True)
            j, _, (o, c_lo, c_hi) = lax.while_loop(lambda st: jnp.logical_and(st[0] >= 0, st[1]), k_block,
                                                   (i - 1, alive(c_lo, c_hi), (o, c_lo, c_hi)))
            first_ref[pl.program_id(0), pl.program_id(1), i] = j + 1
            for s in seqs:
                o_ref[s, pl.ds(qs, QB), :] = o[s]
                tot_ref[s, pl.ds(qs, QB), :] = jnp.broadcast_to(_col_of(c_lo[s], c_hi[s]), (QB, LANES))
                y_ref[s, pl.ds(qs, QB), :] = sb_post(o[s], gain)
            return 0

        lax.fori_loop(0, n_q, q_block, 0)

    shape = jax.ShapeDtypeStruct((n_seq, seq_len, SB_W), F32)
    return pl.pallas_call(
        body, name="sb_fwd", grid=(n_seq // nb, n_pairs),
        in_specs=[seq_spec(SB_Q0), seq_spec(SB_K0), seq_spec(SB_V0), pl.BlockSpec((1, LANES), lambda b, h: (0, h))],
        out_specs=[out_spec] * 3 + [pl.BlockSpec(memory_space=pltpu.SMEM)],
        out_shape=[shape] * 3 + [jax.ShapeDtypeStruct((n_seq // nb, n_pairs, n_q), jnp.int32)],
        compiler_params=_params(("arbitrary", "arbitrary")),
    )(p, p, p, gain)


def sb_bwd(p, gain, o_raw, tot, dy, first, n_seq, seq_len):
    n_pairs = SB_W // LANES
    n_q = seq_len // QB
    nb = min(SB_SEQS, n_seq)

    def seq_spec(c0):
        return pl.BlockSpec((nb, seq_len, LANES), functools.partial(lambda h, b, c0: (b, 0, c0 + h), c0=c0),
                            pipeline_mode=SB_BUFFERS)

    own = pl.BlockSpec((nb, seq_len, LANES), lambda h, b: (b, 0, h), pipeline_mode=SB_BUFFERS)
    par = pl.BlockSpec((1, LANES), lambda h, b: (0, h))

    def body(q_ref, k_ref, v_ref, g_ref, o_ref, tot_ref, dy_ref, first_ref, dq_ref, dk_ref, dv_ref, dg_ref):
        gain = g_ref[...]
        lo = _lane_lo((1, LANES))
        dk_ref[...] = jnp.zeros_like(dk_ref)
        dv_ref[...] = jnp.zeros_like(dv_ref)

        def q_block(i, dgain):
            qs = pl.multiple_of(i * QB, QB)
            seqs = range(nb)
            zeros = [jnp.zeros((QB, 1), F32)] * nb
            qv, dov, t_lo, t_hi = [], [], [], []
            for s in seqs:
                qv.append(sb_split_q(q_ref[s, pl.ds(qs, QB), :]))
                _, post_vjp = jax.vjp(sb_post, o_ref[s, pl.ds(qs, QB), :], gain)
                do, dg_s = post_vjp(dy_ref[s, pl.ds(qs, QB), :])
                dov.append(do)
                dgain = dgain + dg_s
                tot_s = tot_ref[s, pl.ds(qs, QB), :]
                t_lo.append(jnp.max(jnp.where(lo, tot_s, -jnp.inf), axis=1, keepdims=True))
                t_hi.append(jnp.max(jnp.where(lo, -jnp.inf, tot_s), axis=1, keepdims=True))
            add = lambda xs, ys: [x + y for x, y in zip(xs, ys)]
            sub = lambda xs, ys: [x - y for x, y in zip(xs, ys)]

            def tile(ks, carry, diag):
                dq, rem_lo, rem_hi, g_lo, g_hi = carry
                kv = [k_ref[s, pl.ds(ks, QB), :] for s in seqs]
                vv = [v_ref[s, pl.ds(ks, QB), :] for s in seqs]
                fn = functools.partial(sb_tile, diag=diag, from_here=list(zip(rem_lo, rem_hi)))
                (_, s_lo, s_hi), vjp = jax.vjp(fn, qv, kv, vv, zeros, zeros)
                dq_t, dk_t, dv_t, dc_lo, dc_hi = vjp((dov, g_lo, g_hi))
                dq_t = [jnp.where(lo, d_lo, d_hi) for d_lo, d_hi in dq_t]
                for s in seqs:
                    dk_ref[s, pl.ds(ks, QB), :] = dk_ref[s, pl.ds(ks, QB), :] + dk_t[s]
                    dv_ref[s, pl.ds(ks, QB), :] = dv_ref[s, pl.ds(ks, QB), :] + dv_t[s]
                return add(dq, dq_t), sub(rem_lo, s_lo), sub(rem_hi, s_hi), add(g_lo, dc_lo), add(g_hi, dc_hi)

            def k_block(j, carry):
                return tile(pl.multiple_of(j * QB, QB), carry, False)

            carry = ([jnp.zeros((QB, LANES), F32)] * nb, t_lo, t_hi, zeros, zeros)
            carry = lax.fori_loop(first_ref[pl.program_id(1), pl.program_id(0), i], i, k_block, carry)
            carry = tile(qs, carry, True)
            for s in seqs:
                dq_ref[s, pl.ds(qs, QB), :] = carry[0][s] * SB_SCALE
            return dgain

        dgain = lax.fori_loop(0, n_q, q_block, jnp.zeros((1, LANES), F32))
        first = pl.program_id(1) == 0

        @pl.when(first)
        def _():
            dg_ref[...] = dgain

        @pl.when(jnp.logical_not(first))
        def _():
            dg_ref[...] = dg_ref[...] + dgain

    shape = jax.ShapeDtypeStruct((n_seq, seq_len, SB_W), F32)
    return pl.pallas_call(
        body, name="sb_bwd", grid=(n_pairs, n_seq // nb),
        in_specs=[seq_spec(SB_Q0), seq_spec(SB_K0), seq_spec(SB_V0), par, own, own, own,
                  pl.BlockSpec(memory_space=pltpu.SMEM)],
        out_specs=[own, own, own, par],
        out_shape=[shape, shape, shape, jax.ShapeDtypeStruct((1, SB_W), F32)],
        compiler_params=_params(("arbitrary", "arbitrary")),
    )(p, p, p, gain, o_raw, tot, dy, first)


def exchange(name, arrays, gather):
    n = len(arrays)

    def body(*refs):
        copies = exchange_copies(refs[:n], refs[n:2 * n], *refs[2 * n:], gather)
        for cp in copies:
            cp.start()
        for cp in copies:
            cp.wait()

    return pl.pallas_call(
        body, name=name, in_specs=[ANY_SPEC] * n, out_specs=[ANY_SPEC] * n, out_shape=exchange_shapes(arrays, gather),
        scratch_shapes=exchange_sems(n),
    )(*arrays)


ANY_SPEC = pl.BlockSpec(memory_space=pl.ANY)


def exchange_shapes(arrays, gather):
    return [jax.ShapeDtypeStruct(((N_DEV,) + a.shape) if gather else a.shape, a.dtype) for a in arrays]


def exchange_sems(n):
    return [pltpu.SemaphoreType.DMA((n, N_DEV - 1)), pltpu.SemaphoreType.DMA((n, N_DEV - 1)),
            pltpu.SemaphoreType.DMA((n,))]


def exchange_copies(ins, outs, send_sems, recv_sems, local_sems, gather):
    x, y, c = lax.axis_index("x"), lax.axis_index("y"), lax.axis_index("c")
    me = 4 * x + 2 * y + c
    copies = []
    for a, (src_all, dst_all) in enumerate(zip(ins, outs)):
        own = src_all if gather else src_all.at[me]
        copies.append(pltpu.make_async_copy(own, dst_all.at[me], local_sems.at[a]))
        for j in range(1, N_DEV):
            px, py, pc = (x + (j >> 2)) % 2, (y + ((j >> 1) & 1)) % 2, (c + (j & 1)) % 2
            src = src_all if gather else src_all.at[4 * px + 2 * py + pc]
            copies.append(pltpu.make_async_remote_copy(
                src_ref=src, dst_ref=dst_all.at[me], send_sem=send_sems.at[a, j - 1],
                recv_sem=recv_sems.at[a, j - 1], device_id=(px, py, pc), device_id_type=pl.DeviceIdType.MESH))
    return copies


def adamw(name, w, parts, m, v, tile):
    rows, cols = w.shape
    spec = pl.BlockSpec((tile, cols), lambda i: (i, 0))
    part_spec = pl.BlockSpec((N_DEV, tile, cols), lambda i: (0, i, 0))

    def body(w_ref, p_ref, m_ref, v_ref, g_ref, d_ref, nm_ref, nv_ref):
        g = p_ref[0].astype(F32)
        for s in range(1, N_DEV):
            g = g + p_ref[s].astype(F32)
        new_m = ADAM_B1 * m_ref[...] + (1.0 - ADAM_B1) * g
        new_v = ADAM_B2 * v_ref[...] + (1.0 - ADAM_B2) * (g * g)
        m_hat = new_m / (1.0 - ADAM_B1 ** ADAM_STEP)
        v_hat = new_v / (1.0 - ADAM_B2 ** ADAM_STEP)
        g_ref[...] = g
        d_ref[...] = -ADAM_LR * (m_hat / (jnp.sqrt(v_hat) + ADAM_EPS) + ADAM_WD * w_ref[...])
        nm_ref[...] = new_m
        nv_ref[...] = new_v

    shape = jax.ShapeDtypeStruct((rows, cols), F32)
    return pl.pallas_call(
        body, name=name, grid=(rows // tile,), in_specs=[spec, part_spec, spec, spec],
        out_specs=[spec] * 4, out_shape=[shape] * 4, compiler_params=_params(("arbitrary",)),
    )(w, parts, m, v)


SMALL = ("ln1_g", "tok_mu", "w0", "a0", "k_k", "k_a", "r_k", "gn_w", "gn_b", "sb_gain", "ln2_g", "lnf_g")
EARLY = ("w_in", "w_decay_up", "w_aaa_up", "w_gate_up")
LATE = ("w_out", "w_up", "w_down")
BIG = EARLY + LATE
COL_SHARDED = ("w_in", "w_decay_up", "w_aaa_up", "w_gate_up", "w_up")
ORDER = ("ln1_g", "w_in", "tok_mu", "w0", "w_decay_up", "a0", "w_aaa_up", "w_gate_up", "k_k", "k_a", "r_k",
         "gn_w", "gn_b", "sb_gain", "w_out", "ln2_g", "w_up", "w_down", "lnf_g")


def _pack(vectors, rows):
    flat = jnp.concatenate([v.reshape(-1).astype(F32) for v in vectors])
    return jnp.pad(flat, (0, rows * LANES - flat.shape[0])).reshape(rows, LANES)


def _full_cols(gathered):
    d, k, cols = gathered.shape
    return jnp.transpose(gathered, (1, 0, 2)).reshape(k, d * cols)


def _col_parts(full):
    k, n = full.shape
    return jnp.transpose(full.reshape(k, N_DEV, n // N_DEV), (1, 0, 2))


def kernel(x, ln1_g, w_in, tok_mu, w0, w_decay_up, a0, w_aaa_up, w_gate_up, k_k, k_a, r_k, gn_w, gn_b, sb_gain, w_out, ln2_g, w_up, w_down, lnf_g, loss_target, m_ln1_g, m_w_in, m_tok_mu, m_w0, m_w_decay_up, m_a0, m_w_aaa_up, m_w_gate_up, m_k_k, m_k_a, m_r_k, m_gn_w, m_gn_b, m_sb_gain, m_w_out, m_ln2_g, m_w_up, m_w_down, m_lnf_g, v_ln1_g, v_w_in, v_tok_mu, v_w0, v_w_decay_up, v_a0, v_w_aaa_up, v_w_gate_up, v_k_k, v_k_a, v_r_k, v_gn_w, v_gn_b, v_sb_gain, v_w_out, v_ln2_g, v_w_up, v_w_down, v_lnf_g):
    args = dict(locals())
    weights = {n: args[n] for n in ORDER}
    mom_m = {n: args["m_" + n] for n in ORDER}
    mom_v = {n: args["v_" + n] for n in ORDER}

    n_seq, seq_len, d_model = x.shape
    n_tok = n_seq * seq_len
    x2d = x.reshape(n_tok, d_model)
    tgt = loss_target.reshape(n_tok, d_model)
    row = lambda t: t.reshape(1, -1).astype(F32)

    shard = {n: weights[n][0].astype(BF16) for n in BIG}
    gathered = dict(zip(EARLY, exchange("gather_early", [shard[n] for n in EARLY], True)))
    w_in_f = _full_cols(gathered["w_in"])
    zeros64 = jnp.zeros((HEAD_DIM, RWKV_W), BF16)
    wd_pad = jnp.concatenate([_full_cols(gathered["w_decay_up"]), zeros64], axis=0)
    wa_pad = jnp.concatenate([zeros64, _full_cols(gathered["w_aaa_up"])], axis=0)
    wg_f = _full_cols(gathered["w_gate_up"])
    in_cols = w_in_f.shape[1]

    g1, mu, w0r, a0r = row(ln1_g), row(tok_mu), row(w0), row(a0)
    kkr, kar, rkr, gwr, gbr, sgr = row(k_k), row(k_a), row(r_k), row(gn_w), row(gn_b), row(sb_gain)
    g2, gf = row(ln2_g), row(lnf_g)

    (h1,), _ = rw_call("norm1", lambda r, p: ([f_norm(r[0], p[0])], []), [x2d], [g1], [(d_model, BF16)], [], 512)
    p = matmul("proj_in", h1, w_in_f, "nn", F32, 512, in_cols // 2, d_model)
    prep_pars = [mu, w0r, wd_pad, a0r, wa_pad, wg_f]
    (r_, kraw, v_, lw, asig, gate), prep_edges = rwkv_prep_fwd(p, prep_pars, seq_len)
    by_seq = lambda t: t.reshape(n_seq, seq_len, t.shape[-1])
    flat = lambda t: t.reshape(n_tok, t.shape[-1])
    rwkv_in = [by_seq(t) for t in (r_, kraw, v_, lw, asig)]
    z, s0_all, late = rwkv_fwd(*rwkv_in, kkr, kar, rkr, gwr, gbr, n_seq, seq_len,
                               Hosted([shard[n] for n in LATE], True))
    gathered = dict(zip(LATE, late))
    w_out_f = gathered["w_out"].reshape(d_model, d_model)
    w_up_f = _full_cols(gathered["w_up"])
    w_down_f = gathered["w_down"].reshape(-1, d_model)
    d_ff = w_up_f.shape[1]
    z = flat(z)
    y_sb, o_raw, tot, sb_first = sb_fwd(by_seq(p), sgr, n_seq, seq_len)
    y_sb = flat(y_sb)
    (ycat,), _ = rw_call("mix_cat", lambda r, q: ([jnp.concatenate([r[0] * r[1], r[2]], axis=1)], []),
                         [z, gate, y_sb], [], [(d_model, BF16)], [], 512)
    mix = matmul("proj_out", ycat, w_out_f, "nn", F32, 512, d_model, d_model)
    (x2, h2), _ = rw_call("resid_norm2", lambda r, q: ([r[0] + r[1], f_norm(r[0] + r[1], q[0])], []),
                          [x2d, mix], [g2], [(d_model, F32), (d_model, BF16)], [], 512)
    u, act = matmul("mlp_up", h2, w_up_f, "nn", (F32, BF16), 512, d_ff // 4, d_model,
                    epilogue=lambda acc: (acc, jnp.square(jnp.maximum(acc, 0.0))))
    mlp = matmul("mlp_down", act, w_down_f, "nn", F32, 512, d_model, d_ff)

    def final_body(r, q):
        x3 = r[0] + r[1]
        loss_rows, vjp = jax.vjp(lambda a, g: f_final(a, g, r[2]), x3, q[0])
        dx3, dgf = vjp(jnp.ones_like(loss_rows))
        return [dx3, dx3], [jnp.broadcast_to(jnp.sum(loss_rows), (1, LANES)), dgf]

    (dx3, dx3_b), (loss_acc, d_lnf) = rw_call("loss_head", final_body, [x2, mlp, tgt], [gf],
                                              [(d_model, F32), (d_model, BF16)],
                                              [((1, LANES), F32), ((1, d_model), F32)], 256)

    du = matmul("d_act", dx3_b, w_down_f, "nt", BF16, 512, d_ff // 4, d_model, extras=[u],
                epilogue=lambda acc, uv: (acc * (2.0 * jnp.maximum(uv, 0.0)),))
    dw_down = matmul("dw_down", act, dx3_b, "tn", BF16, 512, d_model, 4096)
    dh2 = matmul("d_h2", du, w_up_f, "nt", F32, 512, d_model, d_ff)
    dw_up = matmul("dw_up", h2, du, "tn", BF16, d_model, 512, 4096)

    def norm_bwd(r, q):
        xv, dh, dres = r
        _, vjp = jax.vjp(f_norm, xv, q[0])
        dx, dg = vjp(dh)
        return [dx + dres, dx + dres], [dg]

    (dx2, dx2_b), (d_ln2,) = rw_call("d_norm2", norm_bwd, [x2, dh2, dx3], [g2], [(d_model, F32), (d_model, BF16)],
                                     [((1, d_model), F32)], 256)

    dycat = matmul("d_ycat", dx2_b, w_out_f, "nt", F32, 512, d_model, d_model)
    dw_out = matmul("dw_out", ycat, dx2_b, "tn", BF16, d_model, d_model, 2048)
    (dz, dgate), _ = rw_call("d_mix", lambda r, q: ([r[0] * r[2], r[0] * r[1]], []),
                             [(dycat, RWKV_W, 0), z, gate], [], [(RWKV_W, F32)] * 2, [], 512)
    dq, dk_sb, dv_sb, d_sg = sb_bwd(by_seq(p), sgr, o_raw, tot, by_seq(dycat[:, RWKV_W:]), sb_first, n_seq, seq_len)
    d_sb = [flat(dq), flat(dk_sb), flat(dv_sb)]
    late_grads = {"w_out": dw_out.reshape(N_DEV, -1, d_model), "w_up": _col_parts(dw_up),
                  "w_down": dw_down.reshape(N_DEV, -1, d_model)}
    (dr, dkraw, dv, dlw, dasig, d_kk, d_ka, d_rk, d_gw, d_gb), late_parts = rwkv_bwd(
        *rwkv_in, kkr, kar, rkr, gwr, gbr, s0_all, by_seq(dz), n_seq, seq_len,
        Hosted([late_grads[n] for n in LATE], False))
    prep_cots = [flat(t) for t in (dr, dkraw, dv, dlw, dasig)] + [dgate]
    dp, (d_mu, d_w0, d_wd, d_a0, d_wa, d_wg) = rwkv_prep_bwd(p, prep_edges, prep_pars, prep_cots, d_sb, seq_len)

    dh1 = matmul("d_h1", dp, w_in_f, "nt", F32, 512, d_model, in_cols)
    dw_in = matmul("dw_in", h1, dp, "tn", BF16, d_model, in_cols // 2, 2048)
    (dx,), (d_ln1,) = rw_call("d_norm1", norm_bwd, [x2d, dh1, dx2], [g1], [(d_model, F32)],
                              [((1, d_model), F32)], 256)

    lora_parts = lambda t: _col_parts(t).astype(BF16)
    early_grads = {"w_in": _col_parts(dw_in), "w_decay_up": lora_parts(d_wd[:HEAD_DIM]),
                   "w_aaa_up": lora_parts(d_wa[HEAD_DIM:]), "w_gate_up": lora_parts(d_wg)}
    parts = dict(zip(EARLY, exchange("scatter_early", [early_grads[n] for n in EARLY], False)))
    parts.update(zip(LATE, late_parts))

    small_grads = {"ln1_g": d_ln1, "tok_mu": d_mu, "w0": d_w0, "a0": d_a0, "k_k": d_kk, "k_a": d_ka, "r_k": d_rk,
                   "gn_w": d_gw, "gn_b": d_gb, "sb_gain": d_sg, "ln2_g": d_ln2, "lnf_g": d_lnf}
    n_small = sum(int(weights[n].size) for n in SMALL)
    pack_rows = -(-(n_small + 1) // (8 * LANES)) * 8
    packed = _pack([small_grads[n] for n in SMALL] + [loss_acc[0, :1]], pack_rows)
    (small_parts,) = exchange("gather_small", [packed], True)

    results = {}
    for n in BIG:
        w2d = weights[n][0]
        tile = w2d.shape[0] if w2d.shape[0] <= 256 else 256
        results[n] = adamw("adamw_" + n, w2d, parts[n], mom_m[n][0], mom_v[n][0], tile)
    pk = lambda d: _pack([d[n] for n in SMALL] + [jnp.zeros((1,), F32)], pack_rows)
    sg, sd, sm, sv = adamw("adamw_small", pk(weights), small_parts, pk(mom_m), pk(mom_v), pack_rows)
    off = 0
    for n in SMALL:
        size = int(weights[n].size)
        results[n] = tuple(t.reshape(-1)[off:off + size] for t in (sg, sd, sm, sv))
        off += size
    loss = sg.reshape(-1)[off]

    out = [loss, dx.reshape(x.shape)]
    for kind in range(4):
        out += [results[n][kind].reshape(weights[n].shape) for n in ORDER]
    return tuple(out)
```

```python
import functools
import math

import jax
import jax.numpy as jnp
from jax import lax
from jax.experimental import pallas as pl
from jax.experimental.pallas import tpu as pltpu

F32 = jnp.float32
BF16 = jnp.bfloat16

N_DEV = 8
HEAD_DIM = 64
LANES = 128
RWKV_W = 512
SB_W = 512
LORA_WA = 128
GATE_LORA = 128
RWKV_COLS = 3 * RWKV_W + LORA_WA + GATE_LORA
RMS_EPS = 1e-5
GN_EPS = 64e-5
CHUNK = 64
QB = 128
SB_SCALE = HEAD_DIM ** -0.5
ADAM_LR, ADAM_B1, ADAM_B2, ADAM_EPS, ADAM_WD, ADAM_STEP = 0.001, 0.9, 0.999, 1e-08, 0.01, 10
VMEM_LIMIT = 56 * 1024 * 1024


_DIMS = {
    "nn": (((1,), (0,)), ((), ())),
    "nt": (((1,), (1,)), ((), ())),
    "tn": (((0,), (0,)), ((), ())),
}


def _pieces(x, n):
    if n == 1:
        return [x.astype(BF16)]
    out, rem = [], x.astype(F32)
    for i in range(n):
        p = rem.astype(BF16)
        out.append(p)
        if i + 1 < n:
            rem = rem - p.astype(F32)
    return out


def _dot(a, b, form, pa, pb):
    pieces_a, pieces_b = _pieces(a, pa), _pieces(b, pb)
    keep = max(pa, pb)
    acc = None
    for i, ai in enumerate(pieces_a):
        for j, bj in enumerate(pieces_b):
            if i + j >= keep:
                continue
            t = lax.dot_general(ai, bj, _DIMS[form], preferred_element_type=F32)
            acc = t if acc is None else acc + t
    return acc


BOTH = (True, True)


@functools.partial(jax.custom_vjp, nondiff_argnums=(2, 3, 4, 5))
def mm(a, b, form, pa, pb, diff=BOTH):
    return _dot(a, b, form, pa, pb)


def _mm_fwd(a, b, form, pa, pb, diff):
    return _dot(a, b, form, pa, pb), (a, b)


def _mm_bwd(form, pa, pb, diff, res, g):
    a, b = res
    pg = max(pa, pb)
    da, db = jnp.zeros_like(a), jnp.zeros_like(b)
    if form == "nn":
        if diff[0]:
            da = mm(g, b, "nt", pg, pb)
        if diff[1]:
            db = mm(a, g, "tn", pa, pg)
    elif form == "nt":
        if diff[0]:
            da = mm(g, b, "nn", pg, pb)
        if diff[1]:
            db = mm(g, a, "tn", pg, pa)
    else:
        if diff[0]:
            da = mm(b, g, "nt", pb, pg)
        if diff[1]:
            db = mm(a, g, "nn", pa, pg)
    return da, db


mm.defvjp(_mm_fwd, _mm_bwd)


def _lane_lo(shape):
    return lax.broadcasted_iota(jnp.int32, shape, len(shape) - 1) < HEAD_DIM


def _segsum(x):
    lo = _lane_lo(x.shape)
    s_lo = jnp.sum(jnp.where(lo, x, 0.0), axis=-1, keepdims=True)
    s_hi = jnp.sum(jnp.where(lo, 0.0, x), axis=-1, keepdims=True)
    return jnp.where(lo, s_lo, s_hi)


def _sigmoid(x):
    return 0.5 * (jnp.tanh(0.5 * x) + 1.0)


def _softplus(x):
    return jnp.maximum(x, 0.0) + jnp.log(1.0 + jnp.exp(-jnp.abs(x)))


def f_norm(x, g):
    return x * lax.rsqrt(jnp.mean(x * x, axis=-1, keepdims=True) + RMS_EPS) * g


def f_prep(p, pprev, mu, w0, wd_pad, a0, wa_pad, wg):
    pm = p + mu * (pprev - p)
    r = pm[:, 0:RWKV_W]
    k = pm[:, RWKV_W:2 * RWKV_W]
    v = pm[:, 2 * RWKV_W:3 * RWKV_W]
    xwa = pm[:, 3 * RWKV_W:3 * RWKV_W + LORA_WA]
    xg = pm[:, 3 * RWKV_W + LORA_WA:RWKV_COLS]
    w = -_softplus(-(w0 + mm(jnp.tanh(xwa), wd_pad, "nn", 1, 1))) - 0.5
    lw = -jnp.exp(w)
    asig = _sigmoid(a0 + mm(xwa, wa_pad, "nn", 1, 1))
    gate = mm(_sigmoid(xg), wg, "nn", 1, 1)
    return r, k, v, lw, asig, gate


def _tri(n, kind):
    row = lax.broadcasted_iota(jnp.int32, (n, n), 0)
    col = lax.broadcasted_iota(jnp.int32, (n, n), 1)
    if kind == "lower_incl":
        return row >= col
    return row > col


def rwkv_chunk(state, r, kraw, v, lw, asig, k_k, k_a, r_k, gn_w, gn_b, hp):
    n = len(r)
    L = r[0].shape[0]
    lo = _lane_lo((1, LANES))
    masks = (lo, jnp.logical_not(lo))
    incl = _tri(L, "lower_incl")
    strict = _tri(L, "strict")
    tri = incl.astype(F32)
    eye = (lax.broadcasted_iota(jnp.int32, (L, L), 0) == lax.broadcasted_iota(jnp.int32, (L, L), 1)).astype(F32)
    kk = [x * w for x, w in zip(kraw, k_k)]
    kk = [x / jnp.maximum(jnp.sqrt(_segsum(x * x)), 1e-12) for x in kk]
    k = [x * (1.0 + (s - 1.0) * w) for x, s, w in zip(kraw, asig, k_a)]
    b = [x * s for x, s in zip(kk, asig)]
    c = [mm(tri, x, "nn", 1, 3, (False, True)) for x in lw]
    at = [-x * jnp.exp(ci - li) for x, ci, li in zip(kk, c, lw)]
    rt = [x * jnp.exp(ci) for x, ci in zip(r, c)]
    einv = [jnp.exp(-ci) for ci in c]
    bt = [x * e for x, e in zip(b, einv)]
    kt = [x * e for x, e in zip(k, einv)]
    inst = [(s, m) for s in range(n) for m in masks]
    at_h = [jnp.where(m, at[s], 0.0) for s, m in inst]
    rt_h = [jnp.where(m, rt[s], 0.0) for s, m in inst]
    a_ab = [jnp.where(strict, mm(x, bt[s], "nt", hp, hp), 0.0) for x, (s, _) in zip(at_h, inst)]
    a_ak = [jnp.where(strict, mm(x, kt[s], "nt", hp, hp), 0.0) for x, (s, _) in zip(at_h, inst)]
    b_rb = [jnp.where(incl, mm(x, bt[s], "nt", hp, hp), 0.0) for x, (s, _) in zip(rt_h, inst)]
    b_rk = [jnp.where(incl, mm(x, kt[s], "nt", hp, hp), 0.0) for x, (s, _) in zip(rt_h, inst)]
    tinv = [eye + x for x in a_ab]
    pw = a_ab
    for _ in range(int(math.log2(L)) - 1):
        pw = [mm(x, x, "nn", hp, hp) for x in pw]
        tinv = [t + mm(t, x, "nn", hp, hp) for t, x in zip(tinv, pw)]
    rhs = [mm(x, state[s], "nt", hp, hp) + mm(y, v[s], "nn", hp, hp) for x, y, (s, _) in zip(at_h, a_ak, inst)]
    u_h = [mm(t, x, "nn", hp, hp) for t, x in zip(tinv, rhs)]
    y_h = [mm(x, state[s], "nt", hp, hp) + mm(m1, u, "nn", hp, hp) + mm(m2, v[s], "nn", hp, hp)
           for x, m1, m2, u, (s, _) in zip(rt_h, b_rb, b_rk, u_h, inst)]
    u_all = [jnp.where(lo, u_h[2 * s], u_h[2 * s + 1]) for s in range(n)]
    y_all = [jnp.where(lo, y_h[2 * s], y_h[2 * s + 1]) for s in range(n)]
    c_last = [jnp.sum(x, axis=0, keepdims=True) for x in lw]
    efwd = [jnp.exp(cl - ci) for cl, ci in zip(c_last, c)]
    new_state = [st * jnp.exp(cl) + mm(u, bi * e, "tn", hp, hp) + mm(vi, ki * e, "tn", hp, hp)
                 for st, cl, u, bi, e, vi, ki in zip(state, c_last, u_all, b, efwd, v, k)]
    row_head = lax.broadcasted_iota(jnp.int32, (LANES, LANES), 0) // HEAD_DIM
    col_head = lax.broadcasted_iota(jnp.int32, (LANES, LANES), 1) // HEAD_DIM
    new_state = [jnp.where(row_head == col_head, x, 0.0) for x in new_state]
    outs = []
    for y, ri, ki, vi, w_rk, w_gw, w_gb in zip(y_all, r, k, v, r_k, gn_w, gn_b):
        mean = _segsum(y) * (1.0 / HEAD_DIM)
        d = y - mean
        var = _segsum(d * d) * (1.0 / HEAD_DIM)
        yn = d * lax.rsqrt(var + GN_EPS) * w_gw + w_gb
        outs.append(yn + _segsum(ri * ki * w_rk) * vi)
    return outs, new_state


def sb_tile(q, k, v, c_lo, c_hi, diag, from_here=None):
    n = len(q)
    lo = _lane_lo((1, LANES))
    below = _tri(QB, "strict")
    from_s = _tri(QB, "lower_incl").astype(F32)
    inst = [(s, h) for s in range(n) for h in (0, 1)]
    carry = [(c_lo[s], c_hi[s])[h] for s, h in inst]
    z = [mm(q[s][h], k[s], "nt", 1, 1) for s, h in inst]
    soft = [jnp.log(1.0 + jnp.exp(-jnp.abs(x))) for x in z]
    log_keep = [-jnp.maximum(x, 0.0) - sp for x, sp in zip(z, soft)]
    if diag:
        log_keep = [jnp.where(below, x, 0.0) for x in log_keep]
    own = [jnp.sum(x, axis=1, keepdims=True) for x in log_keep]
    if from_here is not None:
        carry = [lax.stop_gradient(from_here[s][h] - o) + cr for (s, h), o, cr in zip(inst, own, carry)]
    tail = [mm(x, from_s, "nn", SB_SUM_PIECES, 1, (True, False)) for x in log_keep]
    log_a = [x + tl + cr for x, tl, cr in zip(z, tail, carry)]
    if diag:
        log_a = [jnp.where(below, x, -1e30) for x in log_a]
    att = [jnp.exp(x) for x in log_a]
    out_h = [mm(x, v[s], "nn", 1, 1) for x, (s, _) in zip(att, inst)]
    out = [jnp.where(lo, out_h[2 * s], out_h[2 * s + 1]) for s in range(n)]
    return out, [own[2 * s] for s in range(n)], [own[2 * s + 1] for s in range(n)]


def sb_split_q(q):
    lo = _lane_lo((1, LANES))
    qs = q * SB_SCALE
    return jnp.where(lo, qs, 0.0), jnp.where(lo, 0.0, qs)


def sb_post(o, gain):
    return o * lax.rsqrt(_segsum(o * o) * (1.0 / HEAD_DIM) + RMS_EPS) * gain


def f_final(x3, g, target):
    y = f_norm(x3, g)
    err = y - target
    return 0.5 * jnp.mean(err * err, axis=-1, keepdims=True)


def _params(sem):
    return pltpu.CompilerParams(dimension_semantics=sem, vmem_limit_bytes=VMEM_LIMIT)


def rw_call(name, body_fn, rows, pars, out_rows, out_accs, tile):
    rows = [item if isinstance(item, tuple) else (item, item.shape[1], 0) for item in rows]
    row_arrays = [arr for arr, _, _ in rows]
    n_rows = row_arrays[0].shape[0]
    tile = min(tile, n_rows)
    steps = n_rows // tile
    row_specs = [pl.BlockSpec((tile, cols), functools.partial(lambda i, c: (i, c), c=cblk)) for _, cols, cblk in rows]
    par_specs = [pl.BlockSpec(p.shape, lambda i: (0, 0)) for p in pars]
    nr, npar, nor, noa = len(row_arrays), len(pars), len(out_rows), len(out_accs)

    def body(*refs):
        row_vals = [refs[i][...] for i in range(nr)]
        par_vals = [refs[nr + i][...] for i in range(npar)]
        o_refs = refs[nr + npar:nr + npar + nor]
        a_refs = refs[nr + npar + nor:]
        row_outs, acc_outs = body_fn(row_vals, par_vals)
        for ref, val in zip(o_refs, row_outs):
            ref[...] = val.astype(ref.dtype)
        if noa:
            first = pl.program_id(0) == 0

            @pl.when(first)
            def _():
                for ref, val in zip(a_refs, acc_outs):
                    ref[...] = val.astype(ref.dtype)

            @pl.when(jnp.logical_not(first))
            def _():
                for ref, val in zip(a_refs, acc_outs):
                    ref[...] = ref[...] + val.astype(ref.dtype)

    out_shape = [jax.ShapeDtypeStruct((n_rows, c), dt) for c, dt in out_rows]
    out_shape += [jax.ShapeDtypeStruct(s, dt) for s, dt in out_accs]
    out_specs = [pl.BlockSpec((tile, c), lambda i: (i, 0)) for c, _ in out_rows]
    out_specs += [pl.BlockSpec(s, lambda i: (0, 0)) for s, _ in out_accs]
    outs = pl.pallas_call(
        body, name=name, grid=(steps,), in_specs=row_specs + par_specs, out_specs=out_specs,
        out_shape=out_shape, compiler_params=_params(("arbitrary",)),
    )(*row_arrays, *pars)
    return outs[:nor], outs[nor:]


def matmul(name, a, b, form, out_dtype, tm, tn, tk, extras=(), pars=(), epilogue=None, sums=(), hosted=None):
    out_dtypes = out_dtype if isinstance(out_dtype, tuple) else (out_dtype,)
    tm, tn, tk = min(tm, a.shape[1 if form == "tn" else 0]), min(tn, b.shape[0 if form == "nt" else 1]), min(tk, a.shape[0 if form == "tn" else 1])
    if form == "nn":
        (m, kd), n = a.shape, b.shape[1]
        a_spec = pl.BlockSpec((tm, tk), lambda i, j, k: (i, k))
        b_spec = pl.BlockSpec((tk, tn), lambda i, j, k: (k, j))
    elif form == "nt":
        (m, kd), n = a.shape, b.shape[0]
        a_spec = pl.BlockSpec((tm, tk), lambda i, j, k: (i, k))
        b_spec = pl.BlockSpec((tn, tk), lambda i, j, k: (j, k))
    else:
        (kd, m), n = a.shape, b.shape[1]
        a_spec = pl.BlockSpec((tk, tm), lambda i, j, k: (k, i))
        b_spec = pl.BlockSpec((tk, tn), lambda i, j, k: (k, j))
    ksteps = kd // tk

    n_extra, n_par, n_out, n_sum = len(extras), len(pars), len(out_dtypes), len(sums)
    n_in = 2 + n_extra + n_par
    grid = (m // tm, n // tn, ksteps)

    def body(*refs):
        if hosted is not None:
            refs, copies = hosted.split(refs, n_in, n_out + n_sum)
            here = [pl.program_id(d) for d in range(3)]
            hosted.run(copies, functools.reduce(jnp.logical_and, [h == 0 for h in here]),
                       functools.reduce(jnp.logical_and, [h == g - 1 for h, g in zip(here, grid)]))
        a_ref, b_ref, rest = refs[0], refs[1], refs[2:]
        e_refs, o_refs = rest[:n_extra + n_par], rest[n_extra + n_par:n_extra + n_par + n_out]
        s_refs = rest[n_extra + n_par + n_out:n_extra + n_par + n_out + n_sum]
        kstep = pl.program_id(2)
        part = lax.dot_general(a_ref[...].astype(BF16), b_ref[...].astype(BF16), _DIMS[form],
                               preferred_element_type=F32)

        def finish(acc):
            outs = epilogue(acc, *[r[...] for r in e_refs]) if epilogue else (acc,)
            for ref, val in zip(o_refs, outs[:n_out]):
                ref[...] = val.astype(ref.dtype)
            if n_sum:
                first_tile = jnp.logical_and(pl.program_id(0) == 0, pl.program_id(1) == 0)

                @pl.when(first_tile)
                def _():
                    for ref, val in zip(s_refs, outs[n_out:]):
                        ref[...] = val.astype(ref.dtype)

                @pl.when(jnp.logical_not(first_tile))
                def _():
                    for ref, val in zip(s_refs, outs[n_out:]):
                        ref[...] = ref[...] + val.astype(ref.dtype)

        if ksteps == 1:
            finish(part)
            return
        acc_ref = rest[n_extra + n_par + n_out + n_sum]

        @pl.when(kstep == 0)
        def _():
            acc_ref[...] = part

        @pl.when(jnp.logical_and(kstep > 0, kstep < ksteps - 1))
        def _():
            acc_ref[...] = acc_ref[...] + part

        @pl.when(kstep == ksteps - 1)
        def _():
            finish(acc_ref[...] + part)

    out_spec = pl.BlockSpec((tm, tn), lambda i, j, k: (i, j))
    whole = lambda shape: pl.BlockSpec(shape, lambda i, j, k: (0,) * len(shape))
    n_host = hosted.n if hosted is not None else 0
    host_in = list(hosted.arrays) if hosted is not None else []
    host_out = exchange_shapes(hosted.arrays, hosted.gather) if hosted is not None else []
    outs = pl.pallas_call(
        body, name=name, grid=grid,
        in_specs=[a_spec, b_spec] + [out_spec] * n_extra + [whole(t.shape) for t in pars] + [ANY_SPEC] * n_host,
        out_specs=[out_spec] * n_out + [whole(s) for s, _ in sums] + [ANY_SPEC] * n_host,
        out_shape=[jax.ShapeDtypeStruct((m, n), dt) for dt in out_dtypes]
        + [jax.ShapeDtypeStruct(s, dt) for s, dt in sums] + host_out,
        scratch_shapes=([pltpu.VMEM((tm, tn), F32)] if ksteps > 1 else []) + (exchange_sems(n_host) if n_host else []),
        compiler_params=_params(("arbitrary",) * 3 if (sums or n_host) else ("parallel", "parallel", "arbitrary")),
    )(a, b, *extras, *pars, *host_in)
    return outs if (isinstance(out_dtype, tuple) or sums or n_host) else outs[0]


PREP_TILE = 256
PREP_TILE_BWD = 128
SUBLANES = 8


def _shift_in(rows, first):
    rolled = pltpu.roll(rows, 1, 0)
    row = lax.broadcasted_iota(jnp.int32, (SUBLANES, rows.shape[1]), 0)
    head = jnp.where(row == 0, first, rolled[0:SUBLANES])
    return jnp.concatenate([head, rolled[SUBLANES:]], axis=0), rolled


def rwkv_prep_fwd(p, pars, seq_len):
    n_tok = p.shape[0]
    tile = min(PREP_TILE, seq_len)
    tile_b = min(PREP_TILE_BWD, tile)
    steps, per_seq, sub = n_tok // tile, seq_len // tile, tile // tile_b
    n_par = len(pars)

    def body(p_ref, *rest):
        par_refs, out_refs, edge_ref, last8 = rest[:n_par], rest[n_par:n_par + 6], rest[n_par + 6], rest[n_par + 7]
        step = pl.program_id(0)

        @pl.when(step == 0)
        def _():
            last8[...] = jnp.zeros_like(last8)

        rows = p_ref[...]
        before = jnp.where(step % per_seq == 0, 0.0, pltpu.roll(last8[...], 1, 0))
        prev, rolled = _shift_in(rows, before)
        edge_ref[0] = prev[0:SUBLANES]
        for m in range(1, sub):
            edge_ref[m] = rolled[m * tile_b:m * tile_b + SUBLANES]
        last8[...] = rows[tile - SUBLANES:tile]
        for ref, val in zip(out_refs, f_prep(rows, prev, *[r[...] for r in par_refs])):
            ref[...] = val

    row_out = pl.BlockSpec((tile, RWKV_W), lambda i: (i, 0))
    outs = pl.pallas_call(
        body, name="rwkv_prep", grid=(steps,),
        in_specs=[pl.BlockSpec((tile, RWKV_COLS), lambda i: (i, 0))] + [pl.BlockSpec(t.shape, lambda i: (0, 0)) for t in pars],
        out_specs=[row_out] * 6 + [pl.BlockSpec((sub, SUBLANES, RWKV_COLS), lambda i: (i, 0, 0))],
        out_shape=[jax.ShapeDtypeStruct((n_tok, RWKV_W), F32)] * 6
        + [jax.ShapeDtypeStruct((steps * sub, SUBLANES, RWKV_COLS), F32)],
        scratch_shapes=[pltpu.VMEM((SUBLANES, RWKV_COLS), F32)],
        compiler_params=_params(("arbitrary",)),
    )(p, *pars)
    return outs[:6], outs[6]


def rwkv_prep_bwd(p, edges, pars, cots, d_sb, seq_len):
    n_tok = p.shape[0]
    tile = min(PREP_TILE_BWD, seq_len)
    steps, per_seq = n_tok // tile, seq_len // tile
    n_par = len(pars)
    back = lambda i: steps - 1 - i

    def body(p_ref, edge_ref, *rest):
        par_refs, rest = rest[:n_par], rest[n_par:]
        cot_refs, sb_refs, dp_ref, acc_refs, next8 = rest[:6], rest[6:9], rest[9], rest[10:10 + n_par], rest[10 + n_par]
        step = pl.program_id(0)
        first = step == 0

        @pl.when(first)
        def _():
            next8[...] = jnp.zeros_like(next8)

        rows = p_ref[...]
        prev, _ = _shift_in(rows, edge_ref[0])
        _, vjp = jax.vjp(f_prep, rows, prev, *[r[...].astype(F32) for r in par_refs])
        grads = vjp(tuple(r[...] for r in cot_refs))
        d_rows, d_prev = grads[0], grads[1]
        up = pltpu.roll(d_prev, tile - 1, 0)
        ends_seq = back(step) % per_seq == per_seq - 1
        after = jnp.where(ends_seq, 0.0, pltpu.roll(next8[...], SUBLANES - 1, 0))
        row = lax.broadcasted_iota(jnp.int32, (SUBLANES, RWKV_COLS), 0)
        tail = jnp.where(row == SUBLANES - 1, after, up[tile - SUBLANES:tile])
        d_rows = d_rows + jnp.concatenate([up[:tile - SUBLANES], tail], axis=0)
        next8[...] = d_prev[0:SUBLANES]
        dp_ref[...] = jnp.concatenate([d_rows] + [r[...] for r in sb_refs], axis=1).astype(dp_ref.dtype)

        @pl.when(first)
        def _():
            for ref, val in zip(acc_refs, grads[2:]):
                ref[...] = val

        @pl.when(jnp.logical_not(first))
        def _():
            for ref, val in zip(acc_refs, grads[2:]):
                ref[...] = ref[...] + val

    cols = RWKV_COLS + sum(t.shape[1] for t in d_sb)
    half = pl.BlockSpec((tile, RWKV_W), lambda i: (back(i), 0))
    par_specs = [pl.BlockSpec(t.shape, lambda i: (0, 0)) for t in pars]
    outs = pl.pallas_call(
        body, name="d_rwkv_prep", grid=(steps,),
        in_specs=[pl.BlockSpec((tile, RWKV_COLS), lambda i: (back(i), 0)),
                  pl.BlockSpec((1, SUBLANES, RWKV_COLS), lambda i: (back(i), 0, 0))] + par_specs + [half] * 9,
        out_specs=[pl.BlockSpec((tile, cols), lambda i: (back(i), 0))] + par_specs,
        out_shape=[jax.ShapeDtypeStruct((n_tok, cols), BF16)] + [jax.ShapeDtypeStruct(t.shape, F32) for t in pars],
        scratch_shapes=[pltpu.VMEM((SUBLANES, RWKV_COLS), F32)],
        compiler_params=_params(("arbitrary",)),
    )(p, edges, *pars, *cots, *d_sb)
    return outs[0], outs[1:]


RWKV_HP = 1


RWKV_PAIRS = 4


def _rwkv_specs(n_seq, chunk_of):
    width = RWKV_PAIRS * LANES
    row = pl.BlockSpec((n_seq, CHUNK, width), lambda g, c: (0, chunk_of(c), g))
    par = pl.BlockSpec((1, width), lambda g, c: (0, g))
    s0 = pl.BlockSpec((1, 1, RWKV_PAIRS * n_seq, LANES, LANES), lambda g, c: (g, chunk_of(c), 0, 0, 0))
    return row, par, s0


class Hosted:
    def __init__(self, arrays, gather):
        self.arrays, self.gather, self.n = list(arrays), gather, len(arrays)

    def split(self, refs, n_in, n_out):
        n = self.n
        ins, outs, sems = refs[n_in:n_in + n], refs[n_in + n + n_out:n_in + 2 * n + n_out], refs[-3:]
        own = refs[:n_in] + refs[n_in + n:n_in + n + n_out] + refs[n_in + 2 * n + n_out:-3]
        return own, exchange_copies(ins, outs, *sems, self.gather)

    def run(self, copies, first, last):
        @pl.when(first)
        def _():
            for cp in copies:
                cp.start()

        @pl.when(last)
        def _():
            for cp in copies:
                cp.wait()


def rwkv_fwd(r, kraw, v, lw, asig, k_k, k_a, r_k, gn_w, gn_b, n_seq, seq_len, hosted):
    n_chunks = seq_len // CHUNK
    n_groups = RWKV_W // (RWKV_PAIRS * LANES)
    n_inst = RWKV_PAIRS * n_seq
    row, par, s0_spec = _rwkv_specs(n_seq, lambda c: c)
    inst = [(s, pl.ds(pp * LANES, LANES)) for pp in range(RWKV_PAIRS) for s in range(n_seq)]

    def body(*refs):
        own, copies = hosted.split(refs, 10, 2)
        row_refs, par_refs, (z_ref, s0_ref, state) = own[:5], own[5:10], own[10:]
        step = pl.program_id(0) * n_chunks + pl.program_id(1)
        hosted.run(copies, step == 0, step == n_groups * n_chunks - 1)

        @pl.when(pl.program_id(1) == 0)
        def _():
            state[...] = jnp.zeros_like(state)

        s0 = [state[i] for i in range(n_inst)]
        rows = [[ref[s, :, lanes] for s, lanes in inst] for ref in row_refs]
        pars = [[ref[:, lanes] for _, lanes in inst] for ref in par_refs]
        z, s1 = rwkv_chunk(s0, *rows, *pars, RWKV_HP)
        for i, (s, lanes) in enumerate(inst):
            s0_ref[0, 0, i] = s0[i]
            z_ref[s, :, lanes] = z[i]
            state[i] = s1[i]

    outs = pl.pallas_call(
        body, name="rwkv_fwd", grid=(n_groups, n_chunks),
        in_specs=[row] * 5 + [par] * 5 + [ANY_SPEC] * hosted.n, out_specs=[row, s0_spec] + [ANY_SPEC] * hosted.n,
        out_shape=[jax.ShapeDtypeStruct(r.shape, F32),
                   jax.ShapeDtypeStruct((n_groups, n_chunks, n_inst, LANES, LANES), F32)]
        + exchange_shapes(hosted.arrays, hosted.gather),
        scratch_shapes=[pltpu.VMEM((n_inst, LANES, LANES), F32)] + exchange_sems(hosted.n),
        compiler_params=_params(("arbitrary", "arbitrary")),
    )(r, kraw, v, lw, asig, k_k, k_a, r_k, gn_w, gn_b, *hosted.arrays)
    return outs[0], outs[1], outs[2:]


def rwkv_bwd(r, kraw, v, lw, asig, k_k, k_a, r_k, gn_w, gn_b, s0_all, dz, n_seq, seq_len, hosted):
    n_chunks = seq_len // CHUNK
    n_groups = RWKV_W // (RWKV_PAIRS * LANES)
    n_inst = RWKV_PAIRS * n_seq
    row, par, s0_spec = _rwkv_specs(n_seq, lambda c: n_chunks - 1 - c)
    inst = [(s, pl.ds(pp * LANES, LANES)) for pp in range(RWKV_PAIRS) for s in range(n_seq)]

    def body(*refs):
        own, copies = hosted.split(refs, 12, 10)
        row_refs, par_refs, s0_ref, dz_ref = own[:5], own[5:10], own[10], own[11]
        drow_refs, dpar_refs, dstate = own[12:17], own[17:22], own[22]
        step = pl.program_id(0) * n_chunks + pl.program_id(1)
        hosted.run(copies, step == 0, step == n_groups * n_chunks - 1)
        first = pl.program_id(1) == 0

        @pl.when(first)
        def _():
            dstate[...] = jnp.zeros_like(dstate)

        fn = functools.partial(rwkv_chunk, hp=RWKV_HP)
        rows = [[ref[s, :, lanes] for s, lanes in inst] for ref in row_refs]
        pars = [[ref[:, lanes] for _, lanes in inst] for ref in par_refs]
        _, vjp = jax.vjp(fn, [s0_ref[0, 0, i] for i in range(n_inst)], *rows, *pars)
        grads = vjp(([dz_ref[s, :, lanes] for s, lanes in inst], [dstate[i] for i in range(n_inst)]))
        for i, (s, lanes) in enumerate(inst):
            dstate[i] = grads[0][i]
            for ref, val in zip(drow_refs, grads[1:6]):
                ref[s, :, lanes] = val[i]

        def accumulate(start):
            for ref, val in zip(dpar_refs, grads[6:]):
                for pp in range(RWKV_PAIRS):
                    lanes = pl.ds(pp * LANES, LANES)
                    total = functools.reduce(jnp.add, val[pp * n_seq:(pp + 1) * n_seq])
                    ref[:, lanes] = total if start else ref[:, lanes] + total

        @pl.when(first)
        def _():
            accumulate(True)

        @pl.when(jnp.logical_not(first))
        def _():
            accumulate(False)

    rows_shape = jax.ShapeDtypeStruct(r.shape, F32)
    par_shape = jax.ShapeDtypeStruct((1, RWKV_W), F32)
    outs = pl.pallas_call(
        body, name="rwkv_bwd", grid=(n_groups, n_chunks),
        in_specs=[row] * 5 + [par] * 5 + [s0_spec, row] + [ANY_SPEC] * hosted.n,
        out_specs=[row] * 5 + [par] * 5 + [ANY_SPEC] * hosted.n,
        out_shape=[rows_shape] * 5 + [par_shape] * 5 + exchange_shapes(hosted.arrays, hosted.gather),
        scratch_shapes=[pltpu.VMEM((n_inst, LANES, LANES), F32)] + exchange_sems(hosted.n),
        compiler_params=_params(("arbitrary", "arbitrary")),
    )(r, kraw, v, lw, asig, k_k, k_a, r_k, gn_w, gn_b, s0_all, dz, *hosted.arrays)
    return outs[:10], outs[10:]


SB_Q0 = RWKV_COLS // LANES
SB_K0 = SB_Q0 + SB_W // LANES
SB_V0 = SB_K0 + SB_W // LANES
SB_SEQS = 4
SB_BUFFERS = pl.Buffered(1)
SB_SUM_PIECES = 2
SB_DEAD = -110.0


def _col_of(c_lo, c_hi):
    return jnp.where(_lane_lo((1, LANES)), c_lo, c_hi)


def sb_fwd(p, gain, n_seq, seq_len):
    n_pairs = SB_W // LANES
    n_q = seq_len // QB
    nb = min(SB_SEQS, n_seq)

    def seq_spec(c0):
        return pl.BlockSpec((nb, seq_len, LANES), functools.partial(lambda b, h, c0: (b, 0, c0 + h), c0=c0),
                            pipeline_mode=SB_BUFFERS)

    out_spec = pl.BlockSpec((nb, seq_len, LANES), lambda b, h: (b, 0, h), pipeline_mode=SB_BUFFERS)

    def body(q_ref, k_ref, v_ref, g_ref, y_ref, o_ref, tot_ref, first_ref):
        gain = g_ref[...]

        def q_block(i, _):
            qs = pl.multiple_of(i * QB, QB)
            seqs = range(nb)
            zeros = [jnp.zeros((QB, 1), F32)] * nb
            qv = [sb_split_q(q_ref[s, pl.ds(qs, QB), :]) for s in seqs]
            add = lambda xs, ys: [x + y for x, y in zip(xs, ys)]

            def tiles(ks, c_lo, c_hi, diag):
                return sb_tile(qv, [k_ref[s, pl.ds(ks, QB), :] for s in seqs],
                               [v_ref[s, pl.ds(ks, QB), :] for s in seqs], c_lo, c_hi, diag)

            def alive(c_lo, c_hi):
                top = functools.reduce(jnp.maximum, list(c_lo) + list(c_hi))
                return jnp.max(top) > SB_DEAD

            def k_block(state):
                j, _, (o, c_lo, c_hi) = state
                o2, s_lo, s_hi = tiles(pl.multiple_of(j * QB, QB), c_lo, c_hi, False)
                c_lo, c_hi = add(c_lo, s_lo), add(c_hi, s_hi)
                return j - 1, alive(c_lo, c_hi), (add(o, o2), c_lo, c_hi)

            o, c_lo, c_hi = tiles(qs, zeros, zeros, True)
            j, _, (o, c_lo, c_hi) = lax.while_loop(lambda st: jnp.logical_and(st[0] >= 0, st[1]), k_block,
                                                   (i - 1, alive(c_lo, c_hi), (o, c_lo, c_hi)))
            first_ref[pl.program_id(0), pl.program_id(1), i] = j + 1
            for s in seqs:
                o_ref[s, pl.ds(qs, QB), :] = o[s]
                tot_ref[s, pl.ds(qs, QB), :] = jnp.broadcast_to(_col_of(c_lo[s], c_hi[s]), (QB, LANES))
                y_ref[s, pl.ds(qs, QB), :] = sb_post(o[s], gain)
            return 0

        lax.fori_loop(0, n_q, q_block, 0)

    shape = jax.ShapeDtypeStruct((n_seq, seq_len, SB_W), F32)
    return pl.pallas_call(
        body, name="sb_fwd", grid=(n_seq // nb, n_pairs),
        in_specs=[seq_spec(SB_Q0), seq_spec(SB_K0), seq_spec(SB_V0), pl.BlockSpec((1, LANES), lambda b, h: (0, h))],
        out_specs=[out_spec] * 3 + [pl.BlockSpec(memory_space=pltpu.SMEM)],
        out_shape=[shape] * 3 + [jax.ShapeDtypeStruct((n_seq // nb, n_pairs, n_q), jnp.int32)],
        compiler_params=_params(("arbitrary", "arbitrary")),
    )(p, p, p, gain)


def sb_bwd(p, gain, o_raw, tot, dy, first, n_seq, seq_len):
    n_pairs = SB_W // LANES
    n_q = seq_len // QB
    nb = min(SB_SEQS, n_seq)

    def seq_spec(c0):
        return pl.BlockSpec((nb, seq_len, LANES), functools.partial(lambda h, b, c0: (b, 0, c0 + h), c0=c0),
                            pipeline_mode=SB_BUFFERS)

    own = pl.BlockSpec((nb, seq_len, LANES), lambda h, b: (b, 0, h), pipeline_mode=SB_BUFFERS)
    par = pl.BlockSpec((1, LANES), lambda h, b: (0, h))

    def body(q_ref, k_ref, v_ref, g_ref, o_ref, tot_ref, dy_ref, first_ref, dq_ref, dk_ref, dv_ref, dg_ref):
        gain = g_ref[...]
        lo = _lane_lo((1, LANES))
        dk_ref[...] = jnp.zeros_like(dk_ref)
        dv_ref[...] = jnp.zeros_like(dv_ref)

        def q_block(i, dgain):
            qs = pl.multiple_of(i * QB, QB)
            seqs = range(nb)
            zeros = [jnp.zeros((QB, 1), F32)] * nb
            qv, dov, t_lo, t_hi = [], [], [], []
            for s in seqs:
                qv.append(sb_split_q(q_ref[s, pl.ds(qs, QB), :]))
                _, post_vjp = jax.vjp(sb_post, o_ref[s, pl.ds(qs, QB), :], gain)
                do, dg_s = post_vjp(dy_ref[s, pl.ds(qs, QB), :])
                dov.append(do)
                dgain = dgain + dg_s
                tot_s = tot_ref[s, pl.ds(qs, QB), :]
                t_lo.append(jnp.max(jnp.where(lo, tot_s, -jnp.inf), axis=1, keepdims=True))
                t_hi.append(jnp.max(jnp.where(lo, -jnp.inf, tot_s), axis=1, keepdims=True))
            add = lambda xs, ys: [x + y for x, y in zip(xs, ys)]
            sub = lambda xs, ys: [x - y for x, y in zip(xs, ys)]

            def tile(ks, carry, diag):
                dq, rem_lo, rem_hi, g_lo, g_hi = carry
                kv = [k_ref[s, pl.ds(ks, QB), :] for s in seqs]
                vv = [v_ref[s, pl.ds(ks, QB), :] for s in seqs]
                fn = functools.partial(sb_tile, diag=diag, from_here=list(zip(rem_lo, rem_hi)))
                (_, s_lo, s_hi), vjp = jax.vjp(fn, qv, kv, vv, zeros, zeros)
                dq_t, dk_t, dv_t, dc_lo, dc_hi = vjp((dov, g_lo, g_hi))
                dq_t = [jnp.where(lo, d_lo, d_hi) for d_lo, d_hi in dq_t]
                for s in seqs:
                    dk_ref[s, pl.ds(ks, QB), :] = dk_ref[s, pl.ds(ks, QB), :] + dk_t[s]
                    dv_ref[s, pl.ds(ks, QB), :] = dv_ref[s, pl.ds(ks, QB), :] + dv_t[s]
                return add(dq, dq_t), sub(rem_lo, s_lo), sub(rem_hi, s_hi), add(g_lo, dc_lo), add(g_hi, dc_hi)

            def k_block(j, carry):
                return tile(pl.multiple_of(j * QB, QB), carry, False)

            carry = ([jnp.zeros((QB, LANES), F32)] * nb, t_lo, t_hi, zeros, zeros)
            carry = lax.fori_loop(first_ref[pl.program_id(1), pl.program_id(0), i], i, k_block, carry)
            carry = tile(qs, carry, True)
            for s in seqs:
                dq_ref[s, pl.ds(qs, QB), :] = carry[0][s] * SB_SCALE
            return dgain

        dgain = lax.fori_loop(0, n_q, q_block, jnp.zeros((1, LANES), F32))
        first = pl.program_id(1) == 0

        @pl.when(first)
        def _():
            dg_ref[...] = dgain

        @pl.when(jnp.logical_not(first))
        def _():
            dg_ref[...] = dg_ref[...] + dgain

    shape = jax.ShapeDtypeStruct((n_seq, seq_len, SB_W), F32)
    return pl.pallas_call(
        body, name="sb_bwd", grid=(n_pairs, n_seq // nb),
        in_specs=[seq_spec(SB_Q0), seq_spec(SB_K0), seq_spec(SB_V0), par, own, own, own,
                  pl.BlockSpec(memory_space=pltpu.SMEM)],
        out_specs=[own, own, own, par],
        out_shape=[shape, shape, shape, jax.ShapeDtypeStruct((1, SB_W), F32)],
        compiler_params=_params(("arbitrary", "arbitrary")),
    )(p, p, p, gain, o_raw, tot, dy, first)


def exchange(name, arrays, gather):
    n = len(arrays)

    def body(*refs):
        copies = exchange_copies(refs[:n], refs[n:2 * n], *refs[2 * n:], gather)
        for cp in copies:
            cp.start()
        for cp in copies:
            cp.wait()

    return pl.pallas_call(
        body, name=name, in_specs=[ANY_SPEC] * n, out_specs=[ANY_SPEC] * n, out_shape=exchange_shapes(arrays, gather),
        scratch_shapes=exchange_sems(n),
    )(*arrays)


ANY_SPEC = pl.BlockSpec(memory_space=pl.ANY)


def exchange_shapes(arrays, gather):
    return [jax.ShapeDtypeStruct(((N_DEV,) + a.shape) if gather else a.shape, a.dtype) for a in arrays]


def exchange_sems(n):
    return [pltpu.SemaphoreType.DMA((n, N_DEV - 1)), pltpu.SemaphoreType.DMA((n, N_DEV - 1)),
            pltpu.SemaphoreType.DMA((n,))]


def exchange_copies(ins, outs, send_sems, recv_sems, local_sems, gather):
    x, y, c = lax.axis_index("x"), lax.axis_index("y"), lax.axis_index("c")
    me = 4 * x + 2 * y + c
    copies = []
    for a, (src_all, dst_all) in enumerate(zip(ins, outs)):
        own = src_all if gather else src_all.at[me]
        copies.append(pltpu.make_async_copy(own, dst_all.at[me], local_sems.at[a]))
        for j in range(1, N_DEV):
            px, py, pc = (x + (j >> 2)) % 2, (y + ((j >> 1) & 1)) % 2, (c + (j & 1)) % 2
            src = src_all if gather else src_all.at[4 * px + 2 * py + pc]
            copies.append(pltpu.make_async_remote_copy(
                src_ref=src, dst_ref=dst_all.at[me], send_sem=send_sems.at[a, j - 1],
                recv_sem=recv_sems.at[a, j - 1], device_id=(px, py, pc), device_id_type=pl.DeviceIdType.MESH))
    return copies


def adamw(name, w, parts, m, v, tile):
    rows, cols = w.shape
    spec = pl.BlockSpec((tile, cols), lambda i: (i, 0))
    part_spec = pl.BlockSpec((N_DEV, tile, cols), lambda i: (0, i, 0))

    def body(w_ref, p_ref, m_ref, v_ref, g_ref, d_ref, nm_ref, nv_ref):
        g = p_ref[0].astype(F32)
        for s in range(1, N_DEV):
            g = g + p_ref[s].astype(F32)
        new_m = ADAM_B1 * m_ref[...] + (1.0 - ADAM_B1) * g
        new_v = ADAM_B2 * v_ref[...] + (1.0 - ADAM_B2) * (g * g)
        m_hat = new_m / (1.0 - ADAM_B1 ** ADAM_STEP)
        v_hat = new_v / (1.0 - ADAM_B2 ** ADAM_STEP)
        g_ref[...] = g
        d_ref[...] = -ADAM_LR * (m_hat / (jnp.sqrt(v_hat) + ADAM_EPS) + ADAM_WD * w_ref[...])
        nm_ref[...] = new_m
        nv_ref[...] = new_v

    shape = jax.ShapeDtypeStruct((rows, cols), F32)
    return pl.pallas_call(
        body, name=name, grid=(rows // tile,), in_specs=[spec, part_spec, spec, spec],
        out_specs=[spec] * 4, out_shape=[shape] * 4, compiler_params=_params(("arbitrary",)),
    )(w, parts, m, v)


SMALL = ("ln1_g", "tok_mu", "w0", "a0", "k_k", "k_a", "r_k", "gn_w", "gn_b", "sb_gain", "ln2_g", "lnf_g")
EARLY = ("w_in", "w_decay_up", "w_aaa_up", "w_gate_up")
LATE = ("w_out", "w_up", "w_down")
BIG = EARLY + LATE
COL_SHARDED = ("w_in", "w_decay_up", "w_aaa_up", "w_gate_up", "w_up")
ORDER = ("ln1_g", "w_in", "tok_mu", "w0", "w_decay_up", "a0", "w_aaa_up", "w_gate_up", "k_k", "k_a", "r_k",
         "gn_w", "gn_b", "sb_gain", "w_out", "ln2_g", "w_up", "w_down", "lnf_g")


def _pack(vectors, rows):
    flat = jnp.concatenate([v.reshape(-1).astype(F32) for v in vectors])
    return jnp.pad(flat, (0, rows * LANES - flat.shape[0])).reshape(rows, LANES)


def _full_cols(gathered):
    d, k, cols = gathered.shape
    return jnp.transpose(gathered, (1, 0, 2)).reshape(k, d * cols)


def _col_parts(full):
    k, n = full.shape
    return jnp.transpose(full.reshape(k, N_DEV, n // N_DEV), (1, 0, 2))


def kernel(x, ln1_g, w_in, tok_mu, w0, w_decay_up, a0, w_aaa_up, w_gate_up, k_k, k_a, r_k, gn_w, gn_b, sb_gain, w_out, ln2_g, w_up, w_down, lnf_g, loss_target, m_ln1_g, m_w_in, m_tok_mu, m_w0, m_w_decay_up, m_a0, m_w_aaa_up, m_w_gate_up, m_k_k, m_k_a, m_r_k, m_gn_w, m_gn_b, m_sb_gain, m_w_out, m_ln2_g, m_w_up, m_w_down, m_lnf_g, v_ln1_g, v_w_in, v_tok_mu, v_w0, v_w_decay_up, v_a0, v_w_aaa_up, v_w_gate_up, v_k_k, v_k_a, v_r_k, v_gn_w, v_gn_b, v_sb_gain, v_w_out, v_ln2_g, v_w_up, v_w_down, v_lnf_g):
    args = dict(locals())
    weights = {n: args[n] for n in ORDER}
    mom_m = {n: args["m_" + n] for n in ORDER}
    mom_v = {n: args["v_" + n] for n in ORDER}

    n_seq, seq_len, d_model = x.shape
    n_tok = n_seq * seq_len
    x2d = x.reshape(n_tok, d_model)
    tgt = loss_target.reshape(n_tok, d_model)
    row = lambda t: t.reshape(1, -1).astype(F32)

    shard = {n: weights[n][0].astype(BF16) for n in BIG}
    gathered = dict(zip(EARLY, exchange("gather_early", [shard[n] for n in EARLY], True)))
    w_in_f = _full_cols(gathered["w_in"])
    zeros64 = jnp.zeros((HEAD_DIM, RWKV_W), BF16)
    wd_pad = jnp.concatenate([_full_cols(gathered["w_decay_up"]), zeros64], axis=0)
    wa_pad = jnp.concatenate([zeros64, _full_cols(gathered["w_aaa_up"])], axis=0)
    wg_f = _full_cols(gathered["w_gate_up"])
    in_cols = w_in_f.shape[1]

    g1, mu, w0r, a0r = row(ln1_g), row(tok_mu), row(w0), row(a0)
    kkr, kar, rkr, gwr, gbr, sgr = row(k_k), row(k_a), row(r_k), row(gn_w), row(gn_b), row(sb_gain)
    g2, gf = row(ln2_g), row(lnf_g)

    (h1,), _ = rw_call("norm1", lambda r, p: ([f_norm(r[0], p[0])], []), [x2d], [g1], [(d_model, BF16)], [], 512)
    p = matmul("proj_in", h1, w_in_f, "nn", F32, 1024, in_cols // 2, d_model)
    prep_pars = [mu, w0r, wd_pad, a0r, wa_pad, wg_f]
    (r_, kraw, v_, lw, asig, gate), prep_edges = rwkv_prep_fwd(p, prep_pars, seq_len)
    by_seq = lambda t: t.reshape(n_seq, seq_len, t.shape[-1])
    flat = lambda t: t.reshape(n_tok, t.shape[-1])
    rwkv_in = [by_seq(t) for t in (r_, kraw, v_, lw, asig)]
    z, s0_all, late = rwkv_fwd(*rwkv_in, kkr, kar, rkr, gwr, gbr, n_seq, seq_len,
                               Hosted([shard[n] for n in LATE], True))
    gathered = dict(zip(LATE, late))
    w_out_f = gathered["w_out"].reshape(d_model, d_model)
    w_up_f = _full_cols(gathered["w_up"])
    w_down_f = gathered["w_down"].reshape(-1, d_model)
    d_ff = w_up_f.shape[1]
    z = flat(z)
    y_sb, o_raw, tot, sb_first = sb_fwd(by_seq(p), sgr, n_seq, seq_len)
    y_sb = flat(y_sb)
    (ycat,), _ = rw_call("mix_cat", lambda r, q: ([jnp.concatenate([r[0] * r[1], r[2]], axis=1)], []),
                         [z, gate, y_sb], [], [(d_model, BF16)], [], 512)
    x2, h2 = matmul("proj_out", ycat, w_out_f, "nn", (F32, BF16), 512, d_model, d_model, extras=[x2d], pars=[g2],
                    epilogue=lambda acc, xv, g: (xv + acc, f_norm(xv + acc, g)))
    u, act = matmul("mlp_up", h2, w_up_f, "nn", (F32, BF16), 1024, d_ff // 4, d_model,
                    epilogue=lambda acc: (acc, jnp.square(jnp.maximum(acc, 0.0))))

    def loss_epilogue(acc, x2v, target, g):
        loss_rows, vjp = jax.vjp(lambda xv, gv: f_final(xv, gv, target), x2v + acc, g)
        dx3, dgf = vjp(jnp.ones_like(loss_rows))
        return dx3, dx3, jnp.broadcast_to(jnp.sum(loss_rows), (1, LANES)), dgf

    dx3, dx3_b, loss_acc, d_lnf = matmul(
        "mlp_down", act, w_down_f, "nn", (F32, BF16), 512, d_model, d_ff, extras=[x2, tgt], pars=[gf],
        epilogue=loss_epilogue, sums=[((1, LANES), F32), ((1, d_model), F32)])

    du = matmul("d_act", dx3_b, w_down_f, "nt", BF16, 1024, d_ff // 4, d_model, extras=[u],
                epilogue=lambda acc, uv: (acc * (2.0 * jnp.maximum(uv, 0.0)),))
    dw_down = matmul("dw_down", act, dx3_b, "tn", BF16, 512, d_model, 4096)
    dw_up = matmul("dw_up", h2, du, "tn", BF16, d_model, 512, 4096)

    def norm_bwd(acc, xv, dres, g):
        _, vjp = jax.vjp(f_norm, xv, g)
        dx, dg = vjp(acc)
        return dx + dres, dx + dres, dg

    dx2, dx2_b, d_ln2 = matmul("d_h2", du, w_up_f, "nt", (F32, BF16), 512, d_model, d_ff, extras=[x2, dx3], pars=[g2],
                               epilogue=norm_bwd, sums=[((1, d_model), F32)])

    dycat = matmul("d_ycat", dx2_b, w_out_f, "nt", F32, 512, d_model, d_model)
    dw_out = matmul("dw_out", ycat, dx2_b, "tn", BF16, d_model, d_model, 2048)
    (dz, dgate), _ = rw_call("d_mix", lambda r, q: ([r[0] * r[2], r[0] * r[1]], []),
                             [(dycat, RWKV_W, 0), z, gate], [], [(RWKV_W, F32)] * 2, [], 512)
    dq, dk_sb, dv_sb, d_sg = sb_bwd(by_seq(p), sgr, o_raw, tot, by_seq(dycat[:, RWKV_W:]), sb_first, n_seq, seq_len)
    d_sb = [flat(dq), flat(dk_sb), flat(dv_sb)]
    late_grads = {"w_out": dw_out.reshape(N_DEV, -1, d_model), "w_up": _col_parts(dw_up),
                  "w_down": dw_down.reshape(N_DEV, -1, d_model)}
    (dr, dkraw, dv, dlw, dasig, d_kk, d_ka, d_rk, d_gw, d_gb), late_parts = rwkv_bwd(
        *rwkv_in, kkr, kar, rkr, gwr, gbr, s0_all, by_seq(dz), n_seq, seq_len,
        Hosted([late_grads[n] for n in LATE], False))
    prep_cots = [flat(t) for t in (dr, dkraw, dv, dlw, dasig)] + [dgate]
    dp, (d_mu, d_w0, d_wd, d_a0, d_wa, d_wg) = rwkv_prep_bwd(p, prep_edges, prep_pars, prep_cots, d_sb, seq_len)

    dw_in = matmul("dw_in", h1, dp, "tn", BF16, d_model, in_cols // 2, 2048)
    lora_parts = lambda t: _col_parts(t).astype(BF16)
    early_grads = {"w_in": _col_parts(dw_in), "w_decay_up": lora_parts(d_wd[:HEAD_DIM]),
                   "w_aaa_up": lora_parts(d_wa[HEAD_DIM:]), "w_gate_up": lora_parts(d_wg)}
    dx, d_ln1, *early_parts = matmul(
        "d_h1", dp, w_in_f, "nt", (F32,), 512, d_model, in_cols, extras=[x2d, dx2], pars=[g1],
        epilogue=lambda *t: norm_bwd(*t)[1:], sums=[((1, d_model), F32)],
        hosted=Hosted([early_grads[n] for n in EARLY], False))
    parts = dict(zip(EARLY, early_parts))
    parts.update(zip(LATE, late_parts))

    small_grads = {"ln1_g": d_ln1, "tok_mu": d_mu, "w0": d_w0, "a0": d_a0, "k_k": d_kk, "k_a": d_ka, "r_k": d_rk,
                   "gn_w": d_gw, "gn_b": d_gb, "sb_gain": d_sg, "ln2_g": d_ln2, "lnf_g": d_lnf}
    n_small = sum(int(weights[n].size) for n in SMALL)
    pack_rows = -(-(n_small + 1) // (8 * LANES)) * 8
    packed = _pack([small_grads[n] for n in SMALL] + [loss_acc[0, :1]], pack_rows)
    (small_parts,) = exchange("gather_small", [packed], True)

    results = {}
    for n in BIG:
        w2d = weights[n][0]
        tile = w2d.shape[0] if w2d.shape[0] <= 256 else 256
        results[n] = adamw("adamw_" + n, w2d, parts[n], mom_m[n][0], mom_v[n][0], tile)
    pk = lambda d: _pack([d[n] for n in SMALL] + [jnp.zeros((1,), F32)], pack_rows)
    sg, sd, sm, sv = adamw("adamw_small", pk(weights), small_parts, pk(mom_m), pk(mom_v), pack_rows)
    off = 0
    for n in SMALL:
        size = int(weights[n].size)
        results[n] = tuple(t.reshape(-1)[off:off + size] for t in (sg, sd, sm, sv))
        off += size
    loss = sg.reshape(-1)[off]

    out = [loss, dx.reshape(x.shape)]
    for kind in range(4):
        out += [results[n][kind].reshape(weights[n].shape) for n in ORDER]
    return tuple(out)
```

```python
import functools
import math

import jax
import jax.numpy as jnp
from jax import lax
from jax.experimental import pallas as pl
from jax.experimental.pallas import tpu as pltpu

F32 = jnp.float32
BF16 = jnp.bfloat16

N_DEV = 8
HEAD_DIM = 64
LANES = 128
RWKV_W = 512
SB_W = 512
LORA_WA = 128
GATE_LORA = 128
RWKV_COLS = 3 * RWKV_W + LORA_WA + GATE_LORA
RMS_EPS = 1e-5
GN_EPS = 64e-5
CHUNK = 64
QB = 256
SB_SCALE = HEAD_DIM ** -0.5
ADAM_LR, ADAM_B1, ADAM_B2, ADAM_EPS, ADAM_WD, ADAM_STEP = 0.001, 0.9, 0.999, 1e-08, 0.01, 10
VMEM_LIMIT = 56 * 1024 * 1024


_DIMS = {
    "nn": (((1,), (0,)), ((), ())),
    "nt": (((1,), (1,)), ((), ())),
    "tn": (((0,), (0,)), ((), ())),
}


def _pieces(x, n):
    if n == 1:
        return [x.astype(BF16)]
    out, rem = [], x.astype(F32)
    for i in range(n):
        p = rem.astype(BF16)
        out.append(p)
        if i + 1 < n:
            rem = rem - p.astype(F32)
    return out


def _dot(a, b, form, pa, pb):
    pieces_a, pieces_b = _pieces(a, pa), _pieces(b, pb)
    keep = max(pa, pb)
    acc = None
    for i, ai in enumerate(pieces_a):
        for j, bj in enumerate(pieces_b):
            if i + j >= keep:
                continue
            t = lax.dot_general(ai, bj, _DIMS[form], preferred_element_type=F32)
            acc = t if acc is None else acc + t
    return acc


BOTH = (True, True)


@functools.partial(jax.custom_vjp, nondiff_argnums=(2, 3, 4, 5))
def mm(a, b, form, pa, pb, diff=BOTH):
    return _dot(a, b, form, pa, pb)


def _mm_fwd(a, b, form, pa, pb, diff):
    return _dot(a, b, form, pa, pb), (a, b)


def _mm_bwd(form, pa, pb, diff, res, g):
    a, b = res
    pg = max(pa, pb)
    da, db = jnp.zeros_like(a), jnp.zeros_like(b)
    if form == "nn":
        if diff[0]:
            da = mm(g, b, "nt", pg, pb)
        if diff[1]:
            db = mm(a, g, "tn", pa, pg)
    elif form == "nt":
        if diff[0]:
            da = mm(g, b, "nn", pg, pb)
        if diff[1]:
            db = mm(g, a, "tn", pg, pa)
    else:
        if diff[0]:
            da = mm(b, g, "nt", pb, pg)
        if diff[1]:
            db = mm(a, g, "nn", pa, pg)
    return da, db


mm.defvjp(_mm_fwd, _mm_bwd)


def _stack_rows(top, bottom):
    return jnp.concatenate([top, bottom], axis=0)


@jax.custom_vjp
def _split_rows(x):
    half = x.shape[0] // 2
    return x[:half], x[half:]


def _split_rows_fwd(x):
    return _split_rows(x), None


def _split_rows_bwd(_, grads):
    return (_stack_rows(*grads),)


_split_rows.defvjp(_split_rows_fwd, _split_rows_bwd)


def _lane_lo(shape):
    return lax.broadcasted_iota(jnp.int32, shape, len(shape) - 1) < HEAD_DIM


def _segsum(x):
    lo = _lane_lo(x.shape)
    s_lo = jnp.sum(jnp.where(lo, x, 0.0), axis=-1, keepdims=True)
    s_hi = jnp.sum(jnp.where(lo, 0.0, x), axis=-1, keepdims=True)
    return jnp.where(lo, s_lo, s_hi)


def _sigmoid(x):
    return 0.5 * (jnp.tanh(0.5 * x) + 1.0)


def _softplus(x):
    return jnp.maximum(x, 0.0) + jnp.log(1.0 + jnp.exp(-jnp.abs(x)))


def f_norm(x, g):
    return x * lax.rsqrt(jnp.mean(x * x, axis=-1, keepdims=True) + RMS_EPS) * g


def f_prep(p, pprev, mu, w0, wd_pad, a0, wa_pad, wg):
    pm = p + mu * (pprev - p)
    r = pm[:, 0:RWKV_W]
    k = pm[:, RWKV_W:2 * RWKV_W]
    v = pm[:, 2 * RWKV_W:3 * RWKV_W]
    xwa = pm[:, 3 * RWKV_W:3 * RWKV_W + LORA_WA]
    xg = pm[:, 3 * RWKV_W + LORA_WA:RWKV_COLS]
    w = -_softplus(-(w0 + mm(jnp.tanh(xwa), wd_pad, "nn", 1, 1))) - 0.5
    lw = -jnp.exp(w)
    asig = _sigmoid(a0 + mm(xwa, wa_pad, "nn", 1, 1))
    gate = mm(_sigmoid(xg), wg, "nn", 1, 1)
    return r, k, v, lw, asig, gate


def _tri(n, kind):
    row = lax.broadcasted_iota(jnp.int32, (n, n), 0)
    col = lax.broadcasted_iota(jnp.int32, (n, n), 1)
    if kind == "lower_incl":
        return row >= col
    return row > col


def rwkv_chunk(state, r, kraw, v, lw, asig, k_k, k_a, r_k, gn_w, gn_b, hp):
    n = len(r)
    L = r[0].shape[0]
    lo = _lane_lo((1, LANES))
    masks = (lo, jnp.logical_not(lo))
    incl = _tri(L, "lower_incl")
    strict = _tri(L, "strict")
    tri = incl.astype(F32)
    eye = (lax.broadcasted_iota(jnp.int32, (L, L), 0) == lax.broadcasted_iota(jnp.int32, (L, L), 1)).astype(F32)
    kk = [x * w for x, w in zip(kraw, k_k)]
    kk = [x / jnp.maximum(jnp.sqrt(_segsum(x * x)), 1e-12) for x in kk]
    k = [x * (1.0 + (s - 1.0) * w) for x, s, w in zip(kraw, asig, k_a)]
    b = [x * s for x, s in zip(kk, asig)]
    c = [mm(tri, x, "nn", 1, 3, (False, True)) for x in lw]
    at = [-x * jnp.exp(ci - li) for x, ci, li in zip(kk, c, lw)]
    rt = [x * jnp.exp(ci) for x, ci in zip(r, c)]
    einv = [jnp.exp(-ci) for ci in c]
    bt = [x * e for x, e in zip(b, einv)]
    kt = [x * e for x, e in zip(k, einv)]
    inst = [(s, m) for s in range(n) for m in masks]
    ar_h = [_stack_rows(jnp.where(m, at[s], 0.0), jnp.where(m, rt[s], 0.0)) for s, m in inst]
    on_b = [_split_rows(mm(x, bt[s], "nt", hp, hp)) for x, (s, _) in zip(ar_h, inst)]
    on_k = [_split_rows(mm(x, kt[s], "nt", hp, hp)) for x, (s, _) in zip(ar_h, inst)]
    a_ab = [jnp.where(strict, x, 0.0) for x, _ in on_b]
    b_rb = [jnp.where(incl, x, 0.0) for _, x in on_b]
    a_ak = [jnp.where(strict, x, 0.0) for x, _ in on_k]
    b_rk = [jnp.where(incl, x, 0.0) for _, x in on_k]
    tinv = [eye + x for x in a_ab]
    pw = [mm(x, x, "nn", hp, hp) for x in a_ab]
    for _ in range(int(math.log2(L)) - 2):
        both = [_split_rows(mm(_stack_rows(t, x), x, "nn", hp, hp)) for t, x in zip(tinv, pw)]
        tinv = [t + tx for t, (tx, _) in zip(tinv, both)]
        pw = [xx for _, xx in both]
    tinv = [t + mm(t, x, "nn", hp, hp) for t, x in zip(tinv, pw)]
    on_state = [_split_rows(mm(x, state[s], "nt", hp, hp)) for x, (s, _) in zip(ar_h, inst)]
    on_v = [_split_rows(mm(_stack_rows(m1, m2), v[s], "nn", hp, hp)) for m1, m2, (s, _) in zip(a_ak, b_rk, inst)]
    u_h = [mm(t, sa + av, "nn", hp, hp) for t, (sa, _), (av, _) in zip(tinv, on_state, on_v)]
    y_h = [sr + mm(m1, u, "nn", hp, hp) + bv for (_, sr), m1, u, (_, bv) in zip(on_state, b_rb, u_h, on_v)]
    u_all = [jnp.where(lo, u_h[2 * s], u_h[2 * s + 1]) for s in range(n)]
    y_all = [jnp.where(lo, y_h[2 * s], y_h[2 * s + 1]) for s in range(n)]
    c_last = [jnp.sum(x, axis=0, keepdims=True) for x in lw]
    efwd = [jnp.exp(cl - ci) for cl, ci in zip(c_last, c)]
    new_state = [st * jnp.exp(cl) + mm(_stack_rows(u, vi), _stack_rows(bi * e, ki * e), "tn", hp, hp)
                 for st, cl, u, bi, e, vi, ki in zip(state, c_last, u_all, b, efwd, v, k)]
    row_head = lax.broadcasted_iota(jnp.int32, (LANES, LANES), 0) // HEAD_DIM
    col_head = lax.broadcasted_iota(jnp.int32, (LANES, LANES), 1) // HEAD_DIM
    new_state = [jnp.where(row_head == col_head, x, 0.0) for x in new_state]
    outs = []
    for y, ri, ki, vi, w_rk, w_gw, w_gb in zip(y_all, r, k, v, r_k, gn_w, gn_b):
        mean = _segsum(y) * (1.0 / HEAD_DIM)
        d = y - mean
        var = _segsum(d * d) * (1.0 / HEAD_DIM)
        yn = d * lax.rsqrt(var + GN_EPS) * w_gw + w_gb
        outs.append(yn + _segsum(ri * ki * w_rk) * vi)
    return outs, new_state


def sb_tile(q, k, v, c_lo, c_hi, diag, from_here=None):
    n = len(q)
    lo = _lane_lo((1, LANES))
    below = _tri(QB, "strict")
    from_s = _tri(QB, "lower_incl").astype(F32)
    inst = [(s, h) for s in range(n) for h in (0, 1)]
    carry = [(c_lo[s], c_hi[s])[h] for s, h in inst]
    z = [mm(q[s][h], k[s], "nt", 1, 1) for s, h in inst]
    soft = [jnp.log(1.0 + jnp.exp(-jnp.abs(x))) for x in z]
    log_keep = [-jnp.maximum(x, 0.0) - sp for x, sp in zip(z, soft)]
    if diag:
        log_keep = [jnp.where(below, x, 0.0) for x in log_keep]
    own = [jnp.sum(x, axis=1, keepdims=True) for x in log_keep]
    if from_here is not None:
        carry = [lax.stop_gradient(from_here[s][h] - o) + cr for (s, h), o, cr in zip(inst, own, carry)]
    tail = [mm(x, from_s, "nn", SB_SUM_PIECES, 1, (True, False)) for x in log_keep]
    log_a = [x + tl + cr for x, tl, cr in zip(z, tail, carry)]
    if diag:
        log_a = [jnp.where(below, x, -1e30) for x in log_a]
    att = [jnp.exp(x) for x in log_a]
    out_h = [mm(x, v[s], "nn", 1, 1) for x, (s, _) in zip(att, inst)]
    out = [jnp.where(lo, out_h[2 * s], out_h[2 * s + 1]) for s in range(n)]
    return out, [own[2 * s] for s in range(n)], [own[2 * s + 1] for s in range(n)]


def sb_split_q(q):
    lo = _lane_lo((1, LANES))
    qs = q * SB_SCALE
    return jnp.where(lo, qs, 0.0), jnp.where(lo, 0.0, qs)


def sb_post(o, gain):
    return o * lax.rsqrt(_segsum(o * o) * (1.0 / HEAD_DIM) + RMS_EPS) * gain


def f_final(x3, g, target):
    y = f_norm(x3, g)
    err = y - target
    return 0.5 * jnp.mean(err * err, axis=-1, keepdims=True)


def _params(sem):
    return pltpu.CompilerParams(dimension_semantics=sem, vmem_limit_bytes=VMEM_LIMIT)


def rw_call(name, body_fn, rows, pars, out_rows, out_accs, tile):
    rows = [item if isinstance(item, tuple) else (item, item.shape[1], 0) for item in rows]
    row_arrays = [arr for arr, _, _ in rows]
    n_rows = row_arrays[0].shape[0]
    tile = min(tile, n_rows)
    steps = n_rows // tile
    row_specs = [pl.BlockSpec((tile, cols), functools.partial(lambda i, c: (i, c), c=cblk)) for _, cols, cblk in rows]
    par_specs = [pl.BlockSpec(p.shape, lambda i: (0, 0)) for p in pars]
    nr, npar, nor, noa = len(row_arrays), len(pars), len(out_rows), len(out_accs)

    def body(*refs):
        row_vals = [refs[i][...] for i in range(nr)]
        par_vals = [refs[nr + i][...] for i in range(npar)]
        o_refs = refs[nr + npar:nr + npar + nor]
        a_refs = refs[nr + npar + nor:]
        row_outs, acc_outs = body_fn(row_vals, par_vals)
        for ref, val in zip(o_refs, row_outs):
            ref[...] = val.astype(ref.dtype)
        if noa:
            first = pl.program_id(0) == 0

            @pl.when(first)
            def _():
                for ref, val in zip(a_refs, acc_outs):
                    ref[...] = val.astype(ref.dtype)

            @pl.when(jnp.logical_not(first))
            def _():
                for ref, val in zip(a_refs, acc_outs):
                    ref[...] = ref[...] + val.astype(ref.dtype)

    out_shape = [jax.ShapeDtypeStruct((n_rows, c), dt) for c, dt in out_rows]
    out_shape += [jax.ShapeDtypeStruct(s, dt) for s, dt in out_accs]
    out_specs = [pl.BlockSpec((tile, c), lambda i: (i, 0)) for c, _ in out_rows]
    out_specs += [pl.BlockSpec(s, lambda i: (0, 0)) for s, _ in out_accs]
    outs = pl.pallas_call(
        body, name=name, grid=(steps,), in_specs=row_specs + par_specs, out_specs=out_specs,
        out_shape=out_shape, compiler_params=_params(("arbitrary",)),
    )(*row_arrays, *pars)
    return outs[:nor], outs[nor:]


def matmul(name, a, b, form, out_dtype, tm, tn, tk, extras=(), pars=(), epilogue=None, sums=(), hosted=None):
    out_dtypes = out_dtype if isinstance(out_dtype, tuple) else (out_dtype,)
    tm, tn, tk = min(tm, a.shape[1 if form == "tn" else 0]), min(tn, b.shape[0 if form == "nt" else 1]), min(tk, a.shape[0 if form == "tn" else 1])
    if form == "nn":
        (m, kd), n = a.shape, b.shape[1]
        a_spec = pl.BlockSpec((tm, tk), lambda i, j, k: (i, k))
        b_spec = pl.BlockSpec((tk, tn), lambda i, j, k: (k, j))
    elif form == "nt":
        (m, kd), n = a.shape, b.shape[0]
        a_spec = pl.BlockSpec((tm, tk), lambda i, j, k: (i, k))
        b_spec = pl.BlockSpec((tn, tk), lambda i, j, k: (j, k))
    else:
        (kd, m), n = a.shape, b.shape[1]
        a_spec = pl.BlockSpec((tk, tm), lambda i, j, k: (k, i))
        b_spec = pl.BlockSpec((tk, tn), lambda i, j, k: (k, j))
    ksteps = kd // tk

    n_extra, n_par, n_out, n_sum = len(extras), len(pars), len(out_dtypes), len(sums)
    n_in = 2 + n_extra + n_par
    grid = (m // tm, n // tn, ksteps)

    def body(*refs):
        if hosted is not None:
            refs, copies = hosted.split(refs, n_in, n_out + n_sum)
            here = [pl.program_id(d) for d in range(3)]
            hosted.run(copies, functools.reduce(jnp.logical_and, [h == 0 for h in here]),
                       functools.reduce(jnp.logical_and, [h == g - 1 for h, g in zip(here, grid)]))
        a_ref, b_ref, rest = refs[0], refs[1], refs[2:]
        e_refs, o_refs = rest[:n_extra + n_par], rest[n_extra + n_par:n_extra + n_par + n_out]
        s_refs = rest[n_extra + n_par + n_out:n_extra + n_par + n_out + n_sum]
        kstep = pl.program_id(2)
        part = lax.dot_general(a_ref[...].astype(BF16), b_ref[...].astype(BF16), _DIMS[form],
                               preferred_element_type=F32)

        def finish(acc):
            outs = epilogue(acc, *[r[...] for r in e_refs]) if epilogue else (acc,)
            for ref, val in zip(o_refs, outs[:n_out]):
                ref[...] = val.astype(ref.dtype)
            if n_sum:
                first_tile = jnp.logical_and(pl.program_id(0) == 0, pl.program_id(1) == 0)

                @pl.when(first_tile)
                def _():
                    for ref, val in zip(s_refs, outs[n_out:]):
                        ref[...] = val.astype(ref.dtype)

                @pl.when(jnp.logical_not(first_tile))
                def _():
                    for ref, val in zip(s_refs, outs[n_out:]):
                        ref[...] = ref[...] + val.astype(ref.dtype)

        if ksteps == 1:
            finish(part)
            return
        acc_ref = rest[n_extra + n_par + n_out + n_sum]

        @pl.when(kstep == 0)
        def _():
            acc_ref[...] = part

        @pl.when(jnp.logical_and(kstep > 0, kstep < ksteps - 1))
        def _():
            acc_ref[...] = acc_ref[...] + part

        @pl.when(kstep == ksteps - 1)
        def _():
            finish(acc_ref[...] + part)

    out_spec = pl.BlockSpec((tm, tn), lambda i, j, k: (i, j))
    whole = lambda shape: pl.BlockSpec(shape, lambda i, j, k: (0,) * len(shape))
    n_host = hosted.n if hosted is not None else 0
    host_in = list(hosted.arrays) if hosted is not None else []
    host_out = exchange_shapes(hosted.arrays, hosted.gather) if hosted is not None else []
    outs = pl.pallas_call(
        body, name=name, grid=grid,
        in_specs=[a_spec, b_spec] + [out_spec] * n_extra + [whole(t.shape) for t in pars] + [ANY_SPEC] * n_host,
        out_specs=[out_spec] * n_out + [whole(s) for s, _ in sums] + [ANY_SPEC] * n_host,
        out_shape=[jax.ShapeDtypeStruct((m, n), dt) for dt in out_dtypes]
        + [jax.ShapeDtypeStruct(s, dt) for s, dt in sums] + host_out,
        scratch_shapes=([pltpu.VMEM((tm, tn), F32)] if ksteps > 1 else []) + (exchange_sems(n_host) if n_host else []),
        compiler_params=_params(("arbitrary",) * 3 if (sums or n_host) else ("parallel", "parallel", "arbitrary")),
    )(a, b, *extras, *pars, *host_in)
    return outs if (isinstance(out_dtype, tuple) or sums or n_host) else outs[0]


PREP_TILE = 256
PREP_TILE_BWD = 128
SUBLANES = 8


def _shift_in(rows, first):
    rolled = pltpu.roll(rows, 1, 0)
    row = lax.broadcasted_iota(jnp.int32, (SUBLANES, rows.shape[1]), 0)
    head = jnp.where(row == 0, first, rolled[0:SUBLANES])
    return jnp.concatenate([head, rolled[SUBLANES:]], axis=0), rolled


def rwkv_prep_fwd(p, pars, seq_len):
    n_tok = p.shape[0]
    tile = min(PREP_TILE, seq_len)
    tile_b = min(PREP_TILE_BWD, tile)
    steps, per_seq, sub = n_tok // tile, seq_len // tile, tile // tile_b
    n_par = len(pars)

    def body(p_ref, *rest):
        par_refs, out_refs, edge_ref, last8 = rest[:n_par], rest[n_par:n_par + 6], rest[n_par + 6], rest[n_par + 7]
        step = pl.program_id(0)

        @pl.when(step == 0)
        def _():
            last8[...] = jnp.zeros_like(last8)

        rows = p_ref[...]
        before = jnp.where(step % per_seq == 0, 0.0, pltpu.roll(last8[...], 1, 0))
        prev, rolled = _shift_in(rows, before)
        edge_ref[0] = prev[0:SUBLANES]
        for m in range(1, sub):
            edge_ref[m] = rolled[m * tile_b:m * tile_b + SUBLANES]
        last8[...] = rows[tile - SUBLANES:tile]
        for ref, val in zip(out_refs, f_prep(rows, prev, *[r[...] for r in par_refs])):
            ref[...] = val

    row_out = pl.BlockSpec((tile, RWKV_W), lambda i: (i, 0))
    outs = pl.pallas_call(
        body, name="rwkv_prep", grid=(steps,),
        in_specs=[pl.BlockSpec((tile, RWKV_COLS), lambda i: (i, 0))] + [pl.BlockSpec(t.shape, lambda i: (0, 0)) for t in pars],
        out_specs=[row_out] * 6 + [pl.BlockSpec((sub, SUBLANES, RWKV_COLS), lambda i: (i, 0, 0))],
        out_shape=[jax.ShapeDtypeStruct((n_tok, RWKV_W), F32)] * 6
        + [jax.ShapeDtypeStruct((steps * sub, SUBLANES, RWKV_COLS), F32)],
        scratch_shapes=[pltpu.VMEM((SUBLANES, RWKV_COLS), F32)],
        compiler_params=_params(("arbitrary",)),
    )(p, *pars)
    return outs[:6], outs[6]


def rwkv_prep_bwd(p, edges, pars, cots, d_sb, seq_len):
    n_tok = p.shape[0]
    tile = min(PREP_TILE_BWD, seq_len)
    steps, per_seq = n_tok // tile, seq_len // tile
    n_par = len(pars)
    back = lambda i: steps - 1 - i

    def body(p_ref, edge_ref, *rest):
        par_refs, rest = rest[:n_par], rest[n_par:]
        cot_refs, sb_refs, dp_ref, acc_refs, next8 = rest[:6], rest[6:9], rest[9], rest[10:10 + n_par], rest[10 + n_par]
        step = pl.program_id(0)
        first = step == 0

        @pl.when(first)
        def _():
            next8[...] = jnp.zeros_like(next8)

        rows = p_ref[...]
        prev, _ = _shift_in(rows, edge_ref[0])
        _, vjp = jax.vjp(f_prep, rows, prev, *[r[...].astype(F32) for r in par_refs])
        grads = vjp(tuple(r[...] for r in cot_refs))
        d_rows, d_prev = grads[0], grads[1]
        up = pltpu.roll(d_prev, tile - 1, 0)
        ends_seq = back(step) % per_seq == per_seq - 1
        after = jnp.where(ends_seq, 0.0, pltpu.roll(next8[...], SUBLANES - 1, 0))
        row = lax.broadcasted_iota(jnp.int32, (SUBLANES, RWKV_COLS), 0)
        tail = jnp.where(row == SUBLANES - 1, after, up[tile - SUBLANES:tile])
        d_rows = d_rows + jnp.concatenate([up[:tile - SUBLANES], tail], axis=0)
        next8[...] = d_prev[0:SUBLANES]
        dp_ref[...] = jnp.concatenate([d_rows] + [r[...] for r in sb_refs], axis=1).astype(dp_ref.dtype)

        @pl.when(first)
        def _():
            for ref, val in zip(acc_refs, grads[2:]):
                ref[...] = val

        @pl.when(jnp.logical_not(first))
        def _():
            for ref, val in zip(acc_refs, grads[2:]):
                ref[...] = ref[...] + val

    cols = RWKV_COLS + sum(t.shape[1] for t in d_sb)
    half = pl.BlockSpec((tile, RWKV_W), lambda i: (back(i), 0))
    par_specs = [pl.BlockSpec(t.shape, lambda i: (0, 0)) for t in pars]
    outs = pl.pallas_call(
        body, name="d_rwkv_prep", grid=(steps,),
        in_specs=[pl.BlockSpec((tile, RWKV_COLS), lambda i: (back(i), 0)),
                  pl.BlockSpec((1, SUBLANES, RWKV_COLS), lambda i: (back(i), 0, 0))] + par_specs + [half] * 9,
        out_specs=[pl.BlockSpec((tile, cols), lambda i: (back(i), 0))] + par_specs,
        out_shape=[jax.ShapeDtypeStruct((n_tok, cols), BF16)] + [jax.ShapeDtypeStruct(t.shape, F32) for t in pars],
        scratch_shapes=[pltpu.VMEM((SUBLANES, RWKV_COLS), F32)],
        compiler_params=_params(("arbitrary",)),
    )(p, edges, *pars, *cots, *d_sb)
    return outs[0], outs[1:]


RWKV_HP = 1


RWKV_PAIRS = 4


def _rwkv_specs(n_seq, chunk_of):
    width = RWKV_PAIRS * LANES
    row = pl.BlockSpec((n_seq, CHUNK, width), lambda g, c: (0, chunk_of(c), g))
    par = pl.BlockSpec((1, width), lambda g, c: (0, g))
    s0 = pl.BlockSpec((1, 1, RWKV_PAIRS * n_seq, LANES, LANES), lambda g, c: (g, chunk_of(c), 0, 0, 0))
    return row, par, s0


class Hosted:
    def __init__(self, arrays, gather):
        self.arrays, self.gather, self.n = list(arrays), gather, len(arrays)

    def split(self, refs, n_in, n_out):
        n = self.n
        ins, outs, sems = refs[n_in:n_in + n], refs[n_in + n + n_out:n_in + 2 * n + n_out], refs[-3:]
        own = refs[:n_in] + refs[n_in + n:n_in + n + n_out] + refs[n_in + 2 * n + n_out:-3]
        return own, exchange_copies(ins, outs, *sems, self.gather)

    def run(self, copies, first, last):
        @pl.when(first)
        def _():
            for cp in copies:
                cp.start()

        @pl.when(last)
        def _():
            for cp in copies:
                cp.wait()


def rwkv_fwd(r, kraw, v, lw, asig, k_k, k_a, r_k, gn_w, gn_b, n_seq, seq_len, hosted):
    n_chunks = seq_len // CHUNK
    n_groups = RWKV_W // (RWKV_PAIRS * LANES)
    n_inst = RWKV_PAIRS * n_seq
    row, par, s0_spec = _rwkv_specs(n_seq, lambda c: c)
    inst = [(s, pl.ds(pp * LANES, LANES)) for pp in range(RWKV_PAIRS) for s in range(n_seq)]

    def body(*refs):
        own, copies = hosted.split(refs, 10, 2)
        row_refs, par_refs, (z_ref, s0_ref, state) = own[:5], own[5:10], own[10:]
        step = pl.program_id(0) * n_chunks + pl.program_id(1)
        hosted.run(copies, step == 0, step == n_groups * n_chunks - 1)

        @pl.when(pl.program_id(1) == 0)
        def _():
            state[...] = jnp.zeros_like(state)

        s0 = [state[i] for i in range(n_inst)]
        rows = [[ref[s, :, lanes] for s, lanes in inst] for ref in row_refs]
        pars = [[ref[:, lanes] for _, lanes in inst] for ref in par_refs]
        z, s1 = rwkv_chunk(s0, *rows, *pars, RWKV_HP)
        for i, (s, lanes) in enumerate(inst):
            s0_ref[0, 0, i] = s0[i]
            z_ref[s, :, lanes] = z[i]
            state[i] = s1[i]

    outs = pl.pallas_call(
        body, name="rwkv_fwd", grid=(n_groups, n_chunks),
        in_specs=[row] * 5 + [par] * 5 + [ANY_SPEC] * hosted.n, out_specs=[row, s0_spec] + [ANY_SPEC] * hosted.n,
        out_shape=[jax.ShapeDtypeStruct(r.shape, F32),
                   jax.ShapeDtypeStruct((n_groups, n_chunks, n_inst, LANES, LANES), F32)]
        + exchange_shapes(hosted.arrays, hosted.gather),
        scratch_shapes=[pltpu.VMEM((n_inst, LANES, LANES), F32)] + exchange_sems(hosted.n),
        compiler_params=_params(("arbitrary", "arbitrary")),
    )(r, kraw, v, lw, asig, k_k, k_a, r_k, gn_w, gn_b, *hosted.arrays)
    return outs[0], outs[1], outs[2:]


def rwkv_bwd(r, kraw, v, lw, asig, k_k, k_a, r_k, gn_w, gn_b, s0_all, dz, n_seq, seq_len, hosted):
    n_chunks = seq_len // CHUNK
    n_groups = RWKV_W // (RWKV_PAIRS * LANES)
    n_inst = RWKV_PAIRS * n_seq
    row, par, s0_spec = _rwkv_specs(n_seq, lambda c: n_chunks - 1 - c)
    inst = [(s, pl.ds(pp * LANES, LANES)) for pp in range(RWKV_PAIRS) for s in range(n_seq)]

    def body(*refs):
        own, copies = hosted.split(refs, 12, 10)
        row_refs, par_refs, s0_ref, dz_ref = own[:5], own[5:10], own[10], own[11]
        drow_refs, dpar_refs, dstate = own[12:17], own[17:22], own[22]
        step = pl.program_id(0) * n_chunks + pl.program_id(1)
        hosted.run(copies, step == 0, step == n_groups * n_chunks - 1)
        first = pl.program_id(1) == 0

        @pl.when(first)
        def _():
            dstate[...] = jnp.zeros_like(dstate)

        fn = functools.partial(rwkv_chunk, hp=RWKV_HP)
        rows = [[ref[s, :, lanes] for s, lanes in inst] for ref in row_refs]
        pars = [[ref[:, lanes] for _, lanes in inst] for ref in par_refs]
        _, vjp = jax.vjp(fn, [s0_ref[0, 0, i] for i in range(n_inst)], *rows, *pars)
        grads = vjp(([dz_ref[s, :, lanes] for s, lanes in inst], [dstate[i] for i in range(n_inst)]))
        for i, (s, lanes) in enumerate(inst):
            dstate[i] = grads[0][i]
            for ref, val in zip(drow_refs, grads[1:6]):
                ref[s, :, lanes] = val[i]

        def accumulate(start):
            for ref, val in zip(dpar_refs, grads[6:]):
                for pp in range(RWKV_PAIRS):
                    lanes = pl.ds(pp * LANES, LANES)
                    total = functools.reduce(jnp.add, val[pp * n_seq:(pp + 1) * n_seq])
                    ref[:, lanes] = total if start else ref[:, lanes] + total

        @pl.when(first)
        def _():
            accumulate(True)

        @pl.when(jnp.logical_not(first))
        def _():
            accumulate(False)

    rows_shape = jax.ShapeDtypeStruct(r.shape, F32)
    par_shape = jax.ShapeDtypeStruct((1, RWKV_W), F32)
    outs = pl.pallas_call(
        body, name="rwkv_bwd", grid=(n_groups, n_chunks),
        in_specs=[row] * 5 + [par] * 5 + [s0_spec, row] + [ANY_SPEC] * hosted.n,
        out_specs=[row] * 5 + [par] * 5 + [ANY_SPEC] * hosted.n,
        out_shape=[rows_shape] * 5 + [par_shape] * 5 + exchange_shapes(hosted.arrays, hosted.gather),
        scratch_shapes=[pltpu.VMEM((n_inst, LANES, LANES), F32)] + exchange_sems(hosted.n),
        compiler_params=_params(("arbitrary", "arbitrary")),
    )(r, kraw, v, lw, asig, k_k, k_a, r_k, gn_w, gn_b, s0_all, dz, *hosted.arrays)
    return outs[:10], outs[10:]


SB_Q0 = RWKV_COLS // LANES
SB_K0 = SB_Q0 + SB_W // LANES
SB_V0 = SB_K0 + SB_W // LANES
SB_SEQS = 2
SB_BUFFERS = pl.Buffered(1)
SB_SUM_PIECES = 2
SB_DEAD = -110.0


def _col_of(c_lo, c_hi):
    return jnp.where(_lane_lo((1, LANES)), c_lo, c_hi)


def sb_fwd(p, gain, n_seq, seq_len):
    n_pairs = SB_W // LANES
    n_q = seq_len // QB
    nb = min(SB_SEQS, n_seq)

    def seq_spec(c0):
        return pl.BlockSpec((nb, seq_len, LANES), functools.partial(lambda b, h, c0: (b, 0, c0 + h), c0=c0),
                            pipeline_mode=SB_BUFFERS)

    out_spec = pl.BlockSpec((nb, seq_len, LANES), lambda b, h: (b, 0, h), pipeline_mode=SB_BUFFERS)

    def body(q_ref, k_ref, v_ref, g_ref, y_ref, o_ref, tot_ref, first_ref):
        gain = g_ref[...]

        def q_block(i, _):
            qs = pl.multiple_of(i * QB, QB)
            seqs = range(nb)
            zeros = [jnp.zeros((QB, 1), F32)] * nb
            qv = [sb_split_q(q_ref[s, pl.ds(qs, QB), :]) for s in seqs]
            add = lambda xs, ys: [x + y for x, y in zip(xs, ys)]

            def tiles(ks, c_lo, c_hi, diag):
                return sb_tile(qv, [k_ref[s, pl.ds(ks, QB), :] for s in seqs],
                               [v_ref[s, pl.ds(ks, QB), :] for s in seqs], c_lo, c_hi, diag)

            def alive(c_lo, c_hi):
                top = functools.reduce(jnp.maximum, list(c_lo) + list(c_hi))
                return jnp.max(top) > SB_DEAD

            def k_block(state):
                j, _, (o, c_lo, c_hi) = state
                o2, s_lo, s_hi = tiles(pl.multiple_of(j * QB, QB), c_lo, c_hi, False)
                c_lo, c_hi = add(c_lo, s_lo), add(c_hi, s_hi)
                return j - 1, alive(c_lo, c_hi), (add(o, o2), c_lo, c_hi)

            o, c_lo, c_hi = tiles(qs, zeros, zeros, True)
            j, _, (o, c_lo, c_hi) = lax.while_loop(lambda st: jnp.logical_and(st[0] >= 0, st[1]), k_block,
                                                   (i - 1, alive(c_lo, c_hi), (o, c_lo, c_hi)))
            first_ref[pl.program_id(0), pl.program_id(1), i] = j + 1
            for s in seqs:
                o_ref[s, pl.ds(qs, QB), :] = o[s]
                tot_ref[s, pl.ds(qs, QB), :] = jnp.broadcast_to(_col_of(c_lo[s], c_hi[s]), (QB, LANES))
                y_ref[s, pl.ds(qs, QB), :] = sb_post(o[s], gain)
            return 0

        lax.fori_loop(0, n_q, q_block, 0)

    shape = jax.ShapeDtypeStruct((n_seq, seq_len, SB_W), F32)
    return pl.pallas_call(
        body, name="sb_fwd", grid=(n_seq // nb, n_pairs),
        in_specs=[seq_spec(SB_Q0), seq_spec(SB_K0), seq_spec(SB_V0), pl.BlockSpec((1, LANES), lambda b, h: (0, h))],
        out_specs=[out_spec] * 3 + [pl.BlockSpec(memory_space=pltpu.SMEM)],
        out_shape=[shape] * 3 + [jax.ShapeDtypeStruct((n_seq // nb, n_pairs, n_q), jnp.int32)],
        compiler_params=_params(("arbitrary", "arbitrary")),
    )(p, p, p, gain)


def sb_bwd(p, gain, o_raw, tot, dy, first, n_seq, seq_len):
    n_pairs = SB_W // LANES
    n_q = seq_len // QB
    nb = min(SB_SEQS, n_seq)

    def seq_spec(c0):
        return pl.BlockSpec((nb, seq_len, LANES), functools.partial(lambda h, b, c0: (b, 0, c0 + h), c0=c0),
                            pipeline_mode=SB_BUFFERS)

    own = pl.BlockSpec((nb, seq_len, LANES), lambda h, b: (b, 0, h), pipeline_mode=SB_BUFFERS)
    par = pl.BlockSpec((1, LANES), lambda h, b: (0, h))

    def body(q_ref, k_ref, v_ref, g_ref, o_ref, tot_ref, dy_ref, first_ref, dq_ref, dk_ref, dv_ref, dg_ref):
        gain = g_ref[...]
        lo = _lane_lo((1, LANES))
        dk_ref[...] = jnp.zeros_like(dk_ref)
        dv_ref[...] = jnp.zeros_like(dv_ref)

        def q_block(i, dgain):
            qs = pl.multiple_of(i * QB, QB)
            seqs = range(nb)
            zeros = [jnp.zeros((QB, 1), F32)] * nb
            qv, dov, t_lo, t_hi = [], [], [], []
            for s in seqs:
                qv.append(sb_split_q(q_ref[s, pl.ds(qs, QB), :]))
                _, post_vjp = jax.vjp(sb_post, o_ref[s, pl.ds(qs, QB), :], gain)
                do, dg_s = post_vjp(dy_ref[s, pl.ds(qs, QB), :])
                dov.append(do)
                dgain = dgain + dg_s
                tot_s = tot_ref[s, pl.ds(qs, QB), :]
                t_lo.append(jnp.max(jnp.where(lo, tot_s, -jnp.inf), axis=1, keepdims=True))
                t_hi.append(jnp.max(jnp.where(lo, -jnp.inf, tot_s), axis=1, keepdims=True))
            add = lambda xs, ys: [x + y for x, y in zip(xs, ys)]
            sub = lambda xs, ys: [x - y for x, y in zip(xs, ys)]

            def tile(ks, carry, diag):
                dq, rem_lo, rem_hi, g_lo, g_hi = carry
                kv = [k_ref[s, pl.ds(ks, QB), :] for s in seqs]
                vv = [v_ref[s, pl.ds(ks, QB), :] for s in seqs]
                fn = functools.partial(sb_tile, diag=diag, from_here=list(zip(rem_lo, rem_hi)))
                (_, s_lo, s_hi), vjp = jax.vjp(fn, qv, kv, vv, zeros, zeros)
                dq_t, dk_t, dv_t, dc_lo, dc_hi = vjp((dov, g_lo, g_hi))
                dq_t = [jnp.where(lo, d_lo, d_hi) for d_lo, d_hi in dq_t]
                for s in seqs:
                    dk_ref[s, pl.ds(ks, QB), :] = dk_ref[s, pl.ds(ks, QB), :] + dk_t[s]
                    dv_ref[s, pl.ds(ks, QB), :] = dv_ref[s, pl.ds(ks, QB), :] + dv_t[s]
                return add(dq, dq_t), sub(rem_lo, s_lo), sub(rem_hi, s_hi), add(g_lo, dc_lo), add(g_hi, dc_hi)

            def k_block(j, carry):
                return tile(pl.multiple_of(j * QB, QB), carry, False)

            carry = ([jnp.zeros((QB, LANES), F32)] * nb, t_lo, t_hi, zeros, zeros)
            carry = lax.fori_loop(first_ref[pl.program_id(1), pl.program_id(0), i], i, k_block, carry)
            carry = tile(qs, carry, True)
            for s in seqs:
                dq_ref[s, pl.ds(qs, QB), :] = carry[0][s] * SB_SCALE
            return dgain

        dgain = lax.fori_loop(0, n_q, q_block, jnp.zeros((1, LANES), F32))
        first = pl.program_id(1) == 0

        @pl.when(first)
        def _():
            dg_ref[...] = dgain

        @pl.when(jnp.logical_not(first))
        def _():
            dg_ref[...] = dg_ref[...] + dgain

    shape = jax.ShapeDtypeStruct((n_seq, seq_len, SB_W), F32)
    return pl.pallas_call(
        body, name="sb_bwd", grid=(n_pairs, n_seq // nb),
        in_specs=[seq_spec(SB_Q0), seq_spec(SB_K0), seq_spec(SB_V0), par, own, own, own,
                  pl.BlockSpec(memory_space=pltpu.SMEM)],
        out_specs=[own, own, own, par],
        out_shape=[shape, shape, shape, jax.ShapeDtypeStruct((1, SB_W), F32)],
        compiler_params=_params(("arbitrary", "arbitrary")),
    )(p, p, p, gain, o_raw, tot, dy, first)


def exchange(name, arrays, gather):
    n = len(arrays)

    def body(*refs):
        copies = exchange_copies(refs[:n], refs[n:2 * n], *refs[2 * n:], gather)
        for cp in copies:
            cp.start()
        for cp in copies:
            cp.wait()

    return pl.pallas_call(
        body, name=name, in_specs=[ANY_SPEC] * n, out_specs=[ANY_SPEC] * n, out_shape=exchange_shapes(arrays, gather),
        scratch_shapes=exchange_sems(n),
    )(*arrays)


ANY_SPEC = pl.BlockSpec(memory_space=pl.ANY)


def exchange_shapes(arrays, gather):
    return [jax.ShapeDtypeStruct(((N_DEV,) + a.shape) if gather else a.shape, a.dtype) for a in arrays]


def exchange_sems(n):
    return [pltpu.SemaphoreType.DMA((n, N_DEV - 1)), pltpu.SemaphoreType.DMA((n, N_DEV - 1)),
            pltpu.SemaphoreType.DMA((n,))]


def exchange_copies(ins, outs, send_sems, recv_sems, local_sems, gather):
    x, y, c = lax.axis_index("x"), lax.axis_index("y"), lax.axis_index("c")
    me = 4 * x + 2 * y + c
    copies = []
    for a, (src_all, dst_all) in enumerate(zip(ins, outs)):
        own = src_all if gather else src_all.at[me]
        copies.append(pltpu.make_async_copy(own, dst_all.at[me], local_sems.at[a]))
        for j in range(1, N_DEV):
            px, py, pc = (x + (j >> 2)) % 2, (y + ((j >> 1) & 1)) % 2, (c + (j & 1)) % 2
            src = src_all if gather else src_all.at[4 * px + 2 * py + pc]
            copies.append(pltpu.make_async_remote_copy(
                src_ref=src, dst_ref=dst_all.at[me], send_sem=send_sems.at[a, j - 1],
                recv_sem=recv_sems.at[a, j - 1], device_id=(px, py, pc), device_id_type=pl.DeviceIdType.MESH))
    return copies


def adamw(name, w, parts, m, v, tile):
    rows, cols = w.shape
    spec = pl.BlockSpec((tile, cols), lambda i: (i, 0))
    part_spec = pl.BlockSpec((N_DEV, tile, cols), lambda i: (0, i, 0))

    def body(w_ref, p_ref, m_ref, v_ref, g_ref, d_ref, nm_ref, nv_ref):
        g = p_ref[0].astype(F32)
        for s in range(1, N_DEV):
            g = g + p_ref[s].astype(F32)
        new_m = ADAM_B1 * m_ref[...] + (1.0 - ADAM_B1) * g
        new_v = ADAM_B2 * v_ref[...] + (1.0 - ADAM_B2) * (g * g)
        m_hat = new_m / (1.0 - ADAM_B1 ** ADAM_STEP)
        v_hat = new_v / (1.0 - ADAM_B2 ** ADAM_STEP)
        g_ref[...] = g
        d_ref[...] = -ADAM_LR * (m_hat / (jnp.sqrt(v_hat) + ADAM_EPS) + ADAM_WD * w_ref[...])
        nm_ref[...] = new_m
        nv_ref[...] = new_v

    shape = jax.ShapeDtypeStruct((rows, cols), F32)
    return pl.pallas_call(
        body, name=name, grid=(rows // tile,), in_specs=[spec, part_spec, spec, spec],
        out_specs=[spec] * 4, out_shape=[shape] * 4, compiler_params=_params(("arbitrary",)),
    )(w, parts, m, v)


SMALL = ("ln1_g", "tok_mu", "w0", "a0", "k_k", "k_a", "r_k", "gn_w", "gn_b", "sb_gain", "ln2_g", "lnf_g")
EARLY = ("w_in", "w_decay_up", "w_aaa_up", "w_gate_up")
LATE = ("w_out", "w_up", "w_down")
BIG = EARLY + LATE
COL_SHARDED = ("w_in", "w_decay_up", "w_aaa_up", "w_gate_up", "w_up")
ORDER = ("ln1_g", "w_in", "tok_mu", "w0", "w_decay_up", "a0", "w_aaa_up", "w_gate_up", "k_k", "k_a", "r_k",
         "gn_w", "gn_b", "sb_gain", "w_out", "ln2_g", "w_up", "w_down", "lnf_g")


def _pack(vectors, rows):
    flat = jnp.concatenate([v.reshape(-1).astype(F32) for v in vectors])
    return jnp.pad(flat, (0, rows * LANES - flat.shape[0])).reshape(rows, LANES)


def _full_cols(gathered):
    d, k, cols = gathered.shape
    return jnp.transpose(gathered, (1, 0, 2)).reshape(k, d * cols)


def _col_parts(full):
    k, n = full.shape
    return jnp.transpose(full.reshape(k, N_DEV, n // N_DEV), (1, 0, 2))


def kernel(x, ln1_g, w_in, tok_mu, w0, w_decay_up, a0, w_aaa_up, w_gate_up, k_k, k_a, r_k, gn_w, gn_b, sb_gain, w_out, ln2_g, w_up, w_down, lnf_g, loss_target, m_ln1_g, m_w_in, m_tok_mu, m_w0, m_w_decay_up, m_a0, m_w_aaa_up, m_w_gate_up, m_k_k, m_k_a, m_r_k, m_gn_w, m_gn_b, m_sb_gain, m_w_out, m_ln2_g, m_w_up, m_w_down, m_lnf_g, v_ln1_g, v_w_in, v_tok_mu, v_w0, v_w_decay_up, v_a0, v_w_aaa_up, v_w_gate_up, v_k_k, v_k_a, v_r_k, v_gn_w, v_gn_b, v_sb_gain, v_w_out, v_ln2_g, v_w_up, v_w_down, v_lnf_g):
    args = dict(locals())
    weights = {n: args[n] for n in ORDER}
    mom_m = {n: args["m_" + n] for n in ORDER}
    mom_v = {n: args["v_" + n] for n in ORDER}

    n_seq, seq_len, d_model = x.shape
    n_tok = n_seq * seq_len
    x2d = x.reshape(n_tok, d_model)
    tgt = loss_target.reshape(n_tok, d_model)
    row = lambda t: t.reshape(1, -1).astype(F32)

    shard = {n: weights[n][0].astype(BF16) for n in BIG}
    gathered = dict(zip(EARLY, exchange("gather_early", [shard[n] for n in EARLY], True)))
    w_in_f = _full_cols(gathered["w_in"])
    zeros64 = jnp.zeros((HEAD_DIM, RWKV_W), BF16)
    wd_pad = jnp.concatenate([_full_cols(gathered["w_decay_up"]), zeros64], axis=0)
    wa_pad = jnp.concatenate([zeros64, _full_cols(gathered["w_aaa_up"])], axis=0)
    wg_f = _full_cols(gathered["w_gate_up"])
    in_cols = w_in_f.shape[1]

    g1, mu, w0r, a0r = row(ln1_g), row(tok_mu), row(w0), row(a0)
    kkr, kar, rkr, gwr, gbr, sgr = row(k_k), row(k_a), row(r_k), row(gn_w), row(gn_b), row(sb_gain)
    g2, gf = row(ln2_g), row(lnf_g)

    (h1,), _ = rw_call("norm1", lambda r, p: ([f_norm(r[0], p[0])], []), [x2d], [g1], [(d_model, BF16)], [], 512)
    p = matmul("proj_in", h1, w_in_f, "nn", F32, 1024, in_cols // 2, d_model)
    prep_pars = [mu, w0r, wd_pad, a0r, wa_pad, wg_f]
    (r_, kraw, v_, lw, asig, gate), prep_edges = rwkv_prep_fwd(p, prep_pars, seq_len)
    by_seq = lambda t: t.reshape(n_seq, seq_len, t.shape[-1])
    flat = lambda t: t.reshape(n_tok, t.shape[-1])
    rwkv_in = [by_seq(t) for t in (r_, kraw, v_, lw, asig)]
    z, s0_all, late = rwkv_fwd(*rwkv_in, kkr, kar, rkr, gwr, gbr, n_seq, seq_len,
                               Hosted([shard[n] for n in LATE], True))
    gathered = dict(zip(LATE, late))
    w_out_f = gathered["w_out"].reshape(d_model, d_model)
    w_up_f = _full_cols(gathered["w_up"])
    w_down_f = gathered["w_down"].reshape(-1, d_model)
    d_ff = w_up_f.shape[1]
    z = flat(z)
    y_sb, o_raw, tot, sb_first = sb_fwd(by_seq(p), sgr, n_seq, seq_len)
    y_sb = flat(y_sb)
    (ycat,), _ = rw_call("mix_cat", lambda r, q: ([jnp.concatenate([r[0] * r[1], r[2]], axis=1)], []),
                         [z, gate, y_sb], [], [(d_model, BF16)], [], 512)
    x2, h2 = matmul("proj_out", ycat, w_out_f, "nn", (F32, BF16), 512, d_model, d_model, extras=[x2d], pars=[g2],
                    epilogue=lambda acc, xv, g: (xv + acc, f_norm(xv + acc, g)))
    u, act = matmul("mlp_up", h2, w_up_f, "nn", (F32, BF16), 1024, d_ff // 4, d_model,
                    epilogue=lambda acc: (acc, jnp.square(jnp.maximum(acc, 0.0))))

    def loss_epilogue(acc, x2v, target, g):
        loss_rows, vjp = jax.vjp(lambda xv, gv: f_final(xv, gv, target), x2v + acc, g)
        dx3, dgf = vjp(jnp.ones_like(loss_rows))
        return dx3, dx3, jnp.broadcast_to(jnp.sum(loss_rows), (1, LANES)), dgf

    dx3, dx3_b, loss_acc, d_lnf = matmul(
        "mlp_down", act, w_down_f, "nn", (F32, BF16), 512, d_model, d_ff, extras=[x2, tgt], pars=[gf],
        epilogue=loss_epilogue, sums=[((1, LANES), F32), ((1, d_model), F32)])

    du = matmul("d_act", dx3_b, w_down_f, "nt", BF16, 1024, d_ff // 4, d_model, extras=[u],
                epilogue=lambda acc, uv: (acc * (2.0 * jnp.maximum(uv, 0.0)),))
    dw_down = matmul("dw_down", act, dx3_b, "tn", BF16, 512, d_model, 4096)
    dw_up = matmul("dw_up", h2, du, "tn", BF16, d_model, 512, 4096)

    def norm_bwd(acc, xv, dres, g):
        _, vjp = jax.vjp(f_norm, xv, g)
        dx, dg = vjp(acc)
        return dx + dres, dx + dres, dg

    dx2, dx2_b, d_ln2 = matmul("d_h2", du, w_up_f, "nt", (F32, BF16), 512, d_model, d_ff, extras=[x2, dx3], pars=[g2],
                               epilogue=norm_bwd, sums=[((1, d_model), F32)])

    dycat = matmul("d_ycat", dx2_b, w_out_f, "nt", F32, 512, d_model, d_model)
    dw_out = matmul("dw_out", ycat, dx2_b, "tn", BF16, d_model, d_model, 2048)
    (dz, dgate), _ = rw_call("d_mix", lambda r, q: ([r[0] * r[2], r[0] * r[1]], []),
                             [(dycat, RWKV_W, 0), z, gate], [], [(RWKV_W, F32)] * 2, [], 512)
    dq, dk_sb, dv_sb, d_sg = sb_bwd(by_seq(p), sgr, o_raw, tot, by_seq(dycat[:, RWKV_W:]), sb_first, n_seq, seq_len)
    d_sb = [flat(dq), flat(dk_sb), flat(dv_sb)]
    late_grads = {"w_out": dw_out.reshape(N_DEV, -1, d_model), "w_up": _col_parts(dw_up),
                  "w_down": dw_down.reshape(N_DEV, -1, d_model)}
    (dr, dkraw, dv, dlw, dasig, d_kk, d_ka, d_rk, d_gw, d_gb), late_parts = rwkv_bwd(
        *rwkv_in, kkr, kar, rkr, gwr, gbr, s0_all, by_seq(dz), n_seq, seq_len,
        Hosted([late_grads[n] for n in LATE], False))
    prep_cots = [flat(t) for t in (dr, dkraw, dv, dlw, dasig)] + [dgate]
    dp, (d_mu, d_w0, d_wd, d_a0, d_wa, d_wg) = rwkv_prep_bwd(p, prep_edges, prep_pars, prep_cots, d_sb, seq_len)

    dw_in = matmul("dw_in", h1, dp, "tn", BF16, d_model, in_cols // 2, 2048)
    lora_parts = lambda t: _col_parts(t).astype(BF16)
    early_grads = {"w_in": _col_parts(dw_in), "w_decay_up": lora_parts(d_wd[:HEAD_DIM]),
                   "w_aaa_up": lora_parts(d_wa[HEAD_DIM:]), "w_gate_up": lora_parts(d_wg)}
    dx, d_ln1, *early_parts = matmul(
        "d_h1", dp, w_in_f, "nt", (F32,), 512, d_model, in_cols, extras=[x2d, dx2], pars=[g1],
        epilogue=lambda *t: norm_bwd(*t)[1:], sums=[((1, d_model), F32)],
        hosted=Hosted([early_grads[n] for n in EARLY], False))
    parts = dict(zip(EARLY, early_parts))
    parts.update(zip(LATE, late_parts))

    small_grads = {"ln1_g": d_ln1, "tok_mu": d_mu, "w0": d_w0, "a0": d_a0, "k_k": d_kk, "k_a": d_ka, "r_k": d_rk,
                   "gn_w": d_gw, "gn_b": d_gb, "sb_gain": d_sg, "ln2_g": d_ln2, "lnf_g": d_lnf}
    n_small = sum(int(weights[n].size) for n in SMALL)
    pack_rows = -(-(n_small + 1) // (8 * LANES)) * 8
    packed = _pack([small_grads[n] for n in SMALL] + [loss_acc[0, :1]], pack_rows)
    (small_parts,) = exchange("gather_small", [packed], True)

    results = {}
    for n in BIG:
        w2d = weights[n][0]
        tile = w2d.shape[0] if w2d.shape[0] <= 256 else 256
        results[n] = adamw("adamw_" + n, w2d, parts[n], mom_m[n][0], mom_v[n][0], tile)
    pk = lambda d: _pack([d[n] for n in SMALL] + [jnp.zeros((1,), F32)], pack_rows)
    sg, sd, sm, sv = adamw("adamw_small", pk(weights), small_parts, pk(mom_m), pk(mom_v), pack_rows)
    off = 0
    for n in SMALL:
        size = int(weights[n].size)
        results[n] = tuple(t.reshape(-1)[off:off + size] for t in (sg, sd, sm, sv))
        off += size
    loss = sg.reshape(-1)[off]

    out = [loss, dx.reshape(x.shape)]
    for kind in range(4):
        out += [results[n][kind].reshape(weights[n].shape) for n in ORDER]
    return tuple(out)
```

```python
import functools
import math

import jax
import jax.numpy as jnp
from jax import lax
from jax.experimental import pallas as pl
from jax.experimental.pallas import tpu as pltpu

F32 = jnp.float32
BF16 = jnp.bfloat16

N_DEV = 8
HEAD_DIM = 64
LANES = 128
RWKV_W = 512
SB_W = 512
LORA_WA = 128
GATE_LORA = 128
RWKV_COLS = 3 * RWKV_W + LORA_WA + GATE_LORA
RMS_EPS = 1e-5
GN_EPS = 64e-5
CHUNK = 64
QB = 256
SB_SCALE = HEAD_DIM ** -0.5
ADAM_LR, ADAM_B1, ADAM_B2, ADAM_EPS, ADAM_WD, ADAM_STEP = 0.001, 0.9, 0.999, 1e-08, 0.01, 10
VMEM_LIMIT = 56 * 1024 * 1024


_DIMS = {
    "nn": (((1,), (0,)), ((), ())),
    "nt": (((1,), (1,)), ((), ())),
    "tn": (((0,), (0,)), ((), ())),
}


def _pieces(x, n):
    if n == 1:
        return [x.astype(BF16)]
    out, rem = [], x.astype(F32)
    for i in range(n):
        p = rem.astype(BF16)
        out.append(p)
        if i + 1 < n:
            rem = rem - p.astype(F32)
    return out


def _dot(a, b, form, pa, pb):
    pieces_a, pieces_b = _pieces(a, pa), _pieces(b, pb)
    keep = max(pa, pb)
    acc = None
    for i, ai in enumerate(pieces_a):
        for j, bj in enumerate(pieces_b):
            if i + j >= keep:
                continue
            t = lax.dot_general(ai, bj, _DIMS[form], preferred_element_type=F32)
            acc = t if acc is None else acc + t
    return acc


BOTH = (True, True)


@functools.partial(jax.custom_vjp, nondiff_argnums=(2, 3, 4, 5))
def mm(a, b, form, pa, pb, diff=BOTH):
    return _dot(a, b, form, pa, pb)


def _mm_fwd(a, b, form, pa, pb, diff):
    return _dot(a, b, form, pa, pb), (a, b)


def _mm_bwd(form, pa, pb, diff, res, g):
    a, b = res
    pg = max(pa, pb)
    da, db = jnp.zeros_like(a), jnp.zeros_like(b)
    if form == "nn":
        if diff[0]:
            da = mm(g, b, "nt", pg, pb)
        if diff[1]:
            db = mm(a, g, "tn", pa, pg)
    elif form == "nt":
        if diff[0]:
            da = mm(g, b, "nn", pg, pb)
        if diff[1]:
            db = mm(g, a, "tn", pg, pa)
    else:
        if diff[0]:
            da = mm(b, g, "nt", pb, pg)
        if diff[1]:
            db = mm(a, g, "nn", pa, pg)
    return da, db


mm.defvjp(_mm_fwd, _mm_bwd)


def _stack_rows(top, bottom):
    return jnp.concatenate([top, bottom], axis=0)


@jax.custom_vjp
def _split_rows(x):
    half = x.shape[0] // 2
    return x[:half], x[half:]


def _split_rows_fwd(x):
    return _split_rows(x), None


def _split_rows_bwd(_, grads):
    return (_stack_rows(*grads),)


_split_rows.defvjp(_split_rows_fwd, _split_rows_bwd)


def _lane_lo(shape):
    return lax.broadcasted_iota(jnp.int32, shape, len(shape) - 1) < HEAD_DIM


def _segsum(x):
    lo = _lane_lo(x.shape)
    s_lo = jnp.sum(jnp.where(lo, x, 0.0), axis=-1, keepdims=True)
    s_hi = jnp.sum(jnp.where(lo, 0.0, x), axis=-1, keepdims=True)
    return jnp.where(lo, s_lo, s_hi)


def _sigmoid(x):
    return 0.5 * (jnp.tanh(0.5 * x) + 1.0)


def _softplus(x):
    return jnp.maximum(x, 0.0) + jnp.log(1.0 + jnp.exp(-jnp.abs(x)))


def f_norm(x, g):
    return x * lax.rsqrt(jnp.mean(x * x, axis=-1, keepdims=True) + RMS_EPS) * g


def f_prep(p, pprev, mu, w0, wd_pad, a0, wa_pad, wg):
    pm = p + mu * (pprev - p)
    r = pm[:, 0:RWKV_W]
    k = pm[:, RWKV_W:2 * RWKV_W]
    v = pm[:, 2 * RWKV_W:3 * RWKV_W]
    xwa = pm[:, 3 * RWKV_W:3 * RWKV_W + LORA_WA]
    xg = pm[:, 3 * RWKV_W + LORA_WA:RWKV_COLS]
    w = -_softplus(-(w0 + mm(jnp.tanh(xwa), wd_pad, "nn", 1, 1))) - 0.5
    lw = -jnp.exp(w)
    asig = _sigmoid(a0 + mm(xwa, wa_pad, "nn", 1, 1))
    gate = mm(_sigmoid(xg), wg, "nn", 1, 1)
    return r, k, v, lw, asig, gate


def _tri(n, kind):
    row = lax.broadcasted_iota(jnp.int32, (n, n), 0)
    col = lax.broadcasted_iota(jnp.int32, (n, n), 1)
    if kind == "lower_incl":
        return row >= col
    return row > col


def rwkv_chunk(state, r, kraw, v, lw, asig, k_k, k_a, r_k, gn_w, gn_b, hp):
    n = len(r)
    L = r[0].shape[0]
    lo = _lane_lo((1, LANES))
    masks = (lo, jnp.logical_not(lo))
    incl = _tri(L, "lower_incl")
    strict = _tri(L, "strict")
    tri = incl.astype(F32)
    eye = (lax.broadcasted_iota(jnp.int32, (L, L), 0) == lax.broadcasted_iota(jnp.int32, (L, L), 1)).astype(F32)
    kk = [x * w for x, w in zip(kraw, k_k)]
    kk = [x / jnp.maximum(jnp.sqrt(_segsum(x * x)), 1e-12) for x in kk]
    k = [x * (1.0 + (s - 1.0) * w) for x, s, w in zip(kraw, asig, k_a)]
    b = [x * s for x, s in zip(kk, asig)]
    c = [mm(tri, x, "nn", 1, 3, (False, True)) for x in lw]
    at = [-x * jnp.exp(ci - li) for x, ci, li in zip(kk, c, lw)]
    rt = [x * jnp.exp(ci) for x, ci in zip(r, c)]
    einv = [jnp.exp(-ci) for ci in c]
    bt = [x * e for x, e in zip(b, einv)]
    kt = [x * e for x, e in zip(k, einv)]
    inst = [(s, m) for s in range(n) for m in masks]
    ar_h = [_stack_rows(jnp.where(m, at[s], 0.0), jnp.where(m, rt[s], 0.0)) for s, m in inst]
    on_b = [_split_rows(mm(x, bt[s], "nt", hp, hp)) for x, (s, _) in zip(ar_h, inst)]
    on_k = [_split_rows(mm(x, kt[s], "nt", hp, hp)) for x, (s, _) in zip(ar_h, inst)]
    a_ab = [jnp.where(strict, x, 0.0) for x, _ in on_b]
    b_rb = [jnp.where(incl, x, 0.0) for _, x in on_b]
    a_ak = [jnp.where(strict, x, 0.0) for x, _ in on_k]
    b_rk = [jnp.where(incl, x, 0.0) for _, x in on_k]
    tinv = [eye + x for x in a_ab]
    pw = [mm(x, x, "nn", hp, hp) for x in a_ab]
    for _ in range(int(math.log2(L)) - 2):
        both = [_split_rows(mm(_stack_rows(t, x), x, "nn", hp, hp)) for t, x in zip(tinv, pw)]
        tinv = [t + tx for t, (tx, _) in zip(tinv, both)]
        pw = [xx for _, xx in both]
    tinv = [t + mm(t, x, "nn", hp, hp) for t, x in zip(tinv, pw)]
    on_state = [_split_rows(mm(x, state[s], "nt", hp, hp)) for x, (s, _) in zip(ar_h, inst)]
    on_v = [_split_rows(mm(_stack_rows(m1, m2), v[s], "nn", hp, hp)) for m1, m2, (s, _) in zip(a_ak, b_rk, inst)]
    u_h = [mm(t, sa + av, "nn", hp, hp) for t, (sa, _), (av, _) in zip(tinv, on_state, on_v)]
    y_h = [sr + mm(m1, u, "nn", hp, hp) + bv for (_, sr), m1, u, (_, bv) in zip(on_state, b_rb, u_h, on_v)]
    u_all = [jnp.where(lo, u_h[2 * s], u_h[2 * s + 1]) for s in range(n)]
    y_all = [jnp.where(lo, y_h[2 * s], y_h[2 * s + 1]) for s in range(n)]
    c_last = [jnp.sum(x, axis=0, keepdims=True) for x in lw]
    efwd = [jnp.exp(cl - ci) for cl, ci in zip(c_last, c)]
    new_state = [st * jnp.exp(cl) + mm(_stack_rows(u, vi), _stack_rows(bi * e, ki * e), "tn", hp, hp)
                 for st, cl, u, bi, e, vi, ki in zip(state, c_last, u_all, b, efwd, v, k)]
    row_head = lax.broadcasted_iota(jnp.int32, (LANES, LANES), 0) // HEAD_DIM
    col_head = lax.broadcasted_iota(jnp.int32, (LANES, LANES), 1) // HEAD_DIM
    new_state = [jnp.where(row_head == col_head, x, 0.0) for x in new_state]
    outs = []
    for y, ri, ki, vi, w_rk, w_gw, w_gb in zip(y_all, r, k, v, r_k, gn_w, gn_b):
        mean = _segsum(y) * (1.0 / HEAD_DIM)
        d = y - mean
        var = _segsum(d * d) * (1.0 / HEAD_DIM)
        yn = d * lax.rsqrt(var + GN_EPS) * w_gw + w_gb
        outs.append(yn + _segsum(ri * ki * w_rk) * vi)
    return outs, new_state


def sb_tile(q, k, v, c_lo, c_hi, diag, from_here=None):
    n = len(q)
    lo = _lane_lo((1, LANES))
    below = _tri(QB, "strict")
    from_s = _tri(QB, "lower_incl").astype(F32)
    inst = [(s, h) for s in range(n) for h in (0, 1)]
    carry = [(c_lo[s], c_hi[s])[h] for s, h in inst]
    z = [mm(q[s][h], k[s], "nt", 1, 1) for s, h in inst]
    soft = [jnp.log(1.0 + jnp.exp(-jnp.abs(x))) for x in z]
    log_keep = [-jnp.maximum(x, 0.0) - sp for x, sp in zip(z, soft)]
    if diag:
        log_keep = [jnp.where(below, x, 0.0) for x in log_keep]
    own = [jnp.sum(x, axis=1, keepdims=True) for x in log_keep]
    if from_here is not None:
        carry = [lax.stop_gradient(from_here[s][h] - o) + cr for (s, h), o, cr in zip(inst, own, carry)]
    tail = [mm(x, from_s, "nn", SB_SUM_PIECES, 1, (True, False)) for x in log_keep]
    log_a = [x + tl + cr for x, tl, cr in zip(z, tail, carry)]
    if diag:
        log_a = [jnp.where(below, x, -1e30) for x in log_a]
    att = [jnp.exp(x) for x in log_a]
    out_h = [mm(x, v[s], "nn", 1, 1) for x, (s, _) in zip(att, inst)]
    out = [jnp.where(lo, out_h[2 * s], out_h[2 * s + 1]) for s in range(n)]
    return out, [own[2 * s] for s in range(n)], [own[2 * s + 1] for s in range(n)]


def sb_split_q(q):
    lo = _lane_lo((1, LANES))
    qs = q * SB_SCALE
    return jnp.where(lo, qs, 0.0), jnp.where(lo, 0.0, qs)


def sb_post(o, gain):
    return o * lax.rsqrt(_segsum(o * o) * (1.0 / HEAD_DIM) + RMS_EPS) * gain


def f_final(x3, g, target):
    y = f_norm(x3, g)
    err = y - target
    return 0.5 * jnp.mean(err * err, axis=-1, keepdims=True)


def _params(sem):
    return pltpu.CompilerParams(dimension_semantics=sem, vmem_limit_bytes=VMEM_LIMIT)


def rw_call(name, body_fn, rows, pars, out_rows, tile, hosted=None):
    n_rows = rows[0].shape[0]
    tile = min(tile, n_rows)
    steps = n_rows // tile
    row_specs = [pl.BlockSpec((tile, arr.shape[1]), lambda i: (i, 0)) for arr in rows]
    par_specs = [pl.BlockSpec(p.shape, lambda i: (0, 0)) for p in pars]
    nr, npar, nor = len(rows), len(pars), len(out_rows)
    n_host = hosted.n if hosted is not None else 0

    def body(*refs):
        if hosted is not None:
            refs, copies = hosted.split(refs, nr + npar, nor)
            hosted.run(copies, pl.program_id(0) == 0, pl.program_id(0) == steps - 1)
        row_outs = body_fn([r[...] for r in refs[:nr]], [r[...] for r in refs[nr:nr + npar]])
        for ref, val in zip(refs[nr + npar:], row_outs):
            ref[...] = val.astype(ref.dtype)

    outs = pl.pallas_call(
        body, name=name, grid=(steps,), in_specs=row_specs + par_specs + [ANY_SPEC] * n_host,
        out_specs=[pl.BlockSpec((tile, c), lambda i: (i, 0)) for c, _ in out_rows] + [ANY_SPEC] * n_host,
        out_shape=[jax.ShapeDtypeStruct((n_rows, c), dt) for c, dt in out_rows]
        + (exchange_shapes(hosted.arrays, hosted.gather) if n_host else []),
        scratch_shapes=exchange_sems(n_host) if n_host else [],
        compiler_params=_params(("arbitrary",)),
    )(*rows, *pars, *(hosted.arrays if n_host else []))
    return outs


def matmul(name, a, b, form, out_dtype, tm, tn, tk, extras=(), pars=(), epilogue=None, sums=(), hosted=None):
    out_dtypes = out_dtype if isinstance(out_dtype, tuple) else (out_dtype,)
    tm, tn, tk = min(tm, a.shape[1 if form == "tn" else 0]), min(tn, b.shape[0 if form == "nt" else 1]), min(tk, a.shape[0 if form == "tn" else 1])
    if form == "nn":
        (m, kd), n = a.shape, b.shape[1]
        a_spec = pl.BlockSpec((tm, tk), lambda i, j, k: (i, k))
        b_spec = pl.BlockSpec((tk, tn), lambda i, j, k: (k, j))
    elif form == "nt":
        (m, kd), n = a.shape, b.shape[0]
        a_spec = pl.BlockSpec((tm, tk), lambda i, j, k: (i, k))
        b_spec = pl.BlockSpec((tn, tk), lambda i, j, k: (j, k))
    else:
        (kd, m), n = a.shape, b.shape[1]
        a_spec = pl.BlockSpec((tk, tm), lambda i, j, k: (k, i))
        b_spec = pl.BlockSpec((tk, tn), lambda i, j, k: (k, j))
    ksteps = kd // tk

    n_extra, n_par, n_out, n_sum = len(extras), len(pars), len(out_dtypes), len(sums)
    n_in = 2 + n_extra + n_par
    grid = (m // tm, n // tn, ksteps)

    def body(*refs):
        if hosted is not None:
            refs, copies = hosted.split(refs, n_in, n_out + n_sum)
            here = [pl.program_id(d) for d in range(3)]
            hosted.run(copies, functools.reduce(jnp.logical_and, [h == 0 for h in here]),
                       functools.reduce(jnp.logical_and, [h == g - 1 for h, g in zip(here, grid)]))
        a_ref, b_ref, rest = refs[0], refs[1], refs[2:]
        e_refs, o_refs = rest[:n_extra + n_par], rest[n_extra + n_par:n_extra + n_par + n_out]
        s_refs = rest[n_extra + n_par + n_out:n_extra + n_par + n_out + n_sum]
        kstep = pl.program_id(2)
        part = lax.dot_general(a_ref[...].astype(BF16), b_ref[...].astype(BF16), _DIMS[form],
                               preferred_element_type=F32)

        def finish(acc):
            outs = epilogue(acc, *[r[...] for r in e_refs]) if epilogue else (acc,)
            for ref, val in zip(o_refs, outs[:n_out]):
                ref[...] = val.astype(ref.dtype)
            if n_sum:
                first_tile = jnp.logical_and(pl.program_id(0) == 0, pl.program_id(1) == 0)

                @pl.when(first_tile)
                def _():
                    for ref, val in zip(s_refs, outs[n_out:]):
                        ref[...] = val.astype(ref.dtype)

                @pl.when(jnp.logical_not(first_tile))
                def _():
                    for ref, val in zip(s_refs, outs[n_out:]):
                        ref[...] = ref[...] + val.astype(ref.dtype)

        if ksteps == 1:
            finish(part)
            return
        acc_ref = rest[n_extra + n_par + n_out + n_sum]

        @pl.when(kstep == 0)
        def _():
            acc_ref[...] = part

        @pl.when(jnp.logical_and(kstep > 0, kstep < ksteps - 1))
        def _():
            acc_ref[...] = acc_ref[...] + part

        @pl.when(kstep == ksteps - 1)
        def _():
            finish(acc_ref[...] + part)

    out_spec = pl.BlockSpec((tm, tn), lambda i, j, k: (i, j))
    whole = lambda shape: pl.BlockSpec(shape, lambda i, j, k: (0,) * len(shape))
    n_host = hosted.n if hosted is not None else 0
    host_in = list(hosted.arrays) if hosted is not None else []
    host_out = exchange_shapes(hosted.arrays, hosted.gather) if hosted is not None else []
    outs = pl.pallas_call(
        body, name=name, grid=grid,
        in_specs=[a_spec, b_spec] + [out_spec] * n_extra + [whole(t.shape) for t in pars] + [ANY_SPEC] * n_host,
        out_specs=[out_spec] * n_out + [whole(s) for s, _ in sums] + [ANY_SPEC] * n_host,
        out_shape=[jax.ShapeDtypeStruct((m, n), dt) for dt in out_dtypes]
        + [jax.ShapeDtypeStruct(s, dt) for s, dt in sums] + host_out,
        scratch_shapes=([pltpu.VMEM((tm, tn), F32)] if ksteps > 1 else []) + (exchange_sems(n_host) if n_host else []),
        compiler_params=_params(("arbitrary",) * 3 if (sums or n_host) else ("parallel", "parallel", "arbitrary")),
    )(a, b, *extras, *pars, *host_in)
    return outs if (isinstance(out_dtype, tuple) or sums or n_host) else outs[0]


PREP_TILE = 256
PREP_TILE_BWD = 128
SUBLANES = 8


def _shift_in(rows, first):
    rolled = pltpu.roll(rows, 1, 0)
    row = lax.broadcasted_iota(jnp.int32, (SUBLANES, rows.shape[1]), 0)
    head = jnp.where(row == 0, first, rolled[0:SUBLANES])
    return jnp.concatenate([head, rolled[SUBLANES:]], axis=0), rolled


def rwkv_prep_fwd(p, pars, seq_len):
    n_tok = p.shape[0]
    tile = min(PREP_TILE, seq_len)
    tile_b = min(PREP_TILE_BWD, tile)
    steps, per_seq, sub = n_tok // tile, seq_len // tile, tile // tile_b
    n_par = len(pars)

    def body(p_ref, *rest):
        par_refs, out_refs, edge_ref, last8 = rest[:n_par], rest[n_par:n_par + 6], rest[n_par + 6], rest[n_par + 7]
        step = pl.program_id(0)

        @pl.when(step == 0)
        def _():
            last8[...] = jnp.zeros_like(last8)

        rows = p_ref[...]
        before = jnp.where(step % per_seq == 0, 0.0, pltpu.roll(last8[...], 1, 0))
        prev, rolled = _shift_in(rows, before)
        edge_ref[0] = prev[0:SUBLANES]
        for m in range(1, sub):
            edge_ref[m] = rolled[m * tile_b:m * tile_b + SUBLANES]
        last8[...] = rows[tile - SUBLANES:tile]
        for ref, val in zip(out_refs, f_prep(rows, prev, *[r[...] for r in par_refs])):
            ref[...] = val

    row_out = pl.BlockSpec((tile, RWKV_W), lambda i: (i, 0))
    outs = pl.pallas_call(
        body, name="rwkv_prep", grid=(steps,),
        in_specs=[pl.BlockSpec((tile, RWKV_COLS), lambda i: (i, 0))] + [pl.BlockSpec(t.shape, lambda i: (0, 0)) for t in pars],
        out_specs=[row_out] * 6 + [pl.BlockSpec((sub, SUBLANES, RWKV_COLS), lambda i: (i, 0, 0))],
        out_shape=[jax.ShapeDtypeStruct((n_tok, RWKV_W), F32)] * 6
        + [jax.ShapeDtypeStruct((steps * sub, SUBLANES, RWKV_COLS), F32)],
        scratch_shapes=[pltpu.VMEM((SUBLANES, RWKV_COLS), F32)],
        compiler_params=_params(("arbitrary",)),
    )(p, *pars)
    return outs[:6], outs[6]


def rwkv_prep_bwd(p, edges, pars, cots, d_sb, seq_len):
    n_tok = p.shape[0]
    tile = min(PREP_TILE_BWD, seq_len)
    steps, per_seq = n_tok // tile, seq_len // tile
    n_par = len(pars)
    back = lambda i: steps - 1 - i

    def body(p_ref, edge_ref, *rest):
        par_refs, rest = rest[:n_par], rest[n_par:]
        cot_refs, dy_ref, z_ref, sb_refs = rest[:5], rest[5], rest[6], rest[7:10]
        dp_ref, acc_refs, next8 = rest[10], rest[11:11 + n_par], rest[11 + n_par]
        step = pl.program_id(0)
        first = step == 0

        @pl.when(first)
        def _():
            next8[...] = jnp.zeros_like(next8)

        rows = p_ref[...]
        prev, _ = _shift_in(rows, edge_ref[0])
        _, vjp = jax.vjp(f_prep, rows, prev, *[r[...].astype(F32) for r in par_refs])
        grads = vjp(tuple(r[...] for r in cot_refs) + (dy_ref[...] * z_ref[...],))
        d_rows, d_prev = grads[0], grads[1]
        up = pltpu.roll(d_prev, tile - 1, 0)
        ends_seq = back(step) % per_seq == per_seq - 1
        after = jnp.where(ends_seq, 0.0, pltpu.roll(next8[...], SUBLANES - 1, 0))
        row = lax.broadcasted_iota(jnp.int32, (SUBLANES, RWKV_COLS), 0)
        tail = jnp.where(row == SUBLANES - 1, after, up[tile - SUBLANES:tile])
        d_rows = d_rows + jnp.concatenate([up[:tile - SUBLANES], tail], axis=0)
        next8[...] = d_prev[0:SUBLANES]
        dp_ref[...] = jnp.concatenate([d_rows] + [r[...] for r in sb_refs], axis=1).astype(dp_ref.dtype)

        @pl.when(first)
        def _():
            for ref, val in zip(acc_refs, grads[2:]):
                ref[...] = val

        @pl.when(jnp.logical_not(first))
        def _():
            for ref, val in zip(acc_refs, grads[2:]):
                ref[...] = ref[...] + val

    cols = RWKV_COLS + sum(t.shape[1] for t in d_sb)
    half = pl.BlockSpec((tile, RWKV_W), lambda i: (back(i), 0))
    par_specs = [pl.BlockSpec(t.shape, lambda i: (0, 0)) for t in pars]
    outs = pl.pallas_call(
        body, name="d_rwkv_prep", grid=(steps,),
        in_specs=[pl.BlockSpec((tile, RWKV_COLS), lambda i: (back(i), 0)),
                  pl.BlockSpec((1, SUBLANES, RWKV_COLS), lambda i: (back(i), 0, 0))] + par_specs + [half] * 10,
        out_specs=[pl.BlockSpec((tile, cols), lambda i: (back(i), 0))] + par_specs,
        out_shape=[jax.ShapeDtypeStruct((n_tok, cols), BF16)] + [jax.ShapeDtypeStruct(t.shape, F32) for t in pars],
        scratch_shapes=[pltpu.VMEM((SUBLANES, RWKV_COLS), F32)],
        compiler_params=_params(("arbitrary",)),
    )(p, edges, *pars, *cots, *d_sb)
    return outs[0], outs[1:]


RWKV_HP = 1


RWKV_PAIRS = 4


def _rwkv_specs(n_seq, chunk_of):
    width = RWKV_PAIRS * LANES
    row = pl.BlockSpec((n_seq, CHUNK, width), lambda g, c: (0, chunk_of(c), g))
    par = pl.BlockSpec((1, width), lambda g, c: (0, g))
    s0 = pl.BlockSpec((1, 1, RWKV_PAIRS * n_seq, LANES, LANES), lambda g, c: (g, chunk_of(c), 0, 0, 0))
    return row, par, s0


class Hosted:
    def __init__(self, arrays, gather):
        self.arrays, self.gather, self.n = list(arrays), gather, len(arrays)

    def split(self, refs, n_in, n_out):
        n = self.n
        ins, outs, sems = refs[n_in:n_in + n], refs[n_in + n + n_out:n_in + 2 * n + n_out], refs[-3:]
        own = refs[:n_in] + refs[n_in + n:n_in + n + n_out] + refs[n_in + 2 * n + n_out:-3]
        return own, exchange_copies(ins, outs, *sems, self.gather)

    def run(self, copies, first, last):
        @pl.when(first)
        def _():
            for cp in copies:
                cp.start()

        @pl.when(last)
        def _():
            for cp in copies:
                cp.wait()


def rwkv_fwd(r, kraw, v, lw, asig, k_k, k_a, r_k, gn_w, gn_b, n_seq, seq_len, hosted):
    n_chunks = seq_len // CHUNK
    n_groups = RWKV_W // (RWKV_PAIRS * LANES)
    n_inst = RWKV_PAIRS * n_seq
    row, par, s0_spec = _rwkv_specs(n_seq, lambda c: c)
    inst = [(s, pl.ds(pp * LANES, LANES)) for pp in range(RWKV_PAIRS) for s in range(n_seq)]

    def body(*refs):
        own, copies = hosted.split(refs, 10, 2)
        row_refs, par_refs, (z_ref, s0_ref, state) = own[:5], own[5:10], own[10:]
        step = pl.program_id(0) * n_chunks + pl.program_id(1)
        hosted.run(copies, step == 0, step == n_groups * n_chunks - 1)

        @pl.when(pl.program_id(1) == 0)
        def _():
            state[...] = jnp.zeros_like(state)

        s0 = [state[i] for i in range(n_inst)]
        rows = [[ref[s, :, lanes] for s, lanes in inst] for ref in row_refs]
        pars = [[ref[:, lanes] for _, lanes in inst] for ref in par_refs]
        z, s1 = rwkv_chunk(s0, *rows, *pars, RWKV_HP)
        for i, (s, lanes) in enumerate(inst):
            s0_ref[0, 0, i] = s0[i]
            z_ref[s, :, lanes] = z[i]
            state[i] = s1[i]

    outs = pl.pallas_call(
        body, name="rwkv_fwd", grid=(n_groups, n_chunks),
        in_specs=[row] * 5 + [par] * 5 + [ANY_SPEC] * hosted.n, out_specs=[row, s0_spec] + [ANY_SPEC] * hosted.n,
        out_shape=[jax.ShapeDtypeStruct(r.shape, F32),
                   jax.ShapeDtypeStruct((n_groups, n_chunks, n_inst, LANES, LANES), F32)]
        + exchange_shapes(hosted.arrays, hosted.gather),
        scratch_shapes=[pltpu.VMEM((n_inst, LANES, LANES), F32)] + exchange_sems(hosted.n),
        compiler_params=_params(("arbitrary", "arbitrary")),
    )(r, kraw, v, lw, asig, k_k, k_a, r_k, gn_w, gn_b, *hosted.arrays)
    return outs[0], outs[1], outs[2:]


def rwkv_bwd(r, kraw, v, lw, asig, k_k, k_a, r_k, gn_w, gn_b, s0_all, dy, gate, n_seq, seq_len, hosted):
    n_chunks = seq_len // CHUNK
    n_groups = RWKV_W // (RWKV_PAIRS * LANES)
    n_inst = RWKV_PAIRS * n_seq
    row, par, s0_spec = _rwkv_specs(n_seq, lambda c: n_chunks - 1 - c)
    inst = [(s, pl.ds(pp * LANES, LANES)) for pp in range(RWKV_PAIRS) for s in range(n_seq)]

    def body(*refs):
        own, copies = hosted.split(refs, 13, 10)
        row_refs, par_refs, s0_ref, dy_ref, gate_ref = own[:5], own[5:10], own[10], own[11], own[12]
        drow_refs, dpar_refs, dstate = own[13:18], own[18:23], own[23]
        step = pl.program_id(0) * n_chunks + pl.program_id(1)
        hosted.run(copies, step == 0, step == n_groups * n_chunks - 1)
        first = pl.program_id(1) == 0

        @pl.when(first)
        def _():
            dstate[...] = jnp.zeros_like(dstate)

        fn = functools.partial(rwkv_chunk, hp=RWKV_HP)
        rows = [[ref[s, :, lanes] for s, lanes in inst] for ref in row_refs]
        pars = [[ref[:, lanes] for _, lanes in inst] for ref in par_refs]
        _, vjp = jax.vjp(fn, [s0_ref[0, 0, i] for i in range(n_inst)], *rows, *pars)
        dz = [dy_ref[s, :, lanes] * gate_ref[s, :, lanes] for s, lanes in inst]
        grads = vjp((dz, [dstate[i] for i in range(n_inst)]))
        for i, (s, lanes) in enumerate(inst):
            dstate[i] = grads[0][i]
            for ref, val in zip(drow_refs, grads[1:6]):
                ref[s, :, lanes] = val[i]

        def accumulate(start):
            for ref, val in zip(dpar_refs, grads[6:]):
                for pp in range(RWKV_PAIRS):
                    lanes = pl.ds(pp * LANES, LANES)
                    total = functools.reduce(jnp.add, val[pp * n_seq:(pp + 1) * n_seq])
                    ref[:, lanes] = total if start else ref[:, lanes] + total

        @pl.when(first)
        def _():
            accumulate(True)

        @pl.when(jnp.logical_not(first))
        def _():
            accumulate(False)

    rows_shape = jax.ShapeDtypeStruct(r.shape, F32)
    par_shape = jax.ShapeDtypeStruct((1, RWKV_W), F32)
    outs = pl.pallas_call(
        body, name="rwkv_bwd", grid=(n_groups, n_chunks),
        in_specs=[row] * 5 + [par] * 5 + [s0_spec, row, row] + [ANY_SPEC] * hosted.n,
        out_specs=[row] * 5 + [par] * 5 + [ANY_SPEC] * hosted.n,
        out_shape=[rows_shape] * 5 + [par_shape] * 5 + exchange_shapes(hosted.arrays, hosted.gather),
        scratch_shapes=[pltpu.VMEM((n_inst, LANES, LANES), F32)] + exchange_sems(hosted.n),
        compiler_params=_params(("arbitrary", "arbitrary")),
    )(r, kraw, v, lw, asig, k_k, k_a, r_k, gn_w, gn_b, s0_all, dy, gate, *hosted.arrays)
    return outs[:10], outs[10:]


SB_Q0 = RWKV_COLS // LANES
SB_K0 = SB_Q0 + SB_W // LANES
SB_V0 = SB_K0 + SB_W // LANES
SB_SEQS = 2
SB_BUFFERS = pl.Buffered(1)
SB_SUM_PIECES = 2
SB_DEAD = -110.0


def _col_of(c_lo, c_hi):
    return jnp.where(_lane_lo((1, LANES)), c_lo, c_hi)


def sb_fwd(p, gain, n_seq, seq_len, hosted):
    n_pairs = SB_W // LANES
    n_q = seq_len // QB
    nb = min(SB_SEQS, n_seq)

    def seq_spec(c0):
        return pl.BlockSpec((nb, seq_len, LANES), functools.partial(lambda b, h, c0: (b, 0, c0 + h), c0=c0),
                            pipeline_mode=SB_BUFFERS)

    out_spec = pl.BlockSpec((nb, seq_len, LANES), lambda b, h: (b, 0, h), pipeline_mode=SB_BUFFERS)

    def body(*refs):
        own, copies = hosted.split(refs, 4, 4)
        q_ref, k_ref, v_ref, g_ref, y_ref, o_ref, tot_ref, first_ref = own
        step = pl.program_id(0) * n_pairs + pl.program_id(1)
        hosted.run(copies, step == 0, step == (n_seq // nb) * n_pairs - 1)
        gain = g_ref[...]

        def q_block(i, _):
            qs = pl.multiple_of(i * QB, QB)
            seqs = range(nb)
            zeros = [jnp.zeros((QB, 1), F32)] * nb
            qv = [sb_split_q(q_ref[s, pl.ds(qs, QB), :]) for s in seqs]
            add = lambda xs, ys: [x + y for x, y in zip(xs, ys)]

            def tiles(ks, c_lo, c_hi, diag):
                return sb_tile(qv, [k_ref[s, pl.ds(ks, QB), :] for s in seqs],
                               [v_ref[s, pl.ds(ks, QB), :] for s in seqs], c_lo, c_hi, diag)

            def alive(c_lo, c_hi):
                top = functools.reduce(jnp.maximum, list(c_lo) + list(c_hi))
                return jnp.max(top) > SB_DEAD

            def k_block(state):
                j, _, (o, c_lo, c_hi) = state
                o2, s_lo, s_hi = tiles(pl.multiple_of(j * QB, QB), c_lo, c_hi, False)
                c_lo, c_hi = add(c_lo, s_lo), add(c_hi, s_hi)
                return j - 1, alive(c_lo, c_hi), (add(o, o2), c_lo, c_hi)

            o, c_lo, c_hi = tiles(qs, zeros, zeros, True)
            j, _, (o, c_lo, c_hi) = lax.while_loop(lambda st: jnp.logical_and(st[0] >= 0, st[1]), k_block,
                                                   (i - 1, alive(c_lo, c_hi), (o, c_lo, c_hi)))
            first_ref[pl.program_id(0), pl.program_id(1), i] = j + 1
            for s in seqs:
                o_ref[s, pl.ds(qs, QB), :] = o[s]
                tot_ref[s, pl.ds(qs, QB), :] = jnp.broadcast_to(_col_of(c_lo[s], c_hi[s]), (QB, LANES))
                y_ref[s, pl.ds(qs, QB), :] = sb_post(o[s], gain)
            return 0

        lax.fori_loop(0, n_q, q_block, 0)

    shape = jax.ShapeDtypeStruct((n_seq, seq_len, SB_W), F32)
    return pl.pallas_call(
        body, name="sb_fwd", grid=(n_seq // nb, n_pairs),
        in_specs=[seq_spec(SB_Q0), seq_spec(SB_K0), seq_spec(SB_V0), pl.BlockSpec((1, LANES), lambda b, h: (0, h))]
        + [ANY_SPEC] * hosted.n,
        out_specs=[out_spec] * 3 + [pl.BlockSpec(memory_space=pltpu.SMEM)] + [ANY_SPEC] * hosted.n,
        out_shape=[shape] * 3 + [jax.ShapeDtypeStruct((n_seq // nb, n_pairs, n_q), jnp.int32)]
        + exchange_shapes(hosted.arrays, hosted.gather),
        scratch_shapes=exchange_sems(hosted.n),
        compiler_params=_params(("arbitrary", "arbitrary")),
    )(p, p, p, gain, *hosted.arrays)


def sb_bwd(p, gain, o_raw, tot, dy, first, n_seq, seq_len):
    n_pairs = SB_W // LANES
    n_q = seq_len // QB
    nb = min(SB_SEQS, n_seq)

    def seq_spec(c0):
        return pl.BlockSpec((nb, seq_len, LANES), functools.partial(lambda h, b, c0: (b, 0, c0 + h), c0=c0),
                            pipeline_mode=SB_BUFFERS)

    own = pl.BlockSpec((nb, seq_len, LANES), lambda h, b: (b, 0, h), pipeline_mode=SB_BUFFERS)
    par = pl.BlockSpec((1, LANES), lambda h, b: (0, h))

    def body(q_ref, k_ref, v_ref, g_ref, o_ref, tot_ref, dy_ref, first_ref, dq_ref, dk_ref, dv_ref, dg_ref):
        gain = g_ref[...]
        lo = _lane_lo((1, LANES))
        dk_ref[...] = jnp.zeros_like(dk_ref)
        dv_ref[...] = jnp.zeros_like(dv_ref)

        def q_block(i, dgain):
            qs = pl.multiple_of(i * QB, QB)
            seqs = range(nb)
            zeros = [jnp.zeros((QB, 1), F32)] * nb
            qv, dov, t_lo, t_hi = [], [], [], []
            for s in seqs:
                qv.append(sb_split_q(q_ref[s, pl.ds(qs, QB), :]))
                _, post_vjp = jax.vjp(sb_post, o_ref[s, pl.ds(qs, QB), :], gain)
                do, dg_s = post_vjp(dy_ref[s, pl.ds(qs, QB), :])
                dov.append(do)
                dgain = dgain + dg_s
                tot_s = tot_ref[s, pl.ds(qs, QB), :]
                t_lo.append(jnp.max(jnp.where(lo, tot_s, -jnp.inf), axis=1, keepdims=True))
                t_hi.append(jnp.max(jnp.where(lo, -jnp.inf, tot_s), axis=1, keepdims=True))
            add = lambda xs, ys: [x + y for x, y in zip(xs, ys)]
            sub = lambda xs, ys: [x - y for x, y in zip(xs, ys)]

            def tile(ks, carry, diag):
                dq, rem_lo, rem_hi, g_lo, g_hi = carry
                kv = [k_ref[s, pl.ds(ks, QB), :] for s in seqs]
                vv = [v_ref[s, pl.ds(ks, QB), :] for s in seqs]
                fn = functools.partial(sb_tile, diag=diag, from_here=list(zip(rem_lo, rem_hi)))
                (_, s_lo, s_hi), vjp = jax.vjp(fn, qv, kv, vv, zeros, zeros)
                dq_t, dk_t, dv_t, dc_lo, dc_hi = vjp((dov, g_lo, g_hi))
                dq_t = [jnp.where(lo, d_lo, d_hi) for d_lo, d_hi in dq_t]
                for s in seqs:
                    dk_ref[s, pl.ds(ks, QB), :] = dk_ref[s, pl.ds(ks, QB), :] + dk_t[s]
                    dv_ref[s, pl.ds(ks, QB), :] = dv_ref[s, pl.ds(ks, QB), :] + dv_t[s]
                return add(dq, dq_t), sub(rem_lo, s_lo), sub(rem_hi, s_hi), add(g_lo, dc_lo), add(g_hi, dc_hi)

            def k_block(j, carry):
                return tile(pl.multiple_of(j * QB, QB), carry, False)

            carry = ([jnp.zeros((QB, LANES), F32)] * nb, t_lo, t_hi, zeros, zeros)
            carry = lax.fori_loop(first_ref[pl.program_id(1), pl.program_id(0), i], i, k_block, carry)
            carry = tile(qs, carry, True)
            for s in seqs:
                dq_ref[s, pl.ds(qs, QB), :] = carry[0][s] * SB_SCALE
            return dgain

        dgain = lax.fori_loop(0, n_q, q_block, jnp.zeros((1, LANES), F32))
        first = pl.program_id(1) == 0

        @pl.when(first)
        def _():
            dg_ref[...] = dgain

        @pl.when(jnp.logical_not(first))
        def _():
            dg_ref[...] = dg_ref[...] + dgain

    shape = jax.ShapeDtypeStruct((n_seq, seq_len, SB_W), F32)
    return pl.pallas_call(
        body, name="sb_bwd", grid=(n_pairs, n_seq // nb),
        in_specs=[seq_spec(SB_Q0), seq_spec(SB_K0), seq_spec(SB_V0), par, own, own, seq_spec(RWKV_W // LANES),
                  pl.BlockSpec(memory_space=pltpu.SMEM)],
        out_specs=[own, own, own, par],
        out_shape=[shape, shape, shape, jax.ShapeDtypeStruct((1, SB_W), F32)],
        compiler_params=_params(("arbitrary", "arbitrary")),
    )(p, p, p, gain, o_raw, tot, dy, first)


def exchange(name, arrays, gather):
    n = len(arrays)

    def body(*refs):
        copies = exchange_copies(refs[:n], refs[n:2 * n], *refs[2 * n:], gather)
        for cp in copies:
            cp.start()
        for cp in copies:
            cp.wait()

    return pl.pallas_call(
        body, name=name, in_specs=[ANY_SPEC] * n, out_specs=[ANY_SPEC] * n, out_shape=exchange_shapes(arrays, gather),
        scratch_shapes=exchange_sems(n),
    )(*arrays)


ANY_SPEC = pl.BlockSpec(memory_space=pl.ANY)


def exchange_shapes(arrays, gather):
    return [jax.ShapeDtypeStruct(((N_DEV,) + a.shape) if gather else a.shape, a.dtype) for a in arrays]


def exchange_sems(n):
    return [pltpu.SemaphoreType.DMA((n, N_DEV - 1)), pltpu.SemaphoreType.DMA((n, N_DEV - 1)),
            pltpu.SemaphoreType.DMA((n,))]


def exchange_copies(ins, outs, send_sems, recv_sems, local_sems, gather):
    x, y, c = lax.axis_index("x"), lax.axis_index("y"), lax.axis_index("c")
    me = 4 * x + 2 * y + c
    copies = []
    for a, (src_all, dst_all) in enumerate(zip(ins, outs)):
        own = src_all if gather else src_all.at[me]
        copies.append(pltpu.make_async_copy(own, dst_all.at[me], local_sems.at[a]))
        for j in range(1, N_DEV):
            px, py, pc = (x + (j >> 2)) % 2, (y + ((j >> 1) & 1)) % 2, (c + (j & 1)) % 2
            src = src_all if gather else src_all.at[4 * px + 2 * py + pc]
            copies.append(pltpu.make_async_remote_copy(
                src_ref=src, dst_ref=dst_all.at[me], send_sem=send_sems.at[a, j - 1],
                recv_sem=recv_sems.at[a, j - 1], device_id=(px, py, pc), device_id_type=pl.DeviceIdType.MESH))
    return copies


def adamw(name, w, parts, m, v, tile):
    rows, cols = w.shape
    spec = pl.BlockSpec((tile, cols), lambda i: (i, 0))
    part_spec = pl.BlockSpec((N_DEV, tile, cols), lambda i: (0, i, 0))

    def body(w_ref, p_ref, m_ref, v_ref, g_ref, d_ref, nm_ref, nv_ref):
        g = p_ref[0].astype(F32)
        for s in range(1, N_DEV):
            g = g + p_ref[s].astype(F32)
        new_m = ADAM_B1 * m_ref[...] + (1.0 - ADAM_B1) * g
        new_v = ADAM_B2 * v_ref[...] + (1.0 - ADAM_B2) * (g * g)
        m_hat = new_m / (1.0 - ADAM_B1 ** ADAM_STEP)
        v_hat = new_v / (1.0 - ADAM_B2 ** ADAM_STEP)
        g_ref[...] = g
        d_ref[...] = -ADAM_LR * (m_hat / (jnp.sqrt(v_hat) + ADAM_EPS) + ADAM_WD * w_ref[...])
        nm_ref[...] = new_m
        nv_ref[...] = new_v

    shape = jax.ShapeDtypeStruct((rows, cols), F32)
    return pl.pallas_call(
        body, name=name, grid=(rows // tile,), in_specs=[spec, part_spec, spec, spec],
        out_specs=[spec] * 4, out_shape=[shape] * 4, compiler_params=_params(("arbitrary",)),
    )(w, parts, m, v)


SMALL = ("ln1_g", "tok_mu", "w0", "a0", "k_k", "k_a", "r_k", "gn_w", "gn_b", "sb_gain", "ln2_g", "lnf_g")
EARLY = ("w_in", "w_decay_up", "w_aaa_up", "w_gate_up")
LATE = ("w_out", "w_up", "w_down")
BIG = EARLY + LATE
COL_SHARDED = ("w_in", "w_decay_up", "w_aaa_up", "w_gate_up", "w_up")
ORDER = ("ln1_g", "w_in", "tok_mu", "w0", "w_decay_up", "a0", "w_aaa_up", "w_gate_up", "k_k", "k_a", "r_k",
         "gn_w", "gn_b", "sb_gain", "w_out", "ln2_g", "w_up", "w_down", "lnf_g")


def _pack(vectors, rows):
    flat = jnp.concatenate([v.reshape(-1).astype(F32) for v in vectors])
    return jnp.pad(flat, (0, rows * LANES - flat.shape[0])).reshape(rows, LANES)


def _full_cols(gathered):
    d, k, cols = gathered.shape
    return jnp.transpose(gathered, (1, 0, 2)).reshape(k, d * cols)


def _col_parts(full):
    k, n = full.shape
    return jnp.transpose(full.reshape(k, N_DEV, n // N_DEV), (1, 0, 2))


def kernel(x, ln1_g, w_in, tok_mu, w0, w_decay_up, a0, w_aaa_up, w_gate_up, k_k, k_a, r_k, gn_w, gn_b, sb_gain, w_out, ln2_g, w_up, w_down, lnf_g, loss_target, m_ln1_g, m_w_in, m_tok_mu, m_w0, m_w_decay_up, m_a0, m_w_aaa_up, m_w_gate_up, m_k_k, m_k_a, m_r_k, m_gn_w, m_gn_b, m_sb_gain, m_w_out, m_ln2_g, m_w_up, m_w_down, m_lnf_g, v_ln1_g, v_w_in, v_tok_mu, v_w0, v_w_decay_up, v_a0, v_w_aaa_up, v_w_gate_up, v_k_k, v_k_a, v_r_k, v_gn_w, v_gn_b, v_sb_gain, v_w_out, v_ln2_g, v_w_up, v_w_down, v_lnf_g):
    args = dict(locals())
    weights = {n: args[n] for n in ORDER}
    mom_m = {n: args["m_" + n] for n in ORDER}
    mom_v = {n: args["v_" + n] for n in ORDER}

    n_seq, seq_len, d_model = x.shape
    n_tok = n_seq * seq_len
    x2d = x.reshape(n_tok, d_model)
    tgt = loss_target.reshape(n_tok, d_model)
    row = lambda t: t.reshape(1, -1).astype(F32)

    g1 = row(ln1_g)
    shard = {n: weights[n][0].astype(BF16) for n in BIG}
    h1, *early = rw_call("norm1", lambda r, q: [f_norm(r[0], q[0])], [x2d], [g1], [(d_model, BF16)], 512,
                         hosted=Hosted([shard[n] for n in EARLY], True))
    gathered = dict(zip(EARLY, early))
    w_in_f = _full_cols(gathered["w_in"])
    zeros64 = jnp.zeros((HEAD_DIM, RWKV_W), BF16)
    wd_pad = jnp.concatenate([_full_cols(gathered["w_decay_up"]), zeros64], axis=0)
    wa_pad = jnp.concatenate([zeros64, _full_cols(gathered["w_aaa_up"])], axis=0)
    wg_f = _full_cols(gathered["w_gate_up"])
    in_cols = w_in_f.shape[1]

    mu, w0r, a0r = row(tok_mu), row(w0), row(a0)
    kkr, kar, rkr, gwr, gbr, sgr = row(k_k), row(k_a), row(r_k), row(gn_w), row(gn_b), row(sb_gain)
    g2, gf = row(ln2_g), row(lnf_g)

    p = matmul("proj_in", h1, w_in_f, "nn", F32, 1024, in_cols // 2, d_model)
    prep_pars = [mu, w0r, wd_pad, a0r, wa_pad, wg_f]
    (r_, kraw, v_, lw, asig, gate), prep_edges = rwkv_prep_fwd(p, prep_pars, seq_len)
    by_seq = lambda t: t.reshape(n_seq, seq_len, t.shape[-1])
    flat = lambda t: t.reshape(n_tok, t.shape[-1])
    rwkv_in = [by_seq(t) for t in (r_, kraw, v_, lw, asig)]
    z, s0_all, (w_out_g, w_up_g) = rwkv_fwd(*rwkv_in, kkr, kar, rkr, gwr, gbr, n_seq, seq_len,
                                            Hosted([shard["w_out"], shard["w_up"]], True))
    y_sb, o_raw, tot, sb_first, w_down_g = sb_fwd(by_seq(p), sgr, n_seq, seq_len, Hosted([shard["w_down"]], True))
    w_out_f = w_out_g.reshape(d_model, d_model)
    w_up_f = _full_cols(w_up_g)
    w_down_f = w_down_g.reshape(-1, d_model)
    d_ff = w_up_f.shape[1]
    z, y_sb = flat(z), flat(y_sb)
    (ycat,) = rw_call("mix_cat", lambda r, q: [jnp.concatenate([r[0] * r[1], r[2]], axis=1)],
                      [z, gate, y_sb], [], [(d_model, BF16)], 512)
    x2, h2 = matmul("proj_out", ycat, w_out_f, "nn", (F32, BF16), 512, d_model, d_model, extras=[x2d], pars=[g2],
                    epilogue=lambda acc, xv, g: (xv + acc, f_norm(xv + acc, g)))
    u, act = matmul("mlp_up", h2, w_up_f, "nn", (F32, BF16), 1024, d_ff // 4, d_model,
                    epilogue=lambda acc: (acc, jnp.square(jnp.maximum(acc, 0.0))))

    def loss_epilogue(acc, x2v, target, g):
        loss_rows, vjp = jax.vjp(lambda xv, gv: f_final(xv, gv, target), x2v + acc, g)
        dx3, dgf = vjp(jnp.ones_like(loss_rows))
        return dx3, dx3, jnp.broadcast_to(jnp.sum(loss_rows), (1, LANES)), dgf

    dx3, dx3_b, loss_acc, d_lnf = matmul(
        "mlp_down", act, w_down_f, "nn", (F32, BF16), 512, d_model, d_ff, extras=[x2, tgt], pars=[gf],
        epilogue=loss_epilogue, sums=[((1, LANES), F32), ((1, d_model), F32)])

    du = matmul("d_act", dx3_b, w_down_f, "nt", BF16, 1024, d_ff // 4, d_model, extras=[u],
                epilogue=lambda acc, uv: (acc * (2.0 * jnp.maximum(uv, 0.0)),))
    dw_down = matmul("dw_down", act, dx3_b, "tn", BF16, 512, d_model, 4096)
    dw_up = matmul("dw_up", h2, du, "tn", BF16, d_model, 512, 4096)

    def norm_bwd(acc, xv, dres, g):
        _, vjp = jax.vjp(f_norm, xv, g)
        dx, dg = vjp(acc)
        return dx + dres, dx + dres, dg

    dx2, dx2_b, d_ln2 = matmul("d_h2", du, w_up_f, "nt", (F32, BF16), 512, d_model, d_ff, extras=[x2, dx3], pars=[g2],
                               epilogue=norm_bwd, sums=[((1, d_model), F32)])

    dycat = matmul("d_ycat", dx2_b, w_out_f, "nt", F32, 512, d_model, d_model)
    dw_out = matmul("dw_out", ycat, dx2_b, "tn", BF16, d_model, d_model, 2048)
    dq, dk_sb, dv_sb, d_sg = sb_bwd(by_seq(p), sgr, o_raw, tot, by_seq(dycat), sb_first, n_seq, seq_len)
    d_sb = [flat(dq), flat(dk_sb), flat(dv_sb)]
    late_grads = {"w_out": dw_out.reshape(N_DEV, -1, d_model), "w_up": _col_parts(dw_up),
                  "w_down": dw_down.reshape(N_DEV, -1, d_model)}
    (dr, dkraw, dv, dlw, dasig, d_kk, d_ka, d_rk, d_gw, d_gb), late_parts = rwkv_bwd(
        *rwkv_in, kkr, kar, rkr, gwr, gbr, s0_all, by_seq(dycat), by_seq(gate), n_seq, seq_len,
        Hosted([late_grads[n] for n in LATE], False))
    prep_cots = [flat(t) for t in (dr, dkraw, dv, dlw, dasig)] + [dycat, z]
    dp, (d_mu, d_w0, d_wd, d_a0, d_wa, d_wg) = rwkv_prep_bwd(p, prep_edges, prep_pars, prep_cots, d_sb, seq_len)

    dw_in = matmul("dw_in", h1, dp, "tn", BF16, d_model, in_cols // 2, 2048)
    lora_parts = lambda t: _col_parts(t).astype(BF16)
    early_grads = {"w_in": _col_parts(dw_in), "w_decay_up": lora_parts(d_wd[:HEAD_DIM]),
                   "w_aaa_up": lora_parts(d_wa[HEAD_DIM:]), "w_gate_up": lora_parts(d_wg)}
    dx, d_ln1, *early_parts = matmul(
        "d_h1", dp, w_in_f, "nt", (F32,), 512, d_model, in_cols, extras=[x2d, dx2], pars=[g1],
        epilogue=lambda *t: norm_bwd(*t)[1:], sums=[((1, d_model), F32)],
        hosted=Hosted([early_grads[n] for n in EARLY], False))
    parts = dict(zip(EARLY, early_parts))
    parts.update(zip(LATE, late_parts))

    small_grads = {"ln1_g": d_ln1, "tok_mu": d_mu, "w0": d_w0, "a0": d_a0, "k_k": d_kk, "k_a": d_ka, "r_k": d_rk,
                   "gn_w": d_gw, "gn_b": d_gb, "sb_gain": d_sg, "ln2_g": d_ln2, "lnf_g": d_lnf}
    n_small = sum(int(weights[n].size) for n in SMALL)
    pack_rows = -(-(n_small + 1) // (8 * LANES)) * 8
    packed = _pack([small_grads[n] for n in SMALL] + [loss_acc[0, :1]], pack_rows)
    (small_parts,) = exchange("gather_small", [packed], True)

    results = {}
    for n in BIG:
        w2d = weights[n][0]
        tile = w2d.shape[0] if w2d.shape[0] <= 256 else 256
        results[n] = adamw("adamw_" + n, w2d, parts[n], mom_m[n][0], mom_v[n][0], tile)
    pk = lambda d: _pack([d[n] for n in SMALL] + [jnp.zeros((1,), F32)], pack_rows)
    sg, sd, sm, sv = adamw("adamw_small", pk(weights), small_parts, pk(mom_m), pk(mom_v), pack_rows)
    off = 0
    for n in SMALL:
        size = int(weights[n].size)
        results[n] = tuple(t.reshape(-1)[off:off + size] for t in (sg, sd, sm, sv))
        off += size
    loss = sg.reshape(-1)[off]

    out = [loss, dx.reshape(x.shape)]
    for kind in range(4):
        out += [results[n][kind].reshape(weights[n].shape) for n in ORDER]
    return tuple(out)
```

```python
import functools
import math

import jax
import jax.numpy as jnp
from jax import lax
from jax.experimental import pallas as pl
from jax.experimental.pallas import tpu as pltpu

F32 = jnp.float32
BF16 = jnp.bfloat16

N_DEV = 8
HEAD_DIM = 64
LANES = 128
RWKV_W = 512
SB_W = 512
LORA_WA = 128
GATE_LORA = 128
RWKV_COLS = 3 * RWKV_W + LORA_WA + GATE_LORA
RMS_EPS = 1e-5
GN_EPS = 64e-5
CHUNK = 64
QB = 256
SB_SCALE = HEAD_DIM ** -0.5
ADAM_LR, ADAM_B1, ADAM_B2, ADAM_EPS, ADAM_WD, ADAM_STEP = 0.001, 0.9, 0.999, 1e-08, 0.01, 10
VMEM_LIMIT = 56 * 1024 * 1024


_DIMS = {
    "nn": (((1,), (0,)), ((), ())),
    "nt": (((1,), (1,)), ((), ())),
    "tn": (((0,), (0,)), ((), ())),
}


def _pieces(x, n):
    if n == 1:
        return [x.astype(BF16)]
    out, rem = [], x.astype(F32)
    for i in range(n):
        p = rem.astype(BF16)
        out.append(p)
        if i + 1 < n:
            rem = rem - p.astype(F32)
    return out


def _dot(a, b, form, pa, pb):
    pieces_a, pieces_b = _pieces(a, pa), _pieces(b, pb)
    keep = max(pa, pb)
    acc = None
    for i, ai in enumerate(pieces_a):
        for j, bj in enumerate(pieces_b):
            if i + j >= keep:
                continue
            t = lax.dot_general(ai, bj, _DIMS[form], preferred_element_type=F32)
            acc = t if acc is None else acc + t
    return acc


BOTH = (True, True)


@functools.partial(jax.custom_vjp, nondiff_argnums=(2, 3, 4, 5))
def mm(a, b, form, pa, pb, diff=BOTH):
    return _dot(a, b, form, pa, pb)


def _mm_fwd(a, b, form, pa, pb, diff):
    return _dot(a, b, form, pa, pb), (a, b)


def _mm_bwd(form, pa, pb, diff, res, g):
    a, b = res
    pg = max(pa, pb)
    da, db = jnp.zeros_like(a), jnp.zeros_like(b)
    if form == "nn":
        if diff[0]:
            da = mm(g, b, "nt", pg, pb)
        if diff[1]:
            db = mm(a, g, "tn", pa, pg)
    elif form == "nt":
        if diff[0]:
            da = mm(g, b, "nn", pg, pb)
        if diff[1]:
            db = mm(g, a, "tn", pg, pa)
    else:
        if diff[0]:
            da = mm(b, g, "nt", pb, pg)
        if diff[1]:
            db = mm(a, g, "nn", pa, pg)
    return da, db


mm.defvjp(_mm_fwd, _mm_bwd)


def _stack_rows(top, bottom):
    return jnp.concatenate([top, bottom], axis=0)


@jax.custom_vjp
def _split_rows(x):
    half = x.shape[0] // 2
    return x[:half], x[half:]


def _split_rows_fwd(x):
    return _split_rows(x), None


def _split_rows_bwd(_, grads):
    return (_stack_rows(*grads),)


_split_rows.defvjp(_split_rows_fwd, _split_rows_bwd)


def _lane_lo(shape):
    return lax.broadcasted_iota(jnp.int32, shape, len(shape) - 1) < HEAD_DIM


def _segsum(x):
    lo = _lane_lo(x.shape)
    s_lo = jnp.sum(jnp.where(lo, x, 0.0), axis=-1, keepdims=True)
    s_hi = jnp.sum(jnp.where(lo, 0.0, x), axis=-1, keepdims=True)
    return jnp.where(lo, s_lo, s_hi)


def _sigmoid(x):
    return 0.5 * (jnp.tanh(0.5 * x) + 1.0)


def _softplus(x):
    return jnp.maximum(x, 0.0) + jnp.log(1.0 + jnp.exp(-jnp.abs(x)))


@jax.custom_vjp
def _log_one_minus_sigmoid(z):
    return -jnp.maximum(z, 0.0) - jnp.log(1.0 + jnp.exp(-jnp.abs(z)))


def _log_oms_fwd(z):
    out = _log_one_minus_sigmoid(z)
    return out, out


def _log_oms_bwd(out, g):
    return (g * (jnp.exp(out) - 1.0),)


_log_one_minus_sigmoid.defvjp(_log_oms_fwd, _log_oms_bwd)


def f_norm(x, g):
    return x * lax.rsqrt(jnp.mean(x * x, axis=-1, keepdims=True) + RMS_EPS) * g


def f_prep(p, pprev, mu, w0, wd_pad, a0, wa_pad, wg):
    pm = p + mu * (pprev - p)
    r = pm[:, 0:RWKV_W]
    k = pm[:, RWKV_W:2 * RWKV_W]
    v = pm[:, 2 * RWKV_W:3 * RWKV_W]
    xwa = pm[:, 3 * RWKV_W:3 * RWKV_W + LORA_WA]
    xg = pm[:, 3 * RWKV_W + LORA_WA:RWKV_COLS]
    w = -_softplus(-(w0 + mm(jnp.tanh(xwa), wd_pad, "nn", 1, 1))) - 0.5
    lw = -jnp.exp(w)
    asig = _sigmoid(a0 + mm(xwa, wa_pad, "nn", 1, 1))
    gate = mm(_sigmoid(xg), wg, "nn", 1, 1)
    return r, k, v, lw, asig, gate


def _tri(n, kind):
    row = lax.broadcasted_iota(jnp.int32, (n, n), 0)
    col = lax.broadcasted_iota(jnp.int32, (n, n), 1)
    if kind == "lower_incl":
        return row >= col
    return row > col


def rwkv_chunk(state, r, kraw, v, lw, asig, k_k, k_a, r_k, gn_w, gn_b, hp):
    n = len(r)
    L = r[0].shape[0]
    lo = _lane_lo((1, LANES))
    masks = (lo, jnp.logical_not(lo))
    incl = _tri(L, "lower_incl")
    strict = _tri(L, "strict")
    tri = incl.astype(F32)
    eye = (lax.broadcasted_iota(jnp.int32, (L, L), 0) == lax.broadcasted_iota(jnp.int32, (L, L), 1)).astype(F32)
    kk = [x * w for x, w in zip(kraw, k_k)]
    kk = [x / jnp.maximum(jnp.sqrt(_segsum(x * x)), 1e-12) for x in kk]
    k = [x * (1.0 + (s - 1.0) * w) for x, s, w in zip(kraw, asig, k_a)]
    b = [x * s for x, s in zip(kk, asig)]
    c = [mm(tri, x, "nn", 1, 3, (False, True)) for x in lw]
    at = [-x * jnp.exp(ci - li) for x, ci, li in zip(kk, c, lw)]
    rt = [x * jnp.exp(ci) for x, ci in zip(r, c)]
    einv = [jnp.exp(-ci) for ci in c]
    bt = [x * e for x, e in zip(b, einv)]
    kt = [x * e for x, e in zip(k, einv)]
    inst = [(s, m) for s in range(n) for m in masks]
    ar_h = [_stack_rows(jnp.where(m, at[s], 0.0), jnp.where(m, rt[s], 0.0)) for s, m in inst]
    on_b = [_split_rows(mm(x, bt[s], "nt", hp, hp)) for x, (s, _) in zip(ar_h, inst)]
    on_k = [_split_rows(mm(x, kt[s], "nt", hp, hp)) for x, (s, _) in zip(ar_h, inst)]
    a_ab = [jnp.where(strict, x, 0.0) for x, _ in on_b]
    b_rb = [jnp.where(incl, x, 0.0) for _, x in on_b]
    a_ak = [jnp.where(strict, x, 0.0) for x, _ in on_k]
    b_rk = [jnp.where(incl, x, 0.0) for _, x in on_k]
    tinv = [eye + x for x in a_ab]
    pw = [mm(x, x, "nn", hp, hp) for x in a_ab]
    for _ in range(int(math.log2(L)) - 2):
        both = [_split_rows(mm(_stack_rows(t, x), x, "nn", hp, hp)) for t, x in zip(tinv, pw)]
        tinv = [t + tx for t, (tx, _) in zip(tinv, both)]
        pw = [xx for _, xx in both]
    tinv = [t + mm(t, x, "nn", hp, hp) for t, x in zip(tinv, pw)]
    on_state = [_split_rows(mm(x, state[s], "nt", hp, hp)) for x, (s, _) in zip(ar_h, inst)]
    on_v = [_split_rows(mm(_stack_rows(m1, m2), v[s], "nn", hp, hp)) for m1, m2, (s, _) in zip(a_ak, b_rk, inst)]
    u_h = [mm(t, sa + av, "nn", hp, hp) for t, (sa, _), (av, _) in zip(tinv, on_state, on_v)]
    y_h = [sr + mm(m1, u, "nn", hp, hp) + bv for (_, sr), m1, u, (_, bv) in zip(on_state, b_rb, u_h, on_v)]
    u_all = [jnp.where(lo, u_h[2 * s], u_h[2 * s + 1]) for s in range(n)]
    y_all = [jnp.where(lo, y_h[2 * s], y_h[2 * s + 1]) for s in range(n)]
    c_last = [jnp.sum(x, axis=0, keepdims=True) for x in lw]
    efwd = [jnp.exp(cl - ci) for cl, ci in zip(c_last, c)]
    new_state = [st * jnp.exp(cl) + mm(_stack_rows(u, vi), _stack_rows(bi * e, ki * e), "tn", hp, hp)
                 for st, cl, u, bi, e, vi, ki in zip(state, c_last, u_all, b, efwd, v, k)]
    row_head = lax.broadcasted_iota(jnp.int32, (LANES, LANES), 0) // HEAD_DIM
    col_head = lax.broadcasted_iota(jnp.int32, (LANES, LANES), 1) // HEAD_DIM
    new_state = [jnp.where(row_head == col_head, x, 0.0) for x in new_state]
    outs = []
    for y, ri, ki, vi, w_rk, w_gw, w_gb in zip(y_all, r, k, v, r_k, gn_w, gn_b):
        mean = _segsum(y) * (1.0 / HEAD_DIM)
        d = y - mean
        var = _segsum(d * d) * (1.0 / HEAD_DIM)
        yn = d * lax.rsqrt(var + GN_EPS) * w_gw + w_gb
        outs.append(yn + _segsum(ri * ki * w_rk) * vi)
    return outs, new_state


def sb_tile(q, k, v, c_lo, c_hi, diag, from_here=None):
    n = len(q)
    lo = _lane_lo((1, LANES))
    below = _tri(QB, "strict")
    from_s = _tri(QB, "lower_incl").astype(F32)
    inst = [(s, h) for s in range(n) for h in (0, 1)]
    carry = [(c_lo[s], c_hi[s])[h] for s, h in inst]
    z = [mm(q[s][h], k[s], "nt", 1, 1) for s, h in inst]
    log_keep = [_log_one_minus_sigmoid(x) for x in z]
    if diag:
        log_keep = [jnp.where(below, x, 0.0) for x in log_keep]
    own = [jnp.sum(x, axis=1, keepdims=True) for x in log_keep]
    if from_here is not None:
        carry = [lax.stop_gradient(from_here[s][h] - o) + cr for (s, h), o, cr in zip(inst, own, carry)]
    tail = [mm(x, from_s, "nn", SB_SUM_PIECES, 1, (True, False)) for x in log_keep]
    log_a = [x + tl + cr for x, tl, cr in zip(z, tail, carry)]
    if diag:
        log_a = [jnp.where(below, x, -1e30) for x in log_a]
    att = [jnp.exp(x) for x in log_a]
    out_h = [mm(x, v[s], "nn", 1, 1) for x, (s, _) in zip(att, inst)]
    out = [jnp.where(lo, out_h[2 * s], out_h[2 * s + 1]) for s in range(n)]
    return out, [own[2 * s] for s in range(n)], [own[2 * s + 1] for s in range(n)]


def sb_split_q(q):
    lo = _lane_lo((1, LANES))
    qs = q * SB_SCALE
    return jnp.where(lo, qs, 0.0), jnp.where(lo, 0.0, qs)


def sb_post(o, gain):
    return o * lax.rsqrt(_segsum(o * o) * (1.0 / HEAD_DIM) + RMS_EPS) * gain


def f_final(x3, g, target):
    y = f_norm(x3, g)
    err = y - target
    return 0.5 * jnp.mean(err * err, axis=-1, keepdims=True)


def _params(sem):
    return pltpu.CompilerParams(dimension_semantics=sem, vmem_limit_bytes=VMEM_LIMIT)


def rw_call(name, body_fn, rows, pars, out_rows, tile, hosted=None):
    n_rows = rows[0].shape[0]
    tile = min(tile, n_rows)
    steps = n_rows // tile
    row_specs = [pl.BlockSpec((tile, arr.shape[1]), lambda i: (i, 0)) for arr in rows]
    par_specs = [pl.BlockSpec(p.shape, lambda i: (0, 0)) for p in pars]
    nr, npar, nor = len(rows), len(pars), len(out_rows)
    n_host = hosted.n if hosted is not None else 0

    def body(*refs):
        if hosted is not None:
            refs, copies = hosted.split(refs, nr + npar, nor)
            hosted.run(copies, pl.program_id(0) == 0, pl.program_id(0) == steps - 1)
        row_outs = body_fn([r[...] for r in refs[:nr]], [r[...] for r in refs[nr:nr + npar]])
        for ref, val in zip(refs[nr + npar:], row_outs):
            ref[...] = val.astype(ref.dtype)

    outs = pl.pallas_call(
        body, name=name, grid=(steps,), in_specs=row_specs + par_specs + [ANY_SPEC] * n_host,
        out_specs=[pl.BlockSpec((tile, c), lambda i: (i, 0)) for c, _ in out_rows] + [ANY_SPEC] * n_host,
        out_shape=[jax.ShapeDtypeStruct((n_rows, c), dt) for c, dt in out_rows]
        + (exchange_shapes(hosted.arrays, hosted.gather) if n_host else []),
        scratch_shapes=exchange_sems(n_host) if n_host else [],
        compiler_params=_params(("arbitrary",)),
    )(*rows, *pars, *(hosted.arrays if n_host else []))
    return outs


def matmul(name, a, b, form, out_dtype, tm, tn, tk, extras=(), pars=(), epilogue=None, sums=(), hosted=None):
    out_dtypes = out_dtype if isinstance(out_dtype, tuple) else (out_dtype,)
    tm, tn, tk = min(tm, a.shape[1 if form == "tn" else 0]), min(tn, b.shape[0 if form == "nt" else 1]), min(tk, a.shape[0 if form == "tn" else 1])
    if form == "nn":
        (m, kd), n = a.shape, b.shape[1]
        a_spec = pl.BlockSpec((tm, tk), lambda i, j, k: (i, k))
        b_spec = pl.BlockSpec((tk, tn), lambda i, j, k: (k, j))
    elif form == "nt":
        (m, kd), n = a.shape, b.shape[0]
        a_spec = pl.BlockSpec((tm, tk), lambda i, j, k: (i, k))
        b_spec = pl.BlockSpec((tn, tk), lambda i, j, k: (j, k))
    else:
        (kd, m), n = a.shape, b.shape[1]
        a_spec = pl.BlockSpec((tk, tm), lambda i, j, k: (k, i))
        b_spec = pl.BlockSpec((tk, tn), lambda i, j, k: (k, j))
    ksteps = kd // tk

    n_extra, n_par, n_out, n_sum = len(extras), len(pars), len(out_dtypes), len(sums)
    n_in = 2 + n_extra + n_par
    grid = (m // tm, n // tn, ksteps)

    def body(*refs):
        if hosted is not None:
            refs, copies = hosted.split(refs, n_in, n_out + n_sum)
            here = [pl.program_id(d) for d in range(3)]
            hosted.run(copies, functools.reduce(jnp.logical_and, [h == 0 for h in here]),
                       functools.reduce(jnp.logical_and, [h == g - 1 for h, g in zip(here, grid)]))
        a_ref, b_ref, rest = refs[0], refs[1], refs[2:]
        e_refs, o_refs = rest[:n_extra + n_par], rest[n_extra + n_par:n_extra + n_par + n_out]
        s_refs = rest[n_extra + n_par + n_out:n_extra + n_par + n_out + n_sum]
        kstep = pl.program_id(2)
        part = lax.dot_general(a_ref[...].astype(BF16), b_ref[...].astype(BF16), _DIMS[form],
                               preferred_element_type=F32)

        def finish(acc):
            outs = epilogue(acc, *[r[...] for r in e_refs]) if epilogue else (acc,)
            for ref, val in zip(o_refs, outs[:n_out]):
                ref[...] = val.astype(ref.dtype)
            if n_sum:
                first_tile = jnp.logical_and(pl.program_id(0) == 0, pl.program_id(1) == 0)

                @pl.when(first_tile)
                def _():
                    for ref, val in zip(s_refs, outs[n_out:]):
                        ref[...] = val.astype(ref.dtype)

                @pl.when(jnp.logical_not(first_tile))
                def _():
                    for ref, val in zip(s_refs, outs[n_out:]):
                        ref[...] = ref[...] + val.astype(ref.dtype)

        if ksteps == 1:
            finish(part)
            return
        acc_ref = rest[n_extra + n_par + n_out + n_sum]

        @pl.when(kstep == 0)
        def _():
            acc_ref[...] = part

        @pl.when(jnp.logical_and(kstep > 0, kstep < ksteps - 1))
        def _():
            acc_ref[...] = acc_ref[...] + part

        @pl.when(kstep == ksteps - 1)
        def _():
            finish(acc_ref[...] + part)

    out_spec = pl.BlockSpec((tm, tn), lambda i, j, k: (i, j))
    whole = lambda shape: pl.BlockSpec(shape, lambda i, j, k: (0,) * len(shape))
    n_host = hosted.n if hosted is not None else 0
    host_in = list(hosted.arrays) if hosted is not None else []
    host_out = exchange_shapes(hosted.arrays, hosted.gather) if hosted is not None else []
    outs = pl.pallas_call(
        body, name=name, grid=grid,
        in_specs=[a_spec, b_spec] + [out_spec] * n_extra + [whole(t.shape) for t in pars] + [ANY_SPEC] * n_host,
        out_specs=[out_spec] * n_out + [whole(s) for s, _ in sums] + [ANY_SPEC] * n_host,
        out_shape=[jax.ShapeDtypeStruct((m, n), dt) for dt in out_dtypes]
        + [jax.ShapeDtypeStruct(s, dt) for s, dt in sums] + host_out,
        scratch_shapes=([pltpu.VMEM((tm, tn), F32)] if ksteps > 1 else []) + (exchange_sems(n_host) if n_host else []),
        compiler_params=_params(("arbitrary",) * 3 if (sums or n_host) else ("parallel", "parallel", "arbitrary")),
    )(a, b, *extras, *pars, *host_in)
    return outs if (isinstance(out_dtype, tuple) or sums or n_host) else outs[0]


PREP_TILE = 256
PREP_TILE_BWD = 128
SUBLANES = 8


def _shift_in(rows, first):
    rolled = pltpu.roll(rows, 1, 0)
    row = lax.broadcasted_iota(jnp.int32, (SUBLANES, rows.shape[1]), 0)
    head = jnp.where(row == 0, first, rolled[0:SUBLANES])
    return jnp.concatenate([head, rolled[SUBLANES:]], axis=0), rolled


def rwkv_prep_fwd(p, pars, seq_len):
    n_tok = p.shape[0]
    tile = min(PREP_TILE, seq_len)
    tile_b = min(PREP_TILE_BWD, tile)
    steps, per_seq, sub = n_tok // tile, seq_len // tile, tile // tile_b
    n_par = len(pars)

    def body(p_ref, *rest):
        par_refs, out_refs, edge_ref, last8 = rest[:n_par], rest[n_par:n_par + 6], rest[n_par + 6], rest[n_par + 7]
        step = pl.program_id(0)

        @pl.when(step == 0)
        def _():
            last8[...] = jnp.zeros_like(last8)

        rows = p_ref[...]
        before = jnp.where(step % per_seq == 0, 0.0, pltpu.roll(last8[...], 1, 0))
        prev, rolled = _shift_in(rows, before)
        edge_ref[0] = prev[0:SUBLANES]
        for m in range(1, sub):
            edge_ref[m] = rolled[m * tile_b:m * tile_b + SUBLANES]
        last8[...] = rows[tile - SUBLANES:tile]
        for ref, val in zip(out_refs, f_prep(rows, prev, *[r[...] for r in par_refs])):
            ref[...] = val

    row_out = pl.BlockSpec((tile, RWKV_W), lambda i: (i, 0))
    outs = pl.pallas_call(
        body, name="rwkv_prep", grid=(steps,),
        in_specs=[pl.BlockSpec((tile, RWKV_COLS), lambda i: (i, 0))] + [pl.BlockSpec(t.shape, lambda i: (0, 0)) for t in pars],
        out_specs=[row_out] * 6 + [pl.BlockSpec((sub, SUBLANES, RWKV_COLS), lambda i: (i, 0, 0))],
        out_shape=[jax.ShapeDtypeStruct((n_tok, RWKV_W), F32)] * 6
        + [jax.ShapeDtypeStruct((steps * sub, SUBLANES, RWKV_COLS), F32)],
        scratch_shapes=[pltpu.VMEM((SUBLANES, RWKV_COLS), F32)],
        compiler_params=_params(("arbitrary",)),
    )(p, *pars)
    return outs[:6], outs[6]


def rwkv_prep_bwd(p, edges, pars, cots, d_sb, seq_len):
    n_tok = p.shape[0]
    tile = min(PREP_TILE_BWD, seq_len)
    steps, per_seq = n_tok // tile, seq_len // tile
    n_par = len(pars)
    back = lambda i: steps - 1 - i

    def body(p_ref, edge_ref, *rest):
        par_refs, rest = rest[:n_par], rest[n_par:]
        cot_refs, dy_ref, z_ref, sb_refs = rest[:5], rest[5], rest[6], rest[7:10]
        dp_ref, acc_refs, next8 = rest[10], rest[11:11 + n_par], rest[11 + n_par]
        step = pl.program_id(0)
        first = step == 0

        @pl.when(first)
        def _():
            next8[...] = jnp.zeros_like(next8)

        rows = p_ref[...]
        prev, _ = _shift_in(rows, edge_ref[0])
        _, vjp = jax.vjp(f_prep, rows, prev, *[r[...].astype(F32) for r in par_refs])
        grads = vjp(tuple(r[...] for r in cot_refs) + (dy_ref[...] * z_ref[...],))
        d_rows, d_prev = grads[0], grads[1]
        up = pltpu.roll(d_prev, tile - 1, 0)
        ends_seq = back(step) % per_seq == per_seq - 1
        after = jnp.where(ends_seq, 0.0, pltpu.roll(next8[...], SUBLANES - 1, 0))
        row = lax.broadcasted_iota(jnp.int32, (SUBLANES, RWKV_COLS), 0)
        tail = jnp.where(row == SUBLANES - 1, after, up[tile - SUBLANES:tile])
        d_rows = d_rows + jnp.concatenate([up[:tile - SUBLANES], tail], axis=0)
        next8[...] = d_prev[0:SUBLANES]
        dp_ref[...] = jnp.concatenate([d_rows] + [r[...] for r in sb_refs], axis=1).astype(dp_ref.dtype)

        @pl.when(first)
        def _():
            for ref, val in zip(acc_refs, grads[2:]):
                ref[...] = val

        @pl.when(jnp.logical_not(first))
        def _():
            for ref, val in zip(acc_refs, grads[2:]):
                ref[...] = ref[...] + val

    cols = RWKV_COLS + sum(t.shape[1] for t in d_sb)
    half = pl.BlockSpec((tile, RWKV_W), lambda i: (back(i), 0))
    par_specs = [pl.BlockSpec(t.shape, lambda i: (0, 0)) for t in pars]
    outs = pl.pallas_call(
        body, name="d_rwkv_prep", grid=(steps,),
        in_specs=[pl.BlockSpec((tile, RWKV_COLS), lambda i: (back(i), 0)),
                  pl.BlockSpec((1, SUBLANES, RWKV_COLS), lambda i: (back(i), 0, 0))] + par_specs + [half] * 10,
        out_specs=[pl.BlockSpec((tile, cols), lambda i: (back(i), 0))] + par_specs,
        out_shape=[jax.ShapeDtypeStruct((n_tok, cols), BF16)] + [jax.ShapeDtypeStruct(t.shape, F32) for t in pars],
        scratch_shapes=[pltpu.VMEM((SUBLANES, RWKV_COLS), F32)],
        compiler_params=_params(("arbitrary",)),
    )(p, edges, *pars, *cots, *d_sb)
    return outs[0], outs[1:]


RWKV_HP = 1


RWKV_PAIRS = 4


def _rwkv_specs(n_seq, chunk_of):
    width = RWKV_PAIRS * LANES
    row = pl.BlockSpec((n_seq, CHUNK, width), lambda g, c: (0, chunk_of(c), g))
    par = pl.BlockSpec((1, width), lambda g, c: (0, g))
    s0 = pl.BlockSpec((1, 1, RWKV_PAIRS * n_seq, LANES, LANES), lambda g, c: (g, chunk_of(c), 0, 0, 0))
    return row, par, s0


class Hosted:
    def __init__(self, arrays, gather):
        self.arrays, self.gather, self.n = list(arrays), gather, len(arrays)

    def split(self, refs, n_in, n_out):
        n = self.n
        ins, outs, sems = refs[n_in:n_in + n], refs[n_in + n + n_out:n_in + 2 * n + n_out], refs[-3:]
        own = refs[:n_in] + refs[n_in + n:n_in + n + n_out] + refs[n_in + 2 * n + n_out:-3]
        return own, exchange_copies(ins, outs, *sems, self.gather)

    def run(self, copies, first, last):
        @pl.when(first)
        def _():
            for cp in copies:
                cp.start()

        @pl.when(last)
        def _():
            for cp in copies:
                cp.wait()


class HostedChipGather(Hosted):
    def __init__(self, arrays):
        super().__init__(arrays, True)

    def split(self, refs, n_in, n_out):
        n = self.n
        ins, outs, sems = refs[n_in:n_in + n], refs[n_in + n + n_out:n_in + 2 * n + n_out], refs[-3:]
        own = refs[:n_in] + refs[n_in + n:n_in + n + n_out] + refs[n_in + 2 * n + n_out:-3]
        return own, (ins, outs, sems)

    def run(self, state, first, last):
        ins, outs, (send_sems, recv_sems, local_sems) = state
        x, y, c = lax.axis_index("x"), lax.axis_index("y"), lax.axis_index("c")
        chips = [((x + 1) % 2, y), (x, (y + 1) % 2), ((x + 1) % 2, (y + 1) % 2)]

        def block(a, k, of, to, src=None):
            dst = outs[a].at[4 * of[0] + 2 * of[1] + of[2]]
            return pltpu.make_async_remote_copy(
                src_ref=dst if src is None else src, dst_ref=dst, send_sem=send_sems.at[a, k],
                recv_sem=recv_sems.at[a, k], device_id=to, device_id_type=pl.DeviceIdType.MESH)

        me, sibling = (x, y, c), (x, y, 1 - c)
        local = [pltpu.make_async_copy(ins[a], outs[a].at[4 * x + 2 * y + c], local_sems.at[a]) for a in range(self.n)]
        mine = [block(a, 0, me, sibling, ins[a]) for a in range(self.n)]
        mine += [block(a, 1 + j, me, (*chip, c), ins[a]) for a in range(self.n) for j, chip in enumerate(chips)]
        passed = [block(a, 4 + j, (*chip, c), sibling) for a in range(self.n) for j, chip in enumerate(chips)]

        @pl.when(first)
        def _():
            for cp in local + mine:
                cp.start()

        @pl.when(last)
        def _():
            for a in range(self.n):
                for j, chip in enumerate(chips):
                    block(a, 1 + j, (*chip, c), me).wait_recv()
                    passed[a * len(chips) + j].start()
            for a in range(self.n):
                block(a, 0, sibling, me).wait_recv()
                for j, chip in enumerate(chips):
                    block(a, 4 + j, (*chip, 1 - c), me).wait_recv()
            for cp in mine + passed:
                cp.wait_send()
            for cp in local:
                cp.wait()


def rwkv_fwd(r, kraw, v, lw, asig, k_k, k_a, r_k, gn_w, gn_b, n_seq, seq_len, hosted):
    n_chunks = seq_len // CHUNK
    n_groups = RWKV_W // (RWKV_PAIRS * LANES)
    n_inst = RWKV_PAIRS * n_seq
    row, par, s0_spec = _rwkv_specs(n_seq, lambda c: c)
    inst = [(s, pl.ds(pp * LANES, LANES)) for pp in range(RWKV_PAIRS) for s in range(n_seq)]

    def body(*refs):
        own, copies = hosted.split(refs, 10, 2)
        row_refs, par_refs, (z_ref, s0_ref, state) = own[:5], own[5:10], own[10:]
        step = pl.program_id(0) * n_chunks + pl.program_id(1)
        hosted.run(copies, step == 0, step == n_groups * n_chunks - 1)

        @pl.when(pl.program_id(1) == 0)
        def _():
            state[...] = jnp.zeros_like(state)

        s0 = [state[i] for i in range(n_inst)]
        rows = [[ref[s, :, lanes] for s, lanes in inst] for ref in row_refs]
        pars = [[ref[:, lanes] for _, lanes in inst] for ref in par_refs]
        z, s1 = rwkv_chunk(s0, *rows, *pars, RWKV_HP)
        for i, (s, lanes) in enumerate(inst):
            s0_ref[0, 0, i] = s0[i]
            z_ref[s, :, lanes] = z[i]
            state[i] = s1[i]

    outs = pl.pallas_call(
        body, name="rwkv_fwd", grid=(n_groups, n_chunks),
        in_specs=[row] * 5 + [par] * 5 + [ANY_SPEC] * hosted.n, out_specs=[row, s0_spec] + [ANY_SPEC] * hosted.n,
        out_shape=[jax.ShapeDtypeStruct(r.shape, F32),
                   jax.ShapeDtypeStruct((n_groups, n_chunks, n_inst, LANES, LANES), F32)]
        + exchange_shapes(hosted.arrays, hosted.gather),
        scratch_shapes=[pltpu.VMEM((n_inst, LANES, LANES), F32)] + exchange_sems(hosted.n),
        compiler_params=_params(("arbitrary", "arbitrary")),
    )(r, kraw, v, lw, asig, k_k, k_a, r_k, gn_w, gn_b, *hosted.arrays)
    return outs[0], outs[1], outs[2:]


def rwkv_bwd(r, kraw, v, lw, asig, k_k, k_a, r_k, gn_w, gn_b, s0_all, dy, gate, n_seq, seq_len, hosted):
    n_chunks = seq_len // CHUNK
    n_groups = RWKV_W // (RWKV_PAIRS * LANES)
    n_inst = RWKV_PAIRS * n_seq
    row, par, s0_spec = _rwkv_specs(n_seq, lambda c: n_chunks - 1 - c)
    inst = [(s, pl.ds(pp * LANES, LANES)) for pp in range(RWKV_PAIRS) for s in range(n_seq)]

    def body(*refs):
        own, copies = hosted.split(refs, 13, 10)
        row_refs, par_refs, s0_ref, dy_ref, gate_ref = own[:5], own[5:10], own[10], own[11], own[12]
        drow_refs, dpar_refs, dstate = own[13:18], own[18:23], own[23]
        step = pl.program_id(0) * n_chunks + pl.program_id(1)
        hosted.run(copies, step == 0, step == n_groups * n_chunks - 1)
        first = pl.program_id(1) == 0

        @pl.when(first)
        def _():
            dstate[...] = jnp.zeros_like(dstate)

        fn = functools.partial(rwkv_chunk, hp=RWKV_HP)
        rows = [[ref[s, :, lanes] for s, lanes in inst] for ref in row_refs]
        pars = [[ref[:, lanes] for _, lanes in inst] for ref in par_refs]
        _, vjp = jax.vjp(fn, [s0_ref[0, 0, i] for i in range(n_inst)], *rows, *pars)
        dz = [dy_ref[s, :, lanes] * gate_ref[s, :, lanes] for s, lanes in inst]
        grads = vjp((dz, [dstate[i] for i in range(n_inst)]))
        for i, (s, lanes) in enumerate(inst):
            dstate[i] = grads[0][i]
            for ref, val in zip(drow_refs, grads[1:6]):
                ref[s, :, lanes] = val[i]

        def accumulate(start):
            for ref, val in zip(dpar_refs, grads[6:]):
                for pp in range(RWKV_PAIRS):
                    lanes = pl.ds(pp * LANES, LANES)
                    total = functools.reduce(jnp.add, val[pp * n_seq:(pp + 1) * n_seq])
                    ref[:, lanes] = total if start else ref[:, lanes] + total

        @pl.when(first)
        def _():
            accumulate(True)

        @pl.when(jnp.logical_not(first))
        def _():
            accumulate(False)

    rows_shape = jax.ShapeDtypeStruct(r.shape, F32)
    par_shape = jax.ShapeDtypeStruct((1, RWKV_W), F32)
    outs = pl.pallas_call(
        body, name="rwkv_bwd", grid=(n_groups, n_chunks),
        in_specs=[row] * 5 + [par] * 5 + [s0_spec, row, row] + [ANY_SPEC] * hosted.n,
        out_specs=[row] * 5 + [par] * 5 + [ANY_SPEC] * hosted.n,
        out_shape=[rows_shape] * 5 + [par_shape] * 5 + exchange_shapes(hosted.arrays, hosted.gather),
        scratch_shapes=[pltpu.VMEM((n_inst, LANES, LANES), F32)] + exchange_sems(hosted.n),
        compiler_params=_params(("arbitrary", "arbitrary")),
    )(r, kraw, v, lw, asig, k_k, k_a, r_k, gn_w, gn_b, s0_all, dy, gate, *hosted.arrays)
    return outs[:10], outs[10:]


SB_Q0 = RWKV_COLS // LANES
SB_K0 = SB_Q0 + SB_W // LANES
SB_V0 = SB_K0 + SB_W // LANES
SB_SEQS = 2
SB_BUFFERS = pl.Buffered(1)
SB_SUM_PIECES = 2
SB_DEAD = -110.0


def _col_of(c_lo, c_hi):
    return jnp.where(_lane_lo((1, LANES)), c_lo, c_hi)


def sb_fwd(p, gain, n_seq, seq_len, hosted):
    n_pairs = SB_W // LANES
    n_q = seq_len // QB
    nb = min(SB_SEQS, n_seq)

    def seq_spec(c0):
        return pl.BlockSpec((nb, seq_len, LANES), functools.partial(lambda b, h, c0: (b, 0, c0 + h), c0=c0),
                            pipeline_mode=SB_BUFFERS)

    out_spec = pl.BlockSpec((nb, seq_len, LANES), lambda b, h: (b, 0, h), pipeline_mode=SB_BUFFERS)

    def body(*refs):
        own, copies = hosted.split(refs, 4, 4)
        q_ref, k_ref, v_ref, g_ref, y_ref, o_ref, tot_ref, first_ref = own
        step = pl.program_id(0) * n_pairs + pl.program_id(1)
        hosted.run(copies, step == 0, step == (n_seq // nb) * n_pairs - 1)
        gain = g_ref[...]

        def q_block(i, _):
            qs = pl.multiple_of(i * QB, QB)
            seqs = range(nb)
            zeros = [jnp.zeros((QB, 1), F32)] * nb
            qv = [sb_split_q(q_ref[s, pl.ds(qs, QB), :]) for s in seqs]
            add = lambda xs, ys: [x + y for x, y in zip(xs, ys)]

            def tiles(ks, c_lo, c_hi, diag):
                return sb_tile(qv, [k_ref[s, pl.ds(ks, QB), :] for s in seqs],
                               [v_ref[s, pl.ds(ks, QB), :] for s in seqs], c_lo, c_hi, diag)

            def alive(c_lo, c_hi):
                top = functools.reduce(jnp.maximum, list(c_lo) + list(c_hi))
                return jnp.max(top) > SB_DEAD

            def k_block(state):
                j, _, (o, c_lo, c_hi) = state
                o2, s_lo, s_hi = tiles(pl.multiple_of(j * QB, QB), c_lo, c_hi, False)
                c_lo, c_hi = add(c_lo, s_lo), add(c_hi, s_hi)
                return j - 1, alive(c_lo, c_hi), (add(o, o2), c_lo, c_hi)

            o, c_lo, c_hi = tiles(qs, zeros, zeros, True)
            j, _, (o, c_lo, c_hi) = lax.while_loop(lambda st: jnp.logical_and(st[0] >= 0, st[1]), k_block,
                                                   (i - 1, alive(c_lo, c_hi), (o, c_lo, c_hi)))
            first_ref[pl.program_id(0), pl.program_id(1), i] = j + 1
            for s in seqs:
                o_ref[s, pl.ds(qs, QB), :] = o[s]
                tot_ref[s, pl.ds(qs, QB), :] = jnp.broadcast_to(_col_of(c_lo[s], c_hi[s]), (QB, LANES))
                y_ref[s, pl.ds(qs, QB), :] = sb_post(o[s], gain)
            return 0

        lax.fori_loop(0, n_q, q_block, 0)

    shape = jax.ShapeDtypeStruct((n_seq, seq_len, SB_W), F32)
    return pl.pallas_call(
        body, name="sb_fwd", grid=(n_seq // nb, n_pairs),
        in_specs=[seq_spec(SB_Q0), seq_spec(SB_K0), seq_spec(SB_V0), pl.BlockSpec((1, LANES), lambda b, h: (0, h))]
        + [ANY_SPEC] * hosted.n,
        out_specs=[out_spec] * 3 + [pl.BlockSpec(memory_space=pltpu.SMEM)] + [ANY_SPEC] * hosted.n,
        out_shape=[shape] * 3 + [jax.ShapeDtypeStruct((n_seq // nb, n_pairs, n_q), jnp.int32)]
        + exchange_shapes(hosted.arrays, hosted.gather),
        scratch_shapes=exchange_sems(hosted.n),
        compiler_params=_params(("arbitrary", "arbitrary")),
    )(p, p, p, gain, *hosted.arrays)


def sb_bwd(p, gain, o_raw, tot, dy, first, n_seq, seq_len):
    n_pairs = SB_W // LANES
    n_q = seq_len // QB
    nb = min(SB_SEQS, n_seq)

    def seq_spec(c0):
        return pl.BlockSpec((nb, seq_len, LANES), functools.partial(lambda h, b, c0: (b, 0, c0 + h), c0=c0),
                            pipeline_mode=SB_BUFFERS)

    own = pl.BlockSpec((nb, seq_len, LANES), lambda h, b: (b, 0, h), pipeline_mode=SB_BUFFERS)
    par = pl.BlockSpec((1, LANES), lambda h, b: (0, h))

    def body(q_ref, k_ref, v_ref, g_ref, o_ref, tot_ref, dy_ref, first_ref, dq_ref, dk_ref, dv_ref, dg_ref):
        gain = g_ref[...]
        lo = _lane_lo((1, LANES))
        dk_ref[...] = jnp.zeros_like(dk_ref)
        dv_ref[...] = jnp.zeros_like(dv_ref)

        def q_block(i, dgain):
            qs = pl.multiple_of(i * QB, QB)
            seqs = range(nb)
            zeros = [jnp.zeros((QB, 1), F32)] * nb
            qv, dov, t_lo, t_hi = [], [], [], []
            for s in seqs:
                qv.append(sb_split_q(q_ref[s, pl.ds(qs, QB), :]))
                _, post_vjp = jax.vjp(sb_post, o_ref[s, pl.ds(qs, QB), :], gain)
                do, dg_s = post_vjp(dy_ref[s, pl.ds(qs, QB), :])
                dov.append(do)
                dgain = dgain + dg_s
                tot_s = tot_ref[s, pl.ds(qs, QB), :]
                t_lo.append(jnp.max(jnp.where(lo, tot_s, -jnp.inf), axis=1, keepdims=True))
                t_hi.append(jnp.max(jnp.where(lo, -jnp.inf, tot_s), axis=1, keepdims=True))
            add = lambda xs, ys: [x + y for x, y in zip(xs, ys)]
            sub = lambda xs, ys: [x - y for x, y in zip(xs, ys)]

            def tile(ks, carry, diag):
                dq, rem_lo, rem_hi, g_lo, g_hi = carry
                kv = [k_ref[s, pl.ds(ks, QB), :] for s in seqs]
                vv = [v_ref[s, pl.ds(ks, QB), :] for s in seqs]
                fn = functools.partial(sb_tile, diag=diag, from_here=list(zip(rem_lo, rem_hi)))
                (_, s_lo, s_hi), vjp = jax.vjp(fn, qv, kv, vv, zeros, zeros)
                dq_t, dk_t, dv_t, dc_lo, dc_hi = vjp((dov, g_lo, g_hi))
                dq_t = [jnp.where(lo, d_lo, d_hi) for d_lo, d_hi in dq_t]
                for s in seqs:
                    dk_ref[s, pl.ds(ks, QB), :] = dk_ref[s, pl.ds(ks, QB), :] + dk_t[s]
                    dv_ref[s, pl.ds(ks, QB), :] = dv_ref[s, pl.ds(ks, QB), :] + dv_t[s]
                return add(dq, dq_t), sub(rem_lo, s_lo), sub(rem_hi, s_hi), add(g_lo, dc_lo), add(g_hi, dc_hi)

            def k_block(j, carry):
                return tile(pl.multiple_of(j * QB, QB), carry, False)

            carry = ([jnp.zeros((QB, LANES), F32)] * nb, t_lo, t_hi, zeros, zeros)
            carry = lax.fori_loop(first_ref[pl.program_id(1), pl.program_id(0), i], i, k_block, carry)
            carry = tile(qs, carry, True)
            for s in seqs:
                dq_ref[s, pl.ds(qs, QB), :] = carry[0][s] * SB_SCALE
            return dgain

        dgain = lax.fori_loop(0, n_q, q_block, jnp.zeros((1, LANES), F32))
        first = pl.program_id(1) == 0

        @pl.when(first)
        def _():
            dg_ref[...] = dgain

        @pl.when(jnp.logical_not(first))
        def _():
            dg_ref[...] = dg_ref[...] + dgain

    shape = jax.ShapeDtypeStruct((n_seq, seq_len, SB_W), F32)
    return pl.pallas_call(
        body, name="sb_bwd", grid=(n_pairs, n_seq // nb),
        in_specs=[seq_spec(SB_Q0), seq_spec(SB_K0), seq_spec(SB_V0), par, own, own, seq_spec(RWKV_W // LANES),
                  pl.BlockSpec(memory_space=pltpu.SMEM)],
        out_specs=[own, own, own, par],
        out_shape=[shape, shape, shape, jax.ShapeDtypeStruct((1, SB_W), F32)],
        compiler_params=_params(("arbitrary", "arbitrary")),
    )(p, p, p, gain, o_raw, tot, dy, first)


def exchange(name, arrays, gather):
    n = len(arrays)

    def body(*refs):
        copies = exchange_copies(refs[:n], refs[n:2 * n], *refs[2 * n:], gather)
        for cp in copies:
            cp.start()
        for cp in copies:
            cp.wait()

    return pl.pallas_call(
        body, name=name, in_specs=[ANY_SPEC] * n, out_specs=[ANY_SPEC] * n, out_shape=exchange_shapes(arrays, gather),
        scratch_shapes=exchange_sems(n),
    )(*arrays)


ANY_SPEC = pl.BlockSpec(memory_space=pl.ANY)


def exchange_shapes(arrays, gather):
    return [jax.ShapeDtypeStruct(((N_DEV,) + a.shape) if gather else a.shape, a.dtype) for a in arrays]


def exchange_sems(n):
    return [pltpu.SemaphoreType.DMA((n, N_DEV - 1)), pltpu.SemaphoreType.DMA((n, N_DEV - 1)),
            pltpu.SemaphoreType.DMA((n,))]


def exchange_copies(ins, outs, send_sems, recv_sems, local_sems, gather):
    x, y, c = lax.axis_index("x"), lax.axis_index("y"), lax.axis_index("c")
    me = 4 * x + 2 * y + c
    copies = []
    for a, (src_all, dst_all) in enumerate(zip(ins, outs)):
        own = src_all if gather else src_all.at[me]
        copies.append(pltpu.make_async_copy(own, dst_all.at[me], local_sems.at[a]))
        for j in range(1, N_DEV):
            px, py, pc = (x + (j >> 2)) % 2, (y + ((j >> 1) & 1)) % 2, (c + (j & 1)) % 2
            src = src_all if gather else src_all.at[4 * px + 2 * py + pc]
            copies.append(pltpu.make_async_remote_copy(
                src_ref=src, dst_ref=dst_all.at[me], send_sem=send_sems.at[a, j - 1],
                recv_sem=recv_sems.at[a, j - 1], device_id=(px, py, pc), device_id_type=pl.DeviceIdType.MESH))
    return copies


def adamw(name, w, parts, m, v, tile):
    rows, cols = w.shape
    spec = pl.BlockSpec((tile, cols), lambda i: (i, 0))
    part_spec = pl.BlockSpec((N_DEV, tile, cols), lambda i: (0, i, 0))

    def body(w_ref, p_ref, m_ref, v_ref, g_ref, d_ref, nm_ref, nv_ref):
        g = p_ref[0].astype(F32)
        for s in range(1, N_DEV):
            g = g + p_ref[s].astype(F32)
        new_m = ADAM_B1 * m_ref[...] + (1.0 - ADAM_B1) * g
        new_v = ADAM_B2 * v_ref[...] + (1.0 - ADAM_B2) * (g * g)
        m_hat = new_m / (1.0 - ADAM_B1 ** ADAM_STEP)
        v_hat = new_v / (1.0 - ADAM_B2 ** ADAM_STEP)
        g_ref[...] = g
        d_ref[...] = -ADAM_LR * (m_hat / (jnp.sqrt(v_hat) + ADAM_EPS) + ADAM_WD * w_ref[...])
        nm_ref[...] = new_m
        nv_ref[...] = new_v

    shape = jax.ShapeDtypeStruct((rows, cols), F32)
    return pl.pallas_call(
        body, name=name, grid=(rows // tile,), in_specs=[spec, part_spec, spec, spec],
        out_specs=[spec] * 4, out_shape=[shape] * 4, compiler_params=_params(("arbitrary",)),
    )(w, parts, m, v)


SMALL = ("ln1_g", "tok_mu", "w0", "a0", "k_k", "k_a", "r_k", "gn_w", "gn_b", "sb_gain", "ln2_g", "lnf_g")
EARLY = ("w_in", "w_decay_up", "w_aaa_up", "w_gate_up")
LATE = ("w_out", "w_up", "w_down")
BIG = EARLY + LATE
COL_SHARDED = ("w_in", "w_decay_up", "w_aaa_up", "w_gate_up", "w_up")
ORDER = ("ln1_g", "w_in", "tok_mu", "w0", "w_decay_up", "a0", "w_aaa_up", "w_gate_up", "k_k", "k_a", "r_k",
         "gn_w", "gn_b", "sb_gain", "w_out", "ln2_g", "w_up", "w_down", "lnf_g")


def _pack(vectors, rows):
    flat = jnp.concatenate([v.reshape(-1).astype(F32) for v in vectors])
    return jnp.pad(flat, (0, rows * LANES - flat.shape[0])).reshape(rows, LANES)


def _full_cols(gathered):
    d, k, cols = gathered.shape
    return jnp.transpose(gathered, (1, 0, 2)).reshape(k, d * cols)


def _col_parts(full):
    k, n = full.shape
    return jnp.transpose(full.reshape(k, N_DEV, n // N_DEV), (1, 0, 2))


def kernel(x, ln1_g, w_in, tok_mu, w0, w_decay_up, a0, w_aaa_up, w_gate_up, k_k, k_a, r_k, gn_w, gn_b, sb_gain, w_out, ln2_g, w_up, w_down, lnf_g, loss_target, m_ln1_g, m_w_in, m_tok_mu, m_w0, m_w_decay_up, m_a0, m_w_aaa_up, m_w_gate_up, m_k_k, m_k_a, m_r_k, m_gn_w, m_gn_b, m_sb_gain, m_w_out, m_ln2_g, m_w_up, m_w_down, m_lnf_g, v_ln1_g, v_w_in, v_tok_mu, v_w0, v_w_decay_up, v_a0, v_w_aaa_up, v_w_gate_up, v_k_k, v_k_a, v_r_k, v_gn_w, v_gn_b, v_sb_gain, v_w_out, v_ln2_g, v_w_up, v_w_down, v_lnf_g):
    args = dict(locals())
    weights = {n: args[n] for n in ORDER}
    mom_m = {n: args["m_" + n] for n in ORDER}
    mom_v = {n: args["v_" + n] for n in ORDER}

    n_seq, seq_len, d_model = x.shape
    n_tok = n_seq * seq_len
    x2d = x.reshape(n_tok, d_model)
    tgt = loss_target.reshape(n_tok, d_model)
    row = lambda t: t.reshape(1, -1).astype(F32)

    g1 = row(ln1_g)
    shard = {n: weights[n][0].astype(BF16) for n in BIG}
    h1, *early = rw_call("norm1", lambda r, q: [f_norm(r[0], q[0])], [x2d], [g1], [(d_model, BF16)], 512,
                         hosted=HostedChipGather([shard[n] for n in EARLY]))
    gathered = dict(zip(EARLY, early))
    w_in_f = _full_cols(gathered["w_in"])
    zeros64 = jnp.zeros((HEAD_DIM, RWKV_W), BF16)
    wd_pad = jnp.concatenate([_full_cols(gathered["w_decay_up"]), zeros64], axis=0)
    wa_pad = jnp.concatenate([zeros64, _full_cols(gathered["w_aaa_up"])], axis=0)
    wg_f = _full_cols(gathered["w_gate_up"])
    in_cols = w_in_f.shape[1]

    mu, w0r, a0r = row(tok_mu), row(w0), row(a0)
    kkr, kar, rkr, gwr, gbr, sgr = row(k_k), row(k_a), row(r_k), row(gn_w), row(gn_b), row(sb_gain)
    g2, gf = row(ln2_g), row(lnf_g)

    p = matmul("proj_in", h1, w_in_f, "nn", F32, 1024, in_cols // 2, d_model)
    prep_pars = [mu, w0r, wd_pad, a0r, wa_pad, wg_f]
    (r_, kraw, v_, lw, asig, gate), prep_edges = rwkv_prep_fwd(p, prep_pars, seq_len)
    by_seq = lambda t: t.reshape(n_seq, seq_len, t.shape[-1])
    flat = lambda t: t.reshape(n_tok, t.shape[-1])
    rwkv_in = [by_seq(t) for t in (r_, kraw, v_, lw, asig)]
    z, s0_all, (w_out_g, w_up_g) = rwkv_fwd(*rwkv_in, kkr, kar, rkr, gwr, gbr, n_seq, seq_len,
                                            Hosted([shard["w_out"], shard["w_up"]], True))
    y_sb, o_raw, tot, sb_first, w_down_g = sb_fwd(by_seq(p), sgr, n_seq, seq_len, Hosted([shard["w_down"]], True))
    w_out_f = w_out_g.reshape(d_model, d_model)
    w_up_f = _full_cols(w_up_g)
    w_down_f = w_down_g.reshape(-1, d_model)
    d_ff = w_up_f.shape[1]
    z, y_sb = flat(z), flat(y_sb)
    (ycat,) = rw_call("mix_cat", lambda r, q: [jnp.concatenate([r[0] * r[1], r[2]], axis=1)],
                      [z, gate, y_sb], [], [(d_model, BF16)], 512)
    x2, h2 = matmul("proj_out", ycat, w_out_f, "nn", (F32, BF16), 512, d_model, d_model, extras=[x2d], pars=[g2],
                    epilogue=lambda acc, xv, g: (xv + acc, f_norm(xv + acc, g)))
    u, act = matmul("mlp_up", h2, w_up_f, "nn", (F32, BF16), 1024, d_ff // 4, d_model,
                    epilogue=lambda acc: (acc, jnp.square(jnp.maximum(acc, 0.0))))

    def loss_epilogue(acc, x2v, target, g):
        loss_rows, vjp = jax.vjp(lambda xv, gv: f_final(xv, gv, target), x2v + acc, g)
        dx3, dgf = vjp(jnp.ones_like(loss_rows))
        return dx3, dx3, jnp.broadcast_to(jnp.sum(loss_rows), (1, LANES)), dgf

    dx3, dx3_b, loss_acc, d_lnf = matmul(
        "mlp_down", act, w_down_f, "nn", (F32, BF16), 512, d_model, d_ff, extras=[x2, tgt], pars=[gf],
        epilogue=loss_epilogue, sums=[((1, LANES), F32), ((1, d_model), F32)])

    du = matmul("d_act", dx3_b, w_down_f, "nt", BF16, 1024, d_ff // 4, d_model, extras=[u],
                epilogue=lambda acc, uv: (acc * (2.0 * jnp.maximum(uv, 0.0)),))
    dw_down = matmul("dw_down", act, dx3_b, "tn", BF16, 512, d_model, 4096)
    dw_up = matmul("dw_up", h2, du, "tn", BF16, d_model, 512, 4096)

    def norm_bwd(acc, xv, dres, g):
        _, vjp = jax.vjp(f_norm, xv, g)
        dx, dg = vjp(acc)
        return dx + dres, dx + dres, dg

    dx2, dx2_b, d_ln2 = matmul("d_h2", du, w_up_f, "nt", (F32, BF16), 512, d_model, d_ff, extras=[x2, dx3], pars=[g2],
                               epilogue=norm_bwd, sums=[((1, d_model), F32)])

    dycat = matmul("d_ycat", dx2_b, w_out_f, "nt", F32, 512, d_model, d_model)
    dw_out = matmul("dw_out", ycat, dx2_b, "tn", BF16, d_model, d_model, 2048)
    dq, dk_sb, dv_sb, d_sg = sb_bwd(by_seq(p), sgr, o_raw, tot, by_seq(dycat), sb_first, n_seq, seq_len)
    d_sb = [flat(dq), flat(dk_sb), flat(dv_sb)]
    late_grads = {"w_out": dw_out.reshape(N_DEV, -1, d_model), "w_up": _col_parts(dw_up),
                  "w_down": dw_down.reshape(N_DEV, -1, d_model)}
    (dr, dkraw, dv, dlw, dasig, d_kk, d_ka, d_rk, d_gw, d_gb), late_parts = rwkv_bwd(
        *rwkv_in, kkr, kar, rkr, gwr, gbr, s0_all, by_seq(dycat), by_seq(gate), n_seq, seq_len,
        Hosted([late_grads[n] for n in LATE], False))
    prep_cots = [flat(t) for t in (dr, dkraw, dv, dlw, dasig)] + [dycat, z]
    dp, (d_mu, d_w0, d_wd, d_a0, d_wa, d_wg) = rwkv_prep_bwd(p, prep_edges, prep_pars, prep_cots, d_sb, seq_len)

    dw_in = matmul("dw_in", h1, dp, "tn", BF16, d_model, in_cols // 2, 2048)
    lora_parts = lambda t: _col_parts(t).astype(BF16)
    early_grads = {"w_in": _col_parts(dw_in), "w_decay_up": lora_parts(d_wd[:HEAD_DIM]),
                   "w_aaa_up": lora_parts(d_wa[HEAD_DIM:]), "w_gate_up": lora_parts(d_wg)}
    dx, d_ln1, *early_parts = matmul(
        "d_h1", dp, w_in_f, "nt", (F32,), 512, d_model, in_cols, extras=[x2d, dx2], pars=[g1],
        epilogue=lambda *t: norm_bwd(*t)[1:], sums=[((1, d_model), F32)],
        hosted=Hosted([early_grads[n] for n in EARLY], False))
    parts = dict(zip(EARLY, early_parts))
    parts.update(zip(LATE, late_parts))

    small_grads = {"ln1_g": d_ln1, "tok_mu": d_mu, "w0": d_w0, "a0": d_a0, "k_k": d_kk, "k_a": d_ka, "r_k": d_rk,
                   "gn_w": d_gw, "gn_b": d_gb, "sb_gain": d_sg, "ln2_g": d_ln2, "lnf_g": d_lnf}
    n_small = sum(int(weights[n].size) for n in SMALL)
    pack_rows = -(-(n_small + 1) // (8 * LANES)) * 8
    packed = _pack([small_grads[n] for n in SMALL] + [loss_acc[0, :1]], pack_rows)
    (small_parts,) = exchange("gather_small", [packed], True)

    results = {}
    for n in BIG:
        w2d = weights[n][0]
        tile = w2d.shape[0] if w2d.shape[0] <= 256 else 256
        results[n] = adamw("adamw_" + n, w2d, parts[n], mom_m[n][0], mom_v[n][0], tile)
    pk = lambda d: _pack([d[n] for n in SMALL] + [jnp.zeros((1,), F32)], pack_rows)
    sg, sd, sm, sv = adamw("adamw_small", pk(weights), small_parts, pk(mom_m), pk(mom_v), pack_rows)
    off = 0
    for n in SMALL:
        size = int(weights[n].size)
        results[n] = tuple(t.reshape(-1)[off:off + size] for t in (sg, sd, sm, sv))
        off += size
    loss = sg.reshape(-1)[off]

    out = [loss, dx.reshape(x.shape)]
    for kind in range(4):
        out += [results[n][kind].reshape(weights[n].shape) for n in ORDER]
    return tuple(out)
```

```python
import functools
import math

import jax
import jax.numpy as jnp
from jax import lax
from jax.experimental import pallas as pl
from jax.experimental.pallas import tpu as pltpu

F32 = jnp.float32
BF16 = jnp.bfloat16

N_DEV = 8
HEAD_DIM = 64
LANES = 128
RWKV_W = 512
SB_W = 512
LORA_WA = 128
GATE_LORA = 128
RWKV_COLS = 3 * RWKV_W + LORA_WA + GATE_LORA
RMS_EPS = 1e-5
GN_EPS = 64e-5
CHUNK = 64
QB = 256
SB_SCALE = HEAD_DIM ** -0.5
ADAM_LR, ADAM_B1, ADAM_B2, ADAM_EPS, ADAM_WD, ADAM_STEP = 0.001, 0.9, 0.999, 1e-08, 0.01, 10
VMEM_LIMIT = 56 * 1024 * 1024


_DIMS = {
    "nn": (((1,), (0,)), ((), ())),
    "nt": (((1,), (1,)), ((), ())),
    "tn": (((0,), (0,)), ((), ())),
}


def _pieces(x, n):
    if n == 1:
        return [x.astype(BF16)]
    out, rem = [], x.astype(F32)
    for i in range(n):
        p = rem.astype(BF16)
        out.append(p)
        if i + 1 < n:
            rem = rem - p.astype(F32)
    return out


def _dot(a, b, form, pa, pb):
    pieces_a, pieces_b = _pieces(a, pa), _pieces(b, pb)
    keep = max(pa, pb)
    acc = None
    for i, ai in enumerate(pieces_a):
        for j, bj in enumerate(pieces_b):
            if i + j >= keep:
                continue
            t = lax.dot_general(ai, bj, _DIMS[form], preferred_element_type=F32)
            acc = t if acc is None else acc + t
    return acc


BOTH = (True, True)


@functools.partial(jax.custom_vjp, nondiff_argnums=(2, 3, 4, 5))
def mm(a, b, form, pa, pb, diff=BOTH):
    return _dot(a, b, form, pa, pb)


def _mm_fwd(a, b, form, pa, pb, diff):
    return _dot(a, b, form, pa, pb), (a, b)


def _mm_bwd(form, pa, pb, diff, res, g):
    a, b = res
    pg = max(pa, pb)
    da, db = jnp.zeros_like(a), jnp.zeros_like(b)
    if form == "nn":
        if diff[0]:
            da = mm(g, b, "nt", pg, pb)
        if diff[1]:
            db = mm(a, g, "tn", pa, pg)
    elif form == "nt":
        if diff[0]:
            da = mm(g, b, "nn", pg, pb)
        if diff[1]:
            db = mm(g, a, "tn", pg, pa)
    else:
        if diff[0]:
            da = mm(b, g, "nt", pb, pg)
        if diff[1]:
            db = mm(a, g, "nn", pa, pg)
    return da, db


mm.defvjp(_mm_fwd, _mm_bwd)


def _stack_rows(top, bottom):
    return jnp.concatenate([top, bottom], axis=0)


@jax.custom_vjp
def _split_rows(x):
    half = x.shape[0] // 2
    return x[:half], x[half:]


def _split_rows_fwd(x):
    return _split_rows(x), None


def _split_rows_bwd(_, grads):
    return (_stack_rows(*grads),)


_split_rows.defvjp(_split_rows_fwd, _split_rows_bwd)


def _lane_lo(shape):
    return lax.broadcasted_iota(jnp.int32, shape, len(shape) - 1) < HEAD_DIM


def _segsum(x):
    lo = _lane_lo(x.shape)
    s_lo = jnp.sum(jnp.where(lo, x, 0.0), axis=-1, keepdims=True)
    s_hi = jnp.sum(jnp.where(lo, 0.0, x), axis=-1, keepdims=True)
    return jnp.where(lo, s_lo, s_hi)


def _sigmoid(x):
    return 0.5 * (jnp.tanh(0.5 * x) + 1.0)


@jax.custom_vjp
def _log_one_minus_sigmoid(z):
    return -jnp.maximum(z, 0.0) - jnp.log(1.0 + jnp.exp(-jnp.abs(z)))


def _log_oms_fwd(z):
    out = _log_one_minus_sigmoid(z)
    return out, (z, out)


def _log_oms_bwd(res, g):
    z, out = res
    return (-g * jnp.exp(z + out),)


_log_one_minus_sigmoid.defvjp(_log_oms_fwd, _log_oms_bwd)


def f_norm(x, g):
    return x * lax.rsqrt(jnp.mean(x * x, axis=-1, keepdims=True) + RMS_EPS) * g


def f_prep(p, pprev, mu, w0, wd_pad, a0, wa_pad, wg):
    pm = p + mu * (pprev - p)
    r = pm[:, 0:RWKV_W]
    k = pm[:, RWKV_W:2 * RWKV_W]
    v = pm[:, 2 * RWKV_W:3 * RWKV_W]
    xwa = pm[:, 3 * RWKV_W:3 * RWKV_W + LORA_WA]
    xg = pm[:, 3 * RWKV_W + LORA_WA:RWKV_COLS]
    w = _log_one_minus_sigmoid(-(w0 + mm(jnp.tanh(xwa), wd_pad, "nn", 1, 1))) - 0.5
    lw = -jnp.exp(w)
    asig = _sigmoid(a0 + mm(xwa, wa_pad, "nn", 1, 1))
    gate = mm(_sigmoid(xg), wg, "nn", 1, 1)
    return r, k, v, lw, asig, gate


def _tri(n, kind):
    row = lax.broadcasted_iota(jnp.int32, (n, n), 0)
    col = lax.broadcasted_iota(jnp.int32, (n, n), 1)
    if kind == "lower_incl":
        return row >= col
    return row > col


@functools.partial(jax.custom_vjp, nondiff_argnums=(1,))
def _nilpotent_inverses(mats, hp):
    size = mats[0].shape[0]
    eye = (lax.broadcasted_iota(jnp.int32, (size, size), 0) == lax.broadcasted_iota(jnp.int32, (size, size), 1))
    tinv = [eye.astype(F32) + x for x in mats]
    pw = [mm(x, x, "nn", hp, hp) for x in mats]
    for _ in range(int(math.log2(size)) - 2):
        both = [_split_rows(mm(_stack_rows(t, x), x, "nn", hp, hp)) for t, x in zip(tinv, pw)]
        tinv = [t + tx for t, (tx, _) in zip(tinv, both)]
        pw = [xx for _, xx in both]
    return [t + mm(t, x, "nn", hp, hp) for t, x in zip(tinv, pw)]


def _nilpotent_inverses_fwd(mats, hp):
    tinv = _nilpotent_inverses(mats, hp)
    return tinv, tinv


def _nilpotent_inverses_bwd(hp, tinv, grads):
    right = [mm(g, t, "nt", hp, hp) for g, t in zip(grads, tinv)]
    return ([mm(t, x, "tn", hp, hp) for t, x in zip(tinv, right)],)


_nilpotent_inverses.defvjp(_nilpotent_inverses_fwd, _nilpotent_inverses_bwd)


def rwkv_chunk(state, r, kraw, v, lw, asig, k_k, k_a, r_k, gn_w, gn_b, hp):
    n = len(r)
    L = r[0].shape[0]
    lo = _lane_lo((1, LANES))
    masks = (lo, jnp.logical_not(lo))
    incl = _tri(L, "lower_incl")
    strict = _tri(L, "strict")
    tri = incl.astype(F32)
    kk = [x * w for x, w in zip(kraw, k_k)]
    kk = [x / jnp.maximum(jnp.sqrt(_segsum(x * x)), 1e-12) for x in kk]
    k = [x * (1.0 + (s - 1.0) * w) for x, s, w in zip(kraw, asig, k_a)]
    b = [x * s for x, s in zip(kk, asig)]
    c = [mm(tri, x, "nn", 1, 3, (False, True)) for x in lw]
    at = [-x * jnp.exp(ci - li) for x, ci, li in zip(kk, c, lw)]
    rt = [x * jnp.exp(ci) for x, ci in zip(r, c)]
    einv = [jnp.exp(-ci) for ci in c]
    bt = [x * e for x, e in zip(b, einv)]
    kt = [x * e for x, e in zip(k, einv)]
    inst = [(s, m) for s in range(n) for m in masks]
    ar_h = [_stack_rows(jnp.where(m, at[s], 0.0), jnp.where(m, rt[s], 0.0)) for s, m in inst]
    on_b = [_split_rows(mm(x, bt[s], "nt", hp, hp)) for x, (s, _) in zip(ar_h, inst)]
    on_k = [_split_rows(mm(x, kt[s], "nt", hp, hp)) for x, (s, _) in zip(ar_h, inst)]
    a_ab = [jnp.where(strict, x, 0.0) for x, _ in on_b]
    b_rb = [jnp.where(incl, x, 0.0) for _, x in on_b]
    a_ak = [jnp.where(strict, x, 0.0) for x, _ in on_k]
    b_rk = [jnp.where(incl, x, 0.0) for _, x in on_k]
    tinv = _nilpotent_inverses(a_ab, hp)
    on_state = [_split_rows(mm(x, state[s], "nt", hp, hp)) for x, (s, _) in zip(ar_h, inst)]
    on_v = [_split_rows(mm(_stack_rows(m1, m2), v[s], "nn", hp, hp)) for m1, m2, (s, _) in zip(a_ak, b_rk, inst)]
    u_h = [mm(t, sa + av, "nn", hp, hp) for t, (sa, _), (av, _) in zip(tinv, on_state, on_v)]
    y_h = [sr + mm(m1, u, "nn", hp, hp) + bv for (_, sr), m1, u, (_, bv) in zip(on_state, b_rb, u_h, on_v)]
    u_all = [jnp.where(lo, u_h[2 * s], u_h[2 * s + 1]) for s in range(n)]
    y_all = [jnp.where(lo, y_h[2 * s], y_h[2 * s + 1]) for s in range(n)]
    c_last = [jnp.sum(x, axis=0, keepdims=True) for x in lw]
    efwd = [jnp.exp(cl - ci) for cl, ci in zip(c_last, c)]
    new_state = [st * jnp.exp(cl) + mm(_stack_rows(u, vi), _stack_rows(bi * e, ki * e), "tn", hp, hp)
                 for st, cl, u, bi, e, vi, ki in zip(state, c_last, u_all, b, efwd, v, k)]
    row_head = lax.broadcasted_iota(jnp.int32, (LANES, LANES), 0) // HEAD_DIM
    col_head = lax.broadcasted_iota(jnp.int32, (LANES, LANES), 1) // HEAD_DIM
    new_state = [jnp.where(row_head == col_head, x, 0.0) for x in new_state]
    outs = []
    for y, ri, ki, vi, w_rk, w_gw, w_gb in zip(y_all, r, k, v, r_k, gn_w, gn_b):
        mean = _segsum(y) * (1.0 / HEAD_DIM)
        d = y - mean
        var = _segsum(d * d) * (1.0 / HEAD_DIM)
        yn = d * lax.rsqrt(var + GN_EPS) * w_gw + w_gb
        outs.append(yn + _segsum(ri * ki * w_rk) * vi)
    return outs, new_state


def sb_tile(q, k, v, c_lo, c_hi, diag, from_here=None):
    n = len(q)
    lo = _lane_lo((1, LANES))
    below = _tri(QB, "strict")
    from_s = _tri(QB, "lower_incl").astype(F32)
    inst = [(s, h) for s in range(n) for h in (0, 1)]
    carry = [(c_lo[s], c_hi[s])[h] for s, h in inst]
    z = [mm(q[s][h], k[s], "nt", 1, 1) for s, h in inst]
    log_keep = [_log_one_minus_sigmoid(x) for x in z]
    if diag:
        log_keep = [jnp.where(below, x, 0.0) for x in log_keep]
    own = [jnp.sum(x, axis=1, keepdims=True) for x in log_keep]
    if from_here is not None:
        carry = [lax.stop_gradient(from_here[s][h] - o) + cr for (s, h), o, cr in zip(inst, own, carry)]
    tail = [mm(x, from_s, "nn", SB_SUM_PIECES, 1, (True, False)) for x in log_keep]
    log_a = [x + tl + cr for x, tl, cr in zip(z, tail, carry)]
    if diag:
        log_a = [jnp.where(below, x, -1e30) for x in log_a]
    att = [jnp.exp(x) for x in log_a]
    out_h = [mm(x, v[s], "nn", 1, 1) for x, (s, _) in zip(att, inst)]
    out = [jnp.where(lo, out_h[2 * s], out_h[2 * s + 1]) for s in range(n)]
    return out, [own[2 * s] for s in range(n)], [own[2 * s + 1] for s in range(n)]


def sb_split_q(q):
    lo = _lane_lo((1, LANES))
    qs = q * SB_SCALE
    return jnp.where(lo, qs, 0.0), jnp.where(lo, 0.0, qs)


def sb_post(o, gain):
    return o * lax.rsqrt(_segsum(o * o) * (1.0 / HEAD_DIM) + RMS_EPS) * gain


def f_final(x3, g, target):
    y = f_norm(x3, g)
    err = y - target
    return 0.5 * jnp.mean(err * err, axis=-1, keepdims=True)


def _params(sem):
    return pltpu.CompilerParams(dimension_semantics=sem, vmem_limit_bytes=VMEM_LIMIT)


def rw_call(name, body_fn, rows, pars, out_rows, tile, hosted=None):
    n_rows = rows[0].shape[0]
    tile = min(tile, n_rows)
    steps = n_rows // tile
    row_specs = [pl.BlockSpec((tile, arr.shape[1]), lambda i: (i, 0)) for arr in rows]
    par_specs = [pl.BlockSpec(p.shape, lambda i: (0, 0)) for p in pars]
    nr, npar, nor = len(rows), len(pars), len(out_rows)
    n_host = hosted.n if hosted is not None else 0

    def body(*refs):
        if hosted is not None:
            refs, copies = hosted.split(refs, nr + npar, nor)
            hosted.run(copies, pl.program_id(0) == 0, pl.program_id(0) == steps - 1)
        row_outs = body_fn([r[...] for r in refs[:nr]], [r[...] for r in refs[nr:nr + npar]])
        for ref, val in zip(refs[nr + npar:], row_outs):
            ref[...] = val.astype(ref.dtype)

    outs = pl.pallas_call(
        body, name=name, grid=(steps,), in_specs=row_specs + par_specs + [ANY_SPEC] * n_host,
        out_specs=[pl.BlockSpec((tile, c), lambda i: (i, 0)) for c, _ in out_rows] + [ANY_SPEC] * n_host,
        out_shape=[jax.ShapeDtypeStruct((n_rows, c), dt) for c, dt in out_rows]
        + (exchange_shapes(hosted.arrays, hosted.gather) if n_host else []),
        scratch_shapes=exchange_sems(n_host) if n_host else [],
        compiler_params=_params(("arbitrary",)),
    )(*rows, *pars, *(hosted.arrays if n_host else []))
    return outs


def matmul(name, a, b, form, out_dtype, tm, tn, tk, extras=(), pars=(), epilogue=None, sums=(), hosted=None):
    out_dtypes = out_dtype if isinstance(out_dtype, tuple) else (out_dtype,)
    tm, tn, tk = min(tm, a.shape[1 if form == "tn" else 0]), min(tn, b.shape[0 if form == "nt" else 1]), min(tk, a.shape[0 if form == "tn" else 1])
    if form == "nn":
        (m, kd), n = a.shape, b.shape[1]
        a_spec = pl.BlockSpec((tm, tk), lambda i, j, k: (i, k))
        b_spec = pl.BlockSpec((tk, tn), lambda i, j, k: (k, j))
    elif form == "nt":
        (m, kd), n = a.shape, b.shape[0]
        a_spec = pl.BlockSpec((tm, tk), lambda i, j, k: (i, k))
        b_spec = pl.BlockSpec((tn, tk), lambda i, j, k: (j, k))
    else:
        (kd, m), n = a.shape, b.shape[1]
        a_spec = pl.BlockSpec((tk, tm), lambda i, j, k: (k, i))
        b_spec = pl.BlockSpec((tk, tn), lambda i, j, k: (k, j))
    ksteps = kd // tk

    n_extra, n_par, n_out, n_sum = len(extras), len(pars), len(out_dtypes), len(sums)
    n_in = 2 + n_extra + n_par
    grid = (m // tm, n // tn, ksteps)

    def body(*refs):
        if hosted is not None:
            refs, copies = hosted.split(refs, n_in, n_out + n_sum)
            here = [pl.program_id(d) for d in range(3)]
            hosted.run(copies, functools.reduce(jnp.logical_and, [h == 0 for h in here]),
                       functools.reduce(jnp.logical_and, [h == g - 1 for h, g in zip(here, grid)]))
        a_ref, b_ref, rest = refs[0], refs[1], refs[2:]
        e_refs, o_refs = rest[:n_extra + n_par], rest[n_extra + n_par:n_extra + n_par + n_out]
        s_refs = rest[n_extra + n_par + n_out:n_extra + n_par + n_out + n_sum]
        kstep = pl.program_id(2)
        part = lax.dot_general(a_ref[...].astype(BF16), b_ref[...].astype(BF16), _DIMS[form],
                               preferred_element_type=F32)

        def finish(acc):
            outs = epilogue(acc, *[r[...] for r in e_refs]) if epilogue else (acc,)
            for ref, val in zip(o_refs, outs[:n_out]):
                ref[...] = val.astype(ref.dtype)
            if n_sum:
                first_tile = jnp.logical_and(pl.program_id(0) == 0, pl.program_id(1) == 0)

                @pl.when(first_tile)
                def _():
                    for ref, val in zip(s_refs, outs[n_out:]):
                        ref[...] = val.astype(ref.dtype)

                @pl.when(jnp.logical_not(first_tile))
                def _():
                    for ref, val in zip(s_refs, outs[n_out:]):
                        ref[...] = ref[...] + val.astype(ref.dtype)

        if ksteps == 1:
            finish(part)
            return
        acc_ref = rest[n_extra + n_par + n_out + n_sum]

        @pl.when(kstep == 0)
        def _():
            acc_ref[...] = part

        @pl.when(jnp.logical_and(kstep > 0, kstep < ksteps - 1))
        def _():
            acc_ref[...] = acc_ref[...] + part

        @pl.when(kstep == ksteps - 1)
        def _():
            finish(acc_ref[...] + part)

    out_spec = pl.BlockSpec((tm, tn), lambda i, j, k: (i, j))
    whole = lambda shape: pl.BlockSpec(shape, lambda i, j, k: (0,) * len(shape))
    n_host = hosted.n if hosted is not None else 0
    host_in = list(hosted.arrays) if hosted is not None else []
    host_out = exchange_shapes(hosted.arrays, hosted.gather) if hosted is not None else []
    outs = pl.pallas_call(
        body, name=name, grid=grid,
        in_specs=[a_spec, b_spec] + [out_spec] * n_extra + [whole(t.shape) for t in pars] + [ANY_SPEC] * n_host,
        out_specs=[out_spec] * n_out + [whole(s) for s, _ in sums] + [ANY_SPEC] * n_host,
        out_shape=[jax.ShapeDtypeStruct((m, n), dt) for dt in out_dtypes]
        + [jax.ShapeDtypeStruct(s, dt) for s, dt in sums] + host_out,
        scratch_shapes=([pltpu.VMEM((tm, tn), F32)] if ksteps > 1 else []) + (exchange_sems(n_host) if n_host else []),
        compiler_params=_params(("arbitrary",) * 3 if (sums or n_host) else ("parallel", "parallel", "arbitrary")),
    )(a, b, *extras, *pars, *host_in)
    return outs if (isinstance(out_dtype, tuple) or sums or n_host) else outs[0]


PREP_TILE = 256
PREP_TILE_BWD = 128
SUBLANES = 8


def _shift_in(rows, first):
    rolled = pltpu.roll(rows, 1, 0)
    row = lax.broadcasted_iota(jnp.int32, (SUBLANES, rows.shape[1]), 0)
    head = jnp.where(row == 0, first, rolled[0:SUBLANES])
    return jnp.concatenate([head, rolled[SUBLANES:]], axis=0), rolled


def rwkv_prep_fwd(p, pars, seq_len):
    n_tok = p.shape[0]
    tile = min(PREP_TILE, seq_len)
    tile_b = min(PREP_TILE_BWD, tile)
    steps, per_seq, sub = n_tok // tile, seq_len // tile, tile // tile_b
    n_par = len(pars)

    def body(p_ref, *rest):
        par_refs, out_refs, edge_ref, last8 = rest[:n_par], rest[n_par:n_par + 6], rest[n_par + 6], rest[n_par + 7]
        step = pl.program_id(0)

        @pl.when(step == 0)
        def _():
            last8[...] = jnp.zeros_like(last8)

        rows = p_ref[...]
        before = jnp.where(step % per_seq == 0, 0.0, pltpu.roll(last8[...], 1, 0))
        prev, rolled = _shift_in(rows, before)
        edge_ref[0] = prev[0:SUBLANES]
        for m in range(1, sub):
            edge_ref[m] = rolled[m * tile_b:m * tile_b + SUBLANES]
        last8[...] = rows[tile - SUBLANES:tile]
        for ref, val in zip(out_refs, f_prep(rows, prev, *[r[...] for r in par_refs])):
            ref[...] = val

    row_out = pl.BlockSpec((tile, RWKV_W), lambda i: (i, 0))
    outs = pl.pallas_call(
        body, name="rwkv_prep", grid=(steps,),
        in_specs=[pl.BlockSpec((tile, RWKV_COLS), lambda i: (i, 0))] + [pl.BlockSpec(t.shape, lambda i: (0, 0)) for t in pars],
        out_specs=[row_out] * 6 + [pl.BlockSpec((sub, SUBLANES, RWKV_COLS), lambda i: (i, 0, 0))],
        out_shape=[jax.ShapeDtypeStruct((n_tok, RWKV_W), F32)] * 6
        + [jax.ShapeDtypeStruct((steps * sub, SUBLANES, RWKV_COLS), F32)],
        scratch_shapes=[pltpu.VMEM((SUBLANES, RWKV_COLS), F32)],
        compiler_params=_params(("arbitrary",)),
    )(p, *pars)
    return outs[:6], outs[6]


def rwkv_prep_bwd(p, edges, pars, cots, d_sb, seq_len):
    n_tok = p.shape[0]
    tile = min(PREP_TILE_BWD, seq_len)
    steps, per_seq = n_tok // tile, seq_len // tile
    n_par = len(pars)
    back = lambda i: steps - 1 - i

    def body(p_ref, edge_ref, *rest):
        par_refs, rest = rest[:n_par], rest[n_par:]
        cot_refs, dy_ref, z_ref, sb_refs = rest[:5], rest[5], rest[6], rest[7:10]
        dp_ref, acc_refs, next8 = rest[10], rest[11:11 + n_par], rest[11 + n_par]
        step = pl.program_id(0)
        first = step == 0

        @pl.when(first)
        def _():
            next8[...] = jnp.zeros_like(next8)

        rows = p_ref[...]
        prev, _ = _shift_in(rows, edge_ref[0])
        _, vjp = jax.vjp(f_prep, rows, prev, *[r[...].astype(F32) for r in par_refs])
        grads = vjp(tuple(r[...] for r in cot_refs) + (dy_ref[...] * z_ref[...],))
        d_rows, d_prev = grads[0], grads[1]
        up = pltpu.roll(d_prev, tile - 1, 0)
        ends_seq = back(step) % per_seq == per_seq - 1
        after = jnp.where(ends_seq, 0.0, pltpu.roll(next8[...], SUBLANES - 1, 0))
        row = lax.broadcasted_iota(jnp.int32, (SUBLANES, RWKV_COLS), 0)
        tail = jnp.where(row == SUBLANES - 1, after, up[tile - SUBLANES:tile])
        d_rows = d_rows + jnp.concatenate([up[:tile - SUBLANES], tail], axis=0)
        next8[...] = d_prev[0:SUBLANES]
        dp_ref[...] = jnp.concatenate([d_rows] + [r[...] for r in sb_refs], axis=1).astype(dp_ref.dtype)

        @pl.when(first)
        def _():
            for ref, val in zip(acc_refs, grads[2:]):
                ref[...] = val

        @pl.when(jnp.logical_not(first))
        def _():
            for ref, val in zip(acc_refs, grads[2:]):
                ref[...] = ref[...] + val

    cols = RWKV_COLS + sum(t.shape[1] for t in d_sb)
    half = pl.BlockSpec((tile, RWKV_W), lambda i: (back(i), 0))
    par_specs = [pl.BlockSpec(t.shape, lambda i: (0, 0)) for t in pars]
    outs = pl.pallas_call(
        body, name="d_rwkv_prep", grid=(steps,),
        in_specs=[pl.BlockSpec((tile, RWKV_COLS), lambda i: (back(i), 0)),
                  pl.BlockSpec((1, SUBLANES, RWKV_COLS), lambda i: (back(i), 0, 0))] + par_specs + [half] * 10,
        out_specs=[pl.BlockSpec((tile, cols), lambda i: (back(i), 0))] + par_specs,
        out_shape=[jax.ShapeDtypeStruct((n_tok, cols), BF16)] + [jax.ShapeDtypeStruct(t.shape, F32) for t in pars],
        scratch_shapes=[pltpu.VMEM((SUBLANES, RWKV_COLS), F32)],
        compiler_params=_params(("arbitrary",)),
    )(p, edges, *pars, *cots, *d_sb)
    return outs[0], outs[1:]


RWKV_HP = 1


RWKV_PAIRS = 4


def _rwkv_specs(n_seq, chunk_of):
    width = RWKV_PAIRS * LANES
    row = pl.BlockSpec((n_seq, CHUNK, width), lambda g, c: (0, chunk_of(c), g))
    par = pl.BlockSpec((1, width), lambda g, c: (0, g))
    s0 = pl.BlockSpec((1, 1, RWKV_PAIRS * n_seq, LANES, LANES), lambda g, c: (g, chunk_of(c), 0, 0, 0))
    return row, par, s0


class Hosted:
    def __init__(self, arrays, gather):
        self.arrays, self.gather, self.n = list(arrays), gather, len(arrays)

    def split(self, refs, n_in, n_out):
        n = self.n
        ins, outs, sems = refs[n_in:n_in + n], refs[n_in + n + n_out:n_in + 2 * n + n_out], refs[-3:]
        own = refs[:n_in] + refs[n_in + n:n_in + n + n_out] + refs[n_in + 2 * n + n_out:-3]
        return own, exchange_copies(ins, outs, *sems, self.gather)

    def run(self, copies, first, last):
        @pl.when(first)
        def _():
            for cp in copies:
                cp.start()

        @pl.when(last)
        def _():
            for cp in copies:
                cp.wait()


class HostedChipGather(Hosted):
    def __init__(self, arrays):
        super().__init__(arrays, True)

    def split(self, refs, n_in, n_out):
        n = self.n
        ins, outs, sems = refs[n_in:n_in + n], refs[n_in + n + n_out:n_in + 2 * n + n_out], refs[-3:]
        own = refs[:n_in] + refs[n_in + n:n_in + n + n_out] + refs[n_in + 2 * n + n_out:-3]
        return own, (ins, outs, sems)

    def run(self, state, first, last):
        ins, outs, (send_sems, recv_sems, local_sems) = state
        x, y, c = lax.axis_index("x"), lax.axis_index("y"), lax.axis_index("c")
        chips = [((x + 1) % 2, y), (x, (y + 1) % 2), ((x + 1) % 2, (y + 1) % 2)]

        def block(a, k, of, to, src=None):
            dst = outs[a].at[4 * of[0] + 2 * of[1] + of[2]]
            return pltpu.make_async_remote_copy(
                src_ref=dst if src is None else src, dst_ref=dst, send_sem=send_sems.at[a, k],
                recv_sem=recv_sems.at[a, k], device_id=to, device_id_type=pl.DeviceIdType.MESH)

        me, sibling = (x, y, c), (x, y, 1 - c)
        local = [pltpu.make_async_copy(ins[a], outs[a].at[4 * x + 2 * y + c], local_sems.at[a]) for a in range(self.n)]
        mine = [block(a, 0, me, sibling, ins[a]) for a in range(self.n)]
        mine += [block(a, 1 + j, me, (*chip, c), ins[a]) for a in range(self.n) for j, chip in enumerate(chips)]
        passed = [block(a, 4 + j, (*chip, c), sibling) for a in range(self.n) for j, chip in enumerate(chips)]

        @pl.when(first)
        def _():
            for cp in local + mine:
                cp.start()

        @pl.when(last)
        def _():
            for a in range(self.n):
                for j, chip in enumerate(chips):
                    block(a, 1 + j, (*chip, c), me).wait_recv()
                    passed[a * len(chips) + j].start()
            for a in range(self.n):
                block(a, 0, sibling, me).wait_recv()
                for j, chip in enumerate(chips):
                    block(a, 4 + j, (*chip, 1 - c), me).wait_recv()
            for cp in mine + passed:
                cp.wait_send()
            for cp in local:
                cp.wait()


def rwkv_fwd(r, kraw, v, lw, asig, k_k, k_a, r_k, gn_w, gn_b, n_seq, seq_len, hosted):
    n_chunks = seq_len // CHUNK
    n_groups = RWKV_W // (RWKV_PAIRS * LANES)
    n_inst = RWKV_PAIRS * n_seq
    row, par, s0_spec = _rwkv_specs(n_seq, lambda c: c)
    inst = [(s, pl.ds(pp * LANES, LANES)) for pp in range(RWKV_PAIRS) for s in range(n_seq)]

    def body(*refs):
        own, copies = hosted.split(refs, 10, 2)
        row_refs, par_refs, (z_ref, s0_ref, state) = own[:5], own[5:10], own[10:]
        step = pl.program_id(0) * n_chunks + pl.program_id(1)
        hosted.run(copies, step == 0, step == n_groups * n_chunks - 1)

        @pl.when(pl.program_id(1) == 0)
        def _():
            state[...] = jnp.zeros_like(state)

        s0 = [state[i] for i in range(n_inst)]
        rows = [[ref[s, :, lanes] for s, lanes in inst] for ref in row_refs]
        pars = [[ref[:, lanes] for _, lanes in inst] for ref in par_refs]
        z, s1 = rwkv_chunk(s0, *rows, *pars, RWKV_HP)
        for i, (s, lanes) in enumerate(inst):
            s0_ref[0, 0, i] = s0[i]
            z_ref[s, :, lanes] = z[i]
            state[i] = s1[i]

    outs = pl.pallas_call(
        body, name="rwkv_fwd", grid=(n_groups, n_chunks),
        in_specs=[row] * 5 + [par] * 5 + [ANY_SPEC] * hosted.n, out_specs=[row, s0_spec] + [ANY_SPEC] * hosted.n,
        out_shape=[jax.ShapeDtypeStruct(r.shape, F32),
                   jax.ShapeDtypeStruct((n_groups, n_chunks, n_inst, LANES, LANES), F32)]
        + exchange_shapes(hosted.arrays, hosted.gather),
        scratch_shapes=[pltpu.VMEM((n_inst, LANES, LANES), F32)] + exchange_sems(hosted.n),
        compiler_params=_params(("arbitrary", "arbitrary")),
    )(r, kraw, v, lw, asig, k_k, k_a, r_k, gn_w, gn_b, *hosted.arrays)
    return outs[0], outs[1], outs[2:]


def rwkv_bwd(r, kraw, v, lw, asig, k_k, k_a, r_k, gn_w, gn_b, s0_all, dy, gate, n_seq, seq_len, hosted):
    n_chunks = seq_len // CHUNK
    n_groups = RWKV_W // (RWKV_PAIRS * LANES)
    n_inst = RWKV_PAIRS * n_seq
    row, par, s0_spec = _rwkv_specs(n_seq, lambda c: n_chunks - 1 - c)
    inst = [(s, pl.ds(pp * LANES, LANES)) for pp in range(RWKV_PAIRS) for s in range(n_seq)]

    def body(*refs):
        own, copies = hosted.split(refs, 13, 10)
        row_refs, par_refs, s0_ref, dy_ref, gate_ref = own[:5], own[5:10], own[10], own[11], own[12]
        drow_refs, dpar_refs, dstate = own[13:18], own[18:23], own[23]
        step = pl.program_id(0) * n_chunks + pl.program_id(1)
        hosted.run(copies, step == 0, step == n_groups * n_chunks - 1)
        first = pl.program_id(1) == 0

        @pl.when(first)
        def _():
            dstate[...] = jnp.zeros_like(dstate)

        fn = functools.partial(rwkv_chunk, hp=RWKV_HP)
        rows = [[ref[s, :, lanes] for s, lanes in inst] for ref in row_refs]
        pars = [[ref[:, lanes] for _, lanes in inst] for ref in par_refs]
        _, vjp = jax.vjp(fn, [s0_ref[0, 0, i] for i in range(n_inst)], *rows, *pars)
        dz = [dy_ref[s, :, lanes] * gate_ref[s, :, lanes] for s, lanes in inst]
        grads = vjp((dz, [dstate[i] for i in range(n_inst)]))
        for i, (s, lanes) in enumerate(inst):
            dstate[i] = grads[0][i]
            for ref, val in zip(drow_refs, grads[1:6]):
                ref[s, :, lanes] = val[i]

        def accumulate(start):
            for ref, val in zip(dpar_refs, grads[6:]):
                for pp in range(RWKV_PAIRS):
                    lanes = pl.ds(pp * LANES, LANES)
                    total = functools.reduce(jnp.add, val[pp * n_seq:(pp + 1) * n_seq])
                    ref[:, lanes] = total if start else ref[:, lanes] + total

        @pl.when(first)
        def _():
            accumulate(True)

        @pl.when(jnp.logical_not(first))
        def _():
            accumulate(False)

    rows_shape = jax.ShapeDtypeStruct(r.shape, F32)
    par_shape = jax.ShapeDtypeStruct((1, RWKV_W), F32)
    outs = pl.pallas_call(
        body, name="rwkv_bwd", grid=(n_groups, n_chunks),
        in_specs=[row] * 5 + [par] * 5 + [s0_spec, row, row] + [ANY_SPEC] * hosted.n,
        out_specs=[row] * 5 + [par] * 5 + [ANY_SPEC] * hosted.n,
        out_shape=[rows_shape] * 5 + [par_shape] * 5 + exchange_shapes(hosted.arrays, hosted.gather),
        scratch_shapes=[pltpu.VMEM((n_inst, LANES, LANES), F32)] + exchange_sems(hosted.n),
        compiler_params=_params(("arbitrary", "arbitrary")),
    )(r, kraw, v, lw, asig, k_k, k_a, r_k, gn_w, gn_b, s0_all, dy, gate, *hosted.arrays)
    return outs[:10], outs[10:]


SB_Q0 = RWKV_COLS // LANES
SB_K0 = SB_Q0 + SB_W // LANES
SB_V0 = SB_K0 + SB_W // LANES
SB_SEQS = 2
SB_BUFFERS = pl.Buffered(1)
SB_SUM_PIECES = 2
SB_DEAD = -110.0


def _col_of(c_lo, c_hi):
    return jnp.where(_lane_lo((1, LANES)), c_lo, c_hi)


def sb_fwd(p, gain, n_seq, seq_len, hosted):
    n_pairs = SB_W // LANES
    n_q = seq_len // QB
    nb = min(SB_SEQS, n_seq)

    def seq_spec(c0):
        return pl.BlockSpec((nb, seq_len, LANES), functools.partial(lambda b, h, c0: (b, 0, c0 + h), c0=c0),
                            pipeline_mode=SB_BUFFERS)

    out_spec = pl.BlockSpec((nb, seq_len, LANES), lambda b, h: (b, 0, h), pipeline_mode=SB_BUFFERS)

    def body(*refs):
        own, copies = hosted.split(refs, 4, 4)
        q_ref, k_ref, v_ref, g_ref, y_ref, o_ref, tot_ref, first_ref = own
        step = pl.program_id(0) * n_pairs + pl.program_id(1)
        hosted.run(copies, step == 0, step == (n_seq // nb) * n_pairs - 1)
        gain = g_ref[...]

        def q_block(i, _):
            qs = pl.multiple_of(i * QB, QB)
            seqs = range(nb)
            zeros = [jnp.zeros((QB, 1), F32)] * nb
            qv = [sb_split_q(q_ref[s, pl.ds(qs, QB), :]) for s in seqs]
            add = lambda xs, ys: [x + y for x, y in zip(xs, ys)]

            def tiles(ks, c_lo, c_hi, diag):
                return sb_tile(qv, [k_ref[s, pl.ds(ks, QB), :] for s in seqs],
                               [v_ref[s, pl.ds(ks, QB), :] for s in seqs], c_lo, c_hi, diag)

            def alive(c_lo, c_hi):
                top = functools.reduce(jnp.maximum, list(c_lo) + list(c_hi))
                return jnp.max(top) > SB_DEAD

            def k_block(state):
                j, _, (o, c_lo, c_hi) = state
                o2, s_lo, s_hi = tiles(pl.multiple_of(j * QB, QB), c_lo, c_hi, False)
                c_lo, c_hi = add(c_lo, s_lo), add(c_hi, s_hi)
                return j - 1, alive(c_lo, c_hi), (add(o, o2), c_lo, c_hi)

            o, c_lo, c_hi = tiles(qs, zeros, zeros, True)
            j, _, (o, c_lo, c_hi) = lax.while_loop(lambda st: jnp.logical_and(st[0] >= 0, st[1]), k_block,
                                                   (i - 1, alive(c_lo, c_hi), (o, c_lo, c_hi)))
            first_ref[pl.program_id(0), pl.program_id(1), i] = j + 1
            for s in seqs:
                o_ref[s, pl.ds(qs, QB), :] = o[s]
                tot_ref[s, pl.ds(qs, QB), :] = jnp.broadcast_to(_col_of(c_lo[s], c_hi[s]), (QB, LANES))
                y_ref[s, pl.ds(qs, QB), :] = sb_post(o[s], gain)
            return 0

        lax.fori_loop(0, n_q, q_block, 0)

    shape = jax.ShapeDtypeStruct((n_seq, seq_len, SB_W), F32)
    return pl.pallas_call(
        body, name="sb_fwd", grid=(n_seq // nb, n_pairs),
        in_specs=[seq_spec(SB_Q0), seq_spec(SB_K0), seq_spec(SB_V0), pl.BlockSpec((1, LANES), lambda b, h: (0, h))]
        + [ANY_SPEC] * hosted.n,
        out_specs=[out_spec] * 3 + [pl.BlockSpec(memory_space=pltpu.SMEM)] + [ANY_SPEC] * hosted.n,
        out_shape=[shape] * 3 + [jax.ShapeDtypeStruct((n_seq // nb, n_pairs, n_q), jnp.int32)]
        + exchange_shapes(hosted.arrays, hosted.gather),
        scratch_shapes=exchange_sems(hosted.n),
        compiler_params=_params(("arbitrary", "arbitrary")),
    )(p, p, p, gain, *hosted.arrays)


def sb_bwd(p, gain, o_raw, tot, dy, first, n_seq, seq_len):
    n_pairs = SB_W // LANES
    n_q = seq_len // QB
    nb = min(SB_SEQS, n_seq)

    def seq_spec(c0):
        return pl.BlockSpec((nb, seq_len, LANES), functools.partial(lambda h, b, c0: (b, 0, c0 + h), c0=c0),
                            pipeline_mode=SB_BUFFERS)

    own = pl.BlockSpec((nb, seq_len, LANES), lambda h, b: (b, 0, h), pipeline_mode=SB_BUFFERS)
    par = pl.BlockSpec((1, LANES), lambda h, b: (0, h))

    def body(q_ref, k_ref, v_ref, g_ref, o_ref, tot_ref, dy_ref, first_ref, dq_ref, dk_ref, dv_ref, dg_ref):
        gain = g_ref[...]
        lo = _lane_lo((1, LANES))
        dk_ref[...] = jnp.zeros_like(dk_ref)
        dv_ref[...] = jnp.zeros_like(dv_ref)

        def q_block(i, dgain):
            qs = pl.multiple_of(i * QB, QB)
            seqs = range(nb)
            zeros = [jnp.zeros((QB, 1), F32)] * nb
            qv, dov, t_lo, t_hi = [], [], [], []
            for s in seqs:
                qv.append(sb_split_q(q_ref[s, pl.ds(qs, QB), :]))
                _, post_vjp = jax.vjp(sb_post, o_ref[s, pl.ds(qs, QB), :], gain)
                do, dg_s = post_vjp(dy_ref[s, pl.ds(qs, QB), :])
                dov.append(do)
                dgain = dgain + dg_s
                tot_s = tot_ref[s, pl.ds(qs, QB), :]
                t_lo.append(jnp.max(jnp.where(lo, tot_s, -jnp.inf), axis=1, keepdims=True))
                t_hi.append(jnp.max(jnp.where(lo, -jnp.inf, tot_s), axis=1, keepdims=True))
            add = lambda xs, ys: [x + y for x, y in zip(xs, ys)]
            sub = lambda xs, ys: [x - y for x, y in zip(xs, ys)]

            def tile(ks, carry, diag):
                dq, rem_lo, rem_hi, g_lo, g_hi = carry
                kv = [k_ref[s, pl.ds(ks, QB), :] for s in seqs]
                vv = [v_ref[s, pl.ds(ks, QB), :] for s in seqs]
                fn = functools.partial(sb_tile, diag=diag, from_here=list(zip(rem_lo, rem_hi)))
                (_, s_lo, s_hi), vjp = jax.vjp(fn, qv, kv, vv, zeros, zeros)
                dq_t, dk_t, dv_t, dc_lo, dc_hi = vjp((dov, g_lo, g_hi))
                dq_t = [jnp.where(lo, d_lo, d_hi) for d_lo, d_hi in dq_t]
                for s in seqs:
                    dk_ref[s, pl.ds(ks, QB), :] = dk_ref[s, pl.ds(ks, QB), :] + dk_t[s]
                    dv_ref[s, pl.ds(ks, QB), :] = dv_ref[s, pl.ds(ks, QB), :] + dv_t[s]
                return add(dq, dq_t), sub(rem_lo, s_lo), sub(rem_hi, s_hi), add(g_lo, dc_lo), add(g_hi, dc_hi)

            def k_block(j, carry):
                return tile(pl.multiple_of(j * QB, QB), carry, False)

            carry = ([jnp.zeros((QB, LANES), F32)] * nb, t_lo, t_hi, zeros, zeros)
            carry = lax.fori_loop(first_ref[pl.program_id(1), pl.program_id(0), i], i, k_block, carry)
            carry = tile(qs, carry, True)
            for s in seqs:
                dq_ref[s, pl.ds(qs, QB), :] = carry[0][s] * SB_SCALE
            return dgain

        dgain = lax.fori_loop(0, n_q, q_block, jnp.zeros((1, LANES), F32))
        first = pl.program_id(1) == 0

        @pl.when(first)
        def _():
            dg_ref[...] = dgain

        @pl.when(jnp.logical_not(first))
        def _():
            dg_ref[...] = dg_ref[...] + dgain

    shape = jax.ShapeDtypeStruct((n_seq, seq_len, SB_W), F32)
    return pl.pallas_call(
        body, name="sb_bwd", grid=(n_pairs, n_seq // nb),
        in_specs=[seq_spec(SB_Q0), seq_spec(SB_K0), seq_spec(SB_V0), par, own, own, seq_spec(RWKV_W // LANES),
                  pl.BlockSpec(memory_space=pltpu.SMEM)],
        out_specs=[own, own, own, par],
        out_shape=[shape, shape, shape, jax.ShapeDtypeStruct((1, SB_W), F32)],
        compiler_params=_params(("arbitrary", "arbitrary")),
    )(p, p, p, gain, o_raw, tot, dy, first)


def exchange(name, arrays, gather):
    n = len(arrays)

    def body(*refs):
        copies = exchange_copies(refs[:n], refs[n:2 * n], *refs[2 * n:], gather)
        for cp in copies:
            cp.start()
        for cp in copies:
            cp.wait()

    return pl.pallas_call(
        body, name=name, in_specs=[ANY_SPEC] * n, out_specs=[ANY_SPEC] * n, out_shape=exchange_shapes(arrays, gather),
        scratch_shapes=exchange_sems(n),
    )(*arrays)


ANY_SPEC = pl.BlockSpec(memory_space=pl.ANY)


def exchange_shapes(arrays, gather):
    return [jax.ShapeDtypeStruct(((N_DEV,) + a.shape) if gather else a.shape, a.dtype) for a in arrays]


def exchange_sems(n):
    return [pltpu.SemaphoreType.DMA((n, N_DEV - 1)), pltpu.SemaphoreType.DMA((n, N_DEV - 1)),
            pltpu.SemaphoreType.DMA((n,))]


def exchange_copies(ins, outs, send_sems, recv_sems, local_sems, gather):
    x, y, c = lax.axis_index("x"), lax.axis_index("y"), lax.axis_index("c")
    me = 4 * x + 2 * y + c
    copies = []
    for a, (src_all, dst_all) in enumerate(zip(ins, outs)):
        own = src_all if gather else src_all.at[me]
        copies.append(pltpu.make_async_copy(own, dst_all.at[me], local_sems.at[a]))
        for j in range(1, N_DEV):
            px, py, pc = (x + (j >> 2)) % 2, (y + ((j >> 1) & 1)) % 2, (c + (j & 1)) % 2
            src = src_all if gather else src_all.at[4 * px + 2 * py + pc]
            copies.append(pltpu.make_async_remote_copy(
                src_ref=src, dst_ref=dst_all.at[me], send_sem=send_sems.at[a, j - 1],
                recv_sem=recv_sems.at[a, j - 1], device_id=(px, py, pc), device_id_type=pl.DeviceIdType.MESH))
    return copies


def adamw(name, w, parts, m, v, tile):
    rows, cols = w.shape
    spec = pl.BlockSpec((tile, cols), lambda i: (i, 0))
    part_spec = pl.BlockSpec((N_DEV, tile, cols), lambda i: (0, i, 0))

    def body(w_ref, p_ref, m_ref, v_ref, g_ref, d_ref, nm_ref, nv_ref):
        g = p_ref[0].astype(F32)
        for s in range(1, N_DEV):
            g = g + p_ref[s].astype(F32)
        new_m = ADAM_B1 * m_ref[...] + (1.0 - ADAM_B1) * g
        new_v = ADAM_B2 * v_ref[...] + (1.0 - ADAM_B2) * (g * g)
        m_hat = new_m / (1.0 - ADAM_B1 ** ADAM_STEP)
        v_hat = new_v / (1.0 - ADAM_B2 ** ADAM_STEP)
        g_ref[...] = g
        d_ref[...] = -ADAM_LR * (m_hat / (jnp.sqrt(v_hat) + ADAM_EPS) + ADAM_WD * w_ref[...])
        nm_ref[...] = new_m
        nv_ref[...] = new_v

    shape = jax.ShapeDtypeStruct((rows, cols), F32)
    return pl.pallas_call(
        body, name=name, grid=(rows // tile,), in_specs=[spec, part_spec, spec, spec],
        out_specs=[spec] * 4, out_shape=[shape] * 4, compiler_params=_params(("arbitrary",)),
    )(w, parts, m, v)


SMALL = ("ln1_g", "tok_mu", "w0", "a0", "k_k", "k_a", "r_k", "gn_w", "gn_b", "sb_gain", "ln2_g", "lnf_g")
EARLY = ("w_in", "w_decay_up", "w_aaa_up", "w_gate_up")
LATE = ("w_out", "w_up", "w_down")
BIG = EARLY + LATE
ORDER = ("ln1_g", "w_in", "tok_mu", "w0", "w_decay_up", "a0", "w_aaa_up", "w_gate_up", "k_k", "k_a", "r_k",
         "gn_w", "gn_b", "sb_gain", "w_out", "ln2_g", "w_up", "w_down", "lnf_g")


def _pack(vectors, rows):
    flat = jnp.concatenate([v.reshape(-1).astype(F32) for v in vectors])
    return jnp.pad(flat, (0, rows * LANES - flat.shape[0])).reshape(rows, LANES)


def _full_cols(gathered):
    d, k, cols = gathered.shape
    return jnp.transpose(gathered, (1, 0, 2)).reshape(k, d * cols)


def _col_parts(full):
    k, n = full.shape
    return jnp.transpose(full.reshape(k, N_DEV, n // N_DEV), (1, 0, 2))


def kernel(x, ln1_g, w_in, tok_mu, w0, w_decay_up, a0, w_aaa_up, w_gate_up, k_k, k_a, r_k, gn_w, gn_b, sb_gain, w_out, ln2_g, w_up, w_down, lnf_g, loss_target, m_ln1_g, m_w_in, m_tok_mu, m_w0, m_w_decay_up, m_a0, m_w_aaa_up, m_w_gate_up, m_k_k, m_k_a, m_r_k, m_gn_w, m_gn_b, m_sb_gain, m_w_out, m_ln2_g, m_w_up, m_w_down, m_lnf_g, v_ln1_g, v_w_in, v_tok_mu, v_w0, v_w_decay_up, v_a0, v_w_aaa_up, v_w_gate_up, v_k_k, v_k_a, v_r_k, v_gn_w, v_gn_b, v_sb_gain, v_w_out, v_ln2_g, v_w_up, v_w_down, v_lnf_g):
    args = dict(locals())
    weights = {n: args[n] for n in ORDER}
    mom_m = {n: args["m_" + n] for n in ORDER}
    mom_v = {n: args["v_" + n] for n in ORDER}

    n_seq, seq_len, d_model = x.shape
    n_tok = n_seq * seq_len
    x2d = x.reshape(n_tok, d_model)
    tgt = loss_target.reshape(n_tok, d_model)
    row = lambda t: t.reshape(1, -1).astype(F32)

    g1 = row(ln1_g)
    shard = {n: weights[n][0].astype(BF16) for n in BIG}
    h1, *early = rw_call("norm1", lambda r, q: [f_norm(r[0], q[0])], [x2d], [g1], [(d_model, BF16)], 512,
                         hosted=HostedChipGather([shard[n] for n in EARLY]))
    gathered = dict(zip(EARLY, early))
    w_in_f = _full_cols(gathered["w_in"])
    zeros64 = jnp.zeros((HEAD_DIM, RWKV_W), BF16)
    wd_pad = jnp.concatenate([_full_cols(gathered["w_decay_up"]), zeros64], axis=0)
    wa_pad = jnp.concatenate([zeros64, _full_cols(gathered["w_aaa_up"])], axis=0)
    wg_f = _full_cols(gathered["w_gate_up"])
    in_cols = w_in_f.shape[1]

    mu, w0r, a0r = row(tok_mu), row(w0), row(a0)
    kkr, kar, rkr, gwr, gbr, sgr = row(k_k), row(k_a), row(r_k), row(gn_w), row(gn_b), row(sb_gain)
    g2, gf = row(ln2_g), row(lnf_g)

    p = matmul("proj_in", h1, w_in_f, "nn", F32, 1024, in_cols // 2, d_model)
    prep_pars = [mu, w0r, wd_pad, a0r, wa_pad, wg_f]
    (r_, kraw, v_, lw, asig, gate), prep_edges = rwkv_prep_fwd(p, prep_pars, seq_len)
    by_seq = lambda t: t.reshape(n_seq, seq_len, t.shape[-1])
    flat = lambda t: t.reshape(n_tok, t.shape[-1])
    rwkv_in = [by_seq(t) for t in (r_, kraw, v_, lw, asig)]
    z, s0_all, (w_out_g, w_up_g) = rwkv_fwd(*rwkv_in, kkr, kar, rkr, gwr, gbr, n_seq, seq_len,
                                            Hosted([shard["w_out"], shard["w_up"]], True))
    y_sb, o_raw, tot, sb_first, w_down_g = sb_fwd(by_seq(p), sgr, n_seq, seq_len, Hosted([shard["w_down"]], True))
    w_out_f = w_out_g.reshape(d_model, d_model)
    w_up_f = _full_cols(w_up_g)
    w_down_f = w_down_g.reshape(-1, d_model)
    d_ff = w_up_f.shape[1]
    z, y_sb = flat(z), flat(y_sb)
    (ycat,) = rw_call("mix_cat", lambda r, q: [jnp.concatenate([r[0] * r[1], r[2]], axis=1)],
                      [z, gate, y_sb], [], [(d_model, BF16)], 512)
    x2, h2 = matmul("proj_out", ycat, w_out_f, "nn", (F32, BF16), 512, d_model, d_model, extras=[x2d], pars=[g2],
                    epilogue=lambda acc, xv, g: (xv + acc, f_norm(xv + acc, g)))
    u, act = matmul("mlp_up", h2, w_up_f, "nn", (F32, BF16), 1024, d_ff // 4, d_model,
                    epilogue=lambda acc: (acc, jnp.square(jnp.maximum(acc, 0.0))))

    def loss_epilogue(acc, x2v, target, g):
        loss_rows, vjp = jax.vjp(lambda xv, gv: f_final(xv, gv, target), x2v + acc, g)
        dx3, dgf = vjp(jnp.ones_like(loss_rows))
        return dx3, dx3, jnp.broadcast_to(jnp.sum(loss_rows), (1, LANES)), dgf

    dx3, dx3_b, loss_acc, d_lnf = matmul(
        "mlp_down", act, w_down_f, "nn", (F32, BF16), 512, d_model, d_ff, extras=[x2, tgt], pars=[gf],
        epilogue=loss_epilogue, sums=[((1, LANES), F32), ((1, d_model), F32)])

    du = matmul("d_act", dx3_b, w_down_f, "nt", BF16, 1024, d_ff // 4, d_model, extras=[u],
                epilogue=lambda acc, uv: (acc * (2.0 * jnp.maximum(uv, 0.0)),))
    dw_down = matmul("dw_down", act, dx3_b, "tn", BF16, 512, d_model, 4096)
    dw_up = matmul("dw_up", h2, du, "tn", BF16, d_model, 512, 4096)

    def norm_bwd(acc, xv, dres, g):
        _, vjp = jax.vjp(f_norm, xv, g)
        dx, dg = vjp(acc)
        return dx + dres, dx + dres, dg

    dx2, dx2_b, d_ln2 = matmul("d_h2", du, w_up_f, "nt", (F32, BF16), 512, d_model, d_ff, extras=[x2, dx3], pars=[g2],
                               epilogue=norm_bwd, sums=[((1, d_model), F32)])

    dycat = matmul("d_ycat", dx2_b, w_out_f, "nt", F32, 512, d_model, d_model)
    dw_out = matmul("dw_out", ycat, dx2_b, "tn", BF16, d_model, d_model, 2048)
    dq, dk_sb, dv_sb, d_sg = sb_bwd(by_seq(p), sgr, o_raw, tot, by_seq(dycat), sb_first, n_seq, seq_len)
    d_sb = [flat(dq), flat(dk_sb), flat(dv_sb)]
    late_grads = {"w_out": dw_out.reshape(N_DEV, -1, d_model), "w_up": _col_parts(dw_up),
                  "w_down": dw_down.reshape(N_DEV, -1, d_model)}
    (dr, dkraw, dv, dlw, dasig, d_kk, d_ka, d_rk, d_gw, d_gb), late_parts = rwkv_bwd(
        *rwkv_in, kkr, kar, rkr, gwr, gbr, s0_all, by_seq(dycat), by_seq(gate), n_seq, seq_len,
        Hosted([late_grads[n] for n in LATE], False))
    prep_cots = [flat(t) for t in (dr, dkraw, dv, dlw, dasig)] + [dycat, z]
    dp, (d_mu, d_w0, d_wd, d_a0, d_wa, d_wg) = rwkv_prep_bwd(p, prep_edges, prep_pars, prep_cots, d_sb, seq_len)

    dw_in = matmul("dw_in", h1, dp, "tn", BF16, d_model, in_cols // 2, 2048)
    lora_parts = lambda t: _col_parts(t).astype(BF16)
    early_grads = {"w_in": _col_parts(dw_in), "w_decay_up": lora_parts(d_wd[:HEAD_DIM]),
                   "w_aaa_up": lora_parts(d_wa[HEAD_DIM:]), "w_gate_up": lora_parts(d_wg)}
    dx, d_ln1, *early_parts = matmul(
        "d_h1", dp, w_in_f, "nt", (F32,), 512, d_model, in_cols, extras=[x2d, dx2], pars=[g1],
        epilogue=lambda *t: norm_bwd(*t)[1:], sums=[((1, d_model), F32)],
        hosted=Hosted([early_grads[n] for n in EARLY], False))
    parts = dict(zip(EARLY, early_parts))
    parts.update(zip(LATE, late_parts))

    small_grads = {"ln1_g": d_ln1, "tok_mu": d_mu, "w0": d_w0, "a0": d_a0, "k_k": d_kk, "k_a": d_ka, "r_k": d_rk,
                   "gn_w": d_gw, "gn_b": d_gb, "sb_gain": d_sg, "ln2_g": d_ln2, "lnf_g": d_lnf}
    n_small = sum(int(weights[n].size) for n in SMALL)
    pack_rows = -(-(n_small + 1) // (8 * LANES)) * 8
    packed = _pack([small_grads[n] for n in SMALL] + [loss_acc[0, :1]], pack_rows)
    (small_parts,) = exchange("gather_small", [packed], True)

    results = {}
    for n in BIG:
        w2d = weights[n][0]
        tile = w2d.shape[0] if w2d.shape[0] <= 256 else 256
        results[n] = adamw("adamw_" + n, w2d, parts[n], mom_m[n][0], mom_v[n][0], tile)
    pk = lambda d: _pack([d[n] for n in SMALL] + [jnp.zeros((1,), F32)], pack_rows)
    sg, sd, sm, sv = adamw("adamw_small", pk(weights), small_parts, pk(mom_m), pk(mom_v), pack_rows)
    off = 0
    for n in SMALL:
        size = int(weights[n].size)
        results[n] = tuple(t.reshape(-1)[off:off + size] for t in (sg, sd, sm, sv))
        off += size
    loss = sg.reshape(-1)[off]

    out = [loss, dx.reshape(x.shape)]
    for kind in range(4):
        out += [results[n][kind].reshape(weights[n].shape) for n in ORDER]
    return tuple(out)
```

```python
import functools
import math

import jax
import jax.numpy as jnp
from jax import lax
from jax.experimental import pallas as pl
from jax.experimental.pallas import tpu as pltpu

F32 = jnp.float32
BF16 = jnp.bfloat16

N_DEV = 8
HEAD_DIM = 64
LANES = 128
RWKV_W = 512
SB_W = 512
LORA_WA = 128
GATE_LORA = 128
RWKV_COLS = 3 * RWKV_W + LORA_WA + GATE_LORA
RMS_EPS = 1e-5
GN_EPS = 64e-5
CHUNK = 64
QB = 256
SB_SCALE = HEAD_DIM ** -0.5
ADAM_LR, ADAM_B1, ADAM_B2, ADAM_EPS, ADAM_WD, ADAM_STEP = 0.001, 0.9, 0.999, 1e-08, 0.01, 10
VMEM_LIMIT = 56 * 1024 * 1024


_DIMS = {
    "nn": (((1,), (0,)), ((), ())),
    "nt": (((1,), (1,)), ((), ())),
    "tn": (((0,), (0,)), ((), ())),
}


def _pieces(x, n):
    if n == 1:
        return [x.astype(BF16)]
    out, rem = [], x.astype(F32)
    for i in range(n):
        p = rem.astype(BF16)
        out.append(p)
        if i + 1 < n:
            rem = rem - p.astype(F32)
    return out


def _dot(a, b, form, pa, pb):
    pieces_a, pieces_b = _pieces(a, pa), _pieces(b, pb)
    keep = max(pa, pb)
    acc = None
    for i, ai in enumerate(pieces_a):
        for j, bj in enumerate(pieces_b):
            if i + j >= keep:
                continue
            t = lax.dot_general(ai, bj, _DIMS[form], preferred_element_type=F32)
            acc = t if acc is None else acc + t
    return acc


BOTH = (True, True)


@functools.partial(jax.custom_vjp, nondiff_argnums=(2, 3, 4, 5))
def mm(a, b, form, pa, pb, diff=BOTH):
    return _dot(a, b, form, pa, pb)


def _mm_fwd(a, b, form, pa, pb, diff):
    return _dot(a, b, form, pa, pb), (a, b)


def _mm_bwd(form, pa, pb, diff, res, g):
    a, b = res
    pg = max(pa, pb)
    da, db = jnp.zeros_like(a), jnp.zeros_like(b)
    if form == "nn":
        if diff[0]:
            da = mm(g, b, "nt", pg, pb)
        if diff[1]:
            db = mm(a, g, "tn", pa, pg)
    elif form == "nt":
        if diff[0]:
            da = mm(g, b, "nn", pg, pb)
        if diff[1]:
            db = mm(g, a, "tn", pg, pa)
    else:
        if diff[0]:
            da = mm(b, g, "nt", pb, pg)
        if diff[1]:
            db = mm(a, g, "nn", pa, pg)
    return da, db


mm.defvjp(_mm_fwd, _mm_bwd)


def _stack_rows(top, bottom):
    return jnp.concatenate([top, bottom], axis=0)


@jax.custom_vjp
def _split_rows(x):
    half = x.shape[0] // 2
    return x[:half], x[half:]


def _split_rows_fwd(x):
    return _split_rows(x), None


def _split_rows_bwd(_, grads):
    return (_stack_rows(*grads),)


_split_rows.defvjp(_split_rows_fwd, _split_rows_bwd)


def _lane_lo(shape):
    return lax.broadcasted_iota(jnp.int32, shape, len(shape) - 1) < HEAD_DIM


def _segsum(x):
    lo = _lane_lo(x.shape)
    s_lo = jnp.sum(jnp.where(lo, x, 0.0), axis=-1, keepdims=True)
    s_hi = jnp.sum(jnp.where(lo, 0.0, x), axis=-1, keepdims=True)
    return jnp.where(lo, s_lo, s_hi)


def _sigmoid(x):
    return 0.5 * (jnp.tanh(0.5 * x) + 1.0)


@jax.custom_vjp
def _log_one_minus_sigmoid(z):
    return -jnp.maximum(z, 0.0) - jnp.log(1.0 + jnp.exp(-jnp.abs(z)))


def _log_oms_fwd(z):
    out = _log_one_minus_sigmoid(z)
    return out, (z, out)


def _log_oms_bwd(res, g):
    z, out = res
    return (-g * jnp.exp(z + out),)


_log_one_minus_sigmoid.defvjp(_log_oms_fwd, _log_oms_bwd)


def f_norm(x, g):
    return x * lax.rsqrt(jnp.mean(x * x, axis=-1, keepdims=True) + RMS_EPS) * g


def f_prep(p, pprev, mu, w0, wd_pad, a0, wa_pad, wg):
    pm = p + mu * (pprev - p)
    r = pm[:, 0:RWKV_W]
    k = pm[:, RWKV_W:2 * RWKV_W]
    v = pm[:, 2 * RWKV_W:3 * RWKV_W]
    xwa = pm[:, 3 * RWKV_W:3 * RWKV_W + LORA_WA]
    xg = pm[:, 3 * RWKV_W + LORA_WA:RWKV_COLS]
    w = _log_one_minus_sigmoid(-(w0 + mm(jnp.tanh(xwa), wd_pad, "nn", 1, 1))) - 0.5
    lw = -jnp.exp(w)
    asig = _sigmoid(a0 + mm(xwa, wa_pad, "nn", 1, 1))
    gate = mm(_sigmoid(xg), wg, "nn", 1, 1)
    return r, k, v, lw, asig, gate


def _tri(n, kind):
    row = lax.broadcasted_iota(jnp.int32, (n, n), 0)
    col = lax.broadcasted_iota(jnp.int32, (n, n), 1)
    if kind == "lower_incl":
        return row >= col
    return row > col


@functools.partial(jax.custom_vjp, nondiff_argnums=(1,))
def _nilpotent_inverses(mats, hp):
    size = mats[0].shape[0]
    eye = (lax.broadcasted_iota(jnp.int32, (size, size), 0) == lax.broadcasted_iota(jnp.int32, (size, size), 1))
    tinv = [eye.astype(F32) + x for x in mats]
    pw = [mm(x, x, "nn", hp, hp) for x in mats]
    for _ in range(int(math.log2(size)) - 2):
        both = [_split_rows(mm(_stack_rows(t, x), x, "nn", hp, hp)) for t, x in zip(tinv, pw)]
        tinv = [t + tx for t, (tx, _) in zip(tinv, both)]
        pw = [xx for _, xx in both]
    return [t + mm(t, x, "nn", hp, hp) for t, x in zip(tinv, pw)]


def _nilpotent_inverses_fwd(mats, hp):
    tinv = _nilpotent_inverses(mats, hp)
    return tinv, tinv


def _nilpotent_inverses_bwd(hp, tinv, grads):
    right = [mm(g, t, "nt", hp, hp) for g, t in zip(grads, tinv)]
    return ([mm(t, x, "tn", hp, hp) for t, x in zip(tinv, right)],)


_nilpotent_inverses.defvjp(_nilpotent_inverses_fwd, _nilpotent_inverses_bwd)


@functools.partial(jax.custom_vjp, nondiff_argnums=(2,))
def _known_inverses(mats, tinv, hp):
    return list(tinv)


def _known_inverses_fwd(mats, tinv, hp):
    return list(tinv), tinv


def _known_inverses_bwd(hp, tinv, grads):
    return _nilpotent_inverses_bwd(hp, tinv, grads) + ([jnp.zeros_like(t) for t in tinv],)


_known_inverses.defvjp(_known_inverses_fwd, _known_inverses_bwd)


def rwkv_chunk(state, r, kraw, v, lw, asig, k_k, k_a, r_k, gn_w, gn_b, hp, saved_tinv=None, with_tinv=False):
    n = len(r)
    L = r[0].shape[0]
    lo = _lane_lo((1, LANES))
    masks = (lo, jnp.logical_not(lo))
    incl = _tri(L, "lower_incl")
    strict = _tri(L, "strict")
    tri = incl.astype(F32)
    kk = [x * w for x, w in zip(kraw, k_k)]
    kk = [x / jnp.maximum(jnp.sqrt(_segsum(x * x)), 1e-12) for x in kk]
    k = [x * (1.0 + (s - 1.0) * w) for x, s, w in zip(kraw, asig, k_a)]
    b = [x * s for x, s in zip(kk, asig)]
    c = [mm(tri, x, "nn", 1, 3, (False, True)) for x in lw]
    at = [-x * jnp.exp(ci - li) for x, ci, li in zip(kk, c, lw)]
    rt = [x * jnp.exp(ci) for x, ci in zip(r, c)]
    einv = [jnp.exp(-ci) for ci in c]
    bt = [x * e for x, e in zip(b, einv)]
    kt = [x * e for x, e in zip(k, einv)]
    inst = [(s, m) for s in range(n) for m in masks]
    ar_h = [_stack_rows(jnp.where(m, at[s], 0.0), jnp.where(m, rt[s], 0.0)) for s, m in inst]
    on_b = [_split_rows(mm(x, bt[s], "nt", hp, hp)) for x, (s, _) in zip(ar_h, inst)]
    on_k = [_split_rows(mm(x, kt[s], "nt", hp, hp)) for x, (s, _) in zip(ar_h, inst)]
    a_ab = [jnp.where(strict, x, 0.0) for x, _ in on_b]
    b_rb = [jnp.where(incl, x, 0.0) for _, x in on_b]
    a_ak = [jnp.where(strict, x, 0.0) for x, _ in on_k]
    b_rk = [jnp.where(incl, x, 0.0) for _, x in on_k]
    tinv = _nilpotent_inverses(a_ab, hp) if saved_tinv is None else _known_inverses(a_ab, saved_tinv, hp)
    on_state = [_split_rows(mm(x, state[s], "nt", hp, hp)) for x, (s, _) in zip(ar_h, inst)]
    on_v = [_split_rows(mm(_stack_rows(m1, m2), v[s], "nn", hp, hp)) for m1, m2, (s, _) in zip(a_ak, b_rk, inst)]
    u_h = [mm(t, sa + av, "nn", hp, hp) for t, (sa, _), (av, _) in zip(tinv, on_state, on_v)]
    y_h = [sr + mm(m1, u, "nn", hp, hp) + bv for (_, sr), m1, u, (_, bv) in zip(on_state, b_rb, u_h, on_v)]
    u_all = [jnp.where(lo, u_h[2 * s], u_h[2 * s + 1]) for s in range(n)]
    y_all = [jnp.where(lo, y_h[2 * s], y_h[2 * s + 1]) for s in range(n)]
    c_last = [jnp.sum(x, axis=0, keepdims=True) for x in lw]
    efwd = [jnp.exp(cl - ci) for cl, ci in zip(c_last, c)]
    new_state = [st * jnp.exp(cl) + mm(_stack_rows(u, vi), _stack_rows(bi * e, ki * e), "tn", hp, hp)
                 for st, cl, u, bi, e, vi, ki in zip(state, c_last, u_all, b, efwd, v, k)]
    row_head = lax.broadcasted_iota(jnp.int32, (LANES, LANES), 0) // HEAD_DIM
    col_head = lax.broadcasted_iota(jnp.int32, (LANES, LANES), 1) // HEAD_DIM
    new_state = [jnp.where(row_head == col_head, x, 0.0) for x in new_state]
    outs = []
    for y, ri, ki, vi, w_rk, w_gw, w_gb in zip(y_all, r, k, v, r_k, gn_w, gn_b):
        mean = _segsum(y) * (1.0 / HEAD_DIM)
        d = y - mean
        var = _segsum(d * d) * (1.0 / HEAD_DIM)
        yn = d * lax.rsqrt(var + GN_EPS) * w_gw + w_gb
        outs.append(yn + _segsum(ri * ki * w_rk) * vi)
    return (outs, new_state, tinv) if with_tinv else (outs, new_state)


def sb_tile(q, k, v, c_lo, c_hi, diag, from_here=None):
    n = len(q)
    lo = _lane_lo((1, LANES))
    below = _tri(QB, "strict")
    from_s = _tri(QB, "lower_incl").astype(F32)
    inst = [(s, h) for s in range(n) for h in (0, 1)]
    carry = [(c_lo[s], c_hi[s])[h] for s, h in inst]
    z = [mm(q[s][h], k[s], "nt", 1, 1) for s, h in inst]
    log_keep = [_log_one_minus_sigmoid(x) for x in z]
    if diag:
        log_keep = [jnp.where(below, x, 0.0) for x in log_keep]
    own = [jnp.sum(x, axis=1, keepdims=True) for x in log_keep]
    if from_here is not None:
        carry = [lax.stop_gradient(from_here[s][h] - o) + cr for (s, h), o, cr in zip(inst, own, carry)]
    tail = [mm(x, from_s, "nn", SB_SUM_PIECES, 1, (True, False)) for x in log_keep]
    log_a = [x + tl + cr for x, tl, cr in zip(z, tail, carry)]
    if diag:
        log_a = [jnp.where(below, x, -1e30) for x in log_a]
    att = [jnp.exp(x) for x in log_a]
    out_h = [mm(x, v[s], "nn", 1, 1) for x, (s, _) in zip(att, inst)]
    out = [jnp.where(lo, out_h[2 * s], out_h[2 * s + 1]) for s in range(n)]
    return out, [own[2 * s] for s in range(n)], [own[2 * s + 1] for s in range(n)]


def sb_split_q(q):
    lo = _lane_lo((1, LANES))
    qs = q * SB_SCALE
    return jnp.where(lo, qs, 0.0), jnp.where(lo, 0.0, qs)


def sb_post(o, gain):
    return o * lax.rsqrt(_segsum(o * o) * (1.0 / HEAD_DIM) + RMS_EPS) * gain


def f_final(x3, g, target):
    y = f_norm(x3, g)
    err = y - target
    return 0.5 * jnp.mean(err * err, axis=-1, keepdims=True)


def _params(sem):
    return pltpu.CompilerParams(dimension_semantics=sem, vmem_limit_bytes=VMEM_LIMIT)


def rw_call(name, body_fn, rows, pars, out_rows, tile, hosted=None):
    n_rows = rows[0].shape[0]
    tile = min(tile, n_rows)
    steps = n_rows // tile
    row_specs = [pl.BlockSpec((tile, arr.shape[1]), lambda i: (i, 0)) for arr in rows]
    par_specs = [pl.BlockSpec(p.shape, lambda i: (0, 0)) for p in pars]
    nr, npar, nor = len(rows), len(pars), len(out_rows)
    n_host = hosted.n if hosted is not None else 0

    def body(*refs):
        if hosted is not None:
            refs, copies = hosted.split(refs, nr + npar, nor)
            hosted.run(copies, pl.program_id(0) == 0, pl.program_id(0) == steps - 1)
        row_outs = body_fn([r[...] for r in refs[:nr]], [r[...] for r in refs[nr:nr + npar]])
        for ref, val in zip(refs[nr + npar:], row_outs):
            ref[...] = val.astype(ref.dtype)

    outs = pl.pallas_call(
        body, name=name, grid=(steps,), in_specs=row_specs + par_specs + [ANY_SPEC] * n_host,
        out_specs=[pl.BlockSpec((tile, c), lambda i: (i, 0)) for c, _ in out_rows] + [ANY_SPEC] * n_host,
        out_shape=[jax.ShapeDtypeStruct((n_rows, c), dt) for c, dt in out_rows]
        + (exchange_shapes(hosted.arrays, hosted.gather) if n_host else []),
        scratch_shapes=exchange_sems(n_host) if n_host else [],
        compiler_params=_params(("arbitrary",)),
    )(*rows, *pars, *(hosted.arrays if n_host else []))
    return outs


def matmul(name, a, b, form, out_dtype, tm, tn, tk, extras=(), pars=(), epilogue=None, sums=(), hosted=None):
    out_dtypes = out_dtype if isinstance(out_dtype, tuple) else (out_dtype,)
    tm, tn, tk = min(tm, a.shape[1 if form == "tn" else 0]), min(tn, b.shape[0 if form == "nt" else 1]), min(tk, a.shape[0 if form == "tn" else 1])
    if form == "nn":
        (m, kd), n = a.shape, b.shape[1]
        a_spec = pl.BlockSpec((tm, tk), lambda i, j, k: (i, k))
        b_spec = pl.BlockSpec((tk, tn), lambda i, j, k: (k, j))
    elif form == "nt":
        (m, kd), n = a.shape, b.shape[0]
        a_spec = pl.BlockSpec((tm, tk), lambda i, j, k: (i, k))
        b_spec = pl.BlockSpec((tn, tk), lambda i, j, k: (j, k))
    else:
        (kd, m), n = a.shape, b.shape[1]
        a_spec = pl.BlockSpec((tk, tm), lambda i, j, k: (k, i))
        b_spec = pl.BlockSpec((tk, tn), lambda i, j, k: (k, j))
    ksteps = kd // tk

    n_extra, n_par, n_out, n_sum = len(extras), len(pars), len(out_dtypes), len(sums)
    n_in = 2 + n_extra + n_par
    grid = (m // tm, n // tn, ksteps)

    def body(*refs):
        if hosted is not None:
            refs, copies = hosted.split(refs, n_in, n_out + n_sum)
            here = [pl.program_id(d) for d in range(3)]
            hosted.run(copies, functools.reduce(jnp.logical_and, [h == 0 for h in here]),
                       functools.reduce(jnp.logical_and, [h == g - 1 for h, g in zip(here, grid)]))
        a_ref, b_ref, rest = refs[0], refs[1], refs[2:]
        e_refs, o_refs = rest[:n_extra + n_par], rest[n_extra + n_par:n_extra + n_par + n_out]
        s_refs = rest[n_extra + n_par + n_out:n_extra + n_par + n_out + n_sum]
        kstep = pl.program_id(2)
        part = lax.dot_general(a_ref[...].astype(BF16), b_ref[...].astype(BF16), _DIMS[form],
                               preferred_element_type=F32)

        def finish(acc):
            outs = epilogue(acc, *[r[...] for r in e_refs]) if epilogue else (acc,)
            for ref, val in zip(o_refs, outs[:n_out]):
                ref[...] = val.astype(ref.dtype)
            if n_sum:
                first_tile = jnp.logical_and(pl.program_id(0) == 0, pl.program_id(1) == 0)

                @pl.when(first_tile)
                def _():
                    for ref, val in zip(s_refs, outs[n_out:]):
                        ref[...] = val.astype(ref.dtype)

                @pl.when(jnp.logical_not(first_tile))
                def _():
                    for ref, val in zip(s_refs, outs[n_out:]):
                        ref[...] = ref[...] + val.astype(ref.dtype)

        if ksteps == 1:
            finish(part)
            return
        acc_ref = rest[n_extra + n_par + n_out + n_sum]

        @pl.when(kstep == 0)
        def _():
            acc_ref[...] = part

        @pl.when(jnp.logical_and(kstep > 0, kstep < ksteps - 1))
        def _():
            acc_ref[...] = acc_ref[...] + part

        @pl.when(kstep == ksteps - 1)
        def _():
            finish(acc_ref[...] + part)

    out_spec = pl.BlockSpec((tm, tn), lambda i, j, k: (i, j))
    whole = lambda shape: pl.BlockSpec(shape, lambda i, j, k: (0,) * len(shape))
    n_host = hosted.n if hosted is not None else 0
    host_in = list(hosted.arrays) if hosted is not None else []
    host_out = exchange_shapes(hosted.arrays, hosted.gather) if hosted is not None else []
    outs = pl.pallas_call(
        body, name=name, grid=grid,
        in_specs=[a_spec, b_spec] + [out_spec] * n_extra + [whole(t.shape) for t in pars] + [ANY_SPEC] * n_host,
        out_specs=[out_spec] * n_out + [whole(s) for s, _ in sums] + [ANY_SPEC] * n_host,
        out_shape=[jax.ShapeDtypeStruct((m, n), dt) for dt in out_dtypes]
        + [jax.ShapeDtypeStruct(s, dt) for s, dt in sums] + host_out,
        scratch_shapes=([pltpu.VMEM((tm, tn), F32)] if ksteps > 1 else []) + (exchange_sems(n_host) if n_host else []),
        compiler_params=_params(("arbitrary",) * 3 if (sums or n_host) else ("parallel", "parallel", "arbitrary")),
    )(a, b, *extras, *pars, *host_in)
    return outs if (isinstance(out_dtype, tuple) or sums or n_host) else outs[0]


PREP_TILE = 256
PREP_TILE_BWD = 128
SUBLANES = 8


def _shift_in(rows, first):
    rolled = pltpu.roll(rows, 1, 0)
    row = lax.broadcasted_iota(jnp.int32, (SUBLANES, rows.shape[1]), 0)
    head = jnp.where(row == 0, first, rolled[0:SUBLANES])
    return jnp.concatenate([head, rolled[SUBLANES:]], axis=0), rolled


def rwkv_prep_fwd(p, pars, seq_len):
    n_tok = p.shape[0]
    tile = min(PREP_TILE, seq_len)
    tile_b = min(PREP_TILE_BWD, tile)
    steps, per_seq, sub = n_tok // tile, seq_len // tile, tile // tile_b
    n_par = len(pars)

    def body(p_ref, *rest):
        par_refs, out_refs, edge_ref, last8 = rest[:n_par], rest[n_par:n_par + 6], rest[n_par + 6], rest[n_par + 7]
        step = pl.program_id(0)

        @pl.when(step == 0)
        def _():
            last8[...] = jnp.zeros_like(last8)

        rows = p_ref[...]
        before = jnp.where(step % per_seq == 0, 0.0, pltpu.roll(last8[...], 1, 0))
        prev, rolled = _shift_in(rows, before)
        edge_ref[0] = prev[0:SUBLANES]
        for m in range(1, sub):
            edge_ref[m] = rolled[m * tile_b:m * tile_b + SUBLANES]
        last8[...] = rows[tile - SUBLANES:tile]
        for ref, val in zip(out_refs, f_prep(rows, prev, *[r[...] for r in par_refs])):
            ref[...] = val

    row_out = pl.BlockSpec((tile, RWKV_W), lambda i: (i, 0))
    outs = pl.pallas_call(
        body, name="rwkv_prep", grid=(steps,),
        in_specs=[pl.BlockSpec((tile, RWKV_COLS), lambda i: (i, 0))] + [pl.BlockSpec(t.shape, lambda i: (0, 0)) for t in pars],
        out_specs=[row_out] * 6 + [pl.BlockSpec((sub, SUBLANES, RWKV_COLS), lambda i: (i, 0, 0))],
        out_shape=[jax.ShapeDtypeStruct((n_tok, RWKV_W), F32)] * 6
        + [jax.ShapeDtypeStruct((steps * sub, SUBLANES, RWKV_COLS), F32)],
        scratch_shapes=[pltpu.VMEM((SUBLANES, RWKV_COLS), F32)],
        compiler_params=_params(("arbitrary",)),
    )(p, *pars)
    return outs[:6], outs[6]


def rwkv_prep_bwd(p, edges, pars, cots, d_sb, seq_len):
    n_tok = p.shape[0]
    tile = min(PREP_TILE_BWD, seq_len)
    steps, per_seq = n_tok // tile, seq_len // tile
    n_par = len(pars)
    back = lambda i: steps - 1 - i

    def body(p_ref, edge_ref, *rest):
        par_refs, rest = rest[:n_par], rest[n_par:]
        cot_refs, dy_ref, z_ref, sb_refs = rest[:5], rest[5], rest[6], rest[7:10]
        dp_ref, acc_refs, next8 = rest[10], rest[11:11 + n_par], rest[11 + n_par]
        step = pl.program_id(0)
        first = step == 0

        @pl.when(first)
        def _():
            next8[...] = jnp.zeros_like(next8)

        rows = p_ref[...]
        prev, _ = _shift_in(rows, edge_ref[0])
        _, vjp = jax.vjp(f_prep, rows, prev, *[r[...].astype(F32) for r in par_refs])
        grads = vjp(tuple(r[...] for r in cot_refs) + (dy_ref[...] * z_ref[...],))
        d_rows, d_prev = grads[0], grads[1]
        up = pltpu.roll(d_prev, tile - 1, 0)
        ends_seq = back(step) % per_seq == per_seq - 1
        after = jnp.where(ends_seq, 0.0, pltpu.roll(next8[...], SUBLANES - 1, 0))
        row = lax.broadcasted_iota(jnp.int32, (SUBLANES, RWKV_COLS), 0)
        tail = jnp.where(row == SUBLANES - 1, after, up[tile - SUBLANES:tile])
        d_rows = d_rows + jnp.concatenate([up[:tile - SUBLANES], tail], axis=0)
        next8[...] = d_prev[0:SUBLANES]
        dp_ref[...] = jnp.concatenate([d_rows] + [r[...] for r in sb_refs], axis=1).astype(dp_ref.dtype)

        @pl.when(first)
        def _():
            for ref, val in zip(acc_refs, grads[2:]):
                ref[...] = val

        @pl.when(jnp.logical_not(first))
        def _():
            for ref, val in zip(acc_refs, grads[2:]):
                ref[...] = ref[...] + val

    cols = RWKV_COLS + sum(t.shape[1] for t in d_sb)
    half = pl.BlockSpec((tile, RWKV_W), lambda i: (back(i), 0))
    par_specs = [pl.BlockSpec(t.shape, lambda i: (0, 0)) for t in pars]
    outs = pl.pallas_call(
        body, name="d_rwkv_prep", grid=(steps,),
        in_specs=[pl.BlockSpec((tile, RWKV_COLS), lambda i: (back(i), 0)),
                  pl.BlockSpec((1, SUBLANES, RWKV_COLS), lambda i: (back(i), 0, 0))] + par_specs + [half] * 10,
        out_specs=[pl.BlockSpec((tile, cols), lambda i: (back(i), 0))] + par_specs,
        out_shape=[jax.ShapeDtypeStruct((n_tok, cols), BF16)] + [jax.ShapeDtypeStruct(t.shape, F32) for t in pars],
        scratch_shapes=[pltpu.VMEM((SUBLANES, RWKV_COLS), F32)],
        compiler_params=_params(("arbitrary",)),
    )(p, edges, *pars, *cots, *d_sb)
    return outs[0], outs[1:]


RWKV_HP = 1


RWKV_PAIRS = 4


def _rwkv_specs(n_seq, chunk_of):
    width = RWKV_PAIRS * LANES
    row = pl.BlockSpec((n_seq, CHUNK, width), lambda g, c: (0, chunk_of(c), g))
    par = pl.BlockSpec((1, width), lambda g, c: (0, g))
    s0 = pl.BlockSpec((1, 1, RWKV_PAIRS * n_seq, LANES, LANES), lambda g, c: (g, chunk_of(c), 0, 0, 0))
    tinv = pl.BlockSpec((1, 1, 2 * RWKV_PAIRS * n_seq, CHUNK, CHUNK), lambda g, c: (g, chunk_of(c), 0, 0, 0))
    return row, par, s0, tinv


class Hosted:
    def __init__(self, arrays, gather):
        self.arrays, self.gather, self.n = list(arrays), gather, len(arrays)

    def split(self, refs, n_in, n_out):
        n = self.n
        ins, outs, sems = refs[n_in:n_in + n], refs[n_in + n + n_out:n_in + 2 * n + n_out], refs[-3:]
        own = refs[:n_in] + refs[n_in + n:n_in + n + n_out] + refs[n_in + 2 * n + n_out:-3]
        return own, exchange_copies(ins, outs, *sems, self.gather)

    def run(self, copies, first, last):
        @pl.when(first)
        def _():
            for cp in copies:
                cp.start()

        @pl.when(last)
        def _():
            for cp in copies:
                cp.wait()


class HostedChipGather(Hosted):
    def __init__(self, arrays):
        super().__init__(arrays, True)

    def split(self, refs, n_in, n_out):
        n = self.n
        ins, outs, sems = refs[n_in:n_in + n], refs[n_in + n + n_out:n_in + 2 * n + n_out], refs[-3:]
        own = refs[:n_in] + refs[n_in + n:n_in + n + n_out] + refs[n_in + 2 * n + n_out:-3]
        return own, (ins, outs, sems)

    def run(self, state, first, last):
        ins, outs, (send_sems, recv_sems, local_sems) = state
        x, y, c = lax.axis_index("x"), lax.axis_index("y"), lax.axis_index("c")
        chips = [((x + 1) % 2, y), (x, (y + 1) % 2), ((x + 1) % 2, (y + 1) % 2)]

        def block(a, k, of, to, src=None):
            dst = outs[a].at[4 * of[0] + 2 * of[1] + of[2]]
            return pltpu.make_async_remote_copy(
                src_ref=dst if src is None else src, dst_ref=dst, send_sem=send_sems.at[a, k],
                recv_sem=recv_sems.at[a, k], device_id=to, device_id_type=pl.DeviceIdType.MESH)

        me, sibling = (x, y, c), (x, y, 1 - c)
        local = [pltpu.make_async_copy(ins[a], outs[a].at[4 * x + 2 * y + c], local_sems.at[a]) for a in range(self.n)]
        mine = [block(a, 0, me, sibling, ins[a]) for a in range(self.n)]
        mine += [block(a, 1 + j, me, (*chip, c), ins[a]) for a in range(self.n) for j, chip in enumerate(chips)]
        passed = [block(a, 4 + j, (*chip, c), sibling) for a in range(self.n) for j, chip in enumerate(chips)]

        @pl.when(first)
        def _():
            for cp in local + mine:
                cp.start()

        @pl.when(last)
        def _():
            for a in range(self.n):
                for j, chip in enumerate(chips):
                    block(a, 1 + j, (*chip, c), me).wait_recv()
                    passed[a * len(chips) + j].start()
            for a in range(self.n):
                block(a, 0, sibling, me).wait_recv()
                for j, chip in enumerate(chips):
                    block(a, 4 + j, (*chip, 1 - c), me).wait_recv()
            for cp in mine + passed:
                cp.wait_send()
            for cp in local:
                cp.wait()


def rwkv_fwd(r, kraw, v, lw, asig, k_k, k_a, r_k, gn_w, gn_b, n_seq, seq_len, hosted):
    n_chunks = seq_len // CHUNK
    n_groups = RWKV_W // (RWKV_PAIRS * LANES)
    n_inst = RWKV_PAIRS * n_seq
    row, par, s0_spec, tinv_spec = _rwkv_specs(n_seq, lambda c: c)
    inst = [(s, pl.ds(pp * LANES, LANES)) for pp in range(RWKV_PAIRS) for s in range(n_seq)]

    def body(*refs):
        own, copies = hosted.split(refs, 10, 3)
        row_refs, par_refs, (z_ref, s0_ref, tinv_ref, state) = own[:5], own[5:10], own[10:]
        step = pl.program_id(0) * n_chunks + pl.program_id(1)
        hosted.run(copies, step == 0, step == n_groups * n_chunks - 1)

        @pl.when(pl.program_id(1) == 0)
        def _():
            state[...] = jnp.zeros_like(state)

        s0 = [state[i] for i in range(n_inst)]
        rows = [[ref[s, :, lanes] for s, lanes in inst] for ref in row_refs]
        pars = [[ref[:, lanes] for _, lanes in inst] for ref in par_refs]
        z, s1, tinv = rwkv_chunk(s0, *rows, *pars, RWKV_HP, with_tinv=True)
        for i, (s, lanes) in enumerate(inst):
            s0_ref[0, 0, i] = s0[i]
            z_ref[s, :, lanes] = z[i]
            state[i] = s1[i]
        for i, t in enumerate(tinv):
            tinv_ref[0, 0, i] = t

    outs = pl.pallas_call(
        body, name="rwkv_fwd", grid=(n_groups, n_chunks),
        in_specs=[row] * 5 + [par] * 5 + [ANY_SPEC] * hosted.n,
        out_specs=[row, s0_spec, tinv_spec] + [ANY_SPEC] * hosted.n,
        out_shape=[jax.ShapeDtypeStruct(r.shape, F32),
                   jax.ShapeDtypeStruct((n_groups, n_chunks, n_inst, LANES, LANES), F32),
                   jax.ShapeDtypeStruct((n_groups, n_chunks, 2 * n_inst, CHUNK, CHUNK), F32)]
        + exchange_shapes(hosted.arrays, hosted.gather),
        scratch_shapes=[pltpu.VMEM((n_inst, LANES, LANES), F32)] + exchange_sems(hosted.n),
        compiler_params=_params(("arbitrary", "arbitrary")),
    )(r, kraw, v, lw, asig, k_k, k_a, r_k, gn_w, gn_b, *hosted.arrays)
    return outs[0], outs[1], outs[2], outs[3:]


def rwkv_bwd(r, kraw, v, lw, asig, k_k, k_a, r_k, gn_w, gn_b, s0_all, tinv_all, dy, gate, n_seq, seq_len, hosted):
    n_chunks = seq_len // CHUNK
    n_groups = RWKV_W // (RWKV_PAIRS * LANES)
    n_inst = RWKV_PAIRS * n_seq
    row, par, s0_spec, tinv_spec = _rwkv_specs(n_seq, lambda c: n_chunks - 1 - c)
    inst = [(s, pl.ds(pp * LANES, LANES)) for pp in range(RWKV_PAIRS) for s in range(n_seq)]

    def body(*refs):
        own, copies = hosted.split(refs, 14, 10)
        row_refs, par_refs, (s0_ref, tinv_ref, dy_ref, gate_ref) = own[:5], own[5:10], own[10:14]
        drow_refs, dpar_refs, dstate = own[14:19], own[19:24], own[24]
        step = pl.program_id(0) * n_chunks + pl.program_id(1)
        hosted.run(copies, step == 0, step == n_groups * n_chunks - 1)
        first = pl.program_id(1) == 0

        @pl.when(first)
        def _():
            dstate[...] = jnp.zeros_like(dstate)

        fn = functools.partial(rwkv_chunk, hp=RWKV_HP, saved_tinv=[tinv_ref[0, 0, i] for i in range(2 * n_inst)])
        rows = [[ref[s, :, lanes] for s, lanes in inst] for ref in row_refs]
        pars = [[ref[:, lanes] for _, lanes in inst] for ref in par_refs]
        _, vjp = jax.vjp(fn, [s0_ref[0, 0, i] for i in range(n_inst)], *rows, *pars)
        dz = [dy_ref[s, :, lanes] * gate_ref[s, :, lanes] for s, lanes in inst]
        grads = vjp((dz, [dstate[i] for i in range(n_inst)]))
        for i, (s, lanes) in enumerate(inst):
            dstate[i] = grads[0][i]
            for ref, val in zip(drow_refs, grads[1:6]):
                ref[s, :, lanes] = val[i]

        def accumulate(start):
            for ref, val in zip(dpar_refs, grads[6:]):
                for pp in range(RWKV_PAIRS):
                    lanes = pl.ds(pp * LANES, LANES)
                    total = functools.reduce(jnp.add, val[pp * n_seq:(pp + 1) * n_seq])
                    ref[:, lanes] = total if start else ref[:, lanes] + total

        @pl.when(first)
        def _():
            accumulate(True)

        @pl.when(jnp.logical_not(first))
        def _():
            accumulate(False)

    rows_shape = jax.ShapeDtypeStruct(r.shape, F32)
    par_shape = jax.ShapeDtypeStruct((1, RWKV_W), F32)
    outs = pl.pallas_call(
        body, name="rwkv_bwd", grid=(n_groups, n_chunks),
        in_specs=[row] * 5 + [par] * 5 + [s0_spec, tinv_spec, row, row] + [ANY_SPEC] * hosted.n,
        out_specs=[row] * 5 + [par] * 5 + [ANY_SPEC] * hosted.n,
        out_shape=[rows_shape] * 5 + [par_shape] * 5 + exchange_shapes(hosted.arrays, hosted.gather),
        scratch_shapes=[pltpu.VMEM((n_inst, LANES, LANES), F32)] + exchange_sems(hosted.n),
        compiler_params=_params(("arbitrary", "arbitrary")),
    )(r, kraw, v, lw, asig, k_k, k_a, r_k, gn_w, gn_b, s0_all, tinv_all, dy, gate, *hosted.arrays)
    return outs[:10], outs[10:]


SB_Q0 = RWKV_COLS // LANES
SB_K0 = SB_Q0 + SB_W // LANES
SB_V0 = SB_K0 + SB_W // LANES
SB_SEQS = 2
SB_BUFFERS = pl.Buffered(1)
SB_SUM_PIECES = 2
SB_DEAD = -110.0


def _col_of(c_lo, c_hi):
    return jnp.where(_lane_lo((1, LANES)), c_lo, c_hi)


def sb_fwd(p, gain, n_seq, seq_len, hosted):
    n_pairs = SB_W // LANES
    n_q = seq_len // QB
    nb = min(SB_SEQS, n_seq)

    def seq_spec(c0):
        return pl.BlockSpec((nb, seq_len, LANES), functools.partial(lambda b, h, c0: (b, 0, c0 + h), c0=c0),
                            pipeline_mode=SB_BUFFERS)

    out_spec = pl.BlockSpec((nb, seq_len, LANES), lambda b, h: (b, 0, h), pipeline_mode=SB_BUFFERS)

    def body(*refs):
        own, copies = hosted.split(refs, 4, 4)
        q_ref, k_ref, v_ref, g_ref, y_ref, o_ref, tot_ref, first_ref = own
        step = pl.program_id(0) * n_pairs + pl.program_id(1)
        hosted.run(copies, step == 0, step == (n_seq // nb) * n_pairs - 1)
        gain = g_ref[...]

        def q_block(i, _):
            qs = pl.multiple_of(i * QB, QB)
            seqs = range(nb)
            zeros = [jnp.zeros((QB, 1), F32)] * nb
            qv = [sb_split_q(q_ref[s, pl.ds(qs, QB), :]) for s in seqs]
            add = lambda xs, ys: [x + y for x, y in zip(xs, ys)]

            def tiles(ks, c_lo, c_hi, diag):
                return sb_tile(qv, [k_ref[s, pl.ds(ks, QB), :] for s in seqs],
                               [v_ref[s, pl.ds(ks, QB), :] for s in seqs], c_lo, c_hi, diag)

            def alive(c_lo, c_hi):
                top = functools.reduce(jnp.maximum, list(c_lo) + list(c_hi))
                return jnp.max(top) > SB_DEAD

            def k_block(state):
                j, _, (o, c_lo, c_hi) = state
                o2, s_lo, s_hi = tiles(pl.multiple_of(j * QB, QB), c_lo, c_hi, False)
                c_lo, c_hi = add(c_lo, s_lo), add(c_hi, s_hi)
                return j - 1, alive(c_lo, c_hi), (add(o, o2), c_lo, c_hi)

            o, c_lo, c_hi = tiles(qs, zeros, zeros, True)
            j, _, (o, c_lo, c_hi) = lax.while_loop(lambda st: jnp.logical_and(st[0] >= 0, st[1]), k_block,
                                                   (i - 1, alive(c_lo, c_hi), (o, c_lo, c_hi)))
            first_ref[pl.program_id(0), pl.program_id(1), i] = j + 1
            for s in seqs:
                o_ref[s, pl.ds(qs, QB), :] = o[s]
                tot_ref[s, pl.ds(qs, QB), :] = jnp.broadcast_to(_col_of(c_lo[s], c_hi[s]), (QB, LANES))
                y_ref[s, pl.ds(qs, QB), :] = sb_post(o[s], gain)
            return 0

        lax.fori_loop(0, n_q, q_block, 0)

    shape = jax.ShapeDtypeStruct((n_seq, seq_len, SB_W), F32)
    return pl.pallas_call(
        body, name="sb_fwd", grid=(n_seq // nb, n_pairs),
        in_specs=[seq_spec(SB_Q0), seq_spec(SB_K0), seq_spec(SB_V0), pl.BlockSpec((1, LANES), lambda b, h: (0, h))]
        + [ANY_SPEC] * hosted.n,
        out_specs=[out_spec] * 3 + [pl.BlockSpec(memory_space=pltpu.SMEM)] + [ANY_SPEC] * hosted.n,
        out_shape=[shape] * 3 + [jax.ShapeDtypeStruct((n_seq // nb, n_pairs, n_q), jnp.int32)]
        + exchange_shapes(hosted.arrays, hosted.gather),
        scratch_shapes=exchange_sems(hosted.n),
        compiler_params=_params(("arbitrary", "arbitrary")),
    )(p, p, p, gain, *hosted.arrays)


def sb_bwd(p, gain, o_raw, tot, dy, first, n_seq, seq_len):
    n_pairs = SB_W // LANES
    n_q = seq_len // QB
    nb = min(SB_SEQS, n_seq)

    def seq_spec(c0):
        return pl.BlockSpec((nb, seq_len, LANES), functools.partial(lambda h, b, c0: (b, 0, c0 + h), c0=c0),
                            pipeline_mode=SB_BUFFERS)

    own = pl.BlockSpec((nb, seq_len, LANES), lambda h, b: (b, 0, h), pipeline_mode=SB_BUFFERS)
    par = pl.BlockSpec((1, LANES), lambda h, b: (0, h))

    def body(q_ref, k_ref, v_ref, g_ref, o_ref, tot_ref, dy_ref, first_ref, dq_ref, dk_ref, dv_ref, dg_ref):
        gain = g_ref[...]
        lo = _lane_lo((1, LANES))
        dk_ref[...] = jnp.zeros_like(dk_ref)
        dv_ref[...] = jnp.zeros_like(dv_ref)

        def q_block(i, dgain):
            qs = pl.multiple_of(i * QB, QB)
            seqs = range(nb)
            zeros = [jnp.zeros((QB, 1), F32)] * nb
            qv, dov, t_lo, t_hi = [], [], [], []
            for s in seqs:
                qv.append(sb_split_q(q_ref[s, pl.ds(qs, QB), :]))
                _, post_vjp = jax.vjp(sb_post, o_ref[s, pl.ds(qs, QB), :], gain)
                do, dg_s = post_vjp(dy_ref[s, pl.ds(qs, QB), :])
                dov.append(do)
                dgain = dgain + dg_s
                tot_s = tot_ref[s, pl.ds(qs, QB), :]
                t_lo.append(jnp.max(jnp.where(lo, tot_s, -jnp.inf), axis=1, keepdims=True))
                t_hi.append(jnp.max(jnp.where(lo, -jnp.inf, tot_s), axis=1, keepdims=True))
            add = lambda xs, ys: [x + y for x, y in zip(xs, ys)]
            sub = lambda xs, ys: [x - y for x, y in zip(xs, ys)]

            def tile(ks, carry, diag):
                dq, rem_lo, rem_hi, g_lo, g_hi = carry
                kv = [k_ref[s, pl.ds(ks, QB), :] for s in seqs]
                vv = [v_ref[s, pl.ds(ks, QB), :] for s in seqs]
                fn = functools.partial(sb_tile, diag=diag, from_here=list(zip(rem_lo, rem_hi)))
                (_, s_lo, s_hi), vjp = jax.vjp(fn, qv, kv, vv, zeros, zeros)
                dq_t, dk_t, dv_t, dc_lo, dc_hi = vjp((dov, g_lo, g_hi))
                dq_t = [jnp.where(lo, d_lo, d_hi) for d_lo, d_hi in dq_t]
                for s in seqs:
                    dk_ref[s, pl.ds(ks, QB), :] = dk_ref[s, pl.ds(ks, QB), :] + dk_t[s]
                    dv_ref[s, pl.ds(ks, QB), :] = dv_ref[s, pl.ds(ks, QB), :] + dv_t[s]
                return add(dq, dq_t), sub(rem_lo, s_lo), sub(rem_hi, s_hi), add(g_lo, dc_lo), add(g_hi, dc_hi)

            def k_block(j, carry):
                return tile(pl.multiple_of(j * QB, QB), carry, False)

            carry = ([jnp.zeros((QB, LANES), F32)] * nb, t_lo, t_hi, zeros, zeros)
            carry = lax.fori_loop(first_ref[pl.program_id(1), pl.program_id(0), i], i, k_block, carry)
            carry = tile(qs, carry, True)
            for s in seqs:
                dq_ref[s, pl.ds(qs, QB), :] = carry[0][s] * SB_SCALE
            return dgain

        dgain = lax.fori_loop(0, n_q, q_block, jnp.zeros((1, LANES), F32))
        first = pl.program_id(1) == 0

        @pl.when(first)
        def _():
            dg_ref[...] = dgain

        @pl.when(jnp.logical_not(first))
        def _():
            dg_ref[...] = dg_ref[...] + dgain

    shape = jax.ShapeDtypeStruct((n_seq, seq_len, SB_W), F32)
    return pl.pallas_call(
        body, name="sb_bwd", grid=(n_pairs, n_seq // nb),
        in_specs=[seq_spec(SB_Q0), seq_spec(SB_K0), seq_spec(SB_V0), par, own, own, seq_spec(RWKV_W // LANES),
                  pl.BlockSpec(memory_space=pltpu.SMEM)],
        out_specs=[own, own, own, par],
        out_shape=[shape, shape, shape, jax.ShapeDtypeStruct((1, SB_W), F32)],
        compiler_params=_params(("arbitrary", "arbitrary")),
    )(p, p, p, gain, o_raw, tot, dy, first)


def exchange(name, arrays, gather):
    n = len(arrays)

    def body(*refs):
        copies = exchange_copies(refs[:n], refs[n:2 * n], *refs[2 * n:], gather)
        for cp in copies:
            cp.start()
        for cp in copies:
            cp.wait()

    return pl.pallas_call(
        body, name=name, in_specs=[ANY_SPEC] * n, out_specs=[ANY_SPEC] * n, out_shape=exchange_shapes(arrays, gather),
        scratch_shapes=exchange_sems(n),
    )(*arrays)


ANY_SPEC = pl.BlockSpec(memory_space=pl.ANY)


def exchange_shapes(arrays, gather):
    return [jax.ShapeDtypeStruct(((N_DEV,) + a.shape) if gather else a.shape, a.dtype) for a in arrays]


def exchange_sems(n):
    return [pltpu.SemaphoreType.DMA((n, N_DEV - 1)), pltpu.SemaphoreType.DMA((n, N_DEV - 1)),
            pltpu.SemaphoreType.DMA((n,))]


def exchange_copies(ins, outs, send_sems, recv_sems, local_sems, gather):
    x, y, c = lax.axis_index("x"), lax.axis_index("y"), lax.axis_index("c")
    me = 4 * x + 2 * y + c
    copies = []
    for a, (src_all, dst_all) in enumerate(zip(ins, outs)):
        own = src_all if gather else src_all.at[me]
        copies.append(pltpu.make_async_copy(own, dst_all.at[me], local_sems.at[a]))
        for j in range(1, N_DEV):
            px, py, pc = (x + (j >> 2)) % 2, (y + ((j >> 1) & 1)) % 2, (c + (j & 1)) % 2
            src = src_all if gather else src_all.at[4 * px + 2 * py + pc]
            copies.append(pltpu.make_async_remote_copy(
                src_ref=src, dst_ref=dst_all.at[me], send_sem=send_sems.at[a, j - 1],
                recv_sem=recv_sems.at[a, j - 1], device_id=(px, py, pc), device_id_type=pl.DeviceIdType.MESH))
    return copies


def adamw(name, w, parts, m, v, tile):
    rows, cols = w.shape
    spec = pl.BlockSpec((tile, cols), lambda i: (i, 0))
    part_spec = pl.BlockSpec((N_DEV, tile, cols), lambda i: (0, i, 0))

    def body(w_ref, p_ref, m_ref, v_ref, g_ref, d_ref, nm_ref, nv_ref):
        g = p_ref[0].astype(F32)
        for s in range(1, N_DEV):
            g = g + p_ref[s].astype(F32)
        new_m = ADAM_B1 * m_ref[...] + (1.0 - ADAM_B1) * g
        new_v = ADAM_B2 * v_ref[...] + (1.0 - ADAM_B2) * (g * g)
        m_hat = new_m / (1.0 - ADAM_B1 ** ADAM_STEP)
        v_hat = new_v / (1.0 - ADAM_B2 ** ADAM_STEP)
        g_ref[...] = g
        d_ref[...] = -ADAM_LR * (m_hat / (jnp.sqrt(v_hat) + ADAM_EPS) + ADAM_WD * w_ref[...])
        nm_ref[...] = new_m
        nv_ref[...] = new_v

    shape = jax.ShapeDtypeStruct((rows, cols), F32)
    return pl.pallas_call(
        body, name=name, grid=(rows // tile,), in_specs=[spec, part_spec, spec, spec],
        out_specs=[spec] * 4, out_shape=[shape] * 4, compiler_params=_params(("arbitrary",)),
    )(w, parts, m, v)


SMALL = ("ln1_g", "tok_mu", "w0", "a0", "k_k", "k_a", "r_k", "gn_w", "gn_b", "sb_gain", "ln2_g", "lnf_g")
EARLY = ("w_in", "w_decay_up", "w_aaa_up", "w_gate_up")
LATE = ("w_out", "w_up", "w_down")
BIG = EARLY + LATE
ORDER = ("ln1_g", "w_in", "tok_mu", "w0", "w_decay_up", "a0", "w_aaa_up", "w_gate_up", "k_k", "k_a", "r_k",
         "gn_w", "gn_b", "sb_gain", "w_out", "ln2_g", "w_up", "w_down", "lnf_g")


def _pack(vectors, rows):
    flat = jnp.concatenate([v.reshape(-1).astype(F32) for v in vectors])
    return jnp.pad(flat, (0, rows * LANES - flat.shape[0])).reshape(rows, LANES)


def _full_cols(gathered):
    d, k, cols = gathered.shape
    return jnp.transpose(gathered, (1, 0, 2)).reshape(k, d * cols)


def _col_parts(full):
    k, n = full.shape
    return jnp.transpose(full.reshape(k, N_DEV, n // N_DEV), (1, 0, 2))


def kernel(x, ln1_g, w_in, tok_mu, w0, w_decay_up, a0, w_aaa_up, w_gate_up, k_k, k_a, r_k, gn_w, gn_b, sb_gain, w_out, ln2_g, w_up, w_down, lnf_g, loss_target, m_ln1_g, m_w_in, m_tok_mu, m_w0, m_w_decay_up, m_a0, m_w_aaa_up, m_w_gate_up, m_k_k, m_k_a, m_r_k, m_gn_w, m_gn_b, m_sb_gain, m_w_out, m_ln2_g, m_w_up, m_w_down, m_lnf_g, v_ln1_g, v_w_in, v_tok_mu, v_w0, v_w_decay_up, v_a0, v_w_aaa_up, v_w_gate_up, v_k_k, v_k_a, v_r_k, v_gn_w, v_gn_b, v_sb_gain, v_w_out, v_ln2_g, v_w_up, v_w_down, v_lnf_g):
    args = dict(locals())
    weights = {n: args[n] for n in ORDER}
    mom_m = {n: args["m_" + n] for n in ORDER}
    mom_v = {n: args["v_" + n] for n in ORDER}

    n_seq, seq_len, d_model = x.shape
    n_tok = n_seq * seq_len
    x2d = x.reshape(n_tok, d_model)
    tgt = loss_target.reshape(n_tok, d_model)
    row = lambda t: t.reshape(1, -1).astype(F32)

    g1 = row(ln1_g)
    shard = {n: weights[n][0].astype(BF16) for n in BIG}
    h1, *early = rw_call("norm1", lambda r, q: [f_norm(r[0], q[0])], [x2d], [g1], [(d_model, BF16)], 512,
                         hosted=HostedChipGather([shard[n] for n in EARLY]))
    gathered = dict(zip(EARLY, early))
    w_in_f = _full_cols(gathered["w_in"])
    zeros64 = jnp.zeros((HEAD_DIM, RWKV_W), BF16)
    wd_pad = jnp.concatenate([_full_cols(gathered["w_decay_up"]), zeros64], axis=0)
    wa_pad = jnp.concatenate([zeros64, _full_cols(gathered["w_aaa_up"])], axis=0)
    wg_f = _full_cols(gathered["w_gate_up"])
    in_cols = w_in_f.shape[1]

    mu, w0r, a0r = row(tok_mu), row(w0), row(a0)
    kkr, kar, rkr, gwr, gbr, sgr = row(k_k), row(k_a), row(r_k), row(gn_w), row(gn_b), row(sb_gain)
    g2, gf = row(ln2_g), row(lnf_g)

    p = matmul("proj_in", h1, w_in_f, "nn", F32, 1024, in_cols // 2, d_model)
    prep_pars = [mu, w0r, wd_pad, a0r, wa_pad, wg_f]
    (r_, kraw, v_, lw, asig, gate), prep_edges = rwkv_prep_fwd(p, prep_pars, seq_len)
    by_seq = lambda t: t.reshape(n_seq, seq_len, t.shape[-1])
    flat = lambda t: t.reshape(n_tok, t.shape[-1])
    rwkv_in = [by_seq(t) for t in (r_, kraw, v_, lw, asig)]
    z, s0_all, tinv_all, (w_out_g, w_up_g) = rwkv_fwd(
        *rwkv_in, kkr, kar, rkr, gwr, gbr, n_seq, seq_len, Hosted([shard["w_out"], shard["w_up"]], True))
    y_sb, o_raw, tot, sb_first, w_down_g = sb_fwd(by_seq(p), sgr, n_seq, seq_len, Hosted([shard["w_down"]], True))
    w_out_f = w_out_g.reshape(d_model, d_model)
    w_up_f = _full_cols(w_up_g)
    w_down_f = w_down_g.reshape(-1, d_model)
    d_ff = w_up_f.shape[1]
    z, y_sb = flat(z), flat(y_sb)
    (ycat,) = rw_call("mix_cat", lambda r, q: [jnp.concatenate([r[0] * r[1], r[2]], axis=1)],
                      [z, gate, y_sb], [], [(d_model, BF16)], 512)
    x2, h2 = matmul("proj_out", ycat, w_out_f, "nn", (F32, BF16), 512, d_model, d_model, extras=[x2d], pars=[g2],
                    epilogue=lambda acc, xv, g: (xv + acc, f_norm(xv + acc, g)))
    u, act = matmul("mlp_up", h2, w_up_f, "nn", (F32, BF16), 1024, d_ff // 4, d_model,
                    epilogue=lambda acc: (acc, jnp.square(jnp.maximum(acc, 0.0))))

    def loss_epilogue(acc, x2v, target, g):
        loss_rows, vjp = jax.vjp(lambda xv, gv: f_final(xv, gv, target), x2v + acc, g)
        dx3, dgf = vjp(jnp.ones_like(loss_rows))
        return dx3, dx3, jnp.broadcast_to(jnp.sum(loss_rows), (1, LANES)), dgf

    dx3, dx3_b, loss_acc, d_lnf = matmul(
        "mlp_down", act, w_down_f, "nn", (F32, BF16), 512, d_model, d_ff, extras=[x2, tgt], pars=[gf],
        epilogue=loss_epilogue, sums=[((1, LANES), F32), ((1, d_model), F32)])

    du = matmul("d_act", dx3_b, w_down_f, "nt", BF16, 1024, d_ff // 4, d_model, extras=[u],
                epilogue=lambda acc, uv: (acc * (2.0 * jnp.maximum(uv, 0.0)),))
    dw_down = matmul("dw_down", act, dx3_b, "tn", BF16, 512, d_model, 4096)
    dw_up = matmul("dw_up", h2, du, "tn", BF16, d_model, 512, 4096)

    def norm_bwd(acc, xv, dres, g):
        _, vjp = jax.vjp(f_norm, xv, g)
        dx, dg = vjp(acc)
        return dx + dres, dx + dres, dg

    dx2, dx2_b, d_ln2 = matmul("d_h2", du, w_up_f, "nt", (F32, BF16), 512, d_model, d_ff, extras=[x2, dx3], pars=[g2],
                               epilogue=norm_bwd, sums=[((1, d_model), F32)])

    dycat = matmul("d_ycat", dx2_b, w_out_f, "nt", F32, 512, d_model, d_model)
    dw_out = matmul("dw_out", ycat, dx2_b, "tn", BF16, d_model, d_model, 2048)
    dq, dk_sb, dv_sb, d_sg = sb_bwd(by_seq(p), sgr, o_raw, tot, by_seq(dycat), sb_first, n_seq, seq_len)
    d_sb = [flat(dq), flat(dk_sb), flat(dv_sb)]
    late_grads = {"w_out": dw_out.reshape(N_DEV, -1, d_model), "w_up": _col_parts(dw_up),
                  "w_down": dw_down.reshape(N_DEV, -1, d_model)}
    (dr, dkraw, dv, dlw, dasig, d_kk, d_ka, d_rk, d_gw, d_gb), late_parts = rwkv_bwd(
        *rwkv_in, kkr, kar, rkr, gwr, gbr, s0_all, tinv_all, by_seq(dycat), by_seq(gate), n_seq, seq_len,
        Hosted([late_grads[n] for n in LATE], False))
    prep_cots = [flat(t) for t in (dr, dkraw, dv, dlw, dasig)] + [dycat, z]
    dp, (d_mu, d_w0, d_wd, d_a0, d_wa, d_wg) = rwkv_prep_bwd(p, prep_edges, prep_pars, prep_cots, d_sb, seq_len)

    dw_in = matmul("dw_in", h1, dp, "tn", BF16, d_model, in_cols // 2, 2048)
    lora_parts = lambda t: _col_parts(t).astype(BF16)
    early_grads = {"w_in": _col_parts(dw_in), "w_decay_up": lora_parts(d_wd[:HEAD_DIM]),
                   "w_aaa_up": lora_parts(d_wa[HEAD_DIM:]), "w_gate_up": lora_parts(d_wg)}
    dx, d_ln1, *early_parts = matmul(
        "d_h1", dp, w_in_f, "nt", (F32,), 512, d_model, in_cols, extras=[x2d, dx2], pars=[g1],
        epilogue=lambda *t: norm_bwd(*t)[1:], sums=[((1, d_model), F32)],
        hosted=Hosted([early_grads[n] for n in EARLY], False))
    parts = dict(zip(EARLY, early_parts))
    parts.update(zip(LATE, late_parts))

    small_grads = {"ln1_g": d_ln1, "tok_mu": d_mu, "w0": d_w0, "a0": d_a0, "k_k": d_kk, "k_a": d_ka, "r_k": d_rk,
                   "gn_w": d_gw, "gn_b": d_gb, "sb_gain": d_sg, "ln2_g": d_ln2, "lnf_g": d_lnf}
    n_small = sum(int(weights[n].size) for n in SMALL)
    pack_rows = -(-(n_small + 1) // (8 * LANES)) * 8
    packed = _pack([small_grads[n] for n in SMALL] + [loss_acc[0, :1]], pack_rows)
    (small_parts,) = exchange("gather_small", [packed], True)

    results = {}
    for n in BIG:
        w2d = weights[n][0]
        tile = w2d.shape[0] if w2d.shape[0] <= 256 else 256
        results[n] = adamw("adamw_" + n, w2d, parts[n], mom_m[n][0], mom_v[n][0], tile)
    pk = lambda d: _pack([d[n] for n in SMALL] + [jnp.zeros((1,), F32)], pack_rows)
    sg, sd, sm, sv = adamw("adamw_small", pk(weights), small_parts, pk(mom_m), pk(mom_v), pack_rows)
    off = 0
    for n in SMALL:
        size = int(weights[n].size)
        results[n] = tuple(t.reshape(-1)[off:off + size] for t in (sg, sd, sm, sv))
        off += size
    loss = sg.reshape(-1)[off]

    out = [loss, dx.reshape(x.shape)]
    for kind in range(4):
        out += [results[n][kind].reshape(weights[n].shape) for n in ORDER]
    return tuple(out)
```

```python
import functools
import math

import jax
import jax.numpy as jnp
from jax import lax
from jax.experimental import pallas as pl
from jax.experimental.pallas import tpu as pltpu

F32 = jnp.float32
BF16 = jnp.bfloat16

N_DEV = 8
HEAD_DIM = 64
LANES = 128
RWKV_W = 512
SB_W = 512
LORA_WA = 128
GATE_LORA = 128
RWKV_COLS = 3 * RWKV_W + LORA_WA + GATE_LORA
RMS_EPS = 1e-5
GN_EPS = 64e-5
CHUNK = 64
QB = 256
SB_SCALE = HEAD_DIM ** -0.5
ADAM_LR, ADAM_B1, ADAM_B2, ADAM_EPS, ADAM_WD, ADAM_STEP = 0.001, 0.9, 0.999, 1e-08, 0.01, 10
VMEM_LIMIT = 56 * 1024 * 1024


_DIMS = {
    "nn": (((1,), (0,)), ((), ())),
    "nt": (((1,), (1,)), ((), ())),
    "tn": (((0,), (0,)), ((), ())),
}


def _pieces(x, n):
    if n == 1:
        return [x.astype(BF16)]
    out, rem = [], x.astype(F32)
    for i in range(n):
        p = rem.astype(BF16)
        out.append(p)
        if i + 1 < n:
            rem = rem - p.astype(F32)
    return out


def _dot(a, b, form, pa, pb):
    pieces_a, pieces_b = _pieces(a, pa), _pieces(b, pb)
    keep = max(pa, pb)
    acc = None
    for i, ai in enumerate(pieces_a):
        for j, bj in enumerate(pieces_b):
            if i + j >= keep:
                continue
            t = lax.dot_general(ai, bj, _DIMS[form], preferred_element_type=F32)
            acc = t if acc is None else acc + t
    return acc


BOTH = (True, True)


@functools.partial(jax.custom_vjp, nondiff_argnums=(2, 3, 4, 5, 6))
def mm(a, b, form, pa, pb, diff=BOTH, grad_pieces=None):
    return _dot(a, b, form, pa, pb)


def _mm_fwd(a, b, form, pa, pb, diff, grad_pieces):
    return _dot(a, b, form, pa, pb), (a, b)


def _mm_bwd(form, pa, pb, diff, grad_pieces, res, g):
    a, b = res
    pg = grad_pieces or max(pa, pb)
    da, db = jnp.zeros_like(a), jnp.zeros_like(b)
    if form == "nn":
        if diff[0]:
            da = mm(g, b, "nt", pg, pb)
        if diff[1]:
            db = mm(a, g, "tn", pa, pg)
    elif form == "nt":
        if diff[0]:
            da = mm(g, b, "nn", pg, pb)
        if diff[1]:
            db = mm(g, a, "tn", pg, pa)
    else:
        if diff[0]:
            da = mm(b, g, "nt", pb, pg)
        if diff[1]:
            db = mm(a, g, "nn", pa, pg)
    return da, db


mm.defvjp(_mm_fwd, _mm_bwd)


def _stack_rows(top, bottom):
    return jnp.concatenate([top, bottom], axis=0)


@jax.custom_vjp
def _split_rows(x):
    half = x.shape[0] // 2
    return x[:half], x[half:]


def _split_rows_fwd(x):
    return _split_rows(x), None


def _split_rows_bwd(_, grads):
    return (_stack_rows(*grads),)


_split_rows.defvjp(_split_rows_fwd, _split_rows_bwd)


def _lane_lo(shape):
    return lax.broadcasted_iota(jnp.int32, shape, len(shape) - 1) < HEAD_DIM


def _segsum(x):
    lo = _lane_lo(x.shape)
    s_lo = jnp.sum(jnp.where(lo, x, 0.0), axis=-1, keepdims=True)
    s_hi = jnp.sum(jnp.where(lo, 0.0, x), axis=-1, keepdims=True)
    return jnp.where(lo, s_lo, s_hi)


def _sigmoid(x):
    return 0.5 * (jnp.tanh(0.5 * x) + 1.0)


@jax.custom_vjp
def _log_one_minus_sigmoid(z):
    return -jnp.maximum(z, 0.0) - jnp.log(1.0 + jnp.exp(-jnp.abs(z)))


def _log_oms_fwd(z):
    out = _log_one_minus_sigmoid(z)
    return out, (z, out)


def _log_oms_bwd(res, g):
    z, out = res
    return (-g * jnp.exp(z + out),)


_log_one_minus_sigmoid.defvjp(_log_oms_fwd, _log_oms_bwd)


def f_norm(x, g):
    return x * lax.rsqrt(jnp.mean(x * x, axis=-1, keepdims=True) + RMS_EPS) * g


def f_prep(p, pprev, mu, w0, wd_pad, a0, wa_pad, wg):
    pm = p + mu * (pprev - p)
    r = pm[:, 0:RWKV_W]
    k = pm[:, RWKV_W:2 * RWKV_W]
    v = pm[:, 2 * RWKV_W:3 * RWKV_W]
    xwa = pm[:, 3 * RWKV_W:3 * RWKV_W + LORA_WA]
    xg = pm[:, 3 * RWKV_W + LORA_WA:RWKV_COLS]
    w = _log_one_minus_sigmoid(-(w0 + mm(jnp.tanh(xwa), wd_pad, "nn", 1, 1))) - 0.5
    lw = -jnp.exp(w)
    asig = _sigmoid(a0 + mm(xwa, wa_pad, "nn", 1, 1))
    gate = mm(_sigmoid(xg), wg, "nn", 1, 1)
    return r, k, v, lw, asig, gate


def _tri(n, kind):
    row = lax.broadcasted_iota(jnp.int32, (n, n), 0)
    col = lax.broadcasted_iota(jnp.int32, (n, n), 1)
    if kind == "lower_incl":
        return row >= col
    return row > col


@functools.partial(jax.custom_vjp, nondiff_argnums=(1,))
def _nilpotent_inverses(mats, hp):
    size = mats[0].shape[0]
    eye = (lax.broadcasted_iota(jnp.int32, (size, size), 0) == lax.broadcasted_iota(jnp.int32, (size, size), 1))
    tinv = [eye.astype(F32) + x for x in mats]
    pw = [mm(x, x, "nn", hp, hp) for x in mats]
    for _ in range(int(math.log2(size)) - 2):
        both = [_split_rows(mm(_stack_rows(t, x), x, "nn", hp, hp)) for t, x in zip(tinv, pw)]
        tinv = [t + tx for t, (tx, _) in zip(tinv, both)]
        pw = [xx for _, xx in both]
    return [t + mm(t, x, "nn", hp, hp) for t, x in zip(tinv, pw)]


def _nilpotent_inverses_fwd(mats, hp):
    tinv = _nilpotent_inverses(mats, hp)
    return tinv, tinv


def _nilpotent_inverses_bwd(hp, tinv, grads):
    right = [mm(g, t, "nt", hp, hp) for g, t in zip(grads, tinv)]
    return ([mm(t, x, "tn", hp, hp) for t, x in zip(tinv, right)],)


_nilpotent_inverses.defvjp(_nilpotent_inverses_fwd, _nilpotent_inverses_bwd)


@functools.partial(jax.custom_vjp, nondiff_argnums=(2,))
def _known_inverses(mats, tinv, hp):
    return list(tinv)


def _known_inverses_fwd(mats, tinv, hp):
    return list(tinv), tinv


def _known_inverses_bwd(hp, tinv, grads):
    return _nilpotent_inverses_bwd(hp, tinv, grads) + ([jnp.zeros_like(t) for t in tinv],)


_known_inverses.defvjp(_known_inverses_fwd, _known_inverses_bwd)


def rwkv_chunk(state, r, kraw, v, lw, asig, k_k, k_a, r_k, gn_w, gn_b, hp, saved_tinv=None, with_tinv=False):
    n = len(r)
    L = r[0].shape[0]
    lo = _lane_lo((1, LANES))
    masks = (lo, jnp.logical_not(lo))
    incl = _tri(L, "lower_incl")
    strict = _tri(L, "strict")
    tri = incl.astype(F32)
    kk = [x * w for x, w in zip(kraw, k_k)]
    kk = [x / jnp.maximum(jnp.sqrt(_segsum(x * x)), 1e-12) for x in kk]
    k = [x * (1.0 + (s - 1.0) * w) for x, s, w in zip(kraw, asig, k_a)]
    b = [x * s for x, s in zip(kk, asig)]
    c = [mm(tri, x, "nn", 1, 3, (False, True)) for x in lw]
    at = [-x * jnp.exp(ci - li) for x, ci, li in zip(kk, c, lw)]
    rt = [x * jnp.exp(ci) for x, ci in zip(r, c)]
    einv = [jnp.exp(-ci) for ci in c]
    bt = [x * e for x, e in zip(b, einv)]
    kt = [x * e for x, e in zip(k, einv)]
    inst = [(s, m) for s in range(n) for m in masks]
    ar_h = [_stack_rows(jnp.where(m, at[s], 0.0), jnp.where(m, rt[s], 0.0)) for s, m in inst]
    on_b = [_split_rows(mm(x, bt[s], "nt", hp, hp)) for x, (s, _) in zip(ar_h, inst)]
    on_k = [_split_rows(mm(x, kt[s], "nt", hp, hp)) for x, (s, _) in zip(ar_h, inst)]
    a_ab = [jnp.where(strict, x, 0.0) for x, _ in on_b]
    b_rb = [jnp.where(incl, x, 0.0) for _, x in on_b]
    a_ak = [jnp.where(strict, x, 0.0) for x, _ in on_k]
    b_rk = [jnp.where(incl, x, 0.0) for _, x in on_k]
    tinv = _nilpotent_inverses(a_ab, hp) if saved_tinv is None else _known_inverses(a_ab, saved_tinv, hp)
    on_state = [_split_rows(mm(x, state[s], "nt", hp, hp)) for x, (s, _) in zip(ar_h, inst)]
    on_v = [_split_rows(mm(_stack_rows(m1, m2), v[s], "nn", hp, hp)) for m1, m2, (s, _) in zip(a_ak, b_rk, inst)]
    u_h = [mm(t, sa + av, "nn", hp, hp) for t, (sa, _), (av, _) in zip(tinv, on_state, on_v)]
    y_h = [sr + mm(m1, u, "nn", hp, hp) + bv for (_, sr), m1, u, (_, bv) in zip(on_state, b_rb, u_h, on_v)]
    u_all = [jnp.where(lo, u_h[2 * s], u_h[2 * s + 1]) for s in range(n)]
    y_all = [jnp.where(lo, y_h[2 * s], y_h[2 * s + 1]) for s in range(n)]
    c_last = [jnp.sum(x, axis=0, keepdims=True) for x in lw]
    efwd = [jnp.exp(cl - ci) for cl, ci in zip(c_last, c)]
    new_state = [st * jnp.exp(cl) + mm(_stack_rows(u, vi), _stack_rows(bi * e, ki * e), "tn", hp, hp)
                 for st, cl, u, bi, e, vi, ki in zip(state, c_last, u_all, b, efwd, v, k)]
    row_head = lax.broadcasted_iota(jnp.int32, (LANES, LANES), 0) // HEAD_DIM
    col_head = lax.broadcasted_iota(jnp.int32, (LANES, LANES), 1) // HEAD_DIM
    new_state = [jnp.where(row_head == col_head, x, 0.0) for x in new_state]
    outs = []
    for y, ri, ki, vi, w_rk, w_gw, w_gb in zip(y_all, r, k, v, r_k, gn_w, gn_b):
        mean = _segsum(y) * (1.0 / HEAD_DIM)
        d = y - mean
        var = _segsum(d * d) * (1.0 / HEAD_DIM)
        yn = d * lax.rsqrt(var + GN_EPS) * w_gw + w_gb
        outs.append(yn + _segsum(ri * ki * w_rk) * vi)
    return (outs, new_state, tinv) if with_tinv else (outs, new_state)


def sb_tile(q, k, v, c_lo, c_hi, diag, from_here=None):
    n = len(q)
    lo = _lane_lo((1, LANES))
    below = _tri(QB, "strict")
    from_s = _tri(QB, "lower_incl").astype(F32)
    inst = [(s, h) for s in range(n) for h in (0, 1)]
    carry = [(c_lo[s], c_hi[s])[h] for s, h in inst]
    z = [mm(q[s][h], k[s], "nt", 1, 1) for s, h in inst]
    log_keep = [_log_one_minus_sigmoid(x) for x in z]
    if diag:
        log_keep = [jnp.where(below, x, 0.0) for x in log_keep]
    own = [jnp.sum(x, axis=1, keepdims=True) for x in log_keep]
    if from_here is not None:
        carry = [lax.stop_gradient(from_here[s][h] - o) + cr for (s, h), o, cr in zip(inst, own, carry)]
    tail = [mm(x, from_s, "nn", SB_SUM_PIECES, 1, (True, False), 1) for x in log_keep]
    log_a = [x + tl + cr for x, tl, cr in zip(z, tail, carry)]
    if diag:
        log_a = [jnp.where(below, x, -1e30) for x in log_a]
    att = [jnp.exp(x) for x in log_a]
    out_h = [mm(x, v[s], "nn", 1, 1) for x, (s, _) in zip(att, inst)]
    out = [jnp.where(lo, out_h[2 * s], out_h[2 * s + 1]) for s in range(n)]
    return out, [own[2 * s] for s in range(n)], [own[2 * s + 1] for s in range(n)]


def sb_split_q(q):
    lo = _lane_lo((1, LANES))
    qs = q * SB_SCALE
    return jnp.where(lo, qs, 0.0), jnp.where(lo, 0.0, qs)


def sb_post(o, gain):
    return o * lax.rsqrt(_segsum(o * o) * (1.0 / HEAD_DIM) + RMS_EPS) * gain


def f_final(x3, g, target):
    y = f_norm(x3, g)
    err = y - target
    return 0.5 * jnp.mean(err * err, axis=-1, keepdims=True)


def _params(sem):
    return pltpu.CompilerParams(dimension_semantics=sem, vmem_limit_bytes=VMEM_LIMIT)


def rw_call(name, body_fn, rows, pars, out_rows, tile, hosted=None):
    n_rows = rows[0].shape[0]
    tile = min(tile, n_rows)
    steps = n_rows // tile
    row_specs = [pl.BlockSpec((tile, arr.shape[1]), lambda i: (i, 0)) for arr in rows]
    par_specs = [pl.BlockSpec(p.shape, lambda i: (0, 0)) for p in pars]
    nr, npar, nor = len(rows), len(pars), len(out_rows)
    n_host = hosted.n if hosted is not None else 0

    def body(*refs):
        if hosted is not None:
            refs, copies = hosted.split(refs, nr + npar, nor)
            hosted.run(copies, pl.program_id(0) == 0, pl.program_id(0) == steps - 1)
        row_outs = body_fn([r[...] for r in refs[:nr]], [r[...] for r in refs[nr:nr + npar]])
        for ref, val in zip(refs[nr + npar:], row_outs):
            ref[...] = val.astype(ref.dtype)

    outs = pl.pallas_call(
        body, name=name, grid=(steps,), in_specs=row_specs + par_specs + [ANY_SPEC] * n_host,
        out_specs=[pl.BlockSpec((tile, c), lambda i: (i, 0)) for c, _ in out_rows] + [ANY_SPEC] * n_host,
        out_shape=[jax.ShapeDtypeStruct((n_rows, c), dt) for c, dt in out_rows]
        + (exchange_shapes(hosted.arrays, hosted.gather) if n_host else []),
        scratch_shapes=exchange_sems(n_host) if n_host else [],
        compiler_params=_params(("arbitrary",)),
    )(*rows, *pars, *(hosted.arrays if n_host else []))
    return outs


def matmul(name, a, b, form, out_dtype, tm, tn, tk, extras=(), pars=(), epilogue=None, sums=(), hosted=None):
    out_dtypes = out_dtype if isinstance(out_dtype, tuple) else (out_dtype,)
    tm, tn, tk = min(tm, a.shape[1 if form == "tn" else 0]), min(tn, b.shape[0 if form == "nt" else 1]), min(tk, a.shape[0 if form == "tn" else 1])
    if form == "nn":
        (m, kd), n = a.shape, b.shape[1]
        a_spec = pl.BlockSpec((tm, tk), lambda i, j, k: (i, k))
        b_spec = pl.BlockSpec((tk, tn), lambda i, j, k: (k, j))
    elif form == "nt":
        (m, kd), n = a.shape, b.shape[0]
        a_spec = pl.BlockSpec((tm, tk), lambda i, j, k: (i, k))
        b_spec = pl.BlockSpec((tn, tk), lambda i, j, k: (j, k))
    else:
        (kd, m), n = a.shape, b.shape[1]
        a_spec = pl.BlockSpec((tk, tm), lambda i, j, k: (k, i))
        b_spec = pl.BlockSpec((tk, tn), lambda i, j, k: (k, j))
    ksteps = kd // tk

    n_extra, n_par, n_out, n_sum = len(extras), len(pars), len(out_dtypes), len(sums)
    n_in = 2 + n_extra + n_par
    grid = (m // tm, n // tn, ksteps)

    def body(*refs):
        if hosted is not None:
            refs, copies = hosted.split(refs, n_in, n_out + n_sum)
            here = [pl.program_id(d) for d in range(3)]
            hosted.run(copies, functools.reduce(jnp.logical_and, [h == 0 for h in here]),
                       functools.reduce(jnp.logical_and, [h == g - 1 for h, g in zip(here, grid)]))
        a_ref, b_ref, rest = refs[0], refs[1], refs[2:]
        e_refs, o_refs = rest[:n_extra + n_par], rest[n_extra + n_par:n_extra + n_par + n_out]
        s_refs = rest[n_extra + n_par + n_out:n_extra + n_par + n_out + n_sum]
        kstep = pl.program_id(2)
        part = lax.dot_general(a_ref[...].astype(BF16), b_ref[...].astype(BF16), _DIMS[form],
                               preferred_element_type=F32)

        def finish(acc):
            outs = epilogue(acc, *[r[...] for r in e_refs]) if epilogue else (acc,)
            for ref, val in zip(o_refs, outs[:n_out]):
                ref[...] = val.astype(ref.dtype)
            if n_sum:
                first_tile = jnp.logical_and(pl.program_id(0) == 0, pl.program_id(1) == 0)

                @pl.when(first_tile)
                def _():
                    for ref, val in zip(s_refs, outs[n_out:]):
                        ref[...] = val.astype(ref.dtype)

                @pl.when(jnp.logical_not(first_tile))
                def _():
                    for ref, val in zip(s_refs, outs[n_out:]):
                        ref[...] = ref[...] + val.astype(ref.dtype)

        if ksteps == 1:
            finish(part)
            return
        acc_ref = rest[n_extra + n_par + n_out + n_sum]

        @pl.when(kstep == 0)
        def _():
            acc_ref[...] = part

        @pl.when(jnp.logical_and(kstep > 0, kstep < ksteps - 1))
        def _():
            acc_ref[...] = acc_ref[...] + part

        @pl.when(kstep == ksteps - 1)
        def _():
            finish(acc_ref[...] + part)

    out_spec = pl.BlockSpec((tm, tn), lambda i, j, k: (i, j))
    whole = lambda shape: pl.BlockSpec(shape, lambda i, j, k: (0,) * len(shape))
    n_host = hosted.n if hosted is not None else 0
    host_in = list(hosted.arrays) if hosted is not None else []
    host_out = exchange_shapes(hosted.arrays, hosted.gather) if hosted is not None else []
    outs = pl.pallas_call(
        body, name=name, grid=grid,
        in_specs=[a_spec, b_spec] + [out_spec] * n_extra + [whole(t.shape) for t in pars] + [ANY_SPEC] * n_host,
        out_specs=[out_spec] * n_out + [whole(s) for s, _ in sums] + [ANY_SPEC] * n_host,
        out_shape=[jax.ShapeDtypeStruct((m, n), dt) for dt in out_dtypes]
        + [jax.ShapeDtypeStruct(s, dt) for s, dt in sums] + host_out,
        scratch_shapes=([pltpu.VMEM((tm, tn), F32)] if ksteps > 1 else []) + (exchange_sems(n_host) if n_host else []),
        compiler_params=_params(("arbitrary",) * 3 if (sums or n_host) else ("parallel", "parallel", "arbitrary")),
    )(a, b, *extras, *pars, *host_in)
    return outs if (isinstance(out_dtype, tuple) or sums or n_host) else outs[0]


PREP_TILE = 512
PREP_TILE_BWD = 256
SUBLANES = 8


def _shift_in(rows, first):
    rolled = pltpu.roll(rows, 1, 0)
    row = lax.broadcasted_iota(jnp.int32, (SUBLANES, rows.shape[1]), 0)
    head = jnp.where(row == 0, first, rolled[0:SUBLANES])
    return jnp.concatenate([head, rolled[SUBLANES:]], axis=0), rolled


def rwkv_prep_fwd(p, pars, seq_len):
    n_tok = p.shape[0]
    tile = min(PREP_TILE, seq_len)
    tile_b = min(PREP_TILE_BWD, tile)
    steps, per_seq, sub = n_tok // tile, seq_len // tile, tile // tile_b
    n_par = len(pars)

    def body(p_ref, *rest):
        par_refs, out_refs, edge_ref, last8 = rest[:n_par], rest[n_par:n_par + 6], rest[n_par + 6], rest[n_par + 7]
        step = pl.program_id(0)

        @pl.when(step == 0)
        def _():
            last8[...] = jnp.zeros_like(last8)

        rows = p_ref[...]
        before = jnp.where(step % per_seq == 0, 0.0, pltpu.roll(last8[...], 1, 0))
        prev, rolled = _shift_in(rows, before)
        edge_ref[0] = prev[0:SUBLANES]
        for m in range(1, sub):
            edge_ref[m] = rolled[m * tile_b:m * tile_b + SUBLANES]
        last8[...] = rows[tile - SUBLANES:tile]
        for ref, val in zip(out_refs, f_prep(rows, prev, *[r[...] for r in par_refs])):
            ref[...] = val

    row_out = pl.BlockSpec((tile, RWKV_W), lambda i: (i, 0))
    outs = pl.pallas_call(
        body, name="rwkv_prep", grid=(steps,),
        in_specs=[pl.BlockSpec((tile, RWKV_COLS), lambda i: (i, 0))] + [pl.BlockSpec(t.shape, lambda i: (0, 0)) for t in pars],
        out_specs=[row_out] * 6 + [pl.BlockSpec((sub, SUBLANES, RWKV_COLS), lambda i: (i, 0, 0))],
        out_shape=[jax.ShapeDtypeStruct((n_tok, RWKV_W), F32)] * 6
        + [jax.ShapeDtypeStruct((steps * sub, SUBLANES, RWKV_COLS), F32)],
        scratch_shapes=[pltpu.VMEM((SUBLANES, RWKV_COLS), F32)],
        compiler_params=_params(("arbitrary",)),
    )(p, *pars)
    return outs[:6], outs[6]


def rwkv_prep_bwd(p, edges, pars, cots, d_sb, seq_len):
    n_tok = p.shape[0]
    tile = min(PREP_TILE_BWD, seq_len)
    steps, per_seq = n_tok // tile, seq_len // tile
    n_par = len(pars)
    back = lambda i: steps - 1 - i

    def body(p_ref, edge_ref, *rest):
        par_refs, rest = rest[:n_par], rest[n_par:]
        cot_refs, dy_ref, z_ref, sb_refs = rest[:5], rest[5], rest[6], rest[7:10]
        dp_ref, acc_refs, next8 = rest[10], rest[11:11 + n_par], rest[11 + n_par]
        step = pl.program_id(0)
        first = step == 0

        @pl.when(first)
        def _():
            next8[...] = jnp.zeros_like(next8)

        rows = p_ref[...]
        prev, _ = _shift_in(rows, edge_ref[0])
        _, vjp = jax.vjp(f_prep, rows, prev, *[r[...].astype(F32) for r in par_refs])
        grads = vjp(tuple(r[...] for r in cot_refs) + (dy_ref[...] * z_ref[...],))
        d_rows, d_prev = grads[0], grads[1]
        up = pltpu.roll(d_prev, tile - 1, 0)
        ends_seq = back(step) % per_seq == per_seq - 1
        after = jnp.where(ends_seq, 0.0, pltpu.roll(next8[...], SUBLANES - 1, 0))
        row = lax.broadcasted_iota(jnp.int32, (SUBLANES, RWKV_COLS), 0)
        tail = jnp.where(row == SUBLANES - 1, after, up[tile - SUBLANES:tile])
        d_rows = d_rows + jnp.concatenate([up[:tile - SUBLANES], tail], axis=0)
        next8[...] = d_prev[0:SUBLANES]
        dp_ref[...] = jnp.concatenate([d_rows] + [r[...] for r in sb_refs], axis=1).astype(dp_ref.dtype)

        @pl.when(first)
        def _():
            for ref, val in zip(acc_refs, grads[2:]):
                ref[...] = val

        @pl.when(jnp.logical_not(first))
        def _():
            for ref, val in zip(acc_refs, grads[2:]):
                ref[...] = ref[...] + val

    cols = RWKV_COLS + sum(t.shape[1] for t in d_sb)
    half = pl.BlockSpec((tile, RWKV_W), lambda i: (back(i), 0))
    par_specs = [pl.BlockSpec(t.shape, lambda i: (0, 0)) for t in pars]
    outs = pl.pallas_call(
        body, name="d_rwkv_prep", grid=(steps,),
        in_specs=[pl.BlockSpec((tile, RWKV_COLS), lambda i: (back(i), 0)),
                  pl.BlockSpec((1, SUBLANES, RWKV_COLS), lambda i: (back(i), 0, 0))] + par_specs + [half] * 10,
        out_specs=[pl.BlockSpec((tile, cols), lambda i: (back(i), 0))] + par_specs,
        out_shape=[jax.ShapeDtypeStruct((n_tok, cols), BF16)] + [jax.ShapeDtypeStruct(t.shape, F32) for t in pars],
        scratch_shapes=[pltpu.VMEM((SUBLANES, RWKV_COLS), F32)],
        compiler_params=_params(("arbitrary",)),
    )(p, edges, *pars, *cots, *d_sb)
    return outs[0], outs[1:]


RWKV_HP = 1


RWKV_PAIRS = 4


def _rwkv_specs(n_seq, chunk_of):
    width = RWKV_PAIRS * LANES
    row = pl.BlockSpec((n_seq, CHUNK, width), lambda g, c: (0, chunk_of(c), g))
    par = pl.BlockSpec((1, width), lambda g, c: (0, g))
    s0 = pl.BlockSpec((1, 1, RWKV_PAIRS * n_seq, LANES, LANES), lambda g, c: (g, chunk_of(c), 0, 0, 0))
    tinv = pl.BlockSpec((1, 1, 2 * RWKV_PAIRS * n_seq, CHUNK, CHUNK), lambda g, c: (g, chunk_of(c), 0, 0, 0))
    return row, par, s0, tinv


class Hosted:
    def __init__(self, arrays, gather):
        self.arrays, self.gather, self.n = list(arrays), gather, len(arrays)

    def split(self, refs, n_in, n_out):
        n = self.n
        ins, outs, sems = refs[n_in:n_in + n], refs[n_in + n + n_out:n_in + 2 * n + n_out], refs[-3:]
        own = refs[:n_in] + refs[n_in + n:n_in + n + n_out] + refs[n_in + 2 * n + n_out:-3]
        return own, exchange_copies(ins, outs, *sems, self.gather)

    def run(self, copies, first, last):
        @pl.when(first)
        def _():
            for cp in copies:
                cp.start()

        @pl.when(last)
        def _():
            for cp in copies:
                cp.wait()


class HostedChipGather(Hosted):
    def __init__(self, arrays):
        super().__init__(arrays, True)

    def split(self, refs, n_in, n_out):
        n = self.n
        ins, outs, sems = refs[n_in:n_in + n], refs[n_in + n + n_out:n_in + 2 * n + n_out], refs[-3:]
        own = refs[:n_in] + refs[n_in + n:n_in + n + n_out] + refs[n_in + 2 * n + n_out:-3]
        return own, (ins, outs, sems)

    def run(self, state, first, last):
        ins, outs, (send_sems, recv_sems, local_sems) = state
        x, y, c = lax.axis_index("x"), lax.axis_index("y"), lax.axis_index("c")
        chips = [((x + 1) % 2, y), (x, (y + 1) % 2), ((x + 1) % 2, (y + 1) % 2)]

        def block(a, k, of, to, src=None):
            dst = outs[a].at[4 * of[0] + 2 * of[1] + of[2]]
            return pltpu.make_async_remote_copy(
                src_ref=dst if src is None else src, dst_ref=dst, send_sem=send_sems.at[a, k],
                recv_sem=recv_sems.at[a, k], device_id=to, device_id_type=pl.DeviceIdType.MESH)

        me, sibling = (x, y, c), (x, y, 1 - c)
        local = [pltpu.make_async_copy(ins[a], outs[a].at[4 * x + 2 * y + c], local_sems.at[a]) for a in range(self.n)]
        mine = [block(a, 0, me, sibling, ins[a]) for a in range(self.n)]
        mine += [block(a, 1 + j, me, (*chip, c), ins[a]) for a in range(self.n) for j, chip in enumerate(chips)]
        passed = [block(a, 4 + j, (*chip, c), sibling) for a in range(self.n) for j, chip in enumerate(chips)]

        @pl.when(first)
        def _():
            for cp in local + mine:
                cp.start()

        @pl.when(last)
        def _():
            for a in range(self.n):
                for j, chip in enumerate(chips):
                    block(a, 1 + j, (*chip, c), me).wait_recv()
                    passed[a * len(chips) + j].start()
            for a in range(self.n):
                block(a, 0, sibling, me).wait_recv()
                for j, chip in enumerate(chips):
                    block(a, 4 + j, (*chip, 1 - c), me).wait_recv()
            for cp in mine + passed:
                cp.wait_send()
            for cp in local:
                cp.wait()


def rwkv_fwd(r, kraw, v, lw, asig, k_k, k_a, r_k, gn_w, gn_b, n_seq, seq_len, hosted):
    n_chunks = seq_len // CHUNK
    n_groups = RWKV_W // (RWKV_PAIRS * LANES)
    n_inst = RWKV_PAIRS * n_seq
    row, par, s0_spec, tinv_spec = _rwkv_specs(n_seq, lambda c: c)
    inst = [(s, pl.ds(pp * LANES, LANES)) for pp in range(RWKV_PAIRS) for s in range(n_seq)]

    def body(*refs):
        own, copies = hosted.split(refs, 10, 3)
        row_refs, par_refs, (z_ref, s0_ref, tinv_ref, state) = own[:5], own[5:10], own[10:]
        step = pl.program_id(0) * n_chunks + pl.program_id(1)
        hosted.run(copies, step == 0, step == n_groups * n_chunks - 1)

        @pl.when(pl.program_id(1) == 0)
        def _():
            state[...] = jnp.zeros_like(state)

        s0 = [state[i] for i in range(n_inst)]
        rows = [[ref[s, :, lanes] for s, lanes in inst] for ref in row_refs]
        pars = [[ref[:, lanes] for _, lanes in inst] for ref in par_refs]
        z, s1, tinv = rwkv_chunk(s0, *rows, *pars, RWKV_HP, with_tinv=True)
        for i, (s, lanes) in enumerate(inst):
            s0_ref[0, 0, i] = s0[i]
            z_ref[s, :, lanes] = z[i]
            state[i] = s1[i]
        for i, t in enumerate(tinv):
            tinv_ref[0, 0, i] = t

    outs = pl.pallas_call(
        body, name="rwkv_fwd", grid=(n_groups, n_chunks),
        in_specs=[row] * 5 + [par] * 5 + [ANY_SPEC] * hosted.n,
        out_specs=[row, s0_spec, tinv_spec] + [ANY_SPEC] * hosted.n,
        out_shape=[jax.ShapeDtypeStruct(r.shape, F32),
                   jax.ShapeDtypeStruct((n_groups, n_chunks, n_inst, LANES, LANES), F32),
                   jax.ShapeDtypeStruct((n_groups, n_chunks, 2 * n_inst, CHUNK, CHUNK), F32)]
        + exchange_shapes(hosted.arrays, hosted.gather),
        scratch_shapes=[pltpu.VMEM((n_inst, LANES, LANES), F32)] + exchange_sems(hosted.n),
        compiler_params=_params(("arbitrary", "arbitrary")),
    )(r, kraw, v, lw, asig, k_k, k_a, r_k, gn_w, gn_b, *hosted.arrays)
    return outs[0], outs[1], outs[2], outs[3:]


def rwkv_bwd(r, kraw, v, lw, asig, k_k, k_a, r_k, gn_w, gn_b, s0_all, tinv_all, dy, gate, n_seq, seq_len, hosted):
    n_chunks = seq_len // CHUNK
    n_groups = RWKV_W // (RWKV_PAIRS * LANES)
    n_inst = RWKV_PAIRS * n_seq
    row, par, s0_spec, tinv_spec = _rwkv_specs(n_seq, lambda c: n_chunks - 1 - c)
    inst = [(s, pl.ds(pp * LANES, LANES)) for pp in range(RWKV_PAIRS) for s in range(n_seq)]

    def body(*refs):
        own, copies = hosted.split(refs, 14, 10)
        row_refs, par_refs, (s0_ref, tinv_ref, dy_ref, gate_ref) = own[:5], own[5:10], own[10:14]
        drow_refs, dpar_refs, dstate = own[14:19], own[19:24], own[24]
        step = pl.program_id(0) * n_chunks + pl.program_id(1)
        hosted.run(copies, step == 0, step == n_groups * n_chunks - 1)
        first = pl.program_id(1) == 0

        @pl.when(first)
        def _():
            dstate[...] = jnp.zeros_like(dstate)

        fn = functools.partial(rwkv_chunk, hp=RWKV_HP, saved_tinv=[tinv_ref[0, 0, i] for i in range(2 * n_inst)])
        rows = [[ref[s, :, lanes] for s, lanes in inst] for ref in row_refs]
        pars = [[ref[:, lanes] for _, lanes in inst] for ref in par_refs]
        _, vjp = jax.vjp(fn, [s0_ref[0, 0, i] for i in range(n_inst)], *rows, *pars)
        dz = [dy_ref[s, :, lanes] * gate_ref[s, :, lanes] for s, lanes in inst]
        grads = vjp((dz, [dstate[i] for i in range(n_inst)]))
        for i, (s, lanes) in enumerate(inst):
            dstate[i] = grads[0][i]
            for ref, val in zip(drow_refs, grads[1:6]):
                ref[s, :, lanes] = val[i]

        def accumulate(start):
            for ref, val in zip(dpar_refs, grads[6:]):
                for pp in range(RWKV_PAIRS):
                    lanes = pl.ds(pp * LANES, LANES)
                    total = functools.reduce(jnp.add, val[pp * n_seq:(pp + 1) * n_seq])
                    ref[:, lanes] = total if start else ref[:, lanes] + total

        @pl.when(first)
        def _():
            accumulate(True)

        @pl.when(jnp.logical_not(first))
        def _():
            accumulate(False)

    rows_shape = jax.ShapeDtypeStruct(r.shape, F32)
    par_shape = jax.ShapeDtypeStruct((1, RWKV_W), F32)
    outs = pl.pallas_call(
        body, name="rwkv_bwd", grid=(n_groups, n_chunks),
        in_specs=[row] * 5 + [par] * 5 + [s0_spec, tinv_spec, row, row] + [ANY_SPEC] * hosted.n,
        out_specs=[row] * 5 + [par] * 5 + [ANY_SPEC] * hosted.n,
        out_shape=[rows_shape] * 5 + [par_shape] * 5 + exchange_shapes(hosted.arrays, hosted.gather),
        scratch_shapes=[pltpu.VMEM((n_inst, LANES, LANES), F32)] + exchange_sems(hosted.n),
        compiler_params=_params(("arbitrary", "arbitrary")),
    )(r, kraw, v, lw, asig, k_k, k_a, r_k, gn_w, gn_b, s0_all, tinv_all, dy, gate, *hosted.arrays)
    return outs[:10], outs[10:]


SB_Q0 = RWKV_COLS // LANES
SB_K0 = SB_Q0 + SB_W // LANES
SB_V0 = SB_K0 + SB_W // LANES
SB_SEQS = 2
SB_BUFFERS = pl.Buffered(1)
SB_SUM_PIECES = 2
SB_DEAD = -110.0


def _col_of(c_lo, c_hi):
    return jnp.where(_lane_lo((1, LANES)), c_lo, c_hi)


def sb_fwd(p, gain, n_seq, seq_len, hosted):
    n_pairs = SB_W // LANES
    n_q = seq_len // QB
    nb = min(SB_SEQS, n_seq)

    def seq_spec(c0):
        return pl.BlockSpec((nb, seq_len, LANES), functools.partial(lambda b, h, c0: (b, 0, c0 + h), c0=c0),
                            pipeline_mode=SB_BUFFERS)

    out_spec = pl.BlockSpec((nb, seq_len, LANES), lambda b, h: (b, 0, h), pipeline_mode=SB_BUFFERS)

    def body(*refs):
        own, copies = hosted.split(refs, 4, 4)
        q_ref, k_ref, v_ref, g_ref, y_ref, o_ref, tot_ref, first_ref = own
        step = pl.program_id(0) * n_pairs + pl.program_id(1)
        hosted.run(copies, step == 0, step == (n_seq // nb) * n_pairs - 1)
        gain = g_ref[...]

        def q_block(i, _):
            qs = pl.multiple_of(i * QB, QB)
            seqs = range(nb)
            zeros = [jnp.zeros((QB, 1), F32)] * nb
            qv = [sb_split_q(q_ref[s, pl.ds(qs, QB), :]) for s in seqs]
            add = lambda xs, ys: [x + y for x, y in zip(xs, ys)]

            def tiles(ks, c_lo, c_hi, diag):
                return sb_tile(qv, [k_ref[s, pl.ds(ks, QB), :] for s in seqs],
                               [v_ref[s, pl.ds(ks, QB), :] for s in seqs], c_lo, c_hi, diag)

            def alive(c_lo, c_hi):
                top = functools.reduce(jnp.maximum, list(c_lo) + list(c_hi))
                return jnp.max(top) > SB_DEAD

            def k_block(state):
                j, _, (o, c_lo, c_hi) = state
                o2, s_lo, s_hi = tiles(pl.multiple_of(j * QB, QB), c_lo, c_hi, False)
                c_lo, c_hi = add(c_lo, s_lo), add(c_hi, s_hi)
                return j - 1, alive(c_lo, c_hi), (add(o, o2), c_lo, c_hi)

            o, c_lo, c_hi = tiles(qs, zeros, zeros, True)
            j, _, (o, c_lo, c_hi) = lax.while_loop(lambda st: jnp.logical_and(st[0] >= 0, st[1]), k_block,
                                                   (i - 1, alive(c_lo, c_hi), (o, c_lo, c_hi)))
            first_ref[pl.program_id(0), pl.program_id(1), i] = j + 1
            for s in seqs:
                o_ref[s, pl.ds(qs, QB), :] = o[s]
                tot_ref[s, pl.ds(qs, QB), :] = jnp.broadcast_to(_col_of(c_lo[s], c_hi[s]), (QB, LANES))
                y_ref[s, pl.ds(qs, QB), :] = sb_post(o[s], gain)
            return 0

        lax.fori_loop(0, n_q, q_block, 0)

    shape = jax.ShapeDtypeStruct((n_seq, seq_len, SB_W), F32)
    return pl.pallas_call(
        body, name="sb_fwd", grid=(n_seq // nb, n_pairs),
        in_specs=[seq_spec(SB_Q0), seq_spec(SB_K0), seq_spec(SB_V0), pl.BlockSpec((1, LANES), lambda b, h: (0, h))]
        + [ANY_SPEC] * hosted.n,
        out_specs=[out_spec] * 3 + [pl.BlockSpec(memory_space=pltpu.SMEM)] + [ANY_SPEC] * hosted.n,
        out_shape=[shape] * 3 + [jax.ShapeDtypeStruct((n_seq // nb, n_pairs, n_q), jnp.int32)]
        + exchange_shapes(hosted.arrays, hosted.gather),
        scratch_shapes=exchange_sems(hosted.n),
        compiler_params=_params(("arbitrary", "arbitrary")),
    )(p, p, p, gain, *hosted.arrays)


def sb_bwd(p, gain, o_raw, tot, dy, first, n_seq, seq_len):
    n_pairs = SB_W // LANES
    n_q = seq_len // QB
    nb = min(SB_SEQS, n_seq)

    def seq_spec(c0):
        return pl.BlockSpec((nb, seq_len, LANES), functools.partial(lambda h, b, c0: (b, 0, c0 + h), c0=c0),
                            pipeline_mode=SB_BUFFERS)

    own = pl.BlockSpec((nb, seq_len, LANES), lambda h, b: (b, 0, h), pipeline_mode=SB_BUFFERS)
    par = pl.BlockSpec((1, LANES), lambda h, b: (0, h))

    def body(q_ref, k_ref, v_ref, g_ref, o_ref, tot_ref, dy_ref, first_ref, dq_ref, dk_ref, dv_ref, dg_ref):
        gain = g_ref[...]
        lo = _lane_lo((1, LANES))
        dk_ref[...] = jnp.zeros_like(dk_ref)
        dv_ref[...] = jnp.zeros_like(dv_ref)

        def q_block(i, dgain):
            qs = pl.multiple_of(i * QB, QB)
            seqs = range(nb)
            zeros = [jnp.zeros((QB, 1), F32)] * nb
            qv, dov, t_lo, t_hi = [], [], [], []
            for s in seqs:
                qv.append(sb_split_q(q_ref[s, pl.ds(qs, QB), :]))
                _, post_vjp = jax.vjp(sb_post, o_ref[s, pl.ds(qs, QB), :], gain)
                do, dg_s = post_vjp(dy_ref[s, pl.ds(qs, QB), :])
                dov.append(do)
                dgain = dgain + dg_s
                tot_s = tot_ref[s, pl.ds(qs, QB), :]
                t_lo.append(jnp.max(jnp.where(lo, tot_s, -jnp.inf), axis=1, keepdims=True))
                t_hi.append(jnp.max(jnp.where(lo, -jnp.inf, tot_s), axis=1, keepdims=True))
            add = lambda xs, ys: [x + y for x, y in zip(xs, ys)]
            sub = lambda xs, ys: [x - y for x, y in zip(xs, ys)]

            def tile(ks, carry, diag):
                dq, rem_lo, rem_hi, g_lo, g_hi = carry
                kv = [k_ref[s, pl.ds(ks, QB), :] for s in seqs]
                vv = [v_ref[s, pl.ds(ks, QB), :] for s in seqs]
                fn = functools.partial(sb_tile, diag=diag, from_here=list(zip(rem_lo, rem_hi)))
                (_, s_lo, s_hi), vjp = jax.vjp(fn, qv, kv, vv, zeros, zeros)
                dq_t, dk_t, dv_t, dc_lo, dc_hi = vjp((dov, g_lo, g_hi))
                dq_t = [jnp.where(lo, d_lo, d_hi) for d_lo, d_hi in dq_t]
                for s in seqs:
                    dk_ref[s, pl.ds(ks, QB), :] = dk_ref[s, pl.ds(ks, QB), :] + dk_t[s]
                    dv_ref[s, pl.ds(ks, QB), :] = dv_ref[s, pl.ds(ks, QB), :] + dv_t[s]
                return add(dq, dq_t), sub(rem_lo, s_lo), sub(rem_hi, s_hi), add(g_lo, dc_lo), add(g_hi, dc_hi)

            def k_block(j, carry):
                return tile(pl.multiple_of(j * QB, QB), carry, False)

            carry = ([jnp.zeros((QB, LANES), F32)] * nb, t_lo, t_hi, zeros, zeros)
            carry = lax.fori_loop(first_ref[pl.program_id(1), pl.program_id(0), i], i, k_block, carry)
            carry = tile(qs, carry, True)
            for s in seqs:
                dq_ref[s, pl.ds(qs, QB), :] = carry[0][s] * SB_SCALE
            return dgain

        dgain = lax.fori_loop(0, n_q, q_block, jnp.zeros((1, LANES), F32))
        first = pl.program_id(1) == 0

        @pl.when(first)
        def _():
            dg_ref[...] = dgain

        @pl.when(jnp.logical_not(first))
        def _():
            dg_ref[...] = dg_ref[...] + dgain

    shape = jax.ShapeDtypeStruct((n_seq, seq_len, SB_W), F32)
    return pl.pallas_call(
        body, name="sb_bwd", grid=(n_pairs, n_seq // nb),
        in_specs=[seq_spec(SB_Q0), seq_spec(SB_K0), seq_spec(SB_V0), par, own, own, seq_spec(RWKV_W // LANES),
                  pl.BlockSpec(memory_space=pltpu.SMEM)],
        out_specs=[own, own, own, par],
        out_shape=[shape, shape, shape, jax.ShapeDtypeStruct((1, SB_W), F32)],
        compiler_params=_params(("arbitrary", "arbitrary")),
    )(p, p, p, gain, o_raw, tot, dy, first)


def exchange(name, arrays, gather):
    n = len(arrays)

    def body(*refs):
        copies = exchange_copies(refs[:n], refs[n:2 * n], *refs[2 * n:], gather)
        for cp in copies:
            cp.start()
        for cp in copies:
            cp.wait()

    return pl.pallas_call(
        body, name=name, in_specs=[ANY_SPEC] * n, out_specs=[ANY_SPEC] * n, out_shape=exchange_shapes(arrays, gather),
        scratch_shapes=exchange_sems(n),
    )(*arrays)


ANY_SPEC = pl.BlockSpec(memory_space=pl.ANY)


def exchange_shapes(arrays, gather):
    return [jax.ShapeDtypeStruct(((N_DEV,) + a.shape) if gather else a.shape, a.dtype) for a in arrays]


def exchange_sems(n):
    return [pltpu.SemaphoreType.DMA((n, N_DEV - 1)), pltpu.SemaphoreType.DMA((n, N_DEV - 1)),
            pltpu.SemaphoreType.DMA((n,))]


def exchange_copies(ins, outs, send_sems, recv_sems, local_sems, gather):
    x, y, c = lax.axis_index("x"), lax.axis_index("y"), lax.axis_index("c")
    me = 4 * x + 2 * y + c
    copies = []
    for a, (src_all, dst_all) in enumerate(zip(ins, outs)):
        own = src_all if gather else src_all.at[me]
        copies.append(pltpu.make_async_copy(own, dst_all.at[me], local_sems.at[a]))
        for j in range(1, N_DEV):
            px, py, pc = (x + (j >> 2)) % 2, (y + ((j >> 1) & 1)) % 2, (c + (j & 1)) % 2
            src = src_all if gather else src_all.at[4 * px + 2 * py + pc]
            copies.append(pltpu.make_async_remote_copy(
                src_ref=src, dst_ref=dst_all.at[me], send_sem=send_sems.at[a, j - 1],
                recv_sem=recv_sems.at[a, j - 1], device_id=(px, py, pc), device_id_type=pl.DeviceIdType.MESH))
    return copies


def adamw(name, w, parts, m, v, tile):
    rows, cols = w.shape
    spec = pl.BlockSpec((tile, cols), lambda i: (i, 0))
    part_spec = pl.BlockSpec((N_DEV, tile, cols), lambda i: (0, i, 0))

    def body(w_ref, p_ref, m_ref, v_ref, g_ref, d_ref, nm_ref, nv_ref):
        g = p_ref[0].astype(F32)
        for s in range(1, N_DEV):
            g = g + p_ref[s].astype(F32)
        new_m = ADAM_B1 * m_ref[...] + (1.0 - ADAM_B1) * g
        new_v = ADAM_B2 * v_ref[...] + (1.0 - ADAM_B2) * (g * g)
        m_hat = new_m / (1.0 - ADAM_B1 ** ADAM_STEP)
        v_hat = new_v / (1.0 - ADAM_B2 ** ADAM_STEP)
        g_ref[...] = g
        d_ref[...] = -ADAM_LR * (m_hat / (jnp.sqrt(v_hat) + ADAM_EPS) + ADAM_WD * w_ref[...])
        nm_ref[...] = new_m
        nv_ref[...] = new_v

    shape = jax.ShapeDtypeStruct((rows, cols), F32)
    return pl.pallas_call(
        body, name=name, grid=(rows // tile,), in_specs=[spec, part_spec, spec, spec],
        out_specs=[spec] * 4, out_shape=[shape] * 4, compiler_params=_params(("arbitrary",)),
    )(w, parts, m, v)


SMALL = ("ln1_g", "tok_mu", "w0", "a0", "k_k", "k_a", "r_k", "gn_w", "gn_b", "sb_gain", "ln2_g", "lnf_g")
EARLY = ("w_in", "w_decay_up", "w_aaa_up", "w_gate_up")
LATE = ("w_out", "w_up", "w_down")
BIG = EARLY + LATE
ORDER = ("ln1_g", "w_in", "tok_mu", "w0", "w_decay_up", "a0", "w_aaa_up", "w_gate_up", "k_k", "k_a", "r_k",
         "gn_w", "gn_b", "sb_gain", "w_out", "ln2_g", "w_up", "w_down", "lnf_g")


def _pack(vectors, rows):
    flat = jnp.concatenate([v.reshape(-1).astype(F32) for v in vectors])
    return jnp.pad(flat, (0, rows * LANES - flat.shape[0])).reshape(rows, LANES)


def _full_cols(gathered):
    d, k, cols = gathered.shape
    return jnp.transpose(gathered, (1, 0, 2)).reshape(k, d * cols)


def _col_parts(full):
    k, n = full.shape
    return jnp.transpose(full.reshape(k, N_DEV, n // N_DEV), (1, 0, 2))


def kernel(x, ln1_g, w_in, tok_mu, w0, w_decay_up, a0, w_aaa_up, w_gate_up, k_k, k_a, r_k, gn_w, gn_b, sb_gain, w_out, ln2_g, w_up, w_down, lnf_g, loss_target, m_ln1_g, m_w_in, m_tok_mu, m_w0, m_w_decay_up, m_a0, m_w_aaa_up, m_w_gate_up, m_k_k, m_k_a, m_r_k, m_gn_w, m_gn_b, m_sb_gain, m_w_out, m_ln2_g, m_w_up, m_w_down, m_lnf_g, v_ln1_g, v_w_in, v_tok_mu, v_w0, v_w_decay_up, v_a0, v_w_aaa_up, v_w_gate_up, v_k_k, v_k_a, v_r_k, v_gn_w, v_gn_b, v_sb_gain, v_w_out, v_ln2_g, v_w_up, v_w_down, v_lnf_g):
    args = dict(locals())
    weights = {n: args[n] for n in ORDER}
    mom_m = {n: args["m_" + n] for n in ORDER}
    mom_v = {n: args["v_" + n] for n in ORDER}

    n_seq, seq_len, d_model = x.shape
    n_tok = n_seq * seq_len
    x2d = x.reshape(n_tok, d_model)
    tgt = loss_target.reshape(n_tok, d_model)
    row = lambda t: t.reshape(1, -1).astype(F32)

    g1 = row(ln1_g)
    shard = {n: weights[n][0].astype(BF16) for n in BIG}
    h1, *early = rw_call("norm1", lambda r, q: [f_norm(r[0], q[0])], [x2d], [g1], [(d_model, BF16)], 512,
                         hosted=HostedChipGather([shard[n] for n in EARLY]))
    gathered = dict(zip(EARLY, early))
    w_in_f = _full_cols(gathered["w_in"])
    zeros64 = jnp.zeros((HEAD_DIM, RWKV_W), BF16)
    wd_pad = jnp.concatenate([_full_cols(gathered["w_decay_up"]), zeros64], axis=0)
    wa_pad = jnp.concatenate([zeros64, _full_cols(gathered["w_aaa_up"])], axis=0)
    wg_f = _full_cols(gathered["w_gate_up"])
    in_cols = w_in_f.shape[1]

    mu, w0r, a0r = row(tok_mu), row(w0), row(a0)
    kkr, kar, rkr, gwr, gbr, sgr = row(k_k), row(k_a), row(r_k), row(gn_w), row(gn_b), row(sb_gain)
    g2, gf = row(ln2_g), row(lnf_g)

    p = matmul("proj_in", h1, w_in_f, "nn", F32, 1024, in_cols // 2, d_model)
    prep_pars = [mu, w0r, wd_pad, a0r, wa_pad, wg_f]
    (r_, kraw, v_, lw, asig, gate), prep_edges = rwkv_prep_fwd(p, prep_pars, seq_len)
    by_seq = lambda t: t.reshape(n_seq, seq_len, t.shape[-1])
    flat = lambda t: t.reshape(n_tok, t.shape[-1])
    rwkv_in = [by_seq(t) for t in (r_, kraw, v_, lw, asig)]
    z, s0_all, tinv_all, (w_out_g, w_up_g) = rwkv_fwd(
        *rwkv_in, kkr, kar, rkr, gwr, gbr, n_seq, seq_len, Hosted([shard["w_out"], shard["w_up"]], True))
    y_sb, o_raw, tot, sb_first, w_down_g = sb_fwd(by_seq(p), sgr, n_seq, seq_len, Hosted([shard["w_down"]], True))
    w_out_f = w_out_g.reshape(d_model, d_model)
    w_up_f = _full_cols(w_up_g)
    w_down_f = w_down_g.reshape(-1, d_model)
    d_ff = w_up_f.shape[1]
    z, y_sb = flat(z), flat(y_sb)
    (ycat,) = rw_call("mix_cat", lambda r, q: [jnp.concatenate([r[0] * r[1], r[2]], axis=1)],
                      [z, gate, y_sb], [], [(d_model, BF16)], 512)
    x2, h2 = matmul("proj_out", ycat, w_out_f, "nn", (F32, BF16), 512, d_model, d_model, extras=[x2d], pars=[g2],
                    epilogue=lambda acc, xv, g: (xv + acc, f_norm(xv + acc, g)))
    u, act = matmul("mlp_up", h2, w_up_f, "nn", (F32, BF16), 1024, d_ff // 4, d_model,
                    epilogue=lambda acc: (acc, jnp.square(jnp.maximum(acc, 0.0))))

    def loss_epilogue(acc, x2v, target, g):
        loss_rows, vjp = jax.vjp(lambda xv, gv: f_final(xv, gv, target), x2v + acc, g)
        dx3, dgf = vjp(jnp.ones_like(loss_rows))
        return dx3, dx3, jnp.broadcast_to(jnp.sum(loss_rows), (1, LANES)), dgf

    dx3, dx3_b, loss_acc, d_lnf = matmul(
        "mlp_down", act, w_down_f, "nn", (F32, BF16), 512, d_model, d_ff, extras=[x2, tgt], pars=[gf],
        epilogue=loss_epilogue, sums=[((1, LANES), F32), ((1, d_model), F32)])

    du = matmul("d_act", dx3_b, w_down_f, "nt", BF16, 1024, d_ff // 4, d_model, extras=[u],
                epilogue=lambda acc, uv: (acc * (2.0 * jnp.maximum(uv, 0.0)),))
    dw_down = matmul("dw_down", act, dx3_b, "tn", BF16, 512, d_model, 4096)
    dw_up = matmul("dw_up", h2, du, "tn", BF16, d_model, 512, 4096)

    def norm_bwd(acc, xv, dres, g):
        _, vjp = jax.vjp(f_norm, xv, g)
        dx, dg = vjp(acc)
        return dx + dres, dx + dres, dg

    dx2, dx2_b, d_ln2 = matmul("d_h2", du, w_up_f, "nt", (F32, BF16), 512, d_model, d_ff, extras=[x2, dx3], pars=[g2],
                               epilogue=norm_bwd, sums=[((1, d_model), F32)])

    dycat = matmul("d_ycat", dx2_b, w_out_f, "nt", F32, 512, d_model, d_model)
    dw_out = matmul("dw_out", ycat, dx2_b, "tn", BF16, d_model, d_model, 2048)
    dq, dk_sb, dv_sb, d_sg = sb_bwd(by_seq(p), sgr, o_raw, tot, by_seq(dycat), sb_first, n_seq, seq_len)
    d_sb = [flat(dq), flat(dk_sb), flat(dv_sb)]
    late_grads = {"w_out": dw_out.reshape(N_DEV, -1, d_model), "w_up": _col_parts(dw_up),
                  "w_down": dw_down.reshape(N_DEV, -1, d_model)}
    (dr, dkraw, dv, dlw, dasig, d_kk, d_ka, d_rk, d_gw, d_gb), late_parts = rwkv_bwd(
        *rwkv_in, kkr, kar, rkr, gwr, gbr, s0_all, tinv_all, by_seq(dycat), by_seq(gate), n_seq, seq_len,
        Hosted([late_grads[n] for n in LATE], False))
    prep_cots = [flat(t) for t in (dr, dkraw, dv, dlw, dasig)] + [dycat, z]
    dp, (d_mu, d_w0, d_wd, d_a0, d_wa, d_wg) = rwkv_prep_bwd(p, prep_edges, prep_pars, prep_cots, d_sb, seq_len)

    dw_in = matmul("dw_in", h1, dp, "tn", BF16, d_model, in_cols // 2, 2048)
    lora_parts = lambda t: _col_parts(t).astype(BF16)
    early_grads = {"w_in": _col_parts(dw_in), "w_decay_up": lora_parts(d_wd[:HEAD_DIM]),
                   "w_aaa_up": lora_parts(d_wa[HEAD_DIM:]), "w_gate_up": lora_parts(d_wg)}
    dx, d_ln1, *early_parts = matmul(
        "d_h1", dp, w_in_f, "nt", (F32,), 512, d_model, in_cols, extras=[x2d, dx2], pars=[g1],
        epilogue=lambda *t: norm_bwd(*t)[1:], sums=[((1, d_model), F32)],
        hosted=Hosted([early_grads[n] for n in EARLY], False))
    parts = dict(zip(EARLY, early_parts))
    parts.update(zip(LATE, late_parts))

    small_grads = {"ln1_g": d_ln1, "tok_mu": d_mu, "w0": d_w0, "a0": d_a0, "k_k": d_kk, "k_a": d_ka, "r_k": d_rk,
                   "gn_w": d_gw, "gn_b": d_gb, "sb_gain": d_sg, "ln2_g": d_ln2, "lnf_g": d_lnf}
    n_small = sum(int(weights[n].size) for n in SMALL)
    pack_rows = -(-(n_small + 1) // (8 * LANES)) * 8
    packed = _pack([small_grads[n] for n in SMALL] + [loss_acc[0, :1]], pack_rows)
    (small_parts,) = exchange("gather_small", [packed], True)

    results = {}
    for n in BIG:
        w2d = weights[n][0]
        tile = w2d.shape[0] if w2d.shape[0] <= 256 else 256
        results[n] = adamw("adamw_" + n, w2d, parts[n], mom_m[n][0], mom_v[n][0], tile)
    pk = lambda d: _pack([d[n] for n in SMALL] + [jnp.zeros((1,), F32)], pack_rows)
    sg, sd, sm, sv = adamw("adamw_small", pk(weights), small_parts, pk(mom_m), pk(mom_v), pack_rows)
    off = 0
    for n in SMALL:
        size = int(weights[n].size)
        results[n] = tuple(t.reshape(-1)[off:off + size] for t in (sg, sd, sm, sv))
        off += size
    loss = sg.reshape(-1)[off]

    out = [loss, dx.reshape(x.shape)]
    for kind in range(4):
        out += [results[n][kind].reshape(weights[n].shape) for n in ORDER]
    return tuple(out)
```

```python
import functools
import math

import jax
import jax.numpy as jnp
from jax import lax
from jax.experimental import pallas as pl
from jax.experimental.pallas import tpu as pltpu

F32 = jnp.float32
BF16 = jnp.bfloat16

N_DEV = 8
HEAD_DIM = 64
LANES = 128
RWKV_W = 512
SB_W = 512
LORA_WA = 128
GATE_LORA = 128
RWKV_COLS = 3 * RWKV_W + LORA_WA + GATE_LORA
RMS_EPS = 1e-5
GN_EPS = 64e-5
CHUNK = 64
QB = 256
SB_SCALE = HEAD_DIM ** -0.5
ADAM_LR, ADAM_B1, ADAM_B2, ADAM_EPS, ADAM_WD, ADAM_STEP = 0.001, 0.9, 0.999, 1e-08, 0.01, 10
VMEM_LIMIT = 56 * 1024 * 1024


_DIMS = {
    "nn": (((1,), (0,)), ((), ())),
    "nt": (((1,), (1,)), ((), ())),
    "tn": (((0,), (0,)), ((), ())),
}


def _pieces(x, n):
    if n == 1:
        return [x.astype(BF16)]
    out, rem = [], x.astype(F32)
    for i in range(n):
        p = rem.astype(BF16)
        out.append(p)
        if i + 1 < n:
            rem = rem - p.astype(F32)
    return out


def _dot(a, b, form, pa, pb):
    pieces_a, pieces_b = _pieces(a, pa), _pieces(b, pb)
    keep = max(pa, pb)
    acc = None
    for i, ai in enumerate(pieces_a):
        for j, bj in enumerate(pieces_b):
            if i + j >= keep:
                continue
            t = lax.dot_general(ai, bj, _DIMS[form], preferred_element_type=F32)
            acc = t if acc is None else acc + t
    return acc


BOTH = (True, True)


@functools.partial(jax.custom_vjp, nondiff_argnums=(2, 3, 4, 5, 6))
def mm(a, b, form, pa, pb, diff=BOTH, grad_pieces=None):
    return _dot(a, b, form, pa, pb)


def _mm_fwd(a, b, form, pa, pb, diff, grad_pieces):
    return _dot(a, b, form, pa, pb), (a, b)


def _mm_bwd(form, pa, pb, diff, grad_pieces, res, g):
    a, b = res
    pg = grad_pieces or max(pa, pb)
    da, db = jnp.zeros_like(a), jnp.zeros_like(b)
    if form == "nn":
        if diff[0]:
            da = mm(g, b, "nt", pg, pb)
        if diff[1]:
            db = mm(a, g, "tn", pa, pg)
    elif form == "nt":
        if diff[0]:
            da = mm(g, b, "nn", pg, pb)
        if diff[1]:
            db = mm(g, a, "tn", pg, pa)
    else:
        if diff[0]:
            da = mm(b, g, "nt", pb, pg)
        if diff[1]:
            db = mm(a, g, "nn", pa, pg)
    return da, db


mm.defvjp(_mm_fwd, _mm_bwd)


def _stack_rows(top, bottom):
    return jnp.concatenate([top, bottom], axis=0)


@jax.custom_vjp
def _split_rows(x):
    half = x.shape[0] // 2
    return x[:half], x[half:]


def _split_rows_fwd(x):
    return _split_rows(x), None


def _split_rows_bwd(_, grads):
    return (_stack_rows(*grads),)


_split_rows.defvjp(_split_rows_fwd, _split_rows_bwd)


def _lane_lo(shape):
    return lax.broadcasted_iota(jnp.int32, shape, len(shape) - 1) < HEAD_DIM


def _segsum(x):
    lo = _lane_lo(x.shape)
    s_lo = jnp.sum(jnp.where(lo, x, 0.0), axis=-1, keepdims=True)
    s_hi = jnp.sum(jnp.where(lo, 0.0, x), axis=-1, keepdims=True)
    return jnp.where(lo, s_lo, s_hi)


def _sigmoid(x):
    return 0.5 * (jnp.tanh(0.5 * x) + 1.0)


@jax.custom_vjp
def _log_one_minus_sigmoid(z):
    return -jnp.maximum(z, 0.0) - jnp.log(1.0 + jnp.exp(-jnp.abs(z)))


def _log_oms_fwd(z):
    out = _log_one_minus_sigmoid(z)
    return out, (z, out)


def _log_oms_bwd(res, g):
    z, out = res
    return (-g * jnp.exp(z + out),)


_log_one_minus_sigmoid.defvjp(_log_oms_fwd, _log_oms_bwd)


def f_norm(x, g):
    return x * lax.rsqrt(jnp.mean(x * x, axis=-1, keepdims=True) + RMS_EPS) * g


def f_prep(p, pprev, mu, w0, wd_pad, a0, wa_pad, wg):
    pm = p + mu * (pprev - p)
    r = pm[:, 0:RWKV_W]
    k = pm[:, RWKV_W:2 * RWKV_W]
    v = pm[:, 2 * RWKV_W:3 * RWKV_W]
    xwa = pm[:, 3 * RWKV_W:3 * RWKV_W + LORA_WA]
    xg = pm[:, 3 * RWKV_W + LORA_WA:RWKV_COLS]
    w = _log_one_minus_sigmoid(-(w0 + mm(jnp.tanh(xwa), wd_pad, "nn", 1, 1))) - 0.5
    lw = -jnp.exp(w)
    asig = _sigmoid(a0 + mm(xwa, wa_pad, "nn", 1, 1))
    gate = mm(_sigmoid(xg), wg, "nn", 1, 1)
    return r, k, v, lw, asig, gate


def _tri(n, kind):
    row = lax.broadcasted_iota(jnp.int32, (n, n), 0)
    col = lax.broadcasted_iota(jnp.int32, (n, n), 1)
    if kind == "lower_incl":
        return row >= col
    return row > col


@functools.partial(jax.custom_vjp, nondiff_argnums=(1,))
def _nilpotent_inverses(mats, hp):
    size = mats[0].shape[0]
    eye = (lax.broadcasted_iota(jnp.int32, (size, size), 0) == lax.broadcasted_iota(jnp.int32, (size, size), 1))
    tinv = [eye.astype(F32) + x for x in mats]
    pw = [mm(x, x, "nn", hp, hp) for x in mats]
    for _ in range(int(math.log2(size)) - 2):
        both = [_split_rows(mm(_stack_rows(t, x), x, "nn", hp, hp)) for t, x in zip(tinv, pw)]
        tinv = [t + tx for t, (tx, _) in zip(tinv, both)]
        pw = [xx for _, xx in both]
    return [t + mm(t, x, "nn", hp, hp) for t, x in zip(tinv, pw)]


def _nilpotent_inverses_fwd(mats, hp):
    tinv = _nilpotent_inverses(mats, hp)
    return tinv, tinv


def _nilpotent_inverses_bwd(hp, tinv, grads):
    right = [mm(g, t, "nt", hp, hp) for g, t in zip(grads, tinv)]
    return ([mm(t, x, "tn", hp, hp) for t, x in zip(tinv, right)],)


_nilpotent_inverses.defvjp(_nilpotent_inverses_fwd, _nilpotent_inverses_bwd)


@functools.partial(jax.custom_vjp, nondiff_argnums=(2,))
def _known_inverses(mats, tinv, hp):
    return list(tinv)


def _known_inverses_fwd(mats, tinv, hp):
    return list(tinv), tinv


def _known_inverses_bwd(hp, tinv, grads):
    return _nilpotent_inverses_bwd(hp, tinv, grads) + ([jnp.zeros_like(t) for t in tinv],)


_known_inverses.defvjp(_known_inverses_fwd, _known_inverses_bwd)


def rwkv_chunk(state, r, kraw, v, lw, asig, k_k, k_a, r_k, gn_w, gn_b, hp, saved_tinv=None, with_tinv=False):
    n = len(r)
    L = r[0].shape[0]
    lo = _lane_lo((1, LANES))
    masks = (lo, jnp.logical_not(lo))
    incl = _tri(L, "lower_incl")
    strict = _tri(L, "strict")
    tri = incl.astype(F32)
    kk = [x * w for x, w in zip(kraw, k_k)]
    kk = [x / jnp.maximum(jnp.sqrt(_segsum(x * x)), 1e-12) for x in kk]
    k = [x * (1.0 + (s - 1.0) * w) for x, s, w in zip(kraw, asig, k_a)]
    b = [x * s for x, s in zip(kk, asig)]
    c = [mm(tri, x, "nn", 1, 3, (False, True)) for x in lw]
    at = [-x * jnp.exp(ci - li) for x, ci, li in zip(kk, c, lw)]
    rt = [x * jnp.exp(ci) for x, ci in zip(r, c)]
    einv = [jnp.exp(-ci) for ci in c]
    bt = [x * e for x, e in zip(b, einv)]
    kt = [x * e for x, e in zip(k, einv)]
    inst = [(s, m) for s in range(n) for m in masks]
    ar_h = [_stack_rows(jnp.where(m, at[s], 0.0), jnp.where(m, rt[s], 0.0)) for s, m in inst]
    on_b = [_split_rows(mm(x, bt[s], "nt", hp, hp)) for x, (s, _) in zip(ar_h, inst)]
    on_k = [_split_rows(mm(x, kt[s], "nt", hp, hp)) for x, (s, _) in zip(ar_h, inst)]
    a_ab = [jnp.where(strict, x, 0.0) for x, _ in on_b]
    b_rb = [jnp.where(incl, x, 0.0) for _, x in on_b]
    a_ak = [jnp.where(strict, x, 0.0) for x, _ in on_k]
    b_rk = [jnp.where(incl, x, 0.0) for _, x in on_k]
    tinv = _nilpotent_inverses(a_ab, hp) if saved_tinv is None else _known_inverses(a_ab, saved_tinv, hp)
    on_state = [_split_rows(mm(x, state[s], "nt", hp, hp)) for x, (s, _) in zip(ar_h, inst)]
    on_v = [_split_rows(mm(_stack_rows(m1, m2), v[s], "nn", hp, hp)) for m1, m2, (s, _) in zip(a_ak, b_rk, inst)]
    u_h = [mm(t, sa + av, "nn", hp, hp) for t, (sa, _), (av, _) in zip(tinv, on_state, on_v)]
    y_h = [sr + mm(m1, u, "nn", hp, hp) + bv for (_, sr), m1, u, (_, bv) in zip(on_state, b_rb, u_h, on_v)]
    u_all = [jnp.where(lo, u_h[2 * s], u_h[2 * s + 1]) for s in range(n)]
    y_all = [jnp.where(lo, y_h[2 * s], y_h[2 * s + 1]) for s in range(n)]
    c_last = [jnp.sum(x, axis=0, keepdims=True) for x in lw]
    efwd = [jnp.exp(cl - ci) for cl, ci in zip(c_last, c)]
    new_state = [st * jnp.exp(cl) + mm(_stack_rows(u, vi), _stack_rows(bi * e, ki * e), "tn", hp, hp)
                 for st, cl, u, bi, e, vi, ki in zip(state, c_last, u_all, b, efwd, v, k)]
    row_head = lax.broadcasted_iota(jnp.int32, (LANES, LANES), 0) // HEAD_DIM
    col_head = lax.broadcasted_iota(jnp.int32, (LANES, LANES), 1) // HEAD_DIM
    new_state = [jnp.where(row_head == col_head, x, 0.0) for x in new_state]
    outs = []
    for y, ri, ki, vi, w_rk, w_gw, w_gb in zip(y_all, r, k, v, r_k, gn_w, gn_b):
        mean = _segsum(y) * (1.0 / HEAD_DIM)
        d = y - mean
        var = _segsum(d * d) * (1.0 / HEAD_DIM)
        yn = d * lax.rsqrt(var + GN_EPS) * w_gw + w_gb
        outs.append(yn + _segsum(ri * ki * w_rk) * vi)
    return (outs, new_state, tinv) if with_tinv else (outs, new_state)


def sb_tile(q, k, v, c_lo, c_hi, diag, from_here=None):
    n = len(q)
    lo = _lane_lo((1, LANES))
    below = _tri(QB, "strict")
    from_s = _tri(QB, "lower_incl").astype(F32)
    inst = [(s, h) for s in range(n) for h in (0, 1)]
    carry = [(c_lo[s], c_hi[s])[h] for s, h in inst]
    z = [mm(q[s][h], k[s], "nt", 1, 1) for s, h in inst]
    log_keep = [_log_one_minus_sigmoid(x) for x in z]
    if diag:
        log_keep = [jnp.where(below, x, 0.0) for x in log_keep]
    own = [jnp.sum(x, axis=1, keepdims=True) for x in log_keep]
    if from_here is not None:
        carry = [lax.stop_gradient(from_here[s][h] - o) + cr for (s, h), o, cr in zip(inst, own, carry)]
    tail = [mm(x, from_s, "nn", SB_SUM_PIECES, 1, (True, False), 1) for x in log_keep]
    log_a = [x + tl + cr for x, tl, cr in zip(z, tail, carry)]
    if diag:
        log_a = [jnp.where(below, x, -1e30) for x in log_a]
    att = [jnp.exp(x) for x in log_a]
    out_h = [mm(x, v[s], "nn", 1, 1) for x, (s, _) in zip(att, inst)]
    out = [jnp.where(lo, out_h[2 * s], out_h[2 * s + 1]) for s in range(n)]
    return out, [own[2 * s] for s in range(n)], [own[2 * s + 1] for s in range(n)]


def sb_split_q(q):
    lo = _lane_lo((1, LANES))
    qs = q * SB_SCALE
    return jnp.where(lo, qs, 0.0), jnp.where(lo, 0.0, qs)


def sb_post(o, gain):
    return o * lax.rsqrt(_segsum(o * o) * (1.0 / HEAD_DIM) + RMS_EPS) * gain


def f_final(x3, g, target):
    y = f_norm(x3, g)
    err = y - target
    return 0.5 * jnp.mean(err * err, axis=-1, keepdims=True)


def _params(sem):
    return pltpu.CompilerParams(dimension_semantics=sem, vmem_limit_bytes=VMEM_LIMIT)


def rw_call(name, body_fn, rows, pars, out_rows, tile, hosted=None):
    n_rows = rows[0].shape[0]
    tile = min(tile, n_rows)
    steps = n_rows // tile
    row_specs = [pl.BlockSpec((tile, arr.shape[1]), lambda i: (i, 0)) for arr in rows]
    par_specs = [pl.BlockSpec(p.shape, lambda i: (0, 0)) for p in pars]
    nr, npar, nor = len(rows), len(pars), len(out_rows)
    n_host = hosted.n if hosted is not None else 0

    def body(*refs):
        if hosted is not None:
            refs, copies = hosted.split(refs, nr + npar, nor)
            hosted.run(copies, pl.program_id(0) == 0, pl.program_id(0) == steps - 1)
        row_outs = body_fn([r[...] for r in refs[:nr]], [r[...] for r in refs[nr:nr + npar]])
        for ref, val in zip(refs[nr + npar:], row_outs):
            ref[...] = val.astype(ref.dtype)

    outs = pl.pallas_call(
        body, name=name, grid=(steps,), in_specs=row_specs + par_specs + [ANY_SPEC] * n_host,
        out_specs=[pl.BlockSpec((tile, c), lambda i: (i, 0)) for c, _ in out_rows] + [ANY_SPEC] * n_host,
        out_shape=[jax.ShapeDtypeStruct((n_rows, c), dt) for c, dt in out_rows]
        + (exchange_shapes(hosted.arrays, hosted.gather) if n_host else []),
        scratch_shapes=exchange_sems(n_host) if n_host else [],
        compiler_params=_params(("arbitrary",)),
    )(*rows, *pars, *(hosted.arrays if n_host else []))
    return outs


def matmul(name, a, b, form, out_dtype, tm, tn, tk, extras=(), pars=(), epilogue=None, sums=(), hosted=None):
    out_dtypes = out_dtype if isinstance(out_dtype, tuple) else (out_dtype,)
    tm, tn, tk = min(tm, a.shape[1 if form == "tn" else 0]), min(tn, b.shape[0 if form == "nt" else 1]), min(tk, a.shape[0 if form == "tn" else 1])
    if form == "nn":
        (m, kd), n = a.shape, b.shape[1]
        a_spec = pl.BlockSpec((tm, tk), lambda i, j, k: (i, k))
        b_spec = pl.BlockSpec((tk, tn), lambda i, j, k: (k, j))
    elif form == "nt":
        (m, kd), n = a.shape, b.shape[0]
        a_spec = pl.BlockSpec((tm, tk), lambda i, j, k: (i, k))
        b_spec = pl.BlockSpec((tn, tk), lambda i, j, k: (j, k))
    else:
        (kd, m), n = a.shape, b.shape[1]
        a_spec = pl.BlockSpec((tk, tm), lambda i, j, k: (k, i))
        b_spec = pl.BlockSpec((tk, tn), lambda i, j, k: (k, j))
    ksteps = kd // tk

    n_extra, n_par, n_out, n_sum = len(extras), len(pars), len(out_dtypes), len(sums)
    n_in = 2 + n_extra + n_par
    grid = (m // tm, n // tn, ksteps)

    def body(*refs):
        if hosted is not None:
            refs, copies = hosted.split(refs, n_in, n_out + n_sum)
            here = [pl.program_id(d) for d in range(3)]
            hosted.run(copies, functools.reduce(jnp.logical_and, [h == 0 for h in here]),
                       functools.reduce(jnp.logical_and, [h == g - 1 for h, g in zip(here, grid)]))
        a_ref, b_ref, rest = refs[0], refs[1], refs[2:]
        e_refs, o_refs = rest[:n_extra + n_par], rest[n_extra + n_par:n_extra + n_par + n_out]
        s_refs = rest[n_extra + n_par + n_out:n_extra + n_par + n_out + n_sum]
        kstep = pl.program_id(2)
        part = lax.dot_general(a_ref[...].astype(BF16), b_ref[...].astype(BF16), _DIMS[form],
                               preferred_element_type=F32)

        def finish(acc):
            outs = epilogue(acc, *[r[...] for r in e_refs]) if epilogue else (acc,)
            for ref, val in zip(o_refs, outs[:n_out]):
                ref[...] = val.astype(ref.dtype)
            if n_sum:
                first_tile = jnp.logical_and(pl.program_id(0) == 0, pl.program_id(1) == 0)

                @pl.when(first_tile)
                def _():
                    for ref, val in zip(s_refs, outs[n_out:]):
                        ref[...] = val.astype(ref.dtype)

                @pl.when(jnp.logical_not(first_tile))
                def _():
                    for ref, val in zip(s_refs, outs[n_out:]):
                        ref[...] = ref[...] + val.astype(ref.dtype)

        if ksteps == 1:
            finish(part)
            return
        acc_ref = rest[n_extra + n_par + n_out + n_sum]

        @pl.when(kstep == 0)
        def _():
            acc_ref[...] = part

        @pl.when(jnp.logical_and(kstep > 0, kstep < ksteps - 1))
        def _():
            acc_ref[...] = acc_ref[...] + part

        @pl.when(kstep == ksteps - 1)
        def _():
            finish(acc_ref[...] + part)

    out_spec = pl.BlockSpec((tm, tn), lambda i, j, k: (i, j))
    whole = lambda shape: pl.BlockSpec(shape, lambda i, j, k: (0,) * len(shape))
    n_host = hosted.n if hosted is not None else 0
    host_in = list(hosted.arrays) if hosted is not None else []
    host_out = exchange_shapes(hosted.arrays, hosted.gather) if hosted is not None else []
    outs = pl.pallas_call(
        body, name=name, grid=grid,
        in_specs=[a_spec, b_spec] + [out_spec] * n_extra + [whole(t.shape) for t in pars] + [ANY_SPEC] * n_host,
        out_specs=[out_spec] * n_out + [whole(s) for s, _ in sums] + [ANY_SPEC] * n_host,
        out_shape=[jax.ShapeDtypeStruct((m, n), dt) for dt in out_dtypes]
        + [jax.ShapeDtypeStruct(s, dt) for s, dt in sums] + host_out,
        scratch_shapes=([pltpu.VMEM((tm, tn), F32)] if ksteps > 1 else []) + (exchange_sems(n_host) if n_host else []),
        compiler_params=_params(("arbitrary",) * 3 if (sums or n_host) else ("parallel", "parallel", "arbitrary")),
    )(a, b, *extras, *pars, *host_in)
    return outs if (isinstance(out_dtype, tuple) or sums or n_host) else outs[0]


PREP_TILE = 512
PREP_TILE_BWD = 256
SUBLANES = 8


def _shift_in(rows, first):
    rolled = pltpu.roll(rows, 1, 0)
    row = lax.broadcasted_iota(jnp.int32, (SUBLANES, rows.shape[1]), 0)
    head = jnp.where(row == 0, first, rolled[0:SUBLANES])
    return jnp.concatenate([head, rolled[SUBLANES:]], axis=0), rolled


def rwkv_prep_fwd(p, pars, seq_len):
    n_tok = p.shape[0]
    tile = min(PREP_TILE, seq_len)
    tile_b = min(PREP_TILE_BWD, tile)
    steps, per_seq, sub = n_tok // tile, seq_len // tile, tile // tile_b
    n_par = len(pars)

    def body(p_ref, *rest):
        par_refs, out_refs, edge_ref, last8 = rest[:n_par], rest[n_par:n_par + 6], rest[n_par + 6], rest[n_par + 7]
        step = pl.program_id(0)

        @pl.when(step == 0)
        def _():
            last8[...] = jnp.zeros_like(last8)

        rows = p_ref[...]
        before = jnp.where(step % per_seq == 0, 0.0, pltpu.roll(last8[...], 1, 0))
        prev, rolled = _shift_in(rows, before)
        edge_ref[0] = prev[0:SUBLANES]
        for m in range(1, sub):
            edge_ref[m] = rolled[m * tile_b:m * tile_b + SUBLANES]
        last8[...] = rows[tile - SUBLANES:tile]
        for ref, val in zip(out_refs, f_prep(rows, prev, *[r[...] for r in par_refs])):
            ref[...] = val

    row_out = pl.BlockSpec((tile, RWKV_W), lambda i: (i, 0))
    outs = pl.pallas_call(
        body, name="rwkv_prep", grid=(steps,),
        in_specs=[pl.BlockSpec((tile, RWKV_COLS), lambda i: (i, 0))] + [pl.BlockSpec(t.shape, lambda i: (0, 0)) for t in pars],
        out_specs=[row_out] * 6 + [pl.BlockSpec((sub, SUBLANES, RWKV_COLS), lambda i: (i, 0, 0))],
        out_shape=[jax.ShapeDtypeStruct((n_tok, RWKV_W), F32)] * 6
        + [jax.ShapeDtypeStruct((steps * sub, SUBLANES, RWKV_COLS), F32)],
        scratch_shapes=[pltpu.VMEM((SUBLANES, RWKV_COLS), F32)],
        compiler_params=_params(("arbitrary",)),
    )(p, *pars)
    return outs[:6], outs[6]


def rwkv_prep_bwd(p, edges, pars, cots, d_sb, seq_len):
    n_tok = p.shape[0]
    tile = min(PREP_TILE_BWD, seq_len)
    steps, per_seq = n_tok // tile, seq_len // tile
    n_par = len(pars)
    back = lambda i: steps - 1 - i

    def body(p_ref, edge_ref, *rest):
        par_refs, rest = rest[:n_par], rest[n_par:]
        cot_refs, dy_ref, z_ref, sb_refs = rest[:5], rest[5], rest[6], rest[7:10]
        dp_ref, acc_refs, next8 = rest[10], rest[11:11 + n_par], rest[11 + n_par]
        step = pl.program_id(0)
        first = step == 0

        @pl.when(first)
        def _():
            next8[...] = jnp.zeros_like(next8)

        rows = p_ref[...]
        prev, _ = _shift_in(rows, edge_ref[0])
        _, vjp = jax.vjp(f_prep, rows, prev, *[r[...].astype(F32) for r in par_refs])
        grads = vjp(tuple(r[...] for r in cot_refs) + (dy_ref[...] * z_ref[...],))
        d_rows, d_prev = grads[0], grads[1]
        up = pltpu.roll(d_prev, tile - 1, 0)
        ends_seq = back(step) % per_seq == per_seq - 1
        after = jnp.where(ends_seq, 0.0, pltpu.roll(next8[...], SUBLANES - 1, 0))
        row = lax.broadcasted_iota(jnp.int32, (SUBLANES, RWKV_COLS), 0)
        tail = jnp.where(row == SUBLANES - 1, after, up[tile - SUBLANES:tile])
        d_rows = d_rows + jnp.concatenate([up[:tile - SUBLANES], tail], axis=0)
        next8[...] = d_prev[0:SUBLANES]
        dp_ref[...] = jnp.concatenate([d_rows] + [r[...] for r in sb_refs], axis=1).astype(dp_ref.dtype)

        @pl.when(first)
        def _():
            for ref, val in zip(acc_refs, grads[2:]):
                ref[...] = val

        @pl.when(jnp.logical_not(first))
        def _():
            for ref, val in zip(acc_refs, grads[2:]):
                ref[...] = ref[...] + val

    cols = RWKV_COLS + sum(t.shape[1] for t in d_sb)
    half = pl.BlockSpec((tile, RWKV_W), lambda i: (back(i), 0))
    par_specs = [pl.BlockSpec(t.shape, lambda i: (0, 0)) for t in pars]
    outs = pl.pallas_call(
        body, name="d_rwkv_prep", grid=(steps,),
        in_specs=[pl.BlockSpec((tile, RWKV_COLS), lambda i: (back(i), 0)),
                  pl.BlockSpec((1, SUBLANES, RWKV_COLS), lambda i: (back(i), 0, 0))] + par_specs + [half] * 10,
        out_specs=[pl.BlockSpec((tile, cols), lambda i: (back(i), 0))] + par_specs,
        out_shape=[jax.ShapeDtypeStruct((n_tok, cols), BF16)] + [jax.ShapeDtypeStruct(t.shape, F32) for t in pars],
        scratch_shapes=[pltpu.VMEM((SUBLANES, RWKV_COLS), F32)],
        compiler_params=_params(("arbitrary",)),
    )(p, edges, *pars, *cots, *d_sb)
    return outs[0], outs[1:]


RWKV_HP = 1


RWKV_PAIRS = 4


def _rwkv_specs(n_seq, chunk_of):
    width = RWKV_PAIRS * LANES
    row = pl.BlockSpec((n_seq, CHUNK, width), lambda g, c: (0, chunk_of(c), g))
    par = pl.BlockSpec((1, width), lambda g, c: (0, g))
    s0 = pl.BlockSpec((1, 1, RWKV_PAIRS * n_seq, LANES, LANES), lambda g, c: (g, chunk_of(c), 0, 0, 0))
    tinv = pl.BlockSpec((1, 1, 2 * RWKV_PAIRS * n_seq, CHUNK, CHUNK), lambda g, c: (g, chunk_of(c), 0, 0, 0))
    return row, par, s0, tinv


class Hosted:
    def __init__(self, arrays, gather):
        self.arrays, self.gather, self.n = list(arrays), gather, len(arrays)

    def split(self, refs, n_in, n_out):
        n = self.n
        ins, outs, sems = refs[n_in:n_in + n], refs[n_in + n + n_out:n_in + 2 * n + n_out], refs[-3:]
        own = refs[:n_in] + refs[n_in + n:n_in + n + n_out] + refs[n_in + 2 * n + n_out:-3]
        return own, exchange_copies(ins, outs, *sems, self.gather)

    def run(self, copies, first, last):
        @pl.when(first)
        def _():
            for cp in copies:
                cp.start()

        @pl.when(last)
        def _():
            for cp in copies:
                cp.wait()


class HostedChipGather(Hosted):
    def __init__(self, arrays):
        super().__init__(arrays, True)

    def split(self, refs, n_in, n_out):
        n = self.n
        ins, outs, sems = refs[n_in:n_in + n], refs[n_in + n + n_out:n_in + 2 * n + n_out], refs[-3:]
        own = refs[:n_in] + refs[n_in + n:n_in + n + n_out] + refs[n_in + 2 * n + n_out:-3]
        return own, (ins, outs, sems)

    def run(self, state, first, last):
        ins, outs, (send_sems, recv_sems, local_sems) = state
        x, y, c = lax.axis_index("x"), lax.axis_index("y"), lax.axis_index("c")
        chips = [((x + 1) % 2, y), (x, (y + 1) % 2), ((x + 1) % 2, (y + 1) % 2)]

        def block(a, k, of, to, src=None):
            dst = outs[a].at[4 * of[0] + 2 * of[1] + of[2]]
            return pltpu.make_async_remote_copy(
                src_ref=dst if src is None else src, dst_ref=dst, send_sem=send_sems.at[a, k],
                recv_sem=recv_sems.at[a, k], device_id=to, device_id_type=pl.DeviceIdType.MESH)

        me, sibling = (x, y, c), (x, y, 1 - c)
        local = [pltpu.make_async_copy(ins[a], outs[a].at[4 * x + 2 * y + c], local_sems.at[a]) for a in range(self.n)]
        mine = [block(a, 0, me, sibling, ins[a]) for a in range(self.n)]
        mine += [block(a, 1 + j, me, (*chip, c), ins[a]) for a in range(self.n) for j, chip in enumerate(chips)]
        passed = [block(a, 4 + j, (*chip, c), sibling) for a in range(self.n) for j, chip in enumerate(chips)]

        @pl.when(first)
        def _():
            for cp in local + mine:
                cp.start()

        @pl.when(last)
        def _():
            for a in range(self.n):
                for j, chip in enumerate(chips):
                    block(a, 1 + j, (*chip, c), me).wait_recv()
                    passed[a * len(chips) + j].start()
            for a in range(self.n):
                block(a, 0, sibling, me).wait_recv()
                for j, chip in enumerate(chips):
                    block(a, 4 + j, (*chip, 1 - c), me).wait_recv()
            for cp in mine + passed:
                cp.wait_send()
            for cp in local:
                cp.wait()


def rwkv_fwd(r, kraw, v, lw, asig, k_k, k_a, r_k, gn_w, gn_b, n_seq, seq_len, hosted):
    n_chunks = seq_len // CHUNK
    n_groups = RWKV_W // (RWKV_PAIRS * LANES)
    n_inst = RWKV_PAIRS * n_seq
    row, par, s0_spec, tinv_spec = _rwkv_specs(n_seq, lambda c: c)
    inst = [(s, pl.ds(pp * LANES, LANES)) for pp in range(RWKV_PAIRS) for s in range(n_seq)]

    def body(*refs):
        own, copies = hosted.split(refs, 10, 3)
        row_refs, par_refs, (z_ref, s0_ref, tinv_ref, state) = own[:5], own[5:10], own[10:]
        step = pl.program_id(0) * n_chunks + pl.program_id(1)
        hosted.run(copies, step == 0, step == n_groups * n_chunks - 1)

        @pl.when(pl.program_id(1) == 0)
        def _():
            state[...] = jnp.zeros_like(state)

        s0 = [state[i] for i in range(n_inst)]
        rows = [[ref[s, :, lanes] for s, lanes in inst] for ref in row_refs]
        pars = [[ref[:, lanes] for _, lanes in inst] for ref in par_refs]
        z, s1, tinv = rwkv_chunk(s0, *rows, *pars, RWKV_HP, with_tinv=True)
        for i, (s, lanes) in enumerate(inst):
            s0_ref[0, 0, i] = s0[i]
            z_ref[s, :, lanes] = z[i]
            state[i] = s1[i]
        for i, t in enumerate(tinv):
            tinv_ref[0, 0, i] = t

    outs = pl.pallas_call(
        body, name="rwkv_fwd", grid=(n_groups, n_chunks),
        in_specs=[row] * 5 + [par] * 5 + [ANY_SPEC] * hosted.n,
        out_specs=[row, s0_spec, tinv_spec] + [ANY_SPEC] * hosted.n,
        out_shape=[jax.ShapeDtypeStruct(r.shape, F32),
                   jax.ShapeDtypeStruct((n_groups, n_chunks, n_inst, LANES, LANES), F32),
                   jax.ShapeDtypeStruct((n_groups, n_chunks, 2 * n_inst, CHUNK, CHUNK), F32)]
        + exchange_shapes(hosted.arrays, hosted.gather),
        scratch_shapes=[pltpu.VMEM((n_inst, LANES, LANES), F32)] + exchange_sems(hosted.n),
        compiler_params=_params(("arbitrary", "arbitrary")),
    )(r, kraw, v, lw, asig, k_k, k_a, r_k, gn_w, gn_b, *hosted.arrays)
    return outs[0], outs[1], outs[2], outs[3:]


def rwkv_bwd(r, kraw, v, lw, asig, k_k, k_a, r_k, gn_w, gn_b, s0_all, tinv_all, dy, gate, n_seq, seq_len, hosted):
    n_chunks = seq_len // CHUNK
    n_groups = RWKV_W // (RWKV_PAIRS * LANES)
    n_inst = RWKV_PAIRS * n_seq
    row, par, s0_spec, tinv_spec = _rwkv_specs(n_seq, lambda c: n_chunks - 1 - c)
    inst = [(s, pl.ds(pp * LANES, LANES)) for pp in range(RWKV_PAIRS) for s in range(n_seq)]

    def body(*refs):
        own, copies = hosted.split(refs, 14, 10)
        row_refs, par_refs, (s0_ref, tinv_ref, dy_ref, gate_ref) = own[:5], own[5:10], own[10:14]
        drow_refs, dpar_refs, dstate = own[14:19], own[19:24], own[24]
        step = pl.program_id(0) * n_chunks + pl.program_id(1)
        hosted.run(copies, step == 0, step == n_groups * n_chunks - 1)
        first = pl.program_id(1) == 0

        @pl.when(first)
        def _():
            dstate[...] = jnp.zeros_like(dstate)

        fn = functools.partial(rwkv_chunk, hp=RWKV_HP, saved_tinv=[tinv_ref[0, 0, i] for i in range(2 * n_inst)])
        rows = [[ref[s, :, lanes] for s, lanes in inst] for ref in row_refs]
        pars = [[ref[:, lanes] for _, lanes in inst] for ref in par_refs]
        _, vjp = jax.vjp(fn, [s0_ref[0, 0, i] for i in range(n_inst)], *rows, *pars)
        dz = [dy_ref[s, :, lanes] * gate_ref[s, :, lanes] for s, lanes in inst]
        grads = vjp((dz, [dstate[i] for i in range(n_inst)]))
        for i, (s, lanes) in enumerate(inst):
            dstate[i] = grads[0][i]
            for ref, val in zip(drow_refs, grads[1:6]):
                ref[s, :, lanes] = val[i]

        def accumulate(start):
            for ref, val in zip(dpar_refs, grads[6:]):
                for pp in range(RWKV_PAIRS):
                    lanes = pl.ds(pp * LANES, LANES)
                    total = functools.reduce(jnp.add, val[pp * n_seq:(pp + 1) * n_seq])
                    ref[:, lanes] = total if start else ref[:, lanes] + total

        @pl.when(first)
        def _():
            accumulate(True)

        @pl.when(jnp.logical_not(first))
        def _():
            accumulate(False)

    rows_shape = jax.ShapeDtypeStruct(r.shape, F32)
    par_shape = jax.ShapeDtypeStruct((1, RWKV_W), F32)
    outs = pl.pallas_call(
        body, name="rwkv_bwd", grid=(n_groups, n_chunks),
        in_specs=[row] * 5 + [par] * 5 + [s0_spec, tinv_spec, row, row] + [ANY_SPEC] * hosted.n,
        out_specs=[row] * 5 + [par] * 5 + [ANY_SPEC] * hosted.n,
        out_shape=[rows_shape] * 5 + [par_shape] * 5 + exchange_shapes(hosted.arrays, hosted.gather),
        scratch_shapes=[pltpu.VMEM((n_inst, LANES, LANES), F32)] + exchange_sems(hosted.n),
        compiler_params=_params(("arbitrary", "arbitrary")),
    )(r, kraw, v, lw, asig, k_k, k_a, r_k, gn_w, gn_b, s0_all, tinv_all, dy, gate, *hosted.arrays)
    return outs[:10], outs[10:]


SB_Q0 = RWKV_COLS // LANES
SB_K0 = SB_Q0 + SB_W // LANES
SB_V0 = SB_K0 + SB_W // LANES
SB_SEQS = 2
SB_BUFFERS = pl.Buffered(1)
SB_SUM_PIECES = 2
SB_DEAD = -110.0


def _col_of(c_lo, c_hi):
    return jnp.where(_lane_lo((1, LANES)), c_lo, c_hi)


def sb_fwd(p, gain, n_seq, seq_len, hosted):
    n_pairs = SB_W // LANES
    n_q = seq_len // QB
    nb = min(SB_SEQS, n_seq)

    def seq_spec(c0):
        return pl.BlockSpec((nb, seq_len, LANES), functools.partial(lambda b, h, c0: (b, 0, c0 + h), c0=c0),
                            pipeline_mode=SB_BUFFERS)

    out_spec = pl.BlockSpec((nb, seq_len, LANES), lambda b, h: (b, 0, h), pipeline_mode=SB_BUFFERS)

    def body(*refs):
        own, copies = hosted.split(refs, 4, 4)
        q_ref, k_ref, v_ref, g_ref, y_ref, o_ref, tot_ref, first_ref = own
        step = pl.program_id(0) * n_pairs + pl.program_id(1)
        hosted.run(copies, step == 0, step == (n_seq // nb) * n_pairs - 1)
        gain = g_ref[...]

        def q_block(i, _):
            qs = pl.multiple_of(i * QB, QB)
            seqs = range(nb)
            zeros = [jnp.zeros((QB, 1), F32)] * nb
            qv = [sb_split_q(q_ref[s, pl.ds(qs, QB), :]) for s in seqs]
            add = lambda xs, ys: [x + y for x, y in zip(xs, ys)]

            def tiles(ks, c_lo, c_hi, diag):
                return sb_tile(qv, [k_ref[s, pl.ds(ks, QB), :] for s in seqs],
                               [v_ref[s, pl.ds(ks, QB), :] for s in seqs], c_lo, c_hi, diag)

            def alive(c_lo, c_hi):
                top = functools.reduce(jnp.maximum, list(c_lo) + list(c_hi))
                return jnp.max(top) > SB_DEAD

            def k_block(state):
                j, _, (o, c_lo, c_hi) = state
                o2, s_lo, s_hi = tiles(pl.multiple_of(j * QB, QB), c_lo, c_hi, False)
                c_lo, c_hi = add(c_lo, s_lo), add(c_hi, s_hi)
                return j - 1, alive(c_lo, c_hi), (add(o, o2), c_lo, c_hi)

            o, c_lo, c_hi = tiles(qs, zeros, zeros, True)
            j, _, (o, c_lo, c_hi) = lax.while_loop(lambda st: jnp.logical_and(st[0] >= 0, st[1]), k_block,
                                                   (i - 1, alive(c_lo, c_hi), (o, c_lo, c_hi)))
            first_ref[pl.program_id(0), pl.program_id(1), i] = j + 1
            for s in seqs:
                o_ref[s, pl.ds(qs, QB), :] = o[s]
                tot_ref[s, pl.ds(qs, QB), :] = jnp.broadcast_to(_col_of(c_lo[s], c_hi[s]), (QB, LANES))
                y_ref[s, pl.ds(qs, QB), :] = sb_post(o[s], gain)
            return 0

        lax.fori_loop(0, n_q, q_block, 0)

    shape = jax.ShapeDtypeStruct((n_seq, seq_len, SB_W), F32)
    return pl.pallas_call(
        body, name="sb_fwd", grid=(n_seq // nb, n_pairs),
        in_specs=[seq_spec(SB_Q0), seq_spec(SB_K0), seq_spec(SB_V0), pl.BlockSpec((1, LANES), lambda b, h: (0, h))]
        + [ANY_SPEC] * hosted.n,
        out_specs=[out_spec] * 3 + [pl.BlockSpec(memory_space=pltpu.SMEM)] + [ANY_SPEC] * hosted.n,
        out_shape=[shape] * 3 + [jax.ShapeDtypeStruct((n_seq // nb, n_pairs, n_q), jnp.int32)]
        + exchange_shapes(hosted.arrays, hosted.gather),
        scratch_shapes=exchange_sems(hosted.n),
        compiler_params=_params(("arbitrary", "arbitrary")),
    )(p, p, p, gain, *hosted.arrays)


def sb_bwd(p, gain, o_raw, tot, dy, first, n_seq, seq_len):
    n_pairs = SB_W // LANES
    n_q = seq_len // QB
    nb = min(SB_SEQS, n_seq)

    def seq_spec(c0):
        return pl.BlockSpec((nb, seq_len, LANES), functools.partial(lambda h, b, c0: (b, 0, c0 + h), c0=c0),
                            pipeline_mode=SB_BUFFERS)

    own = pl.BlockSpec((nb, seq_len, LANES), lambda h, b: (b, 0, h), pipeline_mode=SB_BUFFERS)
    par = pl.BlockSpec((1, LANES), lambda h, b: (0, h))

    def body(q_ref, k_ref, v_ref, g_ref, o_ref, tot_ref, dy_ref, first_ref, dq_ref, dk_ref, dv_ref, dg_ref):
        gain = g_ref[...]
        lo = _lane_lo((1, LANES))
        dk_ref[...] = jnp.zeros_like(dk_ref)
        dv_ref[...] = jnp.zeros_like(dv_ref)

        def q_block(i, dgain):
            qs = pl.multiple_of(i * QB, QB)
            seqs = range(nb)
            zeros = [jnp.zeros((QB, 1), F32)] * nb
            qv, dov, t_lo, t_hi = [], [], [], []
            for s in seqs:
                qv.append(sb_split_q(q_ref[s, pl.ds(qs, QB), :]))
                _, post_vjp = jax.vjp(sb_post, o_ref[s, pl.ds(qs, QB), :], gain)
                do, dg_s = post_vjp(dy_ref[s, pl.ds(qs, QB), :])
                dov.append(do)
                dgain = dgain + dg_s
                tot_s = tot_ref[s, pl.ds(qs, QB), :]
                t_lo.append(jnp.max(jnp.where(lo, tot_s, -jnp.inf), axis=1, keepdims=True))
                t_hi.append(jnp.max(jnp.where(lo, -jnp.inf, tot_s), axis=1, keepdims=True))
            add = lambda xs, ys: [x + y for x, y in zip(xs, ys)]
            sub = lambda xs, ys: [x - y for x, y in zip(xs, ys)]

            def tile(ks, carry, diag):
                dq, rem_lo, rem_hi, g_lo, g_hi = carry
                kv = [k_ref[s, pl.ds(ks, QB), :] for s in seqs]
                vv = [v_ref[s, pl.ds(ks, QB), :] for s in seqs]
                fn = functools.partial(sb_tile, diag=diag, from_here=list(zip(rem_lo, rem_hi)))
                (_, s_lo, s_hi), vjp = jax.vjp(fn, qv, kv, vv, zeros, zeros)
                dq_t, dk_t, dv_t, dc_lo, dc_hi = vjp((dov, g_lo, g_hi))
                dq_t = [jnp.where(lo, d_lo, d_hi) for d_lo, d_hi in dq_t]
                for s in seqs:
                    dk_ref[s, pl.ds(ks, QB), :] = dk_ref[s, pl.ds(ks, QB), :] + dk_t[s]
                    dv_ref[s, pl.ds(ks, QB), :] = dv_ref[s, pl.ds(ks, QB), :] + dv_t[s]
                return add(dq, dq_t), sub(rem_lo, s_lo), sub(rem_hi, s_hi), add(g_lo, dc_lo), add(g_hi, dc_hi)

            def k_block(j, carry):
                return tile(pl.multiple_of(j * QB, QB), carry, False)

            carry = ([jnp.zeros((QB, LANES), F32)] * nb, t_lo, t_hi, zeros, zeros)
            carry = lax.fori_loop(first_ref[pl.program_id(1), pl.program_id(0), i], i, k_block, carry)
            carry = tile(qs, carry, True)
            for s in seqs:
                dq_ref[s, pl.ds(qs, QB), :] = carry[0][s] * SB_SCALE
            return dgain

        dgain = lax.fori_loop(0, n_q, q_block, jnp.zeros((1, LANES), F32))
        first = pl.program_id(1) == 0

        @pl.when(first)
        def _():
            dg_ref[...] = dgain

        @pl.when(jnp.logical_not(first))
        def _():
            dg_ref[...] = dg_ref[...] + dgain

    shape = jax.ShapeDtypeStruct((n_seq, seq_len, SB_W), F32)
    return pl.pallas_call(
        body, name="sb_bwd", grid=(n_pairs, n_seq // nb),
        in_specs=[seq_spec(SB_Q0), seq_spec(SB_K0), seq_spec(SB_V0), par, own, own, seq_spec(RWKV_W // LANES),
                  pl.BlockSpec(memory_space=pltpu.SMEM)],
        out_specs=[own, own, own, par],
        out_shape=[shape, shape, shape, jax.ShapeDtypeStruct((1, SB_W), F32)],
        compiler_params=_params(("arbitrary", "arbitrary")),
    )(p, p, p, gain, o_raw, tot, dy, first)


def exchange(name, arrays, gather):
    n = len(arrays)

    def body(*refs):
        copies = exchange_copies(refs[:n], refs[n:2 * n], *refs[2 * n:], gather)
        for cp in copies:
            cp.start()
        for cp in copies:
            cp.wait()

    return pl.pallas_call(
        body, name=name, in_specs=[ANY_SPEC] * n, out_specs=[ANY_SPEC] * n, out_shape=exchange_shapes(arrays, gather),
        scratch_shapes=exchange_sems(n),
    )(*arrays)


ANY_SPEC = pl.BlockSpec(memory_space=pl.ANY)


def exchange_shapes(arrays, gather):
    return [jax.ShapeDtypeStruct(((N_DEV,) + a.shape) if gather else a.shape, a.dtype) for a in arrays]


def exchange_sems(n):
    return [pltpu.SemaphoreType.DMA((n, N_DEV - 1)), pltpu.SemaphoreType.DMA((n, N_DEV - 1)),
            pltpu.SemaphoreType.DMA((n,))]


def exchange_copies(ins, outs, send_sems, recv_sems, local_sems, gather):
    x, y, c = lax.axis_index("x"), lax.axis_index("y"), lax.axis_index("c")
    me = 4 * x + 2 * y + c
    copies = []
    for a, (src_all, dst_all) in enumerate(zip(ins, outs)):
        own = src_all if gather else src_all.at[me]
        copies.append(pltpu.make_async_copy(own, dst_all.at[me], local_sems.at[a]))
        for j in range(1, N_DEV):
            px, py, pc = (x + (j >> 2)) % 2, (y + ((j >> 1) & 1)) % 2, (c + (j & 1)) % 2
            src = src_all if gather else src_all.at[4 * px + 2 * py + pc]
            copies.append(pltpu.make_async_remote_copy(
                src_ref=src, dst_ref=dst_all.at[me], send_sem=send_sems.at[a, j - 1],
                recv_sem=recv_sems.at[a, j - 1], device_id=(px, py, pc), device_id_type=pl.DeviceIdType.MESH))
    return copies


def adamw(name, w, parts, m, v, tile):
    rows, cols = w.shape
    spec = pl.BlockSpec((tile, cols), lambda i: (i, 0))
    part_spec = pl.BlockSpec((N_DEV, tile, cols), lambda i: (0, i, 0))

    def body(w_ref, p_ref, m_ref, v_ref, g_ref, d_ref, nm_ref, nv_ref):
        g = p_ref[0].astype(F32)
        for s in range(1, N_DEV):
            g = g + p_ref[s].astype(F32)
        new_m = ADAM_B1 * m_ref[...] + (1.0 - ADAM_B1) * g
        new_v = ADAM_B2 * v_ref[...] + (1.0 - ADAM_B2) * (g * g)
        m_hat = new_m / (1.0 - ADAM_B1 ** ADAM_STEP)
        v_hat = new_v / (1.0 - ADAM_B2 ** ADAM_STEP)
        g_ref[...] = g
        d_ref[...] = -ADAM_LR * (m_hat / (jnp.sqrt(v_hat) + ADAM_EPS) + ADAM_WD * w_ref[...])
        nm_ref[...] = new_m
        nv_ref[...] = new_v

    shape = jax.ShapeDtypeStruct((rows, cols), F32)
    return pl.pallas_call(
        body, name=name, grid=(rows // tile,), in_specs=[spec, part_spec, spec, spec],
        out_specs=[spec] * 4, out_shape=[shape] * 4, compiler_params=_params(("arbitrary",)),
    )(w, parts, m, v)


SMALL = ("ln1_g", "tok_mu", "w0", "a0", "k_k", "k_a", "r_k", "gn_w", "gn_b", "sb_gain", "ln2_g", "lnf_g")
EARLY = ("w_in", "w_decay_up", "w_aaa_up", "w_gate_up")
LATE = ("w_out", "w_up", "w_down")
BIG = EARLY + LATE
ORDER = ("ln1_g", "w_in", "tok_mu", "w0", "w_decay_up", "a0", "w_aaa_up", "w_gate_up", "k_k", "k_a", "r_k",
         "gn_w", "gn_b", "sb_gain", "w_out", "ln2_g", "w_up", "w_down", "lnf_g")


def _pack(vectors, rows):
    flat = jnp.concatenate([v.reshape(-1).astype(F32) for v in vectors])
    return jnp.pad(flat, (0, rows * LANES - flat.shape[0])).reshape(rows, LANES)


def _full_cols(gathered):
    d, k, cols = gathered.shape
    return jnp.transpose(gathered, (1, 0, 2)).reshape(k, d * cols)


def _col_parts(full):
    k, n = full.shape
    return jnp.transpose(full.reshape(k, N_DEV, n // N_DEV), (1, 0, 2))


def kernel(x, ln1_g, w_in, tok_mu, w0, w_decay_up, a0, w_aaa_up, w_gate_up, k_k, k_a, r_k, gn_w, gn_b, sb_gain, w_out, ln2_g, w_up, w_down, lnf_g, loss_target, m_ln1_g, m_w_in, m_tok_mu, m_w0, m_w_decay_up, m_a0, m_w_aaa_up, m_w_gate_up, m_k_k, m_k_a, m_r_k, m_gn_w, m_gn_b, m_sb_gain, m_w_out, m_ln2_g, m_w_up, m_w_down, m_lnf_g, v_ln1_g, v_w_in, v_tok_mu, v_w0, v_w_decay_up, v_a0, v_w_aaa_up, v_w_gate_up, v_k_k, v_k_a, v_r_k, v_gn_w, v_gn_b, v_sb_gain, v_w_out, v_ln2_g, v_w_up, v_w_down, v_lnf_g):
    args = dict(locals())
    weights = {n: args[n] for n in ORDER}
    mom_m = {n: args["m_" + n] for n in ORDER}
    mom_v = {n: args["v_" + n] for n in ORDER}

    n_seq, seq_len, d_model = x.shape
    n_tok = n_seq * seq_len
    x2d = x.reshape(n_tok, d_model)
    tgt = loss_target.reshape(n_tok, d_model)
    row = lambda t: t.reshape(1, -1).astype(F32)

    g1 = row(ln1_g)
    shard = {n: weights[n][0].astype(BF16) for n in BIG}
    h1, *early = rw_call("norm1", lambda r, q: [f_norm(r[0], q[0])], [x2d], [g1], [(d_model, BF16)], 512,
                         hosted=HostedChipGather([shard[n] for n in EARLY]))
    gathered = dict(zip(EARLY, early))
    w_in_f = _full_cols(gathered["w_in"])
    zeros64 = jnp.zeros((HEAD_DIM, RWKV_W), BF16)
    wd_pad = jnp.concatenate([_full_cols(gathered["w_decay_up"]), zeros64], axis=0)
    wa_pad = jnp.concatenate([zeros64, _full_cols(gathered["w_aaa_up"])], axis=0)
    wg_f = _full_cols(gathered["w_gate_up"])
    in_cols = w_in_f.shape[1]

    mu, w0r, a0r = row(tok_mu), row(w0), row(a0)
    kkr, kar, rkr, gwr, gbr, sgr = row(k_k), row(k_a), row(r_k), row(gn_w), row(gn_b), row(sb_gain)
    g2, gf = row(ln2_g), row(lnf_g)

    p = matmul("proj_in", h1, w_in_f, "nn", F32, 1024, in_cols // 2, d_model)
    prep_pars = [mu, w0r, wd_pad, a0r, wa_pad, wg_f]
    (r_, kraw, v_, lw, asig, gate), prep_edges = rwkv_prep_fwd(p, prep_pars, seq_len)
    by_seq = lambda t: t.reshape(n_seq, seq_len, t.shape[-1])
    flat = lambda t: t.reshape(n_tok, t.shape[-1])
    rwkv_in = [by_seq(t) for t in (r_, kraw, v_, lw, asig)]
    z, s0_all, tinv_all, (w_out_g, w_up_g) = rwkv_fwd(
        *rwkv_in, kkr, kar, rkr, gwr, gbr, n_seq, seq_len, Hosted([shard["w_out"], shard["w_up"]], True))
    y_sb, o_raw, tot, sb_first, w_down_g = sb_fwd(by_seq(p), sgr, n_seq, seq_len, Hosted([shard["w_down"]], True))
    w_out_f = w_out_g.reshape(d_model, d_model)
    w_up_f = _full_cols(w_up_g)
    w_down_f = w_down_g.reshape(-1, d_model)
    d_ff = w_up_f.shape[1]
    z, y_sb = flat(z), flat(y_sb)
    (ycat,) = rw_call("mix_cat", lambda r, q: [jnp.concatenate([r[0] * r[1], r[2]], axis=1)],
                      [z, gate, y_sb], [], [(d_model, BF16)], 512)
    x2, h2 = matmul("proj_out", ycat, w_out_f, "nn", (F32, BF16), 512, d_model, d_model, extras=[x2d], pars=[g2],
                    epilogue=lambda acc, xv, g: (xv + acc, f_norm(xv + acc, g)))
    u, act = matmul("mlp_up", h2, w_up_f, "nn", (F32, BF16), 1024, d_ff // 2, d_model,
                    epilogue=lambda acc: (acc, jnp.square(jnp.maximum(acc, 0.0))))

    def loss_epilogue(acc, x2v, target, g):
        loss_rows, vjp = jax.vjp(lambda xv, gv: f_final(xv, gv, target), x2v + acc, g)
        dx3, dgf = vjp(jnp.ones_like(loss_rows))
        return dx3, dx3, jnp.broadcast_to(jnp.sum(loss_rows), (1, LANES)), dgf

    dx3, dx3_b, loss_acc, d_lnf = matmul(
        "mlp_down", act, w_down_f, "nn", (F32, BF16), 512, d_model, d_ff, extras=[x2, tgt], pars=[gf],
        epilogue=loss_epilogue, sums=[((1, LANES), F32), ((1, d_model), F32)])

    du = matmul("d_act", dx3_b, w_down_f, "nt", BF16, 1024, d_ff // 2, d_model, extras=[u],
                epilogue=lambda acc, uv: (acc * (2.0 * jnp.maximum(uv, 0.0)),))
    dw_down = matmul("dw_down", act, dx3_b, "tn", BF16, 512, d_model, 4096)
    dw_up = matmul("dw_up", h2, du, "tn", BF16, d_model, 512, 4096)

    def norm_bwd(acc, xv, dres, g):
        _, vjp = jax.vjp(f_norm, xv, g)
        dx, dg = vjp(acc)
        return dx + dres, dx + dres, dg

    dx2, dx2_b, d_ln2 = matmul("d_h2", du, w_up_f, "nt", (F32, BF16), 512, d_model, d_ff, extras=[x2, dx3], pars=[g2],
                               epilogue=norm_bwd, sums=[((1, d_model), F32)])

    dycat = matmul("d_ycat", dx2_b, w_out_f, "nt", F32, 512, d_model, d_model)
    dw_out = matmul("dw_out", ycat, dx2_b, "tn", BF16, d_model, d_model, 2048)
    dq, dk_sb, dv_sb, d_sg = sb_bwd(by_seq(p), sgr, o_raw, tot, by_seq(dycat), sb_first, n_seq, seq_len)
    d_sb = [flat(dq), flat(dk_sb), flat(dv_sb)]
    late_grads = {"w_out": dw_out.reshape(N_DEV, -1, d_model), "w_up": _col_parts(dw_up),
                  "w_down": dw_down.reshape(N_DEV, -1, d_model)}
    (dr, dkraw, dv, dlw, dasig, d_kk, d_ka, d_rk, d_gw, d_gb), late_parts = rwkv_bwd(
        *rwkv_in, kkr, kar, rkr, gwr, gbr, s0_all, tinv_all, by_seq(dycat), by_seq(gate), n_seq, seq_len,
        Hosted([late_grads[n] for n in LATE], False))
    prep_cots = [flat(t) for t in (dr, dkraw, dv, dlw, dasig)] + [dycat, z]
    dp, (d_mu, d_w0, d_wd, d_a0, d_wa, d_wg) = rwkv_prep_bwd(p, prep_edges, prep_pars, prep_cots, d_sb, seq_len)

    dw_in = matmul("dw_in", h1, dp, "tn", BF16, d_model, in_cols // 2, 2048)
    lora_parts = lambda t: _col_parts(t).astype(BF16)
    early_grads = {"w_in": _col_parts(dw_in), "w_decay_up": lora_parts(d_wd[:HEAD_DIM]),
                   "w_aaa_up": lora_parts(d_wa[HEAD_DIM:]), "w_gate_up": lora_parts(d_wg)}
    dx, d_ln1, *early_parts = matmul(
        "d_h1", dp, w_in_f, "nt", (F32,), 512, d_model, in_cols, extras=[x2d, dx2], pars=[g1],
        epilogue=lambda *t: norm_bwd(*t)[1:], sums=[((1, d_model), F32)],
        hosted=Hosted([early_grads[n] for n in EARLY], False))
    parts = dict(zip(EARLY, early_parts))
    parts.update(zip(LATE, late_parts))

    small_grads = {"ln1_g": d_ln1, "tok_mu": d_mu, "w0": d_w0, "a0": d_a0, "k_k": d_kk, "k_a": d_ka, "r_k": d_rk,
                   "gn_w": d_gw, "gn_b": d_gb, "sb_gain": d_sg, "ln2_g": d_ln2, "lnf_g": d_lnf}
    n_small = sum(int(weights[n].size) for n in SMALL)
    pack_rows = -(-(n_small + 1) // (8 * LANES)) * 8
    packed = _pack([small_grads[n] for n in SMALL] + [loss_acc[0, :1]], pack_rows)
    (small_parts,) = exchange("gather_small", [packed], True)

    results = {}
    for n in BIG:
        w2d = weights[n][0]
        tile = w2d.shape[0] if w2d.shape[0] <= 256 else 256
        results[n] = adamw("adamw_" + n, w2d, parts[n], mom_m[n][0], mom_v[n][0], tile)
    pk = lambda d: _pack([d[n] for n in SMALL] + [jnp.zeros((1,), F32)], pack_rows)
    sg, sd, sm, sv = adamw("adamw_small", pk(weights), small_parts, pk(mom_m), pk(mom_v), pack_rows)
    off = 0
    for n in SMALL:
        size = int(weights[n].size)
        results[n] = tuple(t.reshape(-1)[off:off + size] for t in (sg, sd, sm, sv))
        off += size
    loss = sg.reshape(-1)[off]

    out = [loss, dx.reshape(x.shape)]
    for kind in range(4):
        out += [results[n][kind].reshape(weights[n].shape) for n in ORDER]
    return tuple(out)
```

```python
import functools
import math

import jax
import jax.numpy as jnp
from jax import lax
from jax.experimental import pallas as pl
from jax.experimental.pallas import tpu as pltpu

F32 = jnp.float32
BF16 = jnp.bfloat16

N_DEV = 8
HEAD_DIM = 64
LANES = 128
RWKV_W = 512
SB_W = 512
LORA_WA = 128
GATE_LORA = 128
RWKV_COLS = 3 * RWKV_W + LORA_WA + GATE_LORA
RMS_EPS = 1e-5
GN_EPS = 64e-5
CHUNK = 64
QB = 256
SB_SCALE = HEAD_DIM ** -0.5
ADAM_LR, ADAM_B1, ADAM_B2, ADAM_EPS, ADAM_WD, ADAM_STEP = 0.001, 0.9, 0.999, 1e-08, 0.01, 10
VMEM_LIMIT = 56 * 1024 * 1024


_DIMS = {
    "nn": (((1,), (0,)), ((), ())),
    "nt": (((1,), (1,)), ((), ())),
    "tn": (((0,), (0,)), ((), ())),
}


def _pieces(x, n):
    if n == 1:
        return [x.astype(BF16)]
    out, rem = [], x.astype(F32)
    for i in range(n):
        p = rem.astype(BF16)
        out.append(p)
        if i + 1 < n:
            rem = rem - p.astype(F32)
    return out


def _dot(a, b, form, pa, pb):
    pieces_a, pieces_b = _pieces(a, pa), _pieces(b, pb)
    keep = max(pa, pb)
    acc = None
    for i, ai in enumerate(pieces_a):
        for j, bj in enumerate(pieces_b):
            if i + j >= keep:
                continue
            t = lax.dot_general(ai, bj, _DIMS[form], preferred_element_type=F32)
            acc = t if acc is None else acc + t
    return acc


BOTH = (True, True)


@functools.partial(jax.custom_vjp, nondiff_argnums=(2, 3, 4, 5, 6))
def mm(a, b, form, pa, pb, diff=BOTH, grad_pieces=None):
    return _dot(a, b, form, pa, pb)


def _mm_fwd(a, b, form, pa, pb, diff, grad_pieces):
    return _dot(a, b, form, pa, pb), (a, b)


def _mm_bwd(form, pa, pb, diff, grad_pieces, res, g):
    a, b = res
    pg = grad_pieces or max(pa, pb)
    da, db = jnp.zeros_like(a), jnp.zeros_like(b)
    if form == "nn":
        if diff[0]:
            da = mm(g, b, "nt", pg, pb)
        if diff[1]:
            db = mm(a, g, "tn", pa, pg)
    elif form == "nt":
        if diff[0]:
            da = mm(g, b, "nn", pg, pb)
        if diff[1]:
            db = mm(g, a, "tn", pg, pa)
    else:
        if diff[0]:
            da = mm(b, g, "nt", pb, pg)
        if diff[1]:
            db = mm(a, g, "nn", pa, pg)
    return da, db


mm.defvjp(_mm_fwd, _mm_bwd)


def _stack_rows(top, bottom):
    return jnp.concatenate([top, bottom], axis=0)


@jax.custom_vjp
def _split_rows(x):
    half = x.shape[0] // 2
    return x[:half], x[half:]


def _split_rows_fwd(x):
    return _split_rows(x), None


def _split_rows_bwd(_, grads):
    return (_stack_rows(*grads),)


_split_rows.defvjp(_split_rows_fwd, _split_rows_bwd)


def _lane_lo(shape):
    return lax.broadcasted_iota(jnp.int32, shape, len(shape) - 1) < HEAD_DIM


def _segsum(x):
    lo = _lane_lo(x.shape)
    s_lo = jnp.sum(jnp.where(lo, x, 0.0), axis=-1, keepdims=True)
    s_hi = jnp.sum(jnp.where(lo, 0.0, x), axis=-1, keepdims=True)
    return jnp.where(lo, s_lo, s_hi)


def _sigmoid(x):
    return 0.5 * (jnp.tanh(0.5 * x) + 1.0)


@jax.custom_vjp
def _log_one_minus_sigmoid(z):
    return -jnp.maximum(z, 0.0) - jnp.log(1.0 + jnp.exp(-jnp.abs(z)))


def _log_oms_fwd(z):
    out = _log_one_minus_sigmoid(z)
    return out, (z, out)


def _log_oms_bwd(res, g):
    z, out = res
    return (-g * jnp.exp(z + out),)


_log_one_minus_sigmoid.defvjp(_log_oms_fwd, _log_oms_bwd)


def f_norm(x, g):
    return x * lax.rsqrt(jnp.mean(x * x, axis=-1, keepdims=True) + RMS_EPS) * g


def f_prep(p, pprev, mu, w0, wd_pad, a0, wa_pad, wg):
    pm = p + mu * (pprev - p)
    r = pm[:, 0:RWKV_W]
    k = pm[:, RWKV_W:2 * RWKV_W]
    v = pm[:, 2 * RWKV_W:3 * RWKV_W]
    xwa = pm[:, 3 * RWKV_W:3 * RWKV_W + LORA_WA]
    xg = pm[:, 3 * RWKV_W + LORA_WA:RWKV_COLS]
    w = _log_one_minus_sigmoid(-(w0 + mm(jnp.tanh(xwa), wd_pad, "nn", 1, 1))) - 0.5
    lw = -jnp.exp(w)
    asig = _sigmoid(a0 + mm(xwa, wa_pad, "nn", 1, 1))
    gate = mm(_sigmoid(xg), wg, "nn", 1, 1)
    return r, k, v, lw, asig, gate


def _tri(n, kind):
    row = lax.broadcasted_iota(jnp.int32, (n, n), 0)
    col = lax.broadcasted_iota(jnp.int32, (n, n), 1)
    if kind == "lower_incl":
        return row >= col
    return row > col


@functools.partial(jax.custom_vjp, nondiff_argnums=(1,))
def _nilpotent_inverses(mats, hp):
    size = mats[0].shape[0]
    eye = (lax.broadcasted_iota(jnp.int32, (size, size), 0) == lax.broadcasted_iota(jnp.int32, (size, size), 1))
    tinv = [eye.astype(F32) + x for x in mats]
    pw = [mm(x, x, "nn", hp, hp) for x in mats]
    for _ in range(int(math.log2(size)) - 2):
        both = [_split_rows(mm(_stack_rows(t, x), x, "nn", hp, hp)) for t, x in zip(tinv, pw)]
        tinv = [t + tx for t, (tx, _) in zip(tinv, both)]
        pw = [xx for _, xx in both]
    return [t + mm(t, x, "nn", hp, hp) for t, x in zip(tinv, pw)]


def _nilpotent_inverses_fwd(mats, hp):
    tinv = _nilpotent_inverses(mats, hp)
    return tinv, tinv


def _nilpotent_inverses_bwd(hp, tinv, grads):
    right = [mm(g, t, "nt", hp, hp) for g, t in zip(grads, tinv)]
    return ([mm(t, x, "tn", hp, hp) for t, x in zip(tinv, right)],)


_nilpotent_inverses.defvjp(_nilpotent_inverses_fwd, _nilpotent_inverses_bwd)


@functools.partial(jax.custom_vjp, nondiff_argnums=(2,))
def _known_inverses(mats, tinv, hp):
    return list(tinv)


def _known_inverses_fwd(mats, tinv, hp):
    return list(tinv), tinv


def _known_inverses_bwd(hp, tinv, grads):
    return _nilpotent_inverses_bwd(hp, tinv, grads) + ([jnp.zeros_like(t) for t in tinv],)


_known_inverses.defvjp(_known_inverses_fwd, _known_inverses_bwd)


def rwkv_chunk(state, r, kraw, v, lw, asig, k_k, k_a, r_k, gn_w, gn_b, hp, saved_tinv=None, with_tinv=False):
    n = len(r)
    L = r[0].shape[0]
    lo = _lane_lo((1, LANES))
    masks = (lo, jnp.logical_not(lo))
    incl = _tri(L, "lower_incl")
    strict = _tri(L, "strict")
    tri = incl.astype(F32)
    kk = [x * w for x, w in zip(kraw, k_k)]
    kk = [x / jnp.maximum(jnp.sqrt(_segsum(x * x)), 1e-12) for x in kk]
    k = [x * (1.0 + (s - 1.0) * w) for x, s, w in zip(kraw, asig, k_a)]
    b = [x * s for x, s in zip(kk, asig)]
    c = [mm(tri, x, "nn", 1, 3, (False, True)) for x in lw]
    at = [-x * jnp.exp(ci - li) for x, ci, li in zip(kk, c, lw)]
    rt = [x * jnp.exp(ci) for x, ci in zip(r, c)]
    einv = [jnp.exp(-ci) for ci in c]
    bt = [x * e for x, e in zip(b, einv)]
    kt = [x * e for x, e in zip(k, einv)]
    inst = [(s, m) for s in range(n) for m in masks]
    ar_h = [_stack_rows(jnp.where(m, at[s], 0.0), jnp.where(m, rt[s], 0.0)) for s, m in inst]
    on_b = [_split_rows(mm(x, bt[s], "nt", hp, hp)) for x, (s, _) in zip(ar_h, inst)]
    on_k = [_split_rows(mm(x, kt[s], "nt", hp, hp)) for x, (s, _) in zip(ar_h, inst)]
    a_ab = [jnp.where(strict, x, 0.0) for x, _ in on_b]
    b_rb = [jnp.where(incl, x, 0.0) for _, x in on_b]
    a_ak = [jnp.where(strict, x, 0.0) for x, _ in on_k]
    b_rk = [jnp.where(incl, x, 0.0) for _, x in on_k]
    tinv = _nilpotent_inverses(a_ab, hp) if saved_tinv is None else _known_inverses(a_ab, saved_tinv, hp)
    on_state = [_split_rows(mm(x, state[s], "nt", hp, hp)) for x, (s, _) in zip(ar_h, inst)]
    on_v = [_split_rows(mm(_stack_rows(m1, m2), v[s], "nn", hp, hp)) for m1, m2, (s, _) in zip(a_ak, b_rk, inst)]
    u_h = [mm(t, sa + av, "nn", hp, hp) for t, (sa, _), (av, _) in zip(tinv, on_state, on_v)]
    y_h = [sr + mm(m1, u, "nn", hp, hp) + bv for (_, sr), m1, u, (_, bv) in zip(on_state, b_rb, u_h, on_v)]
    u_all = [jnp.where(lo, u_h[2 * s], u_h[2 * s + 1]) for s in range(n)]
    y_all = [jnp.where(lo, y_h[2 * s], y_h[2 * s + 1]) for s in range(n)]
    c_last = [jnp.sum(x, axis=0, keepdims=True) for x in lw]
    efwd = [jnp.exp(cl - ci) for cl, ci in zip(c_last, c)]
    new_state = [st * jnp.exp(cl) + mm(_stack_rows(u, vi), _stack_rows(bi * e, ki * e), "tn", hp, hp)
                 for st, cl, u, bi, e, vi, ki in zip(state, c_last, u_all, b, efwd, v, k)]
    row_head = lax.broadcasted_iota(jnp.int32, (LANES, LANES), 0) // HEAD_DIM
    col_head = lax.broadcasted_iota(jnp.int32, (LANES, LANES), 1) // HEAD_DIM
    new_state = [jnp.where(row_head == col_head, x, 0.0) for x in new_state]
    outs = []
    for y, ri, ki, vi, w_rk, w_gw, w_gb in zip(y_all, r, k, v, r_k, gn_w, gn_b):
        mean = _segsum(y) * (1.0 / HEAD_DIM)
        d = y - mean
        var = _segsum(d * d) * (1.0 / HEAD_DIM)
        yn = d * lax.rsqrt(var + GN_EPS) * w_gw + w_gb
        outs.append(yn + _segsum(ri * ki * w_rk) * vi)
    return (outs, new_state, tinv) if with_tinv else (outs, new_state)


def sb_tile(q, k, v, c_lo, c_hi, diag, from_here=None):
    n = len(q)
    lo = _lane_lo((1, LANES))
    below = _tri(QB, "strict")
    from_s = _tri(QB, "lower_incl").astype(F32)
    inst = [(s, h) for s in range(n) for h in (0, 1)]
    carry = [(c_lo[s], c_hi[s])[h] for s, h in inst]
    z = [mm(q[s][h], k[s], "nt", 1, 1) for s, h in inst]
    log_keep = [_log_one_minus_sigmoid(x) for x in z]
    if diag:
        log_keep = [jnp.where(below, x, 0.0) for x in log_keep]
    own = [jnp.sum(x, axis=1, keepdims=True) for x in log_keep]
    if from_here is not None:
        carry = [lax.stop_gradient(from_here[s][h] - o) + cr for (s, h), o, cr in zip(inst, own, carry)]
    tail = [mm(x, from_s, "nn", SB_SUM_PIECES, 1, (True, False), 1) for x in log_keep]
    log_a = [x + tl + cr for x, tl, cr in zip(z, tail, carry)]
    if diag:
        log_a = [jnp.where(below, x, -1e30) for x in log_a]
    att = [jnp.exp(x) for x in log_a]
    out_h = [mm(x, v[s], "nn", 1, 1) for x, (s, _) in zip(att, inst)]
    out = [jnp.where(lo, out_h[2 * s], out_h[2 * s + 1]) for s in range(n)]
    return out, [own[2 * s] for s in range(n)], [own[2 * s + 1] for s in range(n)]


def sb_split_q(q):
    lo = _lane_lo((1, LANES))
    qs = q * SB_SCALE
    return jnp.where(lo, qs, 0.0), jnp.where(lo, 0.0, qs)


def sb_post(o, gain):
    return o * lax.rsqrt(_segsum(o * o) * (1.0 / HEAD_DIM) + RMS_EPS) * gain


def f_final(x3, g, target):
    y = f_norm(x3, g)
    err = y - target
    return 0.5 * jnp.mean(err * err, axis=-1, keepdims=True)


def _params(sem):
    return pltpu.CompilerParams(dimension_semantics=sem, vmem_limit_bytes=VMEM_LIMIT)


def rw_call(name, body_fn, rows, pars, out_rows, tile, hosted=None):
    n_rows = rows[0].shape[0]
    tile = min(tile, n_rows)
    steps = n_rows // tile
    row_specs = [pl.BlockSpec((tile, arr.shape[1]), lambda i: (i, 0)) for arr in rows]
    par_specs = [pl.BlockSpec(p.shape, lambda i: (0, 0)) for p in pars]
    nr, npar, nor = len(rows), len(pars), len(out_rows)
    n_host = hosted.n if hosted is not None else 0

    def body(*refs):
        if hosted is not None:
            refs, copies = hosted.split(refs, nr + npar, nor)
            hosted.run(copies, pl.program_id(0) == 0, pl.program_id(0) == steps - 1)
        row_outs = body_fn([r[...] for r in refs[:nr]], [r[...] for r in refs[nr:nr + npar]])
        for ref, val in zip(refs[nr + npar:], row_outs):
            ref[...] = val.astype(ref.dtype)

    outs = pl.pallas_call(
        body, name=name, grid=(steps,), in_specs=row_specs + par_specs + [ANY_SPEC] * n_host,
        out_specs=[pl.BlockSpec((tile, c), lambda i: (i, 0)) for c, _ in out_rows] + [ANY_SPEC] * n_host,
        out_shape=[jax.ShapeDtypeStruct((n_rows, c), dt) for c, dt in out_rows]
        + (exchange_shapes(hosted.arrays, hosted.gather) if n_host else []),
        scratch_shapes=exchange_sems(n_host) if n_host else [],
        compiler_params=_params(("arbitrary",)),
    )(*rows, *pars, *(hosted.arrays if n_host else []))
    return outs


def matmul(name, a, b, form, out_dtype, tm, tn, tk, extras=(), pars=(), epilogue=None, sums=(), hosted=None):
    out_dtypes = out_dtype if isinstance(out_dtype, tuple) else (out_dtype,)
    tm, tn, tk = min(tm, a.shape[1 if form == "tn" else 0]), min(tn, b.shape[0 if form == "nt" else 1]), min(tk, a.shape[0 if form == "tn" else 1])
    if form == "nn":
        (m, kd), n = a.shape, b.shape[1]
        a_spec = pl.BlockSpec((tm, tk), lambda i, j, k: (i, k))
        b_spec = pl.BlockSpec((tk, tn), lambda i, j, k: (k, j))
    elif form == "nt":
        (m, kd), n = a.shape, b.shape[0]
        a_spec = pl.BlockSpec((tm, tk), lambda i, j, k: (i, k))
        b_spec = pl.BlockSpec((tn, tk), lambda i, j, k: (j, k))
    else:
        (kd, m), n = a.shape, b.shape[1]
        a_spec = pl.BlockSpec((tk, tm), lambda i, j, k: (k, i))
        b_spec = pl.BlockSpec((tk, tn), lambda i, j, k: (k, j))
    ksteps = kd // tk

    n_extra, n_par, n_out, n_sum = len(extras), len(pars), len(out_dtypes), len(sums)
    n_in = 2 + n_extra + n_par
    grid = (m // tm, n // tn, ksteps)

    def body(*refs):
        if hosted is not None:
            refs, copies = hosted.split(refs, n_in, n_out + n_sum)
            here = [pl.program_id(d) for d in range(3)]
            hosted.run(copies, functools.reduce(jnp.logical_and, [h == 0 for h in here]),
                       functools.reduce(jnp.logical_and, [h == g - 1 for h, g in zip(here, grid)]))
        a_ref, b_ref, rest = refs[0], refs[1], refs[2:]
        e_refs, o_refs = rest[:n_extra + n_par], rest[n_extra + n_par:n_extra + n_par + n_out]
        s_refs = rest[n_extra + n_par + n_out:n_extra + n_par + n_out + n_sum]
        kstep = pl.program_id(2)
        part = lax.dot_general(a_ref[...].astype(BF16), b_ref[...].astype(BF16), _DIMS[form],
                               preferred_element_type=F32)

        def finish(acc):
            outs = epilogue(acc, *[r[...] for r in e_refs]) if epilogue else (acc,)
            for ref, val in zip(o_refs, outs[:n_out]):
                ref[...] = val.astype(ref.dtype)
            if n_sum:
                first_tile = jnp.logical_and(pl.program_id(0) == 0, pl.program_id(1) == 0)

                @pl.when(first_tile)
                def _():
                    for ref, val in zip(s_refs, outs[n_out:]):
                        ref[...] = val.astype(ref.dtype)

                @pl.when(jnp.logical_not(first_tile))
                def _():
                    for ref, val in zip(s_refs, outs[n_out:]):
                        ref[...] = ref[...] + val.astype(ref.dtype)

        if ksteps == 1:
            finish(part)
            return
        acc_ref = rest[n_extra + n_par + n_out + n_sum]

        @pl.when(kstep == 0)
        def _():
            acc_ref[...] = part

        @pl.when(jnp.logical_and(kstep > 0, kstep < ksteps - 1))
        def _():
            acc_ref[...] = acc_ref[...] + part

        @pl.when(kstep == ksteps - 1)
        def _():
            finish(acc_ref[...] + part)

    out_spec = pl.BlockSpec((tm, tn), lambda i, j, k: (i, j))
    whole = lambda shape: pl.BlockSpec(shape, lambda i, j, k: (0,) * len(shape))
    n_host = hosted.n if hosted is not None else 0
    host_in = list(hosted.arrays) if hosted is not None else []
    host_out = exchange_shapes(hosted.arrays, hosted.gather) if hosted is not None else []
    outs = pl.pallas_call(
        body, name=name, grid=grid,
        in_specs=[a_spec, b_spec] + [out_spec] * n_extra + [whole(t.shape) for t in pars] + [ANY_SPEC] * n_host,
        out_specs=[out_spec] * n_out + [whole(s) for s, _ in sums] + [ANY_SPEC] * n_host,
        out_shape=[jax.ShapeDtypeStruct((m, n), dt) for dt in out_dtypes]
        + [jax.ShapeDtypeStruct(s, dt) for s, dt in sums] + host_out,
        scratch_shapes=([pltpu.VMEM((tm, tn), F32)] if ksteps > 1 else []) + (exchange_sems(n_host) if n_host else []),
        compiler_params=_params(("arbitrary",) * 3 if (sums or n_host) else ("parallel", "parallel", "arbitrary")),
    )(a, b, *extras, *pars, *host_in)
    return outs if (isinstance(out_dtype, tuple) or sums or n_host) else outs[0]


PREP_TILE = 512
PREP_TILE_BWD = 256
SUBLANES = 8


def _shift_in(rows, first):
    rolled = pltpu.roll(rows, 1, 0)
    row = lax.broadcasted_iota(jnp.int32, (SUBLANES, rows.shape[1]), 0)
    head = jnp.where(row == 0, first, rolled[0:SUBLANES])
    return jnp.concatenate([head, rolled[SUBLANES:]], axis=0), rolled


def rwkv_prep_fwd(p, pars, seq_len):
    n_tok = p.shape[0]
    tile = min(PREP_TILE, seq_len)
    tile_b = min(PREP_TILE_BWD, tile)
    steps, per_seq, sub = n_tok // tile, seq_len // tile, tile // tile_b
    n_par = len(pars)

    def body(p_ref, *rest):
        par_refs, out_refs, edge_ref, last8 = rest[:n_par], rest[n_par:n_par + 6], rest[n_par + 6], rest[n_par + 7]
        step = pl.program_id(0)

        @pl.when(step == 0)
        def _():
            last8[...] = jnp.zeros_like(last8)

        rows = p_ref[...]
        before = jnp.where(step % per_seq == 0, 0.0, pltpu.roll(last8[...], 1, 0))
        prev, rolled = _shift_in(rows, before)
        edge_ref[0] = prev[0:SUBLANES]
        for m in range(1, sub):
            edge_ref[m] = rolled[m * tile_b:m * tile_b + SUBLANES]
        last8[...] = rows[tile - SUBLANES:tile]
        for ref, val in zip(out_refs, f_prep(rows, prev, *[r[...] for r in par_refs])):
            ref[...] = val

    row_out = pl.BlockSpec((tile, RWKV_W), lambda i: (i, 0))
    outs = pl.pallas_call(
        body, name="rwkv_prep", grid=(steps,),
        in_specs=[pl.BlockSpec((tile, RWKV_COLS), lambda i: (i, 0))] + [pl.BlockSpec(t.shape, lambda i: (0, 0)) for t in pars],
        out_specs=[row_out] * 6 + [pl.BlockSpec((sub, SUBLANES, RWKV_COLS), lambda i: (i, 0, 0))],
        out_shape=[jax.ShapeDtypeStruct((n_tok, RWKV_W), F32)] * 6
        + [jax.ShapeDtypeStruct((steps * sub, SUBLANES, RWKV_COLS), F32)],
        scratch_shapes=[pltpu.VMEM((SUBLANES, RWKV_COLS), F32)],
        compiler_params=_params(("arbitrary",)),
    )(p, *pars)
    return outs[:6], outs[6]


def rwkv_prep_bwd(p, edges, pars, cots, d_sb, seq_len):
    n_tok = p.shape[0]
    tile = min(PREP_TILE_BWD, seq_len)
    steps, per_seq = n_tok // tile, seq_len // tile
    n_par = len(pars)
    back = lambda i: steps - 1 - i

    def body(p_ref, edge_ref, *rest):
        par_refs, rest = rest[:n_par], rest[n_par:]
        cot_refs, dy_ref, z_ref, sb_refs = rest[:5], rest[5], rest[6], rest[7:10]
        dp_ref, acc_refs, next8 = rest[10], rest[11:11 + n_par], rest[11 + n_par]
        step = pl.program_id(0)
        first = step == 0

        @pl.when(first)
        def _():
            next8[...] = jnp.zeros_like(next8)

        rows = p_ref[...]
        prev, _ = _shift_in(rows, edge_ref[0])
        _, vjp = jax.vjp(f_prep, rows, prev, *[r[...].astype(F32) for r in par_refs])
        grads = vjp(tuple(r[...] for r in cot_refs) + (dy_ref[...] * z_ref[...],))
        d_rows, d_prev = grads[0], grads[1]
        up = pltpu.roll(d_prev, tile - 1, 0)
        ends_seq = back(step) % per_seq == per_seq - 1
        after = jnp.where(ends_seq, 0.0, pltpu.roll(next8[...], SUBLANES - 1, 0))
        row = lax.broadcasted_iota(jnp.int32, (SUBLANES, RWKV_COLS), 0)
        tail = jnp.where(row == SUBLANES - 1, after, up[tile - SUBLANES:tile])
        d_rows = d_rows + jnp.concatenate([up[:tile - SUBLANES], tail], axis=0)
        next8[...] = d_prev[0:SUBLANES]
        dp_ref[...] = jnp.concatenate([d_rows] + [r[...] for r in sb_refs], axis=1).astype(dp_ref.dtype)

        @pl.when(first)
        def _():
            for ref, val in zip(acc_refs, grads[2:]):
                ref[...] = val

        @pl.when(jnp.logical_not(first))
        def _():
            for ref, val in zip(acc_refs, grads[2:]):
                ref[...] = ref[...] + val

    cols = RWKV_COLS + sum(t.shape[1] for t in d_sb)
    half = pl.BlockSpec((tile, RWKV_W), lambda i: (back(i), 0))
    par_specs = [pl.BlockSpec(t.shape, lambda i: (0, 0)) for t in pars]
    outs = pl.pallas_call(
        body, name="d_rwkv_prep", grid=(steps,),
        in_specs=[pl.BlockSpec((tile, RWKV_COLS), lambda i: (back(i), 0)),
                  pl.BlockSpec((1, SUBLANES, RWKV_COLS), lambda i: (back(i), 0, 0))] + par_specs + [half] * 10,
        out_specs=[pl.BlockSpec((tile, cols), lambda i: (back(i), 0))] + par_specs,
        out_shape=[jax.ShapeDtypeStruct((n_tok, cols), BF16)] + [jax.ShapeDtypeStruct(t.shape, F32) for t in pars],
        scratch_shapes=[pltpu.VMEM((SUBLANES, RWKV_COLS), F32)],
        compiler_params=_params(("arbitrary",)),
    )(p, edges, *pars, *cots, *d_sb)
    return outs[0], outs[1:]


RWKV_HP = 1


RWKV_PAIRS = 4


def _rwkv_specs(n_seq, chunk_of):
    width = RWKV_PAIRS * LANES
    row = pl.BlockSpec((n_seq, CHUNK, width), lambda g, c: (0, chunk_of(c), g))
    par = pl.BlockSpec((1, width), lambda g, c: (0, g))
    s0 = pl.BlockSpec((1, 1, RWKV_PAIRS * n_seq, LANES, LANES), lambda g, c: (g, chunk_of(c), 0, 0, 0))
    tinv = pl.BlockSpec((1, 1, 2 * RWKV_PAIRS * n_seq, CHUNK, CHUNK), lambda g, c: (g, chunk_of(c), 0, 0, 0))
    return row, par, s0, tinv


class Hosted:
    def __init__(self, arrays, gather):
        self.arrays, self.gather, self.n = list(arrays), gather, len(arrays)

    def split(self, refs, n_in, n_out):
        n = self.n
        ins, outs, sems = refs[n_in:n_in + n], refs[n_in + n + n_out:n_in + 2 * n + n_out], refs[-3:]
        own = refs[:n_in] + refs[n_in + n:n_in + n + n_out] + refs[n_in + 2 * n + n_out:-3]
        return own, exchange_copies(ins, outs, *sems, self.gather)

    def run(self, copies, first, last):
        @pl.when(first)
        def _():
            for cp in copies:
                cp.start()

        @pl.when(last)
        def _():
            for cp in copies:
                cp.wait()


class HostedChipGather(Hosted):
    def __init__(self, arrays):
        super().__init__(arrays, True)

    def split(self, refs, n_in, n_out):
        n = self.n
        ins, outs, sems = refs[n_in:n_in + n], refs[n_in + n + n_out:n_in + 2 * n + n_out], refs[-3:]
        own = refs[:n_in] + refs[n_in + n:n_in + n + n_out] + refs[n_in + 2 * n + n_out:-3]
        return own, (ins, outs, sems)

    def run(self, state, first, last):
        ins, outs, (send_sems, recv_sems, local_sems) = state
        x, y, c = lax.axis_index("x"), lax.axis_index("y"), lax.axis_index("c")
        chips = [((x + 1) % 2, y), (x, (y + 1) % 2), ((x + 1) % 2, (y + 1) % 2)]

        def block(a, k, of, to, src=None):
            dst = outs[a].at[4 * of[0] + 2 * of[1] + of[2]]
            return pltpu.make_async_remote_copy(
                src_ref=dst if src is None else src, dst_ref=dst, send_sem=send_sems.at[a, k],
                recv_sem=recv_sems.at[a, k], device_id=to, device_id_type=pl.DeviceIdType.MESH)

        me, sibling = (x, y, c), (x, y, 1 - c)
        local = [pltpu.make_async_copy(ins[a], outs[a].at[4 * x + 2 * y + c], local_sems.at[a]) for a in range(self.n)]
        mine = [block(a, 0, me, sibling, ins[a]) for a in range(self.n)]
        mine += [block(a, 1 + j, me, (*chip, c), ins[a]) for a in range(self.n) for j, chip in enumerate(chips)]
        passed = [block(a, 4 + j, (*chip, c), sibling) for a in range(self.n) for j, chip in enumerate(chips)]

        @pl.when(first)
        def _():
            for cp in local + mine:
                cp.start()

        @pl.when(last)
        def _():
            for a in range(self.n):
                for j, chip in enumerate(chips):
                    block(a, 1 + j, (*chip, c), me).wait_recv()
                    passed[a * len(chips) + j].start()
            for a in range(self.n):
                block(a, 0, sibling, me).wait_recv()
                for j, chip in enumerate(chips):
                    block(a, 4 + j, (*chip, 1 - c), me).wait_recv()
            for cp in mine + passed:
                cp.wait_send()
            for cp in local:
                cp.wait()


def rwkv_fwd(r, kraw, v, lw, asig, k_k, k_a, r_k, gn_w, gn_b, n_seq, seq_len, hosted):
    n_chunks = seq_len // CHUNK
    n_groups = RWKV_W // (RWKV_PAIRS * LANES)
    n_inst = RWKV_PAIRS * n_seq
    row, par, s0_spec, tinv_spec = _rwkv_specs(n_seq, lambda c: c)
    inst = [(s, pl.ds(pp * LANES, LANES)) for pp in range(RWKV_PAIRS) for s in range(n_seq)]

    def body(*refs):
        own, copies = hosted.split(refs, 10, 3)
        row_refs, par_refs, (z_ref, s0_ref, tinv_ref, state) = own[:5], own[5:10], own[10:]
        step = pl.program_id(0) * n_chunks + pl.program_id(1)
        hosted.run(copies, step == 0, step == n_groups * n_chunks - 1)

        @pl.when(pl.program_id(1) == 0)
        def _():
            state[...] = jnp.zeros_like(state)

        s0 = [state[i] for i in range(n_inst)]
        rows = [[ref[s, :, lanes] for s, lanes in inst] for ref in row_refs]
        pars = [[ref[:, lanes] for _, lanes in inst] for ref in par_refs]
        z, s1, tinv = rwkv_chunk(s0, *rows, *pars, RWKV_HP, with_tinv=True)
        for i, (s, lanes) in enumerate(inst):
            s0_ref[0, 0, i] = s0[i]
            z_ref[s, :, lanes] = z[i]
            state[i] = s1[i]
        for i, t in enumerate(tinv):
            tinv_ref[0, 0, i] = t

    outs = pl.pallas_call(
        body, name="rwkv_fwd", grid=(n_groups, n_chunks),
        in_specs=[row] * 5 + [par] * 5 + [ANY_SPEC] * hosted.n,
        out_specs=[row, s0_spec, tinv_spec] + [ANY_SPEC] * hosted.n,
        out_shape=[jax.ShapeDtypeStruct(r.shape, F32),
                   jax.ShapeDtypeStruct((n_groups, n_chunks, n_inst, LANES, LANES), F32),
                   jax.ShapeDtypeStruct((n_groups, n_chunks, 2 * n_inst, CHUNK, CHUNK), F32)]
        + exchange_shapes(hosted.arrays, hosted.gather),
        scratch_shapes=[pltpu.VMEM((n_inst, LANES, LANES), F32)] + exchange_sems(hosted.n),
        compiler_params=_params(("arbitrary", "arbitrary")),
    )(r, kraw, v, lw, asig, k_k, k_a, r_k, gn_w, gn_b, *hosted.arrays)
    return outs[0], outs[1], outs[2], outs[3:]


def rwkv_bwd(r, kraw, v, lw, asig, k_k, k_a, r_k, gn_w, gn_b, s0_all, tinv_all, dy, gate, n_seq, seq_len, hosted):
    n_chunks = seq_len // CHUNK
    n_groups = RWKV_W // (RWKV_PAIRS * LANES)
    n_inst = RWKV_PAIRS * n_seq
    row, par, s0_spec, tinv_spec = _rwkv_specs(n_seq, lambda c: n_chunks - 1 - c)
    inst = [(s, pl.ds(pp * LANES, LANES)) for pp in range(RWKV_PAIRS) for s in range(n_seq)]

    def body(*refs):
        own, copies = hosted.split(refs, 14, 10)
        row_refs, par_refs, (s0_ref, tinv_ref, dy_ref, gate_ref) = own[:5], own[5:10], own[10:14]
        drow_refs, dpar_refs, dstate = own[14:19], own[19:24], own[24]
        step = pl.program_id(0) * n_chunks + pl.program_id(1)
        hosted.run(copies, step == 0, step == n_groups * n_chunks - 1)
        first = pl.program_id(1) == 0

        @pl.when(first)
        def _():
            dstate[...] = jnp.zeros_like(dstate)

        fn = functools.partial(rwkv_chunk, hp=RWKV_HP, saved_tinv=[tinv_ref[0, 0, i] for i in range(2 * n_inst)])
        rows = [[ref[s, :, lanes] for s, lanes in inst] for ref in row_refs]
        pars = [[ref[:, lanes] for _, lanes in inst] for ref in par_refs]
        _, vjp = jax.vjp(fn, [s0_ref[0, 0, i] for i in range(n_inst)], *rows, *pars)
        dz = [dy_ref[s, :, lanes] * gate_ref[s, :, lanes] for s, lanes in inst]
        grads = vjp((dz, [dstate[i] for i in range(n_inst)]))
        for i, (s, lanes) in enumerate(inst):
            dstate[i] = grads[0][i]
            for ref, val in zip(drow_refs, grads[1:6]):
                ref[s, :, lanes] = val[i]

        def accumulate(start):
            for ref, val in zip(dpar_refs, grads[6:]):
                for pp in range(RWKV_PAIRS):
                    lanes = pl.ds(pp * LANES, LANES)
                    total = functools.reduce(jnp.add, val[pp * n_seq:(pp + 1) * n_seq])
                    ref[:, lanes] = total if start else ref[:, lanes] + total

        @pl.when(first)
        def _():
            accumulate(True)

        @pl.when(jnp.logical_not(first))
        def _():
            accumulate(False)

    rows_shape = jax.ShapeDtypeStruct(r.shape, F32)
    par_shape = jax.ShapeDtypeStruct((1, RWKV_W), F32)
    outs = pl.pallas_call(
        body, name="rwkv_bwd", grid=(n_groups, n_chunks),
        in_specs=[row] * 5 + [par] * 5 + [s0_spec, tinv_spec, row, row] + [ANY_SPEC] * hosted.n,
        out_specs=[row] * 5 + [par] * 5 + [ANY_SPEC] * hosted.n,
        out_shape=[rows_shape] * 5 + [par_shape] * 5 + exchange_shapes(hosted.arrays, hosted.gather),
        scratch_shapes=[pltpu.VMEM((n_inst, LANES, LANES), F32)] + exchange_sems(hosted.n),
        compiler_params=_params(("arbitrary", "arbitrary")),
    )(r, kraw, v, lw, asig, k_k, k_a, r_k, gn_w, gn_b, s0_all, tinv_all, dy, gate, *hosted.arrays)
    return outs[:10], outs[10:]


SB_Q0 = RWKV_COLS // LANES
SB_K0 = SB_Q0 + SB_W // LANES
SB_V0 = SB_K0 + SB_W // LANES
SB_SEQS = 2
SB_BUFFERS = pl.Buffered(1)
SB_SUM_PIECES = 2
SB_DEAD = -110.0


def _col_of(c_lo, c_hi):
    return jnp.where(_lane_lo((1, LANES)), c_lo, c_hi)


def sb_fwd(p, gain, n_seq, seq_len, hosted):
    n_pairs = SB_W // LANES
    n_q = seq_len // QB
    nb = min(SB_SEQS, n_seq)

    def seq_spec(c0):
        return pl.BlockSpec((nb, seq_len, LANES), functools.partial(lambda b, h, c0: (b, 0, c0 + h), c0=c0),
                            pipeline_mode=SB_BUFFERS)

    out_spec = pl.BlockSpec((nb, seq_len, LANES), lambda b, h: (b, 0, h), pipeline_mode=SB_BUFFERS)

    def body(*refs):
        own, copies = hosted.split(refs, 4, 4)
        q_ref, k_ref, v_ref, g_ref, y_ref, o_ref, tot_ref, first_ref = own
        step = pl.program_id(0) * n_pairs + pl.program_id(1)
        hosted.run(copies, step == 0, step == (n_seq // nb) * n_pairs - 1)
        gain = g_ref[...]

        def q_block(i, _):
            qs = pl.multiple_of(i * QB, QB)
            seqs = range(nb)
            zeros = [jnp.zeros((QB, 1), F32)] * nb
            qv = [sb_split_q(q_ref[s, pl.ds(qs, QB), :]) for s in seqs]
            add = lambda xs, ys: [x + y for x, y in zip(xs, ys)]

            def tiles(ks, c_lo, c_hi, diag):
                return sb_tile(qv, [k_ref[s, pl.ds(ks, QB), :] for s in seqs],
                               [v_ref[s, pl.ds(ks, QB), :] for s in seqs], c_lo, c_hi, diag)

            def alive(c_lo, c_hi):
                top = functools.reduce(jnp.maximum, list(c_lo) + list(c_hi))
                return jnp.max(top) > SB_DEAD

            def k_block(state):
                j, _, (o, c_lo, c_hi) = state
                o2, s_lo, s_hi = tiles(pl.multiple_of(j * QB, QB), c_lo, c_hi, False)
                c_lo, c_hi = add(c_lo, s_lo), add(c_hi, s_hi)
                return j - 1, alive(c_lo, c_hi), (add(o, o2), c_lo, c_hi)

            o, c_lo, c_hi = tiles(qs, zeros, zeros, True)
            j, _, (o, c_lo, c_hi) = lax.while_loop(lambda st: jnp.logical_and(st[0] >= 0, st[1]), k_block,
                                                   (i - 1, alive(c_lo, c_hi), (o, c_lo, c_hi)))
            first_ref[pl.program_id(0), pl.program_id(1), i] = j + 1
            for s in seqs:
                o_ref[s, pl.ds(qs, QB), :] = o[s]
                tot_ref[s, pl.ds(qs, QB), :] = jnp.broadcast_to(_col_of(c_lo[s], c_hi[s]), (QB, LANES))
                y_ref[s, pl.ds(qs, QB), :] = sb_post(o[s], gain)
            return 0

        lax.fori_loop(0, n_q, q_block, 0)

    shape = jax.ShapeDtypeStruct((n_seq, seq_len, SB_W), F32)
    return pl.pallas_call(
        body, name="sb_fwd", grid=(n_seq // nb, n_pairs),
        in_specs=[seq_spec(SB_Q0), seq_spec(SB_K0), seq_spec(SB_V0), pl.BlockSpec((1, LANES), lambda b, h: (0, h))]
        + [ANY_SPEC] * hosted.n,
        out_specs=[out_spec] * 3 + [pl.BlockSpec(memory_space=pltpu.SMEM)] + [ANY_SPEC] * hosted.n,
        out_shape=[shape] * 3 + [jax.ShapeDtypeStruct((n_seq // nb, n_pairs, n_q), jnp.int32)]
        + exchange_shapes(hosted.arrays, hosted.gather),
        scratch_shapes=exchange_sems(hosted.n),
        compiler_params=_params(("arbitrary", "arbitrary")),
    )(p, p, p, gain, *hosted.arrays)


def sb_bwd(p, gain, o_raw, tot, dy, first, n_seq, seq_len):
    n_pairs = SB_W // LANES
    n_q = seq_len // QB
    nb = min(SB_SEQS, n_seq)

    def seq_spec(c0):
        return pl.BlockSpec((nb, seq_len, LANES), functools.partial(lambda h, b, c0: (b, 0, c0 + h), c0=c0),
                            pipeline_mode=SB_BUFFERS)

    own = pl.BlockSpec((nb, seq_len, LANES), lambda h, b: (b, 0, h), pipeline_mode=SB_BUFFERS)
    par = pl.BlockSpec((1, LANES), lambda h, b: (0, h))

    def body(q_ref, k_ref, v_ref, g_ref, o_ref, tot_ref, dy_ref, first_ref, dq_ref, dk_ref, dv_ref, dg_ref):
        gain = g_ref[...]
        lo = _lane_lo((1, LANES))
        dk_ref[...] = jnp.zeros_like(dk_ref)
        dv_ref[...] = jnp.zeros_like(dv_ref)

        def q_block(i, dgain):
            qs = pl.multiple_of(i * QB, QB)
            seqs = range(nb)
            zeros = [jnp.zeros((QB, 1), F32)] * nb
            qv, dov, t_lo, t_hi = [], [], [], []
            for s in seqs:
                qv.append(sb_split_q(q_ref[s, pl.ds(qs, QB), :]))
                _, post_vjp = jax.vjp(sb_post, o_ref[s, pl.ds(qs, QB), :], gain)
                do, dg_s = post_vjp(dy_ref[s, pl.ds(qs, QB), :])
                dov.append(do)
                dgain = dgain + dg_s
                tot_s = tot_ref[s, pl.ds(qs, QB), :]
                t_lo.append(jnp.max(jnp.where(lo, tot_s, -jnp.inf), axis=1, keepdims=True))
                t_hi.append(jnp.max(jnp.where(lo, -jnp.inf, tot_s), axis=1, keepdims=True))
            add = lambda xs, ys: [x + y for x, y in zip(xs, ys)]
            sub = lambda xs, ys: [x - y for x, y in zip(xs, ys)]

            def tile(ks, carry, diag):
                dq, rem_lo, rem_hi, g_lo, g_hi = carry
                kv = [k_ref[s, pl.ds(ks, QB), :] for s in seqs]
                vv = [v_ref[s, pl.ds(ks, QB), :] for s in seqs]
                fn = functools.partial(sb_tile, diag=diag, from_here=list(zip(rem_lo, rem_hi)))
                (_, s_lo, s_hi), vjp = jax.vjp(fn, qv, kv, vv, zeros, zeros)
                dq_t, dk_t, dv_t, dc_lo, dc_hi = vjp((dov, g_lo, g_hi))
                dq_t = [jnp.where(lo, d_lo, d_hi) for d_lo, d_hi in dq_t]
                for s in seqs:
                    dk_ref[s, pl.ds(ks, QB), :] = dk_ref[s, pl.ds(ks, QB), :] + dk_t[s]
                    dv_ref[s, pl.ds(ks, QB), :] = dv_ref[s, pl.ds(ks, QB), :] + dv_t[s]
                return add(dq, dq_t), sub(rem_lo, s_lo), sub(rem_hi, s_hi), add(g_lo, dc_lo), add(g_hi, dc_hi)

            def k_block(j, carry):
                return tile(pl.multiple_of(j * QB, QB), carry, False)

            carry = ([jnp.zeros((QB, LANES), F32)] * nb, t_lo, t_hi, zeros, zeros)
            carry = lax.fori_loop(first_ref[pl.program_id(1), pl.program_id(0), i], i, k_block, carry)
            carry = tile(qs, carry, True)
            for s in seqs:
                dq_ref[s, pl.ds(qs, QB), :] = carry[0][s] * SB_SCALE
            return dgain

        dgain = lax.fori_loop(0, n_q, q_block, jnp.zeros((1, LANES), F32))
        first = pl.program_id(1) == 0

        @pl.when(first)
        def _():
            dg_ref[...] = dgain

        @pl.when(jnp.logical_not(first))
        def _():
            dg_ref[...] = dg_ref[...] + dgain

    shape = jax.ShapeDtypeStruct((n_seq, seq_len, SB_W), F32)
    return pl.pallas_call(
        body, name="sb_bwd", grid=(n_pairs, n_seq // nb),
        in_specs=[seq_spec(SB_Q0), seq_spec(SB_K0), seq_spec(SB_V0), par, own, own, seq_spec(RWKV_W // LANES),
                  pl.BlockSpec(memory_space=pltpu.SMEM)],
        out_specs=[own, own, own, par],
        out_shape=[shape, shape, shape, jax.ShapeDtypeStruct((1, SB_W), F32)],
        compiler_params=_params(("arbitrary", "arbitrary")),
    )(p, p, p, gain, o_raw, tot, dy, first)


def exchange(name, arrays, gather):
    n = len(arrays)

    def body(*refs):
        copies = exchange_copies(refs[:n], refs[n:2 * n], *refs[2 * n:], gather)
        for cp in copies:
            cp.start()
        for cp in copies:
            cp.wait()

    return pl.pallas_call(
        body, name=name, in_specs=[ANY_SPEC] * n, out_specs=[ANY_SPEC] * n, out_shape=exchange_shapes(arrays, gather),
        scratch_shapes=exchange_sems(n),
    )(*arrays)


ANY_SPEC = pl.BlockSpec(memory_space=pl.ANY)


def exchange_shapes(arrays, gather):
    return [jax.ShapeDtypeStruct(((N_DEV,) + a.shape) if gather else a.shape, a.dtype) for a in arrays]


def exchange_sems(n):
    return [pltpu.SemaphoreType.DMA((n, N_DEV - 1)), pltpu.SemaphoreType.DMA((n, N_DEV - 1)),
            pltpu.SemaphoreType.DMA((n,))]


def exchange_copies(ins, outs, send_sems, recv_sems, local_sems, gather):
    x, y, c = lax.axis_index("x"), lax.axis_index("y"), lax.axis_index("c")
    me = 4 * x + 2 * y + c
    copies = []
    for a, (src_all, dst_all) in enumerate(zip(ins, outs)):
        own = src_all if gather else src_all.at[me]
        copies.append(pltpu.make_async_copy(own, dst_all.at[me], local_sems.at[a]))
        for j in range(1, N_DEV):
            px, py, pc = (x + (j >> 2)) % 2, (y + ((j >> 1) & 1)) % 2, (c + (j & 1)) % 2
            src = src_all if gather else src_all.at[4 * px + 2 * py + pc]
            copies.append(pltpu.make_async_remote_copy(
                src_ref=src, dst_ref=dst_all.at[me], send_sem=send_sems.at[a, j - 1],
                recv_sem=recv_sems.at[a, j - 1], device_id=(px, py, pc), device_id_type=pl.DeviceIdType.MESH))
    return copies


def adamw(name, w, parts, m, v, tile):
    rows, cols = w.shape
    spec = pl.BlockSpec((tile, cols), lambda i: (i, 0))
    part_spec = pl.BlockSpec((N_DEV, tile, cols), lambda i: (0, i, 0))

    def body(w_ref, p_ref, m_ref, v_ref, g_ref, d_ref, nm_ref, nv_ref):
        g = p_ref[0].astype(F32)
        for s in range(1, N_DEV):
            g = g + p_ref[s].astype(F32)
        new_m = ADAM_B1 * m_ref[...] + (1.0 - ADAM_B1) * g
        new_v = ADAM_B2 * v_ref[...] + (1.0 - ADAM_B2) * (g * g)
        m_hat = new_m / (1.0 - ADAM_B1 ** ADAM_STEP)
        v_hat = new_v / (1.0 - ADAM_B2 ** ADAM_STEP)
        g_ref[...] = g
        d_ref[...] = -ADAM_LR * (m_hat / (jnp.sqrt(v_hat) + ADAM_EPS) + ADAM_WD * w_ref[...])
        nm_ref[...] = new_m
        nv_ref[...] = new_v

    shape = jax.ShapeDtypeStruct((rows, cols), F32)
    return pl.pallas_call(
        body, name=name, grid=(rows // tile,), in_specs=[spec, part_spec, spec, spec],
        out_specs=[spec] * 4, out_shape=[shape] * 4, compiler_params=_params(("arbitrary",)),
    )(w, parts, m, v)


SMALL = ("ln1_g", "tok_mu", "w0", "a0", "k_k", "k_a", "r_k", "gn_w", "gn_b", "sb_gain", "ln2_g", "lnf_g")
EARLY = ("w_in", "w_decay_up", "w_aaa_up", "w_gate_up")
LATE = ("w_out", "w_up", "w_down")
BIG = EARLY + LATE
ORDER = ("ln1_g", "w_in", "tok_mu", "w0", "w_decay_up", "a0", "w_aaa_up", "w_gate_up", "k_k", "k_a", "r_k",
         "gn_w", "gn_b", "sb_gain", "w_out", "ln2_g", "w_up", "w_down", "lnf_g")


def _pack(vectors, rows):
    flat = jnp.concatenate([v.reshape(-1).astype(F32) for v in vectors])
    return jnp.pad(flat, (0, rows * LANES - flat.shape[0])).reshape(rows, LANES)


def _full_cols(gathered):
    d, k, cols = gathered.shape
    return jnp.transpose(gathered, (1, 0, 2)).reshape(k, d * cols)


def _col_parts(full):
    k, n = full.shape
    return jnp.transpose(full.reshape(k, N_DEV, n // N_DEV), (1, 0, 2))


def kernel(x, ln1_g, w_in, tok_mu, w0, w_decay_up, a0, w_aaa_up, w_gate_up, k_k, k_a, r_k, gn_w, gn_b, sb_gain, w_out, ln2_g, w_up, w_down, lnf_g, loss_target, m_ln1_g, m_w_in, m_tok_mu, m_w0, m_w_decay_up, m_a0, m_w_aaa_up, m_w_gate_up, m_k_k, m_k_a, m_r_k, m_gn_w, m_gn_b, m_sb_gain, m_w_out, m_ln2_g, m_w_up, m_w_down, m_lnf_g, v_ln1_g, v_w_in, v_tok_mu, v_w0, v_w_decay_up, v_a0, v_w_aaa_up, v_w_gate_up, v_k_k, v_k_a, v_r_k, v_gn_w, v_gn_b, v_sb_gain, v_w_out, v_ln2_g, v_w_up, v_w_down, v_lnf_g):
    args = dict(locals())
    weights = {n: args[n] for n in ORDER}
    mom_m = {n: args["m_" + n] for n in ORDER}
    mom_v = {n: args["v_" + n] for n in ORDER}

    n_seq, seq_len, d_model = x.shape
    n_tok = n_seq * seq_len
    x2d = x.reshape(n_tok, d_model)
    tgt = loss_target.reshape(n_tok, d_model)
    row = lambda t: t.reshape(1, -1).astype(F32)

    g1 = row(ln1_g)
    shard = {n: weights[n][0].astype(BF16) for n in BIG}
    h1, *early = rw_call("norm1", lambda r, q: [f_norm(r[0], q[0])], [x2d], [g1], [(d_model, BF16)], 512,
                         hosted=HostedChipGather([shard[n] for n in EARLY]))
    gathered = dict(zip(EARLY, early))
    w_in_f = _full_cols(gathered["w_in"])
    zeros64 = jnp.zeros((HEAD_DIM, RWKV_W), BF16)
    wd_pad = jnp.concatenate([_full_cols(gathered["w_decay_up"]), zeros64], axis=0)
    wa_pad = jnp.concatenate([zeros64, _full_cols(gathered["w_aaa_up"])], axis=0)
    wg_f = _full_cols(gathered["w_gate_up"])
    in_cols = w_in_f.shape[1]

    mu, w0r, a0r = row(tok_mu), row(w0), row(a0)
    kkr, kar, rkr, gwr, gbr, sgr = row(k_k), row(k_a), row(r_k), row(gn_w), row(gn_b), row(sb_gain)
    g2, gf = row(ln2_g), row(lnf_g)

    p = matmul("proj_in", h1, w_in_f, "nn", F32, 1024, in_cols // 2, d_model)
    prep_pars = [mu, w0r, wd_pad, a0r, wa_pad, wg_f]
    (r_, kraw, v_, lw, asig, gate), prep_edges = rwkv_prep_fwd(p, prep_pars, seq_len)
    by_seq = lambda t: t.reshape(n_seq, seq_len, t.shape[-1])
    flat = lambda t: t.reshape(n_tok, t.shape[-1])
    rwkv_in = [by_seq(t) for t in (r_, kraw, v_, lw, asig)]
    z, s0_all, tinv_all, (w_out_g, w_up_g) = rwkv_fwd(
        *rwkv_in, kkr, kar, rkr, gwr, gbr, n_seq, seq_len, Hosted([shard["w_out"], shard["w_up"]], True))
    y_sb, o_raw, tot, sb_first, w_down_g = sb_fwd(by_seq(p), sgr, n_seq, seq_len, Hosted([shard["w_down"]], True))
    w_out_f = w_out_g.reshape(d_model, d_model)
    w_up_f = _full_cols(w_up_g)
    w_down_f = w_down_g.reshape(-1, d_model)
    d_ff = w_up_f.shape[1]
    z, y_sb = flat(z), flat(y_sb)
    (ycat,) = rw_call("mix_cat", lambda r, q: [jnp.concatenate([r[0] * r[1], r[2]], axis=1)],
                      [z, gate, y_sb], [], [(d_model, BF16)], 512)
    x2, h2 = matmul("proj_out", ycat, w_out_f, "nn", (F32, BF16), 1024, d_model, d_model, extras=[x2d], pars=[g2],
                    epilogue=lambda acc, xv, g: (xv + acc, f_norm(xv + acc, g)))
    u, act = matmul("mlp_up", h2, w_up_f, "nn", (F32, BF16), 1024, d_ff // 2, d_model,
                    epilogue=lambda acc: (acc, jnp.square(jnp.maximum(acc, 0.0))))

    def loss_epilogue(acc, x2v, target, g):
        loss_rows, vjp = jax.vjp(lambda xv, gv: f_final(xv, gv, target), x2v + acc, g)
        dx3, dgf = vjp(jnp.ones_like(loss_rows))
        return dx3, dx3, jnp.broadcast_to(jnp.sum(loss_rows), (1, LANES)), dgf

    dx3, dx3_b, loss_acc, d_lnf = matmul(
        "mlp_down", act, w_down_f, "nn", (F32, BF16), 512, d_model, d_ff, extras=[x2, tgt], pars=[gf],
        epilogue=loss_epilogue, sums=[((1, LANES), F32), ((1, d_model), F32)])

    du = matmul("d_act", dx3_b, w_down_f, "nt", BF16, 1024, d_ff // 4, d_model, extras=[u],
                epilogue=lambda acc, uv: (acc * (2.0 * jnp.maximum(uv, 0.0)),))
    dw_down = matmul("dw_down", act, dx3_b, "tn", BF16, 512, d_model, 4096)
    dw_up = matmul("dw_up", h2, du, "tn", BF16, d_model, 512, 4096)

    def norm_bwd(acc, xv, dres, g):
        _, vjp = jax.vjp(f_norm, xv, g)
        dx, dg = vjp(acc)
        return dx + dres, dx + dres, dg

    dx2, dx2_b, d_ln2 = matmul("d_h2", du, w_up_f, "nt", (F32, BF16), 512, d_model, d_ff, extras=[x2, dx3], pars=[g2],
                               epilogue=norm_bwd, sums=[((1, d_model), F32)])

    dycat = matmul("d_ycat", dx2_b, w_out_f, "nt", F32, 1024, d_model, d_model)
    dw_out = matmul("dw_out", ycat, dx2_b, "tn", BF16, d_model, d_model, 2048)
    dq, dk_sb, dv_sb, d_sg = sb_bwd(by_seq(p), sgr, o_raw, tot, by_seq(dycat), sb_first, n_seq, seq_len)
    d_sb = [flat(dq), flat(dk_sb), flat(dv_sb)]
    late_grads = {"w_out": dw_out.reshape(N_DEV, -1, d_model), "w_up": _col_parts(dw_up),
                  "w_down": dw_down.reshape(N_DEV, -1, d_model)}
    (dr, dkraw, dv, dlw, dasig, d_kk, d_ka, d_rk, d_gw, d_gb), late_parts = rwkv_bwd(
        *rwkv_in, kkr, kar, rkr, gwr, gbr, s0_all, tinv_all, by_seq(dycat), by_seq(gate), n_seq, seq_len,
        Hosted([late_grads[n] for n in LATE], False))
    prep_cots = [flat(t) for t in (dr, dkraw, dv, dlw, dasig)] + [dycat, z]
    dp, (d_mu, d_w0, d_wd, d_a0, d_wa, d_wg) = rwkv_prep_bwd(p, prep_edges, prep_pars, prep_cots, d_sb, seq_len)

    dw_in = matmul("dw_in", h1, dp, "tn", BF16, d_model, in_cols // 2, 2048)
    lora_parts = lambda t: _col_parts(t).astype(BF16)
    early_grads = {"w_in": _col_parts(dw_in), "w_decay_up": lora_parts(d_wd[:HEAD_DIM]),
                   "w_aaa_up": lora_parts(d_wa[HEAD_DIM:]), "w_gate_up": lora_parts(d_wg)}
    dx, d_ln1, *early_parts = matmul(
        "d_h1", dp, w_in_f, "nt", (F32,), 512, d_model, in_cols, extras=[x2d, dx2], pars=[g1],
        epilogue=lambda *t: norm_bwd(*t)[1:], sums=[((1, d_model), F32)],
        hosted=Hosted([early_grads[n] for n in EARLY], False))
    parts = dict(zip(EARLY, early_parts))
    parts.update(zip(LATE, late_parts))

    small_grads = {"ln1_g": d_ln1, "tok_mu": d_mu, "w0": d_w0, "a0": d_a0, "k_k": d_kk, "k_a": d_ka, "r_k": d_rk,
                   "gn_w": d_gw, "gn_b": d_gb, "sb_gain": d_sg, "ln2_g": d_ln2, "lnf_g": d_lnf}
    n_small = sum(int(weights[n].size) for n in SMALL)
    pack_rows = -(-(n_small + 1) // (8 * LANES)) * 8
    packed = _pack([small_grads[n] for n in SMALL] + [loss_acc[0, :1]], pack_rows)
    (small_parts,) = exchange("gather_small", [packed], True)

    results = {}
    for n in BIG:
        w2d = weights[n][0]
        tile = w2d.shape[0] if w2d.shape[0] <= 256 else 256
        results[n] = adamw("adamw_" + n, w2d, parts[n], mom_m[n][0], mom_v[n][0], tile)
    pk = lambda d: _pack([d[n] for n in SMALL] + [jnp.zeros((1,), F32)], pack_rows)
    sg, sd, sm, sv = adamw("adamw_small", pk(weights), small_parts, pk(mom_m), pk(mom_v), pack_rows)
    off = 0
    for n in SMALL:
        size = int(weights[n].size)
        results[n] = tuple(t.reshape(-1)[off:off + size] for t in (sg, sd, sm, sv))
        off += size
    loss = sg.reshape(-1)[off]

    out = [loss, dx.reshape(x.shape)]
    for kind in range(4):
        out += [results[n][kind].reshape(weights[n].shape) for n in ORDER]
    return tuple(out)
```

```python
import functools
import math

import jax
import jax.numpy as jnp
from jax import lax
from jax.experimental import pallas as pl
from jax.experimental.pallas import tpu as pltpu

F32 = jnp.float32
BF16 = jnp.bfloat16

N_DEV = 8
HEAD_DIM = 64
LANES = 128
RWKV_W = 512
SB_W = 512
LORA_WA = 128
GATE_LORA = 128
RWKV_COLS = 3 * RWKV_W + LORA_WA + GATE_LORA
RMS_EPS = 1e-5
GN_EPS = 64e-5
CHUNK = 64
QB = 256
SB_SCALE = HEAD_DIM ** -0.5
ADAM_LR, ADAM_B1, ADAM_B2, ADAM_EPS, ADAM_WD, ADAM_STEP = 0.001, 0.9, 0.999, 1e-08, 0.01, 10
VMEM_LIMIT = 56 * 1024 * 1024


_DIMS = {
    "nn": (((1,), (0,)), ((), ())),
    "nt": (((1,), (1,)), ((), ())),
    "tn": (((0,), (0,)), ((), ())),
}


def _pieces(x, n):
    if n == 1:
        return [x.astype(BF16)]
    out, rem = [], x.astype(F32)
    for i in range(n):
        p = rem.astype(BF16)
        out.append(p)
        if i + 1 < n:
            rem = rem - p.astype(F32)
    return out


def _dot(a, b, form, pa, pb):
    pieces_a, pieces_b = _pieces(a, pa), _pieces(b, pb)
    keep = max(pa, pb)
    acc = None
    for i, ai in enumerate(pieces_a):
        for j, bj in enumerate(pieces_b):
            if i + j >= keep:
                continue
            t = lax.dot_general(ai, bj, _DIMS[form], preferred_element_type=F32)
            acc = t if acc is None else acc + t
    return acc


BOTH = (True, True)


@functools.partial(jax.custom_vjp, nondiff_argnums=(2, 3, 4, 5, 6))
def mm(a, b, form, pa, pb, diff=BOTH, grad_pieces=None):
    return _dot(a, b, form, pa, pb)


def _mm_fwd(a, b, form, pa, pb, diff, grad_pieces):
    return _dot(a, b, form, pa, pb), (a, b)


def _mm_bwd(form, pa, pb, diff, grad_pieces, res, g):
    a, b = res
    pg = grad_pieces or max(pa, pb)
    da, db = jnp.zeros_like(a), jnp.zeros_like(b)
    if form == "nn":
        if diff[0]:
            da = mm(g, b, "nt", pg, pb)
        if diff[1]:
            db = mm(a, g, "tn", pa, pg)
    elif form == "nt":
        if diff[0]:
            da = mm(g, b, "nn", pg, pb)
        if diff[1]:
            db = mm(g, a, "tn", pg, pa)
    else:
        if diff[0]:
            da = mm(b, g, "nt", pb, pg)
        if diff[1]:
            db = mm(a, g, "nn", pa, pg)
    return da, db


mm.defvjp(_mm_fwd, _mm_bwd)


def _stack_rows(top, bottom):
    return jnp.concatenate([top, bottom], axis=0)


@jax.custom_vjp
def _split_rows(x):
    half = x.shape[0] // 2
    return x[:half], x[half:]


def _split_rows_fwd(x):
    return _split_rows(x), None


def _split_rows_bwd(_, grads):
    return (_stack_rows(*grads),)


_split_rows.defvjp(_split_rows_fwd, _split_rows_bwd)


def _lane_lo(shape):
    return lax.broadcasted_iota(jnp.int32, shape, len(shape) - 1) < HEAD_DIM


def _segsum(x):
    lo = _lane_lo(x.shape)
    s_lo = jnp.sum(jnp.where(lo, x, 0.0), axis=-1, keepdims=True)
    s_hi = jnp.sum(jnp.where(lo, 0.0, x), axis=-1, keepdims=True)
    return jnp.where(lo, s_lo, s_hi)


def _sigmoid(x):
    return 0.5 * (jnp.tanh(0.5 * x) + 1.0)


@jax.custom_vjp
def _log_one_minus_sigmoid(z):
    return -jnp.maximum(z, 0.0) - jnp.log(1.0 + jnp.exp(-jnp.abs(z)))


def _log_oms_fwd(z):
    out = _log_one_minus_sigmoid(z)
    return out, (z, out)


def _log_oms_bwd(res, g):
    z, out = res
    return (-g * jnp.exp(z + out),)


_log_one_minus_sigmoid.defvjp(_log_oms_fwd, _log_oms_bwd)


def f_norm(x, g):
    return x * lax.rsqrt(jnp.mean(x * x, axis=-1, keepdims=True) + RMS_EPS) * g


def f_prep(p, pprev, mu, w0, wd_pad, a0, wa_pad, wg):
    pm = p + mu * (pprev - p)
    r = pm[:, 0:RWKV_W]
    k = pm[:, RWKV_W:2 * RWKV_W]
    v = pm[:, 2 * RWKV_W:3 * RWKV_W]
    xwa = pm[:, 3 * RWKV_W:3 * RWKV_W + LORA_WA]
    xg = pm[:, 3 * RWKV_W + LORA_WA:RWKV_COLS]
    w = _log_one_minus_sigmoid(-(w0 + mm(jnp.tanh(xwa), wd_pad, "nn", 1, 1))) - 0.5
    lw = -jnp.exp(w)
    asig = _sigmoid(a0 + mm(xwa, wa_pad, "nn", 1, 1))
    gate = mm(_sigmoid(xg), wg, "nn", 1, 1)
    return r, k, v, lw, asig, gate


def _tri(n, kind):
    row = lax.broadcasted_iota(jnp.int32, (n, n), 0)
    col = lax.broadcasted_iota(jnp.int32, (n, n), 1)
    if kind == "lower_incl":
        return row >= col
    return row > col


@functools.partial(jax.custom_vjp, nondiff_argnums=(1,))
def _nilpotent_inverses(mats, hp):
    size = mats[0].shape[0]
    eye = (lax.broadcasted_iota(jnp.int32, (size, size), 0) == lax.broadcasted_iota(jnp.int32, (size, size), 1))
    tinv = [eye.astype(F32) + x for x in mats]
    pw = [mm(x, x, "nn", hp, hp) for x in mats]
    for _ in range(int(math.log2(size)) - 2):
        both = [_split_rows(mm(_stack_rows(t, x), x, "nn", hp, hp)) for t, x in zip(tinv, pw)]
        tinv = [t + tx for t, (tx, _) in zip(tinv, both)]
        pw = [xx for _, xx in both]
    return [t + mm(t, x, "nn", hp, hp) for t, x in zip(tinv, pw)]


def _nilpotent_inverses_fwd(mats, hp):
    tinv = _nilpotent_inverses(mats, hp)
    return tinv, tinv


def _nilpotent_inverses_bwd(hp, tinv, grads):
    right = [mm(g, t, "nt", hp, hp) for g, t in zip(grads, tinv)]
    return ([mm(t, x, "tn", hp, hp) for t, x in zip(tinv, right)],)


_nilpotent_inverses.defvjp(_nilpotent_inverses_fwd, _nilpotent_inverses_bwd)


@functools.partial(jax.custom_vjp, nondiff_argnums=(2,))
def _known_inverses(mats, tinv, hp):
    return list(tinv)


def _known_inverses_fwd(mats, tinv, hp):
    return list(tinv), tinv


def _known_inverses_bwd(hp, tinv, grads):
    return _nilpotent_inverses_bwd(hp, tinv, grads) + ([jnp.zeros_like(t) for t in tinv],)


_known_inverses.defvjp(_known_inverses_fwd, _known_inverses_bwd)


def rwkv_chunk(state, r, kraw, v, lw, asig, k_k, k_a, r_k, gn_w, gn_b, hp, saved_tinv=None, with_tinv=False):
    n = len(r)
    L = r[0].shape[0]
    lo = _lane_lo((1, LANES))
    masks = (lo, jnp.logical_not(lo))
    incl = _tri(L, "lower_incl")
    strict = _tri(L, "strict")
    tri = incl.astype(F32)
    kk = [x * w for x, w in zip(kraw, k_k)]
    kk = [x / jnp.maximum(jnp.sqrt(_segsum(x * x)), 1e-12) for x in kk]
    k = [x * (1.0 + (s - 1.0) * w) for x, s, w in zip(kraw, asig, k_a)]
    b = [x * s for x, s in zip(kk, asig)]
    c = [mm(tri, x, "nn", 1, 3, (False, True)) for x in lw]
    at = [-x * jnp.exp(ci - li) for x, ci, li in zip(kk, c, lw)]
    rt = [x * jnp.exp(ci) for x, ci in zip(r, c)]
    einv = [jnp.exp(-ci) for ci in c]
    bt = [x * e for x, e in zip(b, einv)]
    kt = [x * e for x, e in zip(k, einv)]
    inst = [(s, m) for s in range(n) for m in masks]
    ar_h = [_stack_rows(jnp.where(m, at[s], 0.0), jnp.where(m, rt[s], 0.0)) for s, m in inst]
    on_b = [_split_rows(mm(x, bt[s], "nt", hp, hp)) for x, (s, _) in zip(ar_h, inst)]
    on_k = [_split_rows(mm(x, kt[s], "nt", hp, hp)) for x, (s, _) in zip(ar_h, inst)]
    a_ab = [jnp.where(strict, x, 0.0) for x, _ in on_b]
    b_rb = [jnp.where(incl, x, 0.0) for _, x in on_b]
    a_ak = [jnp.where(strict, x, 0.0) for x, _ in on_k]
    b_rk = [jnp.where(incl, x, 0.0) for _, x in on_k]
    tinv = _nilpotent_inverses(a_ab, hp) if saved_tinv is None else _known_inverses(a_ab, saved_tinv, hp)
    on_state = [_split_rows(mm(x, state[s], "nt", hp, hp)) for x, (s, _) in zip(ar_h, inst)]
    on_v = [_split_rows(mm(_stack_rows(m1, m2), v[s], "nn", hp, hp)) for m1, m2, (s, _) in zip(a_ak, b_rk, inst)]
    u_h = [mm(t, sa + av, "nn", hp, hp) for t, (sa, _), (av, _) in zip(tinv, on_state, on_v)]
    y_h = [sr + mm(m1, u, "nn", hp, hp) + bv for (_, sr), m1, u, (_, bv) in zip(on_state, b_rb, u_h, on_v)]
    u_all = [jnp.where(lo, u_h[2 * s], u_h[2 * s + 1]) for s in range(n)]
    y_all = [jnp.where(lo, y_h[2 * s], y_h[2 * s + 1]) for s in range(n)]
    c_last = [jnp.sum(x, axis=0, keepdims=True) for x in lw]
    efwd = [jnp.exp(cl - ci) for cl, ci in zip(c_last, c)]
    new_state = [st * jnp.exp(cl) + mm(_stack_rows(u, vi), _stack_rows(bi * e, ki * e), "tn", hp, hp)
                 for st, cl, u, bi, e, vi, ki in zip(state, c_last, u_all, b, efwd, v, k)]
    row_head = lax.broadcasted_iota(jnp.int32, (LANES, LANES), 0) // HEAD_DIM
    col_head = lax.broadcasted_iota(jnp.int32, (LANES, LANES), 1) // HEAD_DIM
    new_state = [jnp.where(row_head == col_head, x, 0.0) for x in new_state]
    outs = []
    for y, ri, ki, vi, w_rk, w_gw, w_gb in zip(y_all, r, k, v, r_k, gn_w, gn_b):
        mean = _segsum(y) * (1.0 / HEAD_DIM)
        d = y - mean
        var = _segsum(d * d) * (1.0 / HEAD_DIM)
        yn = d * lax.rsqrt(var + GN_EPS) * w_gw + w_gb
        outs.append(yn + _segsum(ri * ki * w_rk) * vi)
    return (outs, new_state, tinv) if with_tinv else (outs, new_state)


def sb_tile(q, k, v, c_lo, c_hi, diag, from_here=None):
    n = len(q)
    lo = _lane_lo((1, LANES))
    below = _tri(QB, "strict")
    from_s = _tri(QB, "lower_incl").astype(F32)
    inst = [(s, h) for s in range(n) for h in (0, 1)]
    carry = [(c_lo[s], c_hi[s])[h] for s, h in inst]
    z = [mm(q[s][h], k[s], "nt", 1, 1) for s, h in inst]
    log_keep = [_log_one_minus_sigmoid(x) for x in z]
    if diag:
        log_keep = [jnp.where(below, x, 0.0) for x in log_keep]
    own = [jnp.sum(x, axis=1, keepdims=True) for x in log_keep]
    if from_here is not None:
        carry = [lax.stop_gradient(from_here[s][h] - o) + cr for (s, h), o, cr in zip(inst, own, carry)]
    tail = [mm(x, from_s, "nn", SB_SUM_PIECES, 1, (True, False), 1) for x in log_keep]
    log_a = [x + tl + cr for x, tl, cr in zip(z, tail, carry)]
    if diag:
        log_a = [jnp.where(below, x, -1e30) for x in log_a]
    att = [jnp.exp(x) for x in log_a]
    out_h = [mm(x, v[s], "nn", 1, 1) for x, (s, _) in zip(att, inst)]
    out = [jnp.where(lo, out_h[2 * s], out_h[2 * s + 1]) for s in range(n)]
    return out, [own[2 * s] for s in range(n)], [own[2 * s + 1] for s in range(n)]


def sb_split_q(q):
    lo = _lane_lo((1, LANES))
    qs = q * SB_SCALE
    return jnp.where(lo, qs, 0.0), jnp.where(lo, 0.0, qs)


def sb_post(o, gain):
    return o * lax.rsqrt(_segsum(o * o) * (1.0 / HEAD_DIM) + RMS_EPS) * gain


def f_final(x3, g, target):
    y = f_norm(x3, g)
    err = y - target
    return 0.5 * jnp.mean(err * err, axis=-1, keepdims=True)


def _params(sem):
    return pltpu.CompilerParams(dimension_semantics=sem, vmem_limit_bytes=VMEM_LIMIT)


def rw_call(name, body_fn, rows, pars, out_rows, tile, hosted=None):
    n_rows = rows[0].shape[0]
    tile = min(tile, n_rows)
    steps = n_rows // tile
    row_specs = [pl.BlockSpec((tile, arr.shape[1]), lambda i: (i, 0)) for arr in rows]
    par_specs = [pl.BlockSpec(p.shape, lambda i: (0, 0)) for p in pars]
    nr, npar, nor = len(rows), len(pars), len(out_rows)
    n_host = hosted.n if hosted is not None else 0

    def body(*refs):
        if hosted is not None:
            refs, copies = hosted.split(refs, nr + npar, nor)
            hosted.run(copies, pl.program_id(0) == 0, pl.program_id(0) == steps - 1)
        row_outs = body_fn([r[...] for r in refs[:nr]], [r[...] for r in refs[nr:nr + npar]])
        for ref, val in zip(refs[nr + npar:], row_outs):
            ref[...] = val.astype(ref.dtype)

    outs = pl.pallas_call(
        body, name=name, grid=(steps,), in_specs=row_specs + par_specs + [ANY_SPEC] * n_host,
        out_specs=[pl.BlockSpec((tile, c), lambda i: (i, 0)) for c, _ in out_rows] + [ANY_SPEC] * n_host,
        out_shape=[jax.ShapeDtypeStruct((n_rows, c), dt) for c, dt in out_rows]
        + (exchange_shapes(hosted.arrays, hosted.gather) if n_host else []),
        scratch_shapes=exchange_sems(n_host) if n_host else [],
        compiler_params=_params(("arbitrary",)),
    )(*rows, *pars, *(hosted.arrays if n_host else []))
    return outs


def matmul(name, a, b, form, out_dtype, tm, tn, tk, extras=(), pars=(), epilogue=None, sums=(), hosted=None):
    out_dtypes = out_dtype if isinstance(out_dtype, tuple) else (out_dtype,)
    tm, tn, tk = min(tm, a.shape[1 if form == "tn" else 0]), min(tn, b.shape[0 if form == "nt" else 1]), min(tk, a.shape[0 if form == "tn" else 1])
    if form == "nn":
        (m, kd), n = a.shape, b.shape[1]
        a_spec = pl.BlockSpec((tm, tk), lambda i, j, k: (i, k))
        b_spec = pl.BlockSpec((tk, tn), lambda i, j, k: (k, j))
    elif form == "nt":
        (m, kd), n = a.shape, b.shape[0]
        a_spec = pl.BlockSpec((tm, tk), lambda i, j, k: (i, k))
        b_spec = pl.BlockSpec((tn, tk), lambda i, j, k: (j, k))
    else:
        (kd, m), n = a.shape, b.shape[1]
        a_spec = pl.BlockSpec((tk, tm), lambda i, j, k: (k, i))
        b_spec = pl.BlockSpec((tk, tn), lambda i, j, k: (k, j))
    ksteps = kd // tk

    n_extra, n_par, n_out, n_sum = len(extras), len(pars), len(out_dtypes), len(sums)
    n_in = 2 + n_extra + n_par
    grid = (m // tm, n // tn, ksteps)

    def body(*refs):
        if hosted is not None:
            refs, copies = hosted.split(refs, n_in, n_out + n_sum)
            here = [pl.program_id(d) for d in range(3)]
            hosted.run(copies, functools.reduce(jnp.logical_and, [h == 0 for h in here]),
                       functools.reduce(jnp.logical_and, [h == g - 1 for h, g in zip(here, grid)]))
        a_ref, b_ref, rest = refs[0], refs[1], refs[2:]
        e_refs, o_refs = rest[:n_extra + n_par], rest[n_extra + n_par:n_extra + n_par + n_out]
        s_refs = rest[n_extra + n_par + n_out:n_extra + n_par + n_out + n_sum]
        kstep = pl.program_id(2)
        part = lax.dot_general(a_ref[...].astype(BF16), b_ref[...].astype(BF16), _DIMS[form],
                               preferred_element_type=F32)

        def finish(acc):
            outs = epilogue(acc, *[r[...] for r in e_refs]) if epilogue else (acc,)
            for ref, val in zip(o_refs, outs[:n_out]):
                ref[...] = val.astype(ref.dtype)
            if n_sum:
                first_tile = jnp.logical_and(pl.program_id(0) == 0, pl.program_id(1) == 0)

                @pl.when(first_tile)
                def _():
                    for ref, val in zip(s_refs, outs[n_out:]):
                        ref[...] = val.astype(ref.dtype)

                @pl.when(jnp.logical_not(first_tile))
                def _():
                    for ref, val in zip(s_refs, outs[n_out:]):
                        ref[...] = ref[...] + val.astype(ref.dtype)

        if ksteps == 1:
            finish(part)
            return
        acc_ref = rest[n_extra + n_par + n_out + n_sum]

        @pl.when(kstep == 0)
        def _():
            acc_ref[...] = part

        @pl.when(jnp.logical_and(kstep > 0, kstep < ksteps - 1))
        def _():
            acc_ref[...] = acc_ref[...] + part

        @pl.when(kstep == ksteps - 1)
        def _():
            finish(acc_ref[...] + part)

    out_spec = pl.BlockSpec((tm, tn), lambda i, j, k: (i, j))
    whole = lambda shape: pl.BlockSpec(shape, lambda i, j, k: (0,) * len(shape))
    n_host = hosted.n if hosted is not None else 0
    host_in = list(hosted.arrays) if hosted is not None else []
    host_out = exchange_shapes(hosted.arrays, hosted.gather) if hosted is not None else []
    outs = pl.pallas_call(
        body, name=name, grid=grid,
        in_specs=[a_spec, b_spec] + [out_spec] * n_extra + [whole(t.shape) for t in pars] + [ANY_SPEC] * n_host,
        out_specs=[out_spec] * n_out + [whole(s) for s, _ in sums] + [ANY_SPEC] * n_host,
        out_shape=[jax.ShapeDtypeStruct((m, n), dt) for dt in out_dtypes]
        + [jax.ShapeDtypeStruct(s, dt) for s, dt in sums] + host_out,
        scratch_shapes=([pltpu.VMEM((tm, tn), F32)] if ksteps > 1 else []) + (exchange_sems(n_host) if n_host else []),
        compiler_params=_params(("arbitrary",) * 3 if (sums or n_host) else ("parallel", "parallel", "arbitrary")),
    )(a, b, *extras, *pars, *host_in)
    return outs if (isinstance(out_dtype, tuple) or sums or n_host) else outs[0]


PREP_TILE = 512
PREP_TILE_BWD = 256
SUBLANES = 8


def _shift_in(rows, first):
    rolled = pltpu.roll(rows, 1, 0)
    row = lax.broadcasted_iota(jnp.int32, (SUBLANES, rows.shape[1]), 0)
    head = jnp.where(row == 0, first, rolled[0:SUBLANES])
    return jnp.concatenate([head, rolled[SUBLANES:]], axis=0), rolled


def rwkv_prep_fwd(p, pars, seq_len):
    n_tok = p.shape[0]
    tile = min(PREP_TILE, seq_len)
    tile_b = min(PREP_TILE_BWD, tile)
    steps, per_seq, sub = n_tok // tile, seq_len // tile, tile // tile_b
    n_par = len(pars)

    def body(p_ref, *rest):
        par_refs, out_refs, edge_ref, last8 = rest[:n_par], rest[n_par:n_par + 6], rest[n_par + 6], rest[n_par + 7]
        step = pl.program_id(0)

        @pl.when(step == 0)
        def _():
            last8[...] = jnp.zeros_like(last8)

        rows = p_ref[...]
        before = jnp.where(step % per_seq == 0, 0.0, pltpu.roll(last8[...], 1, 0))
        prev, rolled = _shift_in(rows, before)
        edge_ref[0] = prev[0:SUBLANES]
        for m in range(1, sub):
            edge_ref[m] = rolled[m * tile_b:m * tile_b + SUBLANES]
        last8[...] = rows[tile - SUBLANES:tile]
        for ref, val in zip(out_refs, f_prep(rows, prev, *[r[...] for r in par_refs])):
            ref[...] = val

    row_out = pl.BlockSpec((tile, RWKV_W), lambda i: (i, 0))
    outs = pl.pallas_call(
        body, name="rwkv_prep", grid=(steps,),
        in_specs=[pl.BlockSpec((tile, RWKV_COLS), lambda i: (i, 0))] + [pl.BlockSpec(t.shape, lambda i: (0, 0)) for t in pars],
        out_specs=[row_out] * 6 + [pl.BlockSpec((sub, SUBLANES, RWKV_COLS), lambda i: (i, 0, 0))],
        out_shape=[jax.ShapeDtypeStruct((n_tok, RWKV_W), F32)] * 6
        + [jax.ShapeDtypeStruct((steps * sub, SUBLANES, RWKV_COLS), F32)],
        scratch_shapes=[pltpu.VMEM((SUBLANES, RWKV_COLS), F32)],
        compiler_params=_params(("arbitrary",)),
    )(p, *pars)
    return outs[:6], outs[6]


def rwkv_prep_bwd(p, edges, pars, cots, d_sb, seq_len):
    n_tok = p.shape[0]
    tile = min(PREP_TILE_BWD, seq_len)
    steps, per_seq = n_tok // tile, seq_len // tile
    n_par = len(pars)
    back = lambda i: steps - 1 - i

    def body(p_ref, edge_ref, *rest):
        par_refs, rest = rest[:n_par], rest[n_par:]
        cot_refs, dy_ref, z_ref, sb_refs = rest[:5], rest[5], rest[6], rest[7:10]
        dp_ref, acc_refs, next8 = rest[10], rest[11:11 + n_par], rest[11 + n_par]
        step = pl.program_id(0)
        first = step == 0

        @pl.when(first)
        def _():
            next8[...] = jnp.zeros_like(next8)

        rows = p_ref[...]
        prev, _ = _shift_in(rows, edge_ref[0])
        _, vjp = jax.vjp(f_prep, rows, prev, *[r[...].astype(F32) for r in par_refs])
        grads = vjp(tuple(r[...] for r in cot_refs) + (dy_ref[...] * z_ref[...],))
        d_rows, d_prev = grads[0], grads[1]
        up = pltpu.roll(d_prev, tile - 1, 0)
        ends_seq = back(step) % per_seq == per_seq - 1
        after = jnp.where(ends_seq, 0.0, pltpu.roll(next8[...], SUBLANES - 1, 0))
        row = lax.broadcasted_iota(jnp.int32, (SUBLANES, RWKV_COLS), 0)
        tail = jnp.where(row == SUBLANES - 1, after, up[tile - SUBLANES:tile])
        d_rows = d_rows + jnp.concatenate([up[:tile - SUBLANES], tail], axis=0)
        next8[...] = d_prev[0:SUBLANES]
        dp_ref[...] = jnp.concatenate([d_rows] + [r[...] for r in sb_refs], axis=1).astype(dp_ref.dtype)

        @pl.when(first)
        def _():
            for ref, val in zip(acc_refs, grads[2:]):
                ref[...] = val

        @pl.when(jnp.logical_not(first))
        def _():
            for ref, val in zip(acc_refs, grads[2:]):
                ref[...] = ref[...] + val

    cols = RWKV_COLS + sum(t.shape[1] for t in d_sb)
    half = pl.BlockSpec((tile, RWKV_W), lambda i: (back(i), 0))
    par_specs = [pl.BlockSpec(t.shape, lambda i: (0, 0)) for t in pars]
    outs = pl.pallas_call(
        body, name="d_rwkv_prep", grid=(steps,),
        in_specs=[pl.BlockSpec((tile, RWKV_COLS), lambda i: (back(i), 0)),
                  pl.BlockSpec((1, SUBLANES, RWKV_COLS), lambda i: (back(i), 0, 0))] + par_specs + [half] * 10,
        out_specs=[pl.BlockSpec((tile, cols), lambda i: (back(i), 0))] + par_specs,
        out_shape=[jax.ShapeDtypeStruct((n_tok, cols), BF16)] + [jax.ShapeDtypeStruct(t.shape, F32) for t in pars],
        scratch_shapes=[pltpu.VMEM((SUBLANES, RWKV_COLS), F32)],
        compiler_params=_params(("arbitrary",)),
    )(p, edges, *pars, *cots, *d_sb)
    return outs[0], outs[1:]


RWKV_HP = 1


RWKV_PAIRS = 4


def _rwkv_specs(n_seq, chunk_of):
    width = RWKV_PAIRS * LANES
    row = pl.BlockSpec((n_seq, CHUNK, width), lambda g, c: (0, chunk_of(c), g))
    par = pl.BlockSpec((1, width), lambda g, c: (0, g))
    s0 = pl.BlockSpec((1, 1, RWKV_PAIRS * n_seq, LANES, LANES), lambda g, c: (g, chunk_of(c), 0, 0, 0))
    tinv = pl.BlockSpec((1, 1, 2 * RWKV_PAIRS * n_seq, CHUNK, CHUNK), lambda g, c: (g, chunk_of(c), 0, 0, 0))
    return row, par, s0, tinv


class Hosted:
    def __init__(self, arrays, gather):
        self.arrays, self.gather, self.n = list(arrays), gather, len(arrays)

    def split(self, refs, n_in, n_out):
        n = self.n
        ins, outs, sems = refs[n_in:n_in + n], refs[n_in + n + n_out:n_in + 2 * n + n_out], refs[-3:]
        own = refs[:n_in] + refs[n_in + n:n_in + n + n_out] + refs[n_in + 2 * n + n_out:-3]
        return own, exchange_copies(ins, outs, *sems, self.gather)

    def run(self, copies, first, last):
        @pl.when(first)
        def _():
            for cp in copies:
                cp.start()

        @pl.when(last)
        def _():
            for cp in copies:
                cp.wait()


class HostedChipGather(Hosted):
    def __init__(self, arrays):
        super().__init__(arrays, True)

    def split(self, refs, n_in, n_out):
        n = self.n
        ins, outs, sems = refs[n_in:n_in + n], refs[n_in + n + n_out:n_in + 2 * n + n_out], refs[-3:]
        own = refs[:n_in] + refs[n_in + n:n_in + n + n_out] + refs[n_in + 2 * n + n_out:-3]
        return own, (ins, outs, sems)

    def run(self, state, first, last):
        ins, outs, (send_sems, recv_sems, local_sems) = state
        x, y, c = lax.axis_index("x"), lax.axis_index("y"), lax.axis_index("c")
        chips = [((x + 1) % 2, y), (x, (y + 1) % 2), ((x + 1) % 2, (y + 1) % 2)]

        def block(a, k, of, to, src=None):
            dst = outs[a].at[4 * of[0] + 2 * of[1] + of[2]]
            return pltpu.make_async_remote_copy(
                src_ref=dst if src is None else src, dst_ref=dst, send_sem=send_sems.at[a, k],
                recv_sem=recv_sems.at[a, k], device_id=to, device_id_type=pl.DeviceIdType.MESH)

        me, sibling = (x, y, c), (x, y, 1 - c)
        local = [pltpu.make_async_copy(ins[a], outs[a].at[4 * x + 2 * y + c], local_sems.at[a]) for a in range(self.n)]
        mine = [block(a, 0, me, sibling, ins[a]) for a in range(self.n)]
        mine += [block(a, 1 + j, me, (*chip, c), ins[a]) for a in range(self.n) for j, chip in enumerate(chips)]
        passed = [block(a, 4 + j, (*chip, c), sibling) for a in range(self.n) for j, chip in enumerate(chips)]

        @pl.when(first)
        def _():
            for cp in local + mine:
                cp.start()

        @pl.when(last)
        def _():
            for a in range(self.n):
                for j, chip in enumerate(chips):
                    block(a, 1 + j, (*chip, c), me).wait_recv()
                    passed[a * len(chips) + j].start()
            for a in range(self.n):
                block(a, 0, sibling, me).wait_recv()
                for j, chip in enumerate(chips):
                    block(a, 4 + j, (*chip, 1 - c), me).wait_recv()
            for cp in mine + passed:
                cp.wait_send()
            for cp in local:
                cp.wait()


def rwkv_fwd(r, kraw, v, lw, asig, k_k, k_a, r_k, gn_w, gn_b, n_seq, seq_len, hosted):
    n_chunks = seq_len // CHUNK
    n_groups = RWKV_W // (RWKV_PAIRS * LANES)
    n_inst = RWKV_PAIRS * n_seq
    row, par, s0_spec, tinv_spec = _rwkv_specs(n_seq, lambda c: c)
    inst = [(s, pl.ds(pp * LANES, LANES)) for pp in range(RWKV_PAIRS) for s in range(n_seq)]

    def body(*refs):
        own, copies = hosted.split(refs, 10, 3)
        row_refs, par_refs, (z_ref, s0_ref, tinv_ref, state) = own[:5], own[5:10], own[10:]
        step = pl.program_id(0) * n_chunks + pl.program_id(1)
        hosted.run(copies, step == 0, step == n_groups * n_chunks - 1)

        @pl.when(pl.program_id(1) == 0)
        def _():
            state[...] = jnp.zeros_like(state)

        s0 = [state[i] for i in range(n_inst)]
        rows = [[ref[s, :, lanes] for s, lanes in inst] for ref in row_refs]
        pars = [[ref[:, lanes] for _, lanes in inst] for ref in par_refs]
        z, s1, tinv = rwkv_chunk(s0, *rows, *pars, RWKV_HP, with_tinv=True)
        for i, (s, lanes) in enumerate(inst):
            s0_ref[0, 0, i] = s0[i]
            z_ref[s, :, lanes] = z[i]
            state[i] = s1[i]
        for i, t in enumerate(tinv):
            tinv_ref[0, 0, i] = t

    outs = pl.pallas_call(
        body, name="rwkv_fwd", grid=(n_groups, n_chunks),
        in_specs=[row] * 5 + [par] * 5 + [ANY_SPEC] * hosted.n,
        out_specs=[row, s0_spec, tinv_spec] + [ANY_SPEC] * hosted.n,
        out_shape=[jax.ShapeDtypeStruct(r.shape, F32),
                   jax.ShapeDtypeStruct((n_groups, n_chunks, n_inst, LANES, LANES), F32),
                   jax.ShapeDtypeStruct((n_groups, n_chunks, 2 * n_inst, CHUNK, CHUNK), F32)]
        + exchange_shapes(hosted.arrays, hosted.gather),
        scratch_shapes=[pltpu.VMEM((n_inst, LANES, LANES), F32)] + exchange_sems(hosted.n),
        compiler_params=_params(("arbitrary", "arbitrary")),
    )(r, kraw, v, lw, asig, k_k, k_a, r_k, gn_w, gn_b, *hosted.arrays)
    return outs[0], outs[1], outs[2], outs[3:]


def rwkv_bwd(r, kraw, v, lw, asig, k_k, k_a, r_k, gn_w, gn_b, s0_all, tinv_all, dy, gate, n_seq, seq_len, hosted):
    n_chunks = seq_len // CHUNK
    n_groups = RWKV_W // (RWKV_PAIRS * LANES)
    n_inst = RWKV_PAIRS * n_seq
    row, par, s0_spec, tinv_spec = _rwkv_specs(n_seq, lambda c: n_chunks - 1 - c)
    inst = [(s, pl.ds(pp * LANES, LANES)) for pp in range(RWKV_PAIRS) for s in range(n_seq)]

    def body(*refs):
        own, copies = hosted.split(refs, 14, 10)
        row_refs, par_refs, (s0_ref, tinv_ref, dy_ref, gate_ref) = own[:5], own[5:10], own[10:14]
        drow_refs, dpar_refs, dstate = own[14:19], own[19:24], own[24]
        step = pl.program_id(0) * n_chunks + pl.program_id(1)
        hosted.run(copies, step == 0, step == n_groups * n_chunks - 1)
        first = pl.program_id(1) == 0

        @pl.when(first)
        def _():
            dstate[...] = jnp.zeros_like(dstate)

        fn = functools.partial(rwkv_chunk, hp=RWKV_HP, saved_tinv=[tinv_ref[0, 0, i] for i in range(2 * n_inst)])
        rows = [[ref[s, :, lanes] for s, lanes in inst] for ref in row_refs]
        pars = [[ref[:, lanes] for _, lanes in inst] for ref in par_refs]
        _, vjp = jax.vjp(fn, [s0_ref[0, 0, i] for i in range(n_inst)], *rows, *pars)
        dz = [dy_ref[s, :, lanes] * gate_ref[s, :, lanes] for s, lanes in inst]
        grads = vjp((dz, [dstate[i] for i in range(n_inst)]))
        for i, (s, lanes) in enumerate(inst):
            dstate[i] = grads[0][i]
            for ref, val in zip(drow_refs, grads[1:6]):
                ref[s, :, lanes] = val[i]

        def accumulate(start):
            for ref, val in zip(dpar_refs, grads[6:]):
                for pp in range(RWKV_PAIRS):
                    lanes = pl.ds(pp * LANES, LANES)
                    total = functools.reduce(jnp.add, val[pp * n_seq:(pp + 1) * n_seq])
                    ref[:, lanes] = total if start else ref[:, lanes] + total

        @pl.when(first)
        def _():
            accumulate(True)

        @pl.when(jnp.logical_not(first))
        def _():
            accumulate(False)

    rows_shape = jax.ShapeDtypeStruct(r.shape, F32)
    par_shape = jax.ShapeDtypeStruct((1, RWKV_W), F32)
    outs = pl.pallas_call(
        body, name="rwkv_bwd", grid=(n_groups, n_chunks),
        in_specs=[row] * 5 + [par] * 5 + [s0_spec, tinv_spec, row, row] + [ANY_SPEC] * hosted.n,
        out_specs=[row] * 5 + [par] * 5 + [ANY_SPEC] * hosted.n,
        out_shape=[rows_shape] * 5 + [par_shape] * 5 + exchange_shapes(hosted.arrays, hosted.gather),
        scratch_shapes=[pltpu.VMEM((n_inst, LANES, LANES), F32)] + exchange_sems(hosted.n),
        compiler_params=_params(("arbitrary", "arbitrary")),
    )(r, kraw, v, lw, asig, k_k, k_a, r_k, gn_w, gn_b, s0_all, tinv_all, dy, gate, *hosted.arrays)
    return outs[:10], outs[10:]


SB_Q0 = RWKV_COLS // LANES
SB_K0 = SB_Q0 + SB_W // LANES
SB_V0 = SB_K0 + SB_W // LANES
SB_SEQS = 2
SB_BUFFERS = pl.Buffered(1)
SB_SUM_PIECES = 2
SB_DEAD = -110.0


def _col_of(c_lo, c_hi):
    return jnp.where(_lane_lo((1, LANES)), c_lo, c_hi)


def sb_fwd(p, gain, n_seq, seq_len, hosted):
    n_pairs = SB_W // LANES
    n_q = seq_len // QB
    nb = min(SB_SEQS, n_seq)

    def seq_spec(c0):
        return pl.BlockSpec((nb, seq_len, LANES), functools.partial(lambda b, h, c0: (b, 0, c0 + h), c0=c0),
                            pipeline_mode=SB_BUFFERS)

    out_spec = pl.BlockSpec((nb, seq_len, LANES), lambda b, h: (b, 0, h), pipeline_mode=SB_BUFFERS)

    def body(*refs):
        own, copies = hosted.split(refs, 4, 4)
        q_ref, k_ref, v_ref, g_ref, y_ref, o_ref, tot_ref, first_ref = own
        step = pl.program_id(0) * n_pairs + pl.program_id(1)
        hosted.run(copies, step == 0, step == (n_seq // nb) * n_pairs - 1)
        gain = g_ref[...]

        def q_block(i, _):
            qs = pl.multiple_of(i * QB, QB)
            seqs = range(nb)
            zeros = [jnp.zeros((QB, 1), F32)] * nb
            qv = [sb_split_q(q_ref[s, pl.ds(qs, QB), :]) for s in seqs]
            add = lambda xs, ys: [x + y for x, y in zip(xs, ys)]

            def tiles(ks, c_lo, c_hi, diag):
                return sb_tile(qv, [k_ref[s, pl.ds(ks, QB), :] for s in seqs],
                               [v_ref[s, pl.ds(ks, QB), :] for s in seqs], c_lo, c_hi, diag)

            def alive(c_lo, c_hi):
                top = functools.reduce(jnp.maximum, list(c_lo) + list(c_hi))
                return jnp.max(top) > SB_DEAD

            def k_block(state):
                j, _, (o, c_lo, c_hi) = state
                o2, s_lo, s_hi = tiles(pl.multiple_of(j * QB, QB), c_lo, c_hi, False)
                c_lo, c_hi = add(c_lo, s_lo), add(c_hi, s_hi)
                return j - 1, alive(c_lo, c_hi), (add(o, o2), c_lo, c_hi)

            o, c_lo, c_hi = tiles(qs, zeros, zeros, True)
            j, _, (o, c_lo, c_hi) = lax.while_loop(lambda st: jnp.logical_and(st[0] >= 0, st[1]), k_block,
                                                   (i - 1, alive(c_lo, c_hi), (o, c_lo, c_hi)))
            first_ref[pl.program_id(0), pl.program_id(1), i] = j + 1
            for s in seqs:
                o_ref[s, pl.ds(qs, QB), :] = o[s]
                tot_ref[s, pl.ds(qs, QB), :] = jnp.broadcast_to(_col_of(c_lo[s], c_hi[s]), (QB, LANES))
                y_ref[s, pl.ds(qs, QB), :] = sb_post(o[s], gain)
            return 0

        lax.fori_loop(0, n_q, q_block, 0)

    shape = jax.ShapeDtypeStruct((n_seq, seq_len, SB_W), F32)
    return pl.pallas_call(
        body, name="sb_fwd", grid=(n_seq // nb, n_pairs),
        in_specs=[seq_spec(SB_Q0), seq_spec(SB_K0), seq_spec(SB_V0), pl.BlockSpec((1, LANES), lambda b, h: (0, h))]
        + [ANY_SPEC] * hosted.n,
        out_specs=[out_spec] * 3 + [pl.BlockSpec(memory_space=pltpu.SMEM)] + [ANY_SPEC] * hosted.n,
        out_shape=[shape] * 3 + [jax.ShapeDtypeStruct((n_seq // nb, n_pairs, n_q), jnp.int32)]
        + exchange_shapes(hosted.arrays, hosted.gather),
        scratch_shapes=exchange_sems(hosted.n),
        compiler_params=_params(("arbitrary", "arbitrary")),
    )(p, p, p, gain, *hosted.arrays)


def sb_bwd(p, gain, o_raw, tot, dy, first, n_seq, seq_len):
    n_pairs = SB_W // LANES
    n_q = seq_len // QB
    nb = min(SB_SEQS, n_seq)

    def seq_spec(c0):
        return pl.BlockSpec((nb, seq_len, LANES), functools.partial(lambda h, b, c0: (b, 0, c0 + h), c0=c0),
                            pipeline_mode=SB_BUFFERS)

    own = pl.BlockSpec((nb, seq_len, LANES), lambda h, b: (b, 0, h), pipeline_mode=SB_BUFFERS)
    par = pl.BlockSpec((1, LANES), lambda h, b: (0, h))

    def body(q_ref, k_ref, v_ref, g_ref, o_ref, tot_ref, dy_ref, first_ref, dq_ref, dk_ref, dv_ref, dg_ref):
        gain = g_ref[...]
        lo = _lane_lo((1, LANES))
        dk_ref[...] = jnp.zeros_like(dk_ref)
        dv_ref[...] = jnp.zeros_like(dv_ref)

        def q_block(i, dgain):
            qs = pl.multiple_of(i * QB, QB)
            seqs = range(nb)
            zeros = [jnp.zeros((QB, 1), F32)] * nb
            qv, dov, t_lo, t_hi = [], [], [], []
            for s in seqs:
                qv.append(sb_split_q(q_ref[s, pl.ds(qs, QB), :]))
                _, post_vjp = jax.vjp(sb_post, o_ref[s, pl.ds(qs, QB), :], gain)
                do, dg_s = post_vjp(dy_ref[s, pl.ds(qs, QB), :])
                dov.append(do)
                dgain = dgain + dg_s
                tot_s = tot_ref[s, pl.ds(qs, QB), :]
                t_lo.append(jnp.max(jnp.where(lo, tot_s, -jnp.inf), axis=1, keepdims=True))
                t_hi.append(jnp.max(jnp.where(lo, -jnp.inf, tot_s), axis=1, keepdims=True))
            add = lambda xs, ys: [x + y for x, y in zip(xs, ys)]
            sub = lambda xs, ys: [x - y for x, y in zip(xs, ys)]

            def tile(ks, carry, diag):
                dq, rem_lo, rem_hi, g_lo, g_hi = carry
                kv = [k_ref[s, pl.ds(ks, QB), :] for s in seqs]
                vv = [v_ref[s, pl.ds(ks, QB), :] for s in seqs]
                fn = functools.partial(sb_tile, diag=diag, from_here=list(zip(rem_lo, rem_hi)))
                (_, s_lo, s_hi), vjp = jax.vjp(fn, qv, kv, vv, zeros, zeros)
                dq_t, dk_t, dv_t, dc_lo, dc_hi = vjp((dov, g_lo, g_hi))
                dq_t = [jnp.where(lo, d_lo, d_hi) for d_lo, d_hi in dq_t]
                for s in seqs:
                    dk_ref[s, pl.ds(ks, QB), :] = dk_ref[s, pl.ds(ks, QB), :] + dk_t[s]
                    dv_ref[s, pl.ds(ks, QB), :] = dv_ref[s, pl.ds(ks, QB), :] + dv_t[s]
                return add(dq, dq_t), sub(rem_lo, s_lo), sub(rem_hi, s_hi), add(g_lo, dc_lo), add(g_hi, dc_hi)

            def k_block(j, carry):
                return tile(pl.multiple_of(j * QB, QB), carry, False)

            carry = ([jnp.zeros((QB, LANES), F32)] * nb, t_lo, t_hi, zeros, zeros)
            carry = lax.fori_loop(first_ref[pl.program_id(1), pl.program_id(0), i], i, k_block, carry)
            carry = tile(qs, carry, True)
            for s in seqs:
                dq_ref[s, pl.ds(qs, QB), :] = carry[0][s] * SB_SCALE
            return dgain

        dgain = lax.fori_loop(0, n_q, q_block, jnp.zeros((1, LANES), F32))
        first = pl.program_id(1) == 0

        @pl.when(first)
        def _():
            dg_ref[...] = dgain

        @pl.when(jnp.logical_not(first))
        def _():
            dg_ref[...] = dg_ref[...] + dgain

    shape = jax.ShapeDtypeStruct((n_seq, seq_len, SB_W), F32)
    return pl.pallas_call(
        body, name="sb_bwd", grid=(n_pairs, n_seq // nb),
        in_specs=[seq_spec(SB_Q0), seq_spec(SB_K0), seq_spec(SB_V0), par, own, own, seq_spec(RWKV_W // LANES),
                  pl.BlockSpec(memory_space=pltpu.SMEM)],
        out_specs=[own, own, own, par],
        out_shape=[shape, shape, shape, jax.ShapeDtypeStruct((1, SB_W), F32)],
        compiler_params=_params(("arbitrary", "arbitrary")),
    )(p, p, p, gain, o_raw, tot, dy, first)


def exchange(name, arrays, gather):
    n = len(arrays)

    def body(*refs):
        copies = exchange_copies(refs[:n], refs[n:2 * n], *refs[2 * n:], gather)
        for cp in copies:
            cp.start()
        for cp in copies:
            cp.wait()

    return pl.pallas_call(
        body, name=name, in_specs=[ANY_SPEC] * n, out_specs=[ANY_SPEC] * n, out_shape=exchange_shapes(arrays, gather),
        scratch_shapes=exchange_sems(n),
    )(*arrays)


ANY_SPEC = pl.BlockSpec(memory_space=pl.ANY)


def exchange_shapes(arrays, gather):
    return [jax.ShapeDtypeStruct(((N_DEV,) + a.shape) if gather else a.shape, a.dtype) for a in arrays]


def exchange_sems(n):
    return [pltpu.SemaphoreType.DMA((n, N_DEV - 1)), pltpu.SemaphoreType.DMA((n, N_DEV - 1)),
            pltpu.SemaphoreType.DMA((n,))]


def exchange_copies(ins, outs, send_sems, recv_sems, local_sems, gather):
    x, y, c = lax.axis_index("x"), lax.axis_index("y"), lax.axis_index("c")
    me = 4 * x + 2 * y + c
    copies = []
    for a, (src_all, dst_all) in enumerate(zip(ins, outs)):
        own = src_all if gather else src_all.at[me]
        copies.append(pltpu.make_async_copy(own, dst_all.at[me], local_sems.at[a]))
        for j in range(1, N_DEV):
            px, py, pc = (x + (j >> 2)) % 2, (y + ((j >> 1) & 1)) % 2, (c + (j & 1)) % 2
            src = src_all if gather else src_all.at[4 * px + 2 * py + pc]
            copies.append(pltpu.make_async_remote_copy(
                src_ref=src, dst_ref=dst_all.at[me], send_sem=send_sems.at[a, j - 1],
                recv_sem=recv_sems.at[a, j - 1], device_id=(px, py, pc), device_id_type=pl.DeviceIdType.MESH))
    return copies


def adamw(name, w, parts, m, v, tile):
    rows, cols = w.shape
    spec = pl.BlockSpec((tile, cols), lambda i: (i, 0))
    part_spec = pl.BlockSpec((N_DEV, tile, cols), lambda i: (0, i, 0))

    def body(w_ref, p_ref, m_ref, v_ref, g_ref, d_ref, nm_ref, nv_ref):
        g = p_ref[0].astype(F32)
        for s in range(1, N_DEV):
            g = g + p_ref[s].astype(F32)
        new_m = ADAM_B1 * m_ref[...] + (1.0 - ADAM_B1) * g
        new_v = ADAM_B2 * v_ref[...] + (1.0 - ADAM_B2) * (g * g)
        m_hat = new_m / (1.0 - ADAM_B1 ** ADAM_STEP)
        v_hat = new_v / (1.0 - ADAM_B2 ** ADAM_STEP)
        g_ref[...] = g
        d_ref[...] = -ADAM_LR * (m_hat / (jnp.sqrt(v_hat) + ADAM_EPS) + ADAM_WD * w_ref[...])
        nm_ref[...] = new_m
        nv_ref[...] = new_v

    shape = jax.ShapeDtypeStruct((rows, cols), F32)
    return pl.pallas_call(
        body, name=name, grid=(rows // tile,), in_specs=[spec, part_spec, spec, spec],
        out_specs=[spec] * 4, out_shape=[shape] * 4, compiler_params=_params(("arbitrary",)),
    )(w, parts, m, v)


SMALL = ("ln1_g", "tok_mu", "w0", "a0", "k_k", "k_a", "r_k", "gn_w", "gn_b", "sb_gain", "ln2_g", "lnf_g")
EARLY = ("w_in", "w_decay_up", "w_aaa_up", "w_gate_up")
LATE = ("w_out", "w_up", "w_down")
BIG = EARLY + LATE
ORDER = ("ln1_g", "w_in", "tok_mu", "w0", "w_decay_up", "a0", "w_aaa_up", "w_gate_up", "k_k", "k_a", "r_k",
         "gn_w", "gn_b", "sb_gain", "w_out", "ln2_g", "w_up", "w_down", "lnf_g")


def _pack(vectors, rows):
    flat = jnp.concatenate([v.reshape(-1).astype(F32) for v in vectors])
    return jnp.pad(flat, (0, rows * LANES - flat.shape[0])).reshape(rows, LANES)


def _full_cols(gathered):
    d, k, cols = gathered.shape
    return jnp.transpose(gathered, (1, 0, 2)).reshape(k, d * cols)


def _col_parts(full):
    k, n = full.shape
    return jnp.transpose(full.reshape(k, N_DEV, n // N_DEV), (1, 0, 2))


def kernel(x, ln1_g, w_in, tok_mu, w0, w_decay_up, a0, w_aaa_up, w_gate_up, k_k, k_a, r_k, gn_w, gn_b, sb_gain, w_out, ln2_g, w_up, w_down, lnf_g, loss_target, m_ln1_g, m_w_in, m_tok_mu, m_w0, m_w_decay_up, m_a0, m_w_aaa_up, m_w_gate_up, m_k_k, m_k_a, m_r_k, m_gn_w, m_gn_b, m_sb_gain, m_w_out, m_ln2_g, m_w_up, m_w_down, m_lnf_g, v_ln1_g, v_w_in, v_tok_mu, v_w0, v_w_decay_up, v_a0, v_w_aaa_up, v_w_gate_up, v_k_k, v_k_a, v_r_k, v_gn_w, v_gn_b, v_sb_gain, v_w_out, v_ln2_g, v_w_up, v_w_down, v_lnf_g):
    args = dict(locals())
    weights = {n: args[n] for n in ORDER}
    mom_m = {n: args["m_" + n] for n in ORDER}
    mom_v = {n: args["v_" + n] for n in ORDER}

    n_seq, seq_len, d_model = x.shape
    n_tok = n_seq * seq_len
    x2d = x.reshape(n_tok, d_model)
    tgt = loss_target.reshape(n_tok, d_model)
    row = lambda t: t.reshape(1, -1).astype(F32)

    g1 = row(ln1_g)
    shard = {n: weights[n][0].astype(BF16) for n in BIG}
    h1, *early = rw_call("norm1", lambda r, q: [f_norm(r[0], q[0])], [x2d], [g1], [(d_model, BF16)], 512,
                         hosted=HostedChipGather([shard[n] for n in EARLY]))
    gathered = dict(zip(EARLY, early))
    w_in_f = _full_cols(gathered["w_in"])
    zeros64 = jnp.zeros((HEAD_DIM, RWKV_W), BF16)
    wd_pad = jnp.concatenate([_full_cols(gathered["w_decay_up"]), zeros64], axis=0)
    wa_pad = jnp.concatenate([zeros64, _full_cols(gathered["w_aaa_up"])], axis=0)
    wg_f = _full_cols(gathered["w_gate_up"])
    in_cols = w_in_f.shape[1]

    mu, w0r, a0r = row(tok_mu), row(w0), row(a0)
    kkr, kar, rkr, gwr, gbr, sgr = row(k_k), row(k_a), row(r_k), row(gn_w), row(gn_b), row(sb_gain)
    g2, gf = row(ln2_g), row(lnf_g)

    p = matmul("proj_in", h1, w_in_f, "nn", F32, 2048, in_cols // 2, d_model)
    prep_pars = [mu, w0r, wd_pad, a0r, wa_pad, wg_f]
    (r_, kraw, v_, lw, asig, gate), prep_edges = rwkv_prep_fwd(p, prep_pars, seq_len)
    by_seq = lambda t: t.reshape(n_seq, seq_len, t.shape[-1])
    flat = lambda t: t.reshape(n_tok, t.shape[-1])
    rwkv_in = [by_seq(t) for t in (r_, kraw, v_, lw, asig)]
    z, s0_all, tinv_all, (w_out_g, w_up_g) = rwkv_fwd(
        *rwkv_in, kkr, kar, rkr, gwr, gbr, n_seq, seq_len, Hosted([shard["w_out"], shard["w_up"]], True))
    y_sb, o_raw, tot, sb_first, w_down_g = sb_fwd(by_seq(p), sgr, n_seq, seq_len, Hosted([shard["w_down"]], True))
    w_out_f = w_out_g.reshape(d_model, d_model)
    w_up_f = _full_cols(w_up_g)
    w_down_f = w_down_g.reshape(-1, d_model)
    d_ff = w_up_f.shape[1]
    z, y_sb = flat(z), flat(y_sb)
    (ycat,) = rw_call("mix_cat", lambda r, q: [jnp.concatenate([r[0] * r[1], r[2]], axis=1)],
                      [z, gate, y_sb], [], [(d_model, BF16)], 512)
    x2, h2 = matmul("proj_out", ycat, w_out_f, "nn", (F32, BF16), 1024, d_model, d_model, extras=[x2d], pars=[g2],
                    epilogue=lambda acc, xv, g: (xv + acc, f_norm(xv + acc, g)))
    u, act = matmul("mlp_up", h2, w_up_f, "nn", (F32, BF16), 1024, d_ff // 2, d_model,
                    epilogue=lambda acc: (acc, jnp.square(jnp.maximum(acc, 0.0))))

    def loss_epilogue(acc, x2v, target, g):
        loss_rows, vjp = jax.vjp(lambda xv, gv: f_final(xv, gv, target), x2v + acc, g)
        dx3, dgf = vjp(jnp.ones_like(loss_rows))
        return dx3, dx3, jnp.broadcast_to(jnp.sum(loss_rows), (1, LANES)), dgf

    dx3, dx3_b, loss_acc, d_lnf = matmul(
        "mlp_down", act, w_down_f, "nn", (F32, BF16), 512, d_model, d_ff, extras=[x2, tgt], pars=[gf],
        epilogue=loss_epilogue, sums=[((1, LANES), F32), ((1, d_model), F32)])

    du = matmul("d_act", dx3_b, w_down_f, "nt", BF16, 1024, d_ff // 4, d_model, extras=[u],
                epilogue=lambda acc, uv: (acc * (2.0 * jnp.maximum(uv, 0.0)),))
    dw_down = matmul("dw_down", act, dx3_b, "tn", BF16, 512, d_model, 4096)
    dw_up = matmul("dw_up", h2, du, "tn", BF16, d_model, 512, 4096)

    def norm_bwd(acc, xv, dres, g):
        _, vjp = jax.vjp(f_norm, xv, g)
        dx, dg = vjp(acc)
        return dx + dres, dx + dres, dg

    dx2, dx2_b, d_ln2 = matmul("d_h2", du, w_up_f, "nt", (F32, BF16), 512, d_model, d_ff, extras=[x2, dx3], pars=[g2],
                               epilogue=norm_bwd, sums=[((1, d_model), F32)])

    dycat = matmul("d_ycat", dx2_b, w_out_f, "nt", F32, 1024, d_model, d_model)
    dw_out = matmul("dw_out", ycat, dx2_b, "tn", BF16, d_model, d_model, 2048)
    dq, dk_sb, dv_sb, d_sg = sb_bwd(by_seq(p), sgr, o_raw, tot, by_seq(dycat), sb_first, n_seq, seq_len)
    d_sb = [flat(dq), flat(dk_sb), flat(dv_sb)]
    late_grads = {"w_out": dw_out.reshape(N_DEV, -1, d_model), "w_up": _col_parts(dw_up),
                  "w_down": dw_down.reshape(N_DEV, -1, d_model)}
    (dr, dkraw, dv, dlw, dasig, d_kk, d_ka, d_rk, d_gw, d_gb), late_parts = rwkv_bwd(
        *rwkv_in, kkr, kar, rkr, gwr, gbr, s0_all, tinv_all, by_seq(dycat), by_seq(gate), n_seq, seq_len,
        Hosted([late_grads[n] for n in LATE], False))
    prep_cots = [flat(t) for t in (dr, dkraw, dv, dlw, dasig)] + [dycat, z]
    dp, (d_mu, d_w0, d_wd, d_a0, d_wa, d_wg) = rwkv_prep_bwd(p, prep_edges, prep_pars, prep_cots, d_sb, seq_len)

    dw_in = matmul("dw_in", h1, dp, "tn", BF16, d_model, in_cols // 2, 2048)
    lora_parts = lambda t: _col_parts(t).astype(BF16)
    early_grads = {"w_in": _col_parts(dw_in), "w_decay_up": lora_parts(d_wd[:HEAD_DIM]),
                   "w_aaa_up": lora_parts(d_wa[HEAD_DIM:]), "w_gate_up": lora_parts(d_wg)}
    dx, d_ln1, *early_parts = matmul(
        "d_h1", dp, w_in_f, "nt", (F32,), 512, d_model, in_cols, extras=[x2d, dx2], pars=[g1],
        epilogue=lambda *t: norm_bwd(*t)[1:], sums=[((1, d_model), F32)],
        hosted=Hosted([early_grads[n] for n in EARLY], False))
    parts = dict(zip(EARLY, early_parts))
    parts.update(zip(LATE, late_parts))

    small_grads = {"ln1_g": d_ln1, "tok_mu": d_mu, "w0": d_w0, "a0": d_a0, "k_k": d_kk, "k_a": d_ka, "r_k": d_rk,
                   "gn_w": d_gw, "gn_b": d_gb, "sb_gain": d_sg, "ln2_g": d_ln2, "lnf_g": d_lnf}
    n_small = sum(int(weights[n].size) for n in SMALL)
    pack_rows = -(-(n_small + 1) // (8 * LANES)) * 8
    packed = _pack([small_grads[n] for n in SMALL] + [loss_acc[0, :1]], pack_rows)
    (small_parts,) = exchange("gather_small", [packed], True)

    results = {}
    for n in BIG:
        w2d = weights[n][0]
        tile = w2d.shape[0] if w2d.shape[0] <= 256 else 256
        results[n] = adamw("adamw_" + n, w2d, parts[n], mom_m[n][0], mom_v[n][0], tile)
    pk = lambda d: _pack([d[n] for n in SMALL] + [jnp.zeros((1,), F32)], pack_rows)
    sg, sd, sm, sv = adamw("adamw_small", pk(weights), small_parts, pk(mom_m), pk(mom_v), pack_rows)
    off = 0
    for n in SMALL:
        size = int(weights[n].size)
        results[n] = tuple(t.reshape(-1)[off:off + size] for t in (sg, sd, sm, sv))
        off += size
    loss = sg.reshape(-1)[off]

    out = [loss, dx.reshape(x.shape)]
    for kind in range(4):
        out += [results[n][kind].reshape(weights[n].shape) for n in ORDER]
    return tuple(out)
```

```python
import functools
import math

import jax
import jax.numpy as jnp
from jax import lax
from jax.experimental import pallas as pl
from jax.experimental.pallas import tpu as pltpu

F32 = jnp.float32
BF16 = jnp.bfloat16

N_DEV = 8
HEAD_DIM = 64
LANES = 128
RWKV_W = 512
SB_W = 512
LORA_WA = 128
GATE_LORA = 128
RWKV_COLS = 3 * RWKV_W + LORA_WA + GATE_LORA
RMS_EPS = 1e-5
GN_EPS = 64e-5
CHUNK = 64
QB = 256
SB_SCALE = HEAD_DIM ** -0.5
ADAM_LR, ADAM_B1, ADAM_B2, ADAM_EPS, ADAM_WD, ADAM_STEP = 0.001, 0.9, 0.999, 1e-08, 0.01, 10
VMEM_LIMIT = 56 * 1024 * 1024


_DIMS = {
    "nn": (((1,), (0,)), ((), ())),
    "nt": (((1,), (1,)), ((), ())),
    "tn": (((0,), (0,)), ((), ())),
}


def _pieces(x, n):
    if n == 1:
        return [x.astype(BF16)]
    out, rem = [], x.astype(F32)
    for i in range(n):
        p = rem.astype(BF16)
        out.append(p)
        if i + 1 < n:
            rem = rem - p.astype(F32)
    return out


def _dot(a, b, form, pa, pb):
    pieces_a, pieces_b = _pieces(a, pa), _pieces(b, pb)
    keep = max(pa, pb)
    acc = None
    for i, ai in enumerate(pieces_a):
        for j, bj in enumerate(pieces_b):
            if i + j >= keep:
                continue
            t = lax.dot_general(ai, bj, _DIMS[form], preferred_element_type=F32)
            acc = t if acc is None else acc + t
    return acc


BOTH = (True, True)


@functools.partial(jax.custom_vjp, nondiff_argnums=(2, 3, 4, 5, 6))
def mm(a, b, form, pa, pb, diff=BOTH, grad_pieces=None):
    return _dot(a, b, form, pa, pb)


def _mm_fwd(a, b, form, pa, pb, diff, grad_pieces):
    return _dot(a, b, form, pa, pb), (a, b)


def _mm_bwd(form, pa, pb, diff, grad_pieces, res, g):
    a, b = res
    pg = grad_pieces or max(pa, pb)
    da, db = jnp.zeros_like(a), jnp.zeros_like(b)
    if form == "nn":
        if diff[0]:
            da = mm(g, b, "nt", pg, pb)
        if diff[1]:
            db = mm(a, g, "tn", pa, pg)
    elif form == "nt":
        if diff[0]:
            da = mm(g, b, "nn", pg, pb)
        if diff[1]:
            db = mm(g, a, "tn", pg, pa)
    else:
        if diff[0]:
            da = mm(b, g, "nt", pb, pg)
        if diff[1]:
            db = mm(a, g, "nn", pa, pg)
    return da, db


mm.defvjp(_mm_fwd, _mm_bwd)


def _stack_rows(top, bottom):
    return jnp.concatenate([top, bottom], axis=0)


@jax.custom_vjp
def _split_rows(x):
    half = x.shape[0] // 2
    return x[:half], x[half:]


def _split_rows_fwd(x):
    return _split_rows(x), None


def _split_rows_bwd(_, grads):
    return (_stack_rows(*grads),)


_split_rows.defvjp(_split_rows_fwd, _split_rows_bwd)


def _lane_lo(shape):
    return lax.broadcasted_iota(jnp.int32, shape, len(shape) - 1) < HEAD_DIM


def _segsum(x):
    lo = _lane_lo(x.shape)
    s_lo = jnp.sum(jnp.where(lo, x, 0.0), axis=-1, keepdims=True)
    s_hi = jnp.sum(jnp.where(lo, 0.0, x), axis=-1, keepdims=True)
    return jnp.where(lo, s_lo, s_hi)


def _sigmoid(x):
    return 0.5 * (jnp.tanh(0.5 * x) + 1.0)


@jax.custom_vjp
def _log_one_minus_sigmoid(z):
    return -jnp.maximum(z, 0.0) - jnp.log(1.0 + jnp.exp(-jnp.abs(z)))


def _log_oms_fwd(z):
    out = _log_one_minus_sigmoid(z)
    return out, (z, out)


def _log_oms_bwd(res, g):
    z, out = res
    return (-g * jnp.exp(z + out),)


_log_one_minus_sigmoid.defvjp(_log_oms_fwd, _log_oms_bwd)


def f_norm(x, g):
    return x * lax.rsqrt(jnp.mean(x * x, axis=-1, keepdims=True) + RMS_EPS) * g


def f_prep(p, pprev, mu, w0, wd_pad, a0, wa_pad, wg):
    pm = p + mu * (pprev - p)
    r = pm[:, 0:RWKV_W]
    k = pm[:, RWKV_W:2 * RWKV_W]
    v = pm[:, 2 * RWKV_W:3 * RWKV_W]
    xwa = pm[:, 3 * RWKV_W:3 * RWKV_W + LORA_WA]
    xg = pm[:, 3 * RWKV_W + LORA_WA:RWKV_COLS]
    w = _log_one_minus_sigmoid(-(w0 + mm(jnp.tanh(xwa), wd_pad, "nn", 1, 1))) - 0.5
    lw = -jnp.exp(w)
    asig = _sigmoid(a0 + mm(xwa, wa_pad, "nn", 1, 1))
    gate = mm(_sigmoid(xg), wg, "nn", 1, 1)
    return r, k, v, lw, asig, gate


def _tri(n, kind):
    row = lax.broadcasted_iota(jnp.int32, (n, n), 0)
    col = lax.broadcasted_iota(jnp.int32, (n, n), 1)
    if kind == "lower_incl":
        return row >= col
    return row > col


@functools.partial(jax.custom_vjp, nondiff_argnums=(1,))
def _nilpotent_inverses(mats, hp):
    size = mats[0].shape[0]
    eye = (lax.broadcasted_iota(jnp.int32, (size, size), 0) == lax.broadcasted_iota(jnp.int32, (size, size), 1))
    tinv = [eye.astype(F32) + x for x in mats]
    pw = [mm(x, x, "nn", hp, hp) for x in mats]
    for _ in range(int(math.log2(size)) - 2):
        both = [_split_rows(mm(_stack_rows(t, x), x, "nn", hp, hp)) for t, x in zip(tinv, pw)]
        tinv = [t + tx for t, (tx, _) in zip(tinv, both)]
        pw = [xx for _, xx in both]
    return [t + mm(t, x, "nn", hp, hp) for t, x in zip(tinv, pw)]


def _nilpotent_inverses_fwd(mats, hp):
    tinv = _nilpotent_inverses(mats, hp)
    return tinv, tinv


def _nilpotent_inverses_bwd(hp, tinv, grads):
    right = [mm(g, t, "nt", hp, hp) for g, t in zip(grads, tinv)]
    return ([mm(t, x, "tn", hp, hp) for t, x in zip(tinv, right)],)


_nilpotent_inverses.defvjp(_nilpotent_inverses_fwd, _nilpotent_inverses_bwd)


@functools.partial(jax.custom_vjp, nondiff_argnums=(2,))
def _known_inverses(mats, tinv, hp):
    return list(tinv)


def _known_inverses_fwd(mats, tinv, hp):
    return list(tinv), tinv


def _known_inverses_bwd(hp, tinv, grads):
    return _nilpotent_inverses_bwd(hp, tinv, grads) + ([jnp.zeros_like(t) for t in tinv],)


_known_inverses.defvjp(_known_inverses_fwd, _known_inverses_bwd)


def rwkv_chunk(state, r, kraw, v, lw, asig, k_k, k_a, r_k, gn_w, gn_b, hp, saved_tinv=None, with_tinv=False):
    n = len(r)
    L = r[0].shape[0]
    lo = _lane_lo((1, LANES))
    masks = (lo, jnp.logical_not(lo))
    incl = _tri(L, "lower_incl")
    strict = _tri(L, "strict")
    tri = incl.astype(F32)
    kk = [x * w for x, w in zip(kraw, k_k)]
    kk = [x / jnp.maximum(jnp.sqrt(_segsum(x * x)), 1e-12) for x in kk]
    k = [x * (1.0 + (s - 1.0) * w) for x, s, w in zip(kraw, asig, k_a)]
    b = [x * s for x, s in zip(kk, asig)]
    c = [mm(tri, x, "nn", 1, 3, (False, True)) for x in lw]
    at = [-x * jnp.exp(ci - li) for x, ci, li in zip(kk, c, lw)]
    rt = [x * jnp.exp(ci) for x, ci in zip(r, c)]
    einv = [jnp.exp(-ci) for ci in c]
    bt = [x * e for x, e in zip(b, einv)]
    kt = [x * e for x, e in zip(k, einv)]
    inst = [(s, m) for s in range(n) for m in masks]
    ar_h = [_stack_rows(jnp.where(m, at[s], 0.0), jnp.where(m, rt[s], 0.0)) for s, m in inst]
    on_b = [_split_rows(mm(x, bt[s], "nt", hp, hp)) for x, (s, _) in zip(ar_h, inst)]
    on_k = [_split_rows(mm(x, kt[s], "nt", hp, hp)) for x, (s, _) in zip(ar_h, inst)]
    a_ab = [jnp.where(strict, x, 0.0) for x, _ in on_b]
    b_rb = [jnp.where(incl, x, 0.0) for _, x in on_b]
    a_ak = [jnp.where(strict, x, 0.0) for x, _ in on_k]
    b_rk = [jnp.where(incl, x, 0.0) for _, x in on_k]
    tinv = _nilpotent_inverses(a_ab, hp) if saved_tinv is None else _known_inverses(a_ab, saved_tinv, hp)
    on_state = [_split_rows(mm(x, state[s], "nt", hp, hp)) for x, (s, _) in zip(ar_h, inst)]
    on_v = [_split_rows(mm(_stack_rows(m1, m2), v[s], "nn", hp, hp)) for m1, m2, (s, _) in zip(a_ak, b_rk, inst)]
    u_h = [mm(t, sa + av, "nn", hp, hp) for t, (sa, _), (av, _) in zip(tinv, on_state, on_v)]
    y_h = [sr + mm(m1, u, "nn", hp, hp) + bv for (_, sr), m1, u, (_, bv) in zip(on_state, b_rb, u_h, on_v)]
    u_all = [jnp.where(lo, u_h[2 * s], u_h[2 * s + 1]) for s in range(n)]
    y_all = [jnp.where(lo, y_h[2 * s], y_h[2 * s + 1]) for s in range(n)]
    c_last = [jnp.sum(x, axis=0, keepdims=True) for x in lw]
    efwd = [jnp.exp(cl - ci) for cl, ci in zip(c_last, c)]
    new_state = [st * jnp.exp(cl) + mm(_stack_rows(u, vi), _stack_rows(bi * e, ki * e), "tn", hp, hp)
                 for st, cl, u, bi, e, vi, ki in zip(state, c_last, u_all, b, efwd, v, k)]
    row_head = lax.broadcasted_iota(jnp.int32, (LANES, LANES), 0) // HEAD_DIM
    col_head = lax.broadcasted_iota(jnp.int32, (LANES, LANES), 1) // HEAD_DIM
    new_state = [jnp.where(row_head == col_head, x, 0.0) for x in new_state]
    outs = []
    for y, ri, ki, vi, w_rk, w_gw, w_gb in zip(y_all, r, k, v, r_k, gn_w, gn_b):
        mean = _segsum(y) * (1.0 / HEAD_DIM)
        d = y - mean
        var = _segsum(d * d) * (1.0 / HEAD_DIM)
        yn = d * lax.rsqrt(var + GN_EPS) * w_gw + w_gb
        outs.append(yn + _segsum(ri * ki * w_rk) * vi)
    return (outs, new_state, tinv) if with_tinv else (outs, new_state)


def sb_tile(q, k, v, c_lo, c_hi, diag, from_here=None):
    n = len(q)
    lo = _lane_lo((1, LANES))
    below = _tri(QB, "strict")
    from_s = _tri(QB, "lower_incl").astype(F32)
    inst = [(s, h) for s in range(n) for h in (0, 1)]
    carry = [(c_lo[s], c_hi[s])[h] for s, h in inst]
    z = [mm(q[s][h], k[s], "nt", 1, 1) for s, h in inst]
    log_keep = [_log_one_minus_sigmoid(x) for x in z]
    if diag:
        log_keep = [jnp.where(below, x, 0.0) for x in log_keep]
    tail = [mm(x, from_s, "nn", SB_SUM_PIECES, 1, (True, False), 1) for x in log_keep]
    own = [_first_col(x) for x in tail]
    if from_here is not None:
        carry = [lax.stop_gradient(from_here[s][h] - o) + cr for (s, h), o, cr in zip(inst, own, carry)]
    log_a = [x + tl + cr for x, tl, cr in zip(z, tail, carry)]
    if diag:
        log_a = [jnp.where(below, x, -1e30) for x in log_a]
    att = [jnp.exp(x) for x in log_a]
    out_h = [mm(x, v[s], "nn", 1, 1) for x, (s, _) in zip(att, inst)]
    out = [jnp.where(lo, out_h[2 * s], out_h[2 * s + 1]) for s in range(n)]
    return out, [own[2 * s] for s in range(n)], [own[2 * s + 1] for s in range(n)]


@jax.custom_vjp
def _first_col(x):
    return x[:, 0:1]


def _first_col_fwd(x):
    return x[:, 0:1], None


def _first_col_bwd(_, g):
    col = lax.broadcasted_iota(jnp.int32, (g.shape[0], QB), 1)
    return (jnp.where(col == 0, g, 0.0),)


_first_col.defvjp(_first_col_fwd, _first_col_bwd)


def sb_split_q(q):
    lo = _lane_lo((1, LANES))
    qs = q * SB_SCALE
    return jnp.where(lo, qs, 0.0), jnp.where(lo, 0.0, qs)


def sb_post(o, gain):
    return o * lax.rsqrt(_segsum(o * o) * (1.0 / HEAD_DIM) + RMS_EPS) * gain


def f_final(x3, g, target):
    y = f_norm(x3, g)
    err = y - target
    return 0.5 * jnp.mean(err * err, axis=-1, keepdims=True)


def _params(sem):
    return pltpu.CompilerParams(dimension_semantics=sem, vmem_limit_bytes=VMEM_LIMIT)


def rw_call(name, body_fn, rows, pars, out_rows, tile, hosted=None):
    n_rows = rows[0].shape[0]
    tile = min(tile, n_rows)
    steps = n_rows // tile
    row_specs = [pl.BlockSpec((tile, arr.shape[1]), lambda i: (i, 0)) for arr in rows]
    par_specs = [pl.BlockSpec(p.shape, lambda i: (0, 0)) for p in pars]
    nr, npar, nor = len(rows), len(pars), len(out_rows)
    n_host = hosted.n if hosted is not None else 0

    def body(*refs):
        if hosted is not None:
            refs, copies = hosted.split(refs, nr + npar, nor)
            hosted.run(copies, pl.program_id(0) == 0, pl.program_id(0) == steps - 1)
        row_outs = body_fn([r[...] for r in refs[:nr]], [r[...] for r in refs[nr:nr + npar]])
        for ref, val in zip(refs[nr + npar:], row_outs):
            ref[...] = val.astype(ref.dtype)

    outs = pl.pallas_call(
        body, name=name, grid=(steps,), in_specs=row_specs + par_specs + [ANY_SPEC] * n_host,
        out_specs=[pl.BlockSpec((tile, c), lambda i: (i, 0)) for c, _ in out_rows] + [ANY_SPEC] * n_host,
        out_shape=[jax.ShapeDtypeStruct((n_rows, c), dt) for c, dt in out_rows]
        + (exchange_shapes(hosted.arrays, hosted.gather) if n_host else []),
        scratch_shapes=exchange_sems(n_host) if n_host else [],
        compiler_params=_params(("arbitrary",)),
    )(*rows, *pars, *(hosted.arrays if n_host else []))
    return outs


def matmul(name, a, b, form, out_dtype, tm, tn, tk, extras=(), pars=(), epilogue=None, sums=(), hosted=None):
    out_dtypes = out_dtype if isinstance(out_dtype, tuple) else (out_dtype,)
    tm, tn, tk = min(tm, a.shape[1 if form == "tn" else 0]), min(tn, b.shape[0 if form == "nt" else 1]), min(tk, a.shape[0 if form == "tn" else 1])
    if form == "nn":
        (m, kd), n = a.shape, b.shape[1]
        a_spec = pl.BlockSpec((tm, tk), lambda i, j, k: (i, k))
        b_spec = pl.BlockSpec((tk, tn), lambda i, j, k: (k, j))
    elif form == "nt":
        (m, kd), n = a.shape, b.shape[0]
        a_spec = pl.BlockSpec((tm, tk), lambda i, j, k: (i, k))
        b_spec = pl.BlockSpec((tn, tk), lambda i, j, k: (j, k))
    else:
        (kd, m), n = a.shape, b.shape[1]
        a_spec = pl.BlockSpec((tk, tm), lambda i, j, k: (k, i))
        b_spec = pl.BlockSpec((tk, tn), lambda i, j, k: (k, j))
    ksteps = kd // tk

    n_extra, n_par, n_out, n_sum = len(extras), len(pars), len(out_dtypes), len(sums)
    n_in = 2 + n_extra + n_par
    grid = (m // tm, n // tn, ksteps)

    def body(*refs):
        if hosted is not None:
            refs, copies = hosted.split(refs, n_in, n_out + n_sum)
            here = [pl.program_id(d) for d in range(3)]
            hosted.run(copies, functools.reduce(jnp.logical_and, [h == 0 for h in here]),
                       functools.reduce(jnp.logical_and, [h == g - 1 for h, g in zip(here, grid)]))
        a_ref, b_ref, rest = refs[0], refs[1], refs[2:]
        e_refs, o_refs = rest[:n_extra + n_par], rest[n_extra + n_par:n_extra + n_par + n_out]
        s_refs = rest[n_extra + n_par + n_out:n_extra + n_par + n_out + n_sum]
        kstep = pl.program_id(2)
        part = lax.dot_general(a_ref[...].astype(BF16), b_ref[...].astype(BF16), _DIMS[form],
                               preferred_element_type=F32)

        def finish(acc):
            outs = epilogue(acc, *[r[...] for r in e_refs]) if epilogue else (acc,)
            for ref, val in zip(o_refs, outs[:n_out]):
                ref[...] = val.astype(ref.dtype)
            if n_sum:
                first_tile = jnp.logical_and(pl.program_id(0) == 0, pl.program_id(1) == 0)

                @pl.when(first_tile)
                def _():
                    for ref, val in zip(s_refs, outs[n_out:]):
                        ref[...] = val.astype(ref.dtype)

                @pl.when(jnp.logical_not(first_tile))
                def _():
                    for ref, val in zip(s_refs, outs[n_out:]):
                        ref[...] = ref[...] + val.astype(ref.dtype)

        if ksteps == 1:
            finish(part)
            return
        acc_ref = rest[n_extra + n_par + n_out + n_sum]

        @pl.when(kstep == 0)
        def _():
            acc_ref[...] = part

        @pl.when(jnp.logical_and(kstep > 0, kstep < ksteps - 1))
        def _():
            acc_ref[...] = acc_ref[...] + part

        @pl.when(kstep == ksteps - 1)
        def _():
            finish(acc_ref[...] + part)

    out_spec = pl.BlockSpec((tm, tn), lambda i, j, k: (i, j))
    whole = lambda shape: pl.BlockSpec(shape, lambda i, j, k: (0,) * len(shape))
    n_host = hosted.n if hosted is not None else 0
    host_in = list(hosted.arrays) if hosted is not None else []
    host_out = exchange_shapes(hosted.arrays, hosted.gather) if hosted is not None else []
    outs = pl.pallas_call(
        body, name=name, grid=grid,
        in_specs=[a_spec, b_spec] + [out_spec] * n_extra + [whole(t.shape) for t in pars] + [ANY_SPEC] * n_host,
        out_specs=[out_spec] * n_out + [whole(s) for s, _ in sums] + [ANY_SPEC] * n_host,
        out_shape=[jax.ShapeDtypeStruct((m, n), dt) for dt in out_dtypes]
        + [jax.ShapeDtypeStruct(s, dt) for s, dt in sums] + host_out,
        scratch_shapes=([pltpu.VMEM((tm, tn), F32)] if ksteps > 1 else []) + (exchange_sems(n_host) if n_host else []),
        compiler_params=_params(("arbitrary",) * 3 if (sums or n_host) else ("parallel", "parallel", "arbitrary")),
    )(a, b, *extras, *pars, *host_in)
    return outs if (isinstance(out_dtype, tuple) or sums or n_host) else outs[0]


PREP_TILE = 512
PREP_TILE_BWD = 256
SUBLANES = 8


def _shift_in(rows, first):
    rolled = pltpu.roll(rows, 1, 0)
    row = lax.broadcasted_iota(jnp.int32, (SUBLANES, rows.shape[1]), 0)
    head = jnp.where(row == 0, first, rolled[0:SUBLANES])
    return jnp.concatenate([head, rolled[SUBLANES:]], axis=0), rolled


def rwkv_prep_fwd(p, pars, seq_len):
    n_tok = p.shape[0]
    tile = min(PREP_TILE, seq_len)
    tile_b = min(PREP_TILE_BWD, tile)
    steps, per_seq, sub = n_tok // tile, seq_len // tile, tile // tile_b
    n_par = len(pars)

    def body(p_ref, *rest):
        par_refs, out_refs, edge_ref, last8 = rest[:n_par], rest[n_par:n_par + 6], rest[n_par + 6], rest[n_par + 7]
        step = pl.program_id(0)

        @pl.when(step == 0)
        def _():
            last8[...] = jnp.zeros_like(last8)

        rows = p_ref[...]
        before = jnp.where(step % per_seq == 0, 0.0, pltpu.roll(last8[...], 1, 0))
        prev, rolled = _shift_in(rows, before)
        edge_ref[0] = prev[0:SUBLANES]
        for m in range(1, sub):
            edge_ref[m] = rolled[m * tile_b:m * tile_b + SUBLANES]
        last8[...] = rows[tile - SUBLANES:tile]
        for ref, val in zip(out_refs, f_prep(rows, prev, *[r[...] for r in par_refs])):
            ref[...] = val

    row_out = pl.BlockSpec((tile, RWKV_W), lambda i: (i, 0))
    outs = pl.pallas_call(
        body, name="rwkv_prep", grid=(steps,),
        in_specs=[pl.BlockSpec((tile, RWKV_COLS), lambda i: (i, 0))] + [pl.BlockSpec(t.shape, lambda i: (0, 0)) for t in pars],
        out_specs=[row_out] * 6 + [pl.BlockSpec((sub, SUBLANES, RWKV_COLS), lambda i: (i, 0, 0))],
        out_shape=[jax.ShapeDtypeStruct((n_tok, RWKV_W), F32)] * 6
        + [jax.ShapeDtypeStruct((steps * sub, SUBLANES, RWKV_COLS), F32)],
        scratch_shapes=[pltpu.VMEM((SUBLANES, RWKV_COLS), F32)],
        compiler_params=_params(("arbitrary",)),
    )(p, *pars)
    return outs[:6], outs[6]


def rwkv_prep_bwd(p, edges, pars, cots, d_sb, seq_len):
    n_tok = p.shape[0]
    tile = min(PREP_TILE_BWD, seq_len)
    steps, per_seq = n_tok // tile, seq_len // tile
    n_par = len(pars)
    back = lambda i: steps - 1 - i

    def body(p_ref, edge_ref, *rest):
        par_refs, rest = rest[:n_par], rest[n_par:]
        cot_refs, dy_ref, z_ref, sb_refs = rest[:5], rest[5], rest[6], rest[7:10]
        dp_ref, acc_refs, next8 = rest[10], rest[11:11 + n_par], rest[11 + n_par]
        step = pl.program_id(0)
        first = step == 0

        @pl.when(first)
        def _():
            next8[...] = jnp.zeros_like(next8)

        rows = p_ref[...]
        prev, _ = _shift_in(rows, edge_ref[0])
        _, vjp = jax.vjp(f_prep, rows, prev, *[r[...].astype(F32) for r in par_refs])
        grads = vjp(tuple(r[...] for r in cot_refs) + (dy_ref[...] * z_ref[...],))
        d_rows, d_prev = grads[0], grads[1]
        up = pltpu.roll(d_prev, tile - 1, 0)
        ends_seq = back(step) % per_seq == per_seq - 1
        after = jnp.where(ends_seq, 0.0, pltpu.roll(next8[...], SUBLANES - 1, 0))
        row = lax.broadcasted_iota(jnp.int32, (SUBLANES, RWKV_COLS), 0)
        tail = jnp.where(row == SUBLANES - 1, after, up[tile - SUBLANES:tile])
        d_rows = d_rows + jnp.concatenate([up[:tile - SUBLANES], tail], axis=0)
        next8[...] = d_prev[0:SUBLANES]
        dp_ref[...] = jnp.concatenate([d_rows] + [r[...] for r in sb_refs], axis=1).astype(dp_ref.dtype)

        @pl.when(first)
        def _():
            for ref, val in zip(acc_refs, grads[2:]):
                ref[...] = val

        @pl.when(jnp.logical_not(first))
        def _():
            for ref, val in zip(acc_refs, grads[2:]):
                ref[...] = ref[...] + val

    cols = RWKV_COLS + sum(t.shape[1] for t in d_sb)
    half = pl.BlockSpec((tile, RWKV_W), lambda i: (back(i), 0))
    par_specs = [pl.BlockSpec(t.shape, lambda i: (0, 0)) for t in pars]
    outs = pl.pallas_call(
        body, name="d_rwkv_prep", grid=(steps,),
        in_specs=[pl.BlockSpec((tile, RWKV_COLS), lambda i: (back(i), 0)),
                  pl.BlockSpec((1, SUBLANES, RWKV_COLS), lambda i: (back(i), 0, 0))] + par_specs + [half] * 10,
        out_specs=[pl.BlockSpec((tile, cols), lambda i: (back(i), 0))] + par_specs,
        out_shape=[jax.ShapeDtypeStruct((n_tok, cols), BF16)] + [jax.ShapeDtypeStruct(t.shape, F32) for t in pars],
        scratch_shapes=[pltpu.VMEM((SUBLANES, RWKV_COLS), F32)],
        compiler_params=_params(("arbitrary",)),
    )(p, edges, *pars, *cots, *d_sb)
    return outs[0], outs[1:]


RWKV_HP = 1


RWKV_PAIRS = 4


def _rwkv_specs(n_seq, chunk_of):
    width = RWKV_PAIRS * LANES
    row = pl.BlockSpec((n_seq, CHUNK, width), lambda g, c: (0, chunk_of(c), g))
    par = pl.BlockSpec((1, width), lambda g, c: (0, g))
    s0 = pl.BlockSpec((1, 1, RWKV_PAIRS * n_seq, LANES, LANES), lambda g, c: (g, chunk_of(c), 0, 0, 0))
    tinv = pl.BlockSpec((1, 1, 2 * RWKV_PAIRS * n_seq, CHUNK, CHUNK), lambda g, c: (g, chunk_of(c), 0, 0, 0))
    return row, par, s0, tinv


class Hosted:
    def __init__(self, arrays, gather):
        self.arrays, self.gather, self.n = list(arrays), gather, len(arrays)

    def split(self, refs, n_in, n_out):
        n = self.n
        ins, outs, sems = refs[n_in:n_in + n], refs[n_in + n + n_out:n_in + 2 * n + n_out], refs[-3:]
        own = refs[:n_in] + refs[n_in + n:n_in + n + n_out] + refs[n_in + 2 * n + n_out:-3]
        return own, exchange_copies(ins, outs, *sems, self.gather)

    def run(self, copies, first, last):
        @pl.when(first)
        def _():
            for cp in copies:
                cp.start()

        @pl.when(last)
        def _():
            for cp in copies:
                cp.wait()


class HostedChipGather(Hosted):
    def __init__(self, arrays):
        super().__init__(arrays, True)

    def split(self, refs, n_in, n_out):
        n = self.n
        ins, outs, sems = refs[n_in:n_in + n], refs[n_in + n + n_out:n_in + 2 * n + n_out], refs[-3:]
        own = refs[:n_in] + refs[n_in + n:n_in + n + n_out] + refs[n_in + 2 * n + n_out:-3]
        return own, (ins, outs, sems)

    def run(self, state, first, last):
        ins, outs, (send_sems, recv_sems, local_sems) = state
        x, y, c = lax.axis_index("x"), lax.axis_index("y"), lax.axis_index("c")
        chips = [((x + 1) % 2, y), (x, (y + 1) % 2), ((x + 1) % 2, (y + 1) % 2)]

        def block(a, k, of, to, src=None):
            dst = outs[a].at[4 * of[0] + 2 * of[1] + of[2]]
            return pltpu.make_async_remote_copy(
                src_ref=dst if src is None else src, dst_ref=dst, send_sem=send_sems.at[a, k],
                recv_sem=recv_sems.at[a, k], device_id=to, device_id_type=pl.DeviceIdType.MESH)

        me, sibling = (x, y, c), (x, y, 1 - c)
        local = [pltpu.make_async_copy(ins[a], outs[a].at[4 * x + 2 * y + c], local_sems.at[a]) for a in range(self.n)]
        mine = [block(a, 0, me, sibling, ins[a]) for a in range(self.n)]
        mine += [block(a, 1 + j, me, (*chip, c), ins[a]) for a in range(self.n) for j, chip in enumerate(chips)]
        passed = [block(a, 4 + j, (*chip, c), sibling) for a in range(self.n) for j, chip in enumerate(chips)]

        @pl.when(first)
        def _():
            for cp in local + mine:
                cp.start()

        @pl.when(last)
        def _():
            for a in range(self.n):
                for j, chip in enumerate(chips):
                    block(a, 1 + j, (*chip, c), me).wait_recv()
                    passed[a * len(chips) + j].start()
            for a in range(self.n):
                block(a, 0, sibling, me).wait_recv()
                for j, chip in enumerate(chips):
                    block(a, 4 + j, (*chip, 1 - c), me).wait_recv()
            for cp in mine + passed:
                cp.wait_send()
            for cp in local:
                cp.wait()


def rwkv_fwd(r, kraw, v, lw, asig, k_k, k_a, r_k, gn_w, gn_b, n_seq, seq_len, hosted):
    n_chunks = seq_len // CHUNK
    n_groups = RWKV_W // (RWKV_PAIRS * LANES)
    n_inst = RWKV_PAIRS * n_seq
    row, par, s0_spec, tinv_spec = _rwkv_specs(n_seq, lambda c: c)
    inst = [(s, pl.ds(pp * LANES, LANES)) for pp in range(RWKV_PAIRS) for s in range(n_seq)]

    def body(*refs):
        own, copies = hosted.split(refs, 10, 3)
        row_refs, par_refs, (z_ref, s0_ref, tinv_ref, state) = own[:5], own[5:10], own[10:]
        step = pl.program_id(0) * n_chunks + pl.program_id(1)
        hosted.run(copies, step == 0, step == n_groups * n_chunks - 1)

        @pl.when(pl.program_id(1) == 0)
        def _():
            state[...] = jnp.zeros_like(state)

        s0 = [state[i] for i in range(n_inst)]
        rows = [[ref[s, :, lanes] for s, lanes in inst] for ref in row_refs]
        pars = [[ref[:, lanes] for _, lanes in inst] for ref in par_refs]
        z, s1, tinv = rwkv_chunk(s0, *rows, *pars, RWKV_HP, with_tinv=True)
        for i, (s, lanes) in enumerate(inst):
            s0_ref[0, 0, i] = s0[i]
            z_ref[s, :, lanes] = z[i]
            state[i] = s1[i]
        for i, t in enumerate(tinv):
            tinv_ref[0, 0, i] = t

    outs = pl.pallas_call(
        body, name="rwkv_fwd", grid=(n_groups, n_chunks),
        in_specs=[row] * 5 + [par] * 5 + [ANY_SPEC] * hosted.n,
        out_specs=[row, s0_spec, tinv_spec] + [ANY_SPEC] * hosted.n,
        out_shape=[jax.ShapeDtypeStruct(r.shape, F32),
                   jax.ShapeDtypeStruct((n_groups, n_chunks, n_inst, LANES, LANES), F32),
                   jax.ShapeDtypeStruct((n_groups, n_chunks, 2 * n_inst, CHUNK, CHUNK), F32)]
        + exchange_shapes(hosted.arrays, hosted.gather),
        scratch_shapes=[pltpu.VMEM((n_inst, LANES, LANES), F32)] + exchange_sems(hosted.n),
        compiler_params=_params(("arbitrary", "arbitrary")),
    )(r, kraw, v, lw, asig, k_k, k_a, r_k, gn_w, gn_b, *hosted.arrays)
    return outs[0], outs[1], outs[2], outs[3:]


def rwkv_bwd(r, kraw, v, lw, asig, k_k, k_a, r_k, gn_w, gn_b, s0_all, tinv_all, dy, gate, n_seq, seq_len, hosted):
    n_chunks = seq_len // CHUNK
    n_groups = RWKV_W // (RWKV_PAIRS * LANES)
    n_inst = RWKV_PAIRS * n_seq
    row, par, s0_spec, tinv_spec = _rwkv_specs(n_seq, lambda c: n_chunks - 1 - c)
    inst = [(s, pl.ds(pp * LANES, LANES)) for pp in range(RWKV_PAIRS) for s in range(n_seq)]

    def body(*refs):
        own, copies = hosted.split(refs, 14, 10)
        row_refs, par_refs, (s0_ref, tinv_ref, dy_ref, gate_ref) = own[:5], own[5:10], own[10:14]
        drow_refs, dpar_refs, dstate = own[14:19], own[19:24], own[24]
        step = pl.program_id(0) * n_chunks + pl.program_id(1)
        hosted.run(copies, step == 0, step == n_groups * n_chunks - 1)
        first = pl.program_id(1) == 0

        @pl.when(first)
        def _():
            dstate[...] = jnp.zeros_like(dstate)

        fn = functools.partial(rwkv_chunk, hp=RWKV_HP, saved_tinv=[tinv_ref[0, 0, i] for i in range(2 * n_inst)])
        rows = [[ref[s, :, lanes] for s, lanes in inst] for ref in row_refs]
        pars = [[ref[:, lanes] for _, lanes in inst] for ref in par_refs]
        _, vjp = jax.vjp(fn, [s0_ref[0, 0, i] for i in range(n_inst)], *rows, *pars)
        dz = [dy_ref[s, :, lanes] * gate_ref[s, :, lanes] for s, lanes in inst]
        grads = vjp((dz, [dstate[i] for i in range(n_inst)]))
        for i, (s, lanes) in enumerate(inst):
            dstate[i] = grads[0][i]
            for ref, val in zip(drow_refs, grads[1:6]):
                ref[s, :, lanes] = val[i]

        def accumulate(start):
            for ref, val in zip(dpar_refs, grads[6:]):
                for pp in range(RWKV_PAIRS):
                    lanes = pl.ds(pp * LANES, LANES)
                    total = functools.reduce(jnp.add, val[pp * n_seq:(pp + 1) * n_seq])
                    ref[:, lanes] = total if start else ref[:, lanes] + total

        @pl.when(first)
        def _():
            accumulate(True)

        @pl.when(jnp.logical_not(first))
        def _():
            accumulate(False)

    rows_shape = jax.ShapeDtypeStruct(r.shape, F32)
    par_shape = jax.ShapeDtypeStruct((1, RWKV_W), F32)
    outs = pl.pallas_call(
        body, name="rwkv_bwd", grid=(n_groups, n_chunks),
        in_specs=[row] * 5 + [par] * 5 + [s0_spec, tinv_spec, row, row] + [ANY_SPEC] * hosted.n,
        out_specs=[row] * 5 + [par] * 5 + [ANY_SPEC] * hosted.n,
        out_shape=[rows_shape] * 5 + [par_shape] * 5 + exchange_shapes(hosted.arrays, hosted.gather),
        scratch_shapes=[pltpu.VMEM((n_inst, LANES, LANES), F32)] + exchange_sems(hosted.n),
        compiler_params=_params(("arbitrary", "arbitrary")),
    )(r, kraw, v, lw, asig, k_k, k_a, r_k, gn_w, gn_b, s0_all, tinv_all, dy, gate, *hosted.arrays)
    return outs[:10], outs[10:]


SB_Q0 = RWKV_COLS // LANES
SB_K0 = SB_Q0 + SB_W // LANES
SB_V0 = SB_K0 + SB_W // LANES
SB_SEQS = 2
SB_BUFFERS = pl.Buffered(1)
SB_SUM_PIECES = 2
SB_DEAD = -110.0


def _col_of(c_lo, c_hi):
    return jnp.where(_lane_lo((1, LANES)), c_lo, c_hi)


def sb_fwd(p, gain, n_seq, seq_len, hosted):
    n_pairs = SB_W // LANES
    n_q = seq_len // QB
    nb = min(SB_SEQS, n_seq)

    def seq_spec(c0):
        return pl.BlockSpec((nb, seq_len, LANES), functools.partial(lambda b, h, c0: (b, 0, c0 + h), c0=c0),
                            pipeline_mode=SB_BUFFERS)

    out_spec = pl.BlockSpec((nb, seq_len, LANES), lambda b, h: (b, 0, h), pipeline_mode=SB_BUFFERS)

    def body(*refs):
        own, copies = hosted.split(refs, 4, 4)
        q_ref, k_ref, v_ref, g_ref, y_ref, o_ref, tot_ref, first_ref = own
        step = pl.program_id(0) * n_pairs + pl.program_id(1)
        hosted.run(copies, step == 0, step == (n_seq // nb) * n_pairs - 1)
        gain = g_ref[...]

        def q_block(i, _):
            qs = pl.multiple_of(i * QB, QB)
            seqs = range(nb)
            zeros = [jnp.zeros((QB, 1), F32)] * nb
            qv = [sb_split_q(q_ref[s, pl.ds(qs, QB), :]) for s in seqs]
            add = lambda xs, ys: [x + y for x, y in zip(xs, ys)]

            def tiles(ks, c_lo, c_hi, diag):
                return sb_tile(qv, [k_ref[s, pl.ds(ks, QB), :] for s in seqs],
                               [v_ref[s, pl.ds(ks, QB), :] for s in seqs], c_lo, c_hi, diag)

            def alive(c_lo, c_hi):
                top = functools.reduce(jnp.maximum, list(c_lo) + list(c_hi))
                return jnp.max(top) > SB_DEAD

            def k_block(state):
                j, _, (o, c_lo, c_hi) = state
                o2, s_lo, s_hi = tiles(pl.multiple_of(j * QB, QB), c_lo, c_hi, False)
                c_lo, c_hi = add(c_lo, s_lo), add(c_hi, s_hi)
                return j - 1, alive(c_lo, c_hi), (add(o, o2), c_lo, c_hi)

            o, c_lo, c_hi = tiles(qs, zeros, zeros, True)
            j, _, (o, c_lo, c_hi) = lax.while_loop(lambda st: jnp.logical_and(st[0] >= 0, st[1]), k_block,
                                                   (i - 1, alive(c_lo, c_hi), (o, c_lo, c_hi)))
            first_ref[pl.program_id(0), pl.program_id(1), i] = j + 1
            for s in seqs:
                o_ref[s, pl.ds(qs, QB), :] = o[s]
                tot_ref[s, pl.ds(qs, QB), :] = jnp.broadcast_to(_col_of(c_lo[s], c_hi[s]), (QB, LANES))
                y_ref[s, pl.ds(qs, QB), :] = sb_post(o[s], gain)
            return 0

        lax.fori_loop(0, n_q, q_block, 0)

    shape = jax.ShapeDtypeStruct((n_seq, seq_len, SB_W), F32)
    return pl.pallas_call(
        body, name="sb_fwd", grid=(n_seq // nb, n_pairs),
        in_specs=[seq_spec(SB_Q0), seq_spec(SB_K0), seq_spec(SB_V0), pl.BlockSpec((1, LANES), lambda b, h: (0, h))]
        + [ANY_SPEC] * hosted.n,
        out_specs=[out_spec] * 3 + [pl.BlockSpec(memory_space=pltpu.SMEM)] + [ANY_SPEC] * hosted.n,
        out_shape=[shape] * 3 + [jax.ShapeDtypeStruct((n_seq // nb, n_pairs, n_q), jnp.int32)]
        + exchange_shapes(hosted.arrays, hosted.gather),
        scratch_shapes=exchange_sems(hosted.n),
        compiler_params=_params(("arbitrary", "arbitrary")),
    )(p, p, p, gain, *hosted.arrays)


def sb_bwd(p, gain, o_raw, tot, dy, first, n_seq, seq_len):
    n_pairs = SB_W // LANES
    n_q = seq_len // QB
    nb = min(SB_SEQS, n_seq)

    def seq_spec(c0):
        return pl.BlockSpec((nb, seq_len, LANES), functools.partial(lambda h, b, c0: (b, 0, c0 + h), c0=c0),
                            pipeline_mode=SB_BUFFERS)

    own = pl.BlockSpec((nb, seq_len, LANES), lambda h, b: (b, 0, h), pipeline_mode=SB_BUFFERS)
    par = pl.BlockSpec((1, LANES), lambda h, b: (0, h))

    def body(q_ref, k_ref, v_ref, g_ref, o_ref, tot_ref, dy_ref, first_ref, dq_ref, dk_ref, dv_ref, dg_ref):
        gain = g_ref[...]
        lo = _lane_lo((1, LANES))
        dk_ref[...] = jnp.zeros_like(dk_ref)
        dv_ref[...] = jnp.zeros_like(dv_ref)

        def q_block(i, dgain):
            qs = pl.multiple_of(i * QB, QB)
            seqs = range(nb)
            zeros = [jnp.zeros((QB, 1), F32)] * nb
            qv, dov, t_lo, t_hi = [], [], [], []
            for s in seqs:
                qv.append(sb_split_q(q_ref[s, pl.ds(qs, QB), :]))
                _, post_vjp = jax.vjp(sb_post, o_ref[s, pl.ds(qs, QB), :], gain)
                do, dg_s = post_vjp(dy_ref[s, pl.ds(qs, QB), :])
                dov.append(do)
                dgain = dgain + dg_s
                tot_s = tot_ref[s, pl.ds(qs, QB), :]
                t_lo.append(jnp.max(jnp.where(lo, tot_s, -jnp.inf), axis=1, keepdims=True))
                t_hi.append(jnp.max(jnp.where(lo, -jnp.inf, tot_s), axis=1, keepdims=True))
            add = lambda xs, ys: [x + y for x, y in zip(xs, ys)]
            sub = lambda xs, ys: [x - y for x, y in zip(xs, ys)]

            def tile(ks, carry, diag):
                dq, rem_lo, rem_hi, g_lo, g_hi = carry
                kv = [k_ref[s, pl.ds(ks, QB), :] for s in seqs]
                vv = [v_ref[s, pl.ds(ks, QB), :] for s in seqs]
                fn = functools.partial(sb_tile, diag=diag, from_here=list(zip(rem_lo, rem_hi)))
                (_, s_lo, s_hi), vjp = jax.vjp(fn, qv, kv, vv, zeros, zeros)
                dq_t, dk_t, dv_t, dc_lo, dc_hi = vjp((dov, g_lo, g_hi))
                dq_t = [jnp.where(lo, d_lo, d_hi) for d_lo, d_hi in dq_t]
                for s in seqs:
                    dk_ref[s, pl.ds(ks, QB), :] = dk_ref[s, pl.ds(ks, QB), :] + dk_t[s]
                    dv_ref[s, pl.ds(ks, QB), :] = dv_ref[s, pl.ds(ks, QB), :] + dv_t[s]
                return add(dq, dq_t), sub(rem_lo, s_lo), sub(rem_hi, s_hi), add(g_lo, dc_lo), add(g_hi, dc_hi)

            def k_block(j, carry):
                return tile(pl.multiple_of(j * QB, QB), carry, False)

            carry = ([jnp.zeros((QB, LANES), F32)] * nb, t_lo, t_hi, zeros, zeros)
            carry = lax.fori_loop(first_ref[pl.program_id(1), pl.program_id(0), i], i, k_block, carry)
            carry = tile(qs, carry, True)
            for s in seqs:
                dq_ref[s, pl.ds(qs, QB), :] = carry[0][s] * SB_SCALE
            return dgain

        dgain = lax.fori_loop(0, n_q, q_block, jnp.zeros((1, LANES), F32))
        first = pl.program_id(1) == 0

        @pl.when(first)
        def _():
            dg_ref[...] = dgain

        @pl.when(jnp.logical_not(first))
        def _():
            dg_ref[...] = dg_ref[...] + dgain

    shape = jax.ShapeDtypeStruct((n_seq, seq_len, SB_W), F32)
    return pl.pallas_call(
        body, name="sb_bwd", grid=(n_pairs, n_seq // nb),
        in_specs=[seq_spec(SB_Q0), seq_spec(SB_K0), seq_spec(SB_V0), par, own, own, seq_spec(RWKV_W // LANES),
                  pl.BlockSpec(memory_space=pltpu.SMEM)],
        out_specs=[own, own, own, par],
        out_shape=[shape, shape, shape, jax.ShapeDtypeStruct((1, SB_W), F32)],
        compiler_params=_params(("arbitrary", "arbitrary")),
    )(p, p, p, gain, o_raw, tot, dy, first)


def exchange(name, arrays, gather):
    n = len(arrays)

    def body(*refs):
        copies = exchange_copies(refs[:n], refs[n:2 * n], *refs[2 * n:], gather)
        for cp in copies:
            cp.start()
        for cp in copies:
            cp.wait()

    return pl.pallas_call(
        body, name=name, in_specs=[ANY_SPEC] * n, out_specs=[ANY_SPEC] * n, out_shape=exchange_shapes(arrays, gather),
        scratch_shapes=exchange_sems(n),
    )(*arrays)


ANY_SPEC = pl.BlockSpec(memory_space=pl.ANY)


def exchange_shapes(arrays, gather):
    return [jax.ShapeDtypeStruct(((N_DEV,) + a.shape) if gather else a.shape, a.dtype) for a in arrays]


def exchange_sems(n):
    return [pltpu.SemaphoreType.DMA((n, N_DEV - 1)), pltpu.SemaphoreType.DMA((n, N_DEV - 1)),
            pltpu.SemaphoreType.DMA((n,))]


def exchange_copies(ins, outs, send_sems, recv_sems, local_sems, gather):
    x, y, c = lax.axis_index("x"), lax.axis_index("y"), lax.axis_index("c")
    me = 4 * x + 2 * y + c
    copies = []
    for a, (src_all, dst_all) in enumerate(zip(ins, outs)):
        own = src_all if gather else src_all.at[me]
        copies.append(pltpu.make_async_copy(own, dst_all.at[me], local_sems.at[a]))
        for j in range(1, N_DEV):
            px, py, pc = (x + (j >> 2)) % 2, (y + ((j >> 1) & 1)) % 2, (c + (j & 1)) % 2
            src = src_all if gather else src_all.at[4 * px + 2 * py + pc]
            copies.append(pltpu.make_async_remote_copy(
                src_ref=src, dst_ref=dst_all.at[me], send_sem=send_sems.at[a, j - 1],
                recv_sem=recv_sems.at[a, j - 1], device_id=(px, py, pc), device_id_type=pl.DeviceIdType.MESH))
    return copies


def adamw(name, w, parts, m, v, tile):
    rows, cols = w.shape
    spec = pl.BlockSpec((tile, cols), lambda i: (i, 0))
    part_spec = pl.BlockSpec((N_DEV, tile, cols), lambda i: (0, i, 0))

    def body(w_ref, p_ref, m_ref, v_ref, g_ref, d_ref, nm_ref, nv_ref):
        g = p_ref[0].astype(F32)
        for s in range(1, N_DEV):
            g = g + p_ref[s].astype(F32)
        new_m = ADAM_B1 * m_ref[...] + (1.0 - ADAM_B1) * g
        new_v = ADAM_B2 * v_ref[...] + (1.0 - ADAM_B2) * (g * g)
        m_hat = new_m / (1.0 - ADAM_B1 ** ADAM_STEP)
        v_hat = new_v / (1.0 - ADAM_B2 ** ADAM_STEP)
        g_ref[...] = g
        d_ref[...] = -ADAM_LR * (m_hat / (jnp.sqrt(v_hat) + ADAM_EPS) + ADAM_WD * w_ref[...])
        nm_ref[...] = new_m
        nv_ref[...] = new_v

    shape = jax.ShapeDtypeStruct((rows, cols), F32)
    return pl.pallas_call(
        body, name=name, grid=(rows // tile,), in_specs=[spec, part_spec, spec, spec],
        out_specs=[spec] * 4, out_shape=[shape] * 4, compiler_params=_params(("arbitrary",)),
    )(w, parts, m, v)


SMALL = ("ln1_g", "tok_mu", "w0", "a0", "k_k", "k_a", "r_k", "gn_w", "gn_b", "sb_gain", "ln2_g", "lnf_g")
EARLY = ("w_in", "w_decay_up", "w_aaa_up", "w_gate_up")
LATE = ("w_out", "w_up", "w_down")
BIG = EARLY + LATE
ORDER = ("ln1_g", "w_in", "tok_mu", "w0", "w_decay_up", "a0", "w_aaa_up", "w_gate_up", "k_k", "k_a", "r_k",
         "gn_w", "gn_b", "sb_gain", "w_out", "ln2_g", "w_up", "w_down", "lnf_g")


def _pack(vectors, rows):
    flat = jnp.concatenate([v.reshape(-1).astype(F32) for v in vectors])
    return jnp.pad(flat, (0, rows * LANES - flat.shape[0])).reshape(rows, LANES)


def _full_cols(gathered):
    d, k, cols = gathered.shape
    return jnp.transpose(gathered, (1, 0, 2)).reshape(k, d * cols)


def _col_parts(full):
    k, n = full.shape
    return jnp.transpose(full.reshape(k, N_DEV, n // N_DEV), (1, 0, 2))


def kernel(x, ln1_g, w_in, tok_mu, w0, w_decay_up, a0, w_aaa_up, w_gate_up, k_k, k_a, r_k, gn_w, gn_b, sb_gain, w_out, ln2_g, w_up, w_down, lnf_g, loss_target, m_ln1_g, m_w_in, m_tok_mu, m_w0, m_w_decay_up, m_a0, m_w_aaa_up, m_w_gate_up, m_k_k, m_k_a, m_r_k, m_gn_w, m_gn_b, m_sb_gain, m_w_out, m_ln2_g, m_w_up, m_w_down, m_lnf_g, v_ln1_g, v_w_in, v_tok_mu, v_w0, v_w_decay_up, v_a0, v_w_aaa_up, v_w_gate_up, v_k_k, v_k_a, v_r_k, v_gn_w, v_gn_b, v_sb_gain, v_w_out, v_ln2_g, v_w_up, v_w_down, v_lnf_g):
    args = dict(locals())
    weights = {n: args[n] for n in ORDER}
    mom_m = {n: args["m_" + n] for n in ORDER}
    mom_v = {n: args["v_" + n] for n in ORDER}

    n_seq, seq_len, d_model = x.shape
    n_tok = n_seq * seq_len
    x2d = x.reshape(n_tok, d_model)
    tgt = loss_target.reshape(n_tok, d_model)
    row = lambda t: t.reshape(1, -1).astype(F32)

    g1 = row(ln1_g)
    shard = {n: weights[n][0].astype(BF16) for n in BIG}
    h1, *early = rw_call("norm1", lambda r, q: [f_norm(r[0], q[0])], [x2d], [g1], [(d_model, BF16)], 512,
                         hosted=HostedChipGather([shard[n] for n in EARLY]))
    gathered = dict(zip(EARLY, early))
    w_in_f = _full_cols(gathered["w_in"])
    zeros64 = jnp.zeros((HEAD_DIM, RWKV_W), BF16)
    wd_pad = jnp.concatenate([_full_cols(gathered["w_decay_up"]), zeros64], axis=0)
    wa_pad = jnp.concatenate([zeros64, _full_cols(gathered["w_aaa_up"])], axis=0)
    wg_f = _full_cols(gathered["w_gate_up"])
    in_cols = w_in_f.shape[1]

    mu, w0r, a0r = row(tok_mu), row(w0), row(a0)
    kkr, kar, rkr, gwr, gbr, sgr = row(k_k), row(k_a), row(r_k), row(gn_w), row(gn_b), row(sb_gain)
    g2, gf = row(ln2_g), row(lnf_g)

    p = matmul("proj_in", h1, w_in_f, "nn", F32, 2048, in_cols // 2, d_model)
    prep_pars = [mu, w0r, wd_pad, a0r, wa_pad, wg_f]
    (r_, kraw, v_, lw, asig, gate), prep_edges = rwkv_prep_fwd(p, prep_pars, seq_len)
    by_seq = lambda t: t.reshape(n_seq, seq_len, t.shape[-1])
    flat = lambda t: t.reshape(n_tok, t.shape[-1])
    rwkv_in = [by_seq(t) for t in (r_, kraw, v_, lw, asig)]
    z, s0_all, tinv_all, (w_out_g, w_up_g) = rwkv_fwd(
        *rwkv_in, kkr, kar, rkr, gwr, gbr, n_seq, seq_len, Hosted([shard["w_out"], shard["w_up"]], True))
    y_sb, o_raw, tot, sb_first, w_down_g = sb_fwd(by_seq(p), sgr, n_seq, seq_len, Hosted([shard["w_down"]], True))
    w_out_f = w_out_g.reshape(d_model, d_model)
    w_up_f = _full_cols(w_up_g)
    w_down_f = w_down_g.reshape(-1, d_model)
    d_ff = w_up_f.shape[1]
    z, y_sb = flat(z), flat(y_sb)
    (ycat,) = rw_call("mix_cat", lambda r, q: [jnp.concatenate([r[0] * r[1], r[2]], axis=1)],
                      [z, gate, y_sb], [], [(d_model, BF16)], 512)
    x2, h2 = matmul("proj_out", ycat, w_out_f, "nn", (F32, BF16), 1024, d_model, d_model, extras=[x2d], pars=[g2],
                    epilogue=lambda acc, xv, g: (xv + acc, f_norm(xv + acc, g)))
    u, act = matmul("mlp_up", h2, w_up_f, "nn", (F32, BF16), 1024, d_ff // 2, d_model,
                    epilogue=lambda acc: (acc, jnp.square(jnp.maximum(acc, 0.0))))

    def loss_epilogue(acc, x2v, target, g):
        loss_rows, vjp = jax.vjp(lambda xv, gv: f_final(xv, gv, target), x2v + acc, g)
        dx3, dgf = vjp(jnp.ones_like(loss_rows))
        return dx3, dx3, jnp.broadcast_to(jnp.sum(loss_rows), (1, LANES)), dgf

    dx3, dx3_b, loss_acc, d_lnf = matmul(
        "mlp_down", act, w_down_f, "nn", (F32, BF16), 512, d_model, d_ff, extras=[x2, tgt], pars=[gf],
        epilogue=loss_epilogue, sums=[((1, LANES), F32), ((1, d_model), F32)])

    du = matmul("d_act", dx3_b, w_down_f, "nt", BF16, 1024, d_ff // 4, d_model, extras=[u],
                epilogue=lambda acc, uv: (acc * (2.0 * jnp.maximum(uv, 0.0)),))
    dw_down = matmul("dw_down", act, dx3_b, "tn", BF16, 512, d_model, 4096)
    dw_up = matmul("dw_up", h2, du, "tn", BF16, d_model, 512, 4096)

    def norm_bwd(acc, xv, dres, g):
        _, vjp = jax.vjp(f_norm, xv, g)
        dx, dg = vjp(acc)
        return dx + dres, dx + dres, dg

    dx2, dx2_b, d_ln2 = matmul("d_h2", du, w_up_f, "nt", (F32, BF16), 512, d_model, d_ff, extras=[x2, dx3], pars=[g2],
                               epilogue=norm_bwd, sums=[((1, d_model), F32)])

    dycat = matmul("d_ycat", dx2_b, w_out_f, "nt", F32, 1024, d_model, d_model)
    dw_out = matmul("dw_out", ycat, dx2_b, "tn", BF16, d_model, d_model, 2048)
    dq, dk_sb, dv_sb, d_sg = sb_bwd(by_seq(p), sgr, o_raw, tot, by_seq(dycat), sb_first, n_seq, seq_len)
    d_sb = [flat(dq), flat(dk_sb), flat(dv_sb)]
    late_grads = {"w_out": dw_out.reshape(N_DEV, -1, d_model), "w_up": _col_parts(dw_up),
                  "w_down": dw_down.reshape(N_DEV, -1, d_model)}
    (dr, dkraw, dv, dlw, dasig, d_kk, d_ka, d_rk, d_gw, d_gb), late_parts = rwkv_bwd(
        *rwkv_in, kkr, kar, rkr, gwr, gbr, s0_all, tinv_all, by_seq(dycat), by_seq(gate), n_seq, seq_len,
        Hosted([late_grads[n] for n in LATE], False))
    prep_cots = [flat(t) for t in (dr, dkraw, dv, dlw, dasig)] + [dycat, z]
    dp, (d_mu, d_w0, d_wd, d_a0, d_wa, d_wg) = rwkv_prep_bwd(p, prep_edges, prep_pars, prep_cots, d_sb, seq_len)

    dw_in = matmul("dw_in", h1, dp, "tn", BF16, d_model, in_cols // 2, 2048)
    lora_parts = lambda t: _col_parts(t).astype(BF16)
    early_grads = {"w_in": _col_parts(dw_in), "w_decay_up": lora_parts(d_wd[:HEAD_DIM]),
                   "w_aaa_up": lora_parts(d_wa[HEAD_DIM:]), "w_gate_up": lora_parts(d_wg)}
    dx, d_ln1, *early_parts = matmul(
        "d_h1", dp, w_in_f, "nt", (F32,), 512, d_model, in_cols, extras=[x2d, dx2], pars=[g1],
        epilogue=lambda *t: norm_bwd(*t)[1:], sums=[((1, d_model), F32)],
        hosted=Hosted([early_grads[n] for n in EARLY], False))
    parts = dict(zip(EARLY, early_parts))
    parts.update(zip(LATE, late_parts))

    small_grads = {"ln1_g": d_ln1, "tok_mu": d_mu, "w0": d_w0, "a0": d_a0, "k_k": d_kk, "k_a": d_ka, "r_k": d_rk,
                   "gn_w": d_gw, "gn_b": d_gb, "sb_gain": d_sg, "ln2_g": d_ln2, "lnf_g": d_lnf}
    n_small = sum(int(weights[n].size) for n in SMALL)
    pack_rows = -(-(n_small + 1) // (8 * LANES)) * 8
    packed = _pack([small_grads[n] for n in SMALL] + [loss_acc[0, :1]], pack_rows)
    (small_parts,) = exchange("gather_small", [packed], True)

    results = {}
    for n in BIG:
        w2d = weights[n][0]
        tile = w2d.shape[0] if w2d.shape[0] <= 256 else 256
        results[n] = adamw("adamw_" + n, w2d, parts[n], mom_m[n][0], mom_v[n][0], tile)
    pk = lambda d: _pack([d[n] for n in SMALL] + [jnp.zeros((1,), F32)], pack_rows)
    sg, sd, sm, sv = adamw("adamw_small", pk(weights), small_parts, pk(mom_m), pk(mom_v), pack_rows)
    off = 0
    for n in SMALL:
        size = int(weights[n].size)
        results[n] = tuple(t.reshape(-1)[off:off + size] for t in (sg, sd, sm, sv))
        off += size
    loss = sg.reshape(-1)[off]

    out = [loss, dx.reshape(x.shape)]
    for kind in range(4):
        out += [results[n][kind].reshape(weights[n].shape) for n in ORDER]
    return tuple(out)
```
